```python
import jax, jax.numpy as jnp
from jax import lax
import numpy as np

D_MODEL = 1024
BATCH = 32
SEQ = 2048
DEPTH = 2

N_A_LAYERS = DEPTH // 2
N_B_LAYERS = DEPTH - N_A_LAYERS

NORM_EPS = 1e-6

A_WINDOWS = (128, 512, 2048)
A_DILATIONS = (1, 4, 16)
A_GROUPS = 3
A_HEADS = 8
A_HEAD_DIM = 128
A_WIDTH = A_HEADS * A_HEAD_DIM
A_ROT_DIM = A_HEAD_DIM // 4
A_ROPE_THETA = 500000.0
BAND_BLOCK = 128
A_IN_WIDTH = A_GROUPS * 3 * A_WIDTH + A_WIDTH

B_HEADS = 16
B_NOPE = 64
B_ROPE = 32
B_QK_DIM = B_NOPE + B_ROPE
B_VDIM = 64
B_WIDTH = B_HEADS * B_VDIM
B_Q_LORA = 384
B_KV_LORA = 256
B_ROPE_THETA = 10000.0
B_IN_WIDTH = B_Q_LORA + B_WIDTH
ATTN_BLOCK = 128

kernel_name = "yoco_dilated_swa_mla_hybrid"


def rms_norm(x, g):
    xf = x.astype(jnp.float32)
    y = xf * lax.rsqrt(jnp.mean(xf * xf, axis=-1, keepdims=True) + NORM_EPS)
    return (y * g.astype(jnp.float32)).astype(x.dtype)


def rope(x, positions, theta):
    dim = x.shape[-1]
    half = dim // 2
    inv_freq = 1.0 / (theta ** (jnp.arange(half, dtype=jnp.float32) * (2.0 / dim)))
    ang = positions.astype(jnp.float32)[..., None] * inv_freq
    cos = jnp.cos(ang)[:, :, None, :]
    sin = jnp.sin(ang)[:, :, None, :]
    xf = x.astype(jnp.float32)
    x1, x2 = xf[..., :half], xf[..., half:]
    out = jnp.concatenate([x1 * cos - x2 * sin, x2 * cos + x1 * sin], axis=-1)
    return out.astype(x.dtype)


def partial_rope(x, positions):
    return jnp.concatenate(
        [rope(x[..., :A_ROT_DIM], positions, A_ROPE_THETA), x[..., A_ROT_DIM:]], axis=-1)


def dilated_window_attention(q, k, v, window, dilation):
    B, T, H, Dh = q.shape
    L = T // dilation
    w_sub = window // dilation
    Q = BAND_BLOCK
    nb = -(-L // Q)
    Lp = nb * Q

    def split(a):
        a = a.reshape(B, L, dilation, H, Dh).transpose(0, 2, 1, 3, 4)
        a = jnp.pad(a, ((0, 0), (0, 0), (0, Lp - L), (0, 0), (0, 0)))
        return a.reshape(B, dilation, nb, Q, H, Dh)

    def band(a):
        prev = jnp.pad(a, ((0, 0), (0, 0), (1, 0), (0, 0), (0, 0), (0, 0)))[:, :, :-1]
        return jnp.concatenate([prev, a], axis=3)

    qb = split(q)
    kb = band(split(k))
    vb = band(split(v))
    s = jnp.einsum('brnqhd,brnkhd->brnhqk', qb, kb,
                   preferred_element_type=jnp.float32) * (A_HEAD_DIM ** -0.5)
    qi = jnp.arange(Q)[:, None]
    ki = jnp.arange(2 * Q)[None, :]
    dist = Q + qi - ki
    key_pos = (jnp.arange(nb)[:, None, None] - 1) * Q + ki[None]
    allowed = (dist >= 0)[None] & (dist <= w_sub)[None] & (key_pos >= 0)
    s = jnp.where(allowed[None, None, :, None], s, -jnp.inf)
    m = jnp.max(s, axis=-1, keepdims=True)
    p = jnp.exp(s - m)
    l = jnp.sum(p, axis=-1)
    o = jnp.einsum('brnhqk,brnkhd->brnqhd', p, vb.astype(jnp.float32))
    l_q = l.transpose(0, 1, 2, 4, 3)
    o = o / l_q[..., None]
    lse = (m[..., 0] + jnp.log(l)).transpose(0, 1, 2, 4, 3)
    o = o.reshape(B, dilation, Lp, H, Dh)[:, :, :L].transpose(0, 2, 1, 3, 4).reshape(B, T, H, Dh)
    lse = lse.reshape(B, dilation, Lp, H)[:, :, :L].transpose(0, 2, 1, 3).reshape(B, T, H)
    return o, lse


def mixer_a(h, positions, w_in, w_out):
    B, T, _ = h.shape
    proj = h @ w_in
    qkv = proj[..., :A_GROUPS * 3 * A_WIDTH].reshape(B, T, A_GROUPS, 3, A_HEADS, A_HEAD_DIM)
    z = proj[..., A_GROUPS * 3 * A_WIDTH:]
    outs, lses = [], []
    for g in range(A_GROUPS):
        q = partial_rope(qkv[:, :, g, 0], positions)
        k = partial_rope(qkv[:, :, g, 1], positions)
        v = qkv[:, :, g, 2]
        o, lse = dilated_window_attention(q, k, v, A_WINDOWS[g], A_DILATIONS[g])
        outs.append(o)
        lses.append(lse)
    wts = jax.nn.softmax(jnp.stack(lses, axis=0), axis=0)
    o = jnp.einsum('gbth,gbthd->bthd', wts, jnp.stack(outs, axis=0))
    y = o.reshape(B, T, A_WIDTH).astype(h.dtype) * jax.nn.silu(z)
    return y @ w_out


def shared_latent_kv(h, positions, kv_norm, kv_w_down, kv_latent_norm, kv_w_up):
    B, T, _ = h.shape
    hn = rms_norm(h, kv_norm)
    ckr = hn @ kv_w_down
    c_kv = rms_norm(ckr[..., :B_KV_LORA], kv_latent_norm)
    k_rope = rope(ckr[..., B_KV_LORA:][:, :, None, :], positions, B_ROPE_THETA)
    kv = (c_kv @ kv_w_up).reshape(B, T, B_HEADS, B_NOPE + B_VDIM)
    k = jnp.concatenate(
        [kv[..., :B_NOPE], jnp.broadcast_to(k_rope, (B, T, B_HEADS, B_ROPE))], axis=-1)
    v = kv[..., B_NOPE:]
    return k, v


def causal_block_attention(q, k, v):
    B, T, H, Dq = q.shape
    Dv = v.shape[-1]
    nb = T // ATTN_BLOCK
    q_blocks = q.reshape(B, nb, ATTN_BLOCK, H, Dq).transpose(1, 0, 2, 3, 4)
    k_pos = jnp.arange(T)
    vf = v.astype(jnp.float32)

    def one_block(args):
        qb, idx = args
        s = jnp.einsum('bqhd,bkhd->bhqk', qb, k,
                       preferred_element_type=jnp.float32) * (B_QK_DIM ** -0.5)
        q_pos = idx * ATTN_BLOCK + jnp.arange(ATTN_BLOCK)
        mask = k_pos[None, :] <= q_pos[:, None]
        p = jax.nn.softmax(jnp.where(mask[None, None], s, -jnp.inf), axis=-1)
        return jnp.einsum('bhqk,bkhd->bqhd', p, vf)

    o = lax.map(one_block, (q_blocks, jnp.arange(nb)))
    return o.transpose(1, 0, 2, 3, 4).reshape(B, T, H, Dv)


def mixer_b(h, positions, k, v, w_in, q_norm, w_q_up, w_out):
    B, T, _ = h.shape
    proj = h @ w_in
    c_q = rms_norm(proj[..., :B_Q_LORA], q_norm)
    z = proj[..., B_Q_LORA:]
    q = (c_q @ w_q_up).reshape(B, T, B_HEADS, B_QK_DIM)
    q = jnp.concatenate([q[..., :B_NOPE], rope(q[..., B_NOPE:], positions, B_ROPE_THETA)], axis=-1)
    o = causal_block_attention(q, k, v)
    y = o.reshape(B, T, B_WIDTH).astype(h.dtype) * jax.nn.silu(z)
    return y @ w_out


def _fwd_setup_inputs(seed: int = 0) -> dict:
    key = jax.random.key(seed)
    ks = jax.random.split(key, 20)
    f32 = jnp.float32

    def w(k, shape, fan_in):
        return jax.random.normal(k, shape, f32) * (fan_in ** -0.5)

    def gain(k, shape):
        return 1.0 + 0.01 * jax.random.normal(k, shape, f32)

    x = jax.random.normal(ks[0], (BATCH, SEQ, D_MODEL), f32)
    offsets = jax.random.randint(ks[1], (BATCH, 1), 0, 4096, dtype=jnp.int32)
    positions = offsets + jnp.arange(SEQ, dtype=jnp.int32)[None, :]
    return {
        "x": x,
        "positions": positions,
        "a_pre_norm": gain(ks[2], (N_A_LAYERS, D_MODEL)),
        "a_w_in": w(ks[3], (N_A_LAYERS, D_MODEL, A_IN_WIDTH), D_MODEL),
        "a_w_out": w(ks[4], (N_A_LAYERS, A_WIDTH, D_MODEL), A_WIDTH),
        "a_post_norm": gain(ks[5], (N_A_LAYERS, D_MODEL)),
        "kv_norm": gain(ks[6], (D_MODEL,)),
        "kv_w_down": w(ks[7], (D_MODEL, B_KV_LORA + B_ROPE), D_MODEL),
        "kv_latent_norm": gain(ks[8], (B_KV_LORA,)),
        "kv_w_up": w(ks[9], (B_KV_LORA, B_HEADS * (B_NOPE + B_VDIM)), B_KV_LORA),
        "b_pre_norm": gain(ks[10], (N_B_LAYERS, D_MODEL)),
        "b_w_in": w(ks[11], (N_B_LAYERS, D_MODEL, B_IN_WIDTH), D_MODEL),
        "b_q_norm": gain(ks[12], (N_B_LAYERS, B_Q_LORA)),
        "b_w_q_up": w(ks[13], (N_B_LAYERS, B_Q_LORA, B_HEADS * B_QK_DIM), B_Q_LORA),
        "b_w_out": w(ks[14], (N_B_LAYERS, B_WIDTH, D_MODEL), B_WIDTH),
        "b_post_norm": gain(ks[15], (N_B_LAYERS, D_MODEL)),
    }


def _fwd_reference(x, positions, a_pre_norm, a_w_in, a_w_out, a_post_norm,
              kv_norm, kv_w_down, kv_latent_norm, kv_w_up,
              b_pre_norm, b_w_in, b_q_norm, b_w_q_up, b_w_out, b_post_norm):
    h = x
    k_shared = None
    v_shared = None
    for layer in range(DEPTH):
        if layer < N_A_LAYERS:
            i = layer
            y = mixer_a(rms_norm(h, a_pre_norm[i]), positions, a_w_in[i], a_w_out[i])
            h = h + rms_norm(y, a_post_norm[i])
        else:
            if layer == N_A_LAYERS:
                k_shared, v_shared = shared_latent_kv(
                    h, positions, kv_norm, kv_w_down, kv_latent_norm, kv_w_up)
            i = layer - N_A_LAYERS
            y = mixer_b(rms_norm(h, b_pre_norm[i]), positions, k_shared, v_shared,
                        b_w_in[i], b_q_norm[i], b_w_q_up[i], b_w_out[i])
            h = h + rms_norm(y, b_post_norm[i])
    return h


import jax as _jax
import jax.numpy as _jnp

TWIN_FORMAT = 'train_step'
FWD_PARAMS = ['x', 'positions', 'a_pre_norm', 'a_w_in', 'a_w_out', 'a_post_norm', 'kv_norm', 'kv_w_down', 'kv_latent_norm', 'kv_w_up', 'b_pre_norm', 'b_w_in', 'b_q_norm', 'b_w_q_up', 'b_w_out', 'b_post_norm']
TWIN_WEIGHTS = ['a_pre_norm', 'a_w_in', 'a_w_out', 'a_post_norm', 'kv_norm', 'kv_w_down', 'kv_latent_norm', 'kv_w_up', 'b_pre_norm', 'b_w_in', 'b_q_norm', 'b_w_q_up', 'b_w_out', 'b_post_norm']
TWIN_DIFF_INPUT = 'x'
TWIN_INPUTS = ['x', 'positions', 'a_pre_norm', 'a_w_in', 'a_w_out', 'a_post_norm', 'kv_norm', 'kv_w_down', 'kv_latent_norm', 'kv_w_up', 'b_pre_norm', 'b_w_in', 'b_q_norm', 'b_w_q_up', 'b_w_out', 'b_post_norm', 'loss_target', 'm_a_pre_norm', 'm_a_w_in', 'm_a_w_out', 'm_a_post_norm', 'm_kv_norm', 'm_kv_w_down', 'm_kv_latent_norm', 'm_kv_w_up', 'm_b_pre_norm', 'm_b_w_in', 'm_b_q_norm', 'm_b_w_q_up', 'm_b_w_out', 'm_b_post_norm', 'v_a_pre_norm', 'v_a_w_in', 'v_a_w_out', 'v_a_post_norm', 'v_kv_norm', 'v_kv_w_down', 'v_kv_latent_norm', 'v_kv_w_up', 'v_b_pre_norm', 'v_b_w_in', 'v_b_q_norm', 'v_b_w_q_up', 'v_b_w_out', 'v_b_post_norm']
TWIN_OUTPUTS = ['loss', 'grad_x', 'grad_a_pre_norm', 'grad_a_w_in', 'grad_a_w_out', 'grad_a_post_norm', 'grad_kv_norm', 'grad_kv_w_down', 'grad_kv_latent_norm', 'grad_kv_w_up', 'grad_b_pre_norm', 'grad_b_w_in', 'grad_b_q_norm', 'grad_b_w_q_up', 'grad_b_w_out', 'grad_b_post_norm', 'delta_a_pre_norm', 'delta_a_w_in', 'delta_a_w_out', 'delta_a_post_norm', 'delta_kv_norm', 'delta_kv_w_down', 'delta_kv_latent_norm', 'delta_kv_w_up', 'delta_b_pre_norm', 'delta_b_w_in', 'delta_b_q_norm', 'delta_b_w_q_up', 'delta_b_w_out', 'delta_b_post_norm', 'new_m_a_pre_norm', 'new_m_a_w_in', 'new_m_a_w_out', 'new_m_a_post_norm', 'new_m_kv_norm', 'new_m_kv_w_down', 'new_m_kv_latent_norm', 'new_m_kv_w_up', 'new_m_b_pre_norm', 'new_m_b_w_in', 'new_m_b_q_norm', 'new_m_b_w_q_up', 'new_m_b_w_out', 'new_m_b_post_norm', 'new_v_a_pre_norm', 'new_v_a_w_in', 'new_v_a_w_out', 'new_v_a_post_norm', 'new_v_kv_norm', 'new_v_kv_w_down', 'new_v_kv_latent_norm', 'new_v_kv_w_up', 'new_v_b_pre_norm', 'new_v_b_w_in', 'new_v_b_q_norm', 'new_v_b_w_q_up', 'new_v_b_w_out', 'new_v_b_post_norm']
TWIN_LEAF_KINDS = {'loss': 'loss', 'grad_x': 'grad_x', 'grad_a_pre_norm': 'grad_w', 'grad_a_w_in': 'grad_w', 'grad_a_w_out': 'grad_w', 'grad_a_post_norm': 'grad_w', 'grad_kv_norm': 'grad_w', 'grad_kv_w_down': 'grad_w', 'grad_kv_latent_norm': 'grad_w', 'grad_kv_w_up': 'grad_w', 'grad_b_pre_norm': 'grad_w', 'grad_b_w_in': 'grad_w', 'grad_b_q_norm': 'grad_w', 'grad_b_w_q_up': 'grad_w', 'grad_b_w_out': 'grad_w', 'grad_b_post_norm': 'grad_w', 'delta_a_pre_norm': 'delta_w', 'delta_a_w_in': 'delta_w', 'delta_a_w_out': 'delta_w', 'delta_a_post_norm': 'delta_w', 'delta_kv_norm': 'delta_w', 'delta_kv_w_down': 'delta_w', 'delta_kv_latent_norm': 'delta_w', 'delta_kv_w_up': 'delta_w', 'delta_b_pre_norm': 'delta_w', 'delta_b_w_in': 'delta_w', 'delta_b_q_norm': 'delta_w', 'delta_b_w_q_up': 'delta_w', 'delta_b_w_out': 'delta_w', 'delta_b_post_norm': 'delta_w', 'new_m_a_pre_norm': 'new_m', 'new_m_a_w_in': 'new_m', 'new_m_a_w_out': 'new_m', 'new_m_a_post_norm': 'new_m', 'new_m_kv_norm': 'new_m', 'new_m_kv_w_down': 'new_m', 'new_m_kv_latent_norm': 'new_m', 'new_m_kv_w_up': 'new_m', 'new_m_b_pre_norm': 'new_m', 'new_m_b_w_in': 'new_m', 'new_m_b_q_norm': 'new_m', 'new_m_b_w_q_up': 'new_m', 'new_m_b_w_out': 'new_m', 'new_m_b_post_norm': 'new_m', 'new_v_a_pre_norm': 'new_v', 'new_v_a_w_in': 'new_v', 'new_v_a_w_out': 'new_v', 'new_v_a_post_norm': 'new_v', 'new_v_kv_norm': 'new_v', 'new_v_kv_w_down': 'new_v', 'new_v_kv_latent_norm': 'new_v', 'new_v_kv_w_up': 'new_v', 'new_v_b_pre_norm': 'new_v', 'new_v_b_w_in': 'new_v', 'new_v_b_q_norm': 'new_v', 'new_v_b_w_q_up': 'new_v', 'new_v_b_w_out': 'new_v', 'new_v_b_post_norm': 'new_v'}


def _forward(args):
    return _fwd_reference(*[args[k] for k in FWD_PARAMS])


def _output_shape():
    out = _jax.eval_shape(lambda: _forward(_fwd_setup_inputs(0)))
    return out.shape, out.dtype

N_MICROBATCH = 1
ADAM_LR = 0.001
ADAM_B1 = 0.9
ADAM_B2 = 0.999
ADAM_EPS = 1e-08
ADAM_WD = 0.01
ADAM_STEP = 10
PER_EXAMPLE_BATCH_AXIS = {'x': 0, 'positions': 0, 'loss_target': 0}
SHARED_INPUTS = []
_WEIGHT_DTYPES = {'a_pre_norm': _jnp.float32, 'a_w_in': _jnp.float32, 'a_w_out': _jnp.float32, 'a_post_norm': _jnp.float32, 'kv_norm': _jnp.float32, 'kv_w_down': _jnp.float32, 'kv_latent_norm': _jnp.float32, 'kv_w_up': _jnp.float32, 'b_pre_norm': _jnp.float32, 'b_w_in': _jnp.float32, 'b_q_norm': _jnp.float32, 'b_w_q_up': _jnp.float32, 'b_w_out': _jnp.float32, 'b_post_norm': _jnp.float32}
MOMENT_SCALE = {'a_pre_norm': 1.426125e+00, 'a_w_in': 4.352063e-01, 'a_w_out': 8.092329e-01, 'a_post_norm': 6.387103e+01, 'kv_norm': 6.306979e-01, 'kv_w_down': 1.067295e+00, 'kv_latent_norm': 1.670404e+00, 'kv_w_up': 5.271810e-01, 'b_pre_norm': 6.789112e-01, 'b_w_in': 5.947353e-01, 'b_q_norm': 5.970333e-01, 'b_w_q_up': 3.572139e-01, 'b_w_out': 6.121417e-01, 'b_post_norm': 6.419853e+01}


def _to_microbatches(a, axis):
    t = _jnp.moveaxis(a, axis, 0)
    t = t.reshape((N_MICROBATCH, t.shape[0] // N_MICROBATCH) + t.shape[1:])
    return _jnp.moveaxis(t, 1, axis + 1)


def setup_inputs(seed: int = 0) -> dict:
    inp = _fwd_setup_inputs(seed)
    key = _jax.random.fold_in(_jax.random.key(seed), 7919)
    shape, _ = _output_shape()
    out = dict(inp)
    out["loss_target"] = _jax.random.normal(_jax.random.fold_in(key, 0), shape, _jnp.float32)
    for i, name in enumerate(TWIN_WEIGHTS):
        w = inp[name].astype(_jnp.float32)
        if MOMENT_SCALE is None:
            s = _jnp.sqrt(_jnp.mean(_jnp.square(w)) + 1e-30)
        else:
            s = MOMENT_SCALE[name]
        km, kv = _jax.random.split(_jax.random.fold_in(key, i + 1))
        out[name] = w
        out["m_" + name] = s * _jax.random.normal(km, w.shape, _jnp.float32)
        out["v_" + name] = (s * s) * _jax.random.uniform(kv, w.shape, _jnp.float32, 0.5, 1.5)
    if N_MICROBATCH > 1:
        for name, axis in PER_EXAMPLE_BATCH_AXIS.items():
            out[name] = _to_microbatches(out[name], axis)
    return {'x': out['x'], 'positions': out['positions'], 'a_pre_norm': out['a_pre_norm'], 'a_w_in': out['a_w_in'], 'a_w_out': out['a_w_out'], 'a_post_norm': out['a_post_norm'], 'kv_norm': out['kv_norm'], 'kv_w_down': out['kv_w_down'], 'kv_latent_norm': out['kv_latent_norm'], 'kv_w_up': out['kv_w_up'], 'b_pre_norm': out['b_pre_norm'], 'b_w_in': out['b_w_in'], 'b_q_norm': out['b_q_norm'], 'b_w_q_up': out['b_w_q_up'], 'b_w_out': out['b_w_out'], 'b_post_norm': out['b_post_norm'], 'loss_target': out['loss_target'], 'm_a_pre_norm': out['m_a_pre_norm'], 'm_a_w_in': out['m_a_w_in'], 'm_a_w_out': out['m_a_w_out'], 'm_a_post_norm': out['m_a_post_norm'], 'm_kv_norm': out['m_kv_norm'], 'm_kv_w_down': out['m_kv_w_down'], 'm_kv_latent_norm': out['m_kv_latent_norm'], 'm_kv_w_up': out['m_kv_w_up'], 'm_b_pre_norm': out['m_b_pre_norm'], 'm_b_w_in': out['m_b_w_in'], 'm_b_q_norm': out['m_b_q_norm'], 'm_b_w_q_up': out['m_b_w_q_up'], 'm_b_w_out': out['m_b_w_out'], 'm_b_post_norm': out['m_b_post_norm'], 'v_a_pre_norm': out['v_a_pre_norm'], 'v_a_w_in': out['v_a_w_in'], 'v_a_w_out': out['v_a_w_out'], 'v_a_post_norm': out['v_a_post_norm'], 'v_kv_norm': out['v_kv_norm'], 'v_kv_w_down': out['v_kv_w_down'], 'v_kv_latent_norm': out['v_kv_latent_norm'], 'v_kv_w_up': out['v_kv_w_up'], 'v_b_pre_norm': out['v_b_pre_norm'], 'v_b_w_in': out['v_b_w_in'], 'v_b_q_norm': out['v_b_q_norm'], 'v_b_w_q_up': out['v_b_w_q_up'], 'v_b_w_out': out['v_b_w_out'], 'v_b_post_norm': out['v_b_post_norm']}


def _loss(weights, diff, rest, loss_target):
    with _jax.named_scope("forward"):
        args = {**rest, TWIN_DIFF_INPUT: diff, **{k: w.astype(_WEIGHT_DTYPES[k]) for k, w in weights.items()}}
        y = _forward(args)
    with _jax.named_scope("loss_head"):
        err = _jnp.square(y.astype(_jnp.float32) - loss_target)
        return 0.5 * _jnp.sum(_jnp.mean(err, axis=-1)) if err.ndim else 0.5 * err


def _adamw(w, g, m, v):
    m = ADAM_B1 * m + (1.0 - ADAM_B1) * g
    v = ADAM_B2 * v + (1.0 - ADAM_B2) * _jnp.square(g)
    m_hat = m / (1.0 - ADAM_B1 ** ADAM_STEP)
    v_hat = v / (1.0 - ADAM_B2 ** ADAM_STEP)
    delta = -ADAM_LR * (m_hat / (_jnp.sqrt(v_hat) + ADAM_EPS) + ADAM_WD * w)
    return delta, m, v


def reference(x, positions, a_pre_norm, a_w_in, a_w_out, a_post_norm, kv_norm, kv_w_down, kv_latent_norm, kv_w_up, b_pre_norm, b_w_in, b_q_norm, b_w_q_up, b_w_out, b_post_norm, loss_target, m_a_pre_norm, m_a_w_in, m_a_w_out, m_a_post_norm, m_kv_norm, m_kv_w_down, m_kv_latent_norm, m_kv_w_up, m_b_pre_norm, m_b_w_in, m_b_q_norm, m_b_w_q_up, m_b_w_out, m_b_post_norm, v_a_pre_norm, v_a_w_in, v_a_w_out, v_a_post_norm, v_kv_norm, v_kv_w_down, v_kv_latent_norm, v_kv_w_up, v_b_pre_norm, v_b_w_in, v_b_q_norm, v_b_w_q_up, v_b_w_out, v_b_post_norm):
    given = dict(x=x, positions=positions, a_pre_norm=a_pre_norm, a_w_in=a_w_in, a_w_out=a_w_out, a_post_norm=a_post_norm, kv_norm=kv_norm, kv_w_down=kv_w_down, kv_latent_norm=kv_latent_norm, kv_w_up=kv_w_up, b_pre_norm=b_pre_norm, b_w_in=b_w_in, b_q_norm=b_q_norm, b_w_q_up=b_w_q_up, b_w_out=b_w_out, b_post_norm=b_post_norm, loss_target=loss_target, m_a_pre_norm=m_a_pre_norm, m_a_w_in=m_a_w_in, m_a_w_out=m_a_w_out, m_a_post_norm=m_a_post_norm, m_kv_norm=m_kv_norm, m_kv_w_down=m_kv_w_down, m_kv_latent_norm=m_kv_latent_norm, m_kv_w_up=m_kv_w_up, m_b_pre_norm=m_b_pre_norm, m_b_w_in=m_b_w_in, m_b_q_norm=m_b_q_norm, m_b_w_q_up=m_b_w_q_up, m_b_w_out=m_b_w_out, m_b_post_norm=m_b_post_norm, v_a_pre_norm=v_a_pre_norm, v_a_w_in=v_a_w_in, v_a_w_out=v_a_w_out, v_a_post_norm=v_a_post_norm, v_kv_norm=v_kv_norm, v_kv_w_down=v_kv_w_down, v_kv_latent_norm=v_kv_latent_norm, v_kv_w_up=v_kv_w_up, v_b_pre_norm=v_b_pre_norm, v_b_w_in=v_b_w_in, v_b_q_norm=v_b_q_norm, v_b_w_q_up=v_b_w_q_up, v_b_w_out=v_b_w_out, v_b_post_norm=v_b_post_norm)
    weights = {n: given[n] for n in TWIN_WEIGHTS}
    shared = {n: given[n] for n in SHARED_INPUTS}
    per_example = {n: given[n] for n in ['x', 'positions']}
    grad_fn = _jax.value_and_grad(_loss, argnums=(0, 1))

    def one_microbatch(ex, loss_target):
        ex = dict(ex)
        diff = ex.pop(TWIN_DIFF_INPUT)
        return grad_fn(weights, diff, {**shared, **ex}, loss_target)

    if N_MICROBATCH == 1:
        loss, (grad_w, grad_x) = one_microbatch(per_example, given["loss_target"])
    else:
        def body(carry, xs):
            loss_sum, grad_sum = carry
            l_k, (gw_k, gx_k) = one_microbatch(xs[0], xs[1])
            with _jax.named_scope("update"):
                return (loss_sum + l_k, _jax.tree.map(_jnp.add, grad_sum, gw_k)), gx_k

        init = (_jnp.zeros((), _jnp.float32), _jax.tree.map(_jnp.zeros_like, weights))
        (loss, grad_w), grad_x = _jax.lax.scan(body, init, (per_example, given["loss_target"]))
    with _jax.named_scope("update"):
        delta_w, new_m, new_v = {}, {}, {}
        for n in TWIN_WEIGHTS:
            delta_w[n], new_m[n], new_v[n] = _adamw(weights[n], grad_w[n], given["m_" + n], given["v_" + n])
    return (loss, grad_x, *[grad_w[n] for n in TWIN_WEIGHTS], *[delta_w[n] for n in TWIN_WEIGHTS],
            *[new_m[n] for n in TWIN_WEIGHTS], *[new_v[n] for n in TWIN_WEIGHTS])
```

```python
import functools
import math

import jax
import jax.numpy as jnp
from jax import lax
from jax.experimental import pallas as pl
from jax.experimental.pallas import tpu as pltpu

F32 = jnp.float32
BF16 = jnp.bfloat16
MESH = pl.DeviceIdType.MESH

NORM_EPS = 1e-6
NEG = -1e30
LANES = 128
VMEM_LIMIT = 56 * 1024 * 1024

A_GROUPS = 3
A_DILATIONS = (1, 4, 16)
A_HEADS = 8
A_HEAD_DIM = 128
A_WIDTH = A_HEADS * A_HEAD_DIM
A_ROT_HALF = A_HEAD_DIM // 8
A_ROPE_THETA = 500000.0
A_IN_WIDTH = A_GROUPS * 3 * A_WIDTH + A_WIDTH

B_HEADS = 16
B_NOPE = 64
B_ROPE = 32
B_QK_DIM = B_NOPE + B_ROPE
B_VDIM = 64
B_WIDTH = B_HEADS * B_VDIM
B_Q_LORA = 384
B_KV_LORA = 256
B_ROPE_THETA = 10000.0

ADAM_LR = 0.001
ADAM_B1 = 0.9
ADAM_B2 = 0.999
ADAM_EPS = 1e-08
ADAM_WD = 0.01
ADAM_STEP = 10

N_CHIPS = 4
PACK_COLS = 512


def _params(sem=None):
    return pltpu.CompilerParams(dimension_semantics=sem, vmem_limit_bytes=VMEM_LIMIT)


def _tile(n, want):
    t = min(n, want)
    assert n % t == 0, (n, want)
    return t


def _row_tile(n, want):
    for t in range(min(n, want), 0, -1):
        if n % t == 0 and (t % 8 == 0 or t == n):
            return t
    return n


def _rope_tables(positions, theta, lane0):
    half = 16
    inv_freq = 1.0 / (theta ** (jnp.arange(half, dtype=F32) * (2.0 / (2 * half))))
    ang = positions.astype(F32).reshape(-1)[:, None] * inv_freq
    cos, sin = jnp.cos(ang), jnp.sin(ang)
    n = ang.shape[0]
    pre = jnp.zeros((n, lane0), F32)
    post = jnp.zeros((n, LANES - lane0 - 2 * half), F32)
    z16 = jnp.zeros((n, half), F32)
    c = jnp.concatenate([pre + 1.0, cos, cos, post + 1.0], axis=1)
    sa = jnp.concatenate([pre, -sin, z16, post], axis=1)
    sb = jnp.concatenate([pre, z16, sin, post], axis=1)
    return c, sa, sb


def _rope_apply(x, c, sa, sb, sign):
    k = x.shape[1] // LANES
    if k > 1:
        c, sa, sb = (jnp.concatenate([t] * k, axis=1) for t in (c, sa, sb))
    w = x.shape[1]
    up = pltpu.roll(x, w - 16, 1)
    dn = pltpu.roll(x, 16, 1)
    if sign > 0:
        return x * c + up * sa + dn * sb
    return x * c - up * sa - dn * sb


def _matmul(a, b, mode, out_dtype, *, name, tm=512, tn=1024, tk=1024, add=None, rope=None,
            b_koff=0, out_into=None, out_joff=0):
    if mode == "nn":
        m, k = a.shape
        n = b.shape[1]
    elif mode == "nt":
        m, k = a.shape
        n = b.shape[0]
    else:
        k, m = a.shape
        n = b.shape[1]
    tm, tn, tk = _tile(m, tm), _tile(n, tn), _tile(k, tk)
    nk = k // tk
    if mode == "nn":
        a_spec = pl.BlockSpec((tm, tk), lambda j, i, kk: (i, kk))
        b_spec = pl.BlockSpec((tk, tn), lambda j, i, kk: (kk, j))
        dims = (((1,), (0,)), ((), ()))
    elif mode == "nt":
        a_spec = pl.BlockSpec((tm, tk), lambda j, i, kk: (i, kk))
        b_spec = pl.BlockSpec((tn, tk), lambda j, i, kk: (j, kk + b_koff))
        dims = (((1,), (1,)), ((), ()))
    else:
        a_spec = pl.BlockSpec((tk, tm), lambda j, i, kk: (kk, i))
        b_spec = pl.BlockSpec((tk, tn), lambda j, i, kk: (kk, j))
        dims = (((0,), (0,)), ((), ()))
    operands = [a, b]
    in_specs = [a_spec, b_spec]
    if add is not None:
        operands.append(add)
        in_specs.append(pl.BlockSpec((tm, tn), lambda j, i, kk: (i, j)))
    if rope is not None:
        tables, rope_pred = rope
        for t in tables:
            operands.append(t)
            in_specs.append(pl.BlockSpec((tm, LANES), lambda j, i, kk: (i, 0)))
    aliases = {}
    if out_into is not None:
        aliases = {len(operands): 0}
        operands.append(out_into)
        in_specs.append(pl.BlockSpec(memory_space=pl.ANY))
        out_shape = jax.ShapeDtypeStruct(out_into.shape, out_into.dtype)
        out_dtype = out_into.dtype
    else:
        out_shape = jax.ShapeDtypeStruct((m, n), out_dtype)
    out_spec = pl.BlockSpec((tm, tn), lambda j, i, kk: (i, j + out_joff))

    def body(*refs):
        a_ref, b_ref = refs[0], refs[1]
        pos = 2
        add_ref = None
        if add is not None:
            add_ref = refs[pos]
            pos += 1
        tab_refs = None
        if rope is not None:
            tab_refs = refs[pos:pos + 3]
            pos += 3
        if out_into is not None:
            pos += 1
        o_ref = refs[pos]
        acc_ref = refs[pos + 1] if nk > 1 else None

        def finish(res):
            if add_ref is not None:
                res = res + add_ref[...].astype(F32)
            if tab_refs is None:
                o_ref[...] = res.astype(o_ref.dtype)
                return
            flag = rope_pred(pl.program_id(0))
            roped = _rope_apply(res, tab_refs[0][...], tab_refs[1][...], tab_refs[2][...], 1)
            if flag is True:
                o_ref[...] = roped.astype(o_ref.dtype)
                return

            @pl.when(flag)
            def _():
                o_ref[...] = roped.astype(o_ref.dtype)

            @pl.when(jnp.logical_not(flag))
            def _():
                o_ref[...] = res.astype(o_ref.dtype)

        part = lax.dot_general(a_ref[...].astype(BF16), b_ref[...].astype(BF16), dims,
                               preferred_element_type=F32)
        if nk == 1:
            finish(part)
            return
        kk = pl.program_id(2)

        @pl.when(kk == 0)
        def _():
            acc_ref[...] = part

        @pl.when(kk > 0)
        def _():
            acc_ref[...] += part

        @pl.when(kk == nk - 1)
        def _():
            finish(acc_ref[...])

    return pl.pallas_call(
        body, name=name, grid=(n // tn, m // tm, nk), in_specs=in_specs, out_specs=out_spec,
        out_shape=out_shape, input_output_aliases=aliases,
        scratch_shapes=[pltpu.VMEM((tm, tn), F32)] if nk > 1 else [],
        compiler_params=_params(("parallel", "parallel", "arbitrary")),
    )(*operands)


def _rms_fwd(x, g, out_dtype, *, name, add=None, tr=512):
    n, d = x.shape
    tr = _tile(n, tr)
    row = pl.BlockSpec((tr, d), lambda i: (i, 0))
    vec = pl.BlockSpec((1, d), lambda i: (0, 0))

    def body(*refs):
        x_ref, g_ref = refs[0], refs[1]
        o_ref = refs[-1]
        xv = x_ref[...].astype(F32)
        r = lax.rsqrt(jnp.mean(xv * xv, axis=-1, keepdims=True) + NORM_EPS)
        y = xv * r * g_ref[...]
        if add is not None:
            y = refs[2][...] + y
        o_ref[...] = y.astype(o_ref.dtype)

    ops = [x, g] + ([add] if add is not None else [])
    specs = [row, vec] + ([row] if add is not None else [])
    return pl.pallas_call(
        body, name=name, grid=(n // tr,), in_specs=specs, out_specs=row,
        out_shape=jax.ShapeDtypeStruct((n, d), out_dtype), compiler_params=_params(("parallel",)),
    )(*ops)


def _rms_bwd(x, g, dy, out_dtype, *, name, adds=(), tr=512):
    n, d = x.shape
    tr = _tile(n, tr)
    steps = n // tr
    row = pl.BlockSpec((tr, d), lambda i: (i, 0))
    vec = pl.BlockSpec((1, d), lambda i: (0, 0))
    na = len(adds)

    def body(*refs):
        x_ref, g_ref, dy_ref = refs[:3]
        add_refs = refs[3:3 + na]
        dx_ref, dg_ref, acc_ref = refs[3 + na:]
        i = pl.program_id(0)
        xv = x_ref[...].astype(F32)
        r = lax.rsqrt(jnp.mean(xv * xv, axis=-1, keepdims=True) + NORM_EPS)
        xh = xv * r
        dyv = dy_ref[...].astype(F32)
        part = (dyv * xh).reshape(tr // 8, 8, d).sum(axis=0)

        @pl.when(i == 0)
        def _():
            acc_ref[...] = part

        @pl.when(i > 0)
        def _():
            acc_ref[...] += part

        t = dyv * g_ref[...]
        dx = r * (t - xh * jnp.mean(t * xh, axis=-1, keepdims=True))
        for a_ref in add_refs:
            dx = dx + a_ref[...].astype(F32)
        dx_ref[...] = dx.astype(dx_ref.dtype)

        @pl.when(i == steps - 1)
        def _():
            dg_ref[...] = jnp.sum(acc_ref[...], axis=0, keepdims=True)

    return pl.pallas_call(
        body, name=name, grid=(steps,), in_specs=[row, vec, row] + [row] * na,
        out_specs=(row, vec),
        out_shape=(jax.ShapeDtypeStruct((n, d), out_dtype), jax.ShapeDtypeStruct((1, d), F32)),
        scratch_shapes=[pltpu.VMEM((8, d), F32)], compiler_params=_params(("arbitrary",)),
    )(x, g, dy, *adds)


def _kv_latent_fwd(ckr, g_lat, tabs, *, tr=512):
    n = ckr.shape[0]
    tr = _tile(n, tr)
    lat = B_KV_LORA

    def body(c_ref, k_ref, g_ref, tc, tsa, tsb, ckv_ref, kr_ref):
        xv = c_ref[...]
        r = lax.rsqrt(jnp.mean(xv * xv, axis=-1, keepdims=True) + NORM_EPS)
        ckv_ref[...] = (xv * r * g_ref[...]).astype(BF16)
        kr_ref[...] = _rope_apply(k_ref[...], tc[...], tsa[...], tsb[...], 1).astype(BF16)

    tab = pl.BlockSpec((tr, LANES), lambda i: (i, 0))
    return pl.pallas_call(
        body, name="kv_latent_fwd", grid=(n // tr,),
        in_specs=[pl.BlockSpec((tr, lat), lambda i: (i, 0)),
                  pl.BlockSpec((tr, LANES), lambda i: (i, lat // LANES)),
                  pl.BlockSpec((1, lat), lambda i: (0, 0)), tab, tab, tab],
        out_specs=(pl.BlockSpec((tr, lat), lambda i: (i, 0)), tab),
        out_shape=(jax.ShapeDtypeStruct((n, lat), BF16), jax.ShapeDtypeStruct((n, LANES), BF16)),
        compiler_params=_params(("parallel",)),
    )(ckr, ckr, g_lat, *tabs)


def _kv_latent_bwd(dckv, ckr, g_lat, dk_cat, tabs, *, tr=512):
    n = ckr.shape[0]
    tr = _tile(n, tr)
    steps = n // tr
    lat = B_KV_LORA
    wk = dk_cat.shape[1]

    def body(d_ref, c_ref, g_ref, dk_ref, tc, tsa, tsb, o_ref, dg_ref, acc_ref):
        i = pl.program_id(0)
        xv = c_ref[...]
        r = lax.rsqrt(jnp.mean(xv * xv, axis=-1, keepdims=True) + NORM_EPS)
        xh = xv * r
        dyv = d_ref[...]
        part = (dyv * xh).reshape(tr // 8, 8, lat).sum(axis=0)

        @pl.when(i == 0)
        def _():
            acc_ref[...] = part

        @pl.when(i > 0)
        def _():
            acc_ref[...] += part

        t = dyv * g_ref[...]
        dx = r * (t - xh * jnp.mean(t * xh, axis=-1, keepdims=True))
        o_ref[:, 0:lat] = dx.astype(o_ref.dtype)
        dkr = dk_ref[:, 0:LANES].astype(F32)
        for h in range(1, wk // LANES):
            dkr = dkr + dk_ref[:, h * LANES:(h + 1) * LANES].astype(F32)
        o_ref[:, lat:lat + LANES] = _rope_apply(dkr, tc[...], tsa[...], tsb[...], -1).astype(o_ref.dtype)

        @pl.when(i == steps - 1)
        def _():
            dg_ref[...] = jnp.sum(acc_ref[...], axis=0, keepdims=True)

    tab = pl.BlockSpec((tr, LANES), lambda i: (i, 0))
    return pl.pallas_call(
        body, name="kv_latent_bwd", grid=(steps,),
        in_specs=[pl.BlockSpec((tr, lat), lambda i: (i, 0)), pl.BlockSpec((tr, lat), lambda i: (i, 0)),
                  pl.BlockSpec((1, lat), lambda i: (0, 0)), pl.BlockSpec((tr, wk), lambda i: (i, 0)),
                  tab, tab, tab],
        out_specs=(pl.BlockSpec((tr, lat + LANES), lambda i: (i, 0)), pl.BlockSpec((1, lat), lambda i: (0, 0))),
        out_shape=(jax.ShapeDtypeStruct((n, lat + LANES), BF16), jax.ShapeDtypeStruct((1, lat), F32)),
        scratch_shapes=[pltpu.VMEM((8, lat), F32)], compiler_params=_params(("arbitrary",)),
    )(dckv, ckr, g_lat, dk_cat, *tabs)


def _sigmoid(z):
    return 1.0 / (1.0 + jnp.exp(-z))


def _lane_place(cols, width):
    rows = cols[0].shape[0]
    lane = lax.broadcasted_iota(jnp.int32, (rows, width), 1)
    out = jnp.zeros((rows, width), F32)
    for h, col in enumerate(cols):
        out = jnp.where(lane == h, col, out)
    return out


def _merge_gate_fwd(outs, lses, proj, z_block, *, tr=256):
    n, w = outs[0].shape
    tr = _tile(n, tr)
    ng = len(outs)

    def body(*refs):
        o_refs = refs[:ng]
        l_refs = refs[ng:2 * ng]
        z_ref = refs[2 * ng]
        y_ref, om_ref, lse_ref = refs[2 * ng + 1:]
        ls = [r[...] for r in l_refs]
        mx = ls[0]
        for l in ls[1:]:
            mx = jnp.maximum(mx, l)
        ssum = jnp.exp(ls[0] - mx)
        for l in ls[1:]:
            ssum = ssum + jnp.exp(l - mx)
        tot = mx + jnp.log(ssum)
        lse_ref[...] = tot
        ws = [jnp.exp(l - tot) for l in ls]
        for h in range(A_HEADS):
            sl = slice(h * A_HEAD_DIM, (h + 1) * A_HEAD_DIM)
            o = ws[0][:, h:h + 1] * o_refs[0][:, sl]
            for gi in range(1, ng):
                o = o + ws[gi][:, h:h + 1] * o_refs[gi][:, sl]
            z = z_ref[:, sl].astype(F32)
            om_ref[:, sl] = o.astype(BF16)
            y_ref[:, sl] = (o * (z * _sigmoid(z))).astype(BF16)

    row = pl.BlockSpec((tr, w), lambda i: (i, 0))
    lrow = pl.BlockSpec((tr, A_HEADS), lambda i: (i, 0))
    return pl.pallas_call(
        body, name="merge_gate_fwd", grid=(n // tr,),
        in_specs=[row] * ng + [lrow] * ng + [pl.BlockSpec((tr, w), lambda i: (i, z_block))],
        out_specs=(row, row, lrow),
        out_shape=(jax.ShapeDtypeStruct((n, w), BF16), jax.ShapeDtypeStruct((n, w), BF16),
                   jax.ShapeDtypeStruct((n, A_HEADS), F32)),
        compiler_params=_params(("parallel",)),
    )(*outs, *lses, proj)


def _gate_bwd(dy, o, z_arr, z_block, *, name, with_delta, tr=256):
    n, w = dy.shape
    tr = _tile(n, tr)

    def body(*refs):
        dy_ref, o_ref, z_ref, do_ref, dz_ref = refs[:5]
        dyv = dy_ref[...].astype(F32)
        ov = o_ref[...].astype(F32)
        z = z_ref[...].astype(F32)
        sig = _sigmoid(z)
        do = dyv * (z * sig)
        do_ref[...] = do.astype(BF16)
        dz_ref[...] = (dyv * ov * (sig * (1.0 + z * (1.0 - sig)))).astype(BF16)
        if with_delta:
            prod = do * ov
            cols = [jnp.sum(prod[:, h * A_HEAD_DIM:(h + 1) * A_HEAD_DIM], axis=-1, keepdims=True)
                    for h in range(A_HEADS)]
            refs[5][...] = _lane_place(cols, A_HEADS)

    row = pl.BlockSpec((tr, w), lambda i: (i, 0))
    out_specs = [row, row]
    out_shape = [jax.ShapeDtypeStruct((n, w), BF16), jax.ShapeDtypeStruct((n, w), BF16)]
    if with_delta:
        out_specs.append(pl.BlockSpec((tr, A_HEADS), lambda i: (i, 0)))
        out_shape.append(jax.ShapeDtypeStruct((n, A_HEADS), F32))
    return pl.pallas_call(
        body, name=name, grid=(n // tr,),
        in_specs=[row, row, pl.BlockSpec((tr, w), lambda i: (i, z_block))],
        out_specs=tuple(out_specs), out_shape=tuple(out_shape), compiler_params=_params(("parallel",)),
    )(dy, o, z_arr)


def _loss_fwd_bwd(h, target, *, tr=512):
    n, d = h.shape
    tr = _tile(n, tr)
    steps = n // tr

    def body(h_ref, t_ref, dh_ref, loss_ref, acc_ref):
        i = pl.program_id(0)
        e = h_ref[...] - t_ref[...]
        dh_ref[...] = e / d
        part = (e * e).reshape(tr // 8, 8, d).sum(axis=0)

        @pl.when(i == 0)
        def _():
            acc_ref[...] = part

        @pl.when(i > 0)
        def _():
            acc_ref[...] += part

        @pl.when(i == steps - 1)
        def _():
            s = jnp.sum(jnp.sum(acc_ref[...], axis=-1, keepdims=True), axis=0, keepdims=True)
            loss_ref[...] = 0.5 * s / d

    row = pl.BlockSpec((tr, d), lambda i: (i, 0))
    return pl.pallas_call(
        body, name="loss", grid=(steps,), in_specs=[row, row],
        out_specs=(row, pl.BlockSpec((1, 1), lambda i: (0, 0))),
        out_shape=(jax.ShapeDtypeStruct((n, d), F32), jax.ShapeDtypeStruct((1, 1), F32)),
        scratch_shapes=[pltpu.VMEM((8, d), F32)], compiler_params=_params(("arbitrary",)),
    )(h, target)


def _dot_nt(a, b):
    return lax.dot_general(a, b, (((1,), (1,)), ((), ())), preferred_element_type=F32)


def _dot_nn(a, b):
    return lax.dot_general(a, b, (((1,), (0,)), ((), ())), preferred_element_type=F32)


def _dot_tn(a, b):
    return lax.dot_general(a, b, (((0,), (0,)), ((), ())), preferred_element_type=F32)


def _attn_a_fwd(qkv, cb0, qb, *, name):
    bl, dil, ln, _ = qkv.shape
    nb = ln // qb
    scale = A_HEAD_DIM ** -0.5
    hw = A_WIDTH

    def body(q_ref, kc_ref, kp_ref, vc_ref, vp_ref, o_ref, lse_ref):
        i = pl.program_id(2)
        qi = lax.broadcasted_iota(jnp.int32, (qb, qb), 0)
        ki = lax.broadcasted_iota(jnp.int32, (qb, qb), 1)
        mask_c = ki <= qi
        mask_p = jnp.logical_and(ki >= qi, i >= 1)
        cols = []
        for h in range(A_HEADS):
            sl = slice(h * A_HEAD_DIM, (h + 1) * A_HEAD_DIM)
            q = q_ref[:, sl]
            s_c = jnp.where(mask_c, _dot_nt(q, kc_ref[:, sl]) * scale, NEG)
            m = jnp.max(s_c, axis=-1, keepdims=True)
            if nb > 1:
                s_p = jnp.where(mask_p, _dot_nt(q, kp_ref[:, sl]) * scale, NEG)
                m = jnp.maximum(m, jnp.max(s_p, axis=-1, keepdims=True))
            p_c = jnp.exp(s_c - m)
            l = jnp.sum(p_c, axis=-1, keepdims=True)
            acc = _dot_nn(p_c.astype(BF16), vc_ref[:, sl])
            if nb > 1:
                p_p = jnp.exp(s_p - m)
                l = l + jnp.sum(p_p, axis=-1, keepdims=True)
                acc = acc + _dot_nn(p_p.astype(BF16), vp_ref[:, sl])
            o_ref[:, sl] = acc / l
            cols.append(m + jnp.log(l))
        lse_ref[...] = _lane_place(cols, A_HEADS)

    def spec(off, prev):
        if prev:
            return pl.BlockSpec((None, None, qb, hw), lambda b, r, i: (b, r, jnp.maximum(i - 1, 0), cb0 + off))
        return pl.BlockSpec((None, None, qb, hw), lambda b, r, i: (b, r, i, cb0 + off))

    return pl.pallas_call(
        body, name=name, grid=(bl, dil, nb),
        in_specs=[spec(0, False), spec(1, False), spec(1, True), spec(2, False), spec(2, True)],
        out_specs=(pl.BlockSpec((None, None, qb, hw), lambda b, r, i: (b, r, i, 0)),
                   pl.BlockSpec((None, None, qb, A_HEADS), lambda b, r, i: (b, r, i, 0))),
        out_shape=(jax.ShapeDtypeStruct((bl, dil, ln, hw), F32),
                   jax.ShapeDtypeStruct((bl, dil, ln, A_HEADS), F32)),
        compiler_params=_params(("parallel", "parallel", "arbitrary")),
    )(qkv, qkv, qkv, qkv, qkv)


def _attn_a_bwd(qkv, cb0, do, lse, delta, tabs, qb, *, name):
    bl, dil, ln, _ = qkv.shape
    nb = ln // qb
    scale = A_HEAD_DIM ** -0.5
    hw = A_WIDTH

    def body(q_ref, qn_ref, kc_ref, kp_ref, vc_ref, vp_ref, do_ref, don_ref,
             lse_ref, lsen_ref, dl_ref, dln_ref, tc, tsa, tsb, o_ref):
        i = pl.program_id(2)
        qi = lax.broadcasted_iota(jnp.int32, (qb, qb), 0)
        ki = lax.broadcasted_iota(jnp.int32, (qb, qb), 1)
        mask_c = ki <= qi
        mask_p = jnp.logical_and(ki >= qi, i >= 1)
        mask_n = jnp.logical_and(ki >= qi, i + 1 < nb)
        c, sa, sb = tc[...], tsa[...], tsb[...]
        for h in range(A_HEADS):
            sl = slice(h * A_HEAD_DIM, (h + 1) * A_HEAD_DIM)
            q, kc, vc, dov = q_ref[:, sl], kc_ref[:, sl], vc_ref[:, sl], do_ref[:, sl]
            lse_h = lse_ref[:, h:h + 1]
            dl_h = dl_ref[:, h:h + 1]
            p = jnp.exp(jnp.where(mask_c, _dot_nt(q, kc) * scale, NEG) - lse_h)
            ds = (p * (_dot_nt(dov, vc) - dl_h) * scale).astype(BF16)
            dq = _dot_nn(ds, kc)
            dk = _dot_tn(ds, q)
            dv = _dot_tn(p.astype(BF16), dov)
            if nb > 1:
                kp, vp = kp_ref[:, sl], vp_ref[:, sl]
                p = jnp.exp(jnp.where(mask_p, _dot_nt(q, kp) * scale, NEG) - lse_h)
                ds = (p * (_dot_nt(dov, vp) - dl_h) * scale).astype(BF16)
                dq = dq + _dot_nn(ds, kp)
                qn, don = qn_ref[:, sl], don_ref[:, sl]
                p = jnp.exp(jnp.where(mask_n, _dot_nt(qn, kc) * scale, NEG) - lsen_ref[:, h:h + 1])
                ds = (p * (_dot_nt(don, vc) - dln_ref[:, h:h + 1]) * scale).astype(BF16)
                dk = dk + _dot_tn(ds, qn)
                dv = dv + _dot_tn(p.astype(BF16), don)
            o_ref[:, h * A_HEAD_DIM:(h + 1) * A_HEAD_DIM] = _rope_apply(dq, c, sa, sb, -1).astype(BF16)
            o_ref[:, hw + h * A_HEAD_DIM:hw + (h + 1) * A_HEAD_DIM] = _rope_apply(dk, c, sa, sb, -1).astype(BF16)
            o_ref[:, 2 * hw + h * A_HEAD_DIM:2 * hw + (h + 1) * A_HEAD_DIM] = dv.astype(BF16)

    def cur(w, col):
        return pl.BlockSpec((None, None, qb, w), lambda b, r, i: (b, r, i, col))

    def prev(w, col):
        return pl.BlockSpec((None, None, qb, w), lambda b, r, i: (b, r, jnp.maximum(i - 1, 0), col))

    def nxt(w, col):
        return pl.BlockSpec((None, None, qb, w), lambda b, r, i: (b, r, jnp.minimum(i + 1, nb - 1), col))

    return pl.pallas_call(
        body, name=name, grid=(bl, dil, nb),
        in_specs=[cur(hw, cb0), nxt(hw, cb0), cur(hw, cb0 + 1), prev(hw, cb0 + 1),
                  cur(hw, cb0 + 2), prev(hw, cb0 + 2), cur(hw, 0), nxt(hw, 0),
                  cur(A_HEADS, 0), nxt(A_HEADS, 0), cur(A_HEADS, 0), nxt(A_HEADS, 0),
                  cur(LANES, 0), cur(LANES, 0), cur(LANES, 0)],
        out_specs=cur(3 * hw, 0),
        out_shape=jax.ShapeDtypeStruct((bl, dil, ln, 3 * hw), BF16),
        compiler_params=_params(("parallel", "parallel", "arbitrary")),
    )(qkv, qkv, qkv, qkv, qkv, qkv, do, do, lse, lse, delta, delta, *tabs)


def _causal_mask(q0, k0, tq, tk):
    qi = lax.broadcasted_iota(jnp.int32, (tq, tk), 0) + q0
    ki = lax.broadcasted_iota(jnp.int32, (tq, tk), 1) + k0
    return ki <= qi


def _attn_b_fwd(q_cat, kvup, kr, z, tq):
    bl, t, _ = q_cat.shape
    nq = t // tq
    pairs = B_HEADS // 2
    scale = B_QK_DIM ** -0.5
    v_blk0 = (B_HEADS * LANES) // LANES

    def body(q_ref, k_ref, v_ref, kr_ref, z_ref, y_ref, o_ref, lse_ref):
        qi = pl.program_id(2)
        outs, lses = [], []
        for e in range(2):
            sl = slice(e * LANES, (e + 1) * LANES)
            q = q_ref[:, sl]

            def step(kb, carry, sl=sl, q=q):
                m, l, acc = carry
                k0 = pl.multiple_of(kb * tq, tq)
                k = k_ref[pl.ds(k0, tq), sl] + kr_ref[pl.ds(k0, tq), :]
                s = jnp.where(_causal_mask(qi * tq, k0, tq, tq), _dot_nt(q, k) * scale, NEG)
                m_new = jnp.maximum(m, jnp.max(s, axis=-1, keepdims=True))
                alpha = jnp.exp(m - m_new)
                p = jnp.exp(s - m_new)
                l = alpha * l + jnp.sum(p, axis=-1, keepdims=True)
                acc = alpha * acc + _dot_nn(p.astype(BF16), v_ref[pl.ds(k0, tq), :])
                return m_new, l, acc

            init = (jnp.full((tq, 1), NEG, F32), jnp.zeros((tq, 1), F32), jnp.zeros((tq, LANES), F32))
            m, l, acc = lax.fori_loop(0, qi + 1, step, init)
            outs.append(acc / l)
            lses.append(m + jnp.log(l))
        lane = lax.broadcasted_iota(jnp.int32, (tq, LANES), 1)
        first = lane < B_VDIM
        o = jnp.where(first, outs[0], outs[1])
        zv = z_ref[...].astype(F32)
        o_ref[...] = o.astype(BF16)
        y_ref[...] = (o * (zv * _sigmoid(zv))).astype(BF16)
        lse_ref[...] = jnp.where(first, lses[0], lses[1])

    blk = pl.BlockSpec((None, tq, LANES), lambda b, j, i: (b, i, j))
    return pl.pallas_call(
        body, name="attn_b_fwd", grid=(bl, pairs, nq),
        in_specs=[pl.BlockSpec((None, tq, 2 * LANES), lambda b, j, i: (b, i, j)),
                  pl.BlockSpec((None, t, 2 * LANES), lambda b, j, i: (b, 0, j)),
                  pl.BlockSpec((None, t, LANES), lambda b, j, i: (b, 0, v_blk0 + j)),
                  pl.BlockSpec((None, t, LANES), lambda b, j, i: (b, 0, 0)),
                  blk],
        out_specs=(blk, blk, blk),
        out_shape=(jax.ShapeDtypeStruct((bl, t, B_WIDTH), BF16), jax.ShapeDtypeStruct((bl, t, B_WIDTH), BF16),
                   jax.ShapeDtypeStruct((bl, t, B_WIDTH), F32)),
        compiler_params=_params(("parallel", "parallel", "arbitrary")),
    )(q_cat, kvup, kvup, kr, z)


def _head_terms(do, o, lse, e):
    rows = do.shape[0]
    lane = lax.broadcasted_iota(jnp.int32, (rows, LANES), 1)
    mine = (lane < B_VDIM) if e == 0 else (lane >= B_VDIM)
    prod = do.astype(F32) * o.astype(F32)
    dl = jnp.sum(jnp.where(mine, prod, 0.0), axis=-1, keepdims=True)
    do_e = jnp.where(mine, do, jnp.zeros_like(do))
    return do_e, dl, lse[:, e * B_VDIM:e * B_VDIM + 1]


def _attn_b_dq(q_cat, kvup, kr, do, o, lse, tabs, tq):
    bl, t, _ = q_cat.shape
    nq = t // tq
    pairs = B_HEADS // 2
    scale = B_QK_DIM ** -0.5
    v_blk0 = (B_HEADS * LANES) // LANES

    def body(q_ref, k_ref, v_ref, kr_ref, do_ref, o_ref, lse_ref, tc, tsa, tsb, dq_ref):
        qi = pl.program_id(2)
        dov, ov, lsev = do_ref[...], o_ref[...], lse_ref[...]
        for e in range(2):
            sl = slice(e * LANES, (e + 1) * LANES)
            q = q_ref[:, sl]
            do_e, dl, lse_e = _head_terms(dov, ov, lsev, e)

            def step(kb, dq, sl=sl, q=q, do_e=do_e, dl=dl, lse_e=lse_e):
                k0 = pl.multiple_of(kb * tq, tq)
                k = k_ref[pl.ds(k0, tq), sl] + kr_ref[pl.ds(k0, tq), :]
                s = jnp.where(_causal_mask(qi * tq, k0, tq, tq), _dot_nt(q, k) * scale, NEG)
                p = jnp.exp(s - lse_e)
                dp = _dot_nt(do_e, v_ref[pl.ds(k0, tq), :])
                ds = (p * (dp - dl) * scale).astype(BF16)
                return dq + _dot_nn(ds, k)

            dq = lax.fori_loop(0, qi + 1, step, jnp.zeros((tq, LANES), F32))
            dq_ref[:, sl] = _rope_apply(dq, tc[...], tsa[...], tsb[...], -1).astype(BF16)

    blk = pl.BlockSpec((None, tq, LANES), lambda b, j, i: (b, i, j))
    tab = pl.BlockSpec((None, tq, LANES), lambda b, j, i: (b, i, 0))
    qblk = pl.BlockSpec((None, tq, 2 * LANES), lambda b, j, i: (b, i, j))
    return pl.pallas_call(
        body, name="attn_b_dq", grid=(bl, pairs, nq),
        in_specs=[qblk,
                  pl.BlockSpec((None, t, 2 * LANES), lambda b, j, i: (b, 0, j)),
                  pl.BlockSpec((None, t, LANES), lambda b, j, i: (b, 0, v_blk0 + j)),
                  pl.BlockSpec((None, t, LANES), lambda b, j, i: (b, 0, 0)),
                  blk, blk, blk, tab, tab, tab],
        out_specs=qblk,
        out_shape=jax.ShapeDtypeStruct((bl, t, B_HEADS * LANES), BF16),
        compiler_params=_params(("parallel", "parallel", "arbitrary")),
    )(q_cat, kvup, kvup, kr, do, o, lse, *tabs)


def _attn_b_dkv(q_cat, kvup, kr, do, o, lse, tq):
    bl, t, _ = q_cat.shape
    nq = t // tq
    pairs = B_HEADS // 2
    scale = B_QK_DIM ** -0.5
    v_blk0 = (B_HEADS * LANES) // LANES

    def body(q_ref, k_ref, v_ref, kr_ref, do_ref, o_ref, lse_ref, dk_ref, dv_ref):
        kb = pl.program_id(2)
        v = v_ref[...]
        dv = jnp.zeros((tq, LANES), F32)
        for e in range(2):
            sl = slice(e * LANES, (e + 1) * LANES)
            k = k_ref[:, sl] + kr_ref[...]

            def step(qbi, carry, sl=sl, k=k, e=e):
                dk, dv = carry
                q0 = pl.multiple_of(qbi * tq, tq)
                rows = pl.ds(q0, tq)
                q = q_ref[rows, sl]
                do_e, dl, lse_e = _head_terms(do_ref[rows, :], o_ref[rows, :], lse_ref[rows, :], e)
                s = jnp.where(_causal_mask(q0, kb * tq, tq, tq), _dot_nt(q, k) * scale, NEG)
                p = jnp.exp(s - lse_e)
                dv = dv + _dot_tn(p.astype(BF16), do_e)
                ds = (p * (_dot_nt(do_e, v) - dl) * scale).astype(BF16)
                return dk + _dot_tn(ds, q), dv

            dk, dv = lax.fori_loop(kb, nq, step, (jnp.zeros((tq, LANES), F32), dv))
            dk_ref[:, sl] = dk.astype(BF16)
        dv_ref[...] = dv.astype(BF16)

    full = pl.BlockSpec((None, t, LANES), lambda b, j, i: (b, 0, j))
    kblk = pl.BlockSpec((None, tq, 2 * LANES), lambda b, j, i: (b, i, j))
    return pl.pallas_call(
        body, name="attn_b_dkv", grid=(bl, pairs, nq),
        in_specs=[pl.BlockSpec((None, t, 2 * LANES), lambda b, j, i: (b, 0, j)),
                  kblk,
                  pl.BlockSpec((None, tq, LANES), lambda b, j, i: (b, i, v_blk0 + j)),
                  pl.BlockSpec((None, tq, LANES), lambda b, j, i: (b, i, 0)),
                  full, full, full],
        out_specs=(kblk, pl.BlockSpec((None, tq, LANES), lambda b, j, i: (b, i, j))),
        out_shape=(jax.ShapeDtypeStruct((bl, t, B_HEADS * LANES), BF16),
                   jax.ShapeDtypeStruct((bl, t, B_WIDTH), BF16)),
        compiler_params=_params(("parallel", "parallel", "arbitrary")),
    )(q_cat, kvup, kvup, kr, do, o, lse)


def _adamw(w, g, m, v, *, name):
    r, c = w.shape
    tr = _row_tile(r, 256)
    c1 = 1.0 - ADAM_B1
    c2 = 1.0 - ADAM_B2
    bc1 = 1.0 - ADAM_B1 ** ADAM_STEP
    bc2 = 1.0 - ADAM_B2 ** ADAM_STEP

    def body(w_ref, g_ref, m_ref, v_ref, d_ref, nm_ref, nv_ref):
        gv = g_ref[...]
        nm = ADAM_B1 * m_ref[...] + c1 * gv
        nv = ADAM_B2 * v_ref[...] + c2 * (gv * gv)
        nm_ref[...] = nm
        nv_ref[...] = nv
        d_ref[...] = -ADAM_LR * ((nm / bc1) / (jnp.sqrt(nv / bc2) + ADAM_EPS) + ADAM_WD * w_ref[...])

    blk = pl.BlockSpec((tr, c), lambda i: (i, 0))
    sds = jax.ShapeDtypeStruct((r, c), F32)
    return pl.pallas_call(
        body, name=name, grid=(r // tr,), in_specs=[blk] * 4, out_specs=(blk,) * 3,
        out_shape=(sds,) * 3, compiler_params=_params(("parallel",)),
    )(w, g, m, v)


def _add_my_half(stacked, other, core, *, name):
    nch, a, c = stacked.shape
    h = a // 2
    tr = _row_tile(h, 256)
    nblk = h // tr

    def body(core_ref, s_ref, p_ref, o_ref):
        o_ref[...] = s_ref[...] + p_ref[...]

    return pl.pallas_call(
        body, name=name,
        grid_spec=pltpu.PrefetchScalarGridSpec(
            num_scalar_prefetch=1, grid=(nch, nblk),
            in_specs=[pl.BlockSpec((None, tr, c), lambda k, i, cr: (k, cr[0] * nblk + i, 0)),
                      pl.BlockSpec((None, tr, c), lambda k, i, cr: (k, i, 0))],
            out_specs=pl.BlockSpec((None, tr, c), lambda k, i, cr: (k, i, 0))),
        out_shape=jax.ShapeDtypeStruct((nch, h, c), F32),
        compiler_params=_params(("parallel", "parallel")),
    )(core, stacked, other)


def _sum_chips(parts, *, name):
    nch, h, c = parts.shape
    tr = _row_tile(h, 256)

    def body(p_ref, o_ref):
        acc = p_ref[0] + p_ref[1]
        for k in range(2, nch):
            acc = acc + p_ref[k]
        o_ref[...] = acc

    return pl.pallas_call(
        body, name=name, grid=(h // tr,),
        in_specs=[pl.BlockSpec((nch, tr, c), lambda i: (0, i, 0))],
        out_specs=pl.BlockSpec((tr, c), lambda i: (i, 0)),
        out_shape=jax.ShapeDtypeStruct((h, c), F32), compiler_params=_params(("parallel",)),
    )(parts)


def _place():
    x, y, c = lax.axis_index("x"), lax.axis_index("y"), lax.axis_index("c")
    chips = [(1 - x, y), (x, 1 - y), (1 - x, 1 - y)]
    return x, y, c, chips


def _remote(src, dst, send_sems, recv_sems, k, to):
    return pltpu.make_async_remote_copy(src_ref=src, dst_ref=dst, send_sem=send_sems.at[k],
                                        recv_sem=recv_sems.at[k], device_id=to, device_id_type=MESH)


def _hbm_call(body, name, ins, out_shapes, n_remote, n_local):
    any_spec = pl.BlockSpec(memory_space=pl.ANY)
    return pl.pallas_call(
        body, name=name, in_specs=[any_spec] * len(ins), out_specs=tuple([any_spec] * len(out_shapes)),
        out_shape=tuple(out_shapes),
        scratch_shapes=[pltpu.SemaphoreType.DMA((n_remote,)), pltpu.SemaphoreType.DMA((n_remote,)),
                        pltpu.SemaphoreType.DMA((n_local,))],
    )(*ins)


def _all_gather_chips(shards, *, name):
    n = len(shards)

    def body(*refs):
        ins, outs = refs[:n], refs[n:2 * n]
        send_sems, recv_sems, local_sems = refs[2 * n:]
        x, y, c, chips = _place()
        me = 2 * x + y
        own = [pltpu.make_async_copy(ins[s], outs[s].at[me], local_sems.at[s]) for s in range(n)]
        for cp in own:
            cp.start()
        sent = []
        for s in range(n):
            h = ins[s].shape[0] // 2
            for j, (px, py) in enumerate(chips):
                cp = _remote(ins[s].at[pl.ds(c * h, h)], outs[s].at[me, pl.ds(c * h, h)],
                             send_sems, recv_sems, s * 6 + j, (px, py, c))
                cp.start()
                sent.append(cp)
        for s in range(n):
            h = ins[s].shape[0] // 2
            for j, (px, py) in enumerate(chips):
                slab = outs[s].at[2 * px + py, pl.ds(c * h, h)]
                _remote(slab, slab, send_sems, recv_sems, s * 6 + j, (px, py, c)).wait_recv()
                cp = _remote(slab, slab, send_sems, recv_sems, s * 6 + 3 + j, (x, y, 1 - c))
                cp.start()
                sent.append(cp)
        for s in range(n):
            h = ins[s].shape[0] // 2
            for j, (px, py) in enumerate(chips):
                slab = outs[s].at[2 * px + py, pl.ds((1 - c) * h, h)]
                _remote(slab, slab, send_sems, recv_sems, s * 6 + 3 + j, (x, y, 1 - c)).wait_recv()
        for cp in sent:
            cp.wait_send()
        for cp in own:
            cp.wait()

    out_shapes = [jax.ShapeDtypeStruct((N_CHIPS,) + s.shape, s.dtype) for s in shards]
    return _hbm_call(body, name, shards, out_shapes, 6 * n, n)


def _pair_send_other_half(stacked, *, name):
    n = len(stacked)

    def body(*refs):
        ins, outs = refs[:n], refs[n:2 * n]
        send_sems, recv_sems, _ = refs[2 * n:]
        x, y, c, _chips = _place()
        sent = []
        for s in range(n):
            h = ins[s].shape[1] // 2
            cp = _remote(ins[s].at[:, pl.ds((1 - c) * h, h)], outs[s], send_sems, recv_sems, s, (x, y, 1 - c))
            cp.start()
            sent.append(cp)
        for cp in sent:
            cp.wait_recv()
        for cp in sent:
            cp.wait_send()

    out_shapes = [jax.ShapeDtypeStruct((s.shape[0], s.shape[1] // 2, s.shape[2]), s.dtype) for s in stacked]
    return _hbm_call(body, name, stacked, out_shapes, n, 1)


def _chip_exchange(halves, *, name):
    n = len(halves)

    def body(*refs):
        ins, outs = refs[:n], refs[n:2 * n]
        send_sems, recv_sems, local_sems = refs[2 * n:]
        x, y, c, chips = _place()
        me = 2 * x + y
        own = [pltpu.make_async_copy(ins[s].at[me], outs[s].at[me], local_sems.at[s]) for s in range(n)]
        for cp in own:
            cp.start()
        sent = []
        for s in range(n):
            for j, (px, py) in enumerate(chips):
                cp = _remote(ins[s].at[2 * px + py], outs[s].at[me], send_sems, recv_sems, s * 3 + j, (px, py, c))
                cp.start()
                sent.append(cp)
        for s in range(n):
            for j, (px, py) in enumerate(chips):
                slab = outs[s].at[2 * px + py]
                _remote(slab, slab, send_sems, recv_sems, s * 3 + j, (px, py, c)).wait_recv()
        for cp in sent:
            cp.wait_send()
        for cp in own:
            cp.wait()

    out_shapes = [jax.ShapeDtypeStruct(s.shape, s.dtype) for s in halves]
    return _hbm_call(body, name, halves, out_shapes, 3 * n, n)


def _pair_concat(halves, *, name):
    n = len(halves)

    def body(*refs):
        ins, outs = refs[:n], refs[n:2 * n]
        send_sems, recv_sems, local_sems = refs[2 * n:]
        x, y, c, _chips = _place()
        own, sent = [], []
        for s in range(n):
            h = ins[s].shape[0]
            cp = pltpu.make_async_copy(ins[s], outs[s].at[pl.ds(c * h, h)], local_sems.at[s])
            cp.start()
            own.append(cp)
            cp = _remote(ins[s], outs[s].at[pl.ds(c * h, h)], send_sems, recv_sems, s, (x, y, 1 - c))
            cp.start()
            sent.append(cp)
        for s in range(n):
            h = ins[s].shape[0]
            slab = outs[s].at[pl.ds((1 - c) * h, h)]
            _remote(slab, slab, send_sems, recv_sems, s, (x, y, 1 - c)).wait_recv()
        for cp in sent:
            cp.wait_send()
        for cp in own:
            cp.wait()

    out_shapes = [jax.ShapeDtypeStruct((2 * s.shape[0], s.shape[1]), s.dtype) for s in halves]
    return _hbm_call(body, name, halves, out_shapes, n, n)


def _pack_rows(parts, row_multiple):
    flat = jnp.concatenate([p.reshape(-1) for p in parts])
    quantum = row_multiple * PACK_COLS
    pad = (-flat.shape[0]) % quantum
    flat = jnp.pad(flat, (0, pad))
    return flat.reshape(-1, PACK_COLS)


def _unpack(flat, shapes):
    out, pos = [], 0
    for shp in shapes:
        size = math.prod(shp)
        out.append(flat[pos:pos + size].reshape(shp))
        pos += size
    return out


def _to_chunks_cols(full):
    r, c4 = full.shape
    return full.reshape(r, N_CHIPS, c4 // N_CHIPS).transpose(1, 0, 2)


def _from_chunks_cols(stacked):
    nch, r, c = stacked.shape
    return stacked.transpose(1, 0, 2).reshape(r, nch * c)


def _class_major(a, bl, t, dil):
    w = a.shape[-1]
    if dil == 1:
        return a.reshape(bl, 1, t, w)
    return a.reshape(bl, t // dil, dil, w).transpose(0, 2, 1, 3)


def _natural(a):
    bl, dil, ln, w = a.shape
    if dil == 1:
        return a.reshape(bl * ln, w)
    return a.transpose(0, 2, 1, 3).reshape(bl * ln * dil, w)


def _train_step(x, positions, a_pre_norm, a_w_in, a_w_out, a_post_norm, kv_norm, kv_w_down, kv_latent_norm,
                kv_w_up, b_pre_norm, b_w_in, b_q_norm, b_w_q_up, b_w_out, b_post_norm, loss_target, moments):
    bl, t, d = x.shape
    n = bl * t
    qb = t // A_DILATIONS[-1]
    tq = _tile(t, 256)
    dq4 = d // N_CHIPS

    w_in_a_s = a_w_in[0].astype(BF16)
    outs_s = jnp.concatenate([a_w_out[0], b_w_out[0]], axis=0).astype(BF16)
    small_shapes = [kv_w_down.shape, kv_w_up.shape, b_w_in[0].shape, b_w_q_up[0].shape]
    small_s = _pack_rows([kv_w_down, kv_w_up, b_w_in[0], b_w_q_up[0]], 32).astype(BF16)
    gains_s = jnp.pad(jnp.concatenate([a_pre_norm[0], a_post_norm[0]]), (0, 16 * LANES - 2 * dq4)).reshape(16, LANES)
    g_in_a, g_outs, g_small, g_gains = _all_gather_chips([w_in_a_s, outs_s, small_s, gains_s], name="gather_weights")

    w_in_a = _from_chunks_cols(g_in_a)
    w_out_a = g_outs[:, :A_WIDTH // N_CHIPS].reshape(A_WIDTH, d)
    w_out_b = g_outs[:, A_WIDTH // N_CHIPS:].reshape(B_WIDTH, d)
    sm = [_unpack(g_small[k].reshape(-1), small_shapes) for k in range(N_CHIPS)]
    w_down = jnp.concatenate([sm[k][0] for k in range(N_CHIPS)], axis=0)
    w_up = jnp.concatenate([sm[k][1] for k in range(N_CHIPS)], axis=1)
    w_in_b = jnp.concatenate([sm[k][2] for k in range(N_CHIPS)], axis=1)
    w_q_up = jnp.concatenate([sm[k][3] for k in range(N_CHIPS)], axis=1)
    gflat = g_gains.reshape(N_CHIPS, -1)
    g_a_pre = gflat[:, :dq4].reshape(1, d)
    g_a_post = gflat[:, dq4:2 * dq4].reshape(1, d)

    w_up_h = w_up.reshape(B_KV_LORA, B_HEADS, B_NOPE + B_VDIM)
    w_up_k = jnp.pad(w_up_h[:, :, :B_NOPE], ((0, 0), (0, 0), (0, LANES - B_NOPE))).reshape(B_KV_LORA, B_HEADS * LANES)
    w_up_v = w_up_h[:, :, B_NOPE:].reshape(B_KV_LORA, B_WIDTH)
    w_up_cat = jnp.concatenate([w_up_k, w_up_v], axis=1)
    w_q_up_p = jnp.pad(w_q_up.reshape(B_Q_LORA, B_HEADS, B_QK_DIM),
                       ((0, 0), (0, 0), (0, LANES - B_QK_DIM))).reshape(B_Q_LORA, B_HEADS * LANES)
    zeros_d = lambda c: jnp.zeros((d, c), BF16)
    w_down_p = jnp.concatenate([w_down[:, :B_KV_LORA], zeros_d(B_NOPE), w_down[:, B_KV_LORA:],
                                zeros_d(LANES - B_NOPE - B_ROPE)], axis=1)
    w_cq = w_in_b[:, :B_Q_LORA]
    w_z = w_in_b[:, B_Q_LORA:]

    tabs_a = _rope_tables(positions, A_ROPE_THETA, 0)
    tabs_b = _rope_tables(positions, B_ROPE_THETA, B_NOPE)

    h0 = x.reshape(n, d)
    hn_a = _rms_fwd(h0, g_a_pre, BF16, name="a_pre_norm")
    rope_a = (tabs_a, lambda j: jnp.logical_and(j < 3 * A_GROUPS, j % 3 != 2))
    proj_a = _matmul(hn_a, w_in_a, "nn", BF16, name="a_proj", rope=rope_a)
    z_blk_a = 3 * A_GROUPS
    o_groups, lse_groups, qkv_cm = [], [], []
    for g, dil in enumerate(A_DILATIONS):
        if dil == 1:
            src, cb0 = proj_a.reshape(bl, 1, t, A_IN_WIDTH), 3 * g
        else:
            src, cb0 = _class_major(proj_a[:, 3 * g * A_WIDTH:3 * (g + 1) * A_WIDTH], bl, t, dil), 0
        qkv_cm.append((src, cb0))
        o_g, lse_g = _attn_a_fwd(src, cb0, qb, name=f"attn_a_fwd_{g}")
        o_groups.append(_natural(o_g))
        lse_groups.append(_natural(lse_g))
    ypre_a, om_a, lse_a = _merge_gate_fwd(o_groups, lse_groups, proj_a, z_blk_a)
    y_a = _matmul(ypre_a, w_out_a, "nn", F32, name="a_out")
    h1 = _rms_fwd(y_a, g_a_post, F32, name="a_post_norm", add=h0)

    g_kvn = kv_norm.reshape(1, d)
    g_lat = kv_latent_norm.reshape(1, B_KV_LORA)
    hn_kv = _rms_fwd(h1, g_kvn, BF16, name="kv_norm")
    ckr = _matmul(hn_kv, w_down_p, "nn", F32, name="kv_down")
    c_kv, k_rope = _kv_latent_fwd(ckr, g_lat, tabs_b)
    kvup = _matmul(c_kv, w_up_cat, "nn", BF16, name="kv_up")
    hn_b = _rms_fwd(h1, b_pre_norm, BF16, name="b_pre_norm")
    z_b = _matmul(hn_b, w_z, "nn", BF16, name="b_proj_z")
    cq_raw = _matmul(hn_b, w_cq, "nn", F32, name="b_proj_q")
    c_q = _rms_fwd(cq_raw, b_q_norm, BF16, name="b_q_norm")
    q_cat = _matmul(c_q, w_q_up_p, "nn", BF16, name="b_q_up", rope=(tabs_b, lambda j: True))
    r3 = lambda a: a.reshape(bl, t, a.shape[-1])
    tabs_b3 = tuple(r3(tb) for tb in tabs_b)
    ypre_b, o_b, lse_b = _attn_b_fwd(r3(q_cat), r3(kvup), r3(k_rope), r3(z_b), tq)
    y_b = _matmul(ypre_b.reshape(n, B_WIDTH), w_out_b, "nn", F32, name="b_out")
    h2 = _rms_fwd(y_b, b_post_norm, F32, name="b_post_norm", add=h1)
    dh2, loss_part = _loss_fwd_bwd(h2, loss_target.reshape(n, d))

    dy_b, dg_b_post = _rms_bwd(y_b, b_post_norm, dh2, BF16, name="b_post_norm_bwd")
    dypre_b = _matmul(dy_b, w_out_b, "nt", F32, name="b_out_dx")
    dw_out_b = _matmul(ypre_b.reshape(n, B_WIDTH), dy_b, "tn", F32, name="b_out_dw", tm=1024, tk=512)
    do_b, dz_b = _gate_bwd(dypre_b, o_b.reshape(n, B_WIDTH), z_b, 0, name="b_gate_bwd", with_delta=False)
    dq_cat = _attn_b_dq(r3(q_cat), r3(kvup), r3(k_rope), r3(do_b), o_b, lse_b, tabs_b3, tq).reshape(n, -1)
    dk_cat, dv_b = _attn_b_dkv(r3(q_cat), r3(kvup), r3(k_rope), r3(do_b), o_b, lse_b, tq)
    dk_cat, dv_b = dk_cat.reshape(n, -1), dv_b.reshape(n, -1)
    dcq_n = _matmul(dq_cat, w_q_up_p, "nt", F32, name="b_q_up_dx")
    dw_q_up_p = _matmul(c_q, dq_cat, "tn", F32, name="b_q_up_dw", tm=1024, tk=512)
    dcq, dg_b_q = _rms_bwd(cq_raw, b_q_norm, dcq_n, BF16, name="b_q_norm_bwd")
    dhn_b = _matmul(dz_b, w_z, "nt", F32, name="b_proj_z_dx")
    dhn_b = _matmul(dcq, w_cq, "nt", F32, name="b_proj_q_dx", add=dhn_b)
    dw_z = _matmul(hn_b, dz_b, "tn", F32, name="b_proj_z_dw", tm=1024, tk=512)
    dw_cq = _matmul(hn_b, dcq, "tn", F32, name="b_proj_q_dw", tm=1024, tk=512)
    dh1, dg_b_pre = _rms_bwd(h1, b_pre_norm, dhn_b, F32, name="b_pre_norm_bwd", adds=(dh2,))
    dckv_n = _matmul(dk_cat, w_up_k, "nt", F32, name="kv_up_k_dx")
    dckv_n = _matmul(dv_b, w_up_v, "nt", F32, name="kv_up_v_dx", add=dckv_n)
    dw_up_k = _matmul(c_kv, dk_cat, "tn", F32, name="kv_up_k_dw", tm=1024, tk=512)
    dw_up_v = _matmul(c_kv, dv_b, "tn", F32, name="kv_up_v_dw", tm=1024, tk=512)
    dckr, dg_lat = _kv_latent_bwd(dckv_n, ckr, g_lat, dk_cat, tabs_b)
    dhn_kv = _matmul(dckr, w_down_p, "nt", F32, name="kv_down_dx")
    dw_down_p = _matmul(hn_kv, dckr, "tn", F32, name="kv_down_dw", tm=1024, tk=512)
    dh1, dg_kvn = _rms_bwd(h1, g_kvn, dhn_kv, F32, name="kv_norm_bwd", adds=(dh1,))

    dy_a, dg_a_post = _rms_bwd(y_a, g_a_post, dh1, BF16, name="a_post_norm_bwd")
    dypre_a = _matmul(dy_a, w_out_a, "nt", F32, name="a_out_dx")
    dw_out_a = _matmul(ypre_a, dy_a, "tn", F32, name="a_out_dw", tm=1024, tk=512)
    do_a, dz_a, delta_a = _gate_bwd(dypre_a, om_a, proj_a, z_blk_a, name="a_gate_bwd", with_delta=True)
    dw_in_a = jnp.zeros((d, A_IN_WIDTH), F32)
    dhn_a = None
    for g, dil in enumerate(A_DILATIONS):
        src, cb0 = qkv_cm[g]
        cm = lambda a: _class_major(a, bl, t, dil)
        dqkv = _attn_a_bwd(src, cb0, cm(do_a), cm(lse_a), cm(delta_a), tuple(cm(tb) for tb in tabs_a), qb,
                           name=f"attn_a_bwd_{g}")
        dqkv = _natural(dqkv)
        dhn_a = _matmul(dqkv, w_in_a, "nt", F32, name=f"a_proj_dx_{g}", add=dhn_a, b_koff=3 * g * (A_WIDTH // _tile(3 * A_WIDTH, 1024)))
        dw_in_a = _matmul(hn_a, dqkv, "tn", F32, name=f"a_proj_dw_{g}", tm=1024, tk=512,
                          out_into=dw_in_a, out_joff=3 * g * (A_WIDTH // _tile(3 * A_WIDTH, 1024)))
    dhn_a = _matmul(dz_a, w_in_a, "nt", F32, name="a_proj_dx_z", add=dhn_a, b_koff=z_blk_a)
    dw_in_a = _matmul(hn_a, dz_a, "tn", F32, name="a_proj_dw_z", tm=1024, tk=512, out_into=dw_in_a, out_joff=z_blk_a)
    grad_x, dg_a_pre = _rms_bwd(h0, g_a_pre, dhn_a, F32, name="a_pre_norm_bwd", adds=(dh1,))

    dw_up = jnp.concatenate([dw_up_k.reshape(B_KV_LORA, B_HEADS, LANES)[:, :, :B_NOPE],
                             dw_up_v.reshape(B_KV_LORA, B_HEADS, B_VDIM)], axis=2).reshape(B_KV_LORA, -1)
    dw_q_up = dw_q_up_p.reshape(B_Q_LORA, B_HEADS, LANES)[:, :, :B_QK_DIM].reshape(B_Q_LORA, -1)
    dw_down = jnp.concatenate([dw_down_p[:, :B_KV_LORA], dw_down_p[:, B_KV_LORA + B_NOPE:B_KV_LORA + B_NOPE + B_ROPE]], axis=1)
    dw_in_b = jnp.concatenate([dw_cq, dw_z], axis=1)
    vec_rep = [dg_kvn.reshape(-1), dg_lat.reshape(-1), dg_b_pre.reshape(-1), dg_b_q.reshape(-1), dg_b_post.reshape(-1)]
    vec_shapes = [(dq4,), (dq4,)] + [v.shape for v in vec_rep]
    r_big = _to_chunks_cols(dw_in_a)
    r_outs = jnp.concatenate([dw_out_a.reshape(N_CHIPS, A_WIDTH // N_CHIPS, d),
                              dw_out_b.reshape(N_CHIPS, B_WIDTH // N_CHIPS, d)], axis=1)
    down_c = dw_down.reshape(N_CHIPS, dq4, -1)
    up_c = _to_chunks_cols(dw_up)
    inb_c = _to_chunks_cols(dw_in_b)
    qup_c = _to_chunks_cols(dw_q_up)
    small_chunks = []
    for k in range(N_CHIPS):
        vecs = [dg_a_pre.reshape(-1)[k * dq4:(k + 1) * dq4], dg_a_post.reshape(-1)[k * dq4:(k + 1) * dq4]] + vec_rep
        small_chunks.append(_pack_rows([down_c[k], up_c[k], inb_c[k], qup_c[k]] + vecs, 16))
    r_small = jnp.stack(small_chunks)

    core = lax.axis_index("c").astype(jnp.int32).reshape(1)
    stacked = [r_big, r_outs, r_small]
    recv = _pair_send_other_half(stacked, name="reduce_pair_send")
    halves = [_add_my_half(s, p, core, name=f"reduce_pair_add_{i}") for i, (s, p) in enumerate(zip(stacked, recv))]
    parts = _chip_exchange(halves, name="reduce_chip_exchange")
    sums = [_sum_chips(p, name=f"reduce_chip_sum_{i}") for i, p in enumerate(parts)]
    g_big, g_outs_r, g_small_r = _pair_concat(sums, name="reduce_pair_concat")

    grads = {}
    grads["a_w_in"] = g_big
    grads["a_w_out"] = g_outs_r[:A_WIDTH // N_CHIPS]
    grads["b_w_out"] = g_outs_r[A_WIDTH // N_CHIPS:]
    small_out_shapes = [down_c.shape[1:], up_c.shape[1:], inb_c.shape[1:], qup_c.shape[1:]] + vec_shapes
    (grads["kv_w_down"], grads["kv_w_up"], grads["b_w_in"], grads["b_w_q_up"], grads["a_pre_norm"],
     grads["a_post_norm"], grads["kv_norm"], grads["kv_latent_norm"], grads["b_pre_norm"], grads["b_q_norm"],
     grads["b_post_norm"]) = _unpack(g_small_r.reshape(-1), small_out_shapes)

    weights = dict(a_pre_norm=a_pre_norm, a_w_in=a_w_in, a_w_out=a_w_out, a_post_norm=a_post_norm, kv_norm=kv_norm,
                   kv_w_down=kv_w_down, kv_latent_norm=kv_latent_norm, kv_w_up=kv_w_up, b_pre_norm=b_pre_norm,
                   b_w_in=b_w_in, b_q_norm=b_q_norm, b_w_q_up=b_w_q_up, b_w_out=b_w_out, b_post_norm=b_post_norm)
    names = list(weights)
    out_g, out_d, out_m, out_v = [], [], [], []
    for i, nm in enumerate(names):
        w = weights[nm]
        two_d = (1, w.shape[0]) if w.ndim == 1 else (w.shape[-2], w.shape[-1])
        gw = grads[nm].reshape(two_d)
        dlt, new_m, new_v = _adamw(w.reshape(two_d), gw, moments[i].reshape(two_d),
                                   moments[len(names) + i].reshape(two_d), name=f"adamw_{nm}")
        out_g.append(gw.reshape(w.shape))
        out_d.append(dlt.reshape(w.shape))
        out_m.append(new_m.reshape(w.shape))
        out_v.append(new_v.reshape(w.shape))
    loss = lax.psum(loss_part[0, 0], ("x", "y", "c"))
    return (loss, grad_x.reshape(bl, t, d), *out_g, *out_d, *out_m, *out_v)


def kernel(x, positions, a_pre_norm, a_w_in, a_w_out, a_post_norm, kv_norm, kv_w_down, kv_latent_norm, kv_w_up, b_pre_norm, b_w_in, b_q_norm, b_w_q_up, b_w_out, b_post_norm, loss_target, m_a_pre_norm, m_a_w_in, m_a_w_out, m_a_post_norm, m_kv_norm, m_kv_w_down, m_kv_latent_norm, m_kv_w_up, m_b_pre_norm, m_b_w_in, m_b_q_norm, m_b_w_q_up, m_b_w_out, m_b_post_norm, v_a_pre_norm, v_a_w_in, v_a_w_out, v_a_post_norm, v_kv_norm, v_kv_w_down, v_kv_latent_norm, v_kv_w_up, v_b_pre_norm, v_b_w_in, v_b_q_norm, v_b_w_q_up, v_b_w_out, v_b_post_norm):
    moments = (m_a_pre_norm, m_a_w_in, m_a_w_out, m_a_post_norm, m_kv_norm, m_kv_w_down, m_kv_latent_norm, m_kv_w_up,
               m_b_pre_norm, m_b_w_in, m_b_q_norm, m_b_w_q_up, m_b_w_out, m_b_post_norm,
               v_a_pre_norm, v_a_w_in, v_a_w_out, v_a_post_norm, v_kv_norm, v_kv_w_down, v_kv_latent_norm, v_kv_w_up,
               v_b_pre_norm, v_b_w_in, v_b_q_norm, v_b_w_q_up, v_b_w_out, v_b_post_norm)
    return _train_step(x, positions, a_pre_norm, a_w_in, a_w_out, a_post_norm, kv_norm, kv_w_down, kv_latent_norm,
                       kv_w_up, b_pre_norm, b_w_in, b_q_norm, b_w_q_up, b_w_out, b_post_norm, loss_target, moments)
```

```python
import functools
import math

import jax
import jax.numpy as jnp
from jax import lax
from jax.experimental import pallas as pl
from jax.experimental.pallas import tpu as pltpu

F32 = jnp.float32
BF16 = jnp.bfloat16
MESH = pl.DeviceIdType.MESH

NORM_EPS = 1e-6
NEG = -1e30
LANES = 128
VMEM_LIMIT = 56 * 1024 * 1024

A_GROUPS = 3
A_DILATIONS = (1, 4, 16)
A_HEADS = 8
A_HEAD_DIM = 128
A_WIDTH = A_HEADS * A_HEAD_DIM
A_ROT_HALF = A_HEAD_DIM // 8
A_ROPE_THETA = 500000.0
A_IN_WIDTH = A_GROUPS * 3 * A_WIDTH + A_WIDTH

B_HEADS = 16
B_NOPE = 64
B_ROPE = 32
B_QK_DIM = B_NOPE + B_ROPE
B_VDIM = 64
B_WIDTH = B_HEADS * B_VDIM
B_Q_LORA = 384
B_KV_LORA = 256
B_ROPE_THETA = 10000.0

ADAM_LR = 0.001
ADAM_B1 = 0.9
ADAM_B2 = 0.999
ADAM_EPS = 1e-08
ADAM_WD = 0.01
ADAM_STEP = 10

N_CHIPS = 4
PACK_COLS = 512


def _params(sem=None):
    return pltpu.CompilerParams(dimension_semantics=sem, vmem_limit_bytes=VMEM_LIMIT)


def _tile(n, want):
    t = min(n, want)
    assert n % t == 0, (n, want)
    return t


def _row_tile(n, want):
    for t in range(min(n, want), 0, -1):
        if n % t == 0 and (t % 8 == 0 or t == n):
            return t
    return n


def _rope_tables(positions, theta, lane0):
    half = 16
    inv_freq = 1.0 / (theta ** (jnp.arange(half, dtype=F32) * (2.0 / (2 * half))))
    ang = positions.astype(F32).reshape(-1)[:, None] * inv_freq
    cos, sin = jnp.cos(ang), jnp.sin(ang)
    n = ang.shape[0]
    pre = jnp.zeros((n, lane0), F32)
    post = jnp.zeros((n, LANES - lane0 - 2 * half), F32)
    z16 = jnp.zeros((n, half), F32)
    c = jnp.concatenate([pre + 1.0, cos, cos, post + 1.0], axis=1)
    sa = jnp.concatenate([pre, -sin, z16, post], axis=1)
    sb = jnp.concatenate([pre, z16, sin, post], axis=1)
    return c, sa, sb


def _rope_apply(x, c, sa, sb, sign):
    k = x.shape[1] // LANES
    if k > 1:
        c, sa, sb = (jnp.concatenate([t] * k, axis=1) for t in (c, sa, sb))
    w = x.shape[1]
    up = pltpu.roll(x, w - 16, 1)
    dn = pltpu.roll(x, 16, 1)
    if sign > 0:
        return x * c + up * sa + dn * sb
    return x * c - up * sa - dn * sb


def _matmul(a, b, mode, out_dtype, *, name, tm=512, tn=1024, tk=1024, add=None, rope=None,
            b_koff=0, out_into=None, out_joff=0):
    if mode == "nn":
        m, k = a.shape
        n = b.shape[1]
    elif mode == "nt":
        m, k = a.shape
        n = b.shape[0]
    else:
        k, m = a.shape
        n = b.shape[1]
    tm, tn, tk = _tile(m, tm), _tile(n, tn), _tile(k, tk)
    nk = k // tk
    if mode == "nn":
        a_spec = pl.BlockSpec((tm, tk), lambda j, i, kk: (i, kk))
        b_spec = pl.BlockSpec((tk, tn), lambda j, i, kk: (kk, j))
        dims = (((1,), (0,)), ((), ()))
    elif mode == "nt":
        a_spec = pl.BlockSpec((tm, tk), lambda j, i, kk: (i, kk))
        b_spec = pl.BlockSpec((tn, tk), lambda j, i, kk: (j, kk + b_koff))
        dims = (((1,), (1,)), ((), ()))
    else:
        a_spec = pl.BlockSpec((tk, tm), lambda j, i, kk: (kk, i))
        b_spec = pl.BlockSpec((tk, tn), lambda j, i, kk: (kk, j))
        dims = (((0,), (0,)), ((), ()))
    operands = [a, b]
    in_specs = [a_spec, b_spec]
    if add is not None:
        operands.append(add)
        in_specs.append(pl.BlockSpec((tm, tn), lambda j, i, kk: (i, j)))
    if rope is not None:
        tables, rope_pred = rope
        for t in tables:
            operands.append(t)
            in_specs.append(pl.BlockSpec((tm, LANES), lambda j, i, kk: (i, 0)))
    aliases = {}
    if out_into is not None:
        aliases = {len(operands): 0}
        operands.append(out_into)
        in_specs.append(pl.BlockSpec(memory_space=pl.ANY))
        out_shape = jax.ShapeDtypeStruct(out_into.shape, out_into.dtype)
        out_dtype = out_into.dtype
    else:
        out_shape = jax.ShapeDtypeStruct((m, n), out_dtype)
    out_spec = pl.BlockSpec((tm, tn), lambda j, i, kk: (i, j + out_joff))

    def body(*refs):
        a_ref, b_ref = refs[0], refs[1]
        pos = 2
        add_ref = None
        if add is not None:
            add_ref = refs[pos]
            pos += 1
        tab_refs = None
        if rope is not None:
            tab_refs = refs[pos:pos + 3]
            pos += 3
        if out_into is not None:
            pos += 1
        o_ref = refs[pos]
        acc_ref = refs[pos + 1] if nk > 1 else None

        def finish(res):
            if add_ref is not None:
                res = res + add_ref[...].astype(F32)
            if tab_refs is None:
                o_ref[...] = res.astype(o_ref.dtype)
                return
            flag = rope_pred(pl.program_id(0))
            roped = _rope_apply(res, tab_refs[0][...], tab_refs[1][...], tab_refs[2][...], 1)
            if flag is True:
                o_ref[...] = roped.astype(o_ref.dtype)
                return

            @pl.when(flag)
            def _():
                o_ref[...] = roped.astype(o_ref.dtype)

            @pl.when(jnp.logical_not(flag))
            def _():
                o_ref[...] = res.astype(o_ref.dtype)

        part = lax.dot_general(a_ref[...].astype(BF16), b_ref[...].astype(BF16), dims,
                               preferred_element_type=F32)
        if nk == 1:
            finish(part)
            return
        kk = pl.program_id(2)

        @pl.when(kk == 0)
        def _():
            acc_ref[...] = part

        @pl.when(kk > 0)
        def _():
            acc_ref[...] += part

        @pl.when(kk == nk - 1)
        def _():
            finish(acc_ref[...])

    return pl.pallas_call(
        body, name=name, grid=(n // tn, m // tm, nk), in_specs=in_specs, out_specs=out_spec,
        out_shape=out_shape, input_output_aliases=aliases,
        scratch_shapes=[pltpu.VMEM((tm, tn), F32)] if nk > 1 else [],
        compiler_params=_params(("parallel", "parallel", "arbitrary")),
    )(*operands)


def _rms_fwd(x, g, out_dtype, *, name, add=None, tr=512):
    n, d = x.shape
    tr = _tile(n, tr)
    row = pl.BlockSpec((tr, d), lambda i: (i, 0))
    vec = pl.BlockSpec((1, d), lambda i: (0, 0))

    def body(*refs):
        x_ref, g_ref = refs[0], refs[1]
        o_ref = refs[-1]
        xv = x_ref[...].astype(F32)
        r = lax.rsqrt(jnp.mean(xv * xv, axis=-1, keepdims=True) + NORM_EPS)
        y = xv * r * g_ref[...]
        if add is not None:
            y = refs[2][...] + y
        o_ref[...] = y.astype(o_ref.dtype)

    ops = [x, g] + ([add] if add is not None else [])
    specs = [row, vec] + ([row] if add is not None else [])
    return pl.pallas_call(
        body, name=name, grid=(n // tr,), in_specs=specs, out_specs=row,
        out_shape=jax.ShapeDtypeStruct((n, d), out_dtype), compiler_params=_params(("parallel",)),
    )(*ops)


def _rms_bwd(x, g, dy, out_dtype, *, name, adds=(), tr=512):
    n, d = x.shape
    tr = _tile(n, tr)
    steps = n // tr
    row = pl.BlockSpec((tr, d), lambda i: (i, 0))
    vec = pl.BlockSpec((1, d), lambda i: (0, 0))
    na = len(adds)

    def body(*refs):
        x_ref, g_ref, dy_ref = refs[:3]
        add_refs = refs[3:3 + na]
        dx_ref, dg_ref, acc_ref = refs[3 + na:]
        i = pl.program_id(0)
        xv = x_ref[...].astype(F32)
        r = lax.rsqrt(jnp.mean(xv * xv, axis=-1, keepdims=True) + NORM_EPS)
        xh = xv * r
        dyv = dy_ref[...].astype(F32)
        part = (dyv * xh).reshape(tr // 8, 8, d).sum(axis=0)

        @pl.when(i == 0)
        def _():
            acc_ref[...] = part

        @pl.when(i > 0)
        def _():
            acc_ref[...] += part

        t = dyv * g_ref[...]
        dx = r * (t - xh * jnp.mean(t * xh, axis=-1, keepdims=True))
        for a_ref in add_refs:
            dx = dx + a_ref[...].astype(F32)
        dx_ref[...] = dx.astype(dx_ref.dtype)

        @pl.when(i == steps - 1)
        def _():
            dg_ref[...] = jnp.sum(acc_ref[...], axis=0, keepdims=True)

    return pl.pallas_call(
        body, name=name, grid=(steps,), in_specs=[row, vec, row] + [row] * na,
        out_specs=(row, vec),
        out_shape=(jax.ShapeDtypeStruct((n, d), out_dtype), jax.ShapeDtypeStruct((1, d), F32)),
        scratch_shapes=[pltpu.VMEM((8, d), F32)], compiler_params=_params(("arbitrary",)),
    )(x, g, dy, *adds)


def _kv_latent_fwd(ckr, g_lat, tabs, *, tr=512):
    n = ckr.shape[0]
    tr = _tile(n, tr)
    lat = B_KV_LORA

    def body(c_ref, k_ref, g_ref, tc, tsa, tsb, ckv_ref, kr_ref):
        xv = c_ref[...]
        r = lax.rsqrt(jnp.mean(xv * xv, axis=-1, keepdims=True) + NORM_EPS)
        ckv_ref[...] = (xv * r * g_ref[...]).astype(BF16)
        kr_ref[...] = _rope_apply(k_ref[...], tc[...], tsa[...], tsb[...], 1).astype(BF16)

    tab = pl.BlockSpec((tr, LANES), lambda i: (i, 0))
    return pl.pallas_call(
        body, name="kv_latent_fwd", grid=(n // tr,),
        in_specs=[pl.BlockSpec((tr, lat), lambda i: (i, 0)),
                  pl.BlockSpec((tr, LANES), lambda i: (i, lat // LANES)),
                  pl.BlockSpec((1, lat), lambda i: (0, 0)), tab, tab, tab],
        out_specs=(pl.BlockSpec((tr, lat), lambda i: (i, 0)), tab),
        out_shape=(jax.ShapeDtypeStruct((n, lat), BF16), jax.ShapeDtypeStruct((n, LANES), BF16)),
        compiler_params=_params(("parallel",)),
    )(ckr, ckr, g_lat, *tabs)


def _kv_latent_bwd(dckv, ckr, g_lat, dk_cat, tabs, *, tr=512):
    n = ckr.shape[0]
    tr = _tile(n, tr)
    steps = n // tr
    lat = B_KV_LORA
    wk = dk_cat.shape[1]

    def body(d_ref, c_ref, g_ref, dk_ref, tc, tsa, tsb, o_ref, dg_ref, acc_ref):
        i = pl.program_id(0)
        xv = c_ref[...]
        r = lax.rsqrt(jnp.mean(xv * xv, axis=-1, keepdims=True) + NORM_EPS)
        xh = xv * r
        dyv = d_ref[...]
        part = (dyv * xh).reshape(tr // 8, 8, lat).sum(axis=0)

        @pl.when(i == 0)
        def _():
            acc_ref[...] = part

        @pl.when(i > 0)
        def _():
            acc_ref[...] += part

        t = dyv * g_ref[...]
        dx = r * (t - xh * jnp.mean(t * xh, axis=-1, keepdims=True))
        o_ref[:, 0:lat] = dx.astype(o_ref.dtype)
        dkr = dk_ref[:, 0:LANES].astype(F32)
        for h in range(1, wk // LANES):
            dkr = dkr + dk_ref[:, h * LANES:(h + 1) * LANES].astype(F32)
        o_ref[:, lat:lat + LANES] = _rope_apply(dkr, tc[...], tsa[...], tsb[...], -1).astype(o_ref.dtype)

        @pl.when(i == steps - 1)
        def _():
            dg_ref[...] = jnp.sum(acc_ref[...], axis=0, keepdims=True)

    tab = pl.BlockSpec((tr, LANES), lambda i: (i, 0))
    return pl.pallas_call(
        body, name="kv_latent_bwd", grid=(steps,),
        in_specs=[pl.BlockSpec((tr, lat), lambda i: (i, 0)), pl.BlockSpec((tr, lat), lambda i: (i, 0)),
                  pl.BlockSpec((1, lat), lambda i: (0, 0)), pl.BlockSpec((tr, wk), lambda i: (i, 0)),
                  tab, tab, tab],
        out_specs=(pl.BlockSpec((tr, lat + LANES), lambda i: (i, 0)), pl.BlockSpec((1, lat), lambda i: (0, 0))),
        out_shape=(jax.ShapeDtypeStruct((n, lat + LANES), BF16), jax.ShapeDtypeStruct((1, lat), F32)),
        scratch_shapes=[pltpu.VMEM((8, lat), F32)], compiler_params=_params(("arbitrary",)),
    )(dckv, ckr, g_lat, dk_cat, *tabs)


def _sigmoid(z):
    return 1.0 / (1.0 + jnp.exp(-z))


def _lane_place(cols, width):
    rows = cols[0].shape[0]
    lane = lax.broadcasted_iota(jnp.int32, (rows, width), 1)
    out = jnp.zeros((rows, width), F32)
    for h, col in enumerate(cols):
        out = jnp.where(lane == h, col, out)
    return out


def _merge_gate_fwd(outs, lses, proj, z_block, *, tr=256):
    n, w = outs[0].shape
    tr = _tile(n, tr)
    ng = len(outs)

    def body(*refs):
        o_refs = refs[:ng]
        l_refs = refs[ng:2 * ng]
        z_ref = refs[2 * ng]
        y_ref, om_ref, lse_ref = refs[2 * ng + 1:]
        ls = [r[...] for r in l_refs]
        mx = ls[0]
        for l in ls[1:]:
            mx = jnp.maximum(mx, l)
        ssum = jnp.exp(ls[0] - mx)
        for l in ls[1:]:
            ssum = ssum + jnp.exp(l - mx)
        tot = mx + jnp.log(ssum)
        lse_ref[...] = tot
        ws = [jnp.exp(l - tot) for l in ls]
        for h in range(A_HEADS):
            sl = slice(h * A_HEAD_DIM, (h + 1) * A_HEAD_DIM)
            o = ws[0][:, h:h + 1] * o_refs[0][:, sl]
            for gi in range(1, ng):
                o = o + ws[gi][:, h:h + 1] * o_refs[gi][:, sl]
            z = z_ref[:, sl].astype(F32)
            om_ref[:, sl] = o.astype(BF16)
            y_ref[:, sl] = (o * (z * _sigmoid(z))).astype(BF16)

    row = pl.BlockSpec((tr, w), lambda i: (i, 0))
    lrow = pl.BlockSpec((tr, A_HEADS), lambda i: (i, 0))
    return pl.pallas_call(
        body, name="merge_gate_fwd", grid=(n // tr,),
        in_specs=[row] * ng + [lrow] * ng + [pl.BlockSpec((tr, w), lambda i: (i, z_block))],
        out_specs=(row, row, lrow),
        out_shape=(jax.ShapeDtypeStruct((n, w), BF16), jax.ShapeDtypeStruct((n, w), BF16),
                   jax.ShapeDtypeStruct((n, A_HEADS), F32)),
        compiler_params=_params(("parallel",)),
    )(*outs, *lses, proj)


def _gate_bwd(dy, o, z_arr, z_block, *, name, with_delta, tr=256):
    n, w = dy.shape
    tr = _tile(n, tr)

    def body(*refs):
        dy_ref, o_ref, z_ref, do_ref, dz_ref = refs[:5]
        dyv = dy_ref[...].astype(F32)
        ov = o_ref[...].astype(F32)
        z = z_ref[...].astype(F32)
        sig = _sigmoid(z)
        do = dyv * (z * sig)
        do_ref[...] = do.astype(BF16)
        dz_ref[...] = (dyv * ov * (sig * (1.0 + z * (1.0 - sig)))).astype(BF16)
        if with_delta:
            prod = do * ov
            cols = [jnp.sum(prod[:, h * A_HEAD_DIM:(h + 1) * A_HEAD_DIM], axis=-1, keepdims=True)
                    for h in range(A_HEADS)]
            refs[5][...] = _lane_place(cols, A_HEADS)

    row = pl.BlockSpec((tr, w), lambda i: (i, 0))
    out_specs = [row, row]
    out_shape = [jax.ShapeDtypeStruct((n, w), BF16), jax.ShapeDtypeStruct((n, w), BF16)]
    if with_delta:
        out_specs.append(pl.BlockSpec((tr, A_HEADS), lambda i: (i, 0)))
        out_shape.append(jax.ShapeDtypeStruct((n, A_HEADS), F32))
    return pl.pallas_call(
        body, name=name, grid=(n // tr,),
        in_specs=[row, row, pl.BlockSpec((tr, w), lambda i: (i, z_block))],
        out_specs=tuple(out_specs), out_shape=tuple(out_shape), compiler_params=_params(("parallel",)),
    )(dy, o, z_arr)


def _loss_fwd_bwd(h, target, *, tr=512):
    n, d = h.shape
    tr = _tile(n, tr)
    steps = n // tr

    def body(h_ref, t_ref, dh_ref, loss_ref, acc_ref):
        i = pl.program_id(0)
        e = h_ref[...] - t_ref[...]
        dh_ref[...] = e / d
        part = (e * e).reshape(tr // 8, 8, d).sum(axis=0)

        @pl.when(i == 0)
        def _():
            acc_ref[...] = part

        @pl.when(i > 0)
        def _():
            acc_ref[...] += part

        @pl.when(i == steps - 1)
        def _():
            s = jnp.sum(jnp.sum(acc_ref[...], axis=-1, keepdims=True), axis=0, keepdims=True)
            loss_ref[...] = 0.5 * s / d

    row = pl.BlockSpec((tr, d), lambda i: (i, 0))
    return pl.pallas_call(
        body, name="loss", grid=(steps,), in_specs=[row, row],
        out_specs=(row, pl.BlockSpec((1, 1), lambda i: (0, 0))),
        out_shape=(jax.ShapeDtypeStruct((n, d), F32), jax.ShapeDtypeStruct((1, 1), F32)),
        scratch_shapes=[pltpu.VMEM((8, d), F32)], compiler_params=_params(("arbitrary",)),
    )(h, target)


def _dot_nt(a, b):
    return lax.dot_general(a, b, (((1,), (1,)), ((), ())), preferred_element_type=F32)


def _dot_nn(a, b):
    return lax.dot_general(a, b, (((1,), (0,)), ((), ())), preferred_element_type=F32)


def _dot_tn(a, b):
    return lax.dot_general(a, b, (((0,), (0,)), ((), ())), preferred_element_type=F32)


def _attn_a_fwd(qkv, cb0, qb, *, name):
    bl, dil, ln, _ = qkv.shape
    nb = ln // qb
    scale = A_HEAD_DIM ** -0.5
    hw = A_WIDTH

    def body(q_ref, kc_ref, kp_ref, vc_ref, vp_ref, o_ref, lse_ref):
        i = pl.program_id(2)
        qi = lax.broadcasted_iota(jnp.int32, (qb, qb), 0)
        ki = lax.broadcasted_iota(jnp.int32, (qb, qb), 1)
        mask_c = ki <= qi
        mask_p = jnp.logical_and(ki >= qi, i >= 1)
        cols = []
        for h in range(A_HEADS):
            sl = slice(h * A_HEAD_DIM, (h + 1) * A_HEAD_DIM)
            q = q_ref[:, sl]
            s_c = jnp.where(mask_c, _dot_nt(q, kc_ref[:, sl]) * scale, NEG)
            m = jnp.max(s_c, axis=-1, keepdims=True)
            if nb > 1:
                s_p = jnp.where(mask_p, _dot_nt(q, kp_ref[:, sl]) * scale, NEG)
                m = jnp.maximum(m, jnp.max(s_p, axis=-1, keepdims=True))
            p_c = jnp.exp(s_c - m)
            l = jnp.sum(p_c, axis=-1, keepdims=True)
            acc = _dot_nn(p_c.astype(BF16), vc_ref[:, sl])
            if nb > 1:
                p_p = jnp.exp(s_p - m)
                l = l + jnp.sum(p_p, axis=-1, keepdims=True)
                acc = acc + _dot_nn(p_p.astype(BF16), vp_ref[:, sl])
            o_ref[:, sl] = acc / l
            cols.append(m + jnp.log(l))
        lse_ref[...] = _lane_place(cols, A_HEADS)

    def spec(off, prev):
        if prev:
            return pl.BlockSpec((None, None, qb, hw), lambda b, r, i: (b, r, jnp.maximum(i - 1, 0), cb0 + off))
        return pl.BlockSpec((None, None, qb, hw), lambda b, r, i: (b, r, i, cb0 + off))

    return pl.pallas_call(
        body, name=name, grid=(bl, dil, nb),
        in_specs=[spec(0, False), spec(1, False), spec(1, True), spec(2, False), spec(2, True)],
        out_specs=(pl.BlockSpec((None, None, qb, hw), lambda b, r, i: (b, r, i, 0)),
                   pl.BlockSpec((None, None, qb, A_HEADS), lambda b, r, i: (b, r, i, 0))),
        out_shape=(jax.ShapeDtypeStruct((bl, dil, ln, hw), F32),
                   jax.ShapeDtypeStruct((bl, dil, ln, A_HEADS), F32)),
        compiler_params=_params(("parallel", "parallel", "arbitrary")),
    )(qkv, qkv, qkv, qkv, qkv)


def _attn_a_bwd(qkv, cb0, do, lse, delta, tabs, qb, *, name):
    bl, dil, ln, _ = qkv.shape
    nb = ln // qb
    scale = A_HEAD_DIM ** -0.5
    hw = A_WIDTH

    def body(q_ref, qn_ref, kc_ref, kp_ref, vc_ref, vp_ref, do_ref, don_ref,
             lse_ref, lsen_ref, dl_ref, dln_ref, tc, tsa, tsb, o_ref):
        i = pl.program_id(2)
        qi = lax.broadcasted_iota(jnp.int32, (qb, qb), 0)
        ki = lax.broadcasted_iota(jnp.int32, (qb, qb), 1)
        mask_c = ki <= qi
        mask_p = jnp.logical_and(ki >= qi, i >= 1)
        mask_n = jnp.logical_and(ki >= qi, i + 1 < nb)
        c, sa, sb = tc[...], tsa[...], tsb[...]
        for h in range(A_HEADS):
            sl = slice(h * A_HEAD_DIM, (h + 1) * A_HEAD_DIM)
            q, kc, vc, dov = q_ref[:, sl], kc_ref[:, sl], vc_ref[:, sl], do_ref[:, sl]
            lse_h = lse_ref[:, h:h + 1]
            dl_h = dl_ref[:, h:h + 1]
            p = jnp.exp(jnp.where(mask_c, _dot_nt(q, kc) * scale, NEG) - lse_h)
            ds = (p * (_dot_nt(dov, vc) - dl_h) * scale).astype(BF16)
            dq = _dot_nn(ds, kc)
            dk = _dot_tn(ds, q)
            dv = _dot_tn(p.astype(BF16), dov)
            if nb > 1:
                kp, vp = kp_ref[:, sl], vp_ref[:, sl]
                p = jnp.exp(jnp.where(mask_p, _dot_nt(q, kp) * scale, NEG) - lse_h)
                ds = (p * (_dot_nt(dov, vp) - dl_h) * scale).astype(BF16)
                dq = dq + _dot_nn(ds, kp)
                qn, don = qn_ref[:, sl], don_ref[:, sl]
                p = jnp.exp(jnp.where(mask_n, _dot_nt(qn, kc) * scale, NEG) - lsen_ref[:, h:h + 1])
                ds = (p * (_dot_nt(don, vc) - dln_ref[:, h:h + 1]) * scale).astype(BF16)
                dk = dk + _dot_tn(ds, qn)
                dv = dv + _dot_tn(p.astype(BF16), don)
            o_ref[:, h * A_HEAD_DIM:(h + 1) * A_HEAD_DIM] = _rope_apply(dq, c, sa, sb, -1).astype(BF16)
            o_ref[:, hw + h * A_HEAD_DIM:hw + (h + 1) * A_HEAD_DIM] = _rope_apply(dk, c, sa, sb, -1).astype(BF16)
            o_ref[:, 2 * hw + h * A_HEAD_DIM:2 * hw + (h + 1) * A_HEAD_DIM] = dv.astype(BF16)

    def cur(w, col):
        return pl.BlockSpec((None, None, qb, w), lambda b, r, i: (b, r, i, col))

    def prev(w, col):
        return pl.BlockSpec((None, None, qb, w), lambda b, r, i: (b, r, jnp.maximum(i - 1, 0), col))

    def nxt(w, col):
        return pl.BlockSpec((None, None, qb, w), lambda b, r, i: (b, r, jnp.minimum(i + 1, nb - 1), col))

    return pl.pallas_call(
        body, name=name, grid=(bl, dil, nb),
        in_specs=[cur(hw, cb0), nxt(hw, cb0), cur(hw, cb0 + 1), prev(hw, cb0 + 1),
                  cur(hw, cb0 + 2), prev(hw, cb0 + 2), cur(hw, 0), nxt(hw, 0),
                  cur(A_HEADS, 0), nxt(A_HEADS, 0), cur(A_HEADS, 0), nxt(A_HEADS, 0),
                  cur(LANES, 0), cur(LANES, 0), cur(LANES, 0)],
        out_specs=cur(3 * hw, 0),
        out_shape=jax.ShapeDtypeStruct((bl, dil, ln, 3 * hw), BF16),
        compiler_params=_params(("parallel", "parallel", "arbitrary")),
    )(qkv, qkv, qkv, qkv, qkv, qkv, do, do, lse, lse, delta, delta, *tabs)


def _causal_mask(q0, k0, tq, tk):
    qi = lax.broadcasted_iota(jnp.int32, (tq, tk), 0) + q0
    ki = lax.broadcasted_iota(jnp.int32, (tq, tk), 1) + k0
    return ki <= qi


def _attn_b_fwd(q_cat, kvup, kr, z, tq):
    bl, t, _ = q_cat.shape
    nq = t // tq
    pairs = B_HEADS // 2
    scale = B_QK_DIM ** -0.5
    v_blk0 = (B_HEADS * LANES) // LANES

    def body(q_ref, k_ref, v_ref, kr_ref, z_ref, y_ref, o_ref, lse_ref):
        qi = pl.program_id(2)
        outs, lses = [], []
        for e in range(2):
            sl = slice(e * LANES, (e + 1) * LANES)
            q = q_ref[:, sl]

            def step(kb, carry, sl=sl, q=q):
                m, l, acc = carry
                k0 = pl.multiple_of(kb * tq, tq)
                k = k_ref[pl.ds(k0, tq), sl] + kr_ref[pl.ds(k0, tq), :]
                s = jnp.where(_causal_mask(qi * tq, k0, tq, tq), _dot_nt(q, k) * scale, NEG)
                m_new = jnp.maximum(m, jnp.max(s, axis=-1, keepdims=True))
                alpha = jnp.exp(m - m_new)
                p = jnp.exp(s - m_new)
                l = alpha * l + jnp.sum(p, axis=-1, keepdims=True)
                acc = alpha * acc + _dot_nn(p.astype(BF16), v_ref[pl.ds(k0, tq), :])
                return m_new, l, acc

            init = (jnp.full((tq, 1), NEG, F32), jnp.zeros((tq, 1), F32), jnp.zeros((tq, LANES), F32))
            m, l, acc = lax.fori_loop(0, qi + 1, step, init)
            outs.append(acc / l)
            lses.append(m + jnp.log(l))
        lane = lax.broadcasted_iota(jnp.int32, (tq, LANES), 1)
        first = lane < B_VDIM
        o = jnp.where(first, outs[0], outs[1])
        zv = z_ref[...].astype(F32)
        o_ref[...] = o.astype(BF16)
        y_ref[...] = (o * (zv * _sigmoid(zv))).astype(BF16)
        lse_ref[...] = jnp.where(first, lses[0], lses[1])

    blk = pl.BlockSpec((None, tq, LANES), lambda b, j, i: (b, i, j))
    return pl.pallas_call(
        body, name="attn_b_fwd", grid=(bl, pairs, nq),
        in_specs=[pl.BlockSpec((None, tq, 2 * LANES), lambda b, j, i: (b, i, j)),
                  pl.BlockSpec((None, t, 2 * LANES), lambda b, j, i: (b, 0, j)),
                  pl.BlockSpec((None, t, LANES), lambda b, j, i: (b, 0, v_blk0 + j)),
                  pl.BlockSpec((None, t, LANES), lambda b, j, i: (b, 0, 0)),
                  blk],
        out_specs=(blk, blk, blk),
        out_shape=(jax.ShapeDtypeStruct((bl, t, B_WIDTH), BF16), jax.ShapeDtypeStruct((bl, t, B_WIDTH), BF16),
                   jax.ShapeDtypeStruct((bl, t, B_WIDTH), F32)),
        compiler_params=_params(("parallel", "parallel", "arbitrary")),
    )(q_cat, kvup, kvup, kr, z)


def _head_terms(do, o, lse, e):
    rows = do.shape[0]
    lane = lax.broadcasted_iota(jnp.int32, (rows, LANES), 1)
    mine = (lane < B_VDIM) if e == 0 else (lane >= B_VDIM)
    prod = do.astype(F32) * o.astype(F32)
    dl = jnp.sum(jnp.where(mine, prod, 0.0), axis=-1, keepdims=True)
    do_e = jnp.where(mine, do, jnp.zeros_like(do))
    return do_e, dl, lse[:, e * B_VDIM:e * B_VDIM + 1]


def _attn_b_dq(q_cat, kvup, kr, do, o, lse, tabs, tq):
    bl, t, _ = q_cat.shape
    nq = t // tq
    pairs = B_HEADS // 2
    scale = B_QK_DIM ** -0.5
    v_blk0 = (B_HEADS * LANES) // LANES

    def body(q_ref, k_ref, v_ref, kr_ref, do_ref, o_ref, lse_ref, tc, tsa, tsb, dq_ref):
        qi = pl.program_id(2)
        dov, ov, lsev = do_ref[...], o_ref[...], lse_ref[...]
        for e in range(2):
            sl = slice(e * LANES, (e + 1) * LANES)
            q = q_ref[:, sl]
            do_e, dl, lse_e = _head_terms(dov, ov, lsev, e)

            def step(kb, dq, sl=sl, q=q, do_e=do_e, dl=dl, lse_e=lse_e):
                k0 = pl.multiple_of(kb * tq, tq)
                k = k_ref[pl.ds(k0, tq), sl] + kr_ref[pl.ds(k0, tq), :]
                s = jnp.where(_causal_mask(qi * tq, k0, tq, tq), _dot_nt(q, k) * scale, NEG)
                p = jnp.exp(s - lse_e)
                dp = _dot_nt(do_e, v_ref[pl.ds(k0, tq), :])
                ds = (p * (dp - dl) * scale).astype(BF16)
                return dq + _dot_nn(ds, k)

            dq = lax.fori_loop(0, qi + 1, step, jnp.zeros((tq, LANES), F32))
            dq_ref[:, sl] = _rope_apply(dq, tc[...], tsa[...], tsb[...], -1).astype(BF16)

    blk = pl.BlockSpec((None, tq, LANES), lambda b, j, i: (b, i, j))
    tab = pl.BlockSpec((None, tq, LANES), lambda b, j, i: (b, i, 0))
    qblk = pl.BlockSpec((None, tq, 2 * LANES), lambda b, j, i: (b, i, j))
    return pl.pallas_call(
        body, name="attn_b_dq", grid=(bl, pairs, nq),
        in_specs=[qblk,
                  pl.BlockSpec((None, t, 2 * LANES), lambda b, j, i: (b, 0, j)),
                  pl.BlockSpec((None, t, LANES), lambda b, j, i: (b, 0, v_blk0 + j)),
                  pl.BlockSpec((None, t, LANES), lambda b, j, i: (b, 0, 0)),
                  blk, blk, blk, tab, tab, tab],
        out_specs=qblk,
        out_shape=jax.ShapeDtypeStruct((bl, t, B_HEADS * LANES), BF16),
        compiler_params=_params(("parallel", "parallel", "arbitrary")),
    )(q_cat, kvup, kvup, kr, do, o, lse, *tabs)


def _attn_b_dkv(q_cat, kvup, kr, do, o, lse, tq):
    bl, t, _ = q_cat.shape
    nq = t // tq
    pairs = B_HEADS // 2
    scale = B_QK_DIM ** -0.5
    v_blk0 = (B_HEADS * LANES) // LANES

    def body(q_ref, k_ref, v_ref, kr_ref, do_ref, o_ref, lse_ref, dk_ref, dv_ref):
        kb = pl.program_id(2)
        v = v_ref[...]
        dv = jnp.zeros((tq, LANES), F32)
        for e in range(2):
            sl = slice(e * LANES, (e + 1) * LANES)
            k = k_ref[:, sl] + kr_ref[...]

            def step(qbi, carry, sl=sl, k=k, e=e):
                dk, dv = carry
                q0 = pl.multiple_of(qbi * tq, tq)
                rows = pl.ds(q0, tq)
                q = q_ref[rows, sl]
                do_e, dl, lse_e = _head_terms(do_ref[rows, :], o_ref[rows, :], lse_ref[rows, :], e)
                s = jnp.where(_causal_mask(q0, kb * tq, tq, tq), _dot_nt(q, k) * scale, NEG)
                p = jnp.exp(s - lse_e)
                dv = dv + _dot_tn(p.astype(BF16), do_e)
                ds = (p * (_dot_nt(do_e, v) - dl) * scale).astype(BF16)
                return dk + _dot_tn(ds, q), dv

            dk, dv = lax.fori_loop(kb, nq, step, (jnp.zeros((tq, LANES), F32), dv))
            dk_ref[:, sl] = dk.astype(BF16)
        dv_ref[...] = dv.astype(BF16)

    full = pl.BlockSpec((None, t, LANES), lambda b, j, i: (b, 0, j))
    kblk = pl.BlockSpec((None, tq, 2 * LANES), lambda b, j, i: (b, i, j))
    return pl.pallas_call(
        body, name="attn_b_dkv", grid=(bl, pairs, nq),
        in_specs=[pl.BlockSpec((None, t, 2 * LANES), lambda b, j, i: (b, 0, j)),
                  kblk,
                  pl.BlockSpec((None, tq, LANES), lambda b, j, i: (b, i, v_blk0 + j)),
                  pl.BlockSpec((None, tq, LANES), lambda b, j, i: (b, i, 0)),
                  full, full, full],
        out_specs=(kblk, pl.BlockSpec((None, tq, LANES), lambda b, j, i: (b, i, j))),
        out_shape=(jax.ShapeDtypeStruct((bl, t, B_HEADS * LANES), BF16),
                   jax.ShapeDtypeStruct((bl, t, B_WIDTH), BF16)),
        compiler_params=_params(("parallel", "parallel", "arbitrary")),
    )(q_cat, kvup, kvup, kr, do, o, lse)


def _col_to_row(col, rows):
    return jnp.transpose(jnp.broadcast_to(col, (rows, LANES)))[0:1, :]


def _mla_fwd(q_cat, kvup, kr, z, tq):
    bl, t, _ = q_cat.shape
    nq = t // tq
    pairs = B_HEADS // 2
    scale = B_QK_DIM ** -0.5
    v_blk0 = (B_HEADS * LANES) // LANES

    def body(q_ref, k_ref, v_ref, kr_ref, z_ref, y_ref, o_ref, lse_ref, lrow_ref):
        qi = pl.program_id(2)
        qs = [q_ref[:, e * LANES:(e + 1) * LANES] for e in range(2)]
        row = lax.broadcasted_iota(jnp.int32, (tq, tq), 0)
        col = lax.broadcasted_iota(jnp.int32, (tq, tq), 1)
        tri = col <= row

        def tile(kb, carry, masked):
            k0 = pl.multiple_of(kb * tq, tq)
            krv = kr_ref[pl.ds(k0, tq), :]
            v = v_ref[pl.ds(k0, tq), :]
            out = []
            for e in range(2):
                m, l, acc = carry[e]
                k = k_ref[pl.ds(k0, tq), e * LANES:(e + 1) * LANES] + krv
                s = _dot_nt(qs[e], k) * scale
                if masked:
                    s = jnp.where(tri, s, NEG)
                m_new = jnp.maximum(m, jnp.max(s, axis=-1, keepdims=True))
                alpha = jnp.exp(m - m_new)
                p = jnp.exp(s - m_new)
                l = alpha * l + jnp.sum(p, axis=-1, keepdims=True)
                acc = alpha * acc + _dot_nn(p.astype(BF16), v)
                out.append((m_new, l, acc))
            return tuple(out)

        one = (jnp.full((tq, 1), NEG, F32), jnp.zeros((tq, 1), F32), jnp.zeros((tq, LANES), F32))
        carry = lax.fori_loop(0, qi, lambda kb, c: tile(kb, c, False), (one, one))
        carry = tile(qi, carry, True)
        outs = [carry[e][2] / carry[e][1] for e in range(2)]
        lses = [carry[e][0] + jnp.log(carry[e][1]) for e in range(2)]
        lane = lax.broadcasted_iota(jnp.int32, (tq, LANES), 1)
        first = lane < B_VDIM
        o = jnp.where(first, outs[0], outs[1])
        zv = z_ref[...].astype(F32)
        o_ref[...] = o.astype(BF16)
        y_ref[...] = (o * (zv * _sigmoid(zv))).astype(BF16)
        lse_ref[...] = jnp.where(first, lses[0], lses[1])
        for e in range(2):
            lrow_ref[e:e + 1, :] = _col_to_row(lses[e], tq)

    blk = pl.BlockSpec((None, tq, LANES), lambda b, j, i: (b, i, j))
    return pl.pallas_call(
        body, name="mla_fwd", grid=(bl, pairs, nq),
        in_specs=[pl.BlockSpec((None, tq, 2 * LANES), lambda b, j, i: (b, i, j)),
                  pl.BlockSpec((None, t, 2 * LANES), lambda b, j, i: (b, 0, j)),
                  pl.BlockSpec((None, t, LANES), lambda b, j, i: (b, 0, v_blk0 + j)),
                  pl.BlockSpec((None, t, LANES), lambda b, j, i: (b, 0, 0)),
                  blk],
        out_specs=(blk, blk, blk, pl.BlockSpec((None, None, None, 2, tq), lambda b, j, i: (b, j, i, 0, 0))),
        out_shape=(jax.ShapeDtypeStruct((bl, t, B_WIDTH), BF16), jax.ShapeDtypeStruct((bl, t, B_WIDTH), BF16),
                   jax.ShapeDtypeStruct((bl, t, B_WIDTH), F32),
                   jax.ShapeDtypeStruct((bl, pairs, nq, 2, tq), F32)),
        compiler_params=_params(("parallel", "parallel", "arbitrary")),
    )(q_cat, kvup, kvup, kr, z)


def _mla_dq(q_cat, kvup, kr, do, o, lse, tabs, tq):
    bl, t, _ = q_cat.shape
    nq = t // tq
    pairs = B_HEADS // 2
    scale = B_QK_DIM ** -0.5
    v_blk0 = (B_HEADS * LANES) // LANES

    def body(q_ref, k_ref, v_ref, kr_ref, do_ref, o_ref, lse_ref, tc, tsa, tsb, dq_ref, drow_ref):
        qi = pl.program_id(2)
        dov, ov, lsev = do_ref[...], o_ref[...], lse_ref[...]
        qs = [q_ref[:, e * LANES:(e + 1) * LANES] for e in range(2)]
        terms = [_head_terms(dov, ov, lsev, e) for e in range(2)]
        row = lax.broadcasted_iota(jnp.int32, (tq, tq), 0)
        col = lax.broadcasted_iota(jnp.int32, (tq, tq), 1)
        tri = col <= row

        def tile(kb, carry, masked):
            k0 = pl.multiple_of(kb * tq, tq)
            krv = kr_ref[pl.ds(k0, tq), :]
            v = v_ref[pl.ds(k0, tq), :]
            out = []
            for e in range(2):
                do_e, dl, lse_e = terms[e]
                k = k_ref[pl.ds(k0, tq), e * LANES:(e + 1) * LANES] + krv
                s = _dot_nt(qs[e], k) * scale
                if masked:
                    s = jnp.where(tri, s, NEG)
                p = jnp.exp(s - lse_e)
                ds = (p * (_dot_nt(do_e, v) - dl) * scale).astype(BF16)
                out.append(carry[e] + _dot_nn(ds, k))
            return tuple(out)

        zero = jnp.zeros((tq, LANES), F32)
        carry = lax.fori_loop(0, qi, lambda kb, c: tile(kb, c, False), (zero, zero))
        carry = tile(qi, carry, True)
        for e in range(2):
            dq_ref[:, e * LANES:(e + 1) * LANES] = _rope_apply(carry[e], tc[...], tsa[...], tsb[...], -1).astype(BF16)
            drow_ref[e:e + 1, :] = _col_to_row(terms[e][1], tq)

    blk = pl.BlockSpec((None, tq, LANES), lambda b, j, i: (b, i, j))
    tab = pl.BlockSpec((None, tq, LANES), lambda b, j, i: (b, i, 0))
    qblk = pl.BlockSpec((None, tq, 2 * LANES), lambda b, j, i: (b, i, j))
    return pl.pallas_call(
        body, name="mla_dq", grid=(bl, pairs, nq),
        in_specs=[qblk,
                  pl.BlockSpec((None, t, 2 * LANES), lambda b, j, i: (b, 0, j)),
                  pl.BlockSpec((None, t, LANES), lambda b, j, i: (b, 0, v_blk0 + j)),
                  pl.BlockSpec((None, t, LANES), lambda b, j, i: (b, 0, 0)),
                  blk, blk, blk, tab, tab, tab],
        out_specs=(qblk, pl.BlockSpec((None, None, None, 2, tq), lambda b, j, i: (b, j, i, 0, 0))),
        out_shape=(jax.ShapeDtypeStruct((bl, t, B_HEADS * LANES), BF16),
                   jax.ShapeDtypeStruct((bl, pairs, nq, 2, tq), F32)),
        compiler_params=_params(("parallel", "parallel", "arbitrary")),
    )(q_cat, kvup, kvup, kr, do, o, lse, *tabs)


def _mla_dkv(q_cat, kvup, kr, do, lse_rows, delta_rows, tq):
    bl, t, _ = q_cat.shape
    nq = t // tq
    pairs = B_HEADS // 2
    scale = B_QK_DIM ** -0.5
    v_blk0 = (B_HEADS * LANES) // LANES

    def body(q_ref, k_ref, v_ref, kr_ref, do_ref, lrow_ref, drow_ref, dk_ref, dv_ref):
        kb = pl.program_id(2)
        v = v_ref[...]
        krv = kr_ref[...]
        ks = [k_ref[:, e * LANES:(e + 1) * LANES] + krv for e in range(2)]
        krow = lax.broadcasted_iota(jnp.int32, (tq, tq), 0)
        qcol = lax.broadcasted_iota(jnp.int32, (tq, tq), 1)
        tri = krow <= qcol
        lane = lax.broadcasted_iota(jnp.int32, (tq, LANES), 1)
        mine = [lane < B_VDIM, lane >= B_VDIM]

        def tile(qb, carry, masked):
            q0 = pl.multiple_of(qb * tq, tq)
            rows = pl.ds(q0, tq)
            dov = do_ref[rows, :]
            dk0, dk1, dv = carry
            dks = [dk0, dk1]
            for e in range(2):
                q = q_ref[rows, e * LANES:(e + 1) * LANES]
                do_e = jnp.where(mine[e], dov, jnp.zeros_like(dov))
                st = _dot_nt(ks[e], q) * scale
                if masked:
                    st = jnp.where(tri, st, NEG)
                pt = jnp.exp(st - lrow_ref[qb, e:e + 1, :])
                dv = dv + _dot_nn(pt.astype(BF16), do_e)
                dst = (pt * (_dot_nt(v, do_e) - drow_ref[qb, e:e + 1, :]) * scale).astype(BF16)
                dks[e] = dks[e] + _dot_nn(dst, q)
            return dks[0], dks[1], dv

        zero = jnp.zeros((tq, LANES), F32)
        carry = tile(kb, (zero, zero, zero), True)
        dk0, dk1, dv = lax.fori_loop(kb + 1, nq, lambda qb, c: tile(qb, c, False), carry)
        dk_ref[:, 0:LANES] = dk0.astype(BF16)
        dk_ref[:, LANES:2 * LANES] = dk1.astype(BF16)
        dv_ref[...] = dv.astype(BF16)

    full = pl.BlockSpec((None, t, LANES), lambda b, j, i: (b, 0, j))
    rows = pl.BlockSpec((None, None, nq, 2, tq), lambda b, j, i: (b, j, 0, 0, 0))
    kblk = pl.BlockSpec((None, tq, 2 * LANES), lambda b, j, i: (b, i, j))
    return pl.pallas_call(
        body, name="mla_dkv", grid=(bl, pairs, nq),
        in_specs=[pl.BlockSpec((None, t, 2 * LANES), lambda b, j, i: (b, 0, j)),
                  kblk,
                  pl.BlockSpec((None, tq, LANES), lambda b, j, i: (b, i, v_blk0 + j)),
                  pl.BlockSpec((None, tq, LANES), lambda b, j, i: (b, i, 0)),
                  full, rows, rows],
        out_specs=(kblk, pl.BlockSpec((None, tq, LANES), lambda b, j, i: (b, i, j))),
        out_shape=(jax.ShapeDtypeStruct((bl, t, B_HEADS * LANES), BF16),
                   jax.ShapeDtypeStruct((bl, t, B_WIDTH), BF16)),
        compiler_params=_params(("parallel", "parallel", "arbitrary")),
    )(q_cat, kvup, kvup, kr, do, lse_rows, delta_rows)


def _adamw(w, g, m, v, *, name):
    r, c = w.shape
    tr = _row_tile(r, 256)
    c1 = 1.0 - ADAM_B1
    c2 = 1.0 - ADAM_B2
    bc1 = 1.0 - ADAM_B1 ** ADAM_STEP
    bc2 = 1.0 - ADAM_B2 ** ADAM_STEP

    def body(w_ref, g_ref, m_ref, v_ref, d_ref, nm_ref, nv_ref):
        gv = g_ref[...]
        nm = ADAM_B1 * m_ref[...] + c1 * gv
        nv = ADAM_B2 * v_ref[...] + c2 * (gv * gv)
        nm_ref[...] = nm
        nv_ref[...] = nv
        d_ref[...] = -ADAM_LR * ((nm / bc1) / (jnp.sqrt(nv / bc2) + ADAM_EPS) + ADAM_WD * w_ref[...])

    blk = pl.BlockSpec((tr, c), lambda i: (i, 0))
    sds = jax.ShapeDtypeStruct((r, c), F32)
    return pl.pallas_call(
        body, name=name, grid=(r // tr,), in_specs=[blk] * 4, out_specs=(blk,) * 3,
        out_shape=(sds,) * 3, compiler_params=_params(("parallel",)),
    )(w, g, m, v)


def _add_my_half(stacked, other, core, *, name):
    nch, a, c = stacked.shape
    h = a // 2
    tr = _row_tile(h, 256)
    nblk = h // tr

    def body(core_ref, s_ref, p_ref, o_ref):
        o_ref[...] = s_ref[...] + p_ref[...]

    return pl.pallas_call(
        body, name=name,
        grid_spec=pltpu.PrefetchScalarGridSpec(
            num_scalar_prefetch=1, grid=(nch, nblk),
            in_specs=[pl.BlockSpec((None, tr, c), lambda k, i, cr: (k, cr[0] * nblk + i, 0)),
                      pl.BlockSpec((None, tr, c), lambda k, i, cr: (k, i, 0))],
            out_specs=pl.BlockSpec((None, tr, c), lambda k, i, cr: (k, i, 0))),
        out_shape=jax.ShapeDtypeStruct((nch, h, c), F32),
        compiler_params=_params(("parallel", "parallel")),
    )(core, stacked, other)


def _sum_chips(parts, *, name):
    nch, h, c = parts.shape
    tr = _row_tile(h, 256)

    def body(p_ref, o_ref):
        acc = p_ref[0] + p_ref[1]
        for k in range(2, nch):
            acc = acc + p_ref[k]
        o_ref[...] = acc

    return pl.pallas_call(
        body, name=name, grid=(h // tr,),
        in_specs=[pl.BlockSpec((nch, tr, c), lambda i: (0, i, 0))],
        out_specs=pl.BlockSpec((tr, c), lambda i: (i, 0)),
        out_shape=jax.ShapeDtypeStruct((h, c), F32), compiler_params=_params(("parallel",)),
    )(parts)


def _place():
    x, y, c = lax.axis_index("x"), lax.axis_index("y"), lax.axis_index("c")
    chips = [(1 - x, y), (x, 1 - y), (1 - x, 1 - y)]
    return x, y, c, chips


def _remote(src, dst, send_sems, recv_sems, k, to):
    return pltpu.make_async_remote_copy(src_ref=src, dst_ref=dst, send_sem=send_sems.at[k],
                                        recv_sem=recv_sems.at[k], device_id=to, device_id_type=MESH)


def _hbm_call(body, name, ins, out_shapes, n_remote, n_local):
    any_spec = pl.BlockSpec(memory_space=pl.ANY)
    return pl.pallas_call(
        body, name=name, in_specs=[any_spec] * len(ins), out_specs=tuple([any_spec] * len(out_shapes)),
        out_shape=tuple(out_shapes),
        scratch_shapes=[pltpu.SemaphoreType.DMA((n_remote,)), pltpu.SemaphoreType.DMA((n_remote,)),
                        pltpu.SemaphoreType.DMA((n_local,))],
    )(*ins)


def _all_gather_chips(shards, *, name):
    n = len(shards)

    def body(*refs):
        ins, outs = refs[:n], refs[n:2 * n]
        send_sems, recv_sems, local_sems = refs[2 * n:]
        x, y, c, chips = _place()
        me = 2 * x + y
        own = [pltpu.make_async_copy(ins[s], outs[s].at[me], local_sems.at[s]) for s in range(n)]
        for cp in own:
            cp.start()
        sent = []
        for s in range(n):
            h = ins[s].shape[0] // 2
            for j, (px, py) in enumerate(chips):
                cp = _remote(ins[s].at[pl.ds(c * h, h)], outs[s].at[me, pl.ds(c * h, h)],
                             send_sems, recv_sems, s * 6 + j, (px, py, c))
                cp.start()
                sent.append(cp)
        for s in range(n):
            h = ins[s].shape[0] // 2
            for j, (px, py) in enumerate(chips):
                slab = outs[s].at[2 * px + py, pl.ds(c * h, h)]
                _remote(slab, slab, send_sems, recv_sems, s * 6 + j, (px, py, c)).wait_recv()
                cp = _remote(slab, slab, send_sems, recv_sems, s * 6 + 3 + j, (x, y, 1 - c))
                cp.start()
                sent.append(cp)
        for s in range(n):
            h = ins[s].shape[0] // 2
            for j, (px, py) in enumerate(chips):
                slab = outs[s].at[2 * px + py, pl.ds((1 - c) * h, h)]
                _remote(slab, slab, send_sems, recv_sems, s * 6 + 3 + j, (x, y, 1 - c)).wait_recv()
        for cp in sent:
            cp.wait_send()
        for cp in own:
            cp.wait()

    out_shapes = [jax.ShapeDtypeStruct((N_CHIPS,) + s.shape, s.dtype) for s in shards]
    return _hbm_call(body, name, shards, out_shapes, 6 * n, n)


def _pair_send_other_half(stacked, *, name):
    n = len(stacked)

    def body(*refs):
        ins, outs = refs[:n], refs[n:2 * n]
        send_sems, recv_sems, _ = refs[2 * n:]
        x, y, c, _chips = _place()
        sent = []
        for s in range(n):
            h = ins[s].shape[1] // 2
            cp = _remote(ins[s].at[:, pl.ds((1 - c) * h, h)], outs[s], send_sems, recv_sems, s, (x, y, 1 - c))
            cp.start()
            sent.append(cp)
        for cp in sent:
            cp.wait_recv()
        for cp in sent:
            cp.wait_send()

    out_shapes = [jax.ShapeDtypeStruct((s.shape[0], s.shape[1] // 2, s.shape[2]), s.dtype) for s in stacked]
    return _hbm_call(body, name, stacked, out_shapes, n, 1)


def _chip_exchange(halves, *, name):
    n = len(halves)

    def body(*refs):
        ins, outs = refs[:n], refs[n:2 * n]
        send_sems, recv_sems, local_sems = refs[2 * n:]
        x, y, c, chips = _place()
        me = 2 * x + y
        own = [pltpu.make_async_copy(ins[s].at[me], outs[s].at[me], local_sems.at[s]) for s in range(n)]
        for cp in own:
            cp.start()
        sent = []
        for s in range(n):
            for j, (px, py) in enumerate(chips):
                cp = _remote(ins[s].at[2 * px + py], outs[s].at[me], send_sems, recv_sems, s * 3 + j, (px, py, c))
                cp.start()
                sent.append(cp)
        for s in range(n):
            for j, (px, py) in enumerate(chips):
                slab = outs[s].at[2 * px + py]
                _remote(slab, slab, send_sems, recv_sems, s * 3 + j, (px, py, c)).wait_recv()
        for cp in sent:
            cp.wait_send()
        for cp in own:
            cp.wait()

    out_shapes = [jax.ShapeDtypeStruct(s.shape, s.dtype) for s in halves]
    return _hbm_call(body, name, halves, out_shapes, 3 * n, n)


def _pair_concat(halves, *, name):
    n = len(halves)

    def body(*refs):
        ins, outs = refs[:n], refs[n:2 * n]
        send_sems, recv_sems, local_sems = refs[2 * n:]
        x, y, c, _chips = _place()
        own, sent = [], []
        for s in range(n):
            h = ins[s].shape[0]
            cp = pltpu.make_async_copy(ins[s], outs[s].at[pl.ds(c * h, h)], local_sems.at[s])
            cp.start()
            own.append(cp)
            cp = _remote(ins[s], outs[s].at[pl.ds(c * h, h)], send_sems, recv_sems, s, (x, y, 1 - c))
            cp.start()
            sent.append(cp)
        for s in range(n):
            h = ins[s].shape[0]
            slab = outs[s].at[pl.ds((1 - c) * h, h)]
            _remote(slab, slab, send_sems, recv_sems, s, (x, y, 1 - c)).wait_recv()
        for cp in sent:
            cp.wait_send()
        for cp in own:
            cp.wait()

    out_shapes = [jax.ShapeDtypeStruct((2 * s.shape[0], s.shape[1]), s.dtype) for s in halves]
    return _hbm_call(body, name, halves, out_shapes, n, n)


def _pack_rows(parts, row_multiple):
    flat = jnp.concatenate([p.reshape(-1) for p in parts])
    quantum = row_multiple * PACK_COLS
    pad = (-flat.shape[0]) % quantum
    flat = jnp.pad(flat, (0, pad))
    return flat.reshape(-1, PACK_COLS)


def _unpack(flat, shapes):
    out, pos = [], 0
    for shp in shapes:
        size = math.prod(shp)
        out.append(flat[pos:pos + size].reshape(shp))
        pos += size
    return out


def _to_chunks_cols(full):
    r, c4 = full.shape
    return full.reshape(r, N_CHIPS, c4 // N_CHIPS).transpose(1, 0, 2)


def _from_chunks_cols(stacked):
    nch, r, c = stacked.shape
    return stacked.transpose(1, 0, 2).reshape(r, nch * c)


def _class_major(a, bl, t, dil):
    w = a.shape[-1]
    if dil == 1:
        return a.reshape(bl, 1, t, w)
    return a.reshape(bl, t // dil, dil, w).transpose(0, 2, 1, 3)


def _natural(a):
    bl, dil, ln, w = a.shape
    if dil == 1:
        return a.reshape(bl * ln, w)
    return a.transpose(0, 2, 1, 3).reshape(bl * ln * dil, w)


def _train_step(x, positions, a_pre_norm, a_w_in, a_w_out, a_post_norm, kv_norm, kv_w_down, kv_latent_norm,
                kv_w_up, b_pre_norm, b_w_in, b_q_norm, b_w_q_up, b_w_out, b_post_norm, loss_target, moments):
    bl, t, d = x.shape
    n = bl * t
    qb = t // A_DILATIONS[-1]
    tq = _tile(t, 256)
    dq4 = d // N_CHIPS

    w_in_a_s = a_w_in[0].astype(BF16)
    outs_s = jnp.concatenate([a_w_out[0], b_w_out[0]], axis=0).astype(BF16)
    small_shapes = [kv_w_down.shape, kv_w_up.shape, b_w_in[0].shape, b_w_q_up[0].shape]
    small_s = _pack_rows([kv_w_down, kv_w_up, b_w_in[0], b_w_q_up[0]], 32).astype(BF16)
    gains_s = jnp.pad(jnp.concatenate([a_pre_norm[0], a_post_norm[0]]), (0, 16 * LANES - 2 * dq4)).reshape(16, LANES)
    g_in_a, g_outs, g_small, g_gains = _all_gather_chips([w_in_a_s, outs_s, small_s, gains_s], name="gather_weights")

    w_in_a = _from_chunks_cols(g_in_a)
    w_out_a = g_outs[:, :A_WIDTH // N_CHIPS].reshape(A_WIDTH, d)
    w_out_b = g_outs[:, A_WIDTH // N_CHIPS:].reshape(B_WIDTH, d)
    sm = [_unpack(g_small[k].reshape(-1), small_shapes) for k in range(N_CHIPS)]
    w_down = jnp.concatenate([sm[k][0] for k in range(N_CHIPS)], axis=0)
    w_up = jnp.concatenate([sm[k][1] for k in range(N_CHIPS)], axis=1)
    w_in_b = jnp.concatenate([sm[k][2] for k in range(N_CHIPS)], axis=1)
    w_q_up = jnp.concatenate([sm[k][3] for k in range(N_CHIPS)], axis=1)
    gflat = g_gains.reshape(N_CHIPS, -1)
    g_a_pre = gflat[:, :dq4].reshape(1, d)
    g_a_post = gflat[:, dq4:2 * dq4].reshape(1, d)

    w_up_h = w_up.reshape(B_KV_LORA, B_HEADS, B_NOPE + B_VDIM)
    w_up_k = jnp.pad(w_up_h[:, :, :B_NOPE], ((0, 0), (0, 0), (0, LANES - B_NOPE))).reshape(B_KV_LORA, B_HEADS * LANES)
    w_up_v = w_up_h[:, :, B_NOPE:].reshape(B_KV_LORA, B_WIDTH)
    w_up_cat = jnp.concatenate([w_up_k, w_up_v], axis=1)
    w_q_up_p = jnp.pad(w_q_up.reshape(B_Q_LORA, B_HEADS, B_QK_DIM),
                       ((0, 0), (0, 0), (0, LANES - B_QK_DIM))).reshape(B_Q_LORA, B_HEADS * LANES)
    zeros_d = lambda c: jnp.zeros((d, c), BF16)
    w_down_p = jnp.concatenate([w_down[:, :B_KV_LORA], zeros_d(B_NOPE), w_down[:, B_KV_LORA:],
                                zeros_d(LANES - B_NOPE - B_ROPE)], axis=1)
    w_cq = w_in_b[:, :B_Q_LORA]
    w_z = w_in_b[:, B_Q_LORA:]

    tabs_a = _rope_tables(positions, A_ROPE_THETA, 0)
    tabs_b = _rope_tables(positions, B_ROPE_THETA, B_NOPE)

    h0 = x.reshape(n, d)
    hn_a = _rms_fwd(h0, g_a_pre, BF16, name="a_pre_norm")
    rope_a = (tabs_a, lambda j: jnp.logical_and(j < 3 * A_GROUPS, j % 3 != 2))
    proj_a = _matmul(hn_a, w_in_a, "nn", BF16, name="a_proj", rope=rope_a)
    z_blk_a = 3 * A_GROUPS
    o_groups, lse_groups, qkv_cm = [], [], []
    for g, dil in enumerate(A_DILATIONS):
        if dil == 1:
            src, cb0 = proj_a.reshape(bl, 1, t, A_IN_WIDTH), 3 * g
        else:
            src, cb0 = _class_major(proj_a[:, 3 * g * A_WIDTH:3 * (g + 1) * A_WIDTH], bl, t, dil), 0
        qkv_cm.append((src, cb0))
        o_g, lse_g = _attn_a_fwd(src, cb0, qb, name=f"attn_a_fwd_{g}")
        o_groups.append(_natural(o_g))
        lse_groups.append(_natural(lse_g))
    ypre_a, om_a, lse_a = _merge_gate_fwd(o_groups, lse_groups, proj_a, z_blk_a)
    y_a = _matmul(ypre_a, w_out_a, "nn", F32, name="a_out")
    h1 = _rms_fwd(y_a, g_a_post, F32, name="a_post_norm", add=h0)

    g_kvn = kv_norm.reshape(1, d)
    g_lat = kv_latent_norm.reshape(1, B_KV_LORA)
    hn_kv = _rms_fwd(h1, g_kvn, BF16, name="kv_norm")
    ckr = _matmul(hn_kv, w_down_p, "nn", F32, name="kv_down")
    c_kv, k_rope = _kv_latent_fwd(ckr, g_lat, tabs_b)
    kvup = _matmul(c_kv, w_up_cat, "nn", BF16, name="kv_up")
    hn_b = _rms_fwd(h1, b_pre_norm, BF16, name="b_pre_norm")
    z_b = _matmul(hn_b, w_z, "nn", BF16, name="b_proj_z")
    cq_raw = _matmul(hn_b, w_cq, "nn", F32, name="b_proj_q")
    c_q = _rms_fwd(cq_raw, b_q_norm, BF16, name="b_q_norm")
    q_cat = _matmul(c_q, w_q_up_p, "nn", BF16, name="b_q_up", rope=(tabs_b, lambda j: True))
    r3 = lambda a: a.reshape(bl, t, a.shape[-1])
    tabs_b3 = tuple(r3(tb) for tb in tabs_b)
    ypre_b, o_b, lse_b, lse_rows_b = _mla_fwd(r3(q_cat), r3(kvup), r3(k_rope), r3(z_b), tq)
    y_b = _matmul(ypre_b.reshape(n, B_WIDTH), w_out_b, "nn", F32, name="b_out")
    h2 = _rms_fwd(y_b, b_post_norm, F32, name="b_post_norm", add=h1)
    dh2, loss_part = _loss_fwd_bwd(h2, loss_target.reshape(n, d))

    dy_b, dg_b_post = _rms_bwd(y_b, b_post_norm, dh2, BF16, name="b_post_norm_bwd")
    dypre_b = _matmul(dy_b, w_out_b, "nt", F32, name="b_out_dx")
    dw_out_b = _matmul(ypre_b.reshape(n, B_WIDTH), dy_b, "tn", F32, name="b_out_dw", tm=1024, tk=512)
    do_b, dz_b = _gate_bwd(dypre_b, o_b.reshape(n, B_WIDTH), z_b, 0, name="b_gate_bwd", with_delta=False)
    dq_cat, delta_rows_b = _mla_dq(r3(q_cat), r3(kvup), r3(k_rope), r3(do_b), o_b, lse_b, tabs_b3, tq)
    dq_cat = dq_cat.reshape(n, -1)
    dk_cat, dv_b = _mla_dkv(r3(q_cat), r3(kvup), r3(k_rope), r3(do_b), lse_rows_b, delta_rows_b, tq)
    dk_cat, dv_b = dk_cat.reshape(n, -1), dv_b.reshape(n, -1)
    dcq_n = _matmul(dq_cat, w_q_up_p, "nt", F32, name="b_q_up_dx")
    dw_q_up_p = _matmul(c_q, dq_cat, "tn", F32, name="b_q_up_dw", tm=1024, tk=512)
    dcq, dg_b_q = _rms_bwd(cq_raw, b_q_norm, dcq_n, BF16, name="b_q_norm_bwd")
    dhn_b = _matmul(dz_b, w_z, "nt", F32, name="b_proj_z_dx")
    dhn_b = _matmul(dcq, w_cq, "nt", F32, name="b_proj_q_dx", add=dhn_b)
    dw_z = _matmul(hn_b, dz_b, "tn", F32, name="b_proj_z_dw", tm=1024, tk=512)
    dw_cq = _matmul(hn_b, dcq, "tn", F32, name="b_proj_q_dw", tm=1024, tk=512)
    dh1, dg_b_pre = _rms_bwd(h1, b_pre_norm, dhn_b, F32, name="b_pre_norm_bwd", adds=(dh2,))
    dckv_n = _matmul(dk_cat, w_up_k, "nt", F32, name="kv_up_k_dx")
    dckv_n = _matmul(dv_b, w_up_v, "nt", F32, name="kv_up_v_dx", add=dckv_n)
    dw_up_k = _matmul(c_kv, dk_cat, "tn", F32, name="kv_up_k_dw", tm=1024, tk=512)
    dw_up_v = _matmul(c_kv, dv_b, "tn", F32, name="kv_up_v_dw", tm=1024, tk=512)
    dckr, dg_lat = _kv_latent_bwd(dckv_n, ckr, g_lat, dk_cat, tabs_b)
    dhn_kv = _matmul(dckr, w_down_p, "nt", F32, name="kv_down_dx")
    dw_down_p = _matmul(hn_kv, dckr, "tn", F32, name="kv_down_dw", tm=1024, tk=512)
    dh1, dg_kvn = _rms_bwd(h1, g_kvn, dhn_kv, F32, name="kv_norm_bwd", adds=(dh1,))

    dy_a, dg_a_post = _rms_bwd(y_a, g_a_post, dh1, BF16, name="a_post_norm_bwd")
    dypre_a = _matmul(dy_a, w_out_a, "nt", F32, name="a_out_dx")
    dw_out_a = _matmul(ypre_a, dy_a, "tn", F32, name="a_out_dw", tm=1024, tk=512)
    do_a, dz_a, delta_a = _gate_bwd(dypre_a, om_a, proj_a, z_blk_a, name="a_gate_bwd", with_delta=True)
    dw_in_a = jnp.zeros((d, A_IN_WIDTH), F32)
    dhn_a = None
    for g, dil in enumerate(A_DILATIONS):
        src, cb0 = qkv_cm[g]
        cm = lambda a: _class_major(a, bl, t, dil)
        dqkv = _attn_a_bwd(src, cb0, cm(do_a), cm(lse_a), cm(delta_a), tuple(cm(tb) for tb in tabs_a), qb,
                           name=f"attn_a_bwd_{g}")
        dqkv = _natural(dqkv)
        dhn_a = _matmul(dqkv, w_in_a, "nt", F32, name=f"a_proj_dx_{g}", add=dhn_a, b_koff=3 * g * (A_WIDTH // _tile(3 * A_WIDTH, 1024)))
        dw_in_a = _matmul(hn_a, dqkv, "tn", F32, name=f"a_proj_dw_{g}", tm=1024, tk=512,
                          out_into=dw_in_a, out_joff=3 * g * (A_WIDTH // _tile(3 * A_WIDTH, 1024)))
    dhn_a = _matmul(dz_a, w_in_a, "nt", F32, name="a_proj_dx_z", add=dhn_a, b_koff=z_blk_a)
    dw_in_a = _matmul(hn_a, dz_a, "tn", F32, name="a_proj_dw_z", tm=1024, tk=512, out_into=dw_in_a, out_joff=z_blk_a)
    grad_x, dg_a_pre = _rms_bwd(h0, g_a_pre, dhn_a, F32, name="a_pre_norm_bwd", adds=(dh1,))

    dw_up = jnp.concatenate([dw_up_k.reshape(B_KV_LORA, B_HEADS, LANES)[:, :, :B_NOPE],
                             dw_up_v.reshape(B_KV_LORA, B_HEADS, B_VDIM)], axis=2).reshape(B_KV_LORA, -1)
    dw_q_up = dw_q_up_p.reshape(B_Q_LORA, B_HEADS, LANES)[:, :, :B_QK_DIM].reshape(B_Q_LORA, -1)
    dw_down = jnp.concatenate([dw_down_p[:, :B_KV_LORA], dw_down_p[:, B_KV_LORA + B_NOPE:B_KV_LORA + B_NOPE + B_ROPE]], axis=1)
    dw_in_b = jnp.concatenate([dw_cq, dw_z], axis=1)
    vec_rep = [dg_kvn.reshape(-1), dg_lat.reshape(-1), dg_b_pre.reshape(-1), dg_b_q.reshape(-1), dg_b_post.reshape(-1)]
    vec_shapes = [(dq4,), (dq4,)] + [v.shape for v in vec_rep]
    r_big = _to_chunks_cols(dw_in_a)
    r_outs = jnp.concatenate([dw_out_a.reshape(N_CHIPS, A_WIDTH // N_CHIPS, d),
                              dw_out_b.reshape(N_CHIPS, B_WIDTH // N_CHIPS, d)], axis=1)
    down_c = dw_down.reshape(N_CHIPS, dq4, -1)
    up_c = _to_chunks_cols(dw_up)
    inb_c = _to_chunks_cols(dw_in_b)
    qup_c = _to_chunks_cols(dw_q_up)
    small_chunks = []
    for k in range(N_CHIPS):
        vecs = [dg_a_pre.reshape(-1)[k * dq4:(k + 1) * dq4], dg_a_post.reshape(-1)[k * dq4:(k + 1) * dq4]] + vec_rep
        small_chunks.append(_pack_rows([down_c[k], up_c[k], inb_c[k], qup_c[k]] + vecs, 16))
    r_small = jnp.stack(small_chunks)

    core = lax.axis_index("c").astype(jnp.int32).reshape(1)
    stacked = [r_big, r_outs, r_small]
    recv = _pair_send_other_half(stacked, name="reduce_pair_send")
    halves = [_add_my_half(s, p, core, name=f"reduce_pair_add_{i}") for i, (s, p) in enumerate(zip(stacked, recv))]
    parts = _chip_exchange(halves, name="reduce_chip_exchange")
    sums = [_sum_chips(p, name=f"reduce_chip_sum_{i}") for i, p in enumerate(parts)]
    g_big, g_outs_r, g_small_r = _pair_concat(sums, name="reduce_pair_concat")

    grads = {}
    grads["a_w_in"] = g_big
    grads["a_w_out"] = g_outs_r[:A_WIDTH // N_CHIPS]
    grads["b_w_out"] = g_outs_r[A_WIDTH // N_CHIPS:]
    small_out_shapes = [down_c.shape[1:], up_c.shape[1:], inb_c.shape[1:], qup_c.shape[1:]] + vec_shapes
    (grads["kv_w_down"], grads["kv_w_up"], grads["b_w_in"], grads["b_w_q_up"], grads["a_pre_norm"],
     grads["a_post_norm"], grads["kv_norm"], grads["kv_latent_norm"], grads["b_pre_norm"], grads["b_q_norm"],
     grads["b_post_norm"]) = _unpack(g_small_r.reshape(-1), small_out_shapes)

    weights = dict(a_pre_norm=a_pre_norm, a_w_in=a_w_in, a_w_out=a_w_out, a_post_norm=a_post_norm, kv_norm=kv_norm,
                   kv_w_down=kv_w_down, kv_latent_norm=kv_latent_norm, kv_w_up=kv_w_up, b_pre_norm=b_pre_norm,
                   b_w_in=b_w_in, b_q_norm=b_q_norm, b_w_q_up=b_w_q_up, b_w_out=b_w_out, b_post_norm=b_post_norm)
    names = list(weights)
    out_g, out_d, out_m, out_v = [], [], [], []
    for i, nm in enumerate(names):
        w = weights[nm]
        two_d = (1, w.shape[0]) if w.ndim == 1 else (w.shape[-2], w.shape[-1])
        gw = grads[nm].reshape(two_d)
        dlt, new_m, new_v = _adamw(w.reshape(two_d), gw, moments[i].reshape(two_d),
                                   moments[len(names) + i].reshape(two_d), name=f"adamw_{nm}")
        out_g.append(gw.reshape(w.shape))
        out_d.append(dlt.reshape(w.shape))
        out_m.append(new_m.reshape(w.shape))
        out_v.append(new_v.reshape(w.shape))
    loss = lax.psum(loss_part[0, 0], ("x", "y", "c"))
    return (loss, grad_x.reshape(bl, t, d), *out_g, *out_d, *out_m, *out_v)


def kernel(x, positions, a_pre_norm, a_w_in, a_w_out, a_post_norm, kv_norm, kv_w_down, kv_latent_norm, kv_w_up, b_pre_norm, b_w_in, b_q_norm, b_w_q_up, b_w_out, b_post_norm, loss_target, m_a_pre_norm, m_a_w_in, m_a_w_out, m_a_post_norm, m_kv_norm, m_kv_w_down, m_kv_latent_norm, m_kv_w_up, m_b_pre_norm, m_b_w_in, m_b_q_norm, m_b_w_q_up, m_b_w_out, m_b_post_norm, v_a_pre_norm, v_a_w_in, v_a_w_out, v_a_post_norm, v_kv_norm, v_kv_w_down, v_kv_latent_norm, v_kv_w_up, v_b_pre_norm, v_b_w_in, v_b_q_norm, v_b_w_q_up, v_b_w_out, v_b_post_norm):
    moments = (m_a_pre_norm, m_a_w_in, m_a_w_out, m_a_post_norm, m_kv_norm, m_kv_w_down, m_kv_latent_norm, m_kv_w_up,
               m_b_pre_norm, m_b_w_in, m_b_q_norm, m_b_w_q_up, m_b_w_out, m_b_post_norm,
               v_a_pre_norm, v_a_w_in, v_a_w_out, v_a_post_norm, v_kv_norm, v_kv_w_down, v_kv_latent_norm, v_kv_w_up,
               v_b_pre_norm, v_b_w_in, v_b_q_norm, v_b_w_q_up, v_b_w_out, v_b_post_norm)
    return _train_step(x, positions, a_pre_norm, a_w_in, a_w_out, a_post_norm, kv_norm, kv_w_down, kv_latent_norm,
                       kv_w_up, b_pre_norm, b_w_in, b_q_norm, b_w_q_up, b_w_out, b_post_norm, loss_target, moments)
```

```python
import functools
import math

import jax
import jax.numpy as jnp
from jax import lax
from jax.experimental import pallas as pl
from jax.experimental.pallas import tpu as pltpu

F32 = jnp.float32
BF16 = jnp.bfloat16
MESH = pl.DeviceIdType.MESH

NORM_EPS = 1e-6
NEG = -1e30
LANES = 128
VMEM_LIMIT = 56 * 1024 * 1024

A_GROUPS = 3
A_DILATIONS = (1, 4, 16)
A_HEADS = 8
A_HEAD_DIM = 128
A_WIDTH = A_HEADS * A_HEAD_DIM
A_ROT_HALF = A_HEAD_DIM // 8
A_ROPE_THETA = 500000.0
A_IN_WIDTH = A_GROUPS * 3 * A_WIDTH + A_WIDTH

B_HEADS = 16
B_NOPE = 64
B_ROPE = 32
B_QK_DIM = B_NOPE + B_ROPE
B_VDIM = 64
B_WIDTH = B_HEADS * B_VDIM
B_Q_LORA = 384
B_KV_LORA = 256
B_ROPE_THETA = 10000.0

ADAM_LR = 0.001
ADAM_B1 = 0.9
ADAM_B2 = 0.999
ADAM_EPS = 1e-08
ADAM_WD = 0.01
ADAM_STEP = 10

N_CHIPS = 4
PACK_COLS = 512


def _params(sem=None):
    return pltpu.CompilerParams(dimension_semantics=sem, vmem_limit_bytes=VMEM_LIMIT)


def _tile(n, want):
    t = min(n, want)
    assert n % t == 0, (n, want)
    return t


def _row_tile(n, want):
    for t in range(min(n, want), 0, -1):
        if n % t == 0 and (t % 8 == 0 or t == n):
            return t
    return n


def _rope_tables(positions, theta, lane0):
    half = 16
    inv_freq = 1.0 / (theta ** (jnp.arange(half, dtype=F32) * (2.0 / (2 * half))))
    ang = positions.astype(F32).reshape(-1)[:, None] * inv_freq
    cos, sin = jnp.cos(ang), jnp.sin(ang)
    n = ang.shape[0]
    pre = jnp.zeros((n, lane0), F32)
    post = jnp.zeros((n, LANES - lane0 - 2 * half), F32)
    z16 = jnp.zeros((n, half), F32)
    c = jnp.concatenate([pre + 1.0, cos, cos, post + 1.0], axis=1)
    sa = jnp.concatenate([pre, -sin, z16, post], axis=1)
    sb = jnp.concatenate([pre, z16, sin, post], axis=1)
    return c, sa, sb


def _rope_apply(x, c, sa, sb, sign):
    k = x.shape[1] // LANES
    if k > 1:
        c, sa, sb = (jnp.concatenate([t] * k, axis=1) for t in (c, sa, sb))
    w = x.shape[1]
    up = pltpu.roll(x, w - 16, 1)
    dn = pltpu.roll(x, 16, 1)
    if sign > 0:
        return x * c + up * sa + dn * sb
    return x * c - up * sa - dn * sb


def _matmul(a, b, mode, out_dtype, *, name, tm=512, tn=1024, tk=1024, add=None, rope=None,
            b_koff=0, out_into=None, out_joff=0, out_scale=None):
    if mode == "nn":
        m, k = a.shape
        n = b.shape[1]
    elif mode == "nt":
        m, k = a.shape
        n = b.shape[0]
    else:
        k, m = a.shape
        n = b.shape[1]
    tm, tn, tk = _tile(m, tm), _tile(n, tn), _tile(k, tk)
    nk = k // tk
    if mode == "nn":
        a_spec = pl.BlockSpec((tm, tk), lambda j, i, kk: (i, kk))
        b_spec = pl.BlockSpec((tk, tn), lambda j, i, kk: (kk, j))
        dims = (((1,), (0,)), ((), ()))
    elif mode == "nt":
        a_spec = pl.BlockSpec((tm, tk), lambda j, i, kk: (i, kk))
        b_spec = pl.BlockSpec((tn, tk), lambda j, i, kk: (j, kk + b_koff))
        dims = (((1,), (1,)), ((), ()))
    else:
        a_spec = pl.BlockSpec((tk, tm), lambda j, i, kk: (kk, i))
        b_spec = pl.BlockSpec((tk, tn), lambda j, i, kk: (kk, j))
        dims = (((0,), (0,)), ((), ()))
    operands = [a, b]
    in_specs = [a_spec, b_spec]
    if add is not None:
        operands.append(add)
        in_specs.append(pl.BlockSpec((tm, tn), lambda j, i, kk: (i, j)))
    if rope is not None:
        tables, rope_pred = rope
        for t in tables:
            operands.append(t)
            in_specs.append(pl.BlockSpec((tm, LANES), lambda j, i, kk: (i, 0)))
    aliases = {}
    if out_into is not None:
        aliases = {len(operands): 0}
        operands.append(out_into)
        in_specs.append(pl.BlockSpec(memory_space=pl.ANY))
        out_shape = jax.ShapeDtypeStruct(out_into.shape, out_into.dtype)
        out_dtype = out_into.dtype
    else:
        out_shape = jax.ShapeDtypeStruct((m, n), out_dtype)
    out_spec = pl.BlockSpec((tm, tn), lambda j, i, kk: (i, j + out_joff))

    def body(*refs):
        a_ref, b_ref = refs[0], refs[1]
        pos = 2
        add_ref = None
        if add is not None:
            add_ref = refs[pos]
            pos += 1
        tab_refs = None
        if rope is not None:
            tab_refs = refs[pos:pos + 3]
            pos += 3
        if out_into is not None:
            pos += 1
        o_ref = refs[pos]
        acc_ref = refs[pos + 1] if nk > 1 else None

        def finish(res):
            if add_ref is not None:
                res = res + add_ref[...].astype(F32)
            if tab_refs is None:
                o_ref[...] = res.astype(o_ref.dtype)
                return
            flag = rope_pred(pl.program_id(0))
            roped = _rope_apply(res, tab_refs[0][...], tab_refs[1][...], tab_refs[2][...], 1)
            if out_scale is not None:
                roped = roped * out_scale
            if flag is True:
                o_ref[...] = roped.astype(o_ref.dtype)
                return

            @pl.when(flag)
            def _():
                o_ref[...] = roped.astype(o_ref.dtype)

            @pl.when(jnp.logical_not(flag))
            def _():
                o_ref[...] = res.astype(o_ref.dtype)

        part = lax.dot_general(a_ref[...].astype(BF16), b_ref[...].astype(BF16), dims,
                               preferred_element_type=F32)
        if nk == 1:
            finish(part)
            return
        kk = pl.program_id(2)

        @pl.when(kk == 0)
        def _():
            acc_ref[...] = part

        @pl.when(kk > 0)
        def _():
            acc_ref[...] += part

        @pl.when(kk == nk - 1)
        def _():
            finish(acc_ref[...])

    return pl.pallas_call(
        body, name=name, grid=(n // tn, m // tm, nk), in_specs=in_specs, out_specs=out_spec,
        out_shape=out_shape, input_output_aliases=aliases,
        scratch_shapes=[pltpu.VMEM((tm, tn), F32)] if nk > 1 else [],
        compiler_params=_params(("parallel", "parallel", "arbitrary")),
    )(*operands)


def _rms_fwd(x, g, out_dtype, *, name, add=None, tr=512):
    n, d = x.shape
    tr = _tile(n, tr)
    row = pl.BlockSpec((tr, d), lambda i: (i, 0))
    vec = pl.BlockSpec((1, d), lambda i: (0, 0))

    def body(*refs):
        x_ref, g_ref = refs[0], refs[1]
        o_ref = refs[-1]
        xv = x_ref[...].astype(F32)
        r = lax.rsqrt(jnp.mean(xv * xv, axis=-1, keepdims=True) + NORM_EPS)
        y = xv * r * g_ref[...]
        if add is not None:
            y = refs[2][...] + y
        o_ref[...] = y.astype(o_ref.dtype)

    ops = [x, g] + ([add] if add is not None else [])
    specs = [row, vec] + ([row] if add is not None else [])
    return pl.pallas_call(
        body, name=name, grid=(n // tr,), in_specs=specs, out_specs=row,
        out_shape=jax.ShapeDtypeStruct((n, d), out_dtype), compiler_params=_params(("parallel",)),
    )(*ops)


def _rms_bwd(x, g, dy, out_dtype, *, name, adds=(), tr=512):
    n, d = x.shape
    tr = _tile(n, tr)
    steps = n // tr
    row = pl.BlockSpec((tr, d), lambda i: (i, 0))
    vec = pl.BlockSpec((1, d), lambda i: (0, 0))
    na = len(adds)

    def body(*refs):
        x_ref, g_ref, dy_ref = refs[:3]
        add_refs = refs[3:3 + na]
        dx_ref, dg_ref, acc_ref = refs[3 + na:]
        i = pl.program_id(0)
        xv = x_ref[...].astype(F32)
        r = lax.rsqrt(jnp.mean(xv * xv, axis=-1, keepdims=True) + NORM_EPS)
        xh = xv * r
        dyv = dy_ref[...].astype(F32)
        part = (dyv * xh).reshape(tr // 8, 8, d).sum(axis=0)

        @pl.when(i == 0)
        def _():
            acc_ref[...] = part

        @pl.when(i > 0)
        def _():
            acc_ref[...] += part

        t = dyv * g_ref[...]
        dx = r * (t - xh * jnp.mean(t * xh, axis=-1, keepdims=True))
        for a_ref in add_refs:
            dx = dx + a_ref[...].astype(F32)
        dx_ref[...] = dx.astype(dx_ref.dtype)

        @pl.when(i == steps - 1)
        def _():
            dg_ref[...] = jnp.sum(acc_ref[...], axis=0, keepdims=True)

    return pl.pallas_call(
        body, name=name, grid=(steps,), in_specs=[row, vec, row] + [row] * na,
        out_specs=(row, vec),
        out_shape=(jax.ShapeDtypeStruct((n, d), out_dtype), jax.ShapeDtypeStruct((1, d), F32)),
        scratch_shapes=[pltpu.VMEM((8, d), F32)], compiler_params=_params(("arbitrary",)),
    )(x, g, dy, *adds)


def _kv_latent_fwd(ckr, g_lat, tabs, *, tr=512):
    n = ckr.shape[0]
    tr = _tile(n, tr)
    lat = B_KV_LORA

    def body(c_ref, k_ref, g_ref, tc, tsa, tsb, ckv_ref, kr_ref):
        xv = c_ref[...]
        r = lax.rsqrt(jnp.mean(xv * xv, axis=-1, keepdims=True) + NORM_EPS)
        ckv_ref[...] = (xv * r * g_ref[...]).astype(BF16)
        kr_ref[...] = _rope_apply(k_ref[...], tc[...], tsa[...], tsb[...], 1).astype(BF16)

    tab = pl.BlockSpec((tr, LANES), lambda i: (i, 0))
    return pl.pallas_call(
        body, name="kv_latent_fwd", grid=(n // tr,),
        in_specs=[pl.BlockSpec((tr, lat), lambda i: (i, 0)),
                  pl.BlockSpec((tr, LANES), lambda i: (i, lat // LANES)),
                  pl.BlockSpec((1, lat), lambda i: (0, 0)), tab, tab, tab],
        out_specs=(pl.BlockSpec((tr, lat), lambda i: (i, 0)), tab),
        out_shape=(jax.ShapeDtypeStruct((n, lat), BF16), jax.ShapeDtypeStruct((n, LANES), BF16)),
        compiler_params=_params(("parallel",)),
    )(ckr, ckr, g_lat, *tabs)


def _kv_latent_bwd(dckv, ckr, g_lat, dk_cat, tabs, *, tr=512):
    n = ckr.shape[0]
    tr = _tile(n, tr)
    steps = n // tr
    lat = B_KV_LORA
    wk = dk_cat.shape[1]

    def body(d_ref, c_ref, g_ref, dk_ref, tc, tsa, tsb, o_ref, dg_ref, acc_ref):
        i = pl.program_id(0)
        xv = c_ref[...]
        r = lax.rsqrt(jnp.mean(xv * xv, axis=-1, keepdims=True) + NORM_EPS)
        xh = xv * r
        dyv = d_ref[...]
        part = (dyv * xh).reshape(tr // 8, 8, lat).sum(axis=0)

        @pl.when(i == 0)
        def _():
            acc_ref[...] = part

        @pl.when(i > 0)
        def _():
            acc_ref[...] += part

        t = dyv * g_ref[...]
        dx = r * (t - xh * jnp.mean(t * xh, axis=-1, keepdims=True))
        o_ref[:, 0:lat] = dx.astype(o_ref.dtype)
        dkr = dk_ref[:, 0:LANES].astype(F32)
        for h in range(1, wk // LANES):
            dkr = dkr + dk_ref[:, h * LANES:(h + 1) * LANES].astype(F32)
        o_ref[:, lat:lat + LANES] = _rope_apply(dkr, tc[...], tsa[...], tsb[...], -1).astype(o_ref.dtype)

        @pl.when(i == steps - 1)
        def _():
            dg_ref[...] = jnp.sum(acc_ref[...], axis=0, keepdims=True)

    tab = pl.BlockSpec((tr, LANES), lambda i: (i, 0))
    return pl.pallas_call(
        body, name="kv_latent_bwd", grid=(steps,),
        in_specs=[pl.BlockSpec((tr, lat), lambda i: (i, 0)), pl.BlockSpec((tr, lat), lambda i: (i, 0)),
                  pl.BlockSpec((1, lat), lambda i: (0, 0)), pl.BlockSpec((tr, wk), lambda i: (i, 0)),
                  tab, tab, tab],
        out_specs=(pl.BlockSpec((tr, lat + LANES), lambda i: (i, 0)), pl.BlockSpec((1, lat), lambda i: (0, 0))),
        out_shape=(jax.ShapeDtypeStruct((n, lat + LANES), BF16), jax.ShapeDtypeStruct((1, lat), F32)),
        scratch_shapes=[pltpu.VMEM((8, lat), F32)], compiler_params=_params(("arbitrary",)),
    )(dckv, ckr, g_lat, dk_cat, *tabs)


def _sigmoid(z):
    return 1.0 / (1.0 + jnp.exp(-z))


def _lane_place(cols, width):
    rows = cols[0].shape[0]
    lane = lax.broadcasted_iota(jnp.int32, (rows, width), 1)
    out = jnp.zeros((rows, width), F32)
    for h, col in enumerate(cols):
        out = jnp.where(lane == h, col, out)
    return out


def _merge_gate_fwd(outs, lses, proj, z_block, *, tr=256):
    n, w = outs[0].shape
    tr = _tile(n, tr)
    ng = len(outs)

    def body(*refs):
        o_refs = refs[:ng]
        l_refs = refs[ng:2 * ng]
        z_ref = refs[2 * ng]
        y_ref, om_ref, lse_ref = refs[2 * ng + 1:]
        ls = [r[...] for r in l_refs]
        mx = ls[0]
        for l in ls[1:]:
            mx = jnp.maximum(mx, l)
        ssum = jnp.exp(ls[0] - mx)
        for l in ls[1:]:
            ssum = ssum + jnp.exp(l - mx)
        tot = mx + jnp.log(ssum)
        lse_ref[...] = tot
        ws = [jnp.exp(l - tot) for l in ls]
        for h in range(A_HEADS):
            sl = slice(h * A_HEAD_DIM, (h + 1) * A_HEAD_DIM)
            o = ws[0][:, h:h + 1] * o_refs[0][:, sl]
            for gi in range(1, ng):
                o = o + ws[gi][:, h:h + 1] * o_refs[gi][:, sl]
            z = z_ref[:, sl].astype(F32)
            om_ref[:, sl] = o.astype(BF16)
            y_ref[:, sl] = (o * (z * _sigmoid(z))).astype(BF16)

    row = pl.BlockSpec((tr, w), lambda i: (i, 0))
    lrow = pl.BlockSpec((tr, A_HEADS), lambda i: (i, 0))
    return pl.pallas_call(
        body, name="merge_gate_fwd", grid=(n // tr,),
        in_specs=[row] * ng + [lrow] * ng + [pl.BlockSpec((tr, w), lambda i: (i, z_block))],
        out_specs=(row, row, lrow),
        out_shape=(jax.ShapeDtypeStruct((n, w), BF16), jax.ShapeDtypeStruct((n, w), BF16),
                   jax.ShapeDtypeStruct((n, A_HEADS), F32)),
        compiler_params=_params(("parallel",)),
    )(*outs, *lses, proj)


def _gate_bwd(dy, o, z_arr, z_block, *, name, with_delta, tr=256):
    n, w = dy.shape
    tr = _tile(n, tr)

    def body(*refs):
        dy_ref, o_ref, z_ref, do_ref, dz_ref = refs[:5]
        dyv = dy_ref[...].astype(F32)
        ov = o_ref[...].astype(F32)
        z = z_ref[...].astype(F32)
        sig = _sigmoid(z)
        do = dyv * (z * sig)
        do_ref[...] = do.astype(BF16)
        dz_ref[...] = (dyv * ov * (sig * (1.0 + z * (1.0 - sig)))).astype(BF16)
        if with_delta:
            prod = do * ov
            cols = [jnp.sum(prod[:, h * A_HEAD_DIM:(h + 1) * A_HEAD_DIM], axis=-1, keepdims=True)
                    for h in range(A_HEADS)]
            refs[5][...] = _lane_place(cols, A_HEADS)

    row = pl.BlockSpec((tr, w), lambda i: (i, 0))
    out_specs = [row, row]
    out_shape = [jax.ShapeDtypeStruct((n, w), BF16), jax.ShapeDtypeStruct((n, w), BF16)]
    if with_delta:
        out_specs.append(pl.BlockSpec((tr, A_HEADS), lambda i: (i, 0)))
        out_shape.append(jax.ShapeDtypeStruct((n, A_HEADS), F32))
    return pl.pallas_call(
        body, name=name, grid=(n // tr,),
        in_specs=[row, row, pl.BlockSpec((tr, w), lambda i: (i, z_block))],
        out_specs=tuple(out_specs), out_shape=tuple(out_shape), compiler_params=_params(("parallel",)),
    )(dy, o, z_arr)


def _loss_fwd_bwd(h, target, *, tr=512):
    n, d = h.shape
    tr = _tile(n, tr)
    steps = n // tr

    def body(h_ref, t_ref, dh_ref, loss_ref, acc_ref):
        i = pl.program_id(0)
        e = h_ref[...] - t_ref[...]
        dh_ref[...] = e / d
        part = (e * e).reshape(tr // 8, 8, d).sum(axis=0)

        @pl.when(i == 0)
        def _():
            acc_ref[...] = part

        @pl.when(i > 0)
        def _():
            acc_ref[...] += part

        @pl.when(i == steps - 1)
        def _():
            s = jnp.sum(jnp.sum(acc_ref[...], axis=-1, keepdims=True), axis=0, keepdims=True)
            loss_ref[...] = 0.5 * s / d

    row = pl.BlockSpec((tr, d), lambda i: (i, 0))
    return pl.pallas_call(
        body, name="loss", grid=(steps,), in_specs=[row, row],
        out_specs=(row, pl.BlockSpec((1, 1), lambda i: (0, 0))),
        out_shape=(jax.ShapeDtypeStruct((n, d), F32), jax.ShapeDtypeStruct((1, 1), F32)),
        scratch_shapes=[pltpu.VMEM((8, d), F32)], compiler_params=_params(("arbitrary",)),
    )(h, target)


def _dot_nt(a, b):
    return lax.dot_general(a, b, (((1,), (1,)), ((), ())), preferred_element_type=F32)


def _dot_nn(a, b):
    return lax.dot_general(a, b, (((1,), (0,)), ((), ())), preferred_element_type=F32)


def _dot_tn(a, b):
    return lax.dot_general(a, b, (((0,), (0,)), ((), ())), preferred_element_type=F32)


def _attn_a_fwd(qkv, cb0, qb, *, name):
    bl, dil, ln, _ = qkv.shape
    nb = ln // qb
    scale = A_HEAD_DIM ** -0.5
    hw = A_WIDTH

    def body(q_ref, kc_ref, kp_ref, vc_ref, vp_ref, o_ref, lse_ref):
        i = pl.program_id(2)
        qi = lax.broadcasted_iota(jnp.int32, (qb, qb), 0)
        ki = lax.broadcasted_iota(jnp.int32, (qb, qb), 1)
        mask_c = ki <= qi
        mask_p = jnp.logical_and(ki >= qi, i >= 1)
        cols = []
        for h in range(A_HEADS):
            sl = slice(h * A_HEAD_DIM, (h + 1) * A_HEAD_DIM)
            q = q_ref[:, sl]
            s_c = jnp.where(mask_c, _dot_nt(q, kc_ref[:, sl]) * scale, NEG)
            m = jnp.max(s_c, axis=-1, keepdims=True)
            if nb > 1:
                s_p = jnp.where(mask_p, _dot_nt(q, kp_ref[:, sl]) * scale, NEG)
                m = jnp.maximum(m, jnp.max(s_p, axis=-1, keepdims=True))
            p_c = jnp.exp(s_c - m)
            l = jnp.sum(p_c, axis=-1, keepdims=True)
            acc = _dot_nn(p_c.astype(BF16), vc_ref[:, sl])
            if nb > 1:
                p_p = jnp.exp(s_p - m)
                l = l + jnp.sum(p_p, axis=-1, keepdims=True)
                acc = acc + _dot_nn(p_p.astype(BF16), vp_ref[:, sl])
            o_ref[:, sl] = acc / l
            cols.append(m + jnp.log(l))
        lse_ref[...] = _lane_place(cols, A_HEADS)

    def spec(off, prev):
        if prev:
            return pl.BlockSpec((None, None, qb, hw), lambda b, r, i: (b, r, jnp.maximum(i - 1, 0), cb0 + off))
        return pl.BlockSpec((None, None, qb, hw), lambda b, r, i: (b, r, i, cb0 + off))

    return pl.pallas_call(
        body, name=name, grid=(bl, dil, nb),
        in_specs=[spec(0, False), spec(1, False), spec(1, True), spec(2, False), spec(2, True)],
        out_specs=(pl.BlockSpec((None, None, qb, hw), lambda b, r, i: (b, r, i, 0)),
                   pl.BlockSpec((None, None, qb, A_HEADS), lambda b, r, i: (b, r, i, 0))),
        out_shape=(jax.ShapeDtypeStruct((bl, dil, ln, hw), F32),
                   jax.ShapeDtypeStruct((bl, dil, ln, A_HEADS), F32)),
        compiler_params=_params(("parallel", "parallel", "arbitrary")),
    )(qkv, qkv, qkv, qkv, qkv)


def _attn_a_bwd(qkv, cb0, do, lse, delta, tabs, qb, *, name):
    bl, dil, ln, _ = qkv.shape
    nb = ln // qb
    scale = A_HEAD_DIM ** -0.5
    hw = A_WIDTH

    def body(q_ref, qn_ref, kc_ref, kp_ref, vc_ref, vp_ref, do_ref, don_ref,
             lse_ref, lsen_ref, dl_ref, dln_ref, tc, tsa, tsb, o_ref):
        i = pl.program_id(2)
        qi = lax.broadcasted_iota(jnp.int32, (qb, qb), 0)
        ki = lax.broadcasted_iota(jnp.int32, (qb, qb), 1)
        mask_c = ki <= qi
        mask_p = jnp.logical_and(ki >= qi, i >= 1)
        mask_n = jnp.logical_and(ki >= qi, i + 1 < nb)
        c, sa, sb = tc[...], tsa[...], tsb[...]
        for h in range(A_HEADS):
            sl = slice(h * A_HEAD_DIM, (h + 1) * A_HEAD_DIM)
            q, kc, vc, dov = q_ref[:, sl], kc_ref[:, sl], vc_ref[:, sl], do_ref[:, sl]
            lse_h = lse_ref[:, h:h + 1]
            dl_h = dl_ref[:, h:h + 1]
            p = jnp.exp(jnp.where(mask_c, _dot_nt(q, kc) * scale, NEG) - lse_h)
            ds = (p * (_dot_nt(dov, vc) - dl_h) * scale).astype(BF16)
            dq = _dot_nn(ds, kc)
            dk = _dot_tn(ds, q)
            dv = _dot_tn(p.astype(BF16), dov)
            if nb > 1:
                kp, vp = kp_ref[:, sl], vp_ref[:, sl]
                p = jnp.exp(jnp.where(mask_p, _dot_nt(q, kp) * scale, NEG) - lse_h)
                ds = (p * (_dot_nt(dov, vp) - dl_h) * scale).astype(BF16)
                dq = dq + _dot_nn(ds, kp)
                qn, don = qn_ref[:, sl], don_ref[:, sl]
                p = jnp.exp(jnp.where(mask_n, _dot_nt(qn, kc) * scale, NEG) - lsen_ref[:, h:h + 1])
                ds = (p * (_dot_nt(don, vc) - dln_ref[:, h:h + 1]) * scale).astype(BF16)
                dk = dk + _dot_tn(ds, qn)
                dv = dv + _dot_tn(p.astype(BF16), don)
            o_ref[:, h * A_HEAD_DIM:(h + 1) * A_HEAD_DIM] = _rope_apply(dq, c, sa, sb, -1).astype(BF16)
            o_ref[:, hw + h * A_HEAD_DIM:hw + (h + 1) * A_HEAD_DIM] = _rope_apply(dk, c, sa, sb, -1).astype(BF16)
            o_ref[:, 2 * hw + h * A_HEAD_DIM:2 * hw + (h + 1) * A_HEAD_DIM] = dv.astype(BF16)

    def cur(w, col):
        return pl.BlockSpec((None, None, qb, w), lambda b, r, i: (b, r, i, col))

    def prev(w, col):
        return pl.BlockSpec((None, None, qb, w), lambda b, r, i: (b, r, jnp.maximum(i - 1, 0), col))

    def nxt(w, col):
        return pl.BlockSpec((None, None, qb, w), lambda b, r, i: (b, r, jnp.minimum(i + 1, nb - 1), col))

    return pl.pallas_call(
        body, name=name, grid=(bl, dil, nb),
        in_specs=[cur(hw, cb0), nxt(hw, cb0), cur(hw, cb0 + 1), prev(hw, cb0 + 1),
                  cur(hw, cb0 + 2), prev(hw, cb0 + 2), cur(hw, 0), nxt(hw, 0),
                  cur(A_HEADS, 0), nxt(A_HEADS, 0), cur(A_HEADS, 0), nxt(A_HEADS, 0),
                  cur(LANES, 0), cur(LANES, 0), cur(LANES, 0)],
        out_specs=cur(3 * hw, 0),
        out_shape=jax.ShapeDtypeStruct((bl, dil, ln, 3 * hw), BF16),
        compiler_params=_params(("parallel", "parallel", "arbitrary")),
    )(qkv, qkv, qkv, qkv, qkv, qkv, do, do, lse, lse, delta, delta, *tabs)


def _causal_mask(q0, k0, tq, tk):
    qi = lax.broadcasted_iota(jnp.int32, (tq, tk), 0) + q0
    ki = lax.broadcasted_iota(jnp.int32, (tq, tk), 1) + k0
    return ki <= qi


def _attn_b_fwd(q_cat, kvup, kr, z, tq):
    bl, t, _ = q_cat.shape
    nq = t // tq
    pairs = B_HEADS // 2
    scale = B_QK_DIM ** -0.5
    v_blk0 = (B_HEADS * LANES) // LANES

    def body(q_ref, k_ref, v_ref, kr_ref, z_ref, y_ref, o_ref, lse_ref):
        qi = pl.program_id(2)
        outs, lses = [], []
        for e in range(2):
            sl = slice(e * LANES, (e + 1) * LANES)
            q = q_ref[:, sl]

            def step(kb, carry, sl=sl, q=q):
                m, l, acc = carry
                k0 = pl.multiple_of(kb * tq, tq)
                k = k_ref[pl.ds(k0, tq), sl] + kr_ref[pl.ds(k0, tq), :]
                s = jnp.where(_causal_mask(qi * tq, k0, tq, tq), _dot_nt(q, k) * scale, NEG)
                m_new = jnp.maximum(m, jnp.max(s, axis=-1, keepdims=True))
                alpha = jnp.exp(m - m_new)
                p = jnp.exp(s - m_new)
                l = alpha * l + jnp.sum(p, axis=-1, keepdims=True)
                acc = alpha * acc + _dot_nn(p.astype(BF16), v_ref[pl.ds(k0, tq), :])
                return m_new, l, acc

            init = (jnp.full((tq, 1), NEG, F32), jnp.zeros((tq, 1), F32), jnp.zeros((tq, LANES), F32))
            m, l, acc = lax.fori_loop(0, qi + 1, step, init)
            outs.append(acc / l)
            lses.append(m + jnp.log(l))
        lane = lax.broadcasted_iota(jnp.int32, (tq, LANES), 1)
        first = lane < B_VDIM
        o = jnp.where(first, outs[0], outs[1])
        zv = z_ref[...].astype(F32)
        o_ref[...] = o.astype(BF16)
        y_ref[...] = (o * (zv * _sigmoid(zv))).astype(BF16)
        lse_ref[...] = jnp.where(first, lses[0], lses[1])

    blk = pl.BlockSpec((None, tq, LANES), lambda b, j, i: (b, i, j))
    return pl.pallas_call(
        body, name="attn_b_fwd", grid=(bl, pairs, nq),
        in_specs=[pl.BlockSpec((None, tq, 2 * LANES), lambda b, j, i: (b, i, j)),
                  pl.BlockSpec((None, t, 2 * LANES), lambda b, j, i: (b, 0, j)),
                  pl.BlockSpec((None, t, LANES), lambda b, j, i: (b, 0, v_blk0 + j)),
                  pl.BlockSpec((None, t, LANES), lambda b, j, i: (b, 0, 0)),
                  blk],
        out_specs=(blk, blk, blk),
        out_shape=(jax.ShapeDtypeStruct((bl, t, B_WIDTH), BF16), jax.ShapeDtypeStruct((bl, t, B_WIDTH), BF16),
                   jax.ShapeDtypeStruct((bl, t, B_WIDTH), F32)),
        compiler_params=_params(("parallel", "parallel", "arbitrary")),
    )(q_cat, kvup, kvup, kr, z)


def _head_terms(do, o, lse, e):
    rows = do.shape[0]
    lane = lax.broadcasted_iota(jnp.int32, (rows, LANES), 1)
    mine = (lane < B_VDIM) if e == 0 else (lane >= B_VDIM)
    prod = do.astype(F32) * o.astype(F32)
    dl = jnp.sum(jnp.where(mine, prod, 0.0), axis=-1, keepdims=True)
    do_e = jnp.where(mine, do, jnp.zeros_like(do))
    return do_e, dl, lse[:, e * B_VDIM:e * B_VDIM + 1]


def _attn_b_dq(q_cat, kvup, kr, do, o, lse, tabs, tq):
    bl, t, _ = q_cat.shape
    nq = t // tq
    pairs = B_HEADS // 2
    scale = B_QK_DIM ** -0.5
    v_blk0 = (B_HEADS * LANES) // LANES

    def body(q_ref, k_ref, v_ref, kr_ref, do_ref, o_ref, lse_ref, tc, tsa, tsb, dq_ref):
        qi = pl.program_id(2)
        dov, ov, lsev = do_ref[...], o_ref[...], lse_ref[...]
        for e in range(2):
            sl = slice(e * LANES, (e + 1) * LANES)
            q = q_ref[:, sl]
            do_e, dl, lse_e = _head_terms(dov, ov, lsev, e)

            def step(kb, dq, sl=sl, q=q, do_e=do_e, dl=dl, lse_e=lse_e):
                k0 = pl.multiple_of(kb * tq, tq)
                k = k_ref[pl.ds(k0, tq), sl] + kr_ref[pl.ds(k0, tq), :]
                s = jnp.where(_causal_mask(qi * tq, k0, tq, tq), _dot_nt(q, k) * scale, NEG)
                p = jnp.exp(s - lse_e)
                dp = _dot_nt(do_e, v_ref[pl.ds(k0, tq), :])
                ds = (p * (dp - dl) * scale).astype(BF16)
                return dq + _dot_nn(ds, k)

            dq = lax.fori_loop(0, qi + 1, step, jnp.zeros((tq, LANES), F32))
            dq_ref[:, sl] = _rope_apply(dq, tc[...], tsa[...], tsb[...], -1).astype(BF16)

    blk = pl.BlockSpec((None, tq, LANES), lambda b, j, i: (b, i, j))
    tab = pl.BlockSpec((None, tq, LANES), lambda b, j, i: (b, i, 0))
    qblk = pl.BlockSpec((None, tq, 2 * LANES), lambda b, j, i: (b, i, j))
    return pl.pallas_call(
        body, name="attn_b_dq", grid=(bl, pairs, nq),
        in_specs=[qblk,
                  pl.BlockSpec((None, t, 2 * LANES), lambda b, j, i: (b, 0, j)),
                  pl.BlockSpec((None, t, LANES), lambda b, j, i: (b, 0, v_blk0 + j)),
                  pl.BlockSpec((None, t, LANES), lambda b, j, i: (b, 0, 0)),
                  blk, blk, blk, tab, tab, tab],
        out_specs=qblk,
        out_shape=jax.ShapeDtypeStruct((bl, t, B_HEADS * LANES), BF16),
        compiler_params=_params(("parallel", "parallel", "arbitrary")),
    )(q_cat, kvup, kvup, kr, do, o, lse, *tabs)


def _attn_b_dkv(q_cat, kvup, kr, do, o, lse, tq):
    bl, t, _ = q_cat.shape
    nq = t // tq
    pairs = B_HEADS // 2
    scale = B_QK_DIM ** -0.5
    v_blk0 = (B_HEADS * LANES) // LANES

    def body(q_ref, k_ref, v_ref, kr_ref, do_ref, o_ref, lse_ref, dk_ref, dv_ref):
        kb = pl.program_id(2)
        v = v_ref[...]
        dv = jnp.zeros((tq, LANES), F32)
        for e in range(2):
            sl = slice(e * LANES, (e + 1) * LANES)
            k = k_ref[:, sl] + kr_ref[...]

            def step(qbi, carry, sl=sl, k=k, e=e):
                dk, dv = carry
                q0 = pl.multiple_of(qbi * tq, tq)
                rows = pl.ds(q0, tq)
                q = q_ref[rows, sl]
                do_e, dl, lse_e = _head_terms(do_ref[rows, :], o_ref[rows, :], lse_ref[rows, :], e)
                s = jnp.where(_causal_mask(q0, kb * tq, tq, tq), _dot_nt(q, k) * scale, NEG)
                p = jnp.exp(s - lse_e)
                dv = dv + _dot_tn(p.astype(BF16), do_e)
                ds = (p * (_dot_nt(do_e, v) - dl) * scale).astype(BF16)
                return dk + _dot_tn(ds, q), dv

            dk, dv = lax.fori_loop(kb, nq, step, (jnp.zeros((tq, LANES), F32), dv))
            dk_ref[:, sl] = dk.astype(BF16)
        dv_ref[...] = dv.astype(BF16)

    full = pl.BlockSpec((None, t, LANES), lambda b, j, i: (b, 0, j))
    kblk = pl.BlockSpec((None, tq, 2 * LANES), lambda b, j, i: (b, i, j))
    return pl.pallas_call(
        body, name="attn_b_dkv", grid=(bl, pairs, nq),
        in_specs=[pl.BlockSpec((None, t, 2 * LANES), lambda b, j, i: (b, 0, j)),
                  kblk,
                  pl.BlockSpec((None, tq, LANES), lambda b, j, i: (b, i, v_blk0 + j)),
                  pl.BlockSpec((None, tq, LANES), lambda b, j, i: (b, i, 0)),
                  full, full, full],
        out_specs=(kblk, pl.BlockSpec((None, tq, LANES), lambda b, j, i: (b, i, j))),
        out_shape=(jax.ShapeDtypeStruct((bl, t, B_HEADS * LANES), BF16),
                   jax.ShapeDtypeStruct((bl, t, B_WIDTH), BF16)),
        compiler_params=_params(("parallel", "parallel", "arbitrary")),
    )(q_cat, kvup, kvup, kr, do, o, lse)


def _col_to_row(col, rows):
    return jnp.transpose(jnp.broadcast_to(col, (rows, LANES)))[0:1, :]


def _mla_fwd(q_cat, kvup, kr, z, tq):
    bl, t, _ = q_cat.shape
    nq = t // tq
    pairs = B_HEADS // 2
    scale = B_QK_DIM ** -0.5
    v_blk0 = (B_HEADS * LANES) // LANES

    def body(q_ref, k_ref, v_ref, kr_ref, z_ref, y_ref, o_ref, lse_ref, lrow_ref):
        qi = pl.program_id(2)
        qs = [q_ref[:, e * LANES:(e + 1) * LANES] for e in range(2)]
        row = lax.broadcasted_iota(jnp.int32, (tq, tq), 0)
        col = lax.broadcasted_iota(jnp.int32, (tq, tq), 1)
        tri = col <= row

        lane = lax.broadcasted_iota(jnp.int32, (tq, LANES), 1)
        first = lane < B_VDIM
        ones_at = [jnp.where(lane == B_VDIM, 1.0, 0.0).astype(BF16), jnp.where(lane == 0, 1.0, 0.0).astype(BF16)]
        sum_lane = [B_VDIM, 0]

        def tile(kb, carry, masked):
            k0 = pl.multiple_of(kb * tq, tq)
            krv = kr_ref[pl.ds(k0, tq), :]
            v = v_ref[pl.ds(k0, tq), :]
            vs = [jnp.where(first, v, ones_at[0]), jnp.where(first, ones_at[1], v)]
            ss = []
            for e in range(2):
                k = k_ref[pl.ds(k0, tq), e * LANES:(e + 1) * LANES] + krv
                s = _dot_nt(qs[e], k)
                ss.append(jnp.where(tri, s, NEG) if masked else s)
            ms = [jnp.maximum(carry[e][0], jnp.max(ss[e], axis=-1, keepdims=True)) for e in range(2)]
            ps = [jnp.exp2(ss[e] - ms[e]).astype(BF16) for e in range(2)]
            out = []
            for e in range(2):
                alpha = jnp.exp2(carry[e][0] - ms[e])
                out.append((ms[e], alpha * carry[e][1] + _dot_nn(ps[e], vs[e])))
            return tuple(out)

        one = (jnp.full((tq, 1), NEG, F32), jnp.zeros((tq, LANES), F32))
        carry = lax.fori_loop(0, qi, lambda kb, c: tile(kb, c, False), (one, one))
        carry = tile(qi, carry, True)
        ls = [carry[e][1][:, sum_lane[e]:sum_lane[e] + 1] for e in range(2)]
        outs = [carry[e][1] / ls[e] for e in range(2)]
        lses = [carry[e][0] + jnp.log2(ls[e]) for e in range(2)]
        o = jnp.where(first, outs[0], outs[1])
        zv = z_ref[...].astype(F32)
        o_ref[...] = o.astype(BF16)
        y_ref[...] = (o * (zv * _sigmoid(zv))).astype(BF16)
        lse_ref[...] = jnp.where(first, lses[0], lses[1])
        for e in range(2):
            lrow_ref[e:e + 1, :] = _col_to_row(lses[e], tq)

    blk = pl.BlockSpec((None, tq, LANES), lambda b, j, i: (b, i, j))
    return pl.pallas_call(
        body, name="mla_fwd", grid=(bl, pairs, nq),
        in_specs=[pl.BlockSpec((None, tq, 2 * LANES), lambda b, j, i: (b, i, j)),
                  pl.BlockSpec((None, t, 2 * LANES), lambda b, j, i: (b, 0, j)),
                  pl.BlockSpec((None, t, LANES), lambda b, j, i: (b, 0, v_blk0 + j)),
                  pl.BlockSpec((None, t, LANES), lambda b, j, i: (b, 0, 0)),
                  blk],
        out_specs=(blk, blk, blk, pl.BlockSpec((None, None, None, 2, tq), lambda b, j, i: (b, j, i, 0, 0))),
        out_shape=(jax.ShapeDtypeStruct((bl, t, B_WIDTH), BF16), jax.ShapeDtypeStruct((bl, t, B_WIDTH), BF16),
                   jax.ShapeDtypeStruct((bl, t, B_WIDTH), F32),
                   jax.ShapeDtypeStruct((bl, pairs, nq, 2, tq), F32)),
        compiler_params=_params(("parallel", "parallel", "arbitrary")),
    )(q_cat, kvup, kvup, kr, z)


def _mla_dq(q_cat, kvup, kr, do, o, lse, tabs, tq):
    bl, t, _ = q_cat.shape
    nq = t // tq
    pairs = B_HEADS // 2
    scale = B_QK_DIM ** -0.5
    v_blk0 = (B_HEADS * LANES) // LANES

    def body(q_ref, k_ref, v_ref, kr_ref, do_ref, o_ref, lse_ref, tc, tsa, tsb, dq_ref, drow_ref):
        qi = pl.program_id(2)
        dov, ov, lsev = do_ref[...], o_ref[...], lse_ref[...]
        qs = [q_ref[:, e * LANES:(e + 1) * LANES] for e in range(2)]
        terms = [_head_terms(dov, ov, lsev, e) for e in range(2)]
        row = lax.broadcasted_iota(jnp.int32, (tq, tq), 0)
        col = lax.broadcasted_iota(jnp.int32, (tq, tq), 1)
        tri = col <= row

        def tile(kb, carry, masked):
            k0 = pl.multiple_of(kb * tq, tq)
            krv = kr_ref[pl.ds(k0, tq), :]
            v = v_ref[pl.ds(k0, tq), :]
            ks = [k_ref[pl.ds(k0, tq), e * LANES:(e + 1) * LANES] + krv for e in range(2)]
            ss = [_dot_nt(qs[e], ks[e]) for e in range(2)]
            dps = [_dot_nt(terms[e][0], v) for e in range(2)]
            out = []
            for e in range(2):
                s = jnp.where(tri, ss[e], NEG) if masked else ss[e]
                p = jnp.exp2(s - terms[e][2])
                ds = (p * (dps[e] - terms[e][1])).astype(BF16)
                out.append(carry[e] + _dot_nn(ds, ks[e]))
            return tuple(out)

        zero = jnp.zeros((tq, LANES), F32)
        carry = lax.fori_loop(0, qi, lambda kb, c: tile(kb, c, False), (zero, zero))
        carry = tile(qi, carry, True)
        for e in range(2):
            dq_ref[:, e * LANES:(e + 1) * LANES] = _rope_apply(carry[e] * scale, tc[...], tsa[...], tsb[...], -1).astype(BF16)
            drow_ref[e:e + 1, :] = _col_to_row(terms[e][1], tq)

    blk = pl.BlockSpec((None, tq, LANES), lambda b, j, i: (b, i, j))
    tab = pl.BlockSpec((None, tq, LANES), lambda b, j, i: (b, i, 0))
    qblk = pl.BlockSpec((None, tq, 2 * LANES), lambda b, j, i: (b, i, j))
    return pl.pallas_call(
        body, name="mla_dq", grid=(bl, pairs, nq),
        in_specs=[qblk,
                  pl.BlockSpec((None, t, 2 * LANES), lambda b, j, i: (b, 0, j)),
                  pl.BlockSpec((None, t, LANES), lambda b, j, i: (b, 0, v_blk0 + j)),
                  pl.BlockSpec((None, t, LANES), lambda b, j, i: (b, 0, 0)),
                  blk, blk, blk, tab, tab, tab],
        out_specs=(qblk, pl.BlockSpec((None, None, None, 2, tq), lambda b, j, i: (b, j, i, 0, 0))),
        out_shape=(jax.ShapeDtypeStruct((bl, t, B_HEADS * LANES), BF16),
                   jax.ShapeDtypeStruct((bl, pairs, nq, 2, tq), F32)),
        compiler_params=_params(("parallel", "parallel", "arbitrary")),
    )(q_cat, kvup, kvup, kr, do, o, lse, *tabs)


def _mla_dkv(q_cat, kvup, kr, do, lse_rows, delta_rows, tq):
    bl, t, _ = q_cat.shape
    nq = t // tq
    pairs = B_HEADS // 2
    scale = B_QK_DIM ** -0.5
    v_blk0 = (B_HEADS * LANES) // LANES

    def body(q_ref, k_ref, v_ref, kr_ref, do_ref, lrow_ref, drow_ref, dk_ref, dv_ref):
        kb = pl.program_id(2)
        v = v_ref[...]
        krv = kr_ref[...]
        ks = [k_ref[:, e * LANES:(e + 1) * LANES] + krv for e in range(2)]
        krow = lax.broadcasted_iota(jnp.int32, (tq, tq), 0)
        qcol = lax.broadcasted_iota(jnp.int32, (tq, tq), 1)
        tri = krow <= qcol
        lane = lax.broadcasted_iota(jnp.int32, (tq, LANES), 1)
        mine = [lane < B_VDIM, lane >= B_VDIM]

        def tile(qb, carry, masked):
            q0 = pl.multiple_of(qb * tq, tq)
            rows = pl.ds(q0, tq)
            dov = do_ref[rows, :]
            dk0, dk1, dv = carry
            dks = [dk0, dk1]
            qs = [q_ref[rows, e * LANES:(e + 1) * LANES] for e in range(2)]
            does = [jnp.where(mine[e], dov, jnp.zeros_like(dov)) for e in range(2)]
            sts = [_dot_nt(ks[e], qs[e]) for e in range(2)]
            dpts = [_dot_nt(v, does[e]) for e in range(2)]
            for e in range(2):
                st = jnp.where(tri, sts[e], NEG) if masked else sts[e]
                pt = jnp.exp2(st - lrow_ref[qb, e:e + 1, :])
                dv = dv + _dot_nn(pt.astype(BF16), does[e])
                dst = (pt * (dpts[e] - drow_ref[qb, e:e + 1, :])).astype(BF16)
                dks[e] = dks[e] + _dot_nn(dst, qs[e])
            return dks[0], dks[1], dv

        zero = jnp.zeros((tq, LANES), F32)
        carry = tile(kb, (zero, zero, zero), True)
        dk0, dk1, dv = lax.fori_loop(kb + 1, nq, lambda qb, c: tile(qb, c, False), carry)
        ln2 = math.log(2.0)
        dk_ref[:, 0:LANES] = (dk0 * ln2).astype(BF16)
        dk_ref[:, LANES:2 * LANES] = (dk1 * ln2).astype(BF16)
        dv_ref[...] = dv.astype(BF16)

    full = pl.BlockSpec((None, t, LANES), lambda b, j, i: (b, 0, j))
    rows = pl.BlockSpec((None, None, nq, 2, tq), lambda b, j, i: (b, j, 0, 0, 0))
    kblk = pl.BlockSpec((None, tq, 2 * LANES), lambda b, j, i: (b, i, j))
    return pl.pallas_call(
        body, name="mla_dkv", grid=(bl, pairs, nq),
        in_specs=[pl.BlockSpec((None, t, 2 * LANES), lambda b, j, i: (b, 0, j)),
                  kblk,
                  pl.BlockSpec((None, tq, LANES), lambda b, j, i: (b, i, v_blk0 + j)),
                  pl.BlockSpec((None, tq, LANES), lambda b, j, i: (b, i, 0)),
                  full, rows, rows],
        out_specs=(kblk, pl.BlockSpec((None, tq, LANES), lambda b, j, i: (b, i, j))),
        out_shape=(jax.ShapeDtypeStruct((bl, t, B_HEADS * LANES), BF16),
                   jax.ShapeDtypeStruct((bl, t, B_WIDTH), BF16)),
        compiler_params=_params(("parallel", "parallel", "arbitrary")),
    )(q_cat, kvup, kvup, kr, do, lse_rows, delta_rows)


def _adamw(w, g, m, v, *, name):
    r, c = w.shape
    tr = _row_tile(r, 256)
    c1 = 1.0 - ADAM_B1
    c2 = 1.0 - ADAM_B2
    bc1 = 1.0 - ADAM_B1 ** ADAM_STEP
    bc2 = 1.0 - ADAM_B2 ** ADAM_STEP

    def body(w_ref, g_ref, m_ref, v_ref, d_ref, nm_ref, nv_ref):
        gv = g_ref[...]
        nm = ADAM_B1 * m_ref[...] + c1 * gv
        nv = ADAM_B2 * v_ref[...] + c2 * (gv * gv)
        nm_ref[...] = nm
        nv_ref[...] = nv
        d_ref[...] = -ADAM_LR * ((nm / bc1) / (jnp.sqrt(nv / bc2) + ADAM_EPS) + ADAM_WD * w_ref[...])

    blk = pl.BlockSpec((tr, c), lambda i: (i, 0))
    sds = jax.ShapeDtypeStruct((r, c), F32)
    return pl.pallas_call(
        body, name=name, grid=(r // tr,), in_specs=[blk] * 4, out_specs=(blk,) * 3,
        out_shape=(sds,) * 3, compiler_params=_params(("parallel",)),
    )(w, g, m, v)


def _add_my_half(stacked, other, core, *, name):
    nch, a, c = stacked.shape
    h = a // 2
    tr = _row_tile(h, 256)
    nblk = h // tr

    def body(core_ref, s_ref, p_ref, o_ref):
        o_ref[...] = (s_ref[...] + p_ref[...]).astype(o_ref.dtype)

    return pl.pallas_call(
        body, name=name,
        grid_spec=pltpu.PrefetchScalarGridSpec(
            num_scalar_prefetch=1, grid=(nch, nblk),
            in_specs=[pl.BlockSpec((None, tr, c), lambda k, i, cr: (k, cr[0] * nblk + i, 0)),
                      pl.BlockSpec((None, tr, c), lambda k, i, cr: (k, i, 0))],
            out_specs=pl.BlockSpec((None, tr, c), lambda k, i, cr: (k, i, 0))),
        out_shape=jax.ShapeDtypeStruct((nch, h, c), BF16),
        compiler_params=_params(("parallel", "parallel")),
    )(core, stacked, other)


def _sum_chips(parts, *, name):
    nch, h, c = parts.shape
    tr = _row_tile(h, 256)

    def body(p_ref, o_ref):
        acc = p_ref[0].astype(F32) + p_ref[1].astype(F32)
        for k in range(2, nch):
            acc = acc + p_ref[k].astype(F32)
        o_ref[...] = acc

    return pl.pallas_call(
        body, name=name, grid=(h // tr,),
        in_specs=[pl.BlockSpec((nch, tr, c), lambda i: (0, i, 0))],
        out_specs=pl.BlockSpec((tr, c), lambda i: (i, 0)),
        out_shape=jax.ShapeDtypeStruct((h, c), F32), compiler_params=_params(("parallel",)),
    )(parts)


def _place():
    x, y, c = lax.axis_index("x"), lax.axis_index("y"), lax.axis_index("c")
    chips = [(1 - x, y), (x, 1 - y), (1 - x, 1 - y)]
    return x, y, c, chips


def _remote(src, dst, send_sems, recv_sems, k, to):
    return pltpu.make_async_remote_copy(src_ref=src, dst_ref=dst, send_sem=send_sems.at[k],
                                        recv_sem=recv_sems.at[k], device_id=to, device_id_type=MESH)


def _hbm_call(body, name, ins, out_shapes, n_remote, n_local):
    any_spec = pl.BlockSpec(memory_space=pl.ANY)
    return pl.pallas_call(
        body, name=name, in_specs=[any_spec] * len(ins), out_specs=tuple([any_spec] * len(out_shapes)),
        out_shape=tuple(out_shapes),
        scratch_shapes=[pltpu.SemaphoreType.DMA((n_remote,)), pltpu.SemaphoreType.DMA((n_remote,)),
                        pltpu.SemaphoreType.DMA((n_local,))],
    )(*ins)


def _all_gather_chips(shards, *, name):
    n = len(shards)

    def body(*refs):
        ins, outs = refs[:n], refs[n:2 * n]
        send_sems, recv_sems, local_sems = refs[2 * n:]
        x, y, c, chips = _place()
        me = 2 * x + y
        own = [pltpu.make_async_copy(ins[s], outs[s].at[me], local_sems.at[s]) for s in range(n)]
        for cp in own:
            cp.start()
        sent = []
        for s in range(n):
            h = ins[s].shape[0] // 2
            for j, (px, py) in enumerate(chips):
                cp = _remote(ins[s].at[pl.ds(c * h, h)], outs[s].at[me, pl.ds(c * h, h)],
                             send_sems, recv_sems, s * 6 + j, (px, py, c))
                cp.start()
                sent.append(cp)
        for s in range(n):
            h = ins[s].shape[0] // 2
            for j, (px, py) in enumerate(chips):
                slab = outs[s].at[2 * px + py, pl.ds(c * h, h)]
                _remote(slab, slab, send_sems, recv_sems, s * 6 + j, (px, py, c)).wait_recv()
                cp = _remote(slab, slab, send_sems, recv_sems, s * 6 + 3 + j, (x, y, 1 - c))
                cp.start()
                sent.append(cp)
        for s in range(n):
            h = ins[s].shape[0] // 2
            for j, (px, py) in enumerate(chips):
                slab = outs[s].at[2 * px + py, pl.ds((1 - c) * h, h)]
                _remote(slab, slab, send_sems, recv_sems, s * 6 + 3 + j, (x, y, 1 - c)).wait_recv()
        for cp in sent:
            cp.wait_send()
        for cp in own:
            cp.wait()

    out_shapes = [jax.ShapeDtypeStruct((N_CHIPS,) + s.shape, s.dtype) for s in shards]
    return _hbm_call(body, name, shards, out_shapes, 6 * n, n)


def _pair_send_other_half(stacked, *, name):
    n = len(stacked)

    def body(*refs):
        ins, outs = refs[:n], refs[n:2 * n]
        send_sems, recv_sems, _ = refs[2 * n:]
        x, y, c, _chips = _place()
        sent = []
        for s in range(n):
            h = ins[s].shape[1] // 2
            cp = _remote(ins[s].at[:, pl.ds((1 - c) * h, h)], outs[s], send_sems, recv_sems, s, (x, y, 1 - c))
            cp.start()
            sent.append(cp)
        for cp in sent:
            cp.wait_recv()
        for cp in sent:
            cp.wait_send()

    out_shapes = [jax.ShapeDtypeStruct((s.shape[0], s.shape[1] // 2, s.shape[2]), s.dtype) for s in stacked]
    return _hbm_call(body, name, stacked, out_shapes, n, 1)


def _chip_exchange(halves, *, name):
    n = len(halves)

    def body(*refs):
        ins, outs = refs[:n], refs[n:2 * n]
        send_sems, recv_sems, local_sems = refs[2 * n:]
        x, y, c, chips = _place()
        me = 2 * x + y
        own = [pltpu.make_async_copy(ins[s].at[me], outs[s].at[me], local_sems.at[s]) for s in range(n)]
        for cp in own:
            cp.start()
        sent = []
        for s in range(n):
            for j, (px, py) in enumerate(chips):
                cp = _remote(ins[s].at[2 * px + py], outs[s].at[me], send_sems, recv_sems, s * 3 + j, (px, py, c))
                cp.start()
                sent.append(cp)
        for s in range(n):
            for j, (px, py) in enumerate(chips):
                slab = outs[s].at[2 * px + py]
                _remote(slab, slab, send_sems, recv_sems, s * 3 + j, (px, py, c)).wait_recv()
        for cp in sent:
            cp.wait_send()
        for cp in own:
            cp.wait()

    out_shapes = [jax.ShapeDtypeStruct(s.shape, s.dtype) for s in halves]
    return _hbm_call(body, name, halves, out_shapes, 3 * n, n)


def _pair_concat(halves, *, name):
    n = len(halves)

    def body(*refs):
        ins, outs = refs[:n], refs[n:2 * n]
        send_sems, recv_sems, local_sems = refs[2 * n:]
        x, y, c, _chips = _place()
        own, sent = [], []
        for s in range(n):
            h = ins[s].shape[0]
            cp = pltpu.make_async_copy(ins[s], outs[s].at[pl.ds(c * h, h)], local_sems.at[s])
            cp.start()
            own.append(cp)
            cp = _remote(ins[s], outs[s].at[pl.ds(c * h, h)], send_sems, recv_sems, s, (x, y, 1 - c))
            cp.start()
            sent.append(cp)
        for s in range(n):
            h = ins[s].shape[0]
            slab = outs[s].at[pl.ds((1 - c) * h, h)]
            _remote(slab, slab, send_sems, recv_sems, s, (x, y, 1 - c)).wait_recv()
        for cp in sent:
            cp.wait_send()
        for cp in own:
            cp.wait()

    out_shapes = [jax.ShapeDtypeStruct((2 * s.shape[0], s.shape[1]), s.dtype) for s in halves]
    return _hbm_call(body, name, halves, out_shapes, n, n)


def _pack_rows(parts, row_multiple):
    flat = jnp.concatenate([p.reshape(-1) for p in parts])
    quantum = row_multiple * PACK_COLS
    pad = (-flat.shape[0]) % quantum
    flat = jnp.pad(flat, (0, pad))
    return flat.reshape(-1, PACK_COLS)


def _unpack(flat, shapes):
    out, pos = [], 0
    for shp in shapes:
        size = math.prod(shp)
        out.append(flat[pos:pos + size].reshape(shp))
        pos += size
    return out


def _to_chunks_cols(full):
    r, c4 = full.shape
    return full.reshape(r, N_CHIPS, c4 // N_CHIPS).transpose(1, 0, 2)


def _from_chunks_cols(stacked):
    nch, r, c = stacked.shape
    return stacked.transpose(1, 0, 2).reshape(r, nch * c)


def _class_major(a, bl, t, dil):
    w = a.shape[-1]
    if dil == 1:
        return a.reshape(bl, 1, t, w)
    return a.reshape(bl, t // dil, dil, w).transpose(0, 2, 1, 3)


def _natural(a):
    bl, dil, ln, w = a.shape
    if dil == 1:
        return a.reshape(bl * ln, w)
    return a.transpose(0, 2, 1, 3).reshape(bl * ln * dil, w)


def _train_step(x, positions, a_pre_norm, a_w_in, a_w_out, a_post_norm, kv_norm, kv_w_down, kv_latent_norm,
                kv_w_up, b_pre_norm, b_w_in, b_q_norm, b_w_q_up, b_w_out, b_post_norm, loss_target, moments):
    bl, t, d = x.shape
    n = bl * t
    qb = t // A_DILATIONS[-1]
    tq = _tile(t, 256)
    dq4 = d // N_CHIPS

    w_in_a_s = a_w_in[0].astype(BF16)
    outs_s = jnp.concatenate([a_w_out[0], b_w_out[0]], axis=0).astype(BF16)
    small_shapes = [kv_w_down.shape, kv_w_up.shape, b_w_in[0].shape, b_w_q_up[0].shape]
    small_s = _pack_rows([kv_w_down, kv_w_up, b_w_in[0], b_w_q_up[0]], 32).astype(BF16)
    gains_s = jnp.pad(jnp.concatenate([a_pre_norm[0], a_post_norm[0]]), (0, 16 * LANES - 2 * dq4)).reshape(16, LANES)
    g_in_a, g_outs, g_small, g_gains = _all_gather_chips([w_in_a_s, outs_s, small_s, gains_s], name="gather_weights")

    w_in_a = _from_chunks_cols(g_in_a)
    w_out_a = g_outs[:, :A_WIDTH // N_CHIPS].reshape(A_WIDTH, d)
    w_out_b = g_outs[:, A_WIDTH // N_CHIPS:].reshape(B_WIDTH, d)
    sm = [_unpack(g_small[k].reshape(-1), small_shapes) for k in range(N_CHIPS)]
    w_down = jnp.concatenate([sm[k][0] for k in range(N_CHIPS)], axis=0)
    w_up = jnp.concatenate([sm[k][1] for k in range(N_CHIPS)], axis=1)
    w_in_b = jnp.concatenate([sm[k][2] for k in range(N_CHIPS)], axis=1)
    w_q_up = jnp.concatenate([sm[k][3] for k in range(N_CHIPS)], axis=1)
    gflat = g_gains.reshape(N_CHIPS, -1)
    g_a_pre = gflat[:, :dq4].reshape(1, d)
    g_a_post = gflat[:, dq4:2 * dq4].reshape(1, d)

    w_up_h = w_up.reshape(B_KV_LORA, B_HEADS, B_NOPE + B_VDIM)
    w_up_k = jnp.pad(w_up_h[:, :, :B_NOPE], ((0, 0), (0, 0), (0, LANES - B_NOPE))).reshape(B_KV_LORA, B_HEADS * LANES)
    w_up_v = w_up_h[:, :, B_NOPE:].reshape(B_KV_LORA, B_WIDTH)
    w_up_cat = jnp.concatenate([w_up_k, w_up_v], axis=1)
    w_q_up_p = jnp.pad(w_q_up.reshape(B_Q_LORA, B_HEADS, B_QK_DIM),
                       ((0, 0), (0, 0), (0, LANES - B_QK_DIM))).reshape(B_Q_LORA, B_HEADS * LANES)
    zeros_d = lambda c: jnp.zeros((d, c), BF16)
    w_down_p = jnp.concatenate([w_down[:, :B_KV_LORA], zeros_d(B_NOPE), w_down[:, B_KV_LORA:],
                                zeros_d(LANES - B_NOPE - B_ROPE)], axis=1)
    w_cq = w_in_b[:, :B_Q_LORA]
    w_z = w_in_b[:, B_Q_LORA:]

    tabs_a = _rope_tables(positions, A_ROPE_THETA, 0)
    tabs_b = _rope_tables(positions, B_ROPE_THETA, B_NOPE)

    h0 = x.reshape(n, d)
    hn_a = _rms_fwd(h0, g_a_pre, BF16, name="a_pre_norm")
    rope_a = (tabs_a, lambda j: jnp.logical_and(j < 3 * A_GROUPS, j % 3 != 2))
    proj_a = _matmul(hn_a, w_in_a, "nn", BF16, name="a_proj", rope=rope_a)
    z_blk_a = 3 * A_GROUPS
    o_groups, lse_groups, qkv_cm = [], [], []
    for g, dil in enumerate(A_DILATIONS):
        if dil == 1:
            src, cb0 = proj_a.reshape(bl, 1, t, A_IN_WIDTH), 3 * g
        else:
            src, cb0 = _class_major(proj_a[:, 3 * g * A_WIDTH:3 * (g + 1) * A_WIDTH], bl, t, dil), 0
        qkv_cm.append((src, cb0))
        o_g, lse_g = _attn_a_fwd(src, cb0, qb, name=f"attn_a_fwd_{g}")
        o_groups.append(_natural(o_g))
        lse_groups.append(_natural(lse_g))
    ypre_a, om_a, lse_a = _merge_gate_fwd(o_groups, lse_groups, proj_a, z_blk_a)
    y_a = _matmul(ypre_a, w_out_a, "nn", F32, name="a_out")
    h1 = _rms_fwd(y_a, g_a_post, F32, name="a_post_norm", add=h0)

    g_kvn = kv_norm.reshape(1, d)
    g_lat = kv_latent_norm.reshape(1, B_KV_LORA)
    hn_kv = _rms_fwd(h1, g_kvn, BF16, name="kv_norm")
    ckr = _matmul(hn_kv, w_down_p, "nn", F32, name="kv_down")
    c_kv, k_rope = _kv_latent_fwd(ckr, g_lat, tabs_b)
    kvup = _matmul(c_kv, w_up_cat, "nn", BF16, name="kv_up")
    hn_b = _rms_fwd(h1, b_pre_norm, BF16, name="b_pre_norm")
    z_b = _matmul(hn_b, w_z, "nn", BF16, name="b_proj_z")
    cq_raw = _matmul(hn_b, w_cq, "nn", F32, name="b_proj_q")
    c_q = _rms_fwd(cq_raw, b_q_norm, BF16, name="b_q_norm")
    q_cat = _matmul(c_q, w_q_up_p, "nn", BF16, name="b_q_up", rope=(tabs_b, lambda j: True),
                    out_scale=B_QK_DIM ** -0.5 * math.log2(math.e))
    r3 = lambda a: a.reshape(bl, t, a.shape[-1])
    tabs_b3 = tuple(r3(tb) for tb in tabs_b)
    ypre_b, o_b, lse_b, lse_rows_b = _mla_fwd(r3(q_cat), r3(kvup), r3(k_rope), r3(z_b), tq)
    y_b = _matmul(ypre_b.reshape(n, B_WIDTH), w_out_b, "nn", F32, name="b_out")
    h2 = _rms_fwd(y_b, b_post_norm, F32, name="b_post_norm", add=h1)
    dh2, loss_part = _loss_fwd_bwd(h2, loss_target.reshape(n, d))

    dy_b, dg_b_post = _rms_bwd(y_b, b_post_norm, dh2, BF16, name="b_post_norm_bwd")
    dypre_b = _matmul(dy_b, w_out_b, "nt", F32, name="b_out_dx")
    dw_out_b = _matmul(ypre_b.reshape(n, B_WIDTH), dy_b, "tn", F32, name="b_out_dw", tm=1024, tk=512)
    do_b, dz_b = _gate_bwd(dypre_b, o_b.reshape(n, B_WIDTH), z_b, 0, name="b_gate_bwd", with_delta=False)
    dq_cat, delta_rows_b = _mla_dq(r3(q_cat), r3(kvup), r3(k_rope), r3(do_b), o_b, lse_b, tabs_b3, tq)
    dq_cat = dq_cat.reshape(n, -1)
    dk_cat, dv_b = _mla_dkv(r3(q_cat), r3(kvup), r3(k_rope), r3(do_b), lse_rows_b, delta_rows_b, tq)
    dk_cat, dv_b = dk_cat.reshape(n, -1), dv_b.reshape(n, -1)
    dcq_n = _matmul(dq_cat, w_q_up_p, "nt", F32, name="b_q_up_dx")
    dw_q_up_p = _matmul(c_q, dq_cat, "tn", F32, name="b_q_up_dw", tm=1024, tk=512)
    dcq, dg_b_q = _rms_bwd(cq_raw, b_q_norm, dcq_n, BF16, name="b_q_norm_bwd")
    dhn_b = _matmul(dz_b, w_z, "nt", F32, name="b_proj_z_dx")
    dhn_b = _matmul(dcq, w_cq, "nt", F32, name="b_proj_q_dx", add=dhn_b)
    dw_z = _matmul(hn_b, dz_b, "tn", F32, name="b_proj_z_dw", tm=1024, tk=512)
    dw_cq = _matmul(hn_b, dcq, "tn", F32, name="b_proj_q_dw", tm=1024, tk=512)
    dh1, dg_b_pre = _rms_bwd(h1, b_pre_norm, dhn_b, F32, name="b_pre_norm_bwd", adds=(dh2,))
    dckv_n = _matmul(dk_cat, w_up_k, "nt", F32, name="kv_up_k_dx")
    dckv_n = _matmul(dv_b, w_up_v, "nt", F32, name="kv_up_v_dx", add=dckv_n)
    dw_up_k = _matmul(c_kv, dk_cat, "tn", F32, name="kv_up_k_dw", tm=1024, tk=512)
    dw_up_v = _matmul(c_kv, dv_b, "tn", F32, name="kv_up_v_dw", tm=1024, tk=512)
    dckr, dg_lat = _kv_latent_bwd(dckv_n, ckr, g_lat, dk_cat, tabs_b)
    dhn_kv = _matmul(dckr, w_down_p, "nt", F32, name="kv_down_dx")
    dw_down_p = _matmul(hn_kv, dckr, "tn", F32, name="kv_down_dw", tm=1024, tk=512)
    dh1, dg_kvn = _rms_bwd(h1, g_kvn, dhn_kv, F32, name="kv_norm_bwd", adds=(dh1,))

    dy_a, dg_a_post = _rms_bwd(y_a, g_a_post, dh1, BF16, name="a_post_norm_bwd")
    dypre_a = _matmul(dy_a, w_out_a, "nt", F32, name="a_out_dx")
    dw_out_a = _matmul(ypre_a, dy_a, "tn", F32, name="a_out_dw", tm=1024, tk=512)
    do_a, dz_a, delta_a = _gate_bwd(dypre_a, om_a, proj_a, z_blk_a, name="a_gate_bwd", with_delta=True)
    dw_in_a = jnp.zeros((d, A_IN_WIDTH), F32)
    dhn_a = None
    for g, dil in enumerate(A_DILATIONS):
        src, cb0 = qkv_cm[g]
        cm = lambda a: _class_major(a, bl, t, dil)
        dqkv = _attn_a_bwd(src, cb0, cm(do_a), cm(lse_a), cm(delta_a), tuple(cm(tb) for tb in tabs_a), qb,
                           name=f"attn_a_bwd_{g}")
        dqkv = _natural(dqkv)
        dhn_a = _matmul(dqkv, w_in_a, "nt", F32, name=f"a_proj_dx_{g}", add=dhn_a, b_koff=3 * g * (A_WIDTH // _tile(3 * A_WIDTH, 1024)))
        dw_in_a = _matmul(hn_a, dqkv, "tn", F32, name=f"a_proj_dw_{g}", tm=1024, tk=512,
                          out_into=dw_in_a, out_joff=3 * g * (A_WIDTH // _tile(3 * A_WIDTH, 1024)))
    dhn_a = _matmul(dz_a, w_in_a, "nt", F32, name="a_proj_dx_z", add=dhn_a, b_koff=z_blk_a)
    dw_in_a = _matmul(hn_a, dz_a, "tn", F32, name="a_proj_dw_z", tm=1024, tk=512, out_into=dw_in_a, out_joff=z_blk_a)
    grad_x, dg_a_pre = _rms_bwd(h0, g_a_pre, dhn_a, F32, name="a_pre_norm_bwd", adds=(dh1,))

    dw_up = jnp.concatenate([dw_up_k.reshape(B_KV_LORA, B_HEADS, LANES)[:, :, :B_NOPE],
                             dw_up_v.reshape(B_KV_LORA, B_HEADS, B_VDIM)], axis=2).reshape(B_KV_LORA, -1)
    dw_q_up = dw_q_up_p.reshape(B_Q_LORA, B_HEADS, LANES)[:, :, :B_QK_DIM].reshape(B_Q_LORA, -1)
    dw_down = jnp.concatenate([dw_down_p[:, :B_KV_LORA], dw_down_p[:, B_KV_LORA + B_NOPE:B_KV_LORA + B_NOPE + B_ROPE]], axis=1)
    dw_in_b = jnp.concatenate([dw_cq, dw_z], axis=1)
    vec_rep = [dg_kvn.reshape(-1), dg_lat.reshape(-1), dg_b_pre.reshape(-1), dg_b_q.reshape(-1), dg_b_post.reshape(-1)]
    vec_shapes = [(dq4,), (dq4,)] + [v.shape for v in vec_rep]
    r_big = _to_chunks_cols(dw_in_a)
    r_outs = jnp.concatenate([dw_out_a.reshape(N_CHIPS, A_WIDTH // N_CHIPS, d),
                              dw_out_b.reshape(N_CHIPS, B_WIDTH // N_CHIPS, d)], axis=1)
    down_c = dw_down.reshape(N_CHIPS, dq4, -1)
    up_c = _to_chunks_cols(dw_up)
    inb_c = _to_chunks_cols(dw_in_b)
    qup_c = _to_chunks_cols(dw_q_up)
    small_chunks = []
    for k in range(N_CHIPS):
        vecs = [dg_a_pre.reshape(-1)[k * dq4:(k + 1) * dq4], dg_a_post.reshape(-1)[k * dq4:(k + 1) * dq4]] + vec_rep
        small_chunks.append(_pack_rows([down_c[k], up_c[k], inb_c[k], qup_c[k]] + vecs, 16))
    r_small = jnp.stack(small_chunks)

    core = lax.axis_index("c").astype(jnp.int32).reshape(1)
    stacked = [r_big, r_outs, r_small]
    recv = _pair_send_other_half(stacked, name="reduce_pair_send")
    halves = [_add_my_half(s, p, core, name=f"reduce_pair_add_{i}") for i, (s, p) in enumerate(zip(stacked, recv))]
    parts = _chip_exchange(halves, name="reduce_chip_exchange")
    sums = [_sum_chips(p, name=f"reduce_chip_sum_{i}") for i, p in enumerate(parts)]
    g_big, g_outs_r, g_small_r = _pair_concat(sums, name="reduce_pair_concat")

    grads = {}
    grads["a_w_in"] = g_big
    grads["a_w_out"] = g_outs_r[:A_WIDTH // N_CHIPS]
    grads["b_w_out"] = g_outs_r[A_WIDTH // N_CHIPS:]
    small_out_shapes = [down_c.shape[1:], up_c.shape[1:], inb_c.shape[1:], qup_c.shape[1:]] + vec_shapes
    (grads["kv_w_down"], grads["kv_w_up"], grads["b_w_in"], grads["b_w_q_up"], grads["a_pre_norm"],
     grads["a_post_norm"], grads["kv_norm"], grads["kv_latent_norm"], grads["b_pre_norm"], grads["b_q_norm"],
     grads["b_post_norm"]) = _unpack(g_small_r.reshape(-1), small_out_shapes)

    weights = dict(a_pre_norm=a_pre_norm, a_w_in=a_w_in, a_w_out=a_w_out, a_post_norm=a_post_norm, kv_norm=kv_norm,
                   kv_w_down=kv_w_down, kv_latent_norm=kv_latent_norm, kv_w_up=kv_w_up, b_pre_norm=b_pre_norm,
                   b_w_in=b_w_in, b_q_norm=b_q_norm, b_w_q_up=b_w_q_up, b_w_out=b_w_out, b_post_norm=b_post_norm)
    names = list(weights)
    out_g, out_d, out_m, out_v = [], [], [], []
    for i, nm in enumerate(names):
        w = weights[nm]
        two_d = (1, w.shape[0]) if w.ndim == 1 else (w.shape[-2], w.shape[-1])
        gw = grads[nm].reshape(two_d)
        dlt, new_m, new_v = _adamw(w.reshape(two_d), gw, moments[i].reshape(two_d),
                                   moments[len(names) + i].reshape(two_d), name=f"adamw_{nm}")
        out_g.append(gw.reshape(w.shape))
        out_d.append(dlt.reshape(w.shape))
        out_m.append(new_m.reshape(w.shape))
        out_v.append(new_v.reshape(w.shape))
    loss = lax.psum(loss_part[0, 0], ("x", "y", "c"))
    return (loss, grad_x.reshape(bl, t, d), *out_g, *out_d, *out_m, *out_v)


def kernel(x, positions, a_pre_norm, a_w_in, a_w_out, a_post_norm, kv_norm, kv_w_down, kv_latent_norm, kv_w_up, b_pre_norm, b_w_in, b_q_norm, b_w_q_up, b_w_out, b_post_norm, loss_target, m_a_pre_norm, m_a_w_in, m_a_w_out, m_a_post_norm, m_kv_norm, m_kv_w_down, m_kv_latent_norm, m_kv_w_up, m_b_pre_norm, m_b_w_in, m_b_q_norm, m_b_w_q_up, m_b_w_out, m_b_post_norm, v_a_pre_norm, v_a_w_in, v_a_w_out, v_a_post_norm, v_kv_norm, v_kv_w_down, v_kv_latent_norm, v_kv_w_up, v_b_pre_norm, v_b_w_in, v_b_q_norm, v_b_w_q_up, v_b_w_out, v_b_post_norm):
    moments = (m_a_pre_norm, m_a_w_in, m_a_w_out, m_a_post_norm, m_kv_norm, m_kv_w_down, m_kv_latent_norm, m_kv_w_up,
               m_b_pre_norm, m_b_w_in, m_b_q_norm, m_b_w_q_up, m_b_w_out, m_b_post_norm,
               v_a_pre_norm, v_a_w_in, v_a_w_out, v_a_post_norm, v_kv_norm, v_kv_w_down, v_kv_latent_norm, v_kv_w_up,
               v_b_pre_norm, v_b_w_in, v_b_q_norm, v_b_w_q_up, v_b_w_out, v_b_post_norm)
    return _train_step(x, positions, a_pre_norm, a_w_in, a_w_out, a_post_norm, kv_norm, kv_w_down, kv_latent_norm,
                       kv_w_up, b_pre_norm, b_w_in, b_q_norm, b_w_q_up, b_w_out, b_post_norm, loss_target, moments)
```

```python
import math

import jax
import jax.numpy as jnp
from jax import lax
from jax.experimental import pallas as pl
from jax.experimental.pallas import tpu as pltpu

F32 = jnp.float32
BF16 = jnp.bfloat16
MESH = pl.DeviceIdType.MESH

NORM_EPS = 1e-6
NEG = -1e30
LANES = 128
VMEM_LIMIT = 56 * 1024 * 1024
LOG2E = math.log2(math.e)
LN2 = math.log(2.0)

A_GROUPS = 3
A_DILATIONS = (1, 4, 16)
A_HEADS = 8
A_HEAD_DIM = 128
A_WIDTH = A_HEADS * A_HEAD_DIM
A_ROPE_THETA = 500000.0
A_IN_WIDTH = A_GROUPS * 3 * A_WIDTH + A_WIDTH
A_SCALE = A_HEAD_DIM ** -0.5

B_HEADS = 16
B_NOPE = 64
B_ROPE = 32
B_QK_DIM = B_NOPE + B_ROPE
B_VDIM = 64
B_WIDTH = B_HEADS * B_VDIM
B_Q_LORA = 384
B_KV_LORA = 256
B_ROPE_THETA = 10000.0
B_SCALE = B_QK_DIM ** -0.5

ADAM_LR = 0.001
ADAM_B1 = 0.9
ADAM_B2 = 0.999
ADAM_EPS = 1e-08
ADAM_WD = 0.01
ADAM_STEP = 10

N_CHIPS = 4
PACK_COLS = 512


def _params(sem=None):
    return pltpu.CompilerParams(dimension_semantics=sem, vmem_limit_bytes=VMEM_LIMIT)


def _tile(n, want):
    t = min(n, want)
    assert n % t == 0, (n, want)
    return t


def _row_tile(n, want):
    for t in range(min(n, want), 0, -1):
        if n % t == 0 and (t % 16 == 0 or t == n):
            return t
    return n


def _rope_tables(positions, theta, lane0):
    half = 16
    inv_freq = 1.0 / (theta ** (jnp.arange(half, dtype=F32) * (2.0 / (2 * half))))
    n = positions.size
    per_row = LANES // half
    pos = jnp.repeat(positions.astype(F32).reshape(n // per_row, per_row), half, axis=1)
    ang = pos * jnp.tile(inv_freq, per_row)
    cos, sin = jnp.cos(ang).reshape(n, half), jnp.sin(ang).reshape(n, half)
    pre = jnp.zeros((n, lane0), F32)
    post = jnp.zeros((n, LANES - lane0 - 2 * half), F32)
    z16 = jnp.zeros((n, half), F32)
    c = jnp.concatenate([pre + 1.0, cos, cos, post + 1.0], axis=1)
    sa = jnp.concatenate([pre, -sin, z16, post], axis=1)
    sb = jnp.concatenate([pre, z16, sin, post], axis=1)
    return c, sa, sb


def _rope_apply(x, c, sa, sb, sign):
    k = x.shape[1] // LANES
    if k > 1:
        c, sa, sb = (jnp.concatenate([t] * k, axis=1) for t in (c, sa, sb))
    w = x.shape[1]
    up = pltpu.roll(x, w - 16, 1)
    dn = pltpu.roll(x, 16, 1)
    if sign > 0:
        return x * c + up * sa + dn * sb
    return x * c - up * sa - dn * sb


def _matmul(a, b, mode, out_dtype, *, name, tm=512, tn=1024, tk=1024, add=None, rope=None,
            out_scale=None, b_koff=0, out_into=None, out_full=None, out_joff=0, out_chunk_blocks=None):
    if mode == "nn":
        m, k = a.shape
        n = b.shape[1]
    elif mode == "nt":
        m, k = a.shape
        n = b.shape[0]
    else:
        k, m = a.shape
        n = b.shape[1]
    tm, tn, tk = _tile(m, tm), _tile(n, tn), _tile(k, tk)
    nk = k // tk
    if mode == "nn":
        a_spec = pl.BlockSpec((tm, tk), lambda j, i, kk: (i, kk))
        b_spec = pl.BlockSpec((tk, tn), lambda j, i, kk: (kk, j))
        dims = (((1,), (0,)), ((), ()))
    elif mode == "nt":
        a_spec = pl.BlockSpec((tm, tk), lambda j, i, kk: (i, kk))
        b_spec = pl.BlockSpec((tn, tk), lambda j, i, kk: (j, kk + b_koff))
        dims = (((1,), (1,)), ((), ()))
    else:
        a_spec = pl.BlockSpec((tk, tm), lambda j, i, kk: (kk, i))
        b_spec = pl.BlockSpec((tk, tn), lambda j, i, kk: (kk, j))
        dims = (((0,), (0,)), ((), ()))
    operands = [a, b]
    in_specs = [a_spec, b_spec]
    if add is not None:
        operands.append(add)
        in_specs.append(pl.BlockSpec((tm, tn), lambda j, i, kk: (i, j)))
    if rope is not None:
        tables, rope_pred = rope
        for t in tables:
            operands.append(t)
            in_specs.append(pl.BlockSpec((tm, LANES), lambda j, i, kk: (i, 0)))
    aliases = {}
    if out_into is not None:
        aliases = {len(operands): 0}
        operands.append(out_into)
        in_specs.append(pl.BlockSpec(memory_space=pl.ANY))
        out_shape = jax.ShapeDtypeStruct(out_into.shape, out_into.dtype)
    elif out_full is not None:
        out_shape = jax.ShapeDtypeStruct(out_full, out_dtype)
    else:
        out_shape = jax.ShapeDtypeStruct((m, n), out_dtype)
    if out_chunk_blocks is not None:
        out_spec = pl.BlockSpec((None, tm, tn), lambda j, i, kk: ((j + out_joff) // out_chunk_blocks, i,
                                                                  (j + out_joff) % out_chunk_blocks))
    else:
        out_spec = pl.BlockSpec((tm, tn), lambda j, i, kk: (i, j + out_joff))

    def body(*refs):
        a_ref, b_ref = refs[0], refs[1]
        pos = 2
        add_ref = None
        if add is not None:
            add_ref = refs[pos]
            pos += 1
        tab_refs = None
        if rope is not None:
            tab_refs = refs[pos:pos + 3]
            pos += 3
        if out_into is not None:
            pos += 1
        o_ref = refs[pos]
        acc_ref = refs[pos + 1] if nk > 1 else None

        def finish(res):
            if add_ref is not None:
                res = res + add_ref[...].astype(F32)
            if tab_refs is None:
                o_ref[...] = res.astype(o_ref.dtype)
                return
            j = pl.program_id(0)
            flag = rope_pred(j)
            roped = _rope_apply(res, tab_refs[0][...], tab_refs[1][...], tab_refs[2][...], 1)
            if out_scale is not None:
                value, scale_pred = out_scale
                use = scale_pred(j)
                roped = roped * (value if use is True else jnp.where(use, value, 1.0))
            if flag is True:
                o_ref[...] = roped.astype(o_ref.dtype)
                return

            @pl.when(flag)
            def _():
                o_ref[...] = roped.astype(o_ref.dtype)

            @pl.when(jnp.logical_not(flag))
            def _():
                o_ref[...] = res.astype(o_ref.dtype)

        part = lax.dot_general(a_ref[...].astype(BF16), b_ref[...].astype(BF16), dims,
                               preferred_element_type=F32)
        if nk == 1:
            finish(part)
            return
        kk = pl.program_id(2)

        @pl.when(kk == 0)
        def _():
            acc_ref[...] = part

        @pl.when(kk > 0)
        def _():
            acc_ref[...] += part

        @pl.when(kk == nk - 1)
        def _():
            finish(acc_ref[...])

    return pl.pallas_call(
        body, name=name, grid=(n // tn, m // tm, nk), in_specs=in_specs, out_specs=out_spec,
        out_shape=out_shape, input_output_aliases=aliases,
        scratch_shapes=[pltpu.VMEM((tm, tn), F32)] if nk > 1 else [],
        compiler_params=_params(("parallel", "parallel", "arbitrary")),
    )(*operands)


def _rms_fwd(x, g, out_dtype, *, name, add=None, tr=512):
    n, d = x.shape
    tr = _tile(n, tr)
    row = pl.BlockSpec((tr, d), lambda i: (i, 0))
    vec = pl.BlockSpec((1, d), lambda i: (0, 0))

    def body(*refs):
        x_ref, g_ref = refs[0], refs[1]
        o_ref = refs[-1]
        xv = x_ref[...].astype(F32)
        r = lax.rsqrt(jnp.mean(xv * xv, axis=-1, keepdims=True) + NORM_EPS)
        y = xv * r * g_ref[...]
        if add is not None:
            y = refs[2][...] + y
        o_ref[...] = y.astype(o_ref.dtype)

    ops = [x, g] + ([add] if add is not None else [])
    specs = [row, vec] + ([row] if add is not None else [])
    return pl.pallas_call(
        body, name=name, grid=(n // tr,), in_specs=specs, out_specs=row,
        out_shape=jax.ShapeDtypeStruct((n, d), out_dtype), compiler_params=_params(("parallel",)),
    )(*ops)


def _rms_bwd(x, g, dy, out_dtype, *, name, adds=(), tr=512):
    n, d = x.shape
    tr = _tile(n, tr)
    steps = n // tr
    row = pl.BlockSpec((tr, d), lambda i: (i, 0))
    vec = pl.BlockSpec((1, d), lambda i: (0, 0))
    na = len(adds)

    def body(*refs):
        x_ref, g_ref, dy_ref = refs[:3]
        add_refs = refs[3:3 + na]
        dx_ref, dg_ref, acc_ref = refs[3 + na:]
        i = pl.program_id(0)
        xv = x_ref[...].astype(F32)
        r = lax.rsqrt(jnp.mean(xv * xv, axis=-1, keepdims=True) + NORM_EPS)
        xh = xv * r
        dyv = dy_ref[...].astype(F32)
        part = (dyv * xh).reshape(tr // 8, 8, d).sum(axis=0)

        @pl.when(i == 0)
        def _():
            acc_ref[...] = part

        @pl.when(i > 0)
        def _():
            acc_ref[...] += part

        t = dyv * g_ref[...]
        dx = r * (t - xh * jnp.mean(t * xh, axis=-1, keepdims=True))
        for a_ref in add_refs:
            dx = dx + a_ref[...].astype(F32)
        dx_ref[...] = dx.astype(dx_ref.dtype)

        @pl.when(i == steps - 1)
        def _():
            dg_ref[...] = jnp.sum(acc_ref[...], axis=0, keepdims=True)

    return pl.pallas_call(
        body, name=name, grid=(steps,), in_specs=[row, vec, row] + [row] * na,
        out_specs=(row, vec),
        out_shape=(jax.ShapeDtypeStruct((n, d), out_dtype), jax.ShapeDtypeStruct((1, d), F32)),
        scratch_shapes=[pltpu.VMEM((8, d), F32)], compiler_params=_params(("arbitrary",)),
    )(x, g, dy, *adds)


def _kv_latent_fwd(ckr, g_lat, tabs, *, tr=512):
    n = ckr.shape[0]
    tr = _tile(n, tr)
    lat = B_KV_LORA

    def body(c_ref, k_ref, g_ref, tc, tsa, tsb, ckv_ref, kr_ref):
        xv = c_ref[...]
        r = lax.rsqrt(jnp.mean(xv * xv, axis=-1, keepdims=True) + NORM_EPS)
        ckv_ref[...] = (xv * r * g_ref[...]).astype(BF16)
        kr_ref[...] = _rope_apply(k_ref[...], tc[...], tsa[...], tsb[...], 1).astype(BF16)

    tab = pl.BlockSpec((tr, LANES), lambda i: (i, 0))
    return pl.pallas_call(
        body, name="kv_latent_fwd", grid=(n // tr,),
        in_specs=[pl.BlockSpec((tr, lat), lambda i: (i, 0)),
                  pl.BlockSpec((tr, LANES), lambda i: (i, lat // LANES)),
                  pl.BlockSpec((1, lat), lambda i: (0, 0)), tab, tab, tab],
        out_specs=(pl.BlockSpec((tr, lat), lambda i: (i, 0)), tab),
        out_shape=(jax.ShapeDtypeStruct((n, lat), BF16), jax.ShapeDtypeStruct((n, LANES), BF16)),
        compiler_params=_params(("parallel",)),
    )(ckr, ckr, g_lat, *tabs)


def _kv_latent_bwd(dckv, ckr, g_lat, dk_cat, tabs, *, tr=512):
    n = ckr.shape[0]
    tr = _tile(n, tr)
    steps = n // tr
    lat = B_KV_LORA
    wk = dk_cat.shape[1]

    def body(d_ref, c_ref, g_ref, dk_ref, tc, tsa, tsb, o_ref, dg_ref, acc_ref):
        i = pl.program_id(0)
        xv = c_ref[...]
        r = lax.rsqrt(jnp.mean(xv * xv, axis=-1, keepdims=True) + NORM_EPS)
        xh = xv * r
        dyv = d_ref[...]
        part = (dyv * xh).reshape(tr // 8, 8, lat).sum(axis=0)

        @pl.when(i == 0)
        def _():
            acc_ref[...] = part

        @pl.when(i > 0)
        def _():
            acc_ref[...] += part

        t = dyv * g_ref[...]
        dx = r * (t - xh * jnp.mean(t * xh, axis=-1, keepdims=True))
        o_ref[:, 0:lat] = dx.astype(o_ref.dtype)
        dkr = dk_ref[:, 0:LANES].astype(F32)
        for h in range(1, wk // LANES):
            dkr = dkr + dk_ref[:, h * LANES:(h + 1) * LANES].astype(F32)
        o_ref[:, lat:lat + LANES] = _rope_apply(dkr, tc[...], tsa[...], tsb[...], -1).astype(o_ref.dtype)

        @pl.when(i == steps - 1)
        def _():
            dg_ref[...] = jnp.sum(acc_ref[...], axis=0, keepdims=True)

    tab = pl.BlockSpec((tr, LANES), lambda i: (i, 0))
    return pl.pallas_call(
        body, name="kv_latent_bwd", grid=(steps,),
        in_specs=[pl.BlockSpec((tr, lat), lambda i: (i, 0)), pl.BlockSpec((tr, lat), lambda i: (i, 0)),
                  pl.BlockSpec((1, lat), lambda i: (0, 0)), pl.BlockSpec((tr, wk), lambda i: (i, 0)),
                  tab, tab, tab],
        out_specs=(pl.BlockSpec((tr, lat + LANES), lambda i: (i, 0)), pl.BlockSpec((1, lat), lambda i: (0, 0))),
        out_shape=(jax.ShapeDtypeStruct((n, lat + LANES), BF16), jax.ShapeDtypeStruct((1, lat), F32)),
        scratch_shapes=[pltpu.VMEM((8, lat), F32)], compiler_params=_params(("arbitrary",)),
    )(dckv, ckr, g_lat, dk_cat, *tabs)


def _sigmoid(z):
    return 1.0 / (1.0 + jnp.exp(-z))


def _lane_place(cols, width):
    rows = cols[0].shape[0]
    lane = lax.broadcasted_iota(jnp.int32, (rows, width), 1)
    out = jnp.zeros((rows, width), F32)
    for h, col in enumerate(cols):
        out = jnp.where(lane == h, col, out)
    return out


def _merge_gate_fwd(outs, lses, proj, z_block, *, tr=256):
    n, w = outs[0].shape
    tr = _tile(n, tr)
    ng = len(outs)

    def body(*refs):
        o_refs = refs[:ng]
        l_refs = refs[ng:2 * ng]
        z_ref = refs[2 * ng]
        y_ref, om_ref, lse_ref = refs[2 * ng + 1:]
        ls = [r[...] for r in l_refs]
        mx = ls[0]
        for l in ls[1:]:
            mx = jnp.maximum(mx, l)
        ssum = jnp.exp2(ls[0] - mx)
        for l in ls[1:]:
            ssum = ssum + jnp.exp2(l - mx)
        tot = mx + jnp.log2(ssum)
        lse_ref[...] = tot
        ws = [jnp.exp2(l - tot) for l in ls]
        for h in range(A_HEADS):
            sl = slice(h * A_HEAD_DIM, (h + 1) * A_HEAD_DIM)
            o = ws[0][:, h:h + 1] * o_refs[0][:, sl]
            for gi in range(1, ng):
                o = o + ws[gi][:, h:h + 1] * o_refs[gi][:, sl]
            z = z_ref[:, sl].astype(F32)
            om_ref[:, sl] = o.astype(BF16)
            y_ref[:, sl] = (o * (z * _sigmoid(z))).astype(BF16)

    row = pl.BlockSpec((tr, w), lambda i: (i, 0))
    lrow = pl.BlockSpec((tr, A_HEADS), lambda i: (i, 0))
    return pl.pallas_call(
        body, name="merge_gate_fwd", grid=(n // tr,),
        in_specs=[row] * ng + [lrow] * ng + [pl.BlockSpec((tr, w), lambda i: (i, z_block))],
        out_specs=(row, row, lrow),
        out_shape=(jax.ShapeDtypeStruct((n, w), BF16), jax.ShapeDtypeStruct((n, w), BF16),
                   jax.ShapeDtypeStruct((n, A_HEADS), F32)),
        compiler_params=_params(("parallel",)),
    )(*outs, *lses, proj)


def _gate_bwd(dy, o, z_arr, z_block, *, name, with_delta, tr=256):
    n, w = dy.shape
    tr = _tile(n, tr)

    def body(*refs):
        dy_ref, o_ref, z_ref, do_ref, dz_ref = refs[:5]
        dyv = dy_ref[...].astype(F32)
        ov = o_ref[...].astype(F32)
        z = z_ref[...].astype(F32)
        sig = _sigmoid(z)
        do = dyv * (z * sig)
        do_ref[...] = do.astype(BF16)
        dz_ref[...] = (dyv * ov * (sig * (1.0 + z * (1.0 - sig)))).astype(BF16)
        if with_delta:
            prod = do * ov
            cols = [jnp.sum(prod[:, h * A_HEAD_DIM:(h + 1) * A_HEAD_DIM], axis=-1, keepdims=True)
                    for h in range(A_HEADS)]
            refs[5][...] = _lane_place(cols, A_HEADS)

    row = pl.BlockSpec((tr, w), lambda i: (i, 0))
    out_specs = [row, row]
    out_shape = [jax.ShapeDtypeStruct((n, w), BF16), jax.ShapeDtypeStruct((n, w), BF16)]
    if with_delta:
        out_specs.append(pl.BlockSpec((tr, A_HEADS), lambda i: (i, 0)))
        out_shape.append(jax.ShapeDtypeStruct((n, A_HEADS), F32))
    return pl.pallas_call(
        body, name=name, grid=(n // tr,),
        in_specs=[row, row, pl.BlockSpec((tr, w), lambda i: (i, z_block))],
        out_specs=tuple(out_specs), out_shape=tuple(out_shape), compiler_params=_params(("parallel",)),
    )(dy, o, z_arr)


def _loss_fwd_bwd(h, target, *, tr=512):
    n, d = h.shape
    tr = _tile(n, tr)
    steps = n // tr

    def body(h_ref, t_ref, dh_ref, loss_ref, acc_ref):
        i = pl.program_id(0)
        e = h_ref[...] - t_ref[...]
        dh_ref[...] = e / d
        part = (e * e).reshape(tr // 8, 8, d).sum(axis=0)

        @pl.when(i == 0)
        def _():
            acc_ref[...] = part

        @pl.when(i > 0)
        def _():
            acc_ref[...] += part

        @pl.when(i == steps - 1)
        def _():
            s = jnp.sum(jnp.sum(acc_ref[...], axis=-1, keepdims=True), axis=0, keepdims=True)
            loss_ref[...] = 0.5 * s / d

    row = pl.BlockSpec((tr, d), lambda i: (i, 0))
    return pl.pallas_call(
        body, name="loss", grid=(steps,), in_specs=[row, row],
        out_specs=(row, pl.BlockSpec((1, 1), lambda i: (0, 0))),
        out_shape=(jax.ShapeDtypeStruct((n, d), F32), jax.ShapeDtypeStruct((1, 1), F32)),
        scratch_shapes=[pltpu.VMEM((8, d), F32)], compiler_params=_params(("arbitrary",)),
    )(h, target)


def _dot_nt(a, b):
    return lax.dot_general(a, b, (((1,), (1,)), ((), ())), preferred_element_type=F32)


def _dot_nn(a, b):
    return lax.dot_general(a, b, (((1,), (0,)), ((), ())), preferred_element_type=F32)


def _attn_a_fwd(qkv, cb0, qb, *, name):
    bl, dil, ln, _ = qkv.shape
    nb = ln // qb
    hw = A_WIDTH
    heads = range(A_HEADS)
    sls = [slice(h * A_HEAD_DIM, (h + 1) * A_HEAD_DIM) for h in heads]

    def body(q_ref, kc_ref, kp_ref, vc_ref, vp_ref, o_ref, lse_ref):
        i = pl.program_id(2)
        qi = lax.broadcasted_iota(jnp.int32, (qb, qb), 0)
        ki = lax.broadcasted_iota(jnp.int32, (qb, qb), 1)
        mask_c = ki <= qi
        mask_p = jnp.logical_and(ki >= qi, i >= 1)
        s_c = [jnp.where(mask_c, _dot_nt(q_ref[:, sls[h]], kc_ref[:, sls[h]]), NEG) for h in heads]
        m = [jnp.max(s_c[h], axis=-1, keepdims=True) for h in heads]
        if nb > 1:
            s_p = [jnp.where(mask_p, _dot_nt(q_ref[:, sls[h]], kp_ref[:, sls[h]]), NEG) for h in heads]
            m = [jnp.maximum(m[h], jnp.max(s_p[h], axis=-1, keepdims=True)) for h in heads]
        p_c = [jnp.exp2(s_c[h] - m[h]) for h in heads]
        l = [jnp.sum(p_c[h], axis=-1, keepdims=True) for h in heads]
        acc = [_dot_nn(p_c[h].astype(BF16), vc_ref[:, sls[h]]) for h in heads]
        if nb > 1:
            p_p = [jnp.exp2(s_p[h] - m[h]) for h in heads]
            l = [l[h] + jnp.sum(p_p[h], axis=-1, keepdims=True) for h in heads]
            acc = [acc[h] + _dot_nn(p_p[h].astype(BF16), vp_ref[:, sls[h]]) for h in heads]
        for h in heads:
            o_ref[:, sls[h]] = acc[h] / l[h]
        lse_ref[...] = _lane_place([m[h] + jnp.log2(l[h]) for h in heads], A_HEADS)

    def spec(off, prev):
        if prev:
            return pl.BlockSpec((None, None, qb, hw), lambda b, r, i: (b, r, jnp.maximum(i - 1, 0), cb0 + off))
        return pl.BlockSpec((None, None, qb, hw), lambda b, r, i: (b, r, i, cb0 + off))

    return pl.pallas_call(
        body, name=name, grid=(bl, dil, nb),
        in_specs=[spec(0, False), spec(1, False), spec(1, True), spec(2, False), spec(2, True)],
        out_specs=(pl.BlockSpec((None, None, qb, hw), lambda b, r, i: (b, r, i, 0)),
                   pl.BlockSpec((None, None, qb, A_HEADS), lambda b, r, i: (b, r, i, 0))),
        out_shape=(jax.ShapeDtypeStruct((bl, dil, ln, hw), F32),
                   jax.ShapeDtypeStruct((bl, dil, ln, A_HEADS), F32)),
        compiler_params=_params(("parallel", "parallel", "arbitrary")),
    )(qkv, qkv, qkv, qkv, qkv)


def _attn_a_bwd(qkv, cb0, do, lse, delta, lse_t, delta_t, tabs, qb, *, name):
    bl, dil, ln, _ = qkv.shape
    nb = ln // qb
    hw = A_WIDTH

    def body(q_ref, qn_ref, kc_ref, kp_ref, vc_ref, vp_ref, do_ref, don_ref,
             lse_ref, dl_ref, lt_ref, ltn_ref, dt_ref, dtn_ref, tc, tsa, tsb, o_ref):
        i = pl.program_id(2)
        row = lax.broadcasted_iota(jnp.int32, (qb, qb), 0)
        col = lax.broadcasted_iota(jnp.int32, (qb, qb), 1)
        m_qc = col <= row
        m_kc = row <= col
        m_qp = jnp.logical_and(col >= row, i >= 1)
        m_kn = jnp.logical_and(row >= col, i + 1 < nb)
        c, sa, sb = tc[...], tsa[...], tsb[...]
        heads = range(A_HEADS)
        sls = [slice(h * A_HEAD_DIM, (h + 1) * A_HEAD_DIM) for h in heads]
        q, kc = [q_ref[:, sl] for sl in sls], [kc_ref[:, sl] for sl in sls]
        vc, dov = [vc_ref[:, sl] for sl in sls], [do_ref[:, sl] for sl in sls]
        lse_c = [lse_ref[:, h:h + 1] for h in heads]
        dl_c = [dl_ref[:, h:h + 1] for h in heads]
        s = [_dot_nt(q[h], kc[h]) for h in heads]
        st = [_dot_nt(kc[h], q[h]) for h in heads]
        dp = [_dot_nt(dov[h], vc[h]) for h in heads]
        dpt = [_dot_nt(vc[h], dov[h]) for h in heads]
        p = [jnp.exp2(jnp.where(m_qc, s[h], NEG) - lse_c[h]) for h in heads]
        pt = [jnp.exp2(jnp.where(m_kc, st[h], NEG) - lt_ref[h:h + 1, :]) for h in heads]
        dq = [_dot_nn((p[h] * (dp[h] - dl_c[h])).astype(BF16), kc[h]) for h in heads]
        dk = [_dot_nn((pt[h] * (dpt[h] - dt_ref[h:h + 1, :])).astype(BF16), q[h]) for h in heads]
        dv = [_dot_nn(pt[h].astype(BF16), dov[h]) for h in heads]
        if nb > 1:
            kp, vp = [kp_ref[:, sl] for sl in sls], [vp_ref[:, sl] for sl in sls]
            qn, don = [qn_ref[:, sl] for sl in sls], [don_ref[:, sl] for sl in sls]
            s = [_dot_nt(q[h], kp[h]) for h in heads]
            st = [_dot_nt(kc[h], qn[h]) for h in heads]
            dp = [_dot_nt(dov[h], vp[h]) for h in heads]
            dpt = [_dot_nt(vc[h], don[h]) for h in heads]
            p = [jnp.exp2(jnp.where(m_qp, s[h], NEG) - lse_c[h]) for h in heads]
            pt = [jnp.exp2(jnp.where(m_kn, st[h], NEG) - ltn_ref[h:h + 1, :]) for h in heads]
            dq = [dq[h] + _dot_nn((p[h] * (dp[h] - dl_c[h])).astype(BF16), kp[h]) for h in heads]
            dk = [dk[h] + _dot_nn((pt[h] * (dpt[h] - dtn_ref[h:h + 1, :])).astype(BF16), qn[h]) for h in heads]
            dv = [dv[h] + _dot_nn(pt[h].astype(BF16), don[h]) for h in heads]
        for h in heads:
            o_ref[:, h * A_HEAD_DIM:(h + 1) * A_HEAD_DIM] = _rope_apply(dq[h] * A_SCALE, c, sa, sb, -1).astype(BF16)
            o_ref[:, hw + h * A_HEAD_DIM:hw + (h + 1) * A_HEAD_DIM] = _rope_apply(dk[h] * LN2, c, sa, sb, -1).astype(BF16)
            o_ref[:, 2 * hw + h * A_HEAD_DIM:2 * hw + (h + 1) * A_HEAD_DIM] = dv[h].astype(BF16)

    def cur(w, col):
        return pl.BlockSpec((None, None, qb, w), lambda b, r, i: (b, r, i, col))

    def prev(w, col):
        return pl.BlockSpec((None, None, qb, w), lambda b, r, i: (b, r, jnp.maximum(i - 1, 0), col))

    def nxt(w, col):
        return pl.BlockSpec((None, None, qb, w), lambda b, r, i: (b, r, jnp.minimum(i + 1, nb - 1), col))

    t_cur = pl.BlockSpec((None, None, A_HEADS, qb), lambda b, r, i: (b, r, 0, i))
    t_nxt = pl.BlockSpec((None, None, A_HEADS, qb), lambda b, r, i: (b, r, 0, jnp.minimum(i + 1, nb - 1)))
    return pl.pallas_call(
        body, name=name, grid=(bl, dil, nb),
        in_specs=[cur(hw, cb0), nxt(hw, cb0), cur(hw, cb0 + 1), prev(hw, cb0 + 1),
                  cur(hw, cb0 + 2), prev(hw, cb0 + 2), cur(hw, 0), nxt(hw, 0),
                  cur(A_HEADS, 0), cur(A_HEADS, 0), t_cur, t_nxt, t_cur, t_nxt,
                  cur(LANES, 0), cur(LANES, 0), cur(LANES, 0)],
        out_specs=cur(3 * hw, 0),
        out_shape=jax.ShapeDtypeStruct((bl, dil, ln, 3 * hw), BF16),
        compiler_params=_params(("parallel", "parallel", "arbitrary")),
    )(qkv, qkv, qkv, qkv, qkv, qkv, do, do, lse, delta, lse_t, lse_t, delta_t, delta_t, *tabs)


def _head_terms(do, o, lse, e):
    rows = do.shape[0]
    lane = lax.broadcasted_iota(jnp.int32, (rows, LANES), 1)
    mine = (lane < B_VDIM) if e == 0 else (lane >= B_VDIM)
    prod = do.astype(F32) * o.astype(F32)
    dl = jnp.sum(jnp.where(mine, prod, 0.0), axis=-1, keepdims=True)
    do_e = jnp.where(mine, do, jnp.zeros_like(do))
    return do_e, dl, lse[:, e * B_VDIM:e * B_VDIM + 1]


def _col_to_row(col, rows):
    return jnp.transpose(jnp.broadcast_to(col, (rows, LANES)))[0:1, :]


def _mla_fwd(q_cat, kvup, kr, z, tq):
    bl, t, _ = q_cat.shape
    nq = t // tq
    pairs = B_HEADS // 2
    v_blk0 = (B_HEADS * LANES) // LANES

    def body(q_ref, k_ref, v_ref, kr_ref, z_ref, y_ref, o_ref, lse_ref, lrow_ref):
        qi = pl.program_id(2)
        qs = [q_ref[:, e * LANES:(e + 1) * LANES] for e in range(2)]
        row = lax.broadcasted_iota(jnp.int32, (tq, tq), 0)
        col = lax.broadcasted_iota(jnp.int32, (tq, tq), 1)
        tri = col <= row
        lane = lax.broadcasted_iota(jnp.int32, (tq, LANES), 1)
        first = lane < B_VDIM
        ones_at = [jnp.where(lane == B_VDIM, 1.0, 0.0).astype(BF16), jnp.where(lane == 0, 1.0, 0.0).astype(BF16)]
        sum_lane = [B_VDIM, 0]

        def tile(kb, carry, masked):
            k0 = pl.multiple_of(kb * tq, tq)
            krv = kr_ref[pl.ds(k0, tq), :]
            v = v_ref[pl.ds(k0, tq), :]
            vs = [jnp.where(first, v, ones_at[0]), jnp.where(first, ones_at[1], v)]
            ss = []
            for e in range(2):
                k = k_ref[pl.ds(k0, tq), e * LANES:(e + 1) * LANES] + krv
                s = _dot_nt(qs[e], k)
                ss.append(jnp.where(tri, s, NEG) if masked else s)
            ms = [jnp.maximum(carry[e][0], jnp.max(ss[e], axis=-1, keepdims=True)) for e in range(2)]
            ps = [jnp.exp2(ss[e] - ms[e]).astype(BF16) for e in range(2)]
            out = []
            for e in range(2):
                alpha = jnp.exp2(carry[e][0] - ms[e])
                out.append((ms[e], alpha * carry[e][1] + _dot_nn(ps[e], vs[e])))
            return tuple(out)

        one = (jnp.full((tq, 1), NEG, F32), jnp.zeros((tq, LANES), F32))
        carry = lax.fori_loop(0, qi, lambda kb, c: tile(kb, c, False), (one, one))
        carry = tile(qi, carry, True)
        ls = [carry[e][1][:, sum_lane[e]:sum_lane[e] + 1] for e in range(2)]
        outs = [carry[e][1] / ls[e] for e in range(2)]
        lses = [carry[e][0] + jnp.log2(ls[e]) for e in range(2)]
        o = jnp.where(first, outs[0], outs[1])
        zv = z_ref[...].astype(F32)
        o_ref[...] = o.astype(BF16)
        y_ref[...] = (o * (zv * _sigmoid(zv))).astype(BF16)
        lse_ref[...] = jnp.where(first, lses[0], lses[1])
        for e in range(2):
            lrow_ref[e:e + 1, :] = _col_to_row(lses[e], tq)

    blk = pl.BlockSpec((None, tq, LANES), lambda b, j, i: (b, i, j))
    return pl.pallas_call(
        body, name="mla_fwd", grid=(bl, pairs, nq),
        in_specs=[pl.BlockSpec((None, tq, 2 * LANES), lambda b, j, i: (b, i, j)),
                  pl.BlockSpec((None, t, 2 * LANES), lambda b, j, i: (b, 0, j)),
                  pl.BlockSpec((None, t, LANES), lambda b, j, i: (b, 0, v_blk0 + j)),
                  pl.BlockSpec((None, t, LANES), lambda b, j, i: (b, 0, 0)),
                  blk],
        out_specs=(blk, blk, blk, pl.BlockSpec((None, None, None, 2, tq), lambda b, j, i: (b, j, i, 0, 0))),
        out_shape=(jax.ShapeDtypeStruct((bl, t, B_WIDTH), BF16), jax.ShapeDtypeStruct((bl, t, B_WIDTH), BF16),
                   jax.ShapeDtypeStruct((bl, t, B_WIDTH), F32),
                   jax.ShapeDtypeStruct((bl, pairs, nq, 2, tq), F32)),
        compiler_params=_params(("parallel", "parallel", "arbitrary")),
    )(q_cat, kvup, kvup, kr, z)


def _mla_dq(q_cat, kvup, kr, do, o, lse, tabs, tq):
    bl, t, _ = q_cat.shape
    nq = t // tq
    pairs = B_HEADS // 2
    v_blk0 = (B_HEADS * LANES) // LANES

    def body(q_ref, k_ref, v_ref, kr_ref, do_ref, o_ref, lse_ref, tc, tsa, tsb, dq_ref, drow_ref):
        qi = pl.program_id(2)
        dov, ov, lsev = do_ref[...], o_ref[...], lse_ref[...]
        qs = [q_ref[:, e * LANES:(e + 1) * LANES] for e in range(2)]
        terms = [_head_terms(dov, ov, lsev, e) for e in range(2)]
        row = lax.broadcasted_iota(jnp.int32, (tq, tq), 0)
        col = lax.broadcasted_iota(jnp.int32, (tq, tq), 1)
        tri = col <= row

        def tile(kb, carry, masked):
            k0 = pl.multiple_of(kb * tq, tq)
            krv = kr_ref[pl.ds(k0, tq), :]
            v = v_ref[pl.ds(k0, tq), :]
            ks = [k_ref[pl.ds(k0, tq), e * LANES:(e + 1) * LANES] + krv for e in range(2)]
            ss = [_dot_nt(qs[e], ks[e]) for e in range(2)]
            dps = [_dot_nt(terms[e][0], v) for e in range(2)]
            out = []
            for e in range(2):
                s = jnp.where(tri, ss[e], NEG) if masked else ss[e]
                p = jnp.exp2(s - terms[e][2])
                ds = (p * (dps[e] - terms[e][1])).astype(BF16)
                out.append(carry[e] + _dot_nn(ds, ks[e]))
            return tuple(out)

        zero = jnp.zeros((tq, LANES), F32)
        carry = lax.fori_loop(0, qi, lambda kb, c: tile(kb, c, False), (zero, zero))
        carry = tile(qi, carry, True)
        for e in range(2):
            dq_ref[:, e * LANES:(e + 1) * LANES] = _rope_apply(carry[e] * B_SCALE, tc[...], tsa[...], tsb[...], -1).astype(BF16)
            drow_ref[e:e + 1, :] = _col_to_row(terms[e][1], tq)

    blk = pl.BlockSpec((None, tq, LANES), lambda b, j, i: (b, i, j))
    tab = pl.BlockSpec((None, tq, LANES), lambda b, j, i: (b, i, 0))
    qblk = pl.BlockSpec((None, tq, 2 * LANES), lambda b, j, i: (b, i, j))
    return pl.pallas_call(
        body, name="mla_dq", grid=(bl, pairs, nq),
        in_specs=[qblk,
                  pl.BlockSpec((None, t, 2 * LANES), lambda b, j, i: (b, 0, j)),
                  pl.BlockSpec((None, t, LANES), lambda b, j, i: (b, 0, v_blk0 + j)),
                  pl.BlockSpec((None, t, LANES), lambda b, j, i: (b, 0, 0)),
                  blk, blk, blk, tab, tab, tab],
        out_specs=(qblk, pl.BlockSpec((None, None, None, 2, tq), lambda b, j, i: (b, j, i, 0, 0))),
        out_shape=(jax.ShapeDtypeStruct((bl, t, B_HEADS * LANES), BF16),
                   jax.ShapeDtypeStruct((bl, pairs, nq, 2, tq), F32)),
        compiler_params=_params(("parallel", "parallel", "arbitrary")),
    )(q_cat, kvup, kvup, kr, do, o, lse, *tabs)


def _mla_dkv(q_cat, kvup, kr, do, lse_rows, delta_rows, tq):
    bl, t, _ = q_cat.shape
    nq = t // tq
    pairs = B_HEADS // 2
    v_blk0 = (B_HEADS * LANES) // LANES

    def body(q_ref, k_ref, v_ref, kr_ref, do_ref, lrow_ref, drow_ref, dk_ref, dv_ref):
        kb = pl.program_id(2)
        v = v_ref[...]
        krv = kr_ref[...]
        ks = [k_ref[:, e * LANES:(e + 1) * LANES] + krv for e in range(2)]
        krow = lax.broadcasted_iota(jnp.int32, (tq, tq), 0)
        qcol = lax.broadcasted_iota(jnp.int32, (tq, tq), 1)
        tri = krow <= qcol
        lane = lax.broadcasted_iota(jnp.int32, (tq, LANES), 1)
        mine = [lane < B_VDIM, lane >= B_VDIM]

        def tile(qb, carry, masked):
            q0 = pl.multiple_of(qb * tq, tq)
            rows = pl.ds(q0, tq)
            dov = do_ref[rows, :]
            dk0, dk1, dv = carry
            dks = [dk0, dk1]
            qs = [q_ref[rows, e * LANES:(e + 1) * LANES] for e in range(2)]
            does = [jnp.where(mine[e], dov, jnp.zeros_like(dov)) for e in range(2)]
            sts = [_dot_nt(ks[e], qs[e]) for e in range(2)]
            dpts = [_dot_nt(v, does[e]) for e in range(2)]
            for e in range(2):
                st = jnp.where(tri, sts[e], NEG) if masked else sts[e]
                pt = jnp.exp2(st - lrow_ref[qb, e:e + 1, :])
                dv = dv + _dot_nn(pt.astype(BF16), does[e])
                dst = (pt * (dpts[e] - drow_ref[qb, e:e + 1, :])).astype(BF16)
                dks[e] = dks[e] + _dot_nn(dst, qs[e])
            return dks[0], dks[1], dv

        zero = jnp.zeros((tq, LANES), F32)
        carry = tile(kb, (zero, zero, zero), True)
        dk0, dk1, dv = lax.fori_loop(kb + 1, nq, lambda qb, c: tile(qb, c, False), carry)
        dk_ref[:, 0:LANES] = (dk0 * LN2).astype(BF16)
        dk_ref[:, LANES:2 * LANES] = (dk1 * LN2).astype(BF16)
        dv_ref[...] = dv.astype(BF16)

    full = pl.BlockSpec((None, t, LANES), lambda b, j, i: (b, 0, j))
    rows = pl.BlockSpec((None, None, nq, 2, tq), lambda b, j, i: (b, j, 0, 0, 0))
    kblk = pl.BlockSpec((None, tq, 2 * LANES), lambda b, j, i: (b, i, j))
    return pl.pallas_call(
        body, name="mla_dkv", grid=(bl, pairs, nq),
        in_specs=[pl.BlockSpec((None, t, 2 * LANES), lambda b, j, i: (b, 0, j)),
                  kblk,
                  pl.BlockSpec((None, tq, LANES), lambda b, j, i: (b, i, v_blk0 + j)),
                  pl.BlockSpec((None, tq, LANES), lambda b, j, i: (b, i, 0)),
                  full, rows, rows],
        out_specs=(kblk, pl.BlockSpec((None, tq, LANES), lambda b, j, i: (b, i, j))),
        out_shape=(jax.ShapeDtypeStruct((bl, t, B_HEADS * LANES), BF16),
                   jax.ShapeDtypeStruct((bl, t, B_WIDTH), BF16)),
        compiler_params=_params(("parallel", "parallel", "arbitrary")),
    )(q_cat, kvup, kvup, kr, do, lse_rows, delta_rows)


def _adamw(w, g, m, v, *, name):
    r, c = w.shape
    tr = _row_tile(r, 256)
    c1 = 1.0 - ADAM_B1
    c2 = 1.0 - ADAM_B2
    bc1 = 1.0 - ADAM_B1 ** ADAM_STEP
    bc2 = 1.0 - ADAM_B2 ** ADAM_STEP

    def body(w_ref, g_ref, m_ref, v_ref, d_ref, nm_ref, nv_ref):
        gv = g_ref[...]
        nm = ADAM_B1 * m_ref[...] + c1 * gv
        nv = ADAM_B2 * v_ref[...] + c2 * (gv * gv)
        nm_ref[...] = nm
        nv_ref[...] = nv
        d_ref[...] = -ADAM_LR * ((nm / bc1) / (jnp.sqrt(nv / bc2) + ADAM_EPS) + ADAM_WD * w_ref[...])

    blk = pl.BlockSpec((tr, c), lambda i: (i, 0))
    sds = jax.ShapeDtypeStruct((r, c), F32)
    return pl.pallas_call(
        body, name=name, grid=(r // tr,), in_specs=[blk] * 4, out_specs=(blk,) * 3,
        out_shape=(sds,) * 3, compiler_params=_params(("parallel",)),
    )(w, g, m, v)


def _add_my_half(stacked, other, core, out_dtype, *, name):
    nch, a, c = stacked.shape
    h = a // 2
    tr = _row_tile(h, 256)
    nblk = h // tr

    def body(core_ref, s_ref, p_ref, o_ref):
        o_ref[...] = (s_ref[...] + p_ref[...]).astype(o_ref.dtype)

    return pl.pallas_call(
        body, name=name,
        grid_spec=pltpu.PrefetchScalarGridSpec(
            num_scalar_prefetch=1, grid=(nch, nblk),
            in_specs=[pl.BlockSpec((None, tr, c), lambda k, i, cr: (k, cr[0] * nblk + i, 0)),
                      pl.BlockSpec((None, tr, c), lambda k, i, cr: (k, i, 0))],
            out_specs=pl.BlockSpec((None, tr, c), lambda k, i, cr: (k, i, 0))),
        out_shape=jax.ShapeDtypeStruct((nch, h, c), out_dtype),
        compiler_params=_params(("parallel", "parallel")),
    )(core, stacked, other)


def _sum_chips(parts, own, chip, *, name):
    nch, h, c = parts.shape
    tr = _row_tile(h, 256)

    def body(chip_ref, p_ref, own_ref, o_ref):
        me = chip_ref[0]

        def slot(k):
            return jnp.where(me == k, own_ref[k].astype(F32), p_ref[k].astype(F32))

        acc = slot(0) + slot(1)
        for k in range(2, nch):
            acc = acc + slot(k)
        o_ref[...] = acc

    blk = pl.BlockSpec((nch, tr, c), lambda i, cr: (0, i, 0))
    return pl.pallas_call(
        body, name=name,
        grid_spec=pltpu.PrefetchScalarGridSpec(
            num_scalar_prefetch=1, grid=(h // tr,), in_specs=[blk, blk],
            out_specs=pl.BlockSpec((tr, c), lambda i, cr: (i, 0))),
        out_shape=jax.ShapeDtypeStruct((h, c), F32), compiler_params=_params(("parallel",)),
    )(chip, parts, own)


def _join_halves(mine, other, core, *, name):
    h, c = mine.shape
    tr = _row_tile(h, 256)
    nblk = h // tr

    def body(core_ref, m_ref, s_ref, o_ref):
        is_mine = pl.program_id(0) // nblk == core_ref[0]

        @pl.when(is_mine)
        def _():
            o_ref[...] = m_ref[...]

        @pl.when(jnp.logical_not(is_mine))
        def _():
            o_ref[...] = s_ref[...]

    blk = pl.BlockSpec((tr, c), lambda i, cr: (i % nblk, 0))
    return pl.pallas_call(
        body, name=name,
        grid_spec=pltpu.PrefetchScalarGridSpec(
            num_scalar_prefetch=1, grid=(2 * nblk,), in_specs=[blk, blk],
            out_specs=pl.BlockSpec((tr, c), lambda i, cr: (i, 0))),
        out_shape=jax.ShapeDtypeStruct((2 * h, c), F32), compiler_params=_params(("arbitrary",)),
    )(core, mine, other)


def _place():
    x, y, c = lax.axis_index("x"), lax.axis_index("y"), lax.axis_index("c")
    chips = [(1 - x, y), (x, 1 - y), (1 - x, 1 - y)]
    return x, y, c, chips


def _remote(src, dst, send_sems, recv_sems, k, to):
    return pltpu.make_async_remote_copy(src_ref=src, dst_ref=dst, send_sem=send_sems.at[k],
                                        recv_sem=recv_sems.at[k], device_id=to, device_id_type=MESH)


def _hbm_call(body, name, ins, out_shapes, n_remote):
    any_spec = pl.BlockSpec(memory_space=pl.ANY)
    return pl.pallas_call(
        body, name=name, in_specs=[any_spec] * len(ins), out_specs=tuple([any_spec] * len(out_shapes)),
        out_shape=tuple(out_shapes),
        scratch_shapes=[pltpu.SemaphoreType.DMA((n_remote,)), pltpu.SemaphoreType.DMA((n_remote,))],
    )(*ins)


def _all_gather_chips(shards, *, name):
    n = len(shards)

    def body(*refs):
        ins, outs = refs[:n], refs[n:2 * n]
        send_sems, recv_sems = refs[2 * n:]
        x, y, c, chips = _place()
        me = 2 * x + y
        sent = []
        for s in range(n):
            h = ins[s].shape[0] // 2
            for j, (px, py) in enumerate(chips):
                cp = _remote(ins[s].at[pl.ds(c * h, h)], outs[s].at[me, pl.ds(c * h, h)],
                             send_sems, recv_sems, s * 6 + j, (px, py, c))
                cp.start()
                sent.append(cp)
        for s in range(n):
            h = ins[s].shape[0] // 2
            for j, (px, py) in enumerate(chips):
                slab = outs[s].at[2 * px + py, pl.ds(c * h, h)]
                _remote(slab, slab, send_sems, recv_sems, s * 6 + j, (px, py, c)).wait_recv()
                cp = _remote(slab, slab, send_sems, recv_sems, s * 6 + 3 + j, (x, y, 1 - c))
                cp.start()
                sent.append(cp)
        for s in range(n):
            h = ins[s].shape[0] // 2
            for j, (px, py) in enumerate(chips):
                slab = outs[s].at[2 * px + py, pl.ds((1 - c) * h, h)]
                _remote(slab, slab, send_sems, recv_sems, s * 6 + 3 + j, (x, y, 1 - c)).wait_recv()
        for cp in sent:
            cp.wait_send()

    out_shapes = [jax.ShapeDtypeStruct((N_CHIPS,) + s.shape, s.dtype) for s in shards]
    return _hbm_call(body, name, shards, out_shapes, 6 * n)


def _pair_send_other_half(stacked, *, name):
    n = len(stacked)

    def body(*refs):
        ins, outs = refs[:n], refs[n:2 * n]
        send_sems, recv_sems = refs[2 * n:]
        x, y, c, _chips = _place()
        sent = []
        for s in range(n):
            h = ins[s].shape[1] // 2
            cp = _remote(ins[s].at[:, pl.ds((1 - c) * h, h)], outs[s], send_sems, recv_sems, s, (x, y, 1 - c))
            cp.start()
            sent.append(cp)
        for cp in sent:
            cp.wait_recv()
        for cp in sent:
            cp.wait_send()

    out_shapes = [jax.ShapeDtypeStruct((s.shape[0], s.shape[1] // 2, s.shape[2]), s.dtype) for s in stacked]
    return _hbm_call(body, name, stacked, out_shapes, n)


def _chip_exchange(halves, *, name):
    n = len(halves)

    def body(*refs):
        ins, outs = refs[:n], refs[n:2 * n]
        send_sems, recv_sems = refs[2 * n:]
        x, y, c, chips = _place()
        me = 2 * x + y
        sent = []
        for s in range(n):
            for j, (px, py) in enumerate(chips):
                cp = _remote(ins[s].at[2 * px + py], outs[s].at[me], send_sems, recv_sems, s * 3 + j, (px, py, c))
                cp.start()
                sent.append(cp)
        for s in range(n):
            for j, (px, py) in enumerate(chips):
                slab = outs[s].at[2 * px + py]
                _remote(slab, slab, send_sems, recv_sems, s * 3 + j, (px, py, c)).wait_recv()
        for cp in sent:
            cp.wait_send()

    out_shapes = [jax.ShapeDtypeStruct(s.shape, s.dtype) for s in halves]
    return _hbm_call(body, name, halves, out_shapes, 3 * n)


def _pair_swap(halves, *, name):
    n = len(halves)

    def body(*refs):
        ins, outs = refs[:n], refs[n:2 * n]
        send_sems, recv_sems = refs[2 * n:]
        x, y, c, _chips = _place()
        sent = []
        for s in range(n):
            cp = _remote(ins[s], outs[s], send_sems, recv_sems, s, (x, y, 1 - c))
            cp.start()
            sent.append(cp)
        for cp in sent:
            cp.wait_recv()
        for cp in sent:
            cp.wait_send()

    out_shapes = [jax.ShapeDtypeStruct(s.shape, s.dtype) for s in halves]
    return _hbm_call(body, name, halves, out_shapes, n)


def _pack_rows(parts, row_multiple):
    flat = jnp.concatenate([p.reshape(-1) for p in parts])
    quantum = row_multiple * PACK_COLS
    pad = (-flat.shape[0]) % quantum
    flat = jnp.pad(flat, (0, pad))
    return flat.reshape(-1, PACK_COLS)


def _unpack(flat, shapes):
    out, pos = [], 0
    for shp in shapes:
        size = math.prod(shp)
        out.append(flat[pos:pos + size].reshape(shp))
        pos += size
    return out


def _to_chunks_cols(full):
    r, c4 = full.shape
    return full.reshape(r, N_CHIPS, c4 // N_CHIPS).transpose(1, 0, 2)


def _from_chunks_cols(stacked):
    nch, r, c = stacked.shape
    return stacked.transpose(1, 0, 2).reshape(r, nch * c)


def _class_major(a, bl, t, dil):
    w = a.shape[-1]
    if dil == 1:
        return a.reshape(bl, 1, t, w)
    return a.reshape(bl, t // dil, dil, w).transpose(0, 2, 1, 3)


def _natural(a):
    bl, dil, ln, w = a.shape
    if dil == 1:
        return a.reshape(bl * ln, w)
    return a.transpose(0, 2, 1, 3).reshape(bl * ln * dil, w)


def _train_step(x, positions, a_pre_norm, a_w_in, a_w_out, a_post_norm, kv_norm, kv_w_down, kv_latent_norm,
                kv_w_up, b_pre_norm, b_w_in, b_q_norm, b_w_q_up, b_w_out, b_post_norm, loss_target, moments):
    bl, t, d = x.shape
    n = bl * t
    qb = t // A_DILATIONS[-1]
    tq = _tile(t, 256)
    dq4 = d // N_CHIPS
    chip = 2 * lax.axis_index("x") + lax.axis_index("y")
    chip_arr = chip.astype(jnp.int32).reshape(1)
    core_arr = lax.axis_index("c").astype(jnp.int32).reshape(1)

    w_in_a_s = a_w_in[0].astype(BF16)
    outs_s = jnp.concatenate([a_w_out[0], b_w_out[0]], axis=0).astype(BF16)
    small_shapes = [kv_w_down.shape, kv_w_up.shape, b_w_in[0].shape, b_w_q_up[0].shape]
    small_s = _pack_rows([kv_w_down, kv_w_up, b_w_in[0], b_w_q_up[0]], 32).astype(BF16)
    gains_s = jnp.pad(jnp.concatenate([a_pre_norm[0], a_post_norm[0]]), (0, 16 * LANES - 2 * dq4)).reshape(16, LANES)
    shards = [w_in_a_s, outs_s, small_s, gains_s]
    gathered = _all_gather_chips(shards, name="gather_weights")
    g_in_a, g_outs, g_small, g_gains = [lax.dynamic_update_index_in_dim(g, s, chip, 0)
                                        for g, s in zip(gathered, shards)]

    w_in_a = _from_chunks_cols(g_in_a)
    w_out_a = g_outs[:, :A_WIDTH // N_CHIPS].reshape(A_WIDTH, d)
    w_out_b = g_outs[:, A_WIDTH // N_CHIPS:].reshape(B_WIDTH, d)
    sm = [_unpack(g_small[k].reshape(-1), small_shapes) for k in range(N_CHIPS)]
    w_down = jnp.concatenate([sm[k][0] for k in range(N_CHIPS)], axis=0)
    w_up = jnp.concatenate([sm[k][1] for k in range(N_CHIPS)], axis=1)
    w_in_b = jnp.concatenate([sm[k][2] for k in range(N_CHIPS)], axis=1)
    w_q_up = jnp.concatenate([sm[k][3] for k in range(N_CHIPS)], axis=1)
    gflat = g_gains.reshape(N_CHIPS, -1)
    g_a_pre = gflat[:, :dq4].reshape(1, d)
    g_a_post = gflat[:, dq4:2 * dq4].reshape(1, d)

    w_up_h = w_up.reshape(B_KV_LORA, B_HEADS, B_NOPE + B_VDIM)
    w_up_k = jnp.pad(w_up_h[:, :, :B_NOPE], ((0, 0), (0, 0), (0, LANES - B_NOPE))).reshape(B_KV_LORA, B_HEADS * LANES)
    w_up_v = w_up_h[:, :, B_NOPE:].reshape(B_KV_LORA, B_WIDTH)
    w_up_cat = jnp.concatenate([w_up_k, w_up_v], axis=1)
    w_q_up_p = jnp.pad(w_q_up.reshape(B_Q_LORA, B_HEADS, B_QK_DIM),
                       ((0, 0), (0, 0), (0, LANES - B_QK_DIM))).reshape(B_Q_LORA, B_HEADS * LANES)
    zeros_d = lambda c: jnp.zeros((d, c), BF16)
    w_down_p = jnp.concatenate([w_down[:, :B_KV_LORA], zeros_d(B_NOPE), w_down[:, B_KV_LORA:],
                                zeros_d(LANES - B_NOPE - B_ROPE)], axis=1)
    w_cq = w_in_b[:, :B_Q_LORA]
    w_z = w_in_b[:, B_Q_LORA:]

    tabs_a = _rope_tables(positions, A_ROPE_THETA, 0)
    tabs_b = _rope_tables(positions, B_ROPE_THETA, B_NOPE)

    h0 = x.reshape(n, d)
    hn_a = _rms_fwd(h0, g_a_pre, BF16, name="a_pre_norm")
    is_qk = lambda j: jnp.logical_and(j < 3 * A_GROUPS, j % 3 != 2)
    is_q = lambda j: jnp.logical_and(j < 3 * A_GROUPS, j % 3 == 0)
    proj_a = _matmul(hn_a, w_in_a, "nn", BF16, name="a_proj", rope=(tabs_a, is_qk),
                     out_scale=(A_SCALE * LOG2E, is_q))
    z_blk_a = 3 * A_GROUPS
    o_groups, lse_groups, qkv_cm = [], [], []
    for g, dil in enumerate(A_DILATIONS):
        if dil == 1:
            src, cb0 = proj_a.reshape(bl, 1, t, A_IN_WIDTH), 3 * g
        else:
            src, cb0 = _class_major(proj_a[:, 3 * g * A_WIDTH:3 * (g + 1) * A_WIDTH], bl, t, dil), 0
        qkv_cm.append((src, cb0))
        o_g, lse_g = _attn_a_fwd(src, cb0, qb, name=f"attn_a_fwd_{g}")
        o_groups.append(_natural(o_g))
        lse_groups.append(_natural(lse_g))
    ypre_a, om_a, lse_a = _merge_gate_fwd(o_groups, lse_groups, proj_a, z_blk_a)
    y_a = _matmul(ypre_a, w_out_a, "nn", F32, name="a_out")
    h1 = _rms_fwd(y_a, g_a_post, F32, name="a_post_norm", add=h0)

    g_kvn = kv_norm.reshape(1, d)
    g_lat = kv_latent_norm.reshape(1, B_KV_LORA)
    hn_kv = _rms_fwd(h1, g_kvn, BF16, name="kv_norm")
    ckr = _matmul(hn_kv, w_down_p, "nn", F32, name="kv_down")
    c_kv, k_rope = _kv_latent_fwd(ckr, g_lat, tabs_b)
    kvup = _matmul(c_kv, w_up_cat, "nn", BF16, name="kv_up")
    hn_b = _rms_fwd(h1, b_pre_norm, BF16, name="b_pre_norm")
    z_b = _matmul(hn_b, w_z, "nn", BF16, name="b_proj_z")
    cq_raw = _matmul(hn_b, w_cq, "nn", F32, name="b_proj_q")
    c_q = _rms_fwd(cq_raw, b_q_norm, BF16, name="b_q_norm")
    always = lambda j: True
    q_cat = _matmul(c_q, w_q_up_p, "nn", BF16, name="b_q_up", rope=(tabs_b, always),
                    out_scale=(B_SCALE * LOG2E, always))
    r3 = lambda a: a.reshape(bl, t, a.shape[-1])
    tabs_b3 = tuple(r3(tb) for tb in tabs_b)
    ypre_b, o_b, lse_b, lse_rows_b = _mla_fwd(r3(q_cat), r3(kvup), r3(k_rope), r3(z_b), tq)
    y_b = _matmul(ypre_b.reshape(n, B_WIDTH), w_out_b, "nn", F32, name="b_out")
    h2 = _rms_fwd(y_b, b_post_norm, F32, name="b_post_norm", add=h1)
    dh2, loss_part = _loss_fwd_bwd(h2, loss_target.reshape(n, d))

    dy_b, dg_b_post = _rms_bwd(y_b, b_post_norm, dh2, BF16, name="b_post_norm_bwd")
    dypre_b = _matmul(dy_b, w_out_b, "nt", F32, name="b_out_dx")
    dw_out_b = _matmul(ypre_b.reshape(n, B_WIDTH), dy_b, "tn", F32, name="b_out_dw", tm=1024, tk=512)
    do_b, dz_b = _gate_bwd(dypre_b, o_b.reshape(n, B_WIDTH), z_b, 0, name="b_gate_bwd", with_delta=False)
    dq_cat, delta_rows_b = _mla_dq(r3(q_cat), r3(kvup), r3(k_rope), r3(do_b), o_b, lse_b, tabs_b3, tq)
    dq_cat = dq_cat.reshape(n, -1)
    dk_cat, dv_b = _mla_dkv(r3(q_cat), r3(kvup), r3(k_rope), r3(do_b), lse_rows_b, delta_rows_b, tq)
    dk_cat, dv_b = dk_cat.reshape(n, -1), dv_b.reshape(n, -1)
    dcq_n = _matmul(dq_cat, w_q_up_p, "nt", F32, name="b_q_up_dx")
    dw_q_up_p = _matmul(c_q, dq_cat, "tn", F32, name="b_q_up_dw", tm=1024, tk=512)
    dcq, dg_b_q = _rms_bwd(cq_raw, b_q_norm, dcq_n, BF16, name="b_q_norm_bwd")
    dhn_b = _matmul(dz_b, w_z, "nt", F32, name="b_proj_z_dx")
    dhn_b = _matmul(dcq, w_cq, "nt", F32, name="b_proj_q_dx", add=dhn_b)
    dw_z = _matmul(hn_b, dz_b, "tn", F32, name="b_proj_z_dw", tm=1024, tk=512)
    dw_cq = _matmul(hn_b, dcq, "tn", F32, name="b_proj_q_dw", tm=1024, tk=512)
    dh1, dg_b_pre = _rms_bwd(h1, b_pre_norm, dhn_b, F32, name="b_pre_norm_bwd", adds=(dh2,))
    dckv_n = _matmul(dk_cat, w_up_k, "nt", F32, name="kv_up_k_dx")
    dckv_n = _matmul(dv_b, w_up_v, "nt", F32, name="kv_up_v_dx", add=dckv_n)
    dw_up_k = _matmul(c_kv, dk_cat, "tn", F32, name="kv_up_k_dw", tm=1024, tk=512)
    dw_up_v = _matmul(c_kv, dv_b, "tn", F32, name="kv_up_v_dw", tm=1024, tk=512)
    dckr, dg_lat = _kv_latent_bwd(dckv_n, ckr, g_lat, dk_cat, tabs_b)
    dhn_kv = _matmul(dckr, w_down_p, "nt", F32, name="kv_down_dx")
    dw_down_p = _matmul(hn_kv, dckr, "tn", F32, name="kv_down_dw", tm=1024, tk=512)
    dh1, dg_kvn = _rms_bwd(h1, g_kvn, dhn_kv, F32, name="kv_norm_bwd", adds=(dh1,))

    dy_a, dg_a_post = _rms_bwd(y_a, g_a_post, dh1, BF16, name="a_post_norm_bwd")
    dypre_a = _matmul(dy_a, w_out_a, "nt", F32, name="a_out_dx")
    dw_out_a = _matmul(ypre_a, dy_a, "tn", F32, name="a_out_dw", tm=1024, tk=512)
    do_a, dz_a, delta_a = _gate_bwd(dypre_a, om_a, proj_a, z_blk_a, name="a_gate_bwd", with_delta=True)
    dw_cols = A_IN_WIDTH // N_CHIPS
    dw_tn = _tile(dw_cols, 512)
    dw_kwargs = dict(tm=1024, tn=dw_tn, tk=512, out_chunk_blocks=dw_cols // dw_tn)
    r_big = None
    dhn_a = None
    for g, dil in enumerate(A_DILATIONS):
        src, cb0 = qkv_cm[g]
        cm = lambda a: _class_major(a, bl, t, dil)
        swap = lambda a: jnp.swapaxes(a, 2, 3)
        lse_cm, delta_cm = cm(lse_a), cm(delta_a)
        dqkv = _attn_a_bwd(src, cb0, cm(do_a), lse_cm, delta_cm, swap(lse_cm), swap(delta_cm),
                           tuple(cm(tb) for tb in tabs_a), qb, name=f"attn_a_bwd_{g}")
        dqkv = _natural(dqkv)
        dhn_a = _matmul(dqkv, w_in_a, "nt", F32, name=f"a_proj_dx_{g}", add=dhn_a,
                        b_koff=3 * g * A_WIDTH // _tile(3 * A_WIDTH, 1024))
        first = dict(out_full=(N_CHIPS, d, dw_cols)) if r_big is None else dict(out_into=r_big)
        r_big = _matmul(hn_a, dqkv, "tn", F32, name=f"a_proj_dw_{g}", out_joff=3 * g * A_WIDTH // dw_tn,
                        **first, **dw_kwargs)
    dhn_a = _matmul(dz_a, w_in_a, "nt", F32, name="a_proj_dx_z", add=dhn_a, b_koff=z_blk_a)
    r_big = _matmul(hn_a, dz_a, "tn", F32, name="a_proj_dw_z", out_into=r_big,
                    out_joff=z_blk_a * A_WIDTH // dw_tn, **dw_kwargs)
    grad_x, dg_a_pre = _rms_bwd(h0, g_a_pre, dhn_a, F32, name="a_pre_norm_bwd", adds=(dh1,))

    dw_up = jnp.concatenate([dw_up_k.reshape(B_KV_LORA, B_HEADS, LANES)[:, :, :B_NOPE],
                             dw_up_v.reshape(B_KV_LORA, B_HEADS, B_VDIM)], axis=2).reshape(B_KV_LORA, -1)
    dw_q_up = dw_q_up_p.reshape(B_Q_LORA, B_HEADS, LANES)[:, :, :B_QK_DIM].reshape(B_Q_LORA, -1)
    dw_down = jnp.concatenate([dw_down_p[:, :B_KV_LORA], dw_down_p[:, B_KV_LORA + B_NOPE:B_KV_LORA + B_NOPE + B_ROPE]], axis=1)
    dw_in_b = jnp.concatenate([dw_cq, dw_z], axis=1)
    vec_rep = [dg_kvn.reshape(-1), dg_lat.reshape(-1), dg_b_pre.reshape(-1), dg_b_q.reshape(-1),
               dg_b_post.reshape(-1), loss_part.reshape(-1)]
    vec_shapes = [(dq4,), (dq4,)] + [v.shape for v in vec_rep]
    r_outs = jnp.concatenate([dw_out_a.reshape(N_CHIPS, A_WIDTH // N_CHIPS, d),
                              dw_out_b.reshape(N_CHIPS, B_WIDTH // N_CHIPS, d)], axis=1)
    down_c = dw_down.reshape(N_CHIPS, dq4, -1)
    up_c = _to_chunks_cols(dw_up)
    inb_c = _to_chunks_cols(dw_in_b)
    qup_c = _to_chunks_cols(dw_q_up)
    small_chunks = []
    for k in range(N_CHIPS):
        vecs = [dg_a_pre.reshape(-1)[k * dq4:(k + 1) * dq4], dg_a_post.reshape(-1)[k * dq4:(k + 1) * dq4]] + vec_rep
        small_chunks.append(_pack_rows([down_c[k], up_c[k], inb_c[k], qup_c[k]] + vecs, 32))
    r_small = jnp.stack(small_chunks)

    stacked = [r_big, r_outs, r_small]
    payload = [BF16, BF16, F32]
    recv = _pair_send_other_half(stacked, name="reduce_pair_send")
    halves = [_add_my_half(s, p, core_arr, dt, name=f"reduce_pair_add_{i}")
              for i, (s, p, dt) in enumerate(zip(stacked, recv, payload))]
    parts = _chip_exchange(halves, name="reduce_chip_exchange")
    sums = [_sum_chips(p, own, chip_arr, name=f"reduce_chip_sum_{i}") for i, (p, own) in enumerate(zip(parts, halves))]
    others = _pair_swap(sums, name="reduce_pair_swap")
    g_big, g_outs_r, g_small_r = [_join_halves(m, o, core_arr, name=f"reduce_join_{i}")
                                  for i, (m, o) in enumerate(zip(sums, others))]

    grads = {}
    grads["a_w_in"] = g_big
    grads["a_w_out"] = g_outs_r[:A_WIDTH // N_CHIPS]
    grads["b_w_out"] = g_outs_r[A_WIDTH // N_CHIPS:]
    small_out_shapes = [down_c.shape[1:], up_c.shape[1:], inb_c.shape[1:], qup_c.shape[1:]] + vec_shapes
    (grads["kv_w_down"], grads["kv_w_up"], grads["b_w_in"], grads["b_w_q_up"], grads["a_pre_norm"],
     grads["a_post_norm"], grads["kv_norm"], grads["kv_latent_norm"], grads["b_pre_norm"], grads["b_q_norm"],
     grads["b_post_norm"], loss_sum) = _unpack(g_small_r.reshape(-1), small_out_shapes)

    weights = dict(a_pre_norm=a_pre_norm, a_w_in=a_w_in, a_w_out=a_w_out, a_post_norm=a_post_norm, kv_norm=kv_norm,
                   kv_w_down=kv_w_down, kv_latent_norm=kv_latent_norm, kv_w_up=kv_w_up, b_pre_norm=b_pre_norm,
                   b_w_in=b_w_in, b_q_norm=b_q_norm, b_w_q_up=b_w_q_up, b_w_out=b_w_out, b_post_norm=b_post_norm)
    names = list(weights)
    out_g, out_d, out_m, out_v = [], [], [], []
    for i, nm in enumerate(names):
        w = weights[nm]
        two_d = (1, w.shape[0]) if w.ndim == 1 else (w.shape[-2], w.shape[-1])
        gw = grads[nm].reshape(two_d)
        dlt, new_m, new_v = _adamw(w.reshape(two_d), gw, moments[i].reshape(two_d),
                                   moments[len(names) + i].reshape(two_d), name=f"adamw_{nm}")
        out_g.append(gw.reshape(w.shape))
        out_d.append(dlt.reshape(w.shape))
        out_m.append(new_m.reshape(w.shape))
        out_v.append(new_v.reshape(w.shape))
    return (loss_sum.reshape(()), grad_x.reshape(bl, t, d), *out_g, *out_d, *out_m, *out_v)


def kernel(x, positions, a_pre_norm, a_w_in, a_w_out, a_post_norm, kv_norm, kv_w_down, kv_latent_norm, kv_w_up, b_pre_norm, b_w_in, b_q_norm, b_w_q_up, b_w_out, b_post_norm, loss_target, m_a_pre_norm, m_a_w_in, m_a_w_out, m_a_post_norm, m_kv_norm, m_kv_w_down, m_kv_latent_norm, m_kv_w_up, m_b_pre_norm, m_b_w_in, m_b_q_norm, m_b_w_q_up, m_b_w_out, m_b_post_norm, v_a_pre_norm, v_a_w_in, v_a_w_out, v_a_post_norm, v_kv_norm, v_kv_w_down, v_kv_latent_norm, v_kv_w_up, v_b_pre_norm, v_b_w_in, v_b_q_norm, v_b_w_q_up, v_b_w_out, v_b_post_norm):
    moments = (m_a_pre_norm, m_a_w_in, m_a_w_out, m_a_post_norm, m_kv_norm, m_kv_w_down, m_kv_latent_norm, m_kv_w_up,
               m_b_pre_norm, m_b_w_in, m_b_q_norm, m_b_w_q_up, m_b_w_out, m_b_post_norm,
               v_a_pre_norm, v_a_w_in, v_a_w_out, v_a_post_norm, v_kv_norm, v_kv_w_down, v_kv_latent_norm, v_kv_w_up,
               v_b_pre_norm, v_b_w_in, v_b_q_norm, v_b_w_q_up, v_b_w_out, v_b_post_norm)
    return _train_step(x, positions, a_pre_norm, a_w_in, a_w_out, a_post_norm, kv_norm, kv_w_down, kv_latent_norm,
                       kv_w_up, b_pre_norm, b_w_in, b_q_norm, b_w_q_up, b_w_out, b_post_norm, loss_target, moments)
```

```python
import math

import jax
import jax.numpy as jnp
from jax import lax
from jax.experimental import pallas as pl
from jax.experimental.pallas import tpu as pltpu

F32 = jnp.float32
BF16 = jnp.bfloat16
MESH = pl.DeviceIdType.MESH

NORM_EPS = 1e-6
NEG = -1e30
LANES = 128
VMEM_LIMIT = 56 * 1024 * 1024
LOG2E = math.log2(math.e)
LN2 = math.log(2.0)

A_GROUPS = 3
A_DILATIONS = (1, 4, 16)
A_HEADS = 8
A_HEAD_DIM = 128
A_WIDTH = A_HEADS * A_HEAD_DIM
A_ROPE_THETA = 500000.0
A_IN_WIDTH = A_GROUPS * 3 * A_WIDTH + A_WIDTH
A_SCALE = A_HEAD_DIM ** -0.5

B_HEADS = 16
B_NOPE = 64
B_ROPE = 32
B_QK_DIM = B_NOPE + B_ROPE
B_VDIM = 64
B_WIDTH = B_HEADS * B_VDIM
B_Q_LORA = 384
B_KV_LORA = 256
B_ROPE_THETA = 10000.0
B_SCALE = B_QK_DIM ** -0.5

ADAM_LR = 0.001
ADAM_B1 = 0.9
ADAM_B2 = 0.999
ADAM_EPS = 1e-08
ADAM_WD = 0.01
ADAM_STEP = 10

N_CHIPS = 4
PACK_COLS = 512


def _params(sem=None):
    return pltpu.CompilerParams(dimension_semantics=sem, vmem_limit_bytes=VMEM_LIMIT)


def _tile(n, want):
    t = min(n, want)
    assert n % t == 0, (n, want)
    return t


def _row_tile(n, want):
    for t in range(min(n, want), 0, -1):
        if n % t == 0 and (t % 16 == 0 or t == n):
            return t
    return n


def _rope_tables(positions, theta, lane0):
    half = 16
    inv_freq = 1.0 / (theta ** (jnp.arange(half, dtype=F32) * (2.0 / (2 * half))))
    n = positions.size
    per_row = LANES // half
    pos = jnp.repeat(positions.astype(F32).reshape(n // per_row, per_row), half, axis=1)
    ang = pos * jnp.tile(inv_freq, per_row)
    cos, sin = jnp.cos(ang).reshape(n, half), jnp.sin(ang).reshape(n, half)
    pre = jnp.zeros((n, lane0), F32)
    post = jnp.zeros((n, LANES - lane0 - 2 * half), F32)
    z16 = jnp.zeros((n, half), F32)
    c = jnp.concatenate([pre + 1.0, cos, cos, post + 1.0], axis=1)
    sa = jnp.concatenate([pre, -sin, z16, post], axis=1)
    sb = jnp.concatenate([pre, z16, sin, post], axis=1)
    return c, sa, sb


def _rope_apply(x, c, sa, sb, sign):
    k = x.shape[1] // LANES
    if k > 1:
        c, sa, sb = (jnp.concatenate([t] * k, axis=1) for t in (c, sa, sb))
    w = x.shape[1]
    up = pltpu.roll(x, w - 16, 1)
    dn = pltpu.roll(x, 16, 1)
    if sign > 0:
        return x * c + up * sa + dn * sb
    return x * c - up * sa - dn * sb


def _matmul(a, b, mode, out_dtype, *, name, tm=512, tn=1024, tk=1024, add=None, rope=None,
            out_scale=None, b_koff=0, out_into=None, out_full=None, out_joff=0, out_chunk_blocks=None):
    if mode == "nn":
        m, k = a.shape
        n = b.shape[1]
    elif mode == "nt":
        m, k = a.shape
        n = b.shape[0]
    else:
        k, m = a.shape
        n = b.shape[1]
    tm, tn, tk = _tile(m, tm), _tile(n, tn), _tile(k, tk)
    nk = k // tk
    if mode == "nn":
        a_spec = pl.BlockSpec((tm, tk), lambda j, i, kk: (i, kk))
        b_spec = pl.BlockSpec((tk, tn), lambda j, i, kk: (kk, j))
        dims = (((1,), (0,)), ((), ()))
    elif mode == "nt":
        a_spec = pl.BlockSpec((tm, tk), lambda j, i, kk: (i, kk))
        b_spec = pl.BlockSpec((tn, tk), lambda j, i, kk: (j, kk + b_koff))
        dims = (((1,), (1,)), ((), ()))
    else:
        a_spec = pl.BlockSpec((tk, tm), lambda j, i, kk: (kk, i))
        b_spec = pl.BlockSpec((tk, tn), lambda j, i, kk: (kk, j))
        dims = (((0,), (0,)), ((), ()))
    operands = [a, b]
    in_specs = [a_spec, b_spec]
    if add is not None:
        operands.append(add)
        in_specs.append(pl.BlockSpec((tm, tn), lambda j, i, kk: (i, j)))
    if rope is not None:
        tables, rope_pred = rope
        for t in tables:
            operands.append(t)
            in_specs.append(pl.BlockSpec((tm, LANES), lambda j, i, kk: (i, 0)))
    aliases = {}
    if out_into is not None:
        aliases = {len(operands): 0}
        operands.append(out_into)
        in_specs.append(pl.BlockSpec(memory_space=pl.ANY))
        out_shape = jax.ShapeDtypeStruct(out_into.shape, out_into.dtype)
    elif out_full is not None:
        out_shape = jax.ShapeDtypeStruct(out_full, out_dtype)
    else:
        out_shape = jax.ShapeDtypeStruct((m, n), out_dtype)
    if out_chunk_blocks is not None:
        out_spec = pl.BlockSpec((None, tm, tn), lambda j, i, kk: ((j + out_joff) // out_chunk_blocks, i,
                                                                  (j + out_joff) % out_chunk_blocks))
    else:
        out_spec = pl.BlockSpec((tm, tn), lambda j, i, kk: (i, j + out_joff))

    def body(*refs):
        a_ref, b_ref = refs[0], refs[1]
        pos = 2
        add_ref = None
        if add is not None:
            add_ref = refs[pos]
            pos += 1
        tab_refs = None
        if rope is not None:
            tab_refs = refs[pos:pos + 3]
            pos += 3
        if out_into is not None:
            pos += 1
        o_ref = refs[pos]
        acc_ref = refs[pos + 1] if nk > 1 else None

        def finish(res):
            if add_ref is not None:
                res = res + add_ref[...].astype(F32)
            if tab_refs is None:
                o_ref[...] = res.astype(o_ref.dtype)
                return
            j = pl.program_id(0)
            flag = rope_pred(j)
            roped = _rope_apply(res, tab_refs[0][...], tab_refs[1][...], tab_refs[2][...], 1)
            if out_scale is not None:
                value, scale_pred = out_scale
                use = scale_pred(j)
                roped = roped * (value if use is True else jnp.where(use, value, 1.0))
            if flag is True:
                o_ref[...] = roped.astype(o_ref.dtype)
                return

            @pl.when(flag)
            def _():
                o_ref[...] = roped.astype(o_ref.dtype)

            @pl.when(jnp.logical_not(flag))
            def _():
                o_ref[...] = res.astype(o_ref.dtype)

        part = lax.dot_general(a_ref[...].astype(BF16), b_ref[...].astype(BF16), dims,
                               preferred_element_type=F32)
        if nk == 1:
            finish(part)
            return
        kk = pl.program_id(2)

        @pl.when(kk == 0)
        def _():
            acc_ref[...] = part

        @pl.when(kk > 0)
        def _():
            acc_ref[...] += part

        @pl.when(kk == nk - 1)
        def _():
            finish(acc_ref[...])

    return pl.pallas_call(
        body, name=name, grid=(n // tn, m // tm, nk), in_specs=in_specs, out_specs=out_spec,
        out_shape=out_shape, input_output_aliases=aliases,
        scratch_shapes=[pltpu.VMEM((tm, tn), F32)] if nk > 1 else [],
        compiler_params=_params(("parallel", "parallel", "arbitrary")),
    )(*operands)


def _rms_fwd(x, g, out_dtype, *, name, add=None, tr=512):
    n, d = x.shape
    tr = _tile(n, tr)
    row = pl.BlockSpec((tr, d), lambda i: (i, 0))
    vec = pl.BlockSpec((1, d), lambda i: (0, 0))

    def body(*refs):
        x_ref, g_ref = refs[0], refs[1]
        o_ref = refs[-1]
        xv = x_ref[...].astype(F32)
        r = lax.rsqrt(jnp.mean(xv * xv, axis=-1, keepdims=True) + NORM_EPS)
        y = xv * r * g_ref[...]
        if add is not None:
            y = refs[2][...] + y
        o_ref[...] = y.astype(o_ref.dtype)

    ops = [x, g] + ([add] if add is not None else [])
    specs = [row, vec] + ([row] if add is not None else [])
    return pl.pallas_call(
        body, name=name, grid=(n // tr,), in_specs=specs, out_specs=row,
        out_shape=jax.ShapeDtypeStruct((n, d), out_dtype), compiler_params=_params(("parallel",)),
    )(*ops)


def _rms_bwd(x, g, dy, out_dtype, *, name, adds=(), tr=512):
    n, d = x.shape
    tr = _tile(n, tr)
    steps = n // tr
    row = pl.BlockSpec((tr, d), lambda i: (i, 0))
    vec = pl.BlockSpec((1, d), lambda i: (0, 0))
    na = len(adds)

    def body(*refs):
        x_ref, g_ref, dy_ref = refs[:3]
        add_refs = refs[3:3 + na]
        dx_ref, dg_ref, acc_ref = refs[3 + na:]
        i = pl.program_id(0)
        xv = x_ref[...].astype(F32)
        r = lax.rsqrt(jnp.mean(xv * xv, axis=-1, keepdims=True) + NORM_EPS)
        xh = xv * r
        dyv = dy_ref[...].astype(F32)
        part = (dyv * xh).reshape(tr // 8, 8, d).sum(axis=0)

        @pl.when(i == 0)
        def _():
            acc_ref[...] = part

        @pl.when(i > 0)
        def _():
            acc_ref[...] += part

        t = dyv * g_ref[...]
        dx = r * (t - xh * jnp.mean(t * xh, axis=-1, keepdims=True))
        for a_ref in add_refs:
            dx = dx + a_ref[...].astype(F32)
        dx_ref[...] = dx.astype(dx_ref.dtype)

        @pl.when(i == steps - 1)
        def _():
            dg_ref[...] = jnp.sum(acc_ref[...], axis=0, keepdims=True)

    return pl.pallas_call(
        body, name=name, grid=(steps,), in_specs=[row, vec, row] + [row] * na,
        out_specs=(row, vec),
        out_shape=(jax.ShapeDtypeStruct((n, d), out_dtype), jax.ShapeDtypeStruct((1, d), F32)),
        scratch_shapes=[pltpu.VMEM((8, d), F32)], compiler_params=_params(("arbitrary",)),
    )(x, g, dy, *adds)


def _kv_latent_fwd(ckr, g_lat, tabs, *, tr=512):
    n = ckr.shape[0]
    tr = _tile(n, tr)
    lat = B_KV_LORA

    def body(c_ref, k_ref, g_ref, tc, tsa, tsb, ckv_ref, kr_ref):
        xv = c_ref[...]
        r = lax.rsqrt(jnp.mean(xv * xv, axis=-1, keepdims=True) + NORM_EPS)
        ckv_ref[...] = (xv * r * g_ref[...]).astype(BF16)
        kr_ref[...] = _rope_apply(k_ref[...], tc[...], tsa[...], tsb[...], 1).astype(BF16)

    tab = pl.BlockSpec((tr, LANES), lambda i: (i, 0))
    return pl.pallas_call(
        body, name="kv_latent_fwd", grid=(n // tr,),
        in_specs=[pl.BlockSpec((tr, lat), lambda i: (i, 0)),
                  pl.BlockSpec((tr, LANES), lambda i: (i, lat // LANES)),
                  pl.BlockSpec((1, lat), lambda i: (0, 0)), tab, tab, tab],
        out_specs=(pl.BlockSpec((tr, lat), lambda i: (i, 0)), tab),
        out_shape=(jax.ShapeDtypeStruct((n, lat), BF16), jax.ShapeDtypeStruct((n, LANES), BF16)),
        compiler_params=_params(("parallel",)),
    )(ckr, ckr, g_lat, *tabs)


def _kv_latent_bwd(dckv, ckr, g_lat, dk_cat, tabs, *, tr=512):
    n = ckr.shape[0]
    tr = _tile(n, tr)
    steps = n // tr
    lat = B_KV_LORA
    wk = dk_cat.shape[1]

    def body(d_ref, c_ref, g_ref, dk_ref, tc, tsa, tsb, o_ref, dg_ref, acc_ref):
        i = pl.program_id(0)
        xv = c_ref[...]
        r = lax.rsqrt(jnp.mean(xv * xv, axis=-1, keepdims=True) + NORM_EPS)
        xh = xv * r
        dyv = d_ref[...]
        part = (dyv * xh).reshape(tr // 8, 8, lat).sum(axis=0)

        @pl.when(i == 0)
        def _():
            acc_ref[...] = part

        @pl.when(i > 0)
        def _():
            acc_ref[...] += part

        t = dyv * g_ref[...]
        dx = r * (t - xh * jnp.mean(t * xh, axis=-1, keepdims=True))
        o_ref[:, 0:lat] = dx.astype(o_ref.dtype)
        dkr = dk_ref[:, 0:LANES].astype(F32)
        for h in range(1, wk // LANES):
            dkr = dkr + dk_ref[:, h * LANES:(h + 1) * LANES].astype(F32)
        o_ref[:, lat:lat + LANES] = _rope_apply(dkr, tc[...], tsa[...], tsb[...], -1).astype(o_ref.dtype)

        @pl.when(i == steps - 1)
        def _():
            dg_ref[...] = jnp.sum(acc_ref[...], axis=0, keepdims=True)

    tab = pl.BlockSpec((tr, LANES), lambda i: (i, 0))
    return pl.pallas_call(
        body, name="kv_latent_bwd", grid=(steps,),
        in_specs=[pl.BlockSpec((tr, lat), lambda i: (i, 0)), pl.BlockSpec((tr, lat), lambda i: (i, 0)),
                  pl.BlockSpec((1, lat), lambda i: (0, 0)), pl.BlockSpec((tr, wk), lambda i: (i, 0)),
                  tab, tab, tab],
        out_specs=(pl.BlockSpec((tr, lat + LANES), lambda i: (i, 0)), pl.BlockSpec((1, lat), lambda i: (0, 0))),
        out_shape=(jax.ShapeDtypeStruct((n, lat + LANES), BF16), jax.ShapeDtypeStruct((1, lat), F32)),
        scratch_shapes=[pltpu.VMEM((8, lat), F32)], compiler_params=_params(("arbitrary",)),
    )(dckv, ckr, g_lat, dk_cat, *tabs)


def _sigmoid(z):
    return 1.0 / (1.0 + jnp.exp(-z))


def _lane_place(cols, width):
    rows = cols[0].shape[0]
    lane = lax.broadcasted_iota(jnp.int32, (rows, width), 1)
    out = jnp.zeros((rows, width), F32)
    for h, col in enumerate(cols):
        out = jnp.where(lane == h, col, out)
    return out


def _merge_gate_fwd(outs, lses, proj, z_block, *, tr=256):
    n, w = outs[0].shape
    tr = _tile(n, tr)
    ng = len(outs)

    def body(*refs):
        o_refs = refs[:ng]
        l_refs = refs[ng:2 * ng]
        z_ref = refs[2 * ng]
        y_ref, om_ref, lse_ref = refs[2 * ng + 1:]
        ls = [r[...] for r in l_refs]
        mx = ls[0]
        for l in ls[1:]:
            mx = jnp.maximum(mx, l)
        ssum = jnp.exp2(ls[0] - mx)
        for l in ls[1:]:
            ssum = ssum + jnp.exp2(l - mx)
        tot = mx + jnp.log2(ssum)
        lse_ref[...] = tot
        ws = [jnp.exp2(l - tot) for l in ls]
        for h in range(A_HEADS):
            sl = slice(h * A_HEAD_DIM, (h + 1) * A_HEAD_DIM)
            o = ws[0][:, h:h + 1] * o_refs[0][:, sl]
            for gi in range(1, ng):
                o = o + ws[gi][:, h:h + 1] * o_refs[gi][:, sl]
            z = z_ref[:, sl].astype(F32)
            om_ref[:, sl] = o.astype(BF16)
            y_ref[:, sl] = (o * (z * _sigmoid(z))).astype(BF16)

    row = pl.BlockSpec((tr, w), lambda i: (i, 0))
    lrow = pl.BlockSpec((tr, A_HEADS), lambda i: (i, 0))
    return pl.pallas_call(
        body, name="merge_gate_fwd", grid=(n // tr,),
        in_specs=[row] * ng + [lrow] * ng + [pl.BlockSpec((tr, w), lambda i: (i, z_block))],
        out_specs=(row, row, lrow),
        out_shape=(jax.ShapeDtypeStruct((n, w), BF16), jax.ShapeDtypeStruct((n, w), BF16),
                   jax.ShapeDtypeStruct((n, A_HEADS), F32)),
        compiler_params=_params(("parallel",)),
    )(*outs, *lses, proj)


def _gate_bwd(dy, o, z_arr, z_block, *, name, with_delta, tr=256):
    n, w = dy.shape
    tr = _tile(n, tr)

    def body(*refs):
        dy_ref, o_ref, z_ref, do_ref, dz_ref = refs[:5]
        dyv = dy_ref[...].astype(F32)
        ov = o_ref[...].astype(F32)
        z = z_ref[...].astype(F32)
        sig = _sigmoid(z)
        do = dyv * (z * sig)
        do_ref[...] = do.astype(BF16)
        dz_ref[...] = (dyv * ov * (sig * (1.0 + z * (1.0 - sig)))).astype(BF16)
        if with_delta:
            prod = do * ov
            cols = [jnp.sum(prod[:, h * A_HEAD_DIM:(h + 1) * A_HEAD_DIM], axis=-1, keepdims=True)
                    for h in range(A_HEADS)]
            refs[5][...] = _lane_place(cols, A_HEADS)

    row = pl.BlockSpec((tr, w), lambda i: (i, 0))
    out_specs = [row, row]
    out_shape = [jax.ShapeDtypeStruct((n, w), BF16), jax.ShapeDtypeStruct((n, w), BF16)]
    if with_delta:
        out_specs.append(pl.BlockSpec((tr, A_HEADS), lambda i: (i, 0)))
        out_shape.append(jax.ShapeDtypeStruct((n, A_HEADS), F32))
    return pl.pallas_call(
        body, name=name, grid=(n // tr,),
        in_specs=[row, row, pl.BlockSpec((tr, w), lambda i: (i, z_block))],
        out_specs=tuple(out_specs), out_shape=tuple(out_shape), compiler_params=_params(("parallel",)),
    )(dy, o, z_arr)


def _loss_fwd_bwd(h, target, *, tr=512):
    n, d = h.shape
    tr = _tile(n, tr)
    steps = n // tr

    def body(h_ref, t_ref, dh_ref, loss_ref, acc_ref):
        i = pl.program_id(0)
        e = h_ref[...] - t_ref[...]
        dh_ref[...] = e / d
        part = (e * e).reshape(tr // 8, 8, d).sum(axis=0)

        @pl.when(i == 0)
        def _():
            acc_ref[...] = part

        @pl.when(i > 0)
        def _():
            acc_ref[...] += part

        @pl.when(i == steps - 1)
        def _():
            s = jnp.sum(jnp.sum(acc_ref[...], axis=-1, keepdims=True), axis=0, keepdims=True)
            loss_ref[...] = 0.5 * s / d

    row = pl.BlockSpec((tr, d), lambda i: (i, 0))
    return pl.pallas_call(
        body, name="loss", grid=(steps,), in_specs=[row, row],
        out_specs=(row, pl.BlockSpec((1, 1), lambda i: (0, 0))),
        out_shape=(jax.ShapeDtypeStruct((n, d), F32), jax.ShapeDtypeStruct((1, 1), F32)),
        scratch_shapes=[pltpu.VMEM((8, d), F32)], compiler_params=_params(("arbitrary",)),
    )(h, target)


def _dot_nt(a, b):
    return lax.dot_general(a, b, (((1,), (1,)), ((), ())), preferred_element_type=F32)


def _dot_nn(a, b):
    return lax.dot_general(a, b, (((1,), (0,)), ((), ())), preferred_element_type=F32)


def _attn_a_fwd(qkv, cb0, qb, *, name):
    bl, dil, ln, _ = qkv.shape
    nb = ln // qb
    hw = A_WIDTH
    heads = range(A_HEADS)
    sls = [slice(h * A_HEAD_DIM, (h + 1) * A_HEAD_DIM) for h in heads]

    def body(q_ref, kc_ref, kp_ref, vc_ref, vp_ref, o_ref, lse_ref):
        i = pl.program_id(2)
        qi = lax.broadcasted_iota(jnp.int32, (qb, qb), 0)
        ki = lax.broadcasted_iota(jnp.int32, (qb, qb), 1)
        mask_c = ki <= qi
        mask_p = jnp.logical_and(ki >= qi, i >= 1)
        s_c = [jnp.where(mask_c, _dot_nt(q_ref[:, sls[h]], kc_ref[:, sls[h]]), NEG) for h in heads]
        m = [jnp.max(s_c[h], axis=-1, keepdims=True) for h in heads]
        if nb > 1:
            s_p = [jnp.where(mask_p, _dot_nt(q_ref[:, sls[h]], kp_ref[:, sls[h]]), NEG) for h in heads]
            m = [jnp.maximum(m[h], jnp.max(s_p[h], axis=-1, keepdims=True)) for h in heads]
        p_c = [jnp.exp2(s_c[h] - m[h]) for h in heads]
        l = [jnp.sum(p_c[h], axis=-1, keepdims=True) for h in heads]
        acc = [_dot_nn(p_c[h].astype(BF16), vc_ref[:, sls[h]]) for h in heads]
        if nb > 1:
            p_p = [jnp.exp2(s_p[h] - m[h]) for h in heads]
            l = [l[h] + jnp.sum(p_p[h], axis=-1, keepdims=True) for h in heads]
            acc = [acc[h] + _dot_nn(p_p[h].astype(BF16), vp_ref[:, sls[h]]) for h in heads]
        for h in heads:
            o_ref[:, sls[h]] = acc[h] / l[h]
        lse_ref[...] = _lane_place([m[h] + jnp.log2(l[h]) for h in heads], A_HEADS)

    def spec(off, prev):
        if prev:
            return pl.BlockSpec((None, None, qb, hw), lambda b, r, i: (b, r, jnp.maximum(i - 1, 0), cb0 + off))
        return pl.BlockSpec((None, None, qb, hw), lambda b, r, i: (b, r, i, cb0 + off))

    return pl.pallas_call(
        body, name=name, grid=(bl, dil, nb),
        in_specs=[spec(0, False), spec(1, False), spec(1, True), spec(2, False), spec(2, True)],
        out_specs=(pl.BlockSpec((None, None, qb, hw), lambda b, r, i: (b, r, i, 0)),
                   pl.BlockSpec((None, None, qb, A_HEADS), lambda b, r, i: (b, r, i, 0))),
        out_shape=(jax.ShapeDtypeStruct((bl, dil, ln, hw), F32),
                   jax.ShapeDtypeStruct((bl, dil, ln, A_HEADS), F32)),
        compiler_params=_params(("parallel", "parallel", "arbitrary")),
    )(qkv, qkv, qkv, qkv, qkv)


def _attn_a_bwd(qkv, cb0, do, lse, delta, lse_t, delta_t, tabs, qb, *, name):
    bl, dil, ln, _ = qkv.shape
    nb = ln // qb
    hw = A_WIDTH

    def body(q_ref, qn_ref, kc_ref, kp_ref, vc_ref, vp_ref, do_ref, don_ref,
             lse_ref, dl_ref, lt_ref, ltn_ref, dt_ref, dtn_ref, tc, tsa, tsb, o_ref):
        i = pl.program_id(2)
        row = lax.broadcasted_iota(jnp.int32, (qb, qb), 0)
        col = lax.broadcasted_iota(jnp.int32, (qb, qb), 1)
        m_qc = col <= row
        m_kc = row <= col
        m_qp = jnp.logical_and(col >= row, i >= 1)
        m_kn = jnp.logical_and(row >= col, i + 1 < nb)
        c, sa, sb = tc[...], tsa[...], tsb[...]
        heads = range(A_HEADS)
        sls = [slice(h * A_HEAD_DIM, (h + 1) * A_HEAD_DIM) for h in heads]
        q, kc = [q_ref[:, sl] for sl in sls], [kc_ref[:, sl] for sl in sls]
        vc, dov = [vc_ref[:, sl] for sl in sls], [do_ref[:, sl] for sl in sls]
        lse_c = [lse_ref[:, h:h + 1] for h in heads]
        dl_c = [dl_ref[:, h:h + 1] for h in heads]
        s = [_dot_nt(q[h], kc[h]) for h in heads]
        st = [_dot_nt(kc[h], q[h]) for h in heads]
        dp = [_dot_nt(dov[h], vc[h]) for h in heads]
        dpt = [_dot_nt(vc[h], dov[h]) for h in heads]
        p = [jnp.exp2(jnp.where(m_qc, s[h], NEG) - lse_c[h]) for h in heads]
        pt = [jnp.exp2(jnp.where(m_kc, st[h], NEG) - lt_ref[h:h + 1, :]) for h in heads]
        dq = [_dot_nn((p[h] * (dp[h] - dl_c[h])).astype(BF16), kc[h]) for h in heads]
        dk = [_dot_nn((pt[h] * (dpt[h] - dt_ref[h:h + 1, :])).astype(BF16), q[h]) for h in heads]
        dv = [_dot_nn(pt[h].astype(BF16), dov[h]) for h in heads]
        if nb > 1:
            kp, vp = [kp_ref[:, sl] for sl in sls], [vp_ref[:, sl] for sl in sls]
            qn, don = [qn_ref[:, sl] for sl in sls], [don_ref[:, sl] for sl in sls]
            s = [_dot_nt(q[h], kp[h]) for h in heads]
            st = [_dot_nt(kc[h], qn[h]) for h in heads]
            dp = [_dot_nt(dov[h], vp[h]) for h in heads]
            dpt = [_dot_nt(vc[h], don[h]) for h in heads]
            p = [jnp.exp2(jnp.where(m_qp, s[h], NEG) - lse_c[h]) for h in heads]
            pt = [jnp.exp2(jnp.where(m_kn, st[h], NEG) - ltn_ref[h:h + 1, :]) for h in heads]
            dq = [dq[h] + _dot_nn((p[h] * (dp[h] - dl_c[h])).astype(BF16), kp[h]) for h in heads]
            dk = [dk[h] + _dot_nn((pt[h] * (dpt[h] - dtn_ref[h:h + 1, :])).astype(BF16), qn[h]) for h in heads]
            dv = [dv[h] + _dot_nn(pt[h].astype(BF16), don[h]) for h in heads]
        for h in heads:
            o_ref[:, h * A_HEAD_DIM:(h + 1) * A_HEAD_DIM] = _rope_apply(dq[h] * A_SCALE, c, sa, sb, -1).astype(BF16)
            o_ref[:, hw + h * A_HEAD_DIM:hw + (h + 1) * A_HEAD_DIM] = _rope_apply(dk[h] * LN2, c, sa, sb, -1).astype(BF16)
            o_ref[:, 2 * hw + h * A_HEAD_DIM:2 * hw + (h + 1) * A_HEAD_DIM] = dv[h].astype(BF16)

    def cur(w, col):
        return pl.BlockSpec((None, None, qb, w), lambda b, r, i: (b, r, i, col))

    def prev(w, col):
        return pl.BlockSpec((None, None, qb, w), lambda b, r, i: (b, r, jnp.maximum(i - 1, 0), col))

    def nxt(w, col):
        return pl.BlockSpec((None, None, qb, w), lambda b, r, i: (b, r, jnp.minimum(i + 1, nb - 1), col))

    t_cur = pl.BlockSpec((None, None, A_HEADS, qb), lambda b, r, i: (b, r, 0, i))
    t_nxt = pl.BlockSpec((None, None, A_HEADS, qb), lambda b, r, i: (b, r, 0, jnp.minimum(i + 1, nb - 1)))
    return pl.pallas_call(
        body, name=name, grid=(bl, dil, nb),
        in_specs=[cur(hw, cb0), nxt(hw, cb0), cur(hw, cb0 + 1), prev(hw, cb0 + 1),
                  cur(hw, cb0 + 2), prev(hw, cb0 + 2), cur(hw, 0), nxt(hw, 0),
                  cur(A_HEADS, 0), cur(A_HEADS, 0), t_cur, t_nxt, t_cur, t_nxt,
                  cur(LANES, 0), cur(LANES, 0), cur(LANES, 0)],
        out_specs=cur(3 * hw, 0),
        out_shape=jax.ShapeDtypeStruct((bl, dil, ln, 3 * hw), BF16),
        compiler_params=_params(("parallel", "parallel", "arbitrary")),
    )(qkv, qkv, qkv, qkv, qkv, qkv, do, do, lse, delta, lse_t, lse_t, delta_t, delta_t, *tabs)


def _head_terms(do, o, lse, e):
    rows = do.shape[0]
    lane = lax.broadcasted_iota(jnp.int32, (rows, LANES), 1)
    mine = (lane < B_VDIM) if e == 0 else (lane >= B_VDIM)
    prod = do.astype(F32) * o.astype(F32)
    dl = jnp.sum(jnp.where(mine, prod, 0.0), axis=-1, keepdims=True)
    do_e = jnp.where(mine, do, jnp.zeros_like(do))
    return do_e, dl, lse[:, e * B_VDIM:e * B_VDIM + 1]


def _col_to_row(col, rows):
    return jnp.transpose(jnp.broadcast_to(col, (rows, LANES)))[0:1, :]


def _mla_fwd(q_cat, kvup, kr, z, tq):
    bl, t, _ = q_cat.shape
    nq = t // tq
    pairs = B_HEADS // 2
    v_blk0 = (B_HEADS * LANES) // LANES

    def body(q_ref, k_ref, v_ref, kr_ref, z_ref, y_ref, o_ref, lse_ref, lrow_ref, m_ref, acc_ref):
        qi = pl.program_id(2)
        qs = [q_ref[:, e * LANES:(e + 1) * LANES] for e in range(2)]
        row = lax.broadcasted_iota(jnp.int32, (tq, tq), 0)
        col = lax.broadcasted_iota(jnp.int32, (tq, tq), 1)
        tri = col <= row
        sum_lane = [B_VDIM, 0]

        for e in range(2):
            m_ref[e] = jnp.full((tq, LANES), NEG, F32)
            acc_ref[e] = jnp.zeros((tq, LANES), F32)

        def tile(k0, w, masked):
            lane = lax.broadcasted_iota(jnp.int32, (w, LANES), 1)
            first = lane < B_VDIM
            krv = kr_ref[pl.ds(k0, w), :]
            v = v_ref[pl.ds(k0, w), :]
            vs = [jnp.where(first, v, jnp.where(lane == B_VDIM, 1.0, 0.0).astype(BF16)),
                  jnp.where(first, jnp.where(lane == 0, 1.0, 0.0).astype(BF16), v)]
            ss = []
            for e in range(2):
                k = k_ref[pl.ds(k0, w), e * LANES:(e + 1) * LANES] + krv
                s = _dot_nt(qs[e], k)
                ss.append(jnp.where(tri, s, NEG) if masked else s)
            m_old = [m_ref[e] for e in range(2)]
            ms = [jnp.maximum(m_old[e], jnp.max(ss[e], axis=-1, keepdims=True)) for e in range(2)]
            ps = [jnp.exp2(ss[e] - jnp.concatenate([ms[e]] * (w // LANES), axis=1)).astype(BF16) for e in range(2)]
            for e in range(2):
                m_ref[e] = ms[e]
                acc_ref[e] = jnp.exp2(m_old[e] - ms[e]) * acc_ref[e] + _dot_nn(ps[e], vs[e])

        def step(kb2, carry):
            tile(pl.multiple_of(kb2 * 2 * tq, 2 * tq), 2 * tq, False)
            return carry

        lax.fori_loop(0, qi // 2, step, 0)

        @pl.when(qi % 2 == 1)
        def _():
            tile(pl.multiple_of((qi - 1) * tq, tq), tq, False)

        tile(pl.multiple_of(qi * tq, tq), tq, True)
        lane = lax.broadcasted_iota(jnp.int32, (tq, LANES), 1)
        first = lane < B_VDIM
        accs = [acc_ref[e] for e in range(2)]
        ls = [accs[e][:, sum_lane[e]:sum_lane[e] + 1] for e in range(2)]
        outs = [accs[e] / ls[e] for e in range(2)]
        lses = [m_ref[e] + jnp.log2(ls[e]) for e in range(2)]
        o = jnp.where(first, outs[0], outs[1])
        zv = z_ref[...].astype(F32)
        o_ref[...] = o.astype(BF16)
        y_ref[...] = (o * (zv * _sigmoid(zv))).astype(BF16)
        lse_ref[...] = jnp.where(first, lses[0], lses[1])
        for e in range(2):
            lrow_ref[e:e + 1, :] = jnp.transpose(lses[e])[0:1, :]

    blk = pl.BlockSpec((None, tq, LANES), lambda b, j, i: (b, i, j))
    return pl.pallas_call(
        body, name="mla_fwd", grid=(bl, pairs, nq),
        in_specs=[pl.BlockSpec((None, tq, 2 * LANES), lambda b, j, i: (b, i, j)),
                  pl.BlockSpec((None, t, 2 * LANES), lambda b, j, i: (b, 0, j)),
                  pl.BlockSpec((None, t, LANES), lambda b, j, i: (b, 0, v_blk0 + j)),
                  pl.BlockSpec((None, t, LANES), lambda b, j, i: (b, 0, 0)),
                  blk],
        out_specs=(blk, blk, blk, pl.BlockSpec((None, None, None, 2, tq), lambda b, j, i: (b, j, i, 0, 0))),
        out_shape=(jax.ShapeDtypeStruct((bl, t, B_WIDTH), BF16), jax.ShapeDtypeStruct((bl, t, B_WIDTH), BF16),
                   jax.ShapeDtypeStruct((bl, t, B_WIDTH), F32),
                   jax.ShapeDtypeStruct((bl, pairs, nq, 2, tq), F32)),
        scratch_shapes=[pltpu.VMEM((2, tq, LANES), F32), pltpu.VMEM((2, tq, LANES), F32)],
        compiler_params=_params(("parallel", "parallel", "arbitrary")),
    )(q_cat, kvup, kvup, kr, z)


def _mla_dq(q_cat, kvup, kr, do, o, lse, tabs, tq):
    bl, t, _ = q_cat.shape
    nq = t // tq
    pairs = B_HEADS // 2
    v_blk0 = (B_HEADS * LANES) // LANES

    def body(q_ref, k_ref, v_ref, kr_ref, do_ref, o_ref, lse_ref, tc, tsa, tsb, dq_ref, drow_ref, acc_ref):
        qi = pl.program_id(2)
        dov, ov, lsev = do_ref[...], o_ref[...], lse_ref[...]
        qs = [q_ref[:, e * LANES:(e + 1) * LANES] for e in range(2)]
        terms = [_head_terms(dov, ov, lsev, e) for e in range(2)]
        row = lax.broadcasted_iota(jnp.int32, (tq, tq), 0)
        col = lax.broadcasted_iota(jnp.int32, (tq, tq), 1)
        tri = col <= row
        for e in range(2):
            acc_ref[e] = jnp.zeros((tq, LANES), F32)

        def tile(k0, w, masked):
            krv = kr_ref[pl.ds(k0, w), :]
            v = v_ref[pl.ds(k0, w), :]
            ks = [k_ref[pl.ds(k0, w), e * LANES:(e + 1) * LANES] + krv for e in range(2)]
            ss = [_dot_nt(qs[e], ks[e]) for e in range(2)]
            dps = [_dot_nt(terms[e][0], v) for e in range(2)]
            for e in range(2):
                s = jnp.where(tri, ss[e], NEG) if masked else ss[e]
                p = jnp.exp2(s - terms[e][2])
                ds = (p * (dps[e] - terms[e][1])).astype(BF16)
                acc_ref[e] += _dot_nn(ds, ks[e])

        def step(kb2, carry):
            tile(pl.multiple_of(kb2 * 2 * tq, 2 * tq), 2 * tq, False)
            return carry

        lax.fori_loop(0, qi // 2, step, 0)

        @pl.when(qi % 2 == 1)
        def _():
            tile(pl.multiple_of((qi - 1) * tq, tq), tq, False)

        tile(pl.multiple_of(qi * tq, tq), tq, True)
        for e in range(2):
            dq_ref[:, e * LANES:(e + 1) * LANES] = _rope_apply(acc_ref[e] * B_SCALE, tc[...], tsa[...], tsb[...], -1).astype(BF16)
            drow_ref[e:e + 1, :] = _col_to_row(terms[e][1], tq)

    blk = pl.BlockSpec((None, tq, LANES), lambda b, j, i: (b, i, j))
    tab = pl.BlockSpec((None, tq, LANES), lambda b, j, i: (b, i, 0))
    qblk = pl.BlockSpec((None, tq, 2 * LANES), lambda b, j, i: (b, i, j))
    return pl.pallas_call(
        body, name="mla_dq", grid=(bl, pairs, nq),
        in_specs=[qblk,
                  pl.BlockSpec((None, t, 2 * LANES), lambda b, j, i: (b, 0, j)),
                  pl.BlockSpec((None, t, LANES), lambda b, j, i: (b, 0, v_blk0 + j)),
                  pl.BlockSpec((None, t, LANES), lambda b, j, i: (b, 0, 0)),
                  blk, blk, blk, tab, tab, tab],
        out_specs=(qblk, pl.BlockSpec((None, None, None, 2, tq), lambda b, j, i: (b, j, i, 0, 0))),
        out_shape=(jax.ShapeDtypeStruct((bl, t, B_HEADS * LANES), BF16),
                   jax.ShapeDtypeStruct((bl, pairs, nq, 2, tq), F32)),
        scratch_shapes=[pltpu.VMEM((2, tq, LANES), F32)],
        compiler_params=_params(("parallel", "parallel", "arbitrary")),
    )(q_cat, kvup, kvup, kr, do, o, lse, *tabs)


def _mla_dkv(q_cat, kvup, kr, do, lse_rows, delta_rows, tq):
    bl, t, _ = q_cat.shape
    nq = t // tq
    pairs = B_HEADS // 2
    v_blk0 = (B_HEADS * LANES) // LANES

    def body(q_ref, k_ref, v_ref, kr_ref, do_ref, lrow_ref, drow_ref, dk_ref, dv_ref, acc_ref):
        kb = pl.program_id(2)
        v = v_ref[...]
        krv = kr_ref[...]
        ks = [k_ref[:, e * LANES:(e + 1) * LANES] + krv for e in range(2)]
        krow = lax.broadcasted_iota(jnp.int32, (tq, tq), 0)
        qcol = lax.broadcasted_iota(jnp.int32, (tq, tq), 1)
        tri = krow <= qcol
        lane = lax.broadcasted_iota(jnp.int32, (tq, LANES), 1)
        mine = [lane < B_VDIM, lane >= B_VDIM]

        for e in range(3):
            acc_ref[e] = jnp.zeros((tq, LANES), F32)

        def tile(qb, nblk, masked):
            w = nblk * tq
            rows = pl.ds(pl.multiple_of(qb * tq, tq), w)
            dov = do_ref[rows, :]
            lane_w = lax.broadcasted_iota(jnp.int32, (w, LANES), 1)
            mine_w = [lane_w < B_VDIM, lane_w >= B_VDIM]
            qs = [q_ref[rows, e * LANES:(e + 1) * LANES] for e in range(2)]
            does = [jnp.where(mine_w[e], dov, jnp.zeros_like(dov)) for e in range(2)]
            sts = [_dot_nt(ks[e], qs[e]) for e in range(2)]
            dpts = [_dot_nt(v, does[e]) for e in range(2)]

            def rows_of(ref, e):
                return jnp.concatenate([ref[qb + i, e:e + 1, :] for i in range(nblk)], axis=1)

            pts = []
            for e in range(2):
                st = jnp.where(tri, sts[e], NEG) if masked else sts[e]
                pts.append(jnp.exp2(st - rows_of(lrow_ref, e)))
            acc_ref[2] += _dot_nn(pts[0].astype(BF16), does[0]) + _dot_nn(pts[1].astype(BF16), does[1])
            for e in range(2):
                dst = (pts[e] * (dpts[e] - rows_of(drow_ref, e))).astype(BF16)
                acc_ref[e] += _dot_nn(dst, qs[e])

        tile(kb, 1, True)
        rest = nq - 1 - kb
        odd = rest % 2

        @pl.when(odd == 1)
        def _():
            tile(kb + 1, 1, False)

        def step(i, carry):
            tile(kb + 1 + odd + 2 * i, 2, False)
            return carry

        lax.fori_loop(0, rest // 2, step, 0)
        dk_ref[:, 0:LANES] = (acc_ref[0] * LN2).astype(BF16)
        dk_ref[:, LANES:2 * LANES] = (acc_ref[1] * LN2).astype(BF16)
        dv_ref[...] = acc_ref[2].astype(BF16)

    full = pl.BlockSpec((None, t, LANES), lambda b, j, i: (b, 0, j))
    rows = pl.BlockSpec((None, None, nq, 2, tq), lambda b, j, i: (b, j, 0, 0, 0))
    kblk = pl.BlockSpec((None, tq, 2 * LANES), lambda b, j, i: (b, i, j))
    return pl.pallas_call(
        body, name="mla_dkv", grid=(bl, pairs, nq),
        in_specs=[pl.BlockSpec((None, t, 2 * LANES), lambda b, j, i: (b, 0, j)),
                  kblk,
                  pl.BlockSpec((None, tq, LANES), lambda b, j, i: (b, i, v_blk0 + j)),
                  pl.BlockSpec((None, tq, LANES), lambda b, j, i: (b, i, 0)),
                  full, rows, rows],
        out_specs=(kblk, pl.BlockSpec((None, tq, LANES), lambda b, j, i: (b, i, j))),
        out_shape=(jax.ShapeDtypeStruct((bl, t, B_HEADS * LANES), BF16),
                   jax.ShapeDtypeStruct((bl, t, B_WIDTH), BF16)),
        scratch_shapes=[pltpu.VMEM((3, tq, LANES), F32)],
        compiler_params=_params(("parallel", "parallel", "arbitrary")),
    )(q_cat, kvup, kvup, kr, do, lse_rows, delta_rows)


def _adamw(w, g, m, v, *, name):
    r, c = w.shape
    tr = _row_tile(r, 256)
    c1 = 1.0 - ADAM_B1
    c2 = 1.0 - ADAM_B2
    bc1 = 1.0 - ADAM_B1 ** ADAM_STEP
    bc2 = 1.0 - ADAM_B2 ** ADAM_STEP

    def body(w_ref, g_ref, m_ref, v_ref, d_ref, nm_ref, nv_ref):
        gv = g_ref[...]
        nm = ADAM_B1 * m_ref[...] + c1 * gv
        nv = ADAM_B2 * v_ref[...] + c2 * (gv * gv)
        nm_ref[...] = nm
        nv_ref[...] = nv
        d_ref[...] = -ADAM_LR * ((nm / bc1) / (jnp.sqrt(nv / bc2) + ADAM_EPS) + ADAM_WD * w_ref[...])

    blk = pl.BlockSpec((tr, c), lambda i: (i, 0))
    sds = jax.ShapeDtypeStruct((r, c), F32)
    return pl.pallas_call(
        body, name=name, grid=(r // tr,), in_specs=[blk] * 4, out_specs=(blk,) * 3,
        out_shape=(sds,) * 3, compiler_params=_params(("parallel",)),
    )(w, g, m, v)


def _add_my_half(stacked, other, core, out_dtype, *, name):
    nch, a, c = stacked.shape
    h = a // 2
    tr = _row_tile(h, 256)
    nblk = h // tr

    def body(core_ref, s_ref, p_ref, o_ref):
        o_ref[...] = (s_ref[...] + p_ref[...]).astype(o_ref.dtype)

    return pl.pallas_call(
        body, name=name,
        grid_spec=pltpu.PrefetchScalarGridSpec(
            num_scalar_prefetch=1, grid=(nch, nblk),
            in_specs=[pl.BlockSpec((None, tr, c), lambda k, i, cr: (k, cr[0] * nblk + i, 0)),
                      pl.BlockSpec((None, tr, c), lambda k, i, cr: (k, i, 0))],
            out_specs=pl.BlockSpec((None, tr, c), lambda k, i, cr: (k, i, 0))),
        out_shape=jax.ShapeDtypeStruct((nch, h, c), out_dtype),
        compiler_params=_params(("parallel", "parallel")),
    )(core, stacked, other)


def _sum_chips(parts, own, chip, *, name):
    nch, h, c = parts.shape
    tr = _row_tile(h, 256)

    def body(chip_ref, p_ref, own_ref, o_ref):
        me = chip_ref[0]

        def slot(k):
            return jnp.where(me == k, own_ref[k].astype(F32), p_ref[k].astype(F32))

        acc = slot(0) + slot(1)
        for k in range(2, nch):
            acc = acc + slot(k)
        o_ref[...] = acc

    blk = pl.BlockSpec((nch, tr, c), lambda i, cr: (0, i, 0))
    return pl.pallas_call(
        body, name=name,
        grid_spec=pltpu.PrefetchScalarGridSpec(
            num_scalar_prefetch=1, grid=(h // tr,), in_specs=[blk, blk],
            out_specs=pl.BlockSpec((tr, c), lambda i, cr: (i, 0))),
        out_shape=jax.ShapeDtypeStruct((h, c), F32), compiler_params=_params(("parallel",)),
    )(chip, parts, own)


def _join_halves(mine, other, core, *, name):
    h, c = mine.shape
    tr = _row_tile(h, 256)
    nblk = h // tr

    def body(core_ref, m_ref, s_ref, o_ref):
        is_mine = pl.program_id(0) // nblk == core_ref[0]

        @pl.when(is_mine)
        def _():
            o_ref[...] = m_ref[...]

        @pl.when(jnp.logical_not(is_mine))
        def _():
            o_ref[...] = s_ref[...]

    blk = pl.BlockSpec((tr, c), lambda i, cr: (i % nblk, 0))
    return pl.pallas_call(
        body, name=name,
        grid_spec=pltpu.PrefetchScalarGridSpec(
            num_scalar_prefetch=1, grid=(2 * nblk,), in_specs=[blk, blk],
            out_specs=pl.BlockSpec((tr, c), lambda i, cr: (i, 0))),
        out_shape=jax.ShapeDtypeStruct((2 * h, c), F32), compiler_params=_params(("arbitrary",)),
    )(core, mine, other)


def _place():
    x, y, c = lax.axis_index("x"), lax.axis_index("y"), lax.axis_index("c")
    chips = [(1 - x, y), (x, 1 - y), (1 - x, 1 - y)]
    return x, y, c, chips


def _remote(src, dst, send_sems, recv_sems, k, to):
    return pltpu.make_async_remote_copy(src_ref=src, dst_ref=dst, send_sem=send_sems.at[k],
                                        recv_sem=recv_sems.at[k], device_id=to, device_id_type=MESH)


def _hbm_call(body, name, ins, out_shapes, n_remote):
    any_spec = pl.BlockSpec(memory_space=pl.ANY)
    return pl.pallas_call(
        body, name=name, in_specs=[any_spec] * len(ins), out_specs=tuple([any_spec] * len(out_shapes)),
        out_shape=tuple(out_shapes),
        scratch_shapes=[pltpu.SemaphoreType.DMA((n_remote,)), pltpu.SemaphoreType.DMA((n_remote,))],
    )(*ins)


def _all_gather_chips(shards, *, name):
    n = len(shards)

    def body(*refs):
        ins, outs = refs[:n], refs[n:2 * n]
        send_sems, recv_sems = refs[2 * n:]
        x, y, c, chips = _place()
        me = 2 * x + y
        sent = []
        for s in range(n):
            h = ins[s].shape[0] // 2
            for j, (px, py) in enumerate(chips):
                cp = _remote(ins[s].at[pl.ds(c * h, h)], outs[s].at[me, pl.ds(c * h, h)],
                             send_sems, recv_sems, s * 6 + j, (px, py, c))
                cp.start()
                sent.append(cp)
        for s in range(n):
            h = ins[s].shape[0] // 2
            for j, (px, py) in enumerate(chips):
                slab = outs[s].at[2 * px + py, pl.ds(c * h, h)]
                _remote(slab, slab, send_sems, recv_sems, s * 6 + j, (px, py, c)).wait_recv()
                cp = _remote(slab, slab, send_sems, recv_sems, s * 6 + 3 + j, (x, y, 1 - c))
                cp.start()
                sent.append(cp)
        for s in range(n):
            h = ins[s].shape[0] // 2
            for j, (px, py) in enumerate(chips):
                slab = outs[s].at[2 * px + py, pl.ds((1 - c) * h, h)]
                _remote(slab, slab, send_sems, recv_sems, s * 6 + 3 + j, (x, y, 1 - c)).wait_recv()
        for cp in sent:
            cp.wait_send()

    out_shapes = [jax.ShapeDtypeStruct((N_CHIPS,) + s.shape, s.dtype) for s in shards]
    return _hbm_call(body, name, shards, out_shapes, 6 * n)


def _pair_send_other_half(stacked, *, name):
    n = len(stacked)

    def body(*refs):
        ins, outs = refs[:n], refs[n:2 * n]
        send_sems, recv_sems = refs[2 * n:]
        x, y, c, _chips = _place()
        sent = []
        for s in range(n):
            h = ins[s].shape[1] // 2
            cp = _remote(ins[s].at[:, pl.ds((1 - c) * h, h)], outs[s], send_sems, recv_sems, s, (x, y, 1 - c))
            cp.start()
            sent.append(cp)
        for cp in sent:
            cp.wait_recv()
        for cp in sent:
            cp.wait_send()

    out_shapes = [jax.ShapeDtypeStruct((s.shape[0], s.shape[1] // 2, s.shape[2]), s.dtype) for s in stacked]
    return _hbm_call(body, name, stacked, out_shapes, n)


def _chip_exchange(halves, *, name):
    n = len(halves)

    def body(*refs):
        ins, outs = refs[:n], refs[n:2 * n]
        send_sems, recv_sems = refs[2 * n:]
        x, y, c, chips = _place()
        me = 2 * x + y
        sent = []
        for s in range(n):
            for j, (px, py) in enumerate(chips):
                cp = _remote(ins[s].at[2 * px + py], outs[s].at[me], send_sems, recv_sems, s * 3 + j, (px, py, c))
                cp.start()
                sent.append(cp)
        for s in range(n):
            for j, (px, py) in enumerate(chips):
                slab = outs[s].at[2 * px + py]
                _remote(slab, slab, send_sems, recv_sems, s * 3 + j, (px, py, c)).wait_recv()
        for cp in sent:
            cp.wait_send()

    out_shapes = [jax.ShapeDtypeStruct(s.shape, s.dtype) for s in halves]
    return _hbm_call(body, name, halves, out_shapes, 3 * n)


def _pair_swap(halves, *, name):
    n = len(halves)

    def body(*refs):
        ins, outs = refs[:n], refs[n:2 * n]
        send_sems, recv_sems = refs[2 * n:]
        x, y, c, _chips = _place()
        sent = []
        for s in range(n):
            cp = _remote(ins[s], outs[s], send_sems, recv_sems, s, (x, y, 1 - c))
            cp.start()
            sent.append(cp)
        for cp in sent:
            cp.wait_recv()
        for cp in sent:
            cp.wait_send()

    out_shapes = [jax.ShapeDtypeStruct(s.shape, s.dtype) for s in halves]
    return _hbm_call(body, name, halves, out_shapes, n)


def _pack_rows(parts, row_multiple):
    flat = jnp.concatenate([p.reshape(-1) for p in parts])
    quantum = row_multiple * PACK_COLS
    pad = (-flat.shape[0]) % quantum
    flat = jnp.pad(flat, (0, pad))
    return flat.reshape(-1, PACK_COLS)


def _unpack(flat, shapes):
    out, pos = [], 0
    for shp in shapes:
        size = math.prod(shp)
        out.append(flat[pos:pos + size].reshape(shp))
        pos += size
    return out


def _to_chunks_cols(full):
    r, c4 = full.shape
    return full.reshape(r, N_CHIPS, c4 // N_CHIPS).transpose(1, 0, 2)


def _from_chunks_cols(stacked):
    nch, r, c = stacked.shape
    return stacked.transpose(1, 0, 2).reshape(r, nch * c)


def _class_major(a, bl, t, dil):
    w = a.shape[-1]
    if dil == 1:
        return a.reshape(bl, 1, t, w)
    return a.reshape(bl, t // dil, dil, w).transpose(0, 2, 1, 3)


def _natural(a):
    bl, dil, ln, w = a.shape
    if dil == 1:
        return a.reshape(bl * ln, w)
    return a.transpose(0, 2, 1, 3).reshape(bl * ln * dil, w)


def _train_step(x, positions, a_pre_norm, a_w_in, a_w_out, a_post_norm, kv_norm, kv_w_down, kv_latent_norm,
                kv_w_up, b_pre_norm, b_w_in, b_q_norm, b_w_q_up, b_w_out, b_post_norm, loss_target, moments):
    bl, t, d = x.shape
    n = bl * t
    qb = t // A_DILATIONS[-1]
    tq = _tile(t, 256)
    dq4 = d // N_CHIPS
    chip = 2 * lax.axis_index("x") + lax.axis_index("y")
    chip_arr = chip.astype(jnp.int32).reshape(1)
    core_arr = lax.axis_index("c").astype(jnp.int32).reshape(1)

    w_in_a_s = a_w_in[0].astype(BF16)
    outs_s = jnp.concatenate([a_w_out[0], b_w_out[0]], axis=0).astype(BF16)
    small_shapes = [kv_w_down.shape, kv_w_up.shape, b_w_in[0].shape, b_w_q_up[0].shape]
    small_s = _pack_rows([kv_w_down, kv_w_up, b_w_in[0], b_w_q_up[0]], 32).astype(BF16)
    gains_s = jnp.pad(jnp.concatenate([a_pre_norm[0], a_post_norm[0]]), (0, 16 * LANES - 2 * dq4)).reshape(16, LANES)
    shards = [w_in_a_s, outs_s, small_s, gains_s]
    gathered = _all_gather_chips(shards, name="gather_weights")
    g_in_a, g_outs, g_small, g_gains = [lax.dynamic_update_index_in_dim(g, s, chip, 0)
                                        for g, s in zip(gathered, shards)]

    w_in_a = _from_chunks_cols(g_in_a)
    w_out_a = g_outs[:, :A_WIDTH // N_CHIPS].reshape(A_WIDTH, d)
    w_out_b = g_outs[:, A_WIDTH // N_CHIPS:].reshape(B_WIDTH, d)
    sm = [_unpack(g_small[k].reshape(-1), small_shapes) for k in range(N_CHIPS)]
    w_down = jnp.concatenate([sm[k][0] for k in range(N_CHIPS)], axis=0)
    w_up = jnp.concatenate([sm[k][1] for k in range(N_CHIPS)], axis=1)
    w_in_b = jnp.concatenate([sm[k][2] for k in range(N_CHIPS)], axis=1)
    w_q_up = jnp.concatenate([sm[k][3] for k in range(N_CHIPS)], axis=1)
    gflat = g_gains.reshape(N_CHIPS, -1)
    g_a_pre = gflat[:, :dq4].reshape(1, d)
    g_a_post = gflat[:, dq4:2 * dq4].reshape(1, d)

    w_up_h = w_up.reshape(B_KV_LORA, B_HEADS, B_NOPE + B_VDIM)
    w_up_k = jnp.pad(w_up_h[:, :, :B_NOPE], ((0, 0), (0, 0), (0, LANES - B_NOPE))).reshape(B_KV_LORA, B_HEADS * LANES)
    w_up_v = w_up_h[:, :, B_NOPE:].reshape(B_KV_LORA, B_WIDTH)
    w_up_cat = jnp.concatenate([w_up_k, w_up_v], axis=1)
    w_q_up_p = jnp.pad(w_q_up.reshape(B_Q_LORA, B_HEADS, B_QK_DIM),
                       ((0, 0), (0, 0), (0, LANES - B_QK_DIM))).reshape(B_Q_LORA, B_HEADS * LANES)
    zeros_d = lambda c: jnp.zeros((d, c), BF16)
    w_down_p = jnp.concatenate([w_down[:, :B_KV_LORA], zeros_d(B_NOPE), w_down[:, B_KV_LORA:],
                                zeros_d(LANES - B_NOPE - B_ROPE)], axis=1)
    w_cq = w_in_b[:, :B_Q_LORA]
    w_z = w_in_b[:, B_Q_LORA:]

    tabs_a = _rope_tables(positions, A_ROPE_THETA, 0)
    tabs_b = _rope_tables(positions, B_ROPE_THETA, B_NOPE)

    h0 = x.reshape(n, d)
    hn_a = _rms_fwd(h0, g_a_pre, BF16, name="a_pre_norm")
    is_qk = lambda j: jnp.logical_and(j < 3 * A_GROUPS, j % 3 != 2)
    is_q = lambda j: jnp.logical_and(j < 3 * A_GROUPS, j % 3 == 0)
    proj_a = _matmul(hn_a, w_in_a, "nn", BF16, name="a_proj", rope=(tabs_a, is_qk),
                     out_scale=(A_SCALE * LOG2E, is_q))
    z_blk_a = 3 * A_GROUPS
    o_groups, lse_groups, qkv_cm = [], [], []
    for g, dil in enumerate(A_DILATIONS):
        if dil == 1:
            src, cb0 = proj_a.reshape(bl, 1, t, A_IN_WIDTH), 3 * g
        else:
            src, cb0 = _class_major(proj_a[:, 3 * g * A_WIDTH:3 * (g + 1) * A_WIDTH], bl, t, dil), 0
        qkv_cm.append((src, cb0))
        o_g, lse_g = _attn_a_fwd(src, cb0, qb, name=f"attn_a_fwd_{g}")
        o_groups.append(_natural(o_g))
        lse_groups.append(_natural(lse_g))
    ypre_a, om_a, lse_a = _merge_gate_fwd(o_groups, lse_groups, proj_a, z_blk_a)
    y_a = _matmul(ypre_a, w_out_a, "nn", F32, name="a_out")
    h1 = _rms_fwd(y_a, g_a_post, F32, name="a_post_norm", add=h0)

    g_kvn = kv_norm.reshape(1, d)
    g_lat = kv_latent_norm.reshape(1, B_KV_LORA)
    hn_kv = _rms_fwd(h1, g_kvn, BF16, name="kv_norm")
    ckr = _matmul(hn_kv, w_down_p, "nn", F32, name="kv_down")
    c_kv, k_rope = _kv_latent_fwd(ckr, g_lat, tabs_b)
    kvup = _matmul(c_kv, w_up_cat, "nn", BF16, name="kv_up")
    hn_b = _rms_fwd(h1, b_pre_norm, BF16, name="b_pre_norm")
    z_b = _matmul(hn_b, w_z, "nn", BF16, name="b_proj_z")
    cq_raw = _matmul(hn_b, w_cq, "nn", F32, name="b_proj_q")
    c_q = _rms_fwd(cq_raw, b_q_norm, BF16, name="b_q_norm")
    always = lambda j: True
    q_cat = _matmul(c_q, w_q_up_p, "nn", BF16, name="b_q_up", rope=(tabs_b, always),
                    out_scale=(B_SCALE * LOG2E, always))
    r3 = lambda a: a.reshape(bl, t, a.shape[-1])
    tabs_b3 = tuple(r3(tb) for tb in tabs_b)
    ypre_b, o_b, lse_b, lse_rows_b = _mla_fwd(r3(q_cat), r3(kvup), r3(k_rope), r3(z_b), tq)
    y_b = _matmul(ypre_b.reshape(n, B_WIDTH), w_out_b, "nn", F32, name="b_out")
    h2 = _rms_fwd(y_b, b_post_norm, F32, name="b_post_norm", add=h1)
    dh2, loss_part = _loss_fwd_bwd(h2, loss_target.reshape(n, d))

    dy_b, dg_b_post = _rms_bwd(y_b, b_post_norm, dh2, BF16, name="b_post_norm_bwd")
    dypre_b = _matmul(dy_b, w_out_b, "nt", F32, name="b_out_dx")
    dw_out_b = _matmul(ypre_b.reshape(n, B_WIDTH), dy_b, "tn", F32, name="b_out_dw", tm=1024, tk=512)
    do_b, dz_b = _gate_bwd(dypre_b, o_b.reshape(n, B_WIDTH), z_b, 0, name="b_gate_bwd", with_delta=False)
    dq_cat, delta_rows_b = _mla_dq(r3(q_cat), r3(kvup), r3(k_rope), r3(do_b), o_b, lse_b, tabs_b3, tq)
    dq_cat = dq_cat.reshape(n, -1)
    dk_cat, dv_b = _mla_dkv(r3(q_cat), r3(kvup), r3(k_rope), r3(do_b), lse_rows_b, delta_rows_b, tq)
    dk_cat, dv_b = dk_cat.reshape(n, -1), dv_b.reshape(n, -1)
    dcq_n = _matmul(dq_cat, w_q_up_p, "nt", F32, name="b_q_up_dx")
    dw_q_up_p = _matmul(c_q, dq_cat, "tn", F32, name="b_q_up_dw", tm=1024, tk=512)
    dcq, dg_b_q = _rms_bwd(cq_raw, b_q_norm, dcq_n, BF16, name="b_q_norm_bwd")
    dhn_b = _matmul(dz_b, w_z, "nt", F32, name="b_proj_z_dx")
    dhn_b = _matmul(dcq, w_cq, "nt", F32, name="b_proj_q_dx", add=dhn_b)
    dw_z = _matmul(hn_b, dz_b, "tn", F32, name="b_proj_z_dw", tm=1024, tk=512)
    dw_cq = _matmul(hn_b, dcq, "tn", F32, name="b_proj_q_dw", tm=1024, tk=512)
    dh1, dg_b_pre = _rms_bwd(h1, b_pre_norm, dhn_b, F32, name="b_pre_norm_bwd", adds=(dh2,))
    dckv_n = _matmul(dk_cat, w_up_k, "nt", F32, name="kv_up_k_dx")
    dckv_n = _matmul(dv_b, w_up_v, "nt", F32, name="kv_up_v_dx", add=dckv_n)
    dw_up_k = _matmul(c_kv, dk_cat, "tn", F32, name="kv_up_k_dw", tm=1024, tk=512)
    dw_up_v = _matmul(c_kv, dv_b, "tn", F32, name="kv_up_v_dw", tm=1024, tk=512)
    dckr, dg_lat = _kv_latent_bwd(dckv_n, ckr, g_lat, dk_cat, tabs_b)
    dhn_kv = _matmul(dckr, w_down_p, "nt", F32, name="kv_down_dx")
    dw_down_p = _matmul(hn_kv, dckr, "tn", F32, name="kv_down_dw", tm=1024, tk=512)
    dh1, dg_kvn = _rms_bwd(h1, g_kvn, dhn_kv, F32, name="kv_norm_bwd", adds=(dh1,))

    dy_a, dg_a_post = _rms_bwd(y_a, g_a_post, dh1, BF16, name="a_post_norm_bwd")
    dypre_a = _matmul(dy_a, w_out_a, "nt", F32, name="a_out_dx")
    dw_out_a = _matmul(ypre_a, dy_a, "tn", F32, name="a_out_dw", tm=1024, tk=512)
    do_a, dz_a, delta_a = _gate_bwd(dypre_a, om_a, proj_a, z_blk_a, name="a_gate_bwd", with_delta=True)
    dw_cols = A_IN_WIDTH // N_CHIPS
    dw_tn = _tile(dw_cols, 512)
    dw_kwargs = dict(tm=1024, tn=dw_tn, tk=512, out_chunk_blocks=dw_cols // dw_tn)
    r_big = None
    dhn_a = None
    for g, dil in enumerate(A_DILATIONS):
        src, cb0 = qkv_cm[g]
        cm = lambda a: _class_major(a, bl, t, dil)
        swap = lambda a: jnp.swapaxes(a, 2, 3)
        lse_cm, delta_cm = cm(lse_a), cm(delta_a)
        dqkv = _attn_a_bwd(src, cb0, cm(do_a), lse_cm, delta_cm, swap(lse_cm), swap(delta_cm),
                           tuple(cm(tb) for tb in tabs_a), qb, name=f"attn_a_bwd_{g}")
        dqkv = _natural(dqkv)
        dhn_a = _matmul(dqkv, w_in_a, "nt", F32, name=f"a_proj_dx_{g}", add=dhn_a,
                        b_koff=3 * g * A_WIDTH // _tile(3 * A_WIDTH, 1024))
        first = dict(out_full=(N_CHIPS, d, dw_cols)) if r_big is None else dict(out_into=r_big)
        r_big = _matmul(hn_a, dqkv, "tn", F32, name=f"a_proj_dw_{g}", out_joff=3 * g * A_WIDTH // dw_tn,
                        **first, **dw_kwargs)
    dhn_a = _matmul(dz_a, w_in_a, "nt", F32, name="a_proj_dx_z", add=dhn_a, b_koff=z_blk_a)
    r_big = _matmul(hn_a, dz_a, "tn", F32, name="a_proj_dw_z", out_into=r_big,
                    out_joff=z_blk_a * A_WIDTH // dw_tn, **dw_kwargs)
    grad_x, dg_a_pre = _rms_bwd(h0, g_a_pre, dhn_a, F32, name="a_pre_norm_bwd", adds=(dh1,))

    dw_up = jnp.concatenate([dw_up_k.reshape(B_KV_LORA, B_HEADS, LANES)[:, :, :B_NOPE],
                             dw_up_v.reshape(B_KV_LORA, B_HEADS, B_VDIM)], axis=2).reshape(B_KV_LORA, -1)
    dw_q_up = dw_q_up_p.reshape(B_Q_LORA, B_HEADS, LANES)[:, :, :B_QK_DIM].reshape(B_Q_LORA, -1)
    dw_down = jnp.concatenate([dw_down_p[:, :B_KV_LORA], dw_down_p[:, B_KV_LORA + B_NOPE:B_KV_LORA + B_NOPE + B_ROPE]], axis=1)
    dw_in_b = jnp.concatenate([dw_cq, dw_z], axis=1)
    vec_rep = [dg_kvn.reshape(-1), dg_lat.reshape(-1), dg_b_pre.reshape(-1), dg_b_q.reshape(-1),
               dg_b_post.reshape(-1), loss_part.reshape(-1)]
    vec_shapes = [(dq4,), (dq4,)] + [v.shape for v in vec_rep]
    r_outs = jnp.concatenate([dw_out_a.reshape(N_CHIPS, A_WIDTH // N_CHIPS, d),
                              dw_out_b.reshape(N_CHIPS, B_WIDTH // N_CHIPS, d)], axis=1)
    down_c = dw_down.reshape(N_CHIPS, dq4, -1)
    up_c = _to_chunks_cols(dw_up)
    inb_c = _to_chunks_cols(dw_in_b)
    qup_c = _to_chunks_cols(dw_q_up)
    small_chunks = []
    for k in range(N_CHIPS):
        vecs = [dg_a_pre.reshape(-1)[k * dq4:(k + 1) * dq4], dg_a_post.reshape(-1)[k * dq4:(k + 1) * dq4]] + vec_rep
        small_chunks.append(_pack_rows([down_c[k], up_c[k], inb_c[k], qup_c[k]] + vecs, 32))
    r_small = jnp.stack(small_chunks)

    stacked = [r_big, r_outs, r_small]
    payload = [BF16, BF16, F32]
    recv = _pair_send_other_half(stacked, name="reduce_pair_send")
    halves = [_add_my_half(s, p, core_arr, dt, name=f"reduce_pair_add_{i}")
              for i, (s, p, dt) in enumerate(zip(stacked, recv, payload))]
    parts = _chip_exchange(halves, name="reduce_chip_exchange")
    sums = [_sum_chips(p, own, chip_arr, name=f"reduce_chip_sum_{i}") for i, (p, own) in enumerate(zip(parts, halves))]
    others = _pair_swap(sums, name="reduce_pair_swap")
    g_big, g_outs_r, g_small_r = [_join_halves(m, o, core_arr, name=f"reduce_join_{i}")
                                  for i, (m, o) in enumerate(zip(sums, others))]

    grads = {}
    grads["a_w_in"] = g_big
    grads["a_w_out"] = g_outs_r[:A_WIDTH // N_CHIPS]
    grads["b_w_out"] = g_outs_r[A_WIDTH // N_CHIPS:]
    small_out_shapes = [down_c.shape[1:], up_c.shape[1:], inb_c.shape[1:], qup_c.shape[1:]] + vec_shapes
    (grads["kv_w_down"], grads["kv_w_up"], grads["b_w_in"], grads["b_w_q_up"], grads["a_pre_norm"],
     grads["a_post_norm"], grads["kv_norm"], grads["kv_latent_norm"], grads["b_pre_norm"], grads["b_q_norm"],
     grads["b_post_norm"], loss_sum) = _unpack(g_small_r.reshape(-1), small_out_shapes)

    weights = dict(a_pre_norm=a_pre_norm, a_w_in=a_w_in, a_w_out=a_w_out, a_post_norm=a_post_norm, kv_norm=kv_norm,
                   kv_w_down=kv_w_down, kv_latent_norm=kv_latent_norm, kv_w_up=kv_w_up, b_pre_norm=b_pre_norm,
                   b_w_in=b_w_in, b_q_norm=b_q_norm, b_w_q_up=b_w_q_up, b_w_out=b_w_out, b_post_norm=b_post_norm)
    names = list(weights)
    out_g, out_d, out_m, out_v = [], [], [], []
    for i, nm in enumerate(names):
        w = weights[nm]
        two_d = (1, w.shape[0]) if w.ndim == 1 else (w.shape[-2], w.shape[-1])
        gw = grads[nm].reshape(two_d)
        dlt, new_m, new_v = _adamw(w.reshape(two_d), gw, moments[i].reshape(two_d),
                                   moments[len(names) + i].reshape(two_d), name=f"adamw_{nm}")
        out_g.append(gw.reshape(w.shape))
        out_d.append(dlt.reshape(w.shape))
        out_m.append(new_m.reshape(w.shape))
        out_v.append(new_v.reshape(w.shape))
    return (loss_sum.reshape(()), grad_x.reshape(bl, t, d), *out_g, *out_d, *out_m, *out_v)


def kernel(x, positions, a_pre_norm, a_w_in, a_w_out, a_post_norm, kv_norm, kv_w_down, kv_latent_norm, kv_w_up, b_pre_norm, b_w_in, b_q_norm, b_w_q_up, b_w_out, b_post_norm, loss_target, m_a_pre_norm, m_a_w_in, m_a_w_out, m_a_post_norm, m_kv_norm, m_kv_w_down, m_kv_latent_norm, m_kv_w_up, m_b_pre_norm, m_b_w_in, m_b_q_norm, m_b_w_q_up, m_b_w_out, m_b_post_norm, v_a_pre_norm, v_a_w_in, v_a_w_out, v_a_post_norm, v_kv_norm, v_kv_w_down, v_kv_latent_norm, v_kv_w_up, v_b_pre_norm, v_b_w_in, v_b_q_norm, v_b_w_q_up, v_b_w_out, v_b_post_norm):
    moments = (m_a_pre_norm, m_a_w_in, m_a_w_out, m_a_post_norm, m_kv_norm, m_kv_w_down, m_kv_latent_norm, m_kv_w_up,
               m_b_pre_norm, m_b_w_in, m_b_q_norm, m_b_w_q_up, m_b_w_out, m_b_post_norm,
               v_a_pre_norm, v_a_w_in, v_a_w_out, v_a_post_norm, v_kv_norm, v_kv_w_down, v_kv_latent_norm, v_kv_w_up,
               v_b_pre_norm, v_b_w_in, v_b_q_norm, v_b_w_q_up, v_b_w_out, v_b_post_norm)
    return _train_step(x, positions, a_pre_norm, a_w_in, a_w_out, a_post_norm, kv_norm, kv_w_down, kv_latent_norm,
                       kv_w_up, b_pre_norm, b_w_in, b_q_norm, b_w_q_up, b_w_out, b_post_norm, loss_target, moments)
```

```python
import math

import jax
import jax.numpy as jnp
from jax import lax
from jax.experimental import pallas as pl
from jax.experimental.pallas import tpu as pltpu

F32 = jnp.float32
BF16 = jnp.bfloat16
MESH = pl.DeviceIdType.MESH

NORM_EPS = 1e-6
NEG = -1e30
LANES = 128
VMEM_LIMIT = 56 * 1024 * 1024
LOG2E = math.log2(math.e)
LN2 = math.log(2.0)

A_GROUPS = 3
A_DILATIONS = (1, 4, 16)
A_HEADS = 8
A_HEAD_DIM = 128
A_WIDTH = A_HEADS * A_HEAD_DIM
A_ROPE_THETA = 500000.0
A_IN_WIDTH = A_GROUPS * 3 * A_WIDTH + A_WIDTH
A_SCALE = A_HEAD_DIM ** -0.5

B_HEADS = 16
B_NOPE = 64
B_ROPE = 32
B_QK_DIM = B_NOPE + B_ROPE
B_VDIM = 64
B_WIDTH = B_HEADS * B_VDIM
B_Q_LORA = 384
B_KV_LORA = 256
B_ROPE_THETA = 10000.0
B_SCALE = B_QK_DIM ** -0.5

ADAM_LR = 0.001
ADAM_B1 = 0.9
ADAM_B2 = 0.999
ADAM_EPS = 1e-08
ADAM_WD = 0.01
ADAM_STEP = 10

N_CHIPS = 4
PACK_COLS = 512


def _params(sem=None):
    return pltpu.CompilerParams(dimension_semantics=sem, vmem_limit_bytes=VMEM_LIMIT)


def _tile(n, want):
    t = min(n, want)
    assert n % t == 0, (n, want)
    return t


def _row_tile(n, want):
    for t in range(min(n, want), 0, -1):
        if n % t == 0 and (t % 16 == 0 or t == n):
            return t
    return n


def _rope_tables(positions, theta, lane0):
    half = 16
    inv_freq = 1.0 / (theta ** (jnp.arange(half, dtype=F32) * (2.0 / (2 * half))))
    n = positions.size
    per_row = LANES // half
    pos = jnp.repeat(positions.astype(F32).reshape(n // per_row, per_row), half, axis=1)
    ang = pos * jnp.tile(inv_freq, per_row)
    cos, sin = lax.optimization_barrier((jnp.cos(ang), jnp.sin(ang)))
    cos, sin = cos.reshape(n, half), sin.reshape(n, half)
    pre = jnp.zeros((n, lane0), F32)
    post = jnp.zeros((n, LANES - lane0 - 2 * half), F32)
    z16 = jnp.zeros((n, half), F32)
    c = jnp.concatenate([pre + 1.0, cos, cos, post + 1.0], axis=1)
    sa = jnp.concatenate([pre, -sin, z16, post], axis=1)
    sb = jnp.concatenate([pre, z16, sin, post], axis=1)
    return lax.optimization_barrier((c, sa, sb))


def _rope_apply(x, c, sa, sb, sign):
    k = x.shape[1] // LANES
    if k > 1:
        c, sa, sb = (jnp.concatenate([t] * k, axis=1) for t in (c, sa, sb))
    w = x.shape[1]
    up = pltpu.roll(x, w - 16, 1)
    dn = pltpu.roll(x, 16, 1)
    if sign > 0:
        return x * c + up * sa + dn * sb
    return x * c - up * sa - dn * sb


def _matmul(a, b, mode, out_dtype, *, name, tm=512, tn=1024, tk=1024, add=None, rope=None,
            out_scale=None, b_koff=0, b_cols=None, out_into=None, out_full=None, out_joff=0,
            out_chunk_blocks=None):
    if mode == "nn":
        m, k = a.shape
        n = b.shape[1]
    elif mode == "nt":
        m, k = a.shape
        n = b.shape[0]
    else:
        k, m = a.shape
        n = b.shape[1]
    b_j0 = 0
    if b_cols is not None:
        tn = _tile(n, tn)
        b_j0, n = b_cols[0], b_cols[1] * tn
    tm, tn, tk = _tile(m, tm), _tile(n, tn), _tile(k, tk)
    nk = k // tk
    if mode == "nn":
        a_spec = pl.BlockSpec((tm, tk), lambda j, i, kk: (i, kk))
        b_spec = pl.BlockSpec((tk, tn), lambda j, i, kk: (kk, j + b_j0))
        dims = (((1,), (0,)), ((), ()))
    elif mode == "nt":
        a_spec = pl.BlockSpec((tm, tk), lambda j, i, kk: (i, kk))
        b_spec = pl.BlockSpec((tn, tk), lambda j, i, kk: (j, kk + b_koff))
        dims = (((1,), (1,)), ((), ()))
    else:
        a_spec = pl.BlockSpec((tk, tm), lambda j, i, kk: (kk, i))
        b_spec = pl.BlockSpec((tk, tn), lambda j, i, kk: (kk, j))
        dims = (((0,), (0,)), ((), ()))
    operands = [a, b]
    in_specs = [a_spec, b_spec]
    if add is not None:
        operands.append(add)
        in_specs.append(pl.BlockSpec((tm, tn), lambda j, i, kk: (i, j)))
    if rope is not None:
        tables, rope_pred = rope
        for t in tables:
            operands.append(t)
            in_specs.append(pl.BlockSpec((tm, LANES), lambda j, i, kk: (i, 0)))
    aliases = {}
    if out_into is not None:
        aliases = {len(operands): 0}
        operands.append(out_into)
        in_specs.append(pl.BlockSpec(memory_space=pl.ANY))
        out_shape = jax.ShapeDtypeStruct(out_into.shape, out_into.dtype)
    elif out_full is not None:
        out_shape = jax.ShapeDtypeStruct(out_full, out_dtype)
    else:
        out_shape = jax.ShapeDtypeStruct((m, n), out_dtype)
    if out_chunk_blocks is not None:
        out_spec = pl.BlockSpec((None, tm, tn), lambda j, i, kk: ((j + out_joff) // out_chunk_blocks, i,
                                                                  (j + out_joff) % out_chunk_blocks))
    else:
        out_spec = pl.BlockSpec((tm, tn), lambda j, i, kk: (i, j + out_joff))

    def body(*refs):
        a_ref, b_ref = refs[0], refs[1]
        pos = 2
        add_ref = None
        if add is not None:
            add_ref = refs[pos]
            pos += 1
        tab_refs = None
        if rope is not None:
            tab_refs = refs[pos:pos + 3]
            pos += 3
        if out_into is not None:
            pos += 1
        o_ref = refs[pos]
        acc_ref = refs[pos + 1] if nk > 1 else None

        def finish(res):
            if add_ref is not None:
                res = res + add_ref[...].astype(F32)
            if tab_refs is None:
                o_ref[...] = res.astype(o_ref.dtype)
                return
            j = pl.program_id(0)
            flag = rope_pred(j)
            roped = _rope_apply(res, tab_refs[0][...], tab_refs[1][...], tab_refs[2][...], 1)
            if out_scale is not None:
                value, scale_pred = out_scale
                use = scale_pred(j)
                roped = roped * (value if use is True else jnp.where(use, value, 1.0))
            if flag is True:
                o_ref[...] = roped.astype(o_ref.dtype)
                return

            @pl.when(flag)
            def _():
                o_ref[...] = roped.astype(o_ref.dtype)

            @pl.when(jnp.logical_not(flag))
            def _():
                o_ref[...] = res.astype(o_ref.dtype)

        part = lax.dot_general(a_ref[...].astype(BF16), b_ref[...].astype(BF16), dims,
                               preferred_element_type=F32)
        if nk == 1:
            finish(part)
            return
        kk = pl.program_id(2)

        @pl.when(kk == 0)
        def _():
            acc_ref[...] = part

        @pl.when(kk > 0)
        def _():
            acc_ref[...] += part

        @pl.when(kk == nk - 1)
        def _():
            finish(acc_ref[...])

    return pl.pallas_call(
        body, name=name, grid=(n // tn, m // tm, nk), in_specs=in_specs, out_specs=out_spec,
        out_shape=out_shape, input_output_aliases=aliases,
        scratch_shapes=[pltpu.VMEM((tm, tn), F32)] if nk > 1 else [],
        compiler_params=_params(("parallel", "parallel", "arbitrary")),
    )(*operands)


def _rms_fwd(x, g, out_dtype, *, name, add=None, tr=512):
    n, d = x.shape
    tr = _tile(n, tr)
    row = pl.BlockSpec((tr, d), lambda i: (i, 0))
    vec = pl.BlockSpec((1, d), lambda i: (0, 0))

    def body(*refs):
        x_ref, g_ref = refs[0], refs[1]
        o_ref = refs[-1]
        xv = x_ref[...].astype(F32)
        r = lax.rsqrt(jnp.mean(xv * xv, axis=-1, keepdims=True) + NORM_EPS)
        y = xv * r * g_ref[...]
        if add is not None:
            y = refs[2][...] + y
        o_ref[...] = y.astype(o_ref.dtype)

    ops = [x, g] + ([add] if add is not None else [])
    specs = [row, vec] + ([row] if add is not None else [])
    return pl.pallas_call(
        body, name=name, grid=(n // tr,), in_specs=specs, out_specs=row,
        out_shape=jax.ShapeDtypeStruct((n, d), out_dtype), compiler_params=_params(("parallel",)),
    )(*ops)


def _rms_bwd(x, g, dy, out_dtype, *, name, adds=(), dy_more=(), tr=512):
    n, d = x.shape
    tr = _tile(n, tr)
    steps = n // tr
    row = pl.BlockSpec((tr, d), lambda i: (i, 0))
    vec = pl.BlockSpec((1, d), lambda i: (0, 0))
    na = len(adds) + len(dy_more)

    def body(*refs):
        x_ref, g_ref, dy_ref = refs[:3]
        add_refs = refs[3:3 + len(adds)]
        more_refs = refs[3 + len(adds):3 + na]
        dx_ref, dg_ref, acc_ref = refs[3 + na:]
        i = pl.program_id(0)
        xv = x_ref[...].astype(F32)
        r = lax.rsqrt(jnp.mean(xv * xv, axis=-1, keepdims=True) + NORM_EPS)
        xh = xv * r
        dyv = dy_ref[...].astype(F32)
        for m_ref in more_refs:
            dyv = dyv + m_ref[...].astype(F32)
        part = (dyv * xh).reshape(tr // 8, 8, d).sum(axis=0)

        @pl.when(i == 0)
        def _():
            acc_ref[...] = part

        @pl.when(i > 0)
        def _():
            acc_ref[...] += part

        t = dyv * g_ref[...]
        dx = r * (t - xh * jnp.mean(t * xh, axis=-1, keepdims=True))
        for a_ref in add_refs:
            dx = dx + a_ref[...].astype(F32)
        dx_ref[...] = dx.astype(dx_ref.dtype)

        @pl.when(i == steps - 1)
        def _():
            dg_ref[...] = jnp.sum(acc_ref[...], axis=0, keepdims=True)

    return pl.pallas_call(
        body, name=name, grid=(steps,), in_specs=[row, vec, row] + [row] * na,
        out_specs=(row, vec),
        out_shape=(jax.ShapeDtypeStruct((n, d), out_dtype), jax.ShapeDtypeStruct((1, d), F32)),
        scratch_shapes=[pltpu.VMEM((8, d), F32)], compiler_params=_params(("arbitrary",)),
    )(x, g, dy, *adds, *dy_more)


def _kv_latent_fwd(ckr, g_lat, tabs, *, tr=512):
    n = ckr.shape[0]
    tr = _tile(n, tr)
    lat = B_KV_LORA

    def body(c_ref, k_ref, g_ref, tc, tsa, tsb, ckv_ref, kr_ref):
        xv = c_ref[...]
        r = lax.rsqrt(jnp.mean(xv * xv, axis=-1, keepdims=True) + NORM_EPS)
        ckv_ref[...] = (xv * r * g_ref[...]).astype(BF16)
        kr_ref[...] = _rope_apply(k_ref[...], tc[...], tsa[...], tsb[...], 1).astype(BF16)

    tab = pl.BlockSpec((tr, LANES), lambda i: (i, 0))
    return pl.pallas_call(
        body, name="kv_latent_fwd", grid=(n // tr,),
        in_specs=[pl.BlockSpec((tr, lat), lambda i: (i, 0)),
                  pl.BlockSpec((tr, LANES), lambda i: (i, lat // LANES)),
                  pl.BlockSpec((1, lat), lambda i: (0, 0)), tab, tab, tab],
        out_specs=(pl.BlockSpec((tr, lat), lambda i: (i, 0)), tab),
        out_shape=(jax.ShapeDtypeStruct((n, lat), BF16), jax.ShapeDtypeStruct((n, LANES), BF16)),
        compiler_params=_params(("parallel",)),
    )(ckr, ckr, g_lat, *tabs)


def _kv_latent_bwd(dckv, ckr, g_lat, dk_cat, tabs, *, tr=512):
    n = ckr.shape[0]
    tr = _tile(n, tr)
    steps = n // tr
    lat = B_KV_LORA
    wk = dk_cat.shape[1]

    def body(d_ref, c_ref, g_ref, dk_ref, tc, tsa, tsb, o_ref, dg_ref, acc_ref):
        i = pl.program_id(0)
        xv = c_ref[...]
        r = lax.rsqrt(jnp.mean(xv * xv, axis=-1, keepdims=True) + NORM_EPS)
        xh = xv * r
        dyv = d_ref[...]
        part = (dyv * xh).reshape(tr // 8, 8, lat).sum(axis=0)

        @pl.when(i == 0)
        def _():
            acc_ref[...] = part

        @pl.when(i > 0)
        def _():
            acc_ref[...] += part

        t = dyv * g_ref[...]
        dx = r * (t - xh * jnp.mean(t * xh, axis=-1, keepdims=True))
        o_ref[:, 0:lat] = dx.astype(o_ref.dtype)
        dkr = dk_ref[:, 0:LANES].astype(F32)
        for h in range(1, wk // LANES):
            dkr = dkr + dk_ref[:, h * LANES:(h + 1) * LANES].astype(F32)
        o_ref[:, lat:lat + LANES] = _rope_apply(dkr, tc[...], tsa[...], tsb[...], -1).astype(o_ref.dtype)

        @pl.when(i == steps - 1)
        def _():
            dg_ref[...] = jnp.sum(acc_ref[...], axis=0, keepdims=True)

    tab = pl.BlockSpec((tr, LANES), lambda i: (i, 0))
    return pl.pallas_call(
        body, name="kv_latent_bwd", grid=(steps,),
        in_specs=[pl.BlockSpec((tr, lat), lambda i: (i, 0)), pl.BlockSpec((tr, lat), lambda i: (i, 0)),
                  pl.BlockSpec((1, lat), lambda i: (0, 0)), pl.BlockSpec((tr, wk), lambda i: (i, 0)),
                  tab, tab, tab],
        out_specs=(pl.BlockSpec((tr, lat + LANES), lambda i: (i, 0)), pl.BlockSpec((1, lat), lambda i: (0, 0))),
        out_shape=(jax.ShapeDtypeStruct((n, lat + LANES), BF16), jax.ShapeDtypeStruct((1, lat), F32)),
        scratch_shapes=[pltpu.VMEM((8, lat), F32)], compiler_params=_params(("arbitrary",)),
    )(dckv, ckr, g_lat, dk_cat, *tabs)


def _sigmoid(z):
    return 1.0 / (1.0 + jnp.exp(-z))


def _lane_place(cols, width):
    rows = cols[0].shape[0]
    lane = lax.broadcasted_iota(jnp.int32, (rows, width), 1)
    out = jnp.zeros((rows, width), F32)
    for h, col in enumerate(cols):
        out = jnp.where(lane == h, col, out)
    return out


def _merge_gate_fwd(outs, lses, proj, z_block, *, tr=256):
    n, w = outs[0].shape
    tr = _tile(n, tr)
    ng = len(outs)

    def body(*refs):
        o_refs = refs[:ng]
        l_refs = refs[ng:2 * ng]
        z_ref = refs[2 * ng]
        y_ref, om_ref, lse_ref = refs[2 * ng + 1:]
        ls = [r[...] for r in l_refs]
        mx = ls[0]
        for l in ls[1:]:
            mx = jnp.maximum(mx, l)
        ssum = jnp.exp2(ls[0] - mx)
        for l in ls[1:]:
            ssum = ssum + jnp.exp2(l - mx)
        tot = mx + jnp.log2(ssum)
        lse_ref[...] = tot
        ws = [jnp.exp2(l - tot) for l in ls]
        for h in range(A_HEADS):
            sl = slice(h * A_HEAD_DIM, (h + 1) * A_HEAD_DIM)
            o = ws[0][:, h:h + 1] * o_refs[0][:, sl]
            for gi in range(1, ng):
                o = o + ws[gi][:, h:h + 1] * o_refs[gi][:, sl]
            z = z_ref[:, sl].astype(F32)
            om_ref[:, sl] = o.astype(BF16)
            y_ref[:, sl] = (o * (z * _sigmoid(z))).astype(BF16)

    row = pl.BlockSpec((tr, w), lambda i: (i, 0))
    lrow = pl.BlockSpec((tr, A_HEADS), lambda i: (i, 0))
    return pl.pallas_call(
        body, name="merge_gate_fwd", grid=(n // tr,),
        in_specs=[row] * ng + [lrow] * ng + [pl.BlockSpec((tr, w), lambda i: (i, z_block))],
        out_specs=(row, row, lrow),
        out_shape=(jax.ShapeDtypeStruct((n, w), BF16), jax.ShapeDtypeStruct((n, w), BF16),
                   jax.ShapeDtypeStruct((n, A_HEADS), F32)),
        compiler_params=_params(("parallel",)),
    )(*outs, *lses, proj)


def _gate_bwd(dy, o, z_arr, z_block, *, name, with_delta, tr=256):
    n, w = dy.shape
    tr = _tile(n, tr)

    def body(*refs):
        dy_ref, o_ref, z_ref, do_ref, dz_ref = refs[:5]
        dyv = dy_ref[...].astype(F32)
        ov = o_ref[...].astype(F32)
        z = z_ref[...].astype(F32)
        sig = _sigmoid(z)
        do = dyv * (z * sig)
        do_ref[...] = do.astype(BF16)
        dz_ref[...] = (dyv * ov * (sig * (1.0 + z * (1.0 - sig)))).astype(BF16)
        if with_delta:
            prod = do * ov
            cols = [jnp.sum(prod[:, h * A_HEAD_DIM:(h + 1) * A_HEAD_DIM], axis=-1, keepdims=True)
                    for h in range(A_HEADS)]
            refs[5][...] = _lane_place(cols, A_HEADS)

    row = pl.BlockSpec((tr, w), lambda i: (i, 0))
    out_specs = [row, row]
    out_shape = [jax.ShapeDtypeStruct((n, w), BF16), jax.ShapeDtypeStruct((n, w), BF16)]
    if with_delta:
        out_specs.append(pl.BlockSpec((tr, A_HEADS), lambda i: (i, 0)))
        out_shape.append(jax.ShapeDtypeStruct((n, A_HEADS), F32))
    return pl.pallas_call(
        body, name=name, grid=(n // tr,),
        in_specs=[row, row, pl.BlockSpec((tr, w), lambda i: (i, z_block))],
        out_specs=tuple(out_specs), out_shape=tuple(out_shape), compiler_params=_params(("parallel",)),
    )(dy, o, z_arr)


def _loss_fwd_bwd(h, target, *, tr=512):
    n, d = h.shape
    tr = _tile(n, tr)
    steps = n // tr

    def body(h_ref, t_ref, dh_ref, loss_ref, acc_ref):
        i = pl.program_id(0)
        e = h_ref[...] - t_ref[...]
        dh_ref[...] = e / d
        part = (e * e).reshape(tr // 8, 8, d).sum(axis=0)

        @pl.when(i == 0)
        def _():
            acc_ref[...] = part

        @pl.when(i > 0)
        def _():
            acc_ref[...] += part

        @pl.when(i == steps - 1)
        def _():
            s = jnp.sum(jnp.sum(acc_ref[...], axis=-1, keepdims=True), axis=0, keepdims=True)
            loss_ref[...] = 0.5 * s / d

    row = pl.BlockSpec((tr, d), lambda i: (i, 0))
    return pl.pallas_call(
        body, name="loss", grid=(steps,), in_specs=[row, row],
        out_specs=(row, pl.BlockSpec((1, 1), lambda i: (0, 0))),
        out_shape=(jax.ShapeDtypeStruct((n, d), F32), jax.ShapeDtypeStruct((1, 1), F32)),
        scratch_shapes=[pltpu.VMEM((8, d), F32)], compiler_params=_params(("arbitrary",)),
    )(h, target)


def _dot_nt(a, b):
    return lax.dot_general(a, b, (((1,), (1,)), ((), ())), preferred_element_type=F32)


def _dot_nn(a, b):
    return lax.dot_general(a, b, (((1,), (0,)), ((), ())), preferred_element_type=F32)


def _attn_a_fwd(qkv, cb0, qb, *, name):
    bl, dil, ln, _ = qkv.shape
    nb = ln // qb
    hw = A_WIDTH
    heads = range(A_HEADS)
    sls = [slice(h * A_HEAD_DIM, (h + 1) * A_HEAD_DIM) for h in heads]

    def body(q_ref, kc_ref, kp_ref, vc_ref, vp_ref, o_ref, lse_ref):
        i = pl.program_id(2)
        qi = lax.broadcasted_iota(jnp.int32, (qb, qb), 0)
        ki = lax.broadcasted_iota(jnp.int32, (qb, qb), 1)
        mask_c = ki <= qi
        mask_p = jnp.logical_and(ki >= qi, i >= 1)
        s_c = [jnp.where(mask_c, _dot_nt(q_ref[:, sls[h]], kc_ref[:, sls[h]]), NEG) for h in heads]
        m = [jnp.max(s_c[h], axis=-1, keepdims=True) for h in heads]
        if nb > 1:
            s_p = [jnp.where(mask_p, _dot_nt(q_ref[:, sls[h]], kp_ref[:, sls[h]]), NEG) for h in heads]
            m = [jnp.maximum(m[h], jnp.max(s_p[h], axis=-1, keepdims=True)) for h in heads]
        p_c = [jnp.exp2(s_c[h] - m[h]) for h in heads]
        l = [jnp.sum(p_c[h], axis=-1, keepdims=True) for h in heads]
        acc = [_dot_nn(p_c[h].astype(BF16), vc_ref[:, sls[h]]) for h in heads]
        if nb > 1:
            p_p = [jnp.exp2(s_p[h] - m[h]) for h in heads]
            l = [l[h] + jnp.sum(p_p[h], axis=-1, keepdims=True) for h in heads]
            acc = [acc[h] + _dot_nn(p_p[h].astype(BF16), vp_ref[:, sls[h]]) for h in heads]
        for h in heads:
            o_ref[:, sls[h]] = acc[h] / l[h]
        lse_ref[...] = _lane_place([m[h] + jnp.log2(l[h]) for h in heads], A_HEADS)

    def spec(off, prev):
        if prev:
            return pl.BlockSpec((None, None, qb, hw), lambda b, r, i: (b, r, jnp.maximum(i - 1, 0), cb0 + off))
        return pl.BlockSpec((None, None, qb, hw), lambda b, r, i: (b, r, i, cb0 + off))

    return pl.pallas_call(
        body, name=name, grid=(bl, dil, nb),
        in_specs=[spec(0, False), spec(1, False), spec(1, True), spec(2, False), spec(2, True)],
        out_specs=(pl.BlockSpec((None, None, qb, hw), lambda b, r, i: (b, r, i, 0)),
                   pl.BlockSpec((None, None, qb, A_HEADS), lambda b, r, i: (b, r, i, 0))),
        out_shape=(jax.ShapeDtypeStruct((bl, dil, ln, hw), F32),
                   jax.ShapeDtypeStruct((bl, dil, ln, A_HEADS), F32)),
        compiler_params=_params(("parallel", "parallel", "arbitrary")),
    )(qkv, qkv, qkv, qkv, qkv)


def _attn_a_bwd(qkv, cb0, do, lse, delta, lse_t, delta_t, tabs, qb, *, name):
    bl, dil, ln, _ = qkv.shape
    nb = ln // qb
    hw = A_WIDTH

    def body(q_ref, qn_ref, kc_ref, kp_ref, vc_ref, vp_ref, do_ref, don_ref,
             lse_ref, dl_ref, lt_ref, ltn_ref, dt_ref, dtn_ref, tc, tsa, tsb, o_ref):
        i = pl.program_id(2)
        row = lax.broadcasted_iota(jnp.int32, (qb, qb), 0)
        col = lax.broadcasted_iota(jnp.int32, (qb, qb), 1)
        m_qc = col <= row
        m_kc = row <= col
        m_qp = jnp.logical_and(col >= row, i >= 1)
        m_kn = jnp.logical_and(row >= col, i + 1 < nb)
        c, sa, sb = tc[...], tsa[...], tsb[...]
        heads = range(A_HEADS)
        sls = [slice(h * A_HEAD_DIM, (h + 1) * A_HEAD_DIM) for h in heads]
        q, kc = [q_ref[:, sl] for sl in sls], [kc_ref[:, sl] for sl in sls]
        vc, dov = [vc_ref[:, sl] for sl in sls], [do_ref[:, sl] for sl in sls]
        lse_c = [lse_ref[:, h:h + 1] for h in heads]
        dl_c = [dl_ref[:, h:h + 1] for h in heads]
        s = [_dot_nt(q[h], kc[h]) for h in heads]
        st = [_dot_nt(kc[h], q[h]) for h in heads]
        dp = [_dot_nt(dov[h], vc[h]) for h in heads]
        dpt = [_dot_nt(vc[h], dov[h]) for h in heads]
        p = [jnp.exp2(jnp.where(m_qc, s[h], NEG) - lse_c[h]) for h in heads]
        pt = [jnp.exp2(jnp.where(m_kc, st[h], NEG) - lt_ref[h:h + 1, :]) for h in heads]
        dq = [_dot_nn((p[h] * (dp[h] - dl_c[h])).astype(BF16), kc[h]) for h in heads]
        dk = [_dot_nn((pt[h] * (dpt[h] - dt_ref[h:h + 1, :])).astype(BF16), q[h]) for h in heads]
        dv = [_dot_nn(pt[h].astype(BF16), dov[h]) for h in heads]
        if nb > 1:
            kp, vp = [kp_ref[:, sl] for sl in sls], [vp_ref[:, sl] for sl in sls]
            qn, don = [qn_ref[:, sl] for sl in sls], [don_ref[:, sl] for sl in sls]
            s = [_dot_nt(q[h], kp[h]) for h in heads]
            st = [_dot_nt(kc[h], qn[h]) for h in heads]
            dp = [_dot_nt(dov[h], vp[h]) for h in heads]
            dpt = [_dot_nt(vc[h], don[h]) for h in heads]
            p = [jnp.exp2(jnp.where(m_qp, s[h], NEG) - lse_c[h]) for h in heads]
            pt = [jnp.exp2(jnp.where(m_kn, st[h], NEG) - ltn_ref[h:h + 1, :]) for h in heads]
            dq = [dq[h] + _dot_nn((p[h] * (dp[h] - dl_c[h])).astype(BF16), kp[h]) for h in heads]
            dk = [dk[h] + _dot_nn((pt[h] * (dpt[h] - dtn_ref[h:h + 1, :])).astype(BF16), qn[h]) for h in heads]
            dv = [dv[h] + _dot_nn(pt[h].astype(BF16), don[h]) for h in heads]
        for h in heads:
            o_ref[:, h * A_HEAD_DIM:(h + 1) * A_HEAD_DIM] = _rope_apply(dq[h] * A_SCALE, c, sa, sb, -1).astype(BF16)
            o_ref[:, hw + h * A_HEAD_DIM:hw + (h + 1) * A_HEAD_DIM] = _rope_apply(dk[h] * LN2, c, sa, sb, -1).astype(BF16)
            o_ref[:, 2 * hw + h * A_HEAD_DIM:2 * hw + (h + 1) * A_HEAD_DIM] = dv[h].astype(BF16)

    def cur(w, col):
        return pl.BlockSpec((None, None, qb, w), lambda b, r, i: (b, r, i, col))

    def prev(w, col):
        return pl.BlockSpec((None, None, qb, w), lambda b, r, i: (b, r, jnp.maximum(i - 1, 0), col))

    def nxt(w, col):
        return pl.BlockSpec((None, None, qb, w), lambda b, r, i: (b, r, jnp.minimum(i + 1, nb - 1), col))

    t_cur = pl.BlockSpec((None, None, A_HEADS, qb), lambda b, r, i: (b, r, 0, i))
    t_nxt = pl.BlockSpec((None, None, A_HEADS, qb), lambda b, r, i: (b, r, 0, jnp.minimum(i + 1, nb - 1)))
    return pl.pallas_call(
        body, name=name, grid=(bl, dil, nb),
        in_specs=[cur(hw, cb0), nxt(hw, cb0), cur(hw, cb0 + 1), prev(hw, cb0 + 1),
                  cur(hw, cb0 + 2), prev(hw, cb0 + 2), cur(hw, 0), nxt(hw, 0),
                  cur(A_HEADS, 0), cur(A_HEADS, 0), t_cur, t_nxt, t_cur, t_nxt,
                  cur(LANES, 0), cur(LANES, 0), cur(LANES, 0)],
        out_specs=cur(3 * hw, 0),
        out_shape=jax.ShapeDtypeStruct((bl, dil, ln, 3 * hw), BF16),
        compiler_params=_params(("parallel", "parallel", "arbitrary")),
    )(qkv, qkv, qkv, qkv, qkv, qkv, do, do, lse, delta, lse_t, lse_t, delta_t, delta_t, *tabs)


def _head_terms(do, o, lse, e):
    rows = do.shape[0]
    lane = lax.broadcasted_iota(jnp.int32, (rows, LANES), 1)
    mine = (lane < B_VDIM) if e == 0 else (lane >= B_VDIM)
    prod = do.astype(F32) * o.astype(F32)
    dl = jnp.sum(jnp.where(mine, prod, 0.0), axis=-1, keepdims=True)
    do_e = jnp.where(mine, do, jnp.zeros_like(do))
    return do_e, dl, lse[:, e * B_VDIM:e * B_VDIM + 1]


def _col_to_row(col, rows):
    return jnp.transpose(jnp.broadcast_to(col, (rows, LANES)))[0:1, :]


def _mla_fwd(q_cat, kvup, kr, z, tq):
    bl, t, _ = q_cat.shape
    nq = t // tq
    pairs = B_HEADS // 2
    v_blk0 = (B_HEADS * LANES) // LANES

    def body(q_ref, k_ref, v_ref, kr_ref, z_ref, y_ref, o_ref, lse_ref, lrow_ref, m_ref, acc_ref):
        qi = pl.program_id(2)
        qs = [q_ref[:, e * LANES:(e + 1) * LANES] for e in range(2)]
        row = lax.broadcasted_iota(jnp.int32, (tq, tq), 0)
        col = lax.broadcasted_iota(jnp.int32, (tq, tq), 1)
        tri = col <= row
        sum_lane = [B_VDIM, 0]

        for e in range(2):
            m_ref[e] = jnp.full((tq, LANES), NEG, F32)
            acc_ref[e] = jnp.zeros((tq, LANES), F32)

        def tile(k0, w, masked):
            lane = lax.broadcasted_iota(jnp.int32, (w, LANES), 1)
            first = lane < B_VDIM
            krv = kr_ref[pl.ds(k0, w), :]
            v = v_ref[pl.ds(k0, w), :]
            vs = [jnp.where(first, v, jnp.where(lane == B_VDIM, 1.0, 0.0).astype(BF16)),
                  jnp.where(first, jnp.where(lane == 0, 1.0, 0.0).astype(BF16), v)]
            ss = []
            for e in range(2):
                k = k_ref[pl.ds(k0, w), e * LANES:(e + 1) * LANES] + krv
                s = _dot_nt(qs[e], k)
                ss.append(jnp.where(tri, s, NEG) if masked else s)
            m_old = [m_ref[e] for e in range(2)]
            ms = [jnp.maximum(m_old[e], jnp.max(ss[e], axis=-1, keepdims=True)) for e in range(2)]
            ps = [jnp.exp2(ss[e] - jnp.concatenate([ms[e]] * (w // LANES), axis=1)).astype(BF16) for e in range(2)]
            for e in range(2):
                m_ref[e] = ms[e]
                acc_ref[e] = jnp.exp2(m_old[e] - ms[e]) * acc_ref[e] + _dot_nn(ps[e], vs[e])

        def step(kb2, carry):
            tile(pl.multiple_of(kb2 * 2 * tq, 2 * tq), 2 * tq, False)
            return carry

        lax.fori_loop(0, qi // 2, step, 0)

        @pl.when(qi % 2 == 1)
        def _():
            tile(pl.multiple_of((qi - 1) * tq, tq), tq, False)

        tile(pl.multiple_of(qi * tq, tq), tq, True)
        lane = lax.broadcasted_iota(jnp.int32, (tq, LANES), 1)
        first = lane < B_VDIM
        accs = [acc_ref[e] for e in range(2)]
        ls = [accs[e][:, sum_lane[e]:sum_lane[e] + 1] for e in range(2)]
        outs = [accs[e] / ls[e] for e in range(2)]
        lses = [m_ref[e] + jnp.log2(ls[e]) for e in range(2)]
        o = jnp.where(first, outs[0], outs[1])
        zv = z_ref[...].astype(F32)
        o_ref[...] = o.astype(BF16)
        y_ref[...] = (o * (zv * _sigmoid(zv))).astype(BF16)
        lse_ref[...] = jnp.where(first, lses[0], lses[1])
        for e in range(2):
            lrow_ref[e:e + 1, :] = jnp.transpose(lses[e])[0:1, :]

    blk = pl.BlockSpec((None, tq, LANES), lambda b, j, i: (b, i, j))
    return pl.pallas_call(
        body, name="mla_fwd", grid=(bl, pairs, nq),
        in_specs=[pl.BlockSpec((None, tq, 2 * LANES), lambda b, j, i: (b, i, j)),
                  pl.BlockSpec((None, t, 2 * LANES), lambda b, j, i: (b, 0, j)),
                  pl.BlockSpec((None, t, LANES), lambda b, j, i: (b, 0, v_blk0 + j)),
                  pl.BlockSpec((None, t, LANES), lambda b, j, i: (b, 0, 0)),
                  blk],
        out_specs=(blk, blk, blk, pl.BlockSpec((None, None, None, 2, tq), lambda b, j, i: (b, j, i, 0, 0))),
        out_shape=(jax.ShapeDtypeStruct((bl, t, B_WIDTH), BF16), jax.ShapeDtypeStruct((bl, t, B_WIDTH), BF16),
                   jax.ShapeDtypeStruct((bl, t, B_WIDTH), F32),
                   jax.ShapeDtypeStruct((bl, pairs, nq, 2, tq), F32)),
        scratch_shapes=[pltpu.VMEM((2, tq, LANES), F32), pltpu.VMEM((2, tq, LANES), F32)],
        compiler_params=_params(("parallel", "parallel", "arbitrary")),
    )(q_cat, kvup, kvup, kr, z)


def _mla_dq(q_cat, kvup, kr, do, o, lse, tabs, tq):
    bl, t, _ = q_cat.shape
    nq = t // tq
    pairs = B_HEADS // 2
    v_blk0 = (B_HEADS * LANES) // LANES

    def body(q_ref, k_ref, v_ref, kr_ref, do_ref, o_ref, lse_ref, tc, tsa, tsb, dq_ref, drow_ref, acc_ref):
        qi = pl.program_id(2)
        dov, ov, lsev = do_ref[...], o_ref[...], lse_ref[...]
        qs = [q_ref[:, e * LANES:(e + 1) * LANES] for e in range(2)]
        terms = [_head_terms(dov, ov, lsev, e) for e in range(2)]
        row = lax.broadcasted_iota(jnp.int32, (tq, tq), 0)
        col = lax.broadcasted_iota(jnp.int32, (tq, tq), 1)
        tri = col <= row
        for e in range(2):
            acc_ref[e] = jnp.zeros((tq, LANES), F32)

        def tile(k0, w, masked):
            krv = kr_ref[pl.ds(k0, w), :]
            v = v_ref[pl.ds(k0, w), :]
            ks = [k_ref[pl.ds(k0, w), e * LANES:(e + 1) * LANES] + krv for e in range(2)]
            ss = [_dot_nt(qs[e], ks[e]) for e in range(2)]
            dps = [_dot_nt(terms[e][0], v) for e in range(2)]
            for e in range(2):
                s = jnp.where(tri, ss[e], NEG) if masked else ss[e]
                p = jnp.exp2(s - terms[e][2])
                ds = (p * (dps[e] - terms[e][1])).astype(BF16)
                acc_ref[e] += _dot_nn(ds, ks[e])

        def step(kb2, carry):
            tile(pl.multiple_of(kb2 * 2 * tq, 2 * tq), 2 * tq, False)
            return carry

        lax.fori_loop(0, qi // 2, step, 0)

        @pl.when(qi % 2 == 1)
        def _():
            tile(pl.multiple_of((qi - 1) * tq, tq), tq, False)

        tile(pl.multiple_of(qi * tq, tq), tq, True)
        for e in range(2):
            dq_ref[:, e * LANES:(e + 1) * LANES] = _rope_apply(acc_ref[e] * B_SCALE, tc[...], tsa[...], tsb[...], -1).astype(BF16)
            drow_ref[e:e + 1, :] = _col_to_row(terms[e][1], tq)

    blk = pl.BlockSpec((None, tq, LANES), lambda b, j, i: (b, i, j))
    tab = pl.BlockSpec((None, tq, LANES), lambda b, j, i: (b, i, 0))
    qblk = pl.BlockSpec((None, tq, 2 * LANES), lambda b, j, i: (b, i, j))
    return pl.pallas_call(
        body, name="mla_dq", grid=(bl, pairs, nq),
        in_specs=[qblk,
                  pl.BlockSpec((None, t, 2 * LANES), lambda b, j, i: (b, 0, j)),
                  pl.BlockSpec((None, t, LANES), lambda b, j, i: (b, 0, v_blk0 + j)),
                  pl.BlockSpec((None, t, LANES), lambda b, j, i: (b, 0, 0)),
                  blk, blk, blk, tab, tab, tab],
        out_specs=(qblk, pl.BlockSpec((None, None, None, 2, tq), lambda b, j, i: (b, j, i, 0, 0))),
        out_shape=(jax.ShapeDtypeStruct((bl, t, B_HEADS * LANES), BF16),
                   jax.ShapeDtypeStruct((bl, pairs, nq, 2, tq), F32)),
        scratch_shapes=[pltpu.VMEM((2, tq, LANES), F32)],
        compiler_params=_params(("parallel", "parallel", "arbitrary")),
    )(q_cat, kvup, kvup, kr, do, o, lse, *tabs)


def _mla_dkv(q_cat, kvup, kr, do, lse_rows, delta_rows, tq):
    bl, t, _ = q_cat.shape
    nq = t // tq
    pairs = B_HEADS // 2
    v_blk0 = (B_HEADS * LANES) // LANES

    def body(q_ref, k_ref, v_ref, kr_ref, do_ref, lrow_ref, drow_ref, dk_ref, dv_ref, acc_ref):
        kb = pl.program_id(2)
        v = v_ref[...]
        krv = kr_ref[...]
        ks = [k_ref[:, e * LANES:(e + 1) * LANES] + krv for e in range(2)]
        krow = lax.broadcasted_iota(jnp.int32, (tq, tq), 0)
        qcol = lax.broadcasted_iota(jnp.int32, (tq, tq), 1)
        tri = krow <= qcol
        lane = lax.broadcasted_iota(jnp.int32, (tq, LANES), 1)
        mine = [lane < B_VDIM, lane >= B_VDIM]

        for e in range(3):
            acc_ref[e] = jnp.zeros((tq, LANES), F32)

        def tile(qb, nblk, masked):
            w = nblk * tq
            rows = pl.ds(pl.multiple_of(qb * tq, tq), w)
            dov = do_ref[rows, :]
            lane_w = lax.broadcasted_iota(jnp.int32, (w, LANES), 1)
            mine_w = [lane_w < B_VDIM, lane_w >= B_VDIM]
            qs = [q_ref[rows, e * LANES:(e + 1) * LANES] for e in range(2)]
            does = [jnp.where(mine_w[e], dov, jnp.zeros_like(dov)) for e in range(2)]
            sts = [_dot_nt(ks[e], qs[e]) for e in range(2)]
            dpts = [_dot_nt(v, does[e]) for e in range(2)]

            def rows_of(ref, e):
                return jnp.concatenate([ref[qb + i, e:e + 1, :] for i in range(nblk)], axis=1)

            pts = []
            for e in range(2):
                st = jnp.where(tri, sts[e], NEG) if masked else sts[e]
                pts.append(jnp.exp2(st - rows_of(lrow_ref, e)))
            acc_ref[2] += _dot_nn(pts[0].astype(BF16), does[0]) + _dot_nn(pts[1].astype(BF16), does[1])
            for e in range(2):
                dst = (pts[e] * (dpts[e] - rows_of(drow_ref, e))).astype(BF16)
                acc_ref[e] += _dot_nn(dst, qs[e])

        tile(kb, 1, True)
        rest = nq - 1 - kb
        odd = rest % 2

        @pl.when(odd == 1)
        def _():
            tile(kb + 1, 1, False)

        def step(i, carry):
            tile(kb + 1 + odd + 2 * i, 2, False)
            return carry

        lax.fori_loop(0, rest // 2, step, 0)
        dk_ref[:, 0:LANES] = (acc_ref[0] * LN2).astype(BF16)
        dk_ref[:, LANES:2 * LANES] = (acc_ref[1] * LN2).astype(BF16)
        dv_ref[...] = acc_ref[2].astype(BF16)

    full = pl.BlockSpec((None, t, LANES), lambda b, j, i: (b, 0, j))
    rows = pl.BlockSpec((None, None, nq, 2, tq), lambda b, j, i: (b, j, 0, 0, 0))
    kblk = pl.BlockSpec((None, tq, 2 * LANES), lambda b, j, i: (b, i, j))
    return pl.pallas_call(
        body, name="mla_dkv", grid=(bl, pairs, nq),
        in_specs=[pl.BlockSpec((None, t, 2 * LANES), lambda b, j, i: (b, 0, j)),
                  kblk,
                  pl.BlockSpec((None, tq, LANES), lambda b, j, i: (b, i, v_blk0 + j)),
                  pl.BlockSpec((None, tq, LANES), lambda b, j, i: (b, i, 0)),
                  full, rows, rows],
        out_specs=(kblk, pl.BlockSpec((None, tq, LANES), lambda b, j, i: (b, i, j))),
        out_shape=(jax.ShapeDtypeStruct((bl, t, B_HEADS * LANES), BF16),
                   jax.ShapeDtypeStruct((bl, t, B_WIDTH), BF16)),
        scratch_shapes=[pltpu.VMEM((3, tq, LANES), F32)],
        compiler_params=_params(("parallel", "parallel", "arbitrary")),
    )(q_cat, kvup, kvup, kr, do, lse_rows, delta_rows)


def _adamw(w, g, m, v, *, name):
    r, c = w.shape
    tr = _row_tile(r, 256)
    c1 = 1.0 - ADAM_B1
    c2 = 1.0 - ADAM_B2
    bc1 = 1.0 - ADAM_B1 ** ADAM_STEP
    bc2 = 1.0 - ADAM_B2 ** ADAM_STEP

    def body(w_ref, g_ref, m_ref, v_ref, d_ref, nm_ref, nv_ref):
        gv = g_ref[...]
        nm = ADAM_B1 * m_ref[...] + c1 * gv
        nv = ADAM_B2 * v_ref[...] + c2 * (gv * gv)
        nm_ref[...] = nm
        nv_ref[...] = nv
        d_ref[...] = -ADAM_LR * ((nm / bc1) / (jnp.sqrt(nv / bc2) + ADAM_EPS) + ADAM_WD * w_ref[...])

    blk = pl.BlockSpec((tr, c), lambda i: (i, 0))
    sds = jax.ShapeDtypeStruct((r, c), F32)
    return pl.pallas_call(
        body, name=name, grid=(r // tr,), in_specs=[blk] * 4, out_specs=(blk,) * 3,
        out_shape=(sds,) * 3, compiler_params=_params(("parallel",)),
    )(w, g, m, v)


def _add_my_half(stacked, other, core, out_dtype, *, name):
    nch, a, c = stacked.shape
    h = a // 2
    tr = _row_tile(h, 256)
    nblk = h // tr

    def body(core_ref, s_ref, p_ref, o_ref):
        o_ref[...] = (s_ref[...] + p_ref[...]).astype(o_ref.dtype)

    return pl.pallas_call(
        body, name=name,
        grid_spec=pltpu.PrefetchScalarGridSpec(
            num_scalar_prefetch=1, grid=(nch, nblk),
            in_specs=[pl.BlockSpec((None, tr, c), lambda k, i, cr: (k, cr[0] * nblk + i, 0)),
                      pl.BlockSpec((None, tr, c), lambda k, i, cr: (k, i, 0))],
            out_specs=pl.BlockSpec((None, tr, c), lambda k, i, cr: (k, i, 0))),
        out_shape=jax.ShapeDtypeStruct((nch, h, c), out_dtype),
        compiler_params=_params(("parallel", "parallel")),
    )(core, stacked, other)


def _sum_chips(parts, own, chip, *, name):
    nch, h, c = parts.shape
    tr = _row_tile(h, 256)

    def body(chip_ref, p_ref, own_ref, o_ref):
        me = chip_ref[0]

        def slot(k):
            return jnp.where(me == k, own_ref[k].astype(F32), p_ref[k].astype(F32))

        acc = slot(0) + slot(1)
        for k in range(2, nch):
            acc = acc + slot(k)
        o_ref[...] = acc

    blk = pl.BlockSpec((nch, tr, c), lambda i, cr: (0, i, 0))
    return pl.pallas_call(
        body, name=name,
        grid_spec=pltpu.PrefetchScalarGridSpec(
            num_scalar_prefetch=1, grid=(h // tr,), in_specs=[blk, blk],
            out_specs=pl.BlockSpec((tr, c), lambda i, cr: (i, 0))),
        out_shape=jax.ShapeDtypeStruct((h, c), F32), compiler_params=_params(("parallel",)),
    )(chip, parts, own)


def _join_halves(mine, other, core, *, name):
    h, c = mine.shape
    tr = _row_tile(h, 256)
    nblk = h // tr

    def body(core_ref, m_ref, s_ref, o_ref):
        is_mine = pl.program_id(0) // nblk == core_ref[0]

        @pl.when(is_mine)
        def _():
            o_ref[...] = m_ref[...]

        @pl.when(jnp.logical_not(is_mine))
        def _():
            o_ref[...] = s_ref[...]

    blk = pl.BlockSpec((tr, c), lambda i, cr: (i % nblk, 0))
    return pl.pallas_call(
        body, name=name,
        grid_spec=pltpu.PrefetchScalarGridSpec(
            num_scalar_prefetch=1, grid=(2 * nblk,), in_specs=[blk, blk],
            out_specs=pl.BlockSpec((tr, c), lambda i, cr: (i, 0))),
        out_shape=jax.ShapeDtypeStruct((2 * h, c), F32), compiler_params=_params(("arbitrary",)),
    )(core, mine, other)


def _place():
    x, y, c = lax.axis_index("x"), lax.axis_index("y"), lax.axis_index("c")
    chips = [(1 - x, y), (x, 1 - y), (1 - x, 1 - y)]
    return x, y, c, chips


def _remote(src, dst, send_sems, recv_sems, k, to):
    return pltpu.make_async_remote_copy(src_ref=src, dst_ref=dst, send_sem=send_sems.at[k],
                                        recv_sem=recv_sems.at[k], device_id=to, device_id_type=MESH)


def _hbm_call(body, name, ins, out_shapes, n_remote):
    any_spec = pl.BlockSpec(memory_space=pl.ANY)
    return pl.pallas_call(
        body, name=name, in_specs=[any_spec] * len(ins), out_specs=tuple([any_spec] * len(out_shapes)),
        out_shape=tuple(out_shapes),
        scratch_shapes=[pltpu.SemaphoreType.DMA((n_remote,)), pltpu.SemaphoreType.DMA((n_remote,))],
    )(*ins)


def _all_gather_chips(shards, *, name):
    n = len(shards)

    def body(*refs):
        ins, outs = refs[:n], refs[n:2 * n]
        send_sems, recv_sems = refs[2 * n:]
        x, y, c, chips = _place()
        me = 2 * x + y
        sent = []
        for s in range(n):
            h = ins[s].shape[0] // 2
            for j, (px, py) in enumerate(chips):
                cp = _remote(ins[s].at[pl.ds(c * h, h)], outs[s].at[me, pl.ds(c * h, h)],
                             send_sems, recv_sems, s * 6 + j, (px, py, c))
                cp.start()
                sent.append(cp)
        for s in range(n):
            h = ins[s].shape[0] // 2
            for j, (px, py) in enumerate(chips):
                slab = outs[s].at[2 * px + py, pl.ds(c * h, h)]
                _remote(slab, slab, send_sems, recv_sems, s * 6 + j, (px, py, c)).wait_recv()
                cp = _remote(slab, slab, send_sems, recv_sems, s * 6 + 3 + j, (x, y, 1 - c))
                cp.start()
                sent.append(cp)
        for s in range(n):
            h = ins[s].shape[0] // 2
            for j, (px, py) in enumerate(chips):
                slab = outs[s].at[2 * px + py, pl.ds((1 - c) * h, h)]
                _remote(slab, slab, send_sems, recv_sems, s * 6 + 3 + j, (x, y, 1 - c)).wait_recv()
        for cp in sent:
            cp.wait_send()

    out_shapes = [jax.ShapeDtypeStruct((N_CHIPS,) + s.shape, s.dtype) for s in shards]
    return _hbm_call(body, name, shards, out_shapes, 6 * n)


def _pair_send_other_half(stacked, *, name):
    n = len(stacked)

    def body(*refs):
        ins, outs = refs[:n], refs[n:2 * n]
        send_sems, recv_sems = refs[2 * n:]
        x, y, c, _chips = _place()
        sent = []
        for s in range(n):
            h = ins[s].shape[1] // 2
            cp = _remote(ins[s].at[:, pl.ds((1 - c) * h, h)], outs[s], send_sems, recv_sems, s, (x, y, 1 - c))
            cp.start()
            sent.append(cp)
        for cp in sent:
            cp.wait_recv()
        for cp in sent:
            cp.wait_send()

    out_shapes = [jax.ShapeDtypeStruct((s.shape[0], s.shape[1] // 2, s.shape[2]), s.dtype) for s in stacked]
    return _hbm_call(body, name, stacked, out_shapes, n)


def _chip_exchange(halves, *, name):
    n = len(halves)

    def body(*refs):
        ins, outs = refs[:n], refs[n:2 * n]
        send_sems, recv_sems = refs[2 * n:]
        x, y, c, chips = _place()
        me = 2 * x + y
        sent = []
        for s in range(n):
            for j, (px, py) in enumerate(chips):
                cp = _remote(ins[s].at[2 * px + py], outs[s].at[me], send_sems, recv_sems, s * 3 + j, (px, py, c))
                cp.start()
                sent.append(cp)
        for s in range(n):
            for j, (px, py) in enumerate(chips):
                slab = outs[s].at[2 * px + py]
                _remote(slab, slab, send_sems, recv_sems, s * 3 + j, (px, py, c)).wait_recv()
        for cp in sent:
            cp.wait_send()

    out_shapes = [jax.ShapeDtypeStruct(s.shape, s.dtype) for s in halves]
    return _hbm_call(body, name, halves, out_shapes, 3 * n)


def _pair_swap(halves, *, name):
    n = len(halves)

    def body(*refs):
        ins, outs = refs[:n], refs[n:2 * n]
        send_sems, recv_sems = refs[2 * n:]
        x, y, c, _chips = _place()
        sent = []
        for s in range(n):
            cp = _remote(ins[s], outs[s], send_sems, recv_sems, s, (x, y, 1 - c))
            cp.start()
            sent.append(cp)
        for cp in sent:
            cp.wait_recv()
        for cp in sent:
            cp.wait_send()

    out_shapes = [jax.ShapeDtypeStruct(s.shape, s.dtype) for s in halves]
    return _hbm_call(body, name, halves, out_shapes, n)


def _pack_rows(parts, row_multiple):
    flat = jnp.concatenate([p.reshape(-1) for p in parts])
    quantum = row_multiple * PACK_COLS
    pad = (-flat.shape[0]) % quantum
    flat = jnp.pad(flat, (0, pad))
    return flat.reshape(-1, PACK_COLS)


def _unpack(flat, shapes):
    out, pos = [], 0
    for shp in shapes:
        size = math.prod(shp)
        out.append(flat[pos:pos + size].reshape(shp))
        pos += size
    return out


def _to_chunks_cols(full):
    r, c4 = full.shape
    return full.reshape(r, N_CHIPS, c4 // N_CHIPS).transpose(1, 0, 2)


def _from_chunks_cols(stacked):
    nch, r, c = stacked.shape
    return stacked.transpose(1, 0, 2).reshape(r, nch * c)


def _class_major(a, bl, t, dil):
    w = a.shape[-1]
    if dil == 1:
        return a.reshape(bl, 1, t, w)
    return a.reshape(bl, t // dil, dil, w).transpose(0, 2, 1, 3)


def _natural(a):
    bl, dil, ln, w = a.shape
    if dil == 1:
        return a.reshape(bl * ln, w)
    return a.transpose(0, 2, 1, 3).reshape(bl * ln * dil, w)


def _train_step(x, positions, a_pre_norm, a_w_in, a_w_out, a_post_norm, kv_norm, kv_w_down, kv_latent_norm,
                kv_w_up, b_pre_norm, b_w_in, b_q_norm, b_w_q_up, b_w_out, b_post_norm, loss_target, moments):
    bl, t, d = x.shape
    n = bl * t
    qb = t // A_DILATIONS[-1]
    tq = _tile(t, 256)
    dq4 = d // N_CHIPS
    chip = 2 * lax.axis_index("x") + lax.axis_index("y")
    chip_arr = chip.astype(jnp.int32).reshape(1)
    core_arr = lax.axis_index("c").astype(jnp.int32).reshape(1)

    w_in_a_s = a_w_in[0].astype(BF16)
    outs_s = jnp.concatenate([a_w_out[0], b_w_out[0]], axis=0).astype(BF16)
    small_shapes = [kv_w_down.shape, kv_w_up.shape, b_w_in[0].shape, b_w_q_up[0].shape]
    small_s = _pack_rows([kv_w_down, kv_w_up, b_w_in[0], b_w_q_up[0]], 32).astype(BF16)
    gains_s = jnp.pad(jnp.concatenate([a_pre_norm[0], a_post_norm[0]]), (0, 16 * LANES - 2 * dq4)).reshape(16, LANES)
    shards = [w_in_a_s, outs_s, small_s, gains_s]
    gathered = _all_gather_chips(shards, name="gather_weights")
    g_in_a, g_outs, g_small, g_gains = [lax.dynamic_update_index_in_dim(g, s, chip, 0)
                                        for g, s in zip(gathered, shards)]

    w_in_a = _from_chunks_cols(g_in_a)
    w_out_a = g_outs[:, :A_WIDTH // N_CHIPS].reshape(A_WIDTH, d)
    w_out_b = g_outs[:, A_WIDTH // N_CHIPS:].reshape(B_WIDTH, d)
    sm = [_unpack(g_small[k].reshape(-1), small_shapes) for k in range(N_CHIPS)]
    w_down = jnp.concatenate([sm[k][0] for k in range(N_CHIPS)], axis=0)
    w_up = jnp.concatenate([sm[k][1] for k in range(N_CHIPS)], axis=1)
    w_in_b = jnp.concatenate([sm[k][2] for k in range(N_CHIPS)], axis=1)
    w_q_up = jnp.concatenate([sm[k][3] for k in range(N_CHIPS)], axis=1)
    gflat = g_gains.reshape(N_CHIPS, -1)
    g_a_pre = gflat[:, :dq4].reshape(1, d)
    g_a_post = gflat[:, dq4:2 * dq4].reshape(1, d)

    w_up_h = w_up.reshape(B_KV_LORA, B_HEADS, B_NOPE + B_VDIM)
    w_up_k = jnp.pad(w_up_h[:, :, :B_NOPE], ((0, 0), (0, 0), (0, LANES - B_NOPE))).reshape(B_KV_LORA, B_HEADS * LANES)
    w_up_v = w_up_h[:, :, B_NOPE:].reshape(B_KV_LORA, B_WIDTH)
    w_up_cat = jnp.concatenate([w_up_k, w_up_v], axis=1)
    w_q_up_p = jnp.pad(w_q_up.reshape(B_Q_LORA, B_HEADS, B_QK_DIM),
                       ((0, 0), (0, 0), (0, LANES - B_QK_DIM))).reshape(B_Q_LORA, B_HEADS * LANES)
    zeros_d = lambda c: jnp.zeros((d, c), BF16)
    w_down_p = jnp.concatenate([w_down[:, :B_KV_LORA], zeros_d(B_NOPE), w_down[:, B_KV_LORA:],
                                zeros_d(LANES - B_NOPE - B_ROPE)], axis=1)
    w_cq = w_in_b[:, :B_Q_LORA]
    w_z = w_in_b[:, B_Q_LORA:]

    tabs_a = _rope_tables(positions, A_ROPE_THETA, 0)
    tabs_b = _rope_tables(positions, B_ROPE_THETA, B_NOPE)

    h0 = x.reshape(n, d)
    hn_a = _rms_fwd(h0, g_a_pre, BF16, name="a_pre_norm")
    is_qk = lambda j: j != 2
    is_q = lambda j: j == 0
    z_blk_a = 3 * A_GROUPS
    z_a = _matmul(hn_a, w_in_a, "nn", BF16, name="a_proj_z", b_cols=(z_blk_a, 1))
    o_groups, lse_groups, qkv_cm, hn_cm, tabs_cm = [], [], [], [], []
    for g, dil in enumerate(A_DILATIONS):
        flat = lambda a: _class_major(a, bl, t, dil).reshape(n, a.shape[-1])
        hn_g = hn_a if dil == 1 else flat(hn_a)
        tabs_g = tabs_a if dil == 1 else lax.optimization_barrier(tuple(flat(tb) for tb in tabs_a))
        proj_g = _matmul(hn_g, w_in_a, "nn", BF16, name=f"a_proj_{g}", rope=(tabs_g, is_qk),
                         out_scale=(A_SCALE * LOG2E, is_q), b_cols=(3 * g, 3))
        src = proj_g.reshape(bl, dil, t // dil, 3 * A_WIDTH)
        hn_cm.append(hn_g)
        tabs_cm.append(tabs_g)
        qkv_cm.append(src)
        o_g, lse_g = _attn_a_fwd(src, 0, qb, name=f"attn_a_fwd_{g}")
        o_groups.append(_natural(o_g))
        lse_groups.append(_natural(lse_g))
    ypre_a, om_a, lse_a = _merge_gate_fwd(o_groups, lse_groups, z_a, 0)
    y_a = _matmul(ypre_a, w_out_a, "nn", F32, name="a_out")
    h1 = _rms_fwd(y_a, g_a_post, F32, name="a_post_norm", add=h0)

    g_kvn = kv_norm.reshape(1, d)
    g_lat = kv_latent_norm.reshape(1, B_KV_LORA)
    hn_kv = _rms_fwd(h1, g_kvn, BF16, name="kv_norm")
    ckr = _matmul(hn_kv, w_down_p, "nn", F32, name="kv_down")
    c_kv, k_rope = _kv_latent_fwd(ckr, g_lat, tabs_b)
    kvup = _matmul(c_kv, w_up_cat, "nn", BF16, name="kv_up")
    hn_b = _rms_fwd(h1, b_pre_norm, BF16, name="b_pre_norm")
    z_b = _matmul(hn_b, w_z, "nn", BF16, name="b_proj_z")
    cq_raw = _matmul(hn_b, w_cq, "nn", F32, name="b_proj_q")
    c_q = _rms_fwd(cq_raw, b_q_norm, BF16, name="b_q_norm")
    always = lambda j: True
    q_cat = _matmul(c_q, w_q_up_p, "nn", BF16, name="b_q_up", rope=(tabs_b, always),
                    out_scale=(B_SCALE * LOG2E, always))
    r3 = lambda a: a.reshape(bl, t, a.shape[-1])
    tabs_b3 = tuple(r3(tb) for tb in tabs_b)
    ypre_b, o_b, lse_b, lse_rows_b = _mla_fwd(r3(q_cat), r3(kvup), r3(k_rope), r3(z_b), tq)
    y_b = _matmul(ypre_b.reshape(n, B_WIDTH), w_out_b, "nn", F32, name="b_out")
    h2 = _rms_fwd(y_b, b_post_norm, F32, name="b_post_norm", add=h1)
    dh2, loss_part = _loss_fwd_bwd(h2, loss_target.reshape(n, d))

    dy_b, dg_b_post = _rms_bwd(y_b, b_post_norm, dh2, BF16, name="b_post_norm_bwd")
    dypre_b = _matmul(dy_b, w_out_b, "nt", F32, name="b_out_dx")
    dw_out_b = _matmul(ypre_b.reshape(n, B_WIDTH), dy_b, "tn", F32, name="b_out_dw", tm=1024, tk=512)
    do_b, dz_b = _gate_bwd(dypre_b, o_b.reshape(n, B_WIDTH), z_b, 0, name="b_gate_bwd", with_delta=False)
    dq_cat, delta_rows_b = _mla_dq(r3(q_cat), r3(kvup), r3(k_rope), r3(do_b), o_b, lse_b, tabs_b3, tq)
    dq_cat = dq_cat.reshape(n, -1)
    dk_cat, dv_b = _mla_dkv(r3(q_cat), r3(kvup), r3(k_rope), r3(do_b), lse_rows_b, delta_rows_b, tq)
    dk_cat, dv_b = dk_cat.reshape(n, -1), dv_b.reshape(n, -1)
    dcq_n = _matmul(dq_cat, w_q_up_p, "nt", F32, name="b_q_up_dx")
    dw_q_up_p = _matmul(c_q, dq_cat, "tn", F32, name="b_q_up_dw", tm=1024, tk=512)
    dcq, dg_b_q = _rms_bwd(cq_raw, b_q_norm, dcq_n, BF16, name="b_q_norm_bwd")
    dhn_b = _matmul(dz_b, w_z, "nt", F32, name="b_proj_z_dx")
    dhn_b = _matmul(dcq, w_cq, "nt", F32, name="b_proj_q_dx", add=dhn_b)
    dw_z = _matmul(hn_b, dz_b, "tn", F32, name="b_proj_z_dw", tm=1024, tk=512)
    dw_cq = _matmul(hn_b, dcq, "tn", F32, name="b_proj_q_dw", tm=1024, tk=512)
    dh1, dg_b_pre = _rms_bwd(h1, b_pre_norm, dhn_b, F32, name="b_pre_norm_bwd", adds=(dh2,))
    dckv_n = _matmul(dk_cat, w_up_k, "nt", F32, name="kv_up_k_dx")
    dckv_n = _matmul(dv_b, w_up_v, "nt", F32, name="kv_up_v_dx", add=dckv_n)
    dw_up_k = _matmul(c_kv, dk_cat, "tn", F32, name="kv_up_k_dw", tm=1024, tk=512)
    dw_up_v = _matmul(c_kv, dv_b, "tn", F32, name="kv_up_v_dw", tm=1024, tk=512)
    dckr, dg_lat = _kv_latent_bwd(dckv_n, ckr, g_lat, dk_cat, tabs_b)
    dhn_kv = _matmul(dckr, w_down_p, "nt", F32, name="kv_down_dx")
    dw_down_p = _matmul(hn_kv, dckr, "tn", F32, name="kv_down_dw", tm=1024, tk=512)
    dh1, dg_kvn = _rms_bwd(h1, g_kvn, dhn_kv, F32, name="kv_norm_bwd", adds=(dh1,))

    dy_a, dg_a_post = _rms_bwd(y_a, g_a_post, dh1, BF16, name="a_post_norm_bwd")
    dypre_a = _matmul(dy_a, w_out_a, "nt", F32, name="a_out_dx")
    dw_out_a = _matmul(ypre_a, dy_a, "tn", F32, name="a_out_dw", tm=1024, tk=512)
    do_a, dz_a, delta_a = _gate_bwd(dypre_a, om_a, z_a, 0, name="a_gate_bwd", with_delta=True)
    dw_cols = A_IN_WIDTH // N_CHIPS
    dw_tn = _tile(dw_cols, 512)
    dw_kwargs = dict(tm=1024, tn=dw_tn, tk=512, out_chunk_blocks=dw_cols // dw_tn)
    r_big = _matmul(hn_a, dz_a, "tn", F32, name="a_proj_dw_z", out_full=(N_CHIPS, d, dw_cols),
                    out_joff=z_blk_a * A_WIDTH // dw_tn, **dw_kwargs)
    dhn_a = _matmul(dz_a, w_in_a, "nt", F32, name="a_proj_dx_z", b_koff=z_blk_a)
    dhn_more = []
    for g, dil in enumerate(A_DILATIONS):
        cm = lambda a: _class_major(a, bl, t, dil)
        swap = lambda a: jnp.swapaxes(a, 2, 3)
        lse_cm, delta_cm = cm(lse_a), cm(delta_a)
        tabs_g = tuple(tb.reshape(bl, dil, t // dil, LANES) for tb in tabs_cm[g])
        dqkv = _attn_a_bwd(qkv_cm[g], 0, cm(do_a), lse_cm, delta_cm, swap(lse_cm), swap(delta_cm),
                           tabs_g, qb, name=f"attn_a_bwd_{g}").reshape(n, 3 * A_WIDTH)
        koff = 3 * g * A_WIDTH // _tile(3 * A_WIDTH, 1024)
        if dil == 1:
            dhn_a = _matmul(dqkv, w_in_a, "nt", F32, name=f"a_proj_dx_{g}", add=dhn_a, b_koff=koff)
        else:
            part = _matmul(dqkv, w_in_a, "nt", BF16, name=f"a_proj_dx_{g}", b_koff=koff)
            dhn_more.append(_natural(part.reshape(bl, dil, t // dil, d)))
        r_big = _matmul(hn_cm[g], dqkv, "tn", F32, name=f"a_proj_dw_{g}", out_into=r_big,
                        out_joff=3 * g * A_WIDTH // dw_tn, **dw_kwargs)
    grad_x, dg_a_pre = _rms_bwd(h0, g_a_pre, dhn_a, F32, name="a_pre_norm_bwd", adds=(dh1,),
                                dy_more=tuple(dhn_more))

    dw_up = jnp.concatenate([dw_up_k.reshape(B_KV_LORA, B_HEADS, LANES)[:, :, :B_NOPE],
                             dw_up_v.reshape(B_KV_LORA, B_HEADS, B_VDIM)], axis=2).reshape(B_KV_LORA, -1)
    dw_q_up = dw_q_up_p.reshape(B_Q_LORA, B_HEADS, LANES)[:, :, :B_QK_DIM].reshape(B_Q_LORA, -1)
    dw_down = jnp.concatenate([dw_down_p[:, :B_KV_LORA], dw_down_p[:, B_KV_LORA + B_NOPE:B_KV_LORA + B_NOPE + B_ROPE]], axis=1)
    dw_in_b = jnp.concatenate([dw_cq, dw_z], axis=1)
    vec_rep = [dg_kvn.reshape(-1), dg_lat.reshape(-1), dg_b_pre.reshape(-1), dg_b_q.reshape(-1),
               dg_b_post.reshape(-1), loss_part.reshape(-1)]
    vec_shapes = [(dq4,), (dq4,)] + [v.shape for v in vec_rep]
    r_outs = jnp.concatenate([dw_out_a.reshape(N_CHIPS, A_WIDTH // N_CHIPS, d),
                              dw_out_b.reshape(N_CHIPS, B_WIDTH // N_CHIPS, d)], axis=1)
    down_c = dw_down.reshape(N_CHIPS, dq4, -1)
    up_c = _to_chunks_cols(dw_up)
    inb_c = _to_chunks_cols(dw_in_b)
    qup_c = _to_chunks_cols(dw_q_up)
    small_chunks = []
    for k in range(N_CHIPS):
        vecs = [dg_a_pre.reshape(-1)[k * dq4:(k + 1) * dq4], dg_a_post.reshape(-1)[k * dq4:(k + 1) * dq4]] + vec_rep
        small_chunks.append(_pack_rows([down_c[k], up_c[k], inb_c[k], qup_c[k]] + vecs, 32))
    r_small = jnp.stack(small_chunks)

    stacked = [r_big, r_outs, r_small]
    payload = [BF16, BF16, F32]
    recv = _pair_send_other_half(stacked, name="reduce_pair_send")
    halves = [_add_my_half(s, p, core_arr, dt, name=f"reduce_pair_add_{i}")
              for i, (s, p, dt) in enumerate(zip(stacked, recv, payload))]
    parts = _chip_exchange(halves, name="reduce_chip_exchange")
    sums = [_sum_chips(p, own, chip_arr, name=f"reduce_chip_sum_{i}") for i, (p, own) in enumerate(zip(parts, halves))]
    others = _pair_swap(sums, name="reduce_pair_swap")
    g_big, g_outs_r, g_small_r = [_join_halves(m, o, core_arr, name=f"reduce_join_{i}")
                                  for i, (m, o) in enumerate(zip(sums, others))]

    grads = {}
    grads["a_w_in"] = g_big
    grads["a_w_out"] = g_outs_r[:A_WIDTH // N_CHIPS]
    grads["b_w_out"] = g_outs_r[A_WIDTH // N_CHIPS:]
    small_out_shapes = [down_c.shape[1:], up_c.shape[1:], inb_c.shape[1:], qup_c.shape[1:]] + vec_shapes
    (grads["kv_w_down"], grads["kv_w_up"], grads["b_w_in"], grads["b_w_q_up"], grads["a_pre_norm"],
     grads["a_post_norm"], grads["kv_norm"], grads["kv_latent_norm"], grads["b_pre_norm"], grads["b_q_norm"],
     grads["b_post_norm"], loss_sum) = _unpack(g_small_r.reshape(-1), small_out_shapes)

    weights = dict(a_pre_norm=a_pre_norm, a_w_in=a_w_in, a_w_out=a_w_out, a_post_norm=a_post_norm, kv_norm=kv_norm,
                   kv_w_down=kv_w_down, kv_latent_norm=kv_latent_norm, kv_w_up=kv_w_up, b_pre_norm=b_pre_norm,
                   b_w_in=b_w_in, b_q_norm=b_q_norm, b_w_q_up=b_w_q_up, b_w_out=b_w_out, b_post_norm=b_post_norm)
    names = list(weights)
    out_g, out_d, out_m, out_v = [], [], [], []
    for i, nm in enumerate(names):
        w = weights[nm]
        two_d = (1, w.shape[0]) if w.ndim == 1 else (w.shape[-2], w.shape[-1])
        gw = grads[nm].reshape(two_d)
        dlt, new_m, new_v = _adamw(w.reshape(two_d), gw, moments[i].reshape(two_d),
                                   moments[len(names) + i].reshape(two_d), name=f"adamw_{nm}")
        out_g.append(gw.reshape(w.shape))
        out_d.append(dlt.reshape(w.shape))
        out_m.append(new_m.reshape(w.shape))
        out_v.append(new_v.reshape(w.shape))
    return (loss_sum.reshape(()), grad_x.reshape(bl, t, d), *out_g, *out_d, *out_m, *out_v)


def kernel(x, positions, a_pre_norm, a_w_in, a_w_out, a_post_norm, kv_norm, kv_w_down, kv_latent_norm, kv_w_up, b_pre_norm, b_w_in, b_q_norm, b_w_q_up, b_w_out, b_post_norm, loss_target, m_a_pre_norm, m_a_w_in, m_a_w_out, m_a_post_norm, m_kv_norm, m_kv_w_down, m_kv_latent_norm, m_kv_w_up, m_b_pre_norm, m_b_w_in, m_b_q_norm, m_b_w_q_up, m_b_w_out, m_b_post_norm, v_a_pre_norm, v_a_w_in, v_a_w_out, v_a_post_norm, v_kv_norm, v_kv_w_down, v_kv_latent_norm, v_kv_w_up, v_b_pre_norm, v_b_w_in, v_b_q_norm, v_b_w_q_up, v_b_w_out, v_b_post_norm):
    moments = (m_a_pre_norm, m_a_w_in, m_a_w_out, m_a_post_norm, m_kv_norm, m_kv_w_down, m_kv_latent_norm, m_kv_w_up,
               m_b_pre_norm, m_b_w_in, m_b_q_norm, m_b_w_q_up, m_b_w_out, m_b_post_norm,
               v_a_pre_norm, v_a_w_in, v_a_w_out, v_a_post_norm, v_kv_norm, v_kv_w_down, v_kv_latent_norm, v_kv_w_up,
               v_b_pre_norm, v_b_w_in, v_b_q_norm, v_b_w_q_up, v_b_w_out, v_b_post_norm)
    return _train_step(x, positions, a_pre_norm, a_w_in, a_w_out, a_post_norm, kv_norm, kv_w_down, kv_latent_norm,
                       kv_w_up, b_pre_norm, b_w_in, b_q_norm, b_w_q_up, b_w_out, b_post_norm, loss_target, moments)
```

```python
import math

import jax
import jax.numpy as jnp
from jax import lax
from jax.experimental import pallas as pl
from jax.experimental.pallas import tpu as pltpu

F32 = jnp.float32
BF16 = jnp.bfloat16
MESH = pl.DeviceIdType.MESH

NORM_EPS = 1e-6
NEG = -1e30
LANES = 128
VMEM_LIMIT = 56 * 1024 * 1024
LOG2E = math.log2(math.e)
LN2 = math.log(2.0)

A_GROUPS = 3
A_DILATIONS = (1, 4, 16)
A_HEADS = 8
A_HEAD_DIM = 128
A_WIDTH = A_HEADS * A_HEAD_DIM
A_ROPE_THETA = 500000.0
A_IN_WIDTH = A_GROUPS * 3 * A_WIDTH + A_WIDTH
A_SCALE = A_HEAD_DIM ** -0.5

B_HEADS = 16
B_NOPE = 64
B_ROPE = 32
B_QK_DIM = B_NOPE + B_ROPE
B_VDIM = 64
B_WIDTH = B_HEADS * B_VDIM
B_Q_LORA = 384
B_KV_LORA = 256
B_ROPE_THETA = 10000.0
B_SCALE = B_QK_DIM ** -0.5

ADAM_LR = 0.001
ADAM_B1 = 0.9
ADAM_B2 = 0.999
ADAM_EPS = 1e-08
ADAM_WD = 0.01
ADAM_STEP = 10

N_CHIPS = 4
PACK_COLS = 512


def _params(sem=None):
    return pltpu.CompilerParams(dimension_semantics=sem, vmem_limit_bytes=VMEM_LIMIT)


def _tile(n, want):
    t = min(n, want)
    assert n % t == 0, (n, want)
    return t


def _row_tile(n, want):
    for t in range(min(n, want), 0, -1):
        if n % t == 0 and (t % 16 == 0 or t == n):
            return t
    return n


def _rope_tables(positions, theta, lane0):
    half = 16
    inv_freq = 1.0 / (theta ** (jnp.arange(half, dtype=F32) * (2.0 / (2 * half))))
    n = positions.size
    per_row = LANES // half
    pos = jnp.repeat(positions.astype(F32).reshape(n // per_row, per_row), half, axis=1)
    ang = pos * jnp.tile(inv_freq, per_row)
    cos, sin = lax.optimization_barrier((jnp.cos(ang), jnp.sin(ang)))
    cos, sin = cos.reshape(n, half), sin.reshape(n, half)
    pre = jnp.zeros((n, lane0), F32)
    post = jnp.zeros((n, LANES - lane0 - 2 * half), F32)
    z16 = jnp.zeros((n, half), F32)
    c = jnp.concatenate([pre + 1.0, cos, cos, post + 1.0], axis=1)
    sa = jnp.concatenate([pre, -sin, z16, post], axis=1)
    sb = jnp.concatenate([pre, z16, sin, post], axis=1)
    return lax.optimization_barrier((c, sa, sb))


def _rope_apply(x, c, sa, sb, sign):
    k = x.shape[1] // LANES
    if k > 1:
        c, sa, sb = (jnp.concatenate([t] * k, axis=1) for t in (c, sa, sb))
    w = x.shape[1]
    up = pltpu.roll(x, w - 16, 1)
    dn = pltpu.roll(x, 16, 1)
    if sign > 0:
        return x * c + up * sa + dn * sb
    return x * c - up * sa - dn * sb


def _matmul(a, b, mode, out_dtype, *, name, tm=512, tn=1024, tk=1024, add=None, rope=None,
            out_scale=None, b_koff=0, b_cols=None, out_into=None, out_full=None, out_joff=0,
            out_chunk_blocks=None):
    if mode == "nn":
        m, k = a.shape
        n = b.shape[1]
    elif mode == "nt":
        m, k = a.shape
        n = b.shape[0]
    else:
        k, m = a.shape
        n = b.shape[1]
    b_j0 = 0
    if b_cols is not None:
        tn = _tile(n, tn)
        b_j0, n = b_cols[0], b_cols[1] * tn
    tm, tn, tk = _tile(m, tm), _tile(n, tn), _tile(k, tk)
    nk = k // tk
    if mode == "nn":
        a_spec = pl.BlockSpec((tm, tk), lambda j, i, kk: (i, kk))
        b_spec = pl.BlockSpec((tk, tn), lambda j, i, kk: (kk, j + b_j0))
        dims = (((1,), (0,)), ((), ()))
    elif mode == "nt":
        a_spec = pl.BlockSpec((tm, tk), lambda j, i, kk: (i, kk))
        b_spec = pl.BlockSpec((tn, tk), lambda j, i, kk: (j, kk + b_koff))
        dims = (((1,), (1,)), ((), ()))
    else:
        a_spec = pl.BlockSpec((tk, tm), lambda j, i, kk: (kk, i))
        b_spec = pl.BlockSpec((tk, tn), lambda j, i, kk: (kk, j))
        dims = (((0,), (0,)), ((), ()))
    operands = [a, b]
    in_specs = [a_spec, b_spec]
    if add is not None:
        operands.append(add)
        in_specs.append(pl.BlockSpec((tm, tn), lambda j, i, kk: (i, j)))
    if rope is not None:
        tables, rope_pred = rope
        for t in tables:
            operands.append(t)
            in_specs.append(pl.BlockSpec((tm, LANES), lambda j, i, kk: (i, 0)))
    aliases = {}
    if out_into is not None:
        aliases = {len(operands): 0}
        operands.append(out_into)
        in_specs.append(pl.BlockSpec(memory_space=pl.ANY))
        out_shape = jax.ShapeDtypeStruct(out_into.shape, out_into.dtype)
    elif out_full is not None:
        out_shape = jax.ShapeDtypeStruct(out_full, out_dtype)
    else:
        out_shape = jax.ShapeDtypeStruct((m, n), out_dtype)
    if out_chunk_blocks is not None:
        out_spec = pl.BlockSpec((None, tm, tn), lambda j, i, kk: ((j + out_joff) // out_chunk_blocks, i,
                                                                  (j + out_joff) % out_chunk_blocks))
    else:
        out_spec = pl.BlockSpec((tm, tn), lambda j, i, kk: (i, j + out_joff))

    def body(*refs):
        a_ref, b_ref = refs[0], refs[1]
        pos = 2
        add_ref = None
        if add is not None:
            add_ref = refs[pos]
            pos += 1
        tab_refs = None
        if rope is not None:
            tab_refs = refs[pos:pos + 3]
            pos += 3
        if out_into is not None:
            pos += 1
        o_ref = refs[pos]
        acc_ref = refs[pos + 1] if nk > 1 else None

        def finish(res):
            if add_ref is not None:
                res = res + add_ref[...].astype(F32)
            if tab_refs is None:
                o_ref[...] = res.astype(o_ref.dtype)
                return
            j = pl.program_id(0)
            flag = rope_pred(j)
            roped = _rope_apply(res, tab_refs[0][...], tab_refs[1][...], tab_refs[2][...], 1)
            if out_scale is not None:
                value, scale_pred = out_scale
                use = scale_pred(j)
                roped = roped * (value if use is True else jnp.where(use, value, 1.0))
            if flag is True:
                o_ref[...] = roped.astype(o_ref.dtype)
                return

            @pl.when(flag)
            def _():
                o_ref[...] = roped.astype(o_ref.dtype)

            @pl.when(jnp.logical_not(flag))
            def _():
                o_ref[...] = res.astype(o_ref.dtype)

        part = lax.dot_general(a_ref[...].astype(BF16), b_ref[...].astype(BF16), dims,
                               preferred_element_type=F32)
        if nk == 1:
            finish(part)
            return
        kk = pl.program_id(2)

        @pl.when(kk == 0)
        def _():
            acc_ref[...] = part

        @pl.when(kk > 0)
        def _():
            acc_ref[...] += part

        @pl.when(kk == nk - 1)
        def _():
            finish(acc_ref[...])

    return pl.pallas_call(
        body, name=name, grid=(n // tn, m // tm, nk), in_specs=in_specs, out_specs=out_spec,
        out_shape=out_shape, input_output_aliases=aliases,
        scratch_shapes=[pltpu.VMEM((tm, tn), F32)] if nk > 1 else [],
        compiler_params=_params(("parallel", "parallel", "arbitrary")),
    )(*operands)


def _rms_fwd(x, g, out_dtype, *, name, add=None, tr=512):
    n, d = x.shape
    tr = _tile(n, tr)
    row = pl.BlockSpec((tr, d), lambda i: (i, 0))
    vec = pl.BlockSpec((1, d), lambda i: (0, 0))

    def body(*refs):
        x_ref, g_ref = refs[0], refs[1]
        o_ref = refs[-1]
        xv = x_ref[...].astype(F32)
        r = lax.rsqrt(jnp.mean(xv * xv, axis=-1, keepdims=True) + NORM_EPS)
        y = xv * r * g_ref[...]
        if add is not None:
            y = refs[2][...] + y
        o_ref[...] = y.astype(o_ref.dtype)

    ops = [x, g] + ([add] if add is not None else [])
    specs = [row, vec] + ([row] if add is not None else [])
    return pl.pallas_call(
        body, name=name, grid=(n // tr,), in_specs=specs, out_specs=row,
        out_shape=jax.ShapeDtypeStruct((n, d), out_dtype), compiler_params=_params(("parallel",)),
    )(*ops)


def _rms_bwd(x, g, dy, out_dtype, *, name, adds=(), dy_more=(), tr=512):
    n, d = x.shape
    tr = _tile(n, tr)
    steps = n // tr
    row = pl.BlockSpec((tr, d), lambda i: (i, 0))
    vec = pl.BlockSpec((1, d), lambda i: (0, 0))
    na = len(adds) + len(dy_more)

    def body(*refs):
        x_ref, g_ref, dy_ref = refs[:3]
        add_refs = refs[3:3 + len(adds)]
        more_refs = refs[3 + len(adds):3 + na]
        dx_ref, dg_ref, acc_ref = refs[3 + na:]
        i = pl.program_id(0)
        xv = x_ref[...].astype(F32)
        r = lax.rsqrt(jnp.mean(xv * xv, axis=-1, keepdims=True) + NORM_EPS)
        xh = xv * r
        dyv = dy_ref[...].astype(F32)
        for m_ref in more_refs:
            dyv = dyv + m_ref[...].astype(F32)
        part = (dyv * xh).reshape(tr // 8, 8, d).sum(axis=0)

        @pl.when(i == 0)
        def _():
            acc_ref[...] = part

        @pl.when(i > 0)
        def _():
            acc_ref[...] += part

        t = dyv * g_ref[...]
        dx = r * (t - xh * jnp.mean(t * xh, axis=-1, keepdims=True))
        for a_ref in add_refs:
            dx = dx + a_ref[...].astype(F32)
        dx_ref[...] = dx.astype(dx_ref.dtype)

        @pl.when(i == steps - 1)
        def _():
            dg_ref[...] = jnp.sum(acc_ref[...], axis=0, keepdims=True)

    return pl.pallas_call(
        body, name=name, grid=(steps,), in_specs=[row, vec, row] + [row] * na,
        out_specs=(row, vec),
        out_shape=(jax.ShapeDtypeStruct((n, d), out_dtype), jax.ShapeDtypeStruct((1, d), F32)),
        scratch_shapes=[pltpu.VMEM((8, d), F32)], compiler_params=_params(("arbitrary",)),
    )(x, g, dy, *adds, *dy_more)


def _rms(xv, g):
    return xv * lax.rsqrt(jnp.mean(xv * xv, axis=-1, keepdims=True) + NORM_EPS) * g


def _post_norm_block(y, g, h_in, next_gains, *, name, tr=512):
    n, d = y.shape
    tr = _tile(n, tr)
    nk = len(next_gains)
    row = pl.BlockSpec((tr, d), lambda i: (i, 0))
    vec = pl.BlockSpec((1, d), lambda i: (0, 0))

    def body(*refs):
        y_ref, g_ref, h_ref = refs[:3]
        gk_refs = refs[3:3 + nk]
        o_ref = refs[3 + nk]
        hn_refs = refs[4 + nk:]
        h = h_ref[...] + _rms(y_ref[...], g_ref[...])
        o_ref[...] = h
        for gk_ref, hn_ref in zip(gk_refs, hn_refs):
            hn_ref[...] = _rms(h, gk_ref[...]).astype(BF16)

    return pl.pallas_call(
        body, name=name, grid=(n // tr,), in_specs=[row, vec, row] + [vec] * nk,
        out_specs=(row,) * (1 + nk),
        out_shape=(jax.ShapeDtypeStruct((n, d), F32),) + (jax.ShapeDtypeStruct((n, d), BF16),) * nk,
        compiler_params=_params(("parallel",)),
    )(y, g, h_in, *next_gains)


def _post_norm_loss(y, g, h_in, target, *, tr=512):
    n, d = y.shape
    tr = _tile(n, tr)
    steps = n // tr
    row = pl.BlockSpec((tr, d), lambda i: (i, 0))

    def body(y_ref, g_ref, h_ref, t_ref, dh_ref, loss_ref, acc_ref):
        i = pl.program_id(0)
        e = h_ref[...] + _rms(y_ref[...], g_ref[...]) - t_ref[...]
        dh_ref[...] = e / d
        part = (e * e).reshape(tr // 8, 8, d).sum(axis=0)

        @pl.when(i == 0)
        def _():
            acc_ref[...] = part

        @pl.when(i > 0)
        def _():
            acc_ref[...] += part

        @pl.when(i == steps - 1)
        def _():
            s = jnp.sum(jnp.sum(acc_ref[...], axis=-1, keepdims=True), axis=0, keepdims=True)
            loss_ref[...] = 0.5 * s / d

    return pl.pallas_call(
        body, name="b_post_norm_loss", grid=(steps,),
        in_specs=[row, pl.BlockSpec((1, d), lambda i: (0, 0)), row, row],
        out_specs=(row, pl.BlockSpec((1, 1), lambda i: (0, 0))),
        out_shape=(jax.ShapeDtypeStruct((n, d), F32), jax.ShapeDtypeStruct((1, 1), F32)),
        scratch_shapes=[pltpu.VMEM((8, d), F32)], compiler_params=_params(("arbitrary",)),
    )(y, g, h_in, target)


def _rms_bwd_pair(x, g1, dy1, g2, dy2, add, *, name, tr=512):
    n, d = x.shape
    tr = _tile(n, tr)
    steps = n // tr
    row = pl.BlockSpec((tr, d), lambda i: (i, 0))
    vec = pl.BlockSpec((1, d), lambda i: (0, 0))

    def body(x_ref, g1_ref, d1_ref, g2_ref, d2_ref, add_ref, dx_ref, dg1_ref, dg2_ref, acc_ref):
        i = pl.program_id(0)
        xv = x_ref[...]
        r = lax.rsqrt(jnp.mean(xv * xv, axis=-1, keepdims=True) + NORM_EPS)
        xh = xv * r
        dx = add_ref[...]
        for k, (g_ref, d_ref) in enumerate(((g1_ref, d1_ref), (g2_ref, d2_ref))):
            dyv = d_ref[...].astype(F32)
            part = (dyv * xh).reshape(tr // 8, 8, d).sum(axis=0)

            @pl.when(i == 0)
            def _(part=part, k=k):
                acc_ref[k] = part

            @pl.when(i > 0)
            def _(part=part, k=k):
                acc_ref[k] += part

            t = dyv * g_ref[...]
            dx = dx + r * (t - xh * jnp.mean(t * xh, axis=-1, keepdims=True))
        dx_ref[...] = dx

        @pl.when(i == steps - 1)
        def _():
            dg1_ref[...] = jnp.sum(acc_ref[0], axis=0, keepdims=True)
            dg2_ref[...] = jnp.sum(acc_ref[1], axis=0, keepdims=True)

    return pl.pallas_call(
        body, name=name, grid=(steps,), in_specs=[row, vec, row, vec, row, row],
        out_specs=(row, vec, vec),
        out_shape=(jax.ShapeDtypeStruct((n, d), F32), jax.ShapeDtypeStruct((1, d), F32),
                   jax.ShapeDtypeStruct((1, d), F32)),
        scratch_shapes=[pltpu.VMEM((2, 8, d), F32)], compiler_params=_params(("arbitrary",)),
    )(x, g1, dy1, g2, dy2, add)


def _kv_latent_fwd(ckr, g_lat, tabs, *, tr=512):
    n = ckr.shape[0]
    tr = _tile(n, tr)
    lat = B_KV_LORA

    def body(c_ref, k_ref, g_ref, tc, tsa, tsb, ckv_ref, kr_ref):
        xv = c_ref[...]
        r = lax.rsqrt(jnp.mean(xv * xv, axis=-1, keepdims=True) + NORM_EPS)
        ckv_ref[...] = (xv * r * g_ref[...]).astype(BF16)
        kr_ref[...] = _rope_apply(k_ref[...], tc[...], tsa[...], tsb[...], 1).astype(BF16)

    tab = pl.BlockSpec((tr, LANES), lambda i: (i, 0))
    return pl.pallas_call(
        body, name="kv_latent_fwd", grid=(n // tr,),
        in_specs=[pl.BlockSpec((tr, lat), lambda i: (i, 0)),
                  pl.BlockSpec((tr, LANES), lambda i: (i, lat // LANES)),
                  pl.BlockSpec((1, lat), lambda i: (0, 0)), tab, tab, tab],
        out_specs=(pl.BlockSpec((tr, lat), lambda i: (i, 0)), tab),
        out_shape=(jax.ShapeDtypeStruct((n, lat), BF16), jax.ShapeDtypeStruct((n, LANES), BF16)),
        compiler_params=_params(("parallel",)),
    )(ckr, ckr, g_lat, *tabs)


def _kv_latent_bwd(dckv, ckr, g_lat, dk_cat, tabs, *, tr=512):
    n = ckr.shape[0]
    tr = _tile(n, tr)
    steps = n // tr
    lat = B_KV_LORA
    wk = dk_cat.shape[1]

    def body(d_ref, c_ref, g_ref, dk_ref, tc, tsa, tsb, o_ref, dg_ref, acc_ref):
        i = pl.program_id(0)
        xv = c_ref[...]
        r = lax.rsqrt(jnp.mean(xv * xv, axis=-1, keepdims=True) + NORM_EPS)
        xh = xv * r
        dyv = d_ref[...]
        part = (dyv * xh).reshape(tr // 8, 8, lat).sum(axis=0)

        @pl.when(i == 0)
        def _():
            acc_ref[...] = part

        @pl.when(i > 0)
        def _():
            acc_ref[...] += part

        t = dyv * g_ref[...]
        dx = r * (t - xh * jnp.mean(t * xh, axis=-1, keepdims=True))
        o_ref[:, 0:lat] = dx.astype(o_ref.dtype)
        dkr = dk_ref[:, 0:LANES].astype(F32)
        for h in range(1, wk // LANES):
            dkr = dkr + dk_ref[:, h * LANES:(h + 1) * LANES].astype(F32)
        o_ref[:, lat:lat + LANES] = _rope_apply(dkr, tc[...], tsa[...], tsb[...], -1).astype(o_ref.dtype)

        @pl.when(i == steps - 1)
        def _():
            dg_ref[...] = jnp.sum(acc_ref[...], axis=0, keepdims=True)

    tab = pl.BlockSpec((tr, LANES), lambda i: (i, 0))
    return pl.pallas_call(
        body, name="kv_latent_bwd", grid=(steps,),
        in_specs=[pl.BlockSpec((tr, lat), lambda i: (i, 0)), pl.BlockSpec((tr, lat), lambda i: (i, 0)),
                  pl.BlockSpec((1, lat), lambda i: (0, 0)), pl.BlockSpec((tr, wk), lambda i: (i, 0)),
                  tab, tab, tab],
        out_specs=(pl.BlockSpec((tr, lat + LANES), lambda i: (i, 0)), pl.BlockSpec((1, lat), lambda i: (0, 0))),
        out_shape=(jax.ShapeDtypeStruct((n, lat + LANES), BF16), jax.ShapeDtypeStruct((1, lat), F32)),
        scratch_shapes=[pltpu.VMEM((8, lat), F32)], compiler_params=_params(("arbitrary",)),
    )(dckv, ckr, g_lat, dk_cat, *tabs)


def _sigmoid(z):
    return 1.0 / (1.0 + jnp.exp(-z))


def _lane_place(cols, width):
    rows = cols[0].shape[0]
    lane = lax.broadcasted_iota(jnp.int32, (rows, width), 1)
    out = jnp.zeros((rows, width), F32)
    for h, col in enumerate(cols):
        out = jnp.where(lane == h, col, out)
    return out


def _merge_gate_fwd(outs, lses, proj, z_block, *, tr=256):
    n, w = outs[0].shape
    tr = _tile(n, tr)
    ng = len(outs)

    def body(*refs):
        o_refs = refs[:ng]
        l_refs = refs[ng:2 * ng]
        z_ref = refs[2 * ng]
        y_ref, om_ref, lse_ref = refs[2 * ng + 1:]
        ls = [r[...] for r in l_refs]
        mx = ls[0]
        for l in ls[1:]:
            mx = jnp.maximum(mx, l)
        ssum = jnp.exp2(ls[0] - mx)
        for l in ls[1:]:
            ssum = ssum + jnp.exp2(l - mx)
        tot = mx + jnp.log2(ssum)
        lse_ref[...] = tot
        ws = [jnp.exp2(l - tot) for l in ls]
        for h in range(A_HEADS):
            sl = slice(h * A_HEAD_DIM, (h + 1) * A_HEAD_DIM)
            o = ws[0][:, h:h + 1] * o_refs[0][:, sl]
            for gi in range(1, ng):
                o = o + ws[gi][:, h:h + 1] * o_refs[gi][:, sl]
            z = z_ref[:, sl].astype(F32)
            om_ref[:, sl] = o.astype(BF16)
            y_ref[:, sl] = (o * (z * _sigmoid(z))).astype(BF16)

    row = pl.BlockSpec((tr, w), lambda i: (i, 0))
    lrow = pl.BlockSpec((tr, A_HEADS), lambda i: (i, 0))
    return pl.pallas_call(
        body, name="merge_gate_fwd", grid=(n // tr,),
        in_specs=[row] * ng + [lrow] * ng + [pl.BlockSpec((tr, w), lambda i: (i, z_block))],
        out_specs=(row, row, lrow),
        out_shape=(jax.ShapeDtypeStruct((n, w), BF16), jax.ShapeDtypeStruct((n, w), BF16),
                   jax.ShapeDtypeStruct((n, A_HEADS), F32)),
        compiler_params=_params(("parallel",)),
    )(*outs, *lses, proj)


def _gate_bwd(dy, o, z_arr, z_block, *, name, with_delta, tr=256):
    n, w = dy.shape
    tr = _tile(n, tr)

    def body(*refs):
        dy_ref, o_ref, z_ref, do_ref, dz_ref = refs[:5]
        dyv = dy_ref[...].astype(F32)
        ov = o_ref[...].astype(F32)
        z = z_ref[...].astype(F32)
        sig = _sigmoid(z)
        do = dyv * (z * sig)
        do_ref[...] = do.astype(BF16)
        dz_ref[...] = (dyv * ov * (sig * (1.0 + z * (1.0 - sig)))).astype(BF16)
        if with_delta:
            prod = do * ov
            cols = [jnp.sum(prod[:, h * A_HEAD_DIM:(h + 1) * A_HEAD_DIM], axis=-1, keepdims=True)
                    for h in range(A_HEADS)]
            refs[5][...] = _lane_place(cols, A_HEADS)

    row = pl.BlockSpec((tr, w), lambda i: (i, 0))
    out_specs = [row, row]
    out_shape = [jax.ShapeDtypeStruct((n, w), BF16), jax.ShapeDtypeStruct((n, w), BF16)]
    if with_delta:
        out_specs.append(pl.BlockSpec((tr, A_HEADS), lambda i: (i, 0)))
        out_shape.append(jax.ShapeDtypeStruct((n, A_HEADS), F32))
    return pl.pallas_call(
        body, name=name, grid=(n // tr,),
        in_specs=[row, row, pl.BlockSpec((tr, w), lambda i: (i, z_block))],
        out_specs=tuple(out_specs), out_shape=tuple(out_shape), compiler_params=_params(("parallel",)),
    )(dy, o, z_arr)


def _loss_fwd_bwd(h, target, *, tr=512):
    n, d = h.shape
    tr = _tile(n, tr)
    steps = n // tr

    def body(h_ref, t_ref, dh_ref, loss_ref, acc_ref):
        i = pl.program_id(0)
        e = h_ref[...] - t_ref[...]
        dh_ref[...] = e / d
        part = (e * e).reshape(tr // 8, 8, d).sum(axis=0)

        @pl.when(i == 0)
        def _():
            acc_ref[...] = part

        @pl.when(i > 0)
        def _():
            acc_ref[...] += part

        @pl.when(i == steps - 1)
        def _():
            s = jnp.sum(jnp.sum(acc_ref[...], axis=-1, keepdims=True), axis=0, keepdims=True)
            loss_ref[...] = 0.5 * s / d

    row = pl.BlockSpec((tr, d), lambda i: (i, 0))
    return pl.pallas_call(
        body, name="loss", grid=(steps,), in_specs=[row, row],
        out_specs=(row, pl.BlockSpec((1, 1), lambda i: (0, 0))),
        out_shape=(jax.ShapeDtypeStruct((n, d), F32), jax.ShapeDtypeStruct((1, 1), F32)),
        scratch_shapes=[pltpu.VMEM((8, d), F32)], compiler_params=_params(("arbitrary",)),
    )(h, target)


def _dot_nt(a, b):
    return lax.dot_general(a, b, (((1,), (1,)), ((), ())), preferred_element_type=F32)


def _dot_nn(a, b):
    return lax.dot_general(a, b, (((1,), (0,)), ((), ())), preferred_element_type=F32)


def _attn_a_fwd(qkv, cb0, qb, out_dtype, *, name):
    bl, dil, ln, _ = qkv.shape
    nb = ln // qb
    hw = A_WIDTH
    heads = range(A_HEADS)
    sls = [slice(h * A_HEAD_DIM, (h + 1) * A_HEAD_DIM) for h in heads]

    def body(*refs):
        if nb > 1:
            q_ref, kc_ref, vc_ref, kp_ref, vp_ref, o_ref, lse_ref = refs
        else:
            q_ref, kc_ref, vc_ref, o_ref, lse_ref = refs
        i = pl.program_id(2)
        qi = lax.broadcasted_iota(jnp.int32, (qb, qb), 0)
        ki = lax.broadcasted_iota(jnp.int32, (qb, qb), 1)
        mask_c = ki <= qi
        mask_p = jnp.logical_and(ki >= qi, i >= 1)
        s_c = [jnp.where(mask_c, _dot_nt(q_ref[:, sls[h]], kc_ref[:, sls[h]]), NEG) for h in heads]
        m = [jnp.max(s_c[h], axis=-1, keepdims=True) for h in heads]
        if nb > 1:
            s_p = [jnp.where(mask_p, _dot_nt(q_ref[:, sls[h]], kp_ref[:, sls[h]]), NEG) for h in heads]
            m = [jnp.maximum(m[h], jnp.max(s_p[h], axis=-1, keepdims=True)) for h in heads]
        p_c = [jnp.exp2(s_c[h] - m[h]) for h in heads]
        l = [jnp.sum(p_c[h], axis=-1, keepdims=True) for h in heads]
        acc = [_dot_nn(p_c[h].astype(BF16), vc_ref[:, sls[h]]) for h in heads]
        if nb > 1:
            p_p = [jnp.exp2(s_p[h] - m[h]) for h in heads]
            l = [l[h] + jnp.sum(p_p[h], axis=-1, keepdims=True) for h in heads]
            acc = [acc[h] + _dot_nn(p_p[h].astype(BF16), vp_ref[:, sls[h]]) for h in heads]
        for h in heads:
            o_ref[:, sls[h]] = (acc[h] / l[h]).astype(o_ref.dtype)
        lse_ref[...] = _lane_place([m[h] + jnp.log2(l[h]) for h in heads], A_HEADS)

    def spec(off, prev):
        if prev:
            return pl.BlockSpec((None, None, qb, hw), lambda b, r, i: (b, r, jnp.maximum(i - 1, 0), cb0 + off))
        return pl.BlockSpec((None, None, qb, hw), lambda b, r, i: (b, r, i, cb0 + off))

    return pl.pallas_call(
        body, name=name, grid=(bl, dil, nb),
        in_specs=[spec(0, False), spec(1, False), spec(2, False)] + ([spec(1, True), spec(2, True)] if nb > 1 else []),
        out_specs=(pl.BlockSpec((None, None, qb, hw), lambda b, r, i: (b, r, i, 0)),
                   pl.BlockSpec((None, None, qb, A_HEADS), lambda b, r, i: (b, r, i, 0))),
        out_shape=(jax.ShapeDtypeStruct((bl, dil, ln, hw), out_dtype),
                   jax.ShapeDtypeStruct((bl, dil, ln, A_HEADS), F32)),
        compiler_params=_params(("parallel", "parallel", "arbitrary")),
    )(*([qkv] * (5 if nb > 1 else 3)))


def _attn_a_bwd(qkv, cb0, do, lse, delta, lse_t, delta_t, tabs, qb, *, name):
    bl, dil, ln, _ = qkv.shape
    nb = ln // qb
    hw = A_WIDTH

    def body(*refs):
        if nb > 1:
            (q_ref, kc_ref, vc_ref, do_ref, lse_ref, dl_ref, lt_ref, dt_ref, tc, tsa, tsb,
             qn_ref, kp_ref, vp_ref, don_ref, ltn_ref, dtn_ref, o_ref) = refs
        else:
            q_ref, kc_ref, vc_ref, do_ref, lse_ref, dl_ref, lt_ref, dt_ref, tc, tsa, tsb, o_ref = refs
        i = pl.program_id(2)
        row = lax.broadcasted_iota(jnp.int32, (qb, qb), 0)
        col = lax.broadcasted_iota(jnp.int32, (qb, qb), 1)
        m_qc = col <= row
        m_kc = row <= col
        m_qp = jnp.logical_and(col >= row, i >= 1)
        m_kn = jnp.logical_and(row >= col, i + 1 < nb)
        c, sa, sb = tc[...], tsa[...], tsb[...]
        heads = range(A_HEADS)
        sls = [slice(h * A_HEAD_DIM, (h + 1) * A_HEAD_DIM) for h in heads]
        q, kc = [q_ref[:, sl] for sl in sls], [kc_ref[:, sl] for sl in sls]
        vc, dov = [vc_ref[:, sl] for sl in sls], [do_ref[:, sl] for sl in sls]
        lse_c = [lse_ref[:, h:h + 1] for h in heads]
        dl_c = [dl_ref[:, h:h + 1] for h in heads]
        s = [_dot_nt(q[h], kc[h]) for h in heads]
        st = [_dot_nt(kc[h], q[h]) for h in heads]
        dp = [_dot_nt(dov[h], vc[h]) for h in heads]
        dpt = [_dot_nt(vc[h], dov[h]) for h in heads]
        p = [jnp.exp2(jnp.where(m_qc, s[h], NEG) - lse_c[h]) for h in heads]
        pt = [jnp.exp2(jnp.where(m_kc, st[h], NEG) - lt_ref[h:h + 1, :]) for h in heads]
        dq = [_dot_nn((p[h] * (dp[h] - dl_c[h])).astype(BF16), kc[h]) for h in heads]
        dk = [_dot_nn((pt[h] * (dpt[h] - dt_ref[h:h + 1, :])).astype(BF16), q[h]) for h in heads]
        dv = [_dot_nn(pt[h].astype(BF16), dov[h]) for h in heads]
        if nb > 1:
            kp, vp = [kp_ref[:, sl] for sl in sls], [vp_ref[:, sl] for sl in sls]
            qn, don = [qn_ref[:, sl] for sl in sls], [don_ref[:, sl] for sl in sls]
            s = [_dot_nt(q[h], kp[h]) for h in heads]
            st = [_dot_nt(kc[h], qn[h]) for h in heads]
            dp = [_dot_nt(dov[h], vp[h]) for h in heads]
            dpt = [_dot_nt(vc[h], don[h]) for h in heads]
            p = [jnp.exp2(jnp.where(m_qp, s[h], NEG) - lse_c[h]) for h in heads]
            pt = [jnp.exp2(jnp.where(m_kn, st[h], NEG) - ltn_ref[h:h + 1, :]) for h in heads]
            dq = [dq[h] + _dot_nn((p[h] * (dp[h] - dl_c[h])).astype(BF16), kp[h]) for h in heads]
            dk = [dk[h] + _dot_nn((pt[h] * (dpt[h] - dtn_ref[h:h + 1, :])).astype(BF16), qn[h]) for h in heads]
            dv = [dv[h] + _dot_nn(pt[h].astype(BF16), don[h]) for h in heads]
        for h in heads:
            o_ref[:, h * A_HEAD_DIM:(h + 1) * A_HEAD_DIM] = _rope_apply(dq[h] * A_SCALE, c, sa, sb, -1).astype(BF16)
            o_ref[:, hw + h * A_HEAD_DIM:hw + (h + 1) * A_HEAD_DIM] = _rope_apply(dk[h] * LN2, c, sa, sb, -1).astype(BF16)
            o_ref[:, 2 * hw + h * A_HEAD_DIM:2 * hw + (h + 1) * A_HEAD_DIM] = dv[h].astype(BF16)

    def cur(w, col):
        return pl.BlockSpec((None, None, qb, w), lambda b, r, i: (b, r, i, col))

    def prev(w, col):
        return pl.BlockSpec((None, None, qb, w), lambda b, r, i: (b, r, jnp.maximum(i - 1, 0), col))

    def nxt(w, col):
        return pl.BlockSpec((None, None, qb, w), lambda b, r, i: (b, r, jnp.minimum(i + 1, nb - 1), col))

    t_cur = pl.BlockSpec((None, None, A_HEADS, qb), lambda b, r, i: (b, r, 0, i))
    t_nxt = pl.BlockSpec((None, None, A_HEADS, qb), lambda b, r, i: (b, r, 0, jnp.minimum(i + 1, nb - 1)))
    in_specs = [cur(hw, cb0), cur(hw, cb0 + 1), cur(hw, cb0 + 2), cur(hw, 0), cur(A_HEADS, 0), cur(A_HEADS, 0),
                t_cur, t_cur, cur(LANES, 0), cur(LANES, 0), cur(LANES, 0)]
    operands = [qkv, qkv, qkv, do, lse, delta, lse_t, delta_t, *tabs]
    if nb > 1:
        in_specs += [nxt(hw, cb0), prev(hw, cb0 + 1), prev(hw, cb0 + 2), nxt(hw, 0), t_nxt, t_nxt]
        operands += [qkv, qkv, qkv, do, lse_t, delta_t]
    return pl.pallas_call(
        body, name=name, grid=(bl, dil, nb), in_specs=in_specs, out_specs=cur(3 * hw, 0),
        out_shape=jax.ShapeDtypeStruct((bl, dil, ln, 3 * hw), BF16),
        compiler_params=_params(("parallel", "parallel", "arbitrary")),
    )(*operands)


def _head_terms(do, o, lse, e):
    rows = do.shape[0]
    lane = lax.broadcasted_iota(jnp.int32, (rows, LANES), 1)
    mine = (lane < B_VDIM) if e == 0 else (lane >= B_VDIM)
    prod = do.astype(F32) * o.astype(F32)
    dl = jnp.sum(jnp.where(mine, prod, 0.0), axis=-1, keepdims=True)
    do_e = jnp.where(mine, do, jnp.zeros_like(do))
    return do_e, dl, lse[:, e * B_VDIM:e * B_VDIM + 1]


def _col_to_row(col, rows):
    return jnp.transpose(jnp.broadcast_to(col, (rows, LANES)))[0:1, :]


def _mla_fwd(q_cat, kvup, kr, z, tq):
    bl, t, _ = q_cat.shape
    nq = t // tq
    pairs = B_HEADS // 2
    v_blk0 = (B_HEADS * LANES) // LANES

    def body(q_ref, k_ref, v_ref, kr_ref, z_ref, y_ref, o_ref, lse_ref, lrow_ref, m_ref, acc_ref):
        qi = pl.program_id(2)
        qs = [q_ref[:, e * LANES:(e + 1) * LANES] for e in range(2)]
        row = lax.broadcasted_iota(jnp.int32, (tq, tq), 0)
        col = lax.broadcasted_iota(jnp.int32, (tq, tq), 1)
        tri = col <= row
        sum_lane = [B_VDIM, 0]

        for e in range(2):
            m_ref[e] = jnp.full((tq, LANES), NEG, F32)
            acc_ref[e] = jnp.zeros((tq, LANES), F32)

        def tile(k0, w, masked):
            lane = lax.broadcasted_iota(jnp.int32, (w, LANES), 1)
            first = lane < B_VDIM
            krv = kr_ref[pl.ds(k0, w), :]
            v = v_ref[pl.ds(k0, w), :]
            vs = [jnp.where(first, v, jnp.where(lane == B_VDIM, 1.0, 0.0).astype(BF16)),
                  jnp.where(first, jnp.where(lane == 0, 1.0, 0.0).astype(BF16), v)]
            ss = []
            for e in range(2):
                k = k_ref[pl.ds(k0, w), e * LANES:(e + 1) * LANES] + krv
                s = _dot_nt(qs[e], k)
                ss.append(jnp.where(tri, s, NEG) if masked else s)
            m_old = [m_ref[e] for e in range(2)]
            ms = [jnp.maximum(m_old[e], jnp.max(ss[e], axis=-1, keepdims=True)) for e in range(2)]
            ps = [jnp.exp2(ss[e] - jnp.concatenate([ms[e]] * (w // LANES), axis=1)).astype(BF16) for e in range(2)]
            for e in range(2):
                m_ref[e] = ms[e]
                acc_ref[e] = jnp.exp2(m_old[e] - ms[e]) * acc_ref[e] + _dot_nn(ps[e], vs[e])

        def step(kb2, carry):
            tile(pl.multiple_of(kb2 * 2 * tq, 2 * tq), 2 * tq, False)
            return carry

        lax.fori_loop(0, qi // 2, step, 0)

        @pl.when(qi % 2 == 1)
        def _():
            tile(pl.multiple_of((qi - 1) * tq, tq), tq, False)

        tile(pl.multiple_of(qi * tq, tq), tq, True)
        lane = lax.broadcasted_iota(jnp.int32, (tq, LANES), 1)
        first = lane < B_VDIM
        accs = [acc_ref[e] for e in range(2)]
        ls = [accs[e][:, sum_lane[e]:sum_lane[e] + 1] for e in range(2)]
        outs = [accs[e] / ls[e] for e in range(2)]
        lses = [m_ref[e] + jnp.log2(ls[e]) for e in range(2)]
        o = jnp.where(first, outs[0], outs[1])
        zv = z_ref[...].astype(F32)
        o_ref[...] = o.astype(BF16)
        y_ref[...] = (o * (zv * _sigmoid(zv))).astype(BF16)
        lse_ref[...] = jnp.where(first, lses[0], lses[1])
        for e in range(2):
            lrow_ref[e:e + 1, :] = jnp.transpose(lses[e])[0:1, :]

    blk = pl.BlockSpec((None, tq, LANES), lambda b, j, i: (b, i, j))
    return pl.pallas_call(
        body, name="mla_fwd", grid=(bl, pairs, nq),
        in_specs=[pl.BlockSpec((None, tq, 2 * LANES), lambda b, j, i: (b, i, j)),
                  pl.BlockSpec((None, t, 2 * LANES), lambda b, j, i: (b, 0, j)),
                  pl.BlockSpec((None, t, LANES), lambda b, j, i: (b, 0, v_blk0 + j)),
                  pl.BlockSpec((None, t, LANES), lambda b, j, i: (b, 0, 0)),
                  blk],
        out_specs=(blk, blk, blk, pl.BlockSpec((None, None, None, 2, tq), lambda b, j, i: (b, j, i, 0, 0))),
        out_shape=(jax.ShapeDtypeStruct((bl, t, B_WIDTH), BF16), jax.ShapeDtypeStruct((bl, t, B_WIDTH), BF16),
                   jax.ShapeDtypeStruct((bl, t, B_WIDTH), F32),
                   jax.ShapeDtypeStruct((bl, pairs, nq, 2, tq), F32)),
        scratch_shapes=[pltpu.VMEM((2, tq, LANES), F32), pltpu.VMEM((2, tq, LANES), F32)],
        compiler_params=_params(("parallel", "parallel", "arbitrary")),
    )(q_cat, kvup, kvup, kr, z)


def _mla_dq(q_cat, kvup, kr, do, o, lse, tabs, tq):
    bl, t, _ = q_cat.shape
    nq = t // tq
    pairs = B_HEADS // 2
    v_blk0 = (B_HEADS * LANES) // LANES

    def body(q_ref, k_ref, v_ref, kr_ref, do_ref, o_ref, lse_ref, tc, tsa, tsb, dq_ref, drow_ref, acc_ref):
        qi = pl.program_id(2)
        dov, ov, lsev = do_ref[...], o_ref[...], lse_ref[...]
        qs = [q_ref[:, e * LANES:(e + 1) * LANES] for e in range(2)]
        terms = [_head_terms(dov, ov, lsev, e) for e in range(2)]
        row = lax.broadcasted_iota(jnp.int32, (tq, tq), 0)
        col = lax.broadcasted_iota(jnp.int32, (tq, tq), 1)
        tri = col <= row
        for e in range(2):
            acc_ref[e] = jnp.zeros((tq, LANES), F32)

        def tile(k0, w, masked):
            krv = kr_ref[pl.ds(k0, w), :]
            v = v_ref[pl.ds(k0, w), :]
            ks = [k_ref[pl.ds(k0, w), e * LANES:(e + 1) * LANES] + krv for e in range(2)]
            ss = [_dot_nt(qs[e], ks[e]) for e in range(2)]
            dps = [_dot_nt(terms[e][0], v) for e in range(2)]
            for e in range(2):
                s = jnp.where(tri, ss[e], NEG) if masked else ss[e]
                p = jnp.exp2(s - terms[e][2])
                ds = (p * (dps[e] - terms[e][1])).astype(BF16)
                acc_ref[e] += _dot_nn(ds, ks[e])

        def step(kb2, carry):
            tile(pl.multiple_of(kb2 * 2 * tq, 2 * tq), 2 * tq, False)
            return carry

        lax.fori_loop(0, qi // 2, step, 0)

        @pl.when(qi % 2 == 1)
        def _():
            tile(pl.multiple_of((qi - 1) * tq, tq), tq, False)

        tile(pl.multiple_of(qi * tq, tq), tq, True)
        for e in range(2):
            dq_ref[:, e * LANES:(e + 1) * LANES] = _rope_apply(acc_ref[e] * B_SCALE, tc[...], tsa[...], tsb[...], -1).astype(BF16)
            drow_ref[e:e + 1, :] = _col_to_row(terms[e][1], tq)

    blk = pl.BlockSpec((None, tq, LANES), lambda b, j, i: (b, i, j))
    tab = pl.BlockSpec((None, tq, LANES), lambda b, j, i: (b, i, 0))
    qblk = pl.BlockSpec((None, tq, 2 * LANES), lambda b, j, i: (b, i, j))
    return pl.pallas_call(
        body, name="mla_dq", grid=(bl, pairs, nq),
        in_specs=[qblk,
                  pl.BlockSpec((None, t, 2 * LANES), lambda b, j, i: (b, 0, j)),
                  pl.BlockSpec((None, t, LANES), lambda b, j, i: (b, 0, v_blk0 + j)),
                  pl.BlockSpec((None, t, LANES), lambda b, j, i: (b, 0, 0)),
                  blk, blk, blk, tab, tab, tab],
        out_specs=(qblk, pl.BlockSpec((None, None, None, 2, tq), lambda b, j, i: (b, j, i, 0, 0))),
        out_shape=(jax.ShapeDtypeStruct((bl, t, B_HEADS * LANES), BF16),
                   jax.ShapeDtypeStruct((bl, pairs, nq, 2, tq), F32)),
        scratch_shapes=[pltpu.VMEM((2, tq, LANES), F32)],
        compiler_params=_params(("parallel", "parallel", "arbitrary")),
    )(q_cat, kvup, kvup, kr, do, o, lse, *tabs)


def _mla_dkv(q_cat, kvup, kr, do, lse_rows, delta_rows, tq):
    bl, t, _ = q_cat.shape
    nq = t // tq
    pairs = B_HEADS // 2
    v_blk0 = (B_HEADS * LANES) // LANES

    def body(q_ref, k_ref, v_ref, kr_ref, do_ref, lrow_ref, drow_ref, dk_ref, dv_ref, acc_ref):
        kb = pl.program_id(2)
        v = v_ref[...]
        krv = kr_ref[...]
        ks = [k_ref[:, e * LANES:(e + 1) * LANES] + krv for e in range(2)]
        krow = lax.broadcasted_iota(jnp.int32, (tq, tq), 0)
        qcol = lax.broadcasted_iota(jnp.int32, (tq, tq), 1)
        tri = krow <= qcol
        lane = lax.broadcasted_iota(jnp.int32, (tq, LANES), 1)
        mine = [lane < B_VDIM, lane >= B_VDIM]

        for e in range(3):
            acc_ref[e] = jnp.zeros((tq, LANES), F32)

        def tile(qb, nblk, masked):
            w = nblk * tq
            rows = pl.ds(pl.multiple_of(qb * tq, tq), w)
            dov = do_ref[rows, :]
            lane_w = lax.broadcasted_iota(jnp.int32, (w, LANES), 1)
            mine_w = [lane_w < B_VDIM, lane_w >= B_VDIM]
            qs = [q_ref[rows, e * LANES:(e + 1) * LANES] for e in range(2)]
            does = [jnp.where(mine_w[e], dov, jnp.zeros_like(dov)) for e in range(2)]
            sts = [_dot_nt(ks[e], qs[e]) for e in range(2)]
            dpts = [_dot_nt(v, does[e]) for e in range(2)]

            def rows_of(ref, e):
                return jnp.concatenate([ref[qb + i, e:e + 1, :] for i in range(nblk)], axis=1)

            pts = []
            for e in range(2):
                st = jnp.where(tri, sts[e], NEG) if masked else sts[e]
                pts.append(jnp.exp2(st - rows_of(lrow_ref, e)))
            acc_ref[2] += _dot_nn(pts[0].astype(BF16), does[0]) + _dot_nn(pts[1].astype(BF16), does[1])
            for e in range(2):
                dst = (pts[e] * (dpts[e] - rows_of(drow_ref, e))).astype(BF16)
                acc_ref[e] += _dot_nn(dst, qs[e])

        tile(kb, 1, True)
        rest = nq - 1 - kb
        odd = rest % 2

        @pl.when(odd == 1)
        def _():
            tile(kb + 1, 1, False)

        def step(i, carry):
            tile(kb + 1 + odd + 2 * i, 2, False)
            return carry

        lax.fori_loop(0, rest // 2, step, 0)
        dk_ref[:, 0:LANES] = (acc_ref[0] * LN2).astype(BF16)
        dk_ref[:, LANES:2 * LANES] = (acc_ref[1] * LN2).astype(BF16)
        dv_ref[...] = acc_ref[2].astype(BF16)

    full = pl.BlockSpec((None, t, LANES), lambda b, j, i: (b, 0, j))
    rows = pl.BlockSpec((None, None, nq, 2, tq), lambda b, j, i: (b, j, 0, 0, 0))
    kblk = pl.BlockSpec((None, tq, 2 * LANES), lambda b, j, i: (b, i, j))
    return pl.pallas_call(
        body, name="mla_dkv", grid=(bl, pairs, nq),
        in_specs=[pl.BlockSpec((None, t, 2 * LANES), lambda b, j, i: (b, 0, j)),
                  kblk,
                  pl.BlockSpec((None, tq, LANES), lambda b, j, i: (b, i, v_blk0 + j)),
                  pl.BlockSpec((None, tq, LANES), lambda b, j, i: (b, i, 0)),
                  full, rows, rows],
        out_specs=(kblk, pl.BlockSpec((None, tq, LANES), lambda b, j, i: (b, i, j))),
        out_shape=(jax.ShapeDtypeStruct((bl, t, B_HEADS * LANES), BF16),
                   jax.ShapeDtypeStruct((bl, t, B_WIDTH), BF16)),
        scratch_shapes=[pltpu.VMEM((3, tq, LANES), F32)],
        compiler_params=_params(("parallel", "parallel", "arbitrary")),
    )(q_cat, kvup, kvup, kr, do, lse_rows, delta_rows)


def _adamw(w, g, m, v, *, name):
    r, c = w.shape
    tr = _row_tile(r, 256)
    c1 = 1.0 - ADAM_B1
    c2 = 1.0 - ADAM_B2
    bc1 = 1.0 - ADAM_B1 ** ADAM_STEP
    bc2 = 1.0 - ADAM_B2 ** ADAM_STEP

    def body(w_ref, g_ref, m_ref, v_ref, d_ref, nm_ref, nv_ref):
        gv = g_ref[...]
        nm = ADAM_B1 * m_ref[...] + c1 * gv
        nv = ADAM_B2 * v_ref[...] + c2 * (gv * gv)
        nm_ref[...] = nm
        nv_ref[...] = nv
        d_ref[...] = -ADAM_LR * ((nm / bc1) / (jnp.sqrt(nv / bc2) + ADAM_EPS) + ADAM_WD * w_ref[...])

    blk = pl.BlockSpec((tr, c), lambda i: (i, 0))
    sds = jax.ShapeDtypeStruct((r, c), F32)
    return pl.pallas_call(
        body, name=name, grid=(r // tr,), in_specs=[blk] * 4, out_specs=(blk,) * 3,
        out_shape=(sds,) * 3, compiler_params=_params(("parallel",)),
    )(w, g, m, v)


def _add_my_half(stacked, other, core, out_dtype, *, name):
    nch, a, c = stacked.shape
    h = a // 2
    tr = _row_tile(h, 256)
    nblk = h // tr

    def body(core_ref, s_ref, p_ref, o_ref):
        o_ref[...] = (s_ref[...] + p_ref[...]).astype(o_ref.dtype)

    return pl.pallas_call(
        body, name=name,
        grid_spec=pltpu.PrefetchScalarGridSpec(
            num_scalar_prefetch=1, grid=(nch, nblk),
            in_specs=[pl.BlockSpec((None, tr, c), lambda k, i, cr: (k, cr[0] * nblk + i, 0)),
                      pl.BlockSpec((None, tr, c), lambda k, i, cr: (k, i, 0))],
            out_specs=pl.BlockSpec((None, tr, c), lambda k, i, cr: (k, i, 0))),
        out_shape=jax.ShapeDtypeStruct((nch, h, c), out_dtype),
        compiler_params=_params(("parallel", "parallel")),
    )(core, stacked, other)


def _sum_chips(parts, own, chip, *, name):
    nch, h, c = parts.shape
    tr = _row_tile(h, 256)

    def body(chip_ref, p_ref, own_ref, o_ref):
        me = chip_ref[0]

        def slot(k):
            return jnp.where(me == k, own_ref[k].astype(F32), p_ref[k].astype(F32))

        acc = slot(0) + slot(1)
        for k in range(2, nch):
            acc = acc + slot(k)
        o_ref[...] = acc

    blk = pl.BlockSpec((nch, tr, c), lambda i, cr: (0, i, 0))
    return pl.pallas_call(
        body, name=name,
        grid_spec=pltpu.PrefetchScalarGridSpec(
            num_scalar_prefetch=1, grid=(h // tr,), in_specs=[blk, blk],
            out_specs=pl.BlockSpec((tr, c), lambda i, cr: (i, 0))),
        out_shape=jax.ShapeDtypeStruct((h, c), F32), compiler_params=_params(("parallel",)),
    )(chip, parts, own)


def _join_halves(mine, other, core, *, name):
    h, c = mine.shape
    tr = _row_tile(h, 256)
    nblk = h // tr

    def body(core_ref, m_ref, s_ref, o_ref):
        is_mine = pl.program_id(0) // nblk == core_ref[0]

        @pl.when(is_mine)
        def _():
            o_ref[...] = m_ref[...]

        @pl.when(jnp.logical_not(is_mine))
        def _():
            o_ref[...] = s_ref[...]

    blk = pl.BlockSpec((tr, c), lambda i, cr: (i % nblk, 0))
    return pl.pallas_call(
        body, name=name,
        grid_spec=pltpu.PrefetchScalarGridSpec(
            num_scalar_prefetch=1, grid=(2 * nblk,), in_specs=[blk, blk],
            out_specs=pl.BlockSpec((tr, c), lambda i, cr: (i, 0))),
        out_shape=jax.ShapeDtypeStruct((2 * h, c), F32), compiler_params=_params(("arbitrary",)),
    )(core, mine, other)


def _place():
    x, y, c = lax.axis_index("x"), lax.axis_index("y"), lax.axis_index("c")
    chips = [(1 - x, y), (x, 1 - y), (1 - x, 1 - y)]
    return x, y, c, chips


def _remote(src, dst, send_sems, recv_sems, k, to):
    return pltpu.make_async_remote_copy(src_ref=src, dst_ref=dst, send_sem=send_sems.at[k],
                                        recv_sem=recv_sems.at[k], device_id=to, device_id_type=MESH)


def _hbm_call(body, name, ins, out_shapes, n_remote):
    any_spec = pl.BlockSpec(memory_space=pl.ANY)
    return pl.pallas_call(
        body, name=name, in_specs=[any_spec] * len(ins), out_specs=tuple([any_spec] * len(out_shapes)),
        out_shape=tuple(out_shapes),
        scratch_shapes=[pltpu.SemaphoreType.DMA((n_remote,)), pltpu.SemaphoreType.DMA((n_remote,))],
    )(*ins)


def _all_gather_chips(shards, *, name):
    n = len(shards)

    def body(*refs):
        ins, outs = refs[:n], refs[n:2 * n]
        send_sems, recv_sems = refs[2 * n:]
        x, y, c, chips = _place()
        me = 2 * x + y
        sent = []
        for s in range(n):
            h = ins[s].shape[0] // 2
            for j, (px, py) in enumerate(chips):
                cp = _remote(ins[s].at[pl.ds(c * h, h)], outs[s].at[me, pl.ds(c * h, h)],
                             send_sems, recv_sems, s * 6 + j, (px, py, c))
                cp.start()
                sent.append(cp)
        for s in range(n):
            h = ins[s].shape[0] // 2
            for j, (px, py) in enumerate(chips):
                slab = outs[s].at[2 * px + py, pl.ds(c * h, h)]
                _remote(slab, slab, send_sems, recv_sems, s * 6 + j, (px, py, c)).wait_recv()
                cp = _remote(slab, slab, send_sems, recv_sems, s * 6 + 3 + j, (x, y, 1 - c))
                cp.start()
                sent.append(cp)
        for s in range(n):
            h = ins[s].shape[0] // 2
            for j, (px, py) in enumerate(chips):
                slab = outs[s].at[2 * px + py, pl.ds((1 - c) * h, h)]
                _remote(slab, slab, send_sems, recv_sems, s * 6 + 3 + j, (x, y, 1 - c)).wait_recv()
        for cp in sent:
            cp.wait_send()

    out_shapes = [jax.ShapeDtypeStruct((N_CHIPS,) + s.shape, s.dtype) for s in shards]
    return _hbm_call(body, name, shards, out_shapes, 6 * n)


def _pair_send_other_half(stacked, *, name):
    n = len(stacked)

    def body(*refs):
        ins, outs = refs[:n], refs[n:2 * n]
        send_sems, recv_sems = refs[2 * n:]
        x, y, c, _chips = _place()
        sent = []
        for s in range(n):
            h = ins[s].shape[1] // 2
            cp = _remote(ins[s].at[:, pl.ds((1 - c) * h, h)], outs[s], send_sems, recv_sems, s, (x, y, 1 - c))
            cp.start()
            sent.append(cp)
        for cp in sent:
            cp.wait_recv()
        for cp in sent:
            cp.wait_send()

    out_shapes = [jax.ShapeDtypeStruct((s.shape[0], s.shape[1] // 2, s.shape[2]), s.dtype) for s in stacked]
    return _hbm_call(body, name, stacked, out_shapes, n)


def _chip_exchange(halves, *, name):
    n = len(halves)

    def body(*refs):
        ins, outs = refs[:n], refs[n:2 * n]
        send_sems, recv_sems = refs[2 * n:]
        x, y, c, chips = _place()
        me = 2 * x + y
        sent = []
        for s in range(n):
            for j, (px, py) in enumerate(chips):
                cp = _remote(ins[s].at[2 * px + py], outs[s].at[me], send_sems, recv_sems, s * 3 + j, (px, py, c))
                cp.start()
                sent.append(cp)
        for s in range(n):
            for j, (px, py) in enumerate(chips):
                slab = outs[s].at[2 * px + py]
                _remote(slab, slab, send_sems, recv_sems, s * 3 + j, (px, py, c)).wait_recv()
        for cp in sent:
            cp.wait_send()

    out_shapes = [jax.ShapeDtypeStruct(s.shape, s.dtype) for s in halves]
    return _hbm_call(body, name, halves, out_shapes, 3 * n)


def _pair_swap(halves, *, name):
    n = len(halves)

    def body(*refs):
        ins, outs = refs[:n], refs[n:2 * n]
        send_sems, recv_sems = refs[2 * n:]
        x, y, c, _chips = _place()
        sent = []
        for s in range(n):
            cp = _remote(ins[s], outs[s], send_sems, recv_sems, s, (x, y, 1 - c))
            cp.start()
            sent.append(cp)
        for cp in sent:
            cp.wait_recv()
        for cp in sent:
            cp.wait_send()

    out_shapes = [jax.ShapeDtypeStruct(s.shape, s.dtype) for s in halves]
    return _hbm_call(body, name, halves, out_shapes, n)


def _pack_rows(parts, row_multiple):
    flat = jnp.concatenate([p.reshape(-1) for p in parts])
    quantum = row_multiple * PACK_COLS
    pad = (-flat.shape[0]) % quantum
    flat = jnp.pad(flat, (0, pad))
    return flat.reshape(-1, PACK_COLS)


def _unpack(flat, shapes):
    out, pos = [], 0
    for shp in shapes:
        size = math.prod(shp)
        out.append(flat[pos:pos + size].reshape(shp))
        pos += size
    return out


def _to_chunks_cols(full):
    r, c4 = full.shape
    return full.reshape(r, N_CHIPS, c4 // N_CHIPS).transpose(1, 0, 2)


def _from_chunks_cols(stacked):
    nch, r, c = stacked.shape
    return stacked.transpose(1, 0, 2).reshape(r, nch * c)


def _class_major(a, bl, t, dil):
    w = a.shape[-1]
    if dil == 1:
        return a.reshape(bl, 1, t, w)
    return a.reshape(bl, t // dil, dil, w).transpose(0, 2, 1, 3)


def _natural(a):
    bl, dil, ln, w = a.shape
    if dil == 1:
        return a.reshape(bl * ln, w)
    return a.transpose(0, 2, 1, 3).reshape(bl * ln * dil, w)


def _train_step(x, positions, a_pre_norm, a_w_in, a_w_out, a_post_norm, kv_norm, kv_w_down, kv_latent_norm,
                kv_w_up, b_pre_norm, b_w_in, b_q_norm, b_w_q_up, b_w_out, b_post_norm, loss_target, moments):
    bl, t, d = x.shape
    n = bl * t
    qb = t // A_DILATIONS[-1]
    tq = _tile(t, 256)
    dq4 = d // N_CHIPS
    chip = 2 * lax.axis_index("x") + lax.axis_index("y")
    chip_arr = chip.astype(jnp.int32).reshape(1)
    core_arr = lax.axis_index("c").astype(jnp.int32).reshape(1)

    w_in_a_s = a_w_in[0].astype(BF16)
    outs_s = jnp.concatenate([a_w_out[0], b_w_out[0]], axis=0).astype(BF16)
    small_shapes = [kv_w_down.shape, kv_w_up.shape, b_w_in[0].shape, b_w_q_up[0].shape]
    small_s = _pack_rows([kv_w_down, kv_w_up, b_w_in[0], b_w_q_up[0]], 32).astype(BF16)
    gains_s = jnp.pad(jnp.concatenate([a_pre_norm[0], a_post_norm[0]]), (0, 16 * LANES - 2 * dq4)).reshape(16, LANES)
    shards = [w_in_a_s, outs_s, small_s, gains_s]
    gathered = _all_gather_chips(shards, name="gather_weights")
    g_in_a, g_outs, g_small, g_gains = [lax.dynamic_update_index_in_dim(g, s, chip, 0)
                                        for g, s in zip(gathered, shards)]

    w_in_a = _from_chunks_cols(g_in_a)
    w_out_a = g_outs[:, :A_WIDTH // N_CHIPS].reshape(A_WIDTH, d)
    w_out_b = g_outs[:, A_WIDTH // N_CHIPS:].reshape(B_WIDTH, d)
    sm = [_unpack(g_small[k].reshape(-1), small_shapes) for k in range(N_CHIPS)]
    w_down = jnp.concatenate([sm[k][0] for k in range(N_CHIPS)], axis=0)
    w_up = jnp.concatenate([sm[k][1] for k in range(N_CHIPS)], axis=1)
    w_in_b = jnp.concatenate([sm[k][2] for k in range(N_CHIPS)], axis=1)
    w_q_up = jnp.concatenate([sm[k][3] for k in range(N_CHIPS)], axis=1)
    gflat = g_gains.reshape(N_CHIPS, -1)
    g_a_pre = gflat[:, :dq4].reshape(1, d)
    g_a_post = gflat[:, dq4:2 * dq4].reshape(1, d)

    w_up_h = w_up.reshape(B_KV_LORA, B_HEADS, B_NOPE + B_VDIM)
    w_up_k = jnp.pad(w_up_h[:, :, :B_NOPE], ((0, 0), (0, 0), (0, LANES - B_NOPE))).reshape(B_KV_LORA, B_HEADS * LANES)
    w_up_v = w_up_h[:, :, B_NOPE:].reshape(B_KV_LORA, B_WIDTH)
    w_up_cat = jnp.concatenate([w_up_k, w_up_v], axis=1)
    w_q_up_p = jnp.pad(w_q_up.reshape(B_Q_LORA, B_HEADS, B_QK_DIM),
                       ((0, 0), (0, 0), (0, LANES - B_QK_DIM))).reshape(B_Q_LORA, B_HEADS * LANES)
    zeros_d = lambda c: jnp.zeros((d, c), BF16)
    w_down_p = jnp.concatenate([w_down[:, :B_KV_LORA], zeros_d(B_NOPE), w_down[:, B_KV_LORA:],
                                zeros_d(LANES - B_NOPE - B_ROPE)], axis=1)
    w_cq = w_in_b[:, :B_Q_LORA]
    w_z = w_in_b[:, B_Q_LORA:]

    tabs_a = _rope_tables(positions, A_ROPE_THETA, 0)
    tabs_b = _rope_tables(positions, B_ROPE_THETA, B_NOPE)

    h0 = x.reshape(n, d)
    hn_a = _rms_fwd(h0, g_a_pre, BF16, name="a_pre_norm")
    is_qk = lambda j: j != 2
    is_q = lambda j: j == 0
    z_blk_a = 3 * A_GROUPS
    z_a = _matmul(hn_a, w_in_a, "nn", BF16, name="a_proj_z", b_cols=(z_blk_a, 1))
    o_groups, lse_groups, qkv_cm, hn_cm, tabs_cm = [], [], [], [], []
    for g, dil in enumerate(A_DILATIONS):
        flat = lambda a: _class_major(a, bl, t, dil).reshape(n, a.shape[-1])
        hn_g = hn_a if dil == 1 else flat(hn_a)
        tabs_g = tabs_a if dil == 1 else lax.optimization_barrier(tuple(flat(tb) for tb in tabs_a))
        proj_g = _matmul(hn_g, w_in_a, "nn", BF16, name=f"a_proj_{g}", rope=(tabs_g, is_qk),
                         out_scale=(A_SCALE * LOG2E, is_q), b_cols=(3 * g, 3))
        src = proj_g.reshape(bl, dil, t // dil, 3 * A_WIDTH)
        hn_cm.append(hn_g)
        tabs_cm.append(tabs_g)
        qkv_cm.append(src)
        o_g, lse_g = _attn_a_fwd(src, 0, qb, BF16, name=f"attn_a_fwd_{g}")
        o_groups.append(_natural(o_g))
        lse_groups.append(_natural(lse_g))
    ypre_a, om_a, lse_a = _merge_gate_fwd(o_groups, lse_groups, z_a, 0)
    y_a = _matmul(ypre_a, w_out_a, "nn", F32, name="a_out")
    g_kvn = kv_norm.reshape(1, d)
    g_lat = kv_latent_norm.reshape(1, B_KV_LORA)
    h1, hn_kv, hn_b = _post_norm_block(y_a, g_a_post, h0, [g_kvn, b_pre_norm], name="a_post_norm")

    ckr = _matmul(hn_kv, w_down_p, "nn", F32, name="kv_down")
    c_kv, k_rope = _kv_latent_fwd(ckr, g_lat, tabs_b)
    kvup = _matmul(c_kv, w_up_cat, "nn", BF16, name="kv_up")
    z_b = _matmul(hn_b, w_z, "nn", BF16, name="b_proj_z")
    cq_raw = _matmul(hn_b, w_cq, "nn", F32, name="b_proj_q")
    c_q = _rms_fwd(cq_raw, b_q_norm, BF16, name="b_q_norm")
    always = lambda j: True
    q_cat = _matmul(c_q, w_q_up_p, "nn", BF16, name="b_q_up", rope=(tabs_b, always),
                    out_scale=(B_SCALE * LOG2E, always))
    r3 = lambda a: a.reshape(bl, t, a.shape[-1])
    tabs_b3 = tuple(r3(tb) for tb in tabs_b)
    ypre_b, o_b, lse_b, lse_rows_b = _mla_fwd(r3(q_cat), r3(kvup), r3(k_rope), r3(z_b), tq)
    y_b = _matmul(ypre_b.reshape(n, B_WIDTH), w_out_b, "nn", F32, name="b_out")
    dh2, loss_part = _post_norm_loss(y_b, b_post_norm, h1, loss_target.reshape(n, d))

    dy_b, dg_b_post = _rms_bwd(y_b, b_post_norm, dh2, BF16, name="b_post_norm_bwd")
    dypre_b = _matmul(dy_b, w_out_b, "nt", BF16, name="b_out_dx")
    dw_out_b = _matmul(ypre_b.reshape(n, B_WIDTH), dy_b, "tn", F32, name="b_out_dw", tm=1024, tk=512)
    do_b, dz_b = _gate_bwd(dypre_b, o_b.reshape(n, B_WIDTH), z_b, 0, name="b_gate_bwd", with_delta=False)
    dq_cat, delta_rows_b = _mla_dq(r3(q_cat), r3(kvup), r3(k_rope), r3(do_b), o_b, lse_b, tabs_b3, tq)
    dq_cat = dq_cat.reshape(n, -1)
    dk_cat, dv_b = _mla_dkv(r3(q_cat), r3(kvup), r3(k_rope), r3(do_b), lse_rows_b, delta_rows_b, tq)
    dk_cat, dv_b = dk_cat.reshape(n, -1), dv_b.reshape(n, -1)
    dcq_n = _matmul(dq_cat, w_q_up_p, "nt", F32, name="b_q_up_dx")
    dw_q_up_p = _matmul(c_q, dq_cat, "tn", F32, name="b_q_up_dw", tm=1024, tk=512)
    dcq, dg_b_q = _rms_bwd(cq_raw, b_q_norm, dcq_n, BF16, name="b_q_norm_bwd")
    dhn_b = _matmul(dz_b, w_z, "nt", F32, name="b_proj_z_dx")
    dhn_b = _matmul(dcq, w_cq, "nt", F32, name="b_proj_q_dx", add=dhn_b)
    dw_z = _matmul(hn_b, dz_b, "tn", F32, name="b_proj_z_dw", tm=1024, tk=512)
    dw_cq = _matmul(hn_b, dcq, "tn", F32, name="b_proj_q_dw", tm=1024, tk=512)
    dckv_n = _matmul(dk_cat, w_up_k, "nt", F32, name="kv_up_k_dx")
    dckv_n = _matmul(dv_b, w_up_v, "nt", F32, name="kv_up_v_dx", add=dckv_n)
    dw_up_k = _matmul(c_kv, dk_cat, "tn", F32, name="kv_up_k_dw", tm=1024, tk=512)
    dw_up_v = _matmul(c_kv, dv_b, "tn", F32, name="kv_up_v_dw", tm=1024, tk=512)
    dckr, dg_lat = _kv_latent_bwd(dckv_n, ckr, g_lat, dk_cat, tabs_b)
    dhn_kv = _matmul(dckr, w_down_p, "nt", F32, name="kv_down_dx")
    dw_down_p = _matmul(hn_kv, dckr, "tn", F32, name="kv_down_dw", tm=1024, tk=512)
    dh1, dg_b_pre, dg_kvn = _rms_bwd_pair(h1, b_pre_norm, dhn_b, g_kvn, dhn_kv, dh2, name="h1_norms_bwd")

    dy_a, dg_a_post = _rms_bwd(y_a, g_a_post, dh1, BF16, name="a_post_norm_bwd")
    dypre_a = _matmul(dy_a, w_out_a, "nt", BF16, name="a_out_dx")
    dw_out_a = _matmul(ypre_a, dy_a, "tn", F32, name="a_out_dw", tm=1024, tk=512)
    do_a, dz_a, delta_a = _gate_bwd(dypre_a, om_a, z_a, 0, name="a_gate_bwd", with_delta=True)
    dw_cols = A_IN_WIDTH // N_CHIPS
    dw_tn = _tile(dw_cols, 512)
    dw_kwargs = dict(tm=1024, tn=dw_tn, tk=512, out_chunk_blocks=dw_cols // dw_tn)
    r_big = _matmul(hn_a, dz_a, "tn", F32, name="a_proj_dw_z", out_full=(N_CHIPS, d, dw_cols),
                    out_joff=z_blk_a * A_WIDTH // dw_tn, **dw_kwargs)
    dhn_a = _matmul(dz_a, w_in_a, "nt", F32, name="a_proj_dx_z", b_koff=z_blk_a)
    dhn_more = []
    for g, dil in enumerate(A_DILATIONS):
        cm = lambda a: _class_major(a, bl, t, dil)
        swap = lambda a: jnp.swapaxes(a, 2, 3)
        lse_cm, delta_cm = cm(lse_a), cm(delta_a)
        tabs_g = tuple(tb.reshape(bl, dil, t // dil, LANES) for tb in tabs_cm[g])
        dqkv = _attn_a_bwd(qkv_cm[g], 0, cm(do_a), lse_cm, delta_cm, swap(lse_cm), swap(delta_cm),
                           tabs_g, qb, name=f"attn_a_bwd_{g}").reshape(n, 3 * A_WIDTH)
        koff = 3 * g * A_WIDTH // _tile(3 * A_WIDTH, 1024)
        if dil == 1:
            dhn_a = _matmul(dqkv, w_in_a, "nt", F32, name=f"a_proj_dx_{g}", add=dhn_a, b_koff=koff)
        else:
            part = _matmul(dqkv, w_in_a, "nt", BF16, name=f"a_proj_dx_{g}", b_koff=koff)
            dhn_more.append(_natural(part.reshape(bl, dil, t // dil, d)))
        r_big = _matmul(hn_cm[g], dqkv, "tn", F32, name=f"a_proj_dw_{g}", out_into=r_big,
                        out_joff=3 * g * A_WIDTH // dw_tn, **dw_kwargs)
    grad_x, dg_a_pre = _rms_bwd(h0, g_a_pre, dhn_a, F32, name="a_pre_norm_bwd", adds=(dh1,),
                                dy_more=tuple(dhn_more))

    dw_up = jnp.concatenate([dw_up_k.reshape(B_KV_LORA, B_HEADS, LANES)[:, :, :B_NOPE],
                             dw_up_v.reshape(B_KV_LORA, B_HEADS, B_VDIM)], axis=2).reshape(B_KV_LORA, -1)
    dw_q_up = dw_q_up_p.reshape(B_Q_LORA, B_HEADS, LANES)[:, :, :B_QK_DIM].reshape(B_Q_LORA, -1)
    dw_down = jnp.concatenate([dw_down_p[:, :B_KV_LORA], dw_down_p[:, B_KV_LORA + B_NOPE:B_KV_LORA + B_NOPE + B_ROPE]], axis=1)
    dw_in_b = jnp.concatenate([dw_cq, dw_z], axis=1)
    vec_rep = [dg_kvn.reshape(-1), dg_lat.reshape(-1), dg_b_pre.reshape(-1), dg_b_q.reshape(-1),
               dg_b_post.reshape(-1), loss_part.reshape(-1)]
    vec_shapes = [(dq4,), (dq4,)] + [v.shape for v in vec_rep]
    r_outs = jnp.concatenate([dw_out_a.reshape(N_CHIPS, A_WIDTH // N_CHIPS, d),
                              dw_out_b.reshape(N_CHIPS, B_WIDTH // N_CHIPS, d)], axis=1)
    down_c = dw_down.reshape(N_CHIPS, dq4, -1)
    up_c = _to_chunks_cols(dw_up)
    inb_c = _to_chunks_cols(dw_in_b)
    qup_c = _to_chunks_cols(dw_q_up)
    small_chunks = []
    for k in range(N_CHIPS):
        vecs = [dg_a_pre.reshape(-1)[k * dq4:(k + 1) * dq4], dg_a_post.reshape(-1)[k * dq4:(k + 1) * dq4]] + vec_rep
        small_chunks.append(_pack_rows([down_c[k], up_c[k], inb_c[k], qup_c[k]] + vecs, 32))
    r_small = jnp.stack(small_chunks)

    stacked = [r_big, r_outs, r_small]
    payload = [BF16, BF16, F32]
    recv = _pair_send_other_half(stacked, name="reduce_pair_send")
    halves = [_add_my_half(s, p, core_arr, dt, name=f"reduce_pair_add_{i}")
              for i, (s, p, dt) in enumerate(zip(stacked, recv, payload))]
    parts = _chip_exchange(halves, name="reduce_chip_exchange")
    sums = [_sum_chips(p, own, chip_arr, name=f"reduce_chip_sum_{i}") for i, (p, own) in enumerate(zip(parts, halves))]
    others = _pair_swap(sums, name="reduce_pair_swap")
    g_big, g_outs_r, g_small_r = [_join_halves(m, o, core_arr, name=f"reduce_join_{i}")
                                  for i, (m, o) in enumerate(zip(sums, others))]

    grads = {}
    grads["a_w_in"] = g_big
    grads["a_w_out"] = g_outs_r[:A_WIDTH // N_CHIPS]
    grads["b_w_out"] = g_outs_r[A_WIDTH // N_CHIPS:]
    small_out_shapes = [down_c.shape[1:], up_c.shape[1:], inb_c.shape[1:], qup_c.shape[1:]] + vec_shapes
    (grads["kv_w_down"], grads["kv_w_up"], grads["b_w_in"], grads["b_w_q_up"], grads["a_pre_norm"],
     grads["a_post_norm"], grads["kv_norm"], grads["kv_latent_norm"], grads["b_pre_norm"], grads["b_q_norm"],
     grads["b_post_norm"], loss_sum) = _unpack(g_small_r.reshape(-1), small_out_shapes)

    weights = dict(a_pre_norm=a_pre_norm, a_w_in=a_w_in, a_w_out=a_w_out, a_post_norm=a_post_norm, kv_norm=kv_norm,
                   kv_w_down=kv_w_down, kv_latent_norm=kv_latent_norm, kv_w_up=kv_w_up, b_pre_norm=b_pre_norm,
                   b_w_in=b_w_in, b_q_norm=b_q_norm, b_w_q_up=b_w_q_up, b_w_out=b_w_out, b_post_norm=b_post_norm)
    names = list(weights)
    out_g, out_d, out_m, out_v = [], [], [], []
    for i, nm in enumerate(names):
        w = weights[nm]
        two_d = (1, w.shape[0]) if w.ndim == 1 else (w.shape[-2], w.shape[-1])
        gw = grads[nm].reshape(two_d)
        dlt, new_m, new_v = _adamw(w.reshape(two_d), gw, moments[i].reshape(two_d),
                                   moments[len(names) + i].reshape(two_d), name=f"adamw_{nm}")
        out_g.append(gw.reshape(w.shape))
        out_d.append(dlt.reshape(w.shape))
        out_m.append(new_m.reshape(w.shape))
        out_v.append(new_v.reshape(w.shape))
    return (loss_sum.reshape(()), grad_x.reshape(bl, t, d), *out_g, *out_d, *out_m, *out_v)


def kernel(x, positions, a_pre_norm, a_w_in, a_w_out, a_post_norm, kv_norm, kv_w_down, kv_latent_norm, kv_w_up, b_pre_norm, b_w_in, b_q_norm, b_w_q_up, b_w_out, b_post_norm, loss_target, m_a_pre_norm, m_a_w_in, m_a_w_out, m_a_post_norm, m_kv_norm, m_kv_w_down, m_kv_latent_norm, m_kv_w_up, m_b_pre_norm, m_b_w_in, m_b_q_norm, m_b_w_q_up, m_b_w_out, m_b_post_norm, v_a_pre_norm, v_a_w_in, v_a_w_out, v_a_post_norm, v_kv_norm, v_kv_w_down, v_kv_latent_norm, v_kv_w_up, v_b_pre_norm, v_b_w_in, v_b_q_norm, v_b_w_q_up, v_b_w_out, v_b_post_norm):
    moments = (m_a_pre_norm, m_a_w_in, m_a_w_out, m_a_post_norm, m_kv_norm, m_kv_w_down, m_kv_latent_norm, m_kv_w_up,
               m_b_pre_norm, m_b_w_in, m_b_q_norm, m_b_w_q_up, m_b_w_out, m_b_post_norm,
               v_a_pre_norm, v_a_w_in, v_a_w_out, v_a_post_norm, v_kv_norm, v_kv_w_down, v_kv_latent_norm, v_kv_w_up,
               v_b_pre_norm, v_b_w_in, v_b_q_norm, v_b_w_q_up, v_b_w_out, v_b_post_norm)
    return _train_step(x, positions, a_pre_norm, a_w_in, a_w_out, a_post_norm, kv_norm, kv_w_down, kv_latent_norm,
                       kv_w_up, b_pre_norm, b_w_in, b_q_norm, b_w_q_up, b_w_out, b_post_norm, loss_target, moments)
```

```python
import math

import jax
import jax.numpy as jnp
from jax import lax
from jax.experimental import pallas as pl
from jax.experimental.pallas import tpu as pltpu

F32 = jnp.float32
BF16 = jnp.bfloat16
MESH = pl.DeviceIdType.MESH

NORM_EPS = 1e-6
NEG = -1e30
LANES = 128
VMEM_LIMIT = 56 * 1024 * 1024
LOG2E = math.log2(math.e)
LN2 = math.log(2.0)

A_GROUPS = 3
A_DILATIONS = (1, 4, 16)
A_HEADS = 8
A_HEAD_DIM = 128
A_WIDTH = A_HEADS * A_HEAD_DIM
A_ROPE_THETA = 500000.0
A_IN_WIDTH = A_GROUPS * 3 * A_WIDTH + A_WIDTH
A_SCALE = A_HEAD_DIM ** -0.5

B_HEADS = 16
B_NOPE = 64
B_ROPE = 32
B_QK_DIM = B_NOPE + B_ROPE
B_VDIM = 64
B_WIDTH = B_HEADS * B_VDIM
B_Q_LORA = 384
B_KV_LORA = 256
B_ROPE_THETA = 10000.0
B_SCALE = B_QK_DIM ** -0.5

ADAM_LR = 0.001
ADAM_B1 = 0.9
ADAM_B2 = 0.999
ADAM_EPS = 1e-08
ADAM_WD = 0.01
ADAM_STEP = 10

N_CHIPS = 4
PACK_COLS = 512


def _params(sem=None):
    return pltpu.CompilerParams(dimension_semantics=sem, vmem_limit_bytes=VMEM_LIMIT)


def _tile(n, want):
    t = min(n, want)
    assert n % t == 0, (n, want)
    return t


def _row_tile(n, want):
    for t in range(min(n, want), 0, -1):
        if n % t == 0 and (t % 16 == 0 or t == n):
            return t
    return n


def _rope_tables(positions, theta, lane0):
    half = 16
    inv_freq = 1.0 / (theta ** (jnp.arange(half, dtype=F32) * (2.0 / (2 * half))))
    n = positions.size
    per_row = LANES // half
    pos = jnp.repeat(positions.astype(F32).reshape(n // per_row, per_row), half, axis=1)
    ang = pos * jnp.tile(inv_freq, per_row)
    cos, sin = lax.optimization_barrier((jnp.cos(ang), jnp.sin(ang)))
    cos, sin = cos.reshape(n, half), sin.reshape(n, half)
    pre = jnp.zeros((n, lane0), F32)
    post = jnp.zeros((n, LANES - lane0 - 2 * half), F32)
    z16 = jnp.zeros((n, half), F32)
    c = jnp.concatenate([pre + 1.0, cos, cos, post + 1.0], axis=1)
    sa = jnp.concatenate([pre, -sin, z16, post], axis=1)
    sb = jnp.concatenate([pre, z16, sin, post], axis=1)
    return lax.optimization_barrier((c, sa, sb))


def _rope_apply(x, c, sa, sb, sign):
    k = x.shape[1] // LANES
    if k > 1:
        c, sa, sb = (jnp.concatenate([t] * k, axis=1) for t in (c, sa, sb))
    w = x.shape[1]
    up = pltpu.roll(x, w - 16, 1)
    dn = pltpu.roll(x, 16, 1)
    if sign > 0:
        return x * c + up * sa + dn * sb
    return x * c - up * sa - dn * sb


def _matmul(a, b, mode, out_dtype, *, name, tm=512, tn=1024, tk=1024, add=None, rope=None,
            out_scale=None, b_koff=0, b_cols=None, out_into=None, out_full=None, out_joff=0,
            out_chunk_blocks=None, after=None):
    if mode == "nn":
        m, k = a.shape
        n = b.shape[1]
    elif mode == "nt":
        m, k = a.shape
        n = b.shape[0]
    else:
        k, m = a.shape
        n = b.shape[1]
    b_j0 = 0
    if b_cols is not None:
        tn = _tile(n, tn)
        b_j0, n = b_cols[0], b_cols[1] * tn
    tm, tn, tk = _tile(m, tm), _tile(n, tn), _tile(k, tk)
    nk = k // tk
    if mode == "nn":
        a_spec = pl.BlockSpec((tm, tk), lambda j, i, kk: (i, kk))
        b_spec = pl.BlockSpec((tk, tn), lambda j, i, kk: (kk, j + b_j0))
        dims = (((1,), (0,)), ((), ()))
    elif mode == "nt":
        a_spec = pl.BlockSpec((tm, tk), lambda j, i, kk: (i, kk))
        b_spec = pl.BlockSpec((tn, tk), lambda j, i, kk: (j, kk + b_koff))
        dims = (((1,), (1,)), ((), ()))
    else:
        a_spec = pl.BlockSpec((tk, tm), lambda j, i, kk: (kk, i))
        b_spec = pl.BlockSpec((tk, tn), lambda j, i, kk: (kk, j))
        dims = (((0,), (0,)), ((), ()))
    operands = [a, b]
    in_specs = [a_spec, b_spec]
    if add is not None:
        operands.append(add)
        in_specs.append(pl.BlockSpec((tm, tn), lambda j, i, kk: (i, j)))
    if rope is not None:
        tables, rope_pred = rope
        for t in tables:
            operands.append(t)
            in_specs.append(pl.BlockSpec((tm, LANES), lambda j, i, kk: (i, 0)))
    aliases = {}
    if out_into is not None:
        aliases = {len(operands): 0}
        operands.append(out_into)
        in_specs.append(pl.BlockSpec(memory_space=pl.ANY))
        out_shape = jax.ShapeDtypeStruct(out_into.shape, out_into.dtype)
    elif out_full is not None:
        out_shape = jax.ShapeDtypeStruct(out_full, out_dtype)
    else:
        out_shape = jax.ShapeDtypeStruct((m, n), out_dtype)
    if after is not None:
        operands.append(after)
        in_specs.append(pl.BlockSpec(memory_space=pl.ANY))
    if out_chunk_blocks is not None:
        out_spec = pl.BlockSpec((None, tm, tn), lambda j, i, kk: ((j + out_joff) // out_chunk_blocks, i,
                                                                  (j + out_joff) % out_chunk_blocks))
    else:
        out_spec = pl.BlockSpec((tm, tn), lambda j, i, kk: (i, j + out_joff))

    def body(*refs):
        a_ref, b_ref = refs[0], refs[1]
        pos = 2
        add_ref = None
        if add is not None:
            add_ref = refs[pos]
            pos += 1
        tab_refs = None
        if rope is not None:
            tab_refs = refs[pos:pos + 3]
            pos += 3
        if out_into is not None:
            pos += 1
        if after is not None:
            pos += 1
        o_ref = refs[pos]
        acc_ref = refs[pos + 1] if nk > 1 else None

        def finish(res):
            if add_ref is not None:
                res = res + add_ref[...].astype(F32)
            if tab_refs is None:
                o_ref[...] = res.astype(o_ref.dtype)
                return
            j = pl.program_id(0)
            flag = rope_pred(j)
            roped = _rope_apply(res, tab_refs[0][...], tab_refs[1][...], tab_refs[2][...], 1)
            if out_scale is not None:
                value, scale_pred = out_scale
                use = scale_pred(j)
                roped = roped * (value if use is True else jnp.where(use, value, 1.0))
            if flag is True:
                o_ref[...] = roped.astype(o_ref.dtype)
                return

            @pl.when(flag)
            def _():
                o_ref[...] = roped.astype(o_ref.dtype)

            @pl.when(jnp.logical_not(flag))
            def _():
                o_ref[...] = res.astype(o_ref.dtype)

        part = lax.dot_general(a_ref[...].astype(BF16), b_ref[...].astype(BF16), dims,
                               preferred_element_type=F32)
        if nk == 1:
            finish(part)
            return
        kk = pl.program_id(2)

        @pl.when(kk == 0)
        def _():
            acc_ref[...] = part

        @pl.when(kk > 0)
        def _():
            acc_ref[...] += part

        @pl.when(kk == nk - 1)
        def _():
            finish(acc_ref[...])

    return pl.pallas_call(
        body, name=name, grid=(n // tn, m // tm, nk), in_specs=in_specs, out_specs=out_spec,
        out_shape=out_shape, input_output_aliases=aliases,
        scratch_shapes=[pltpu.VMEM((tm, tn), F32)] if nk > 1 else [],
        compiler_params=_params(("parallel", "parallel", "arbitrary")),
    )(*operands)


def _rms_fwd(x, g, out_dtype, *, name, add=None, tr=512):
    n, d = x.shape
    tr = _tile(n, tr)
    row = pl.BlockSpec((tr, d), lambda i: (i, 0))
    vec = pl.BlockSpec((1, d), lambda i: (0, 0))

    def body(*refs):
        x_ref, g_ref = refs[0], refs[1]
        o_ref = refs[-1]
        xv = x_ref[...].astype(F32)
        r = lax.rsqrt(jnp.mean(xv * xv, axis=-1, keepdims=True) + NORM_EPS)
        y = xv * r * g_ref[...]
        if add is not None:
            y = refs[2][...] + y
        o_ref[...] = y.astype(o_ref.dtype)

    ops = [x, g] + ([add] if add is not None else [])
    specs = [row, vec] + ([row] if add is not None else [])
    return pl.pallas_call(
        body, name=name, grid=(n // tr,), in_specs=specs, out_specs=row,
        out_shape=jax.ShapeDtypeStruct((n, d), out_dtype), compiler_params=_params(("parallel",)),
    )(*ops)


def _rms_bwd(x, g, dy, out_dtype, *, name, adds=(), dy_more=(), tr=512):
    n, d = x.shape
    tr = _tile(n, tr)
    steps = n // tr
    row = pl.BlockSpec((tr, d), lambda i: (i, 0))
    vec = pl.BlockSpec((1, d), lambda i: (0, 0))
    na = len(adds) + len(dy_more)

    def body(*refs):
        x_ref, g_ref, dy_ref = refs[:3]
        add_refs = refs[3:3 + len(adds)]
        more_refs = refs[3 + len(adds):3 + na]
        dx_ref, dg_ref, acc_ref = refs[3 + na:]
        i = pl.program_id(0)
        xv = x_ref[...].astype(F32)
        r = lax.rsqrt(jnp.mean(xv * xv, axis=-1, keepdims=True) + NORM_EPS)
        xh = xv * r
        dyv = dy_ref[...].astype(F32)
        for m_ref in more_refs:
            dyv = dyv + m_ref[...].astype(F32)
        part = (dyv * xh).reshape(tr // 8, 8, d).sum(axis=0)

        @pl.when(i == 0)
        def _():
            acc_ref[...] = part

        @pl.when(i > 0)
        def _():
            acc_ref[...] += part

        t = dyv * g_ref[...]
        dx = r * (t - xh * jnp.mean(t * xh, axis=-1, keepdims=True))
        for a_ref in add_refs:
            dx = dx + a_ref[...].astype(F32)
        dx_ref[...] = dx.astype(dx_ref.dtype)

        @pl.when(i == steps - 1)
        def _():
            dg_ref[...] = jnp.sum(acc_ref[...], axis=0, keepdims=True)

    return pl.pallas_call(
        body, name=name, grid=(steps,), in_specs=[row, vec, row] + [row] * na,
        out_specs=(row, vec),
        out_shape=(jax.ShapeDtypeStruct((n, d), out_dtype), jax.ShapeDtypeStruct((1, d), F32)),
        scratch_shapes=[pltpu.VMEM((8, d), F32)], compiler_params=_params(("arbitrary",)),
    )(x, g, dy, *adds, *dy_more)


def _rms(xv, g):
    return xv * lax.rsqrt(jnp.mean(xv * xv, axis=-1, keepdims=True) + NORM_EPS) * g


def _post_norm_block(y, g, h_in, next_gains, *, name, tr=512):
    n, d = y.shape
    tr = _tile(n, tr)
    nk = len(next_gains)
    row = pl.BlockSpec((tr, d), lambda i: (i, 0))
    vec = pl.BlockSpec((1, d), lambda i: (0, 0))

    def body(*refs):
        y_ref, g_ref, h_ref = refs[:3]
        gk_refs = refs[3:3 + nk]
        o_ref = refs[3 + nk]
        hn_refs = refs[4 + nk:]
        h = h_ref[...] + _rms(y_ref[...], g_ref[...])
        o_ref[...] = h
        for gk_ref, hn_ref in zip(gk_refs, hn_refs):
            hn_ref[...] = _rms(h, gk_ref[...]).astype(BF16)

    return pl.pallas_call(
        body, name=name, grid=(n // tr,), in_specs=[row, vec, row] + [vec] * nk,
        out_specs=(row,) * (1 + nk),
        out_shape=(jax.ShapeDtypeStruct((n, d), F32),) + (jax.ShapeDtypeStruct((n, d), BF16),) * nk,
        compiler_params=_params(("parallel",)),
    )(y, g, h_in, *next_gains)


def _post_norm_loss(y, g, h_in, target, *, tr=512):
    n, d = y.shape
    tr = _tile(n, tr)
    steps = n // tr
    row = pl.BlockSpec((tr, d), lambda i: (i, 0))

    def body(y_ref, g_ref, h_ref, t_ref, dh_ref, loss_ref, acc_ref):
        i = pl.program_id(0)
        e = h_ref[...] + _rms(y_ref[...], g_ref[...]) - t_ref[...]
        dh_ref[...] = e / d
        part = (e * e).reshape(tr // 8, 8, d).sum(axis=0)

        @pl.when(i == 0)
        def _():
            acc_ref[...] = part

        @pl.when(i > 0)
        def _():
            acc_ref[...] += part

        @pl.when(i == steps - 1)
        def _():
            s = jnp.sum(jnp.sum(acc_ref[...], axis=-1, keepdims=True), axis=0, keepdims=True)
            loss_ref[...] = 0.5 * s / d

    return pl.pallas_call(
        body, name="b_post_norm_loss", grid=(steps,),
        in_specs=[row, pl.BlockSpec((1, d), lambda i: (0, 0)), row, row],
        out_specs=(row, pl.BlockSpec((1, 1), lambda i: (0, 0))),
        out_shape=(jax.ShapeDtypeStruct((n, d), F32), jax.ShapeDtypeStruct((1, 1), F32)),
        scratch_shapes=[pltpu.VMEM((8, d), F32)], compiler_params=_params(("arbitrary",)),
    )(y, g, h_in, target)


def _rms_bwd_pair(x, g1, dy1, g2, dy2, add, *, name, tr=512):
    n, d = x.shape
    tr = _tile(n, tr)
    steps = n // tr
    row = pl.BlockSpec((tr, d), lambda i: (i, 0))
    vec = pl.BlockSpec((1, d), lambda i: (0, 0))

    def body(x_ref, g1_ref, d1_ref, g2_ref, d2_ref, add_ref, dx_ref, dg1_ref, dg2_ref, acc_ref):
        i = pl.program_id(0)
        xv = x_ref[...]
        r = lax.rsqrt(jnp.mean(xv * xv, axis=-1, keepdims=True) + NORM_EPS)
        xh = xv * r
        dx = add_ref[...]
        for k, (g_ref, d_ref) in enumerate(((g1_ref, d1_ref), (g2_ref, d2_ref))):
            dyv = d_ref[...].astype(F32)
            part = (dyv * xh).reshape(tr // 8, 8, d).sum(axis=0)

            @pl.when(i == 0)
            def _(part=part, k=k):
                acc_ref[k] = part

            @pl.when(i > 0)
            def _(part=part, k=k):
                acc_ref[k] += part

            t = dyv * g_ref[...]
            dx = dx + r * (t - xh * jnp.mean(t * xh, axis=-1, keepdims=True))
        dx_ref[...] = dx

        @pl.when(i == steps - 1)
        def _():
            dg1_ref[...] = jnp.sum(acc_ref[0], axis=0, keepdims=True)
            dg2_ref[...] = jnp.sum(acc_ref[1], axis=0, keepdims=True)

    return pl.pallas_call(
        body, name=name, grid=(steps,), in_specs=[row, vec, row, vec, row, row],
        out_specs=(row, vec, vec),
        out_shape=(jax.ShapeDtypeStruct((n, d), F32), jax.ShapeDtypeStruct((1, d), F32),
                   jax.ShapeDtypeStruct((1, d), F32)),
        scratch_shapes=[pltpu.VMEM((2, 8, d), F32)], compiler_params=_params(("arbitrary",)),
    )(x, g1, dy1, g2, dy2, add)


def _kv_latent_fwd(ckr, g_lat, tabs, *, tr=512):
    n = ckr.shape[0]
    tr = _tile(n, tr)
    lat = B_KV_LORA

    def body(c_ref, k_ref, g_ref, tc, tsa, tsb, ckv_ref, kr_ref):
        xv = c_ref[...]
        r = lax.rsqrt(jnp.mean(xv * xv, axis=-1, keepdims=True) + NORM_EPS)
        ckv_ref[...] = (xv * r * g_ref[...]).astype(BF16)
        kr_ref[...] = _rope_apply(k_ref[...], tc[...], tsa[...], tsb[...], 1).astype(BF16)

    tab = pl.BlockSpec((tr, LANES), lambda i: (i, 0))
    return pl.pallas_call(
        body, name="kv_latent_fwd", grid=(n // tr,),
        in_specs=[pl.BlockSpec((tr, lat), lambda i: (i, 0)),
                  pl.BlockSpec((tr, LANES), lambda i: (i, lat // LANES)),
                  pl.BlockSpec((1, lat), lambda i: (0, 0)), tab, tab, tab],
        out_specs=(pl.BlockSpec((tr, lat), lambda i: (i, 0)), tab),
        out_shape=(jax.ShapeDtypeStruct((n, lat), BF16), jax.ShapeDtypeStruct((n, LANES), BF16)),
        compiler_params=_params(("parallel",)),
    )(ckr, ckr, g_lat, *tabs)


def _kv_latent_bwd(dckv, ckr, g_lat, dk_cat, tabs, *, tr=512):
    n = ckr.shape[0]
    tr = _tile(n, tr)
    steps = n // tr
    lat = B_KV_LORA
    wk = dk_cat.shape[1]

    def body(d_ref, c_ref, g_ref, dk_ref, tc, tsa, tsb, o_ref, dg_ref, acc_ref):
        i = pl.program_id(0)
        xv = c_ref[...]
        r = lax.rsqrt(jnp.mean(xv * xv, axis=-1, keepdims=True) + NORM_EPS)
        xh = xv * r
        dyv = d_ref[...]
        part = (dyv * xh).reshape(tr // 8, 8, lat).sum(axis=0)

        @pl.when(i == 0)
        def _():
            acc_ref[...] = part

        @pl.when(i > 0)
        def _():
            acc_ref[...] += part

        t = dyv * g_ref[...]
        dx = r * (t - xh * jnp.mean(t * xh, axis=-1, keepdims=True))
        o_ref[:, 0:lat] = dx.astype(o_ref.dtype)
        dkr = dk_ref[:, 0:LANES].astype(F32)
        for h in range(1, wk // LANES):
            dkr = dkr + dk_ref[:, h * LANES:(h + 1) * LANES].astype(F32)
        o_ref[:, lat:lat + LANES] = _rope_apply(dkr, tc[...], tsa[...], tsb[...], -1).astype(o_ref.dtype)

        @pl.when(i == steps - 1)
        def _():
            dg_ref[...] = jnp.sum(acc_ref[...], axis=0, keepdims=True)

    tab = pl.BlockSpec((tr, LANES), lambda i: (i, 0))
    return pl.pallas_call(
        body, name="kv_latent_bwd", grid=(steps,),
        in_specs=[pl.BlockSpec((tr, lat), lambda i: (i, 0)), pl.BlockSpec((tr, lat), lambda i: (i, 0)),
                  pl.BlockSpec((1, lat), lambda i: (0, 0)), pl.BlockSpec((tr, wk), lambda i: (i, 0)),
                  tab, tab, tab],
        out_specs=(pl.BlockSpec((tr, lat + LANES), lambda i: (i, 0)), pl.BlockSpec((1, lat), lambda i: (0, 0))),
        out_shape=(jax.ShapeDtypeStruct((n, lat + LANES), BF16), jax.ShapeDtypeStruct((1, lat), F32)),
        scratch_shapes=[pltpu.VMEM((8, lat), F32)], compiler_params=_params(("arbitrary",)),
    )(dckv, ckr, g_lat, dk_cat, *tabs)


def _sigmoid(z):
    return 1.0 / (1.0 + jnp.exp(-z))


def _lane_place(cols, width):
    rows = cols[0].shape[0]
    lane = lax.broadcasted_iota(jnp.int32, (rows, width), 1)
    out = jnp.zeros((rows, width), F32)
    for h, col in enumerate(cols):
        out = jnp.where(lane == h, col, out)
    return out


def _merge_gate_fwd(outs, lses, proj, z_block, *, tr=256):
    n, w = outs[0].shape
    tr = _tile(n, tr)
    ng = len(outs)

    def body(*refs):
        o_refs = refs[:ng]
        l_refs = refs[ng:2 * ng]
        z_ref = refs[2 * ng]
        y_ref, om_ref, lse_ref = refs[2 * ng + 1:]
        ls = [r[...] for r in l_refs]
        mx = ls[0]
        for l in ls[1:]:
            mx = jnp.maximum(mx, l)
        ssum = jnp.exp2(ls[0] - mx)
        for l in ls[1:]:
            ssum = ssum + jnp.exp2(l - mx)
        tot = mx + jnp.log2(ssum)
        lse_ref[...] = tot
        ws = [jnp.exp2(l - tot) for l in ls]
        for h in range(A_HEADS):
            sl = slice(h * A_HEAD_DIM, (h + 1) * A_HEAD_DIM)
            o = ws[0][:, h:h + 1] * o_refs[0][:, sl]
            for gi in range(1, ng):
                o = o + ws[gi][:, h:h + 1] * o_refs[gi][:, sl]
            z = z_ref[:, sl].astype(F32)
            om_ref[:, sl] = o.astype(BF16)
            y_ref[:, sl] = (o * (z * _sigmoid(z))).astype(BF16)

    row = pl.BlockSpec((tr, w), lambda i: (i, 0))
    lrow = pl.BlockSpec((tr, A_HEADS), lambda i: (i, 0))
    return pl.pallas_call(
        body, name="merge_gate_fwd", grid=(n // tr,),
        in_specs=[row] * ng + [lrow] * ng + [pl.BlockSpec((tr, w), lambda i: (i, z_block))],
        out_specs=(row, row, lrow),
        out_shape=(jax.ShapeDtypeStruct((n, w), BF16), jax.ShapeDtypeStruct((n, w), BF16),
                   jax.ShapeDtypeStruct((n, A_HEADS), F32)),
        compiler_params=_params(("parallel",)),
    )(*outs, *lses, proj)


def _gate_bwd(dy, o, z_arr, z_block, *, name, with_delta, tr=256):
    n, w = dy.shape
    tr = _tile(n, tr)

    def body(*refs):
        dy_ref, o_ref, z_ref, do_ref, dz_ref = refs[:5]
        dyv = dy_ref[...].astype(F32)
        ov = o_ref[...].astype(F32)
        z = z_ref[...].astype(F32)
        sig = _sigmoid(z)
        do = dyv * (z * sig)
        do_ref[...] = do.astype(BF16)
        dz_ref[...] = (dyv * ov * (sig * (1.0 + z * (1.0 - sig)))).astype(BF16)
        if with_delta:
            prod = do * ov
            cols = [jnp.sum(prod[:, h * A_HEAD_DIM:(h + 1) * A_HEAD_DIM], axis=-1, keepdims=True)
                    for h in range(A_HEADS)]
            refs[5][...] = _lane_place(cols, A_HEADS)

    row = pl.BlockSpec((tr, w), lambda i: (i, 0))
    out_specs = [row, row]
    out_shape = [jax.ShapeDtypeStruct((n, w), BF16), jax.ShapeDtypeStruct((n, w), BF16)]
    if with_delta:
        out_specs.append(pl.BlockSpec((tr, A_HEADS), lambda i: (i, 0)))
        out_shape.append(jax.ShapeDtypeStruct((n, A_HEADS), F32))
    return pl.pallas_call(
        body, name=name, grid=(n // tr,),
        in_specs=[row, row, pl.BlockSpec((tr, w), lambda i: (i, z_block))],
        out_specs=tuple(out_specs), out_shape=tuple(out_shape), compiler_params=_params(("parallel",)),
    )(dy, o, z_arr)


def _loss_fwd_bwd(h, target, *, tr=512):
    n, d = h.shape
    tr = _tile(n, tr)
    steps = n // tr

    def body(h_ref, t_ref, dh_ref, loss_ref, acc_ref):
        i = pl.program_id(0)
        e = h_ref[...] - t_ref[...]
        dh_ref[...] = e / d
        part = (e * e).reshape(tr // 8, 8, d).sum(axis=0)

        @pl.when(i == 0)
        def _():
            acc_ref[...] = part

        @pl.when(i > 0)
        def _():
            acc_ref[...] += part

        @pl.when(i == steps - 1)
        def _():
            s = jnp.sum(jnp.sum(acc_ref[...], axis=-1, keepdims=True), axis=0, keepdims=True)
            loss_ref[...] = 0.5 * s / d

    row = pl.BlockSpec((tr, d), lambda i: (i, 0))
    return pl.pallas_call(
        body, name="loss", grid=(steps,), in_specs=[row, row],
        out_specs=(row, pl.BlockSpec((1, 1), lambda i: (0, 0))),
        out_shape=(jax.ShapeDtypeStruct((n, d), F32), jax.ShapeDtypeStruct((1, 1), F32)),
        scratch_shapes=[pltpu.VMEM((8, d), F32)], compiler_params=_params(("arbitrary",)),
    )(h, target)


def _dot_nt(a, b):
    return lax.dot_general(a, b, (((1,), (1,)), ((), ())), preferred_element_type=F32)


def _dot_nn(a, b):
    return lax.dot_general(a, b, (((1,), (0,)), ((), ())), preferred_element_type=F32)


def _attn_a_fwd(qkv, cb0, qb, out_dtype, *, name):
    bl, dil, ln, _ = qkv.shape
    nb = ln // qb
    hw = A_WIDTH
    heads = range(A_HEADS)
    sls = [slice(h * A_HEAD_DIM, (h + 1) * A_HEAD_DIM) for h in heads]

    def body(*refs):
        if nb > 1:
            q_ref, kc_ref, vc_ref, kp_ref, vp_ref, o_ref, lse_ref = refs
        else:
            q_ref, kc_ref, vc_ref, o_ref, lse_ref = refs
        i = pl.program_id(2)
        qi = lax.broadcasted_iota(jnp.int32, (qb, qb), 0)
        ki = lax.broadcasted_iota(jnp.int32, (qb, qb), 1)
        mask_c = ki <= qi
        mask_p = jnp.logical_and(ki >= qi, i >= 1)
        s_c = [jnp.where(mask_c, _dot_nt(q_ref[:, sls[h]], kc_ref[:, sls[h]]), NEG) for h in heads]
        m = [jnp.max(s_c[h], axis=-1, keepdims=True) for h in heads]
        if nb > 1:
            s_p = [jnp.where(mask_p, _dot_nt(q_ref[:, sls[h]], kp_ref[:, sls[h]]), NEG) for h in heads]
            m = [jnp.maximum(m[h], jnp.max(s_p[h], axis=-1, keepdims=True)) for h in heads]
        p_c = [jnp.exp2(s_c[h] - m[h]) for h in heads]
        l = [jnp.sum(p_c[h], axis=-1, keepdims=True) for h in heads]
        acc = [_dot_nn(p_c[h].astype(BF16), vc_ref[:, sls[h]]) for h in heads]
        if nb > 1:
            p_p = [jnp.exp2(s_p[h] - m[h]) for h in heads]
            l = [l[h] + jnp.sum(p_p[h], axis=-1, keepdims=True) for h in heads]
            acc = [acc[h] + _dot_nn(p_p[h].astype(BF16), vp_ref[:, sls[h]]) for h in heads]
        for h in heads:
            o_ref[:, sls[h]] = (acc[h] / l[h]).astype(o_ref.dtype)
        lse_ref[...] = _lane_place([m[h] + jnp.log2(l[h]) for h in heads], A_HEADS)

    def spec(off, prev):
        if prev:
            return pl.BlockSpec((None, None, qb, hw), lambda b, r, i: (b, r, jnp.maximum(i - 1, 0), cb0 + off))
        return pl.BlockSpec((None, None, qb, hw), lambda b, r, i: (b, r, i, cb0 + off))

    return pl.pallas_call(
        body, name=name, grid=(bl, dil, nb),
        in_specs=[spec(0, False), spec(1, False), spec(2, False)] + ([spec(1, True), spec(2, True)] if nb > 1 else []),
        out_specs=(pl.BlockSpec((None, None, qb, hw), lambda b, r, i: (b, r, i, 0)),
                   pl.BlockSpec((None, None, qb, A_HEADS), lambda b, r, i: (b, r, i, 0))),
        out_shape=(jax.ShapeDtypeStruct((bl, dil, ln, hw), out_dtype),
                   jax.ShapeDtypeStruct((bl, dil, ln, A_HEADS), F32)),
        compiler_params=_params(("parallel", "parallel", "arbitrary")),
    )(*([qkv] * (5 if nb > 1 else 3)))


def _attn_a_bwd(qkv, cb0, do, lse, delta, lse_t, delta_t, tabs, qb, *, name):
    bl, dil, ln, _ = qkv.shape
    nb = ln // qb
    hw = A_WIDTH

    def body(*refs):
        if nb > 1:
            (q_ref, kc_ref, vc_ref, do_ref, lse_ref, dl_ref, lt_ref, dt_ref, tc, tsa, tsb,
             qn_ref, kp_ref, vp_ref, don_ref, ltn_ref, dtn_ref, o_ref) = refs
        else:
            q_ref, kc_ref, vc_ref, do_ref, lse_ref, dl_ref, lt_ref, dt_ref, tc, tsa, tsb, o_ref = refs
        i = pl.program_id(2)
        row = lax.broadcasted_iota(jnp.int32, (qb, qb), 0)
        col = lax.broadcasted_iota(jnp.int32, (qb, qb), 1)
        m_qc = col <= row
        m_kc = row <= col
        m_qp = jnp.logical_and(col >= row, i >= 1)
        m_kn = jnp.logical_and(row >= col, i + 1 < nb)
        c, sa, sb = tc[...], tsa[...], tsb[...]
        heads = range(A_HEADS)
        sls = [slice(h * A_HEAD_DIM, (h + 1) * A_HEAD_DIM) for h in heads]
        q, kc = [q_ref[:, sl] for sl in sls], [kc_ref[:, sl] for sl in sls]
        vc, dov = [vc_ref[:, sl] for sl in sls], [do_ref[:, sl] for sl in sls]
        lse_c = [lse_ref[:, h:h + 1] for h in heads]
        dl_c = [dl_ref[:, h:h + 1] for h in heads]
        s = [_dot_nt(q[h], kc[h]) for h in heads]
        st = [_dot_nt(kc[h], q[h]) for h in heads]
        dp = [_dot_nt(dov[h], vc[h]) for h in heads]
        dpt = [_dot_nt(vc[h], dov[h]) for h in heads]
        p = [jnp.exp2(jnp.where(m_qc, s[h], NEG) - lse_c[h]) for h in heads]
        pt = [jnp.exp2(jnp.where(m_kc, st[h], NEG) - lt_ref[h:h + 1, :]) for h in heads]
        dq = [_dot_nn((p[h] * (dp[h] - dl_c[h])).astype(BF16), kc[h]) for h in heads]
        dk = [_dot_nn((pt[h] * (dpt[h] - dt_ref[h:h + 1, :])).astype(BF16), q[h]) for h in heads]
        dv = [_dot_nn(pt[h].astype(BF16), dov[h]) for h in heads]
        if nb > 1:
            kp, vp = [kp_ref[:, sl] for sl in sls], [vp_ref[:, sl] for sl in sls]
            qn, don = [qn_ref[:, sl] for sl in sls], [don_ref[:, sl] for sl in sls]
            s = [_dot_nt(q[h], kp[h]) for h in heads]
            st = [_dot_nt(kc[h], qn[h]) for h in heads]
            dp = [_dot_nt(dov[h], vp[h]) for h in heads]
            dpt = [_dot_nt(vc[h], don[h]) for h in heads]
            p = [jnp.exp2(jnp.where(m_qp, s[h], NEG) - lse_c[h]) for h in heads]
            pt = [jnp.exp2(jnp.where(m_kn, st[h], NEG) - ltn_ref[h:h + 1, :]) for h in heads]
            dq = [dq[h] + _dot_nn((p[h] * (dp[h] - dl_c[h])).astype(BF16), kp[h]) for h in heads]
            dk = [dk[h] + _dot_nn((pt[h] * (dpt[h] - dtn_ref[h:h + 1, :])).astype(BF16), qn[h]) for h in heads]
            dv = [dv[h] + _dot_nn(pt[h].astype(BF16), don[h]) for h in heads]
        for h in heads:
            o_ref[:, h * A_HEAD_DIM:(h + 1) * A_HEAD_DIM] = _rope_apply(dq[h] * A_SCALE, c, sa, sb, -1).astype(BF16)
            o_ref[:, hw + h * A_HEAD_DIM:hw + (h + 1) * A_HEAD_DIM] = _rope_apply(dk[h] * LN2, c, sa, sb, -1).astype(BF16)
            o_ref[:, 2 * hw + h * A_HEAD_DIM:2 * hw + (h + 1) * A_HEAD_DIM] = dv[h].astype(BF16)

    def cur(w, col):
        return pl.BlockSpec((None, None, qb, w), lambda b, r, i: (b, r, i, col))

    def prev(w, col):
        return pl.BlockSpec((None, None, qb, w), lambda b, r, i: (b, r, jnp.maximum(i - 1, 0), col))

    def nxt(w, col):
        return pl.BlockSpec((None, None, qb, w), lambda b, r, i: (b, r, jnp.minimum(i + 1, nb - 1), col))

    t_cur = pl.BlockSpec((None, None, A_HEADS, qb), lambda b, r, i: (b, r, 0, i))
    t_nxt = pl.BlockSpec((None, None, A_HEADS, qb), lambda b, r, i: (b, r, 0, jnp.minimum(i + 1, nb - 1)))
    in_specs = [cur(hw, cb0), cur(hw, cb0 + 1), cur(hw, cb0 + 2), cur(hw, 0), cur(A_HEADS, 0), cur(A_HEADS, 0),
                t_cur, t_cur, cur(LANES, 0), cur(LANES, 0), cur(LANES, 0)]
    operands = [qkv, qkv, qkv, do, lse, delta, lse_t, delta_t, *tabs]
    if nb > 1:
        in_specs += [nxt(hw, cb0), prev(hw, cb0 + 1), prev(hw, cb0 + 2), nxt(hw, 0), t_nxt, t_nxt]
        operands += [qkv, qkv, qkv, do, lse_t, delta_t]
    return pl.pallas_call(
        body, name=name, grid=(bl, dil, nb), in_specs=in_specs, out_specs=cur(3 * hw, 0),
        out_shape=jax.ShapeDtypeStruct((bl, dil, ln, 3 * hw), BF16),
        compiler_params=_params(("parallel", "parallel", "arbitrary")),
    )(*operands)


def _head_terms(do, o, lse, e):
    rows = do.shape[0]
    lane = lax.broadcasted_iota(jnp.int32, (rows, LANES), 1)
    mine = (lane < B_VDIM) if e == 0 else (lane >= B_VDIM)
    prod = do.astype(F32) * o.astype(F32)
    dl = jnp.sum(jnp.where(mine, prod, 0.0), axis=-1, keepdims=True)
    do_e = jnp.where(mine, do, jnp.zeros_like(do))
    return do_e, dl, lse[:, e * B_VDIM:e * B_VDIM + 1]


def _col_to_row(col, rows):
    return jnp.transpose(jnp.broadcast_to(col, (rows, LANES)))[0:1, :]


def _mla_fwd(q_cat, kvup, kr, z, tq):
    bl, t, _ = q_cat.shape
    nq = t // tq
    pairs = B_HEADS // 2
    v_blk0 = (B_HEADS * LANES) // LANES

    def body(q_ref, k_ref, v_ref, kr_ref, z_ref, y_ref, o_ref, lse_ref, lrow_ref, m_ref, acc_ref):
        qi = pl.program_id(2)
        qs = [q_ref[:, e * LANES:(e + 1) * LANES] for e in range(2)]
        row = lax.broadcasted_iota(jnp.int32, (tq, tq), 0)
        col = lax.broadcasted_iota(jnp.int32, (tq, tq), 1)
        tri = col <= row
        sum_lane = [B_VDIM, 0]

        for e in range(2):
            m_ref[e] = jnp.full((tq, LANES), NEG, F32)
            acc_ref[e] = jnp.zeros((tq, LANES), F32)

        def tile(k0, w, masked):
            lane = lax.broadcasted_iota(jnp.int32, (w, LANES), 1)
            first = lane < B_VDIM
            krv = kr_ref[pl.ds(k0, w), :]
            v = v_ref[pl.ds(k0, w), :]
            vs = [jnp.where(first, v, jnp.where(lane == B_VDIM, 1.0, 0.0).astype(BF16)),
                  jnp.where(first, jnp.where(lane == 0, 1.0, 0.0).astype(BF16), v)]
            ss = []
            for e in range(2):
                k = k_ref[pl.ds(k0, w), e * LANES:(e + 1) * LANES] + krv
                s = _dot_nt(qs[e], k)
                ss.append(jnp.where(tri, s, NEG) if masked else s)
            m_old = [m_ref[e] for e in range(2)]
            ms = [jnp.maximum(m_old[e], jnp.max(ss[e], axis=-1, keepdims=True)) for e in range(2)]
            ps = [jnp.exp2(ss[e] - jnp.concatenate([ms[e]] * (w // LANES), axis=1)).astype(BF16) for e in range(2)]
            for e in range(2):
                m_ref[e] = ms[e]
                acc_ref[e] = jnp.exp2(m_old[e] - ms[e]) * acc_ref[e] + _dot_nn(ps[e], vs[e])

        def step(kb2, carry):
            tile(pl.multiple_of(kb2 * 2 * tq, 2 * tq), 2 * tq, False)
            return carry

        lax.fori_loop(0, qi // 2, step, 0)

        @pl.when(qi % 2 == 1)
        def _():
            tile(pl.multiple_of((qi - 1) * tq, tq), tq, False)

        tile(pl.multiple_of(qi * tq, tq), tq, True)
        lane = lax.broadcasted_iota(jnp.int32, (tq, LANES), 1)
        first = lane < B_VDIM
        accs = [acc_ref[e] for e in range(2)]
        ls = [accs[e][:, sum_lane[e]:sum_lane[e] + 1] for e in range(2)]
        outs = [accs[e] / ls[e] for e in range(2)]
        lses = [m_ref[e] + jnp.log2(ls[e]) for e in range(2)]
        o = jnp.where(first, outs[0], outs[1])
        zv = z_ref[...].astype(F32)
        o_ref[...] = o.astype(BF16)
        y_ref[...] = (o * (zv * _sigmoid(zv))).astype(BF16)
        lse_ref[...] = jnp.where(first, lses[0], lses[1])
        for e in range(2):
            lrow_ref[e:e + 1, :] = jnp.transpose(lses[e])[0:1, :]

    blk = pl.BlockSpec((None, tq, LANES), lambda b, j, i: (b, i, j))
    return pl.pallas_call(
        body, name="mla_fwd", grid=(bl, pairs, nq),
        in_specs=[pl.BlockSpec((None, tq, 2 * LANES), lambda b, j, i: (b, i, j)),
                  pl.BlockSpec((None, t, 2 * LANES), lambda b, j, i: (b, 0, j)),
                  pl.BlockSpec((None, t, LANES), lambda b, j, i: (b, 0, v_blk0 + j)),
                  pl.BlockSpec((None, t, LANES), lambda b, j, i: (b, 0, 0)),
                  blk],
        out_specs=(blk, blk, blk, pl.BlockSpec((None, None, None, 2, tq), lambda b, j, i: (b, j, i, 0, 0))),
        out_shape=(jax.ShapeDtypeStruct((bl, t, B_WIDTH), BF16), jax.ShapeDtypeStruct((bl, t, B_WIDTH), BF16),
                   jax.ShapeDtypeStruct((bl, t, B_WIDTH), F32),
                   jax.ShapeDtypeStruct((bl, pairs, nq, 2, tq), F32)),
        scratch_shapes=[pltpu.VMEM((2, tq, LANES), F32), pltpu.VMEM((2, tq, LANES), F32)],
        compiler_params=_params(("parallel", "parallel", "arbitrary")),
    )(q_cat, kvup, kvup, kr, z)


def _mla_dq(q_cat, kvup, kr, do, o, lse, tabs, tq):
    bl, t, _ = q_cat.shape
    nq = t // tq
    pairs = B_HEADS // 2
    v_blk0 = (B_HEADS * LANES) // LANES

    def body(q_ref, k_ref, v_ref, kr_ref, do_ref, o_ref, lse_ref, tc, tsa, tsb, dq_ref, drow_ref, acc_ref):
        qi = pl.program_id(2)
        dov, ov, lsev = do_ref[...], o_ref[...], lse_ref[...]
        qs = [q_ref[:, e * LANES:(e + 1) * LANES] for e in range(2)]
        terms = [_head_terms(dov, ov, lsev, e) for e in range(2)]
        row = lax.broadcasted_iota(jnp.int32, (tq, tq), 0)
        col = lax.broadcasted_iota(jnp.int32, (tq, tq), 1)
        tri = col <= row
        for e in range(2):
            acc_ref[e] = jnp.zeros((tq, LANES), F32)

        def tile(k0, w, masked):
            krv = kr_ref[pl.ds(k0, w), :]
            v = v_ref[pl.ds(k0, w), :]
            ks = [k_ref[pl.ds(k0, w), e * LANES:(e + 1) * LANES] + krv for e in range(2)]
            ss = [_dot_nt(qs[e], ks[e]) for e in range(2)]
            dps = [_dot_nt(terms[e][0], v) for e in range(2)]
            for e in range(2):
                s = jnp.where(tri, ss[e], NEG) if masked else ss[e]
                p = jnp.exp2(s - terms[e][2])
                ds = (p * (dps[e] - terms[e][1])).astype(BF16)
                acc_ref[e] += _dot_nn(ds, ks[e])

        def step(kb2, carry):
            tile(pl.multiple_of(kb2 * 2 * tq, 2 * tq), 2 * tq, False)
            return carry

        lax.fori_loop(0, qi // 2, step, 0)

        @pl.when(qi % 2 == 1)
        def _():
            tile(pl.multiple_of((qi - 1) * tq, tq), tq, False)

        tile(pl.multiple_of(qi * tq, tq), tq, True)
        for e in range(2):
            dq_ref[:, e * LANES:(e + 1) * LANES] = _rope_apply(acc_ref[e] * B_SCALE, tc[...], tsa[...], tsb[...], -1).astype(BF16)
            drow_ref[e:e + 1, :] = _col_to_row(terms[e][1], tq)

    blk = pl.BlockSpec((None, tq, LANES), lambda b, j, i: (b, i, j))
    tab = pl.BlockSpec((None, tq, LANES), lambda b, j, i: (b, i, 0))
    qblk = pl.BlockSpec((None, tq, 2 * LANES), lambda b, j, i: (b, i, j))
    return pl.pallas_call(
        body, name="mla_dq", grid=(bl, pairs, nq),
        in_specs=[qblk,
                  pl.BlockSpec((None, t, 2 * LANES), lambda b, j, i: (b, 0, j)),
                  pl.BlockSpec((None, t, LANES), lambda b, j, i: (b, 0, v_blk0 + j)),
                  pl.BlockSpec((None, t, LANES), lambda b, j, i: (b, 0, 0)),
                  blk, blk, blk, tab, tab, tab],
        out_specs=(qblk, pl.BlockSpec((None, None, None, 2, tq), lambda b, j, i: (b, j, i, 0, 0))),
        out_shape=(jax.ShapeDtypeStruct((bl, t, B_HEADS * LANES), BF16),
                   jax.ShapeDtypeStruct((bl, pairs, nq, 2, tq), F32)),
        scratch_shapes=[pltpu.VMEM((2, tq, LANES), F32)],
        compiler_params=_params(("parallel", "parallel", "arbitrary")),
    )(q_cat, kvup, kvup, kr, do, o, lse, *tabs)


def _mla_dkv(q_cat, kvup, kr, do, lse_rows, delta_rows, tq):
    bl, t, _ = q_cat.shape
    nq = t // tq
    pairs = B_HEADS // 2
    v_blk0 = (B_HEADS * LANES) // LANES

    def body(q_ref, k_ref, v_ref, kr_ref, do_ref, lrow_ref, drow_ref, dk_ref, dv_ref, acc_ref):
        kb = pl.program_id(2)
        v = v_ref[...]
        krv = kr_ref[...]
        ks = [k_ref[:, e * LANES:(e + 1) * LANES] + krv for e in range(2)]
        krow = lax.broadcasted_iota(jnp.int32, (tq, tq), 0)
        qcol = lax.broadcasted_iota(jnp.int32, (tq, tq), 1)
        tri = krow <= qcol
        lane = lax.broadcasted_iota(jnp.int32, (tq, LANES), 1)
        mine = [lane < B_VDIM, lane >= B_VDIM]

        for e in range(3):
            acc_ref[e] = jnp.zeros((tq, LANES), F32)

        def tile(qb, nblk, masked):
            w = nblk * tq
            rows = pl.ds(pl.multiple_of(qb * tq, tq), w)
            dov = do_ref[rows, :]
            lane_w = lax.broadcasted_iota(jnp.int32, (w, LANES), 1)
            mine_w = [lane_w < B_VDIM, lane_w >= B_VDIM]
            qs = [q_ref[rows, e * LANES:(e + 1) * LANES] for e in range(2)]
            does = [jnp.where(mine_w[e], dov, jnp.zeros_like(dov)) for e in range(2)]
            sts = [_dot_nt(ks[e], qs[e]) for e in range(2)]
            dpts = [_dot_nt(v, does[e]) for e in range(2)]

            def rows_of(ref, e):
                return jnp.concatenate([ref[qb + i, e:e + 1, :] for i in range(nblk)], axis=1)

            pts = []
            for e in range(2):
                st = jnp.where(tri, sts[e], NEG) if masked else sts[e]
                pts.append(jnp.exp2(st - rows_of(lrow_ref, e)))
            acc_ref[2] += _dot_nn(pts[0].astype(BF16), does[0]) + _dot_nn(pts[1].astype(BF16), does[1])
            for e in range(2):
                dst = (pts[e] * (dpts[e] - rows_of(drow_ref, e))).astype(BF16)
                acc_ref[e] += _dot_nn(dst, qs[e])

        tile(kb, 1, True)
        rest = nq - 1 - kb
        odd = rest % 2

        @pl.when(odd == 1)
        def _():
            tile(kb + 1, 1, False)

        def step(i, carry):
            tile(kb + 1 + odd + 2 * i, 2, False)
            return carry

        lax.fori_loop(0, rest // 2, step, 0)
        dk_ref[:, 0:LANES] = (acc_ref[0] * LN2).astype(BF16)
        dk_ref[:, LANES:2 * LANES] = (acc_ref[1] * LN2).astype(BF16)
        dv_ref[...] = acc_ref[2].astype(BF16)

    full = pl.BlockSpec((None, t, LANES), lambda b, j, i: (b, 0, j))
    rows = pl.BlockSpec((None, None, nq, 2, tq), lambda b, j, i: (b, j, 0, 0, 0))
    kblk = pl.BlockSpec((None, tq, 2 * LANES), lambda b, j, i: (b, i, j))
    return pl.pallas_call(
        body, name="mla_dkv", grid=(bl, pairs, nq),
        in_specs=[pl.BlockSpec((None, t, 2 * LANES), lambda b, j, i: (b, 0, j)),
                  kblk,
                  pl.BlockSpec((None, tq, LANES), lambda b, j, i: (b, i, v_blk0 + j)),
                  pl.BlockSpec((None, tq, LANES), lambda b, j, i: (b, i, 0)),
                  full, rows, rows],
        out_specs=(kblk, pl.BlockSpec((None, tq, LANES), lambda b, j, i: (b, i, j))),
        out_shape=(jax.ShapeDtypeStruct((bl, t, B_HEADS * LANES), BF16),
                   jax.ShapeDtypeStruct((bl, t, B_WIDTH), BF16)),
        scratch_shapes=[pltpu.VMEM((3, tq, LANES), F32)],
        compiler_params=_params(("parallel", "parallel", "arbitrary")),
    )(q_cat, kvup, kvup, kr, do, lse_rows, delta_rows)


def _adamw(w, g, m, v, *, name):
    r, c = w.shape
    tr = _row_tile(r, 256)
    c1 = 1.0 - ADAM_B1
    c2 = 1.0 - ADAM_B2
    bc1 = 1.0 - ADAM_B1 ** ADAM_STEP
    bc2 = 1.0 - ADAM_B2 ** ADAM_STEP

    def body(w_ref, g_ref, m_ref, v_ref, d_ref, nm_ref, nv_ref):
        gv = g_ref[...]
        nm = ADAM_B1 * m_ref[...] + c1 * gv
        nv = ADAM_B2 * v_ref[...] + c2 * (gv * gv)
        nm_ref[...] = nm
        nv_ref[...] = nv
        d_ref[...] = -ADAM_LR * ((nm / bc1) / (jnp.sqrt(nv / bc2) + ADAM_EPS) + ADAM_WD * w_ref[...])

    blk = pl.BlockSpec((tr, c), lambda i: (i, 0))
    sds = jax.ShapeDtypeStruct((r, c), F32)
    return pl.pallas_call(
        body, name=name, grid=(r // tr,), in_specs=[blk] * 4, out_specs=(blk,) * 3,
        out_shape=(sds,) * 3, compiler_params=_params(("parallel",)),
    )(w, g, m, v)


def _add_my_half(stacked, other, core, out_dtype, *, name):
    nch, a, c = stacked.shape
    h = a // 2
    tr = _row_tile(h, 256)
    nblk = h // tr

    def body(core_ref, s_ref, p_ref, o_ref):
        o_ref[...] = (s_ref[...] + p_ref[...]).astype(o_ref.dtype)

    return pl.pallas_call(
        body, name=name,
        grid_spec=pltpu.PrefetchScalarGridSpec(
            num_scalar_prefetch=1, grid=(nch, nblk),
            in_specs=[pl.BlockSpec((None, tr, c), lambda k, i, cr: (k, cr[0] * nblk + i, 0)),
                      pl.BlockSpec((None, tr, c), lambda k, i, cr: (k, i, 0))],
            out_specs=pl.BlockSpec((None, tr, c), lambda k, i, cr: (k, i, 0))),
        out_shape=jax.ShapeDtypeStruct((nch, h, c), out_dtype),
        compiler_params=_params(("parallel", "parallel")),
    )(core, stacked, other)


def _sum_chips(parts, own, chip, *, name):
    nch, h, c = parts.shape
    tr = _row_tile(h, 256)

    def body(chip_ref, p_ref, own_ref, o_ref):
        me = chip_ref[0]

        def slot(k):
            return jnp.where(me == k, own_ref[k].astype(F32), p_ref[k].astype(F32))

        acc = slot(0) + slot(1)
        for k in range(2, nch):
            acc = acc + slot(k)
        o_ref[...] = acc

    blk = pl.BlockSpec((nch, tr, c), lambda i, cr: (0, i, 0))
    return pl.pallas_call(
        body, name=name,
        grid_spec=pltpu.PrefetchScalarGridSpec(
            num_scalar_prefetch=1, grid=(h // tr,), in_specs=[blk, blk],
            out_specs=pl.BlockSpec((tr, c), lambda i, cr: (i, 0))),
        out_shape=jax.ShapeDtypeStruct((h, c), F32), compiler_params=_params(("parallel",)),
    )(chip, parts, own)


def _join_halves(mine, other, core, *, name):
    h, c = mine.shape
    tr = _row_tile(h, 256)
    nblk = h // tr

    def body(core_ref, m_ref, s_ref, o_ref):
        is_mine = pl.program_id(0) // nblk == core_ref[0]

        @pl.when(is_mine)
        def _():
            o_ref[...] = m_ref[...]

        @pl.when(jnp.logical_not(is_mine))
        def _():
            o_ref[...] = s_ref[...]

    blk = pl.BlockSpec((tr, c), lambda i, cr: (i % nblk, 0))
    return pl.pallas_call(
        body, name=name,
        grid_spec=pltpu.PrefetchScalarGridSpec(
            num_scalar_prefetch=1, grid=(2 * nblk,), in_specs=[blk, blk],
            out_specs=pl.BlockSpec((tr, c), lambda i, cr: (i, 0))),
        out_shape=jax.ShapeDtypeStruct((2 * h, c), F32), compiler_params=_params(("arbitrary",)),
    )(core, mine, other)


def _place():
    x, y, c = lax.axis_index("x"), lax.axis_index("y"), lax.axis_index("c")
    chips = [(1 - x, y), (x, 1 - y), (1 - x, 1 - y)]
    return x, y, c, chips


def _remote(src, dst, send_sems, recv_sems, k, to):
    return pltpu.make_async_remote_copy(src_ref=src, dst_ref=dst, send_sem=send_sems.at[k],
                                        recv_sem=recv_sems.at[k], device_id=to, device_id_type=MESH)


def _hbm_call(body, name, ins, out_shapes, n_remote):
    any_spec = pl.BlockSpec(memory_space=pl.ANY)
    return pl.pallas_call(
        body, name=name, in_specs=[any_spec] * len(ins), out_specs=tuple([any_spec] * len(out_shapes)),
        out_shape=tuple(out_shapes),
        scratch_shapes=[pltpu.SemaphoreType.DMA((n_remote,)), pltpu.SemaphoreType.DMA((n_remote,))],
    )(*ins)


def _all_gather_chips(shards, *, name):
    n = len(shards)

    def body(*refs):
        ins, outs = refs[:n], refs[n:2 * n]
        send_sems, recv_sems = refs[2 * n:]
        x, y, c, chips = _place()
        me = 2 * x + y
        sent = []
        for s in range(n):
            h = ins[s].shape[0] // 2
            for j, (px, py) in enumerate(chips):
                cp = _remote(ins[s].at[pl.ds(c * h, h)], outs[s].at[me, pl.ds(c * h, h)],
                             send_sems, recv_sems, s * 6 + j, (px, py, c))
                cp.start()
                sent.append(cp)
        for s in range(n):
            h = ins[s].shape[0] // 2
            for j, (px, py) in enumerate(chips):
                slab = outs[s].at[2 * px + py, pl.ds(c * h, h)]
                _remote(slab, slab, send_sems, recv_sems, s * 6 + j, (px, py, c)).wait_recv()
                cp = _remote(slab, slab, send_sems, recv_sems, s * 6 + 3 + j, (x, y, 1 - c))
                cp.start()
                sent.append(cp)
        for s in range(n):
            h = ins[s].shape[0] // 2
            for j, (px, py) in enumerate(chips):
                slab = outs[s].at[2 * px + py, pl.ds((1 - c) * h, h)]
                _remote(slab, slab, send_sems, recv_sems, s * 6 + 3 + j, (x, y, 1 - c)).wait_recv()
        for cp in sent:
            cp.wait_send()

    out_shapes = [jax.ShapeDtypeStruct((N_CHIPS,) + s.shape, s.dtype) for s in shards]
    return _hbm_call(body, name, shards, out_shapes, 6 * n)


def _pair_send_other_half(stacked, *, name):
    n = len(stacked)

    def body(*refs):
        ins, outs = refs[:n], refs[n:2 * n]
        send_sems, recv_sems = refs[2 * n:]
        x, y, c, _chips = _place()
        sent = []
        for s in range(n):
            h = ins[s].shape[1] // 2
            cp = _remote(ins[s].at[:, pl.ds((1 - c) * h, h)], outs[s], send_sems, recv_sems, s, (x, y, 1 - c))
            cp.start()
            sent.append(cp)
        for cp in sent:
            cp.wait_recv()
        for cp in sent:
            cp.wait_send()

    out_shapes = [jax.ShapeDtypeStruct((s.shape[0], s.shape[1] // 2, s.shape[2]), s.dtype) for s in stacked]
    return _hbm_call(body, name, stacked, out_shapes, n)


def _chip_exchange(halves, *, name):
    n = len(halves)

    def body(*refs):
        ins, outs = refs[:n], refs[n:2 * n]
        send_sems, recv_sems = refs[2 * n:]
        x, y, c, chips = _place()
        me = 2 * x + y
        sent = []
        for s in range(n):
            for j, (px, py) in enumerate(chips):
                cp = _remote(ins[s].at[2 * px + py], outs[s].at[me], send_sems, recv_sems, s * 3 + j, (px, py, c))
                cp.start()
                sent.append(cp)
        for s in range(n):
            for j, (px, py) in enumerate(chips):
                slab = outs[s].at[2 * px + py]
                _remote(slab, slab, send_sems, recv_sems, s * 3 + j, (px, py, c)).wait_recv()
        for cp in sent:
            cp.wait_send()

    out_shapes = [jax.ShapeDtypeStruct(s.shape, s.dtype) for s in halves]
    return _hbm_call(body, name, halves, out_shapes, 3 * n)


def _chip_exchange_start(halves, *, name):
    n = len(halves)
    hbm = pl.BlockSpec(memory_space=pltpu.HBM)
    sem = pl.BlockSpec(memory_space=pltpu.SEMAPHORE)

    def body(*refs):
        ins, lands = refs[:n], refs[n:2 * n]
        send_sems, recv_sems = refs[2 * n], refs[2 * n + 1]
        token = refs[-1]
        x, y, c, chips = _place()
        me = 2 * x + y
        for s in range(n):
            for j, (px, py) in enumerate(chips):
                _remote(ins[s].at[2 * px + py], lands[s].at[me], send_sems, recv_sems, s * 3 + j, (px, py, c)).start()
        token[...] = jnp.zeros_like(token)

    slabs = [pltpu.HBM(s.shape, s.dtype) for s in halves]
    outs = pl.pallas_call(
        body, name=name,
        out_shape=(pltpu.SemaphoreType.DMA((3 * n,)), pltpu.SemaphoreType.DMA((3 * n,)), *slabs, *slabs,
                   jax.ShapeDtypeStruct((8, LANES), F32)),
        in_specs=[hbm] * (2 * n), out_specs=(sem, sem, *([hbm] * (2 * n)), pl.BlockSpec(memory_space=pltpu.VMEM)),
        input_output_aliases={i: 2 + i for i in range(2 * n)},
        compiler_params=pltpu.CompilerParams(has_side_effects=pltpu.SideEffectType.DATAFLOW_SIDE_EFFECTING),
    )(*[pltpu.with_memory_space_constraint(s, pltpu.HBM) for s in halves],
      *[pltpu.with_memory_space_constraint(lax.empty(s.shape, s.dtype), pltpu.HBM) for s in halves])
    return outs[0], outs[1], list(outs[2:2 + n]), list(outs[2 + n:2 + 2 * n]), outs[-1]


def _chip_exchange_wait(send_sems, recv_sems, sent, lands, after, *, name):
    n = len(sent)
    hbm = pl.BlockSpec(memory_space=pltpu.HBM)
    sem = pl.BlockSpec(memory_space=pltpu.SEMAPHORE)

    def body(*refs):
        ins, lands_in = refs[:n], refs[n:2 * n]
        send_sems, recv_sems = refs[2 * n], refs[2 * n + 1]
        x, y, c, chips = _place()
        me = 2 * x + y
        for s in range(n):
            for j, (px, py) in enumerate(chips):
                k = 2 * px + py
                _remote(ins[s].at[k], lands_in[s].at[me], send_sems, recv_sems, s * 3 + j, (px, py, c)).wait_send()
                _remote(ins[s].at[k], lands_in[s].at[k], send_sems, recv_sems, s * 3 + j, (px, py, c)).wait_recv()

    slabs = [pltpu.HBM(s.shape, s.dtype) for s in sent]
    outs = pl.pallas_call(
        body, name=name, out_shape=(*slabs, *slabs),
        in_specs=[hbm] * (2 * n) + [sem, sem, pl.BlockSpec(memory_space=pl.ANY)],
        out_specs=tuple([hbm] * (2 * n)), input_output_aliases={i: i for i in range(2 * n)},
        compiler_params=pltpu.CompilerParams(has_side_effects=pltpu.SideEffectType.DATAFLOW_SIDE_EFFECTING),
    )(*sent, *lands, send_sems, recv_sems, after)
    return list(outs[n:])


def _pair_swap(halves, *, name):
    n = len(halves)

    def body(*refs):
        ins, outs = refs[:n], refs[n:2 * n]
        send_sems, recv_sems = refs[2 * n:]
        x, y, c, _chips = _place()
        sent = []
        for s in range(n):
            cp = _remote(ins[s], outs[s], send_sems, recv_sems, s, (x, y, 1 - c))
            cp.start()
            sent.append(cp)
        for cp in sent:
            cp.wait_recv()
        for cp in sent:
            cp.wait_send()

    out_shapes = [jax.ShapeDtypeStruct(s.shape, s.dtype) for s in halves]
    return _hbm_call(body, name, halves, out_shapes, n)


def _pack_rows(parts, row_multiple):
    flat = jnp.concatenate([p.reshape(-1) for p in parts])
    quantum = row_multiple * PACK_COLS
    pad = (-flat.shape[0]) % quantum
    flat = jnp.pad(flat, (0, pad))
    return flat.reshape(-1, PACK_COLS)


def _unpack(flat, shapes):
    out, pos = [], 0
    for shp in shapes:
        size = math.prod(shp)
        out.append(flat[pos:pos + size].reshape(shp))
        pos += size
    return out


def _to_chunks_cols(full):
    r, c4 = full.shape
    return full.reshape(r, N_CHIPS, c4 // N_CHIPS).transpose(1, 0, 2)


def _from_chunks_cols(stacked):
    nch, r, c = stacked.shape
    return stacked.transpose(1, 0, 2).reshape(r, nch * c)


def _class_major(a, bl, t, dil):
    w = a.shape[-1]
    if dil == 1:
        return a.reshape(bl, 1, t, w)
    return a.reshape(bl, t // dil, dil, w).transpose(0, 2, 1, 3)


def _natural(a):
    bl, dil, ln, w = a.shape
    if dil == 1:
        return a.reshape(bl * ln, w)
    return a.transpose(0, 2, 1, 3).reshape(bl * ln * dil, w)


def _train_step(x, positions, a_pre_norm, a_w_in, a_w_out, a_post_norm, kv_norm, kv_w_down, kv_latent_norm,
                kv_w_up, b_pre_norm, b_w_in, b_q_norm, b_w_q_up, b_w_out, b_post_norm, loss_target, moments):
    bl, t, d = x.shape
    n = bl * t
    qb = t // A_DILATIONS[-1]
    tq = _tile(t, 256)
    dq4 = d // N_CHIPS
    chip = 2 * lax.axis_index("x") + lax.axis_index("y")
    chip_arr = chip.astype(jnp.int32).reshape(1)
    core_arr = lax.axis_index("c").astype(jnp.int32).reshape(1)

    w_in_a_s = a_w_in[0].astype(BF16)
    outs_s = jnp.concatenate([a_w_out[0], b_w_out[0]], axis=0).astype(BF16)
    small_shapes = [kv_w_down.shape, kv_w_up.shape, b_w_in[0].shape, b_w_q_up[0].shape]
    small_s = _pack_rows([kv_w_down, kv_w_up, b_w_in[0], b_w_q_up[0]], 32).astype(BF16)
    gains_s = jnp.pad(jnp.concatenate([a_pre_norm[0], a_post_norm[0]]), (0, 16 * LANES - 2 * dq4)).reshape(16, LANES)
    shards = [w_in_a_s, outs_s, small_s, gains_s]
    gathered = _all_gather_chips(shards, name="gather_weights")
    g_in_a, g_outs, g_small, g_gains = [lax.dynamic_update_index_in_dim(g, s, chip, 0)
                                        for g, s in zip(gathered, shards)]

    w_in_a = _from_chunks_cols(g_in_a)
    w_out_a = g_outs[:, :A_WIDTH // N_CHIPS].reshape(A_WIDTH, d)
    w_out_b = g_outs[:, A_WIDTH // N_CHIPS:].reshape(B_WIDTH, d)
    sm = [_unpack(g_small[k].reshape(-1), small_shapes) for k in range(N_CHIPS)]
    w_down = jnp.concatenate([sm[k][0] for k in range(N_CHIPS)], axis=0)
    w_up = jnp.concatenate([sm[k][1] for k in range(N_CHIPS)], axis=1)
    w_in_b = jnp.concatenate([sm[k][2] for k in range(N_CHIPS)], axis=1)
    w_q_up = jnp.concatenate([sm[k][3] for k in range(N_CHIPS)], axis=1)
    gflat = g_gains.reshape(N_CHIPS, -1)
    g_a_pre = gflat[:, :dq4].reshape(1, d)
    g_a_post = gflat[:, dq4:2 * dq4].reshape(1, d)

    w_up_h = w_up.reshape(B_KV_LORA, B_HEADS, B_NOPE + B_VDIM)
    w_up_k = jnp.pad(w_up_h[:, :, :B_NOPE], ((0, 0), (0, 0), (0, LANES - B_NOPE))).reshape(B_KV_LORA, B_HEADS * LANES)
    w_up_v = w_up_h[:, :, B_NOPE:].reshape(B_KV_LORA, B_WIDTH)
    w_up_cat = jnp.concatenate([w_up_k, w_up_v], axis=1)
    w_q_up_p = jnp.pad(w_q_up.reshape(B_Q_LORA, B_HEADS, B_QK_DIM),
                       ((0, 0), (0, 0), (0, LANES - B_QK_DIM))).reshape(B_Q_LORA, B_HEADS * LANES)
    zeros_d = lambda c: jnp.zeros((d, c), BF16)
    w_down_p = jnp.concatenate([w_down[:, :B_KV_LORA], zeros_d(B_NOPE), w_down[:, B_KV_LORA:],
                                zeros_d(LANES - B_NOPE - B_ROPE)], axis=1)
    w_cq = w_in_b[:, :B_Q_LORA]
    w_z = w_in_b[:, B_Q_LORA:]

    tabs_a = _rope_tables(positions, A_ROPE_THETA, 0)
    tabs_b = _rope_tables(positions, B_ROPE_THETA, B_NOPE)

    h0 = x.reshape(n, d)
    hn_a = _rms_fwd(h0, g_a_pre, BF16, name="a_pre_norm")
    is_qk = lambda j: j != 2
    is_q = lambda j: j == 0
    z_blk_a = 3 * A_GROUPS
    z_a = _matmul(hn_a, w_in_a, "nn", BF16, name="a_proj_z", b_cols=(z_blk_a, 1))
    o_groups, lse_groups, qkv_cm, hn_cm, tabs_cm = [], [], [], [], []
    for g, dil in enumerate(A_DILATIONS):
        flat = lambda a: _class_major(a, bl, t, dil).reshape(n, a.shape[-1])
        hn_g = hn_a if dil == 1 else flat(hn_a)
        tabs_g = tabs_a if dil == 1 else lax.optimization_barrier(tuple(flat(tb) for tb in tabs_a))
        proj_g = _matmul(hn_g, w_in_a, "nn", BF16, name=f"a_proj_{g}", rope=(tabs_g, is_qk),
                         out_scale=(A_SCALE * LOG2E, is_q), b_cols=(3 * g, 3))
        src = proj_g.reshape(bl, dil, t // dil, 3 * A_WIDTH)
        hn_cm.append(hn_g)
        tabs_cm.append(tabs_g)
        qkv_cm.append(src)
        o_g, lse_g = _attn_a_fwd(src, 0, qb, BF16, name=f"attn_a_fwd_{g}")
        o_groups.append(_natural(o_g))
        lse_groups.append(_natural(lse_g))
    ypre_a, om_a, lse_a = _merge_gate_fwd(o_groups, lse_groups, z_a, 0)
    y_a = _matmul(ypre_a, w_out_a, "nn", F32, name="a_out")
    g_kvn = kv_norm.reshape(1, d)
    g_lat = kv_latent_norm.reshape(1, B_KV_LORA)
    h1, hn_kv, hn_b = _post_norm_block(y_a, g_a_post, h0, [g_kvn, b_pre_norm], name="a_post_norm")

    ckr = _matmul(hn_kv, w_down_p, "nn", F32, name="kv_down")
    c_kv, k_rope = _kv_latent_fwd(ckr, g_lat, tabs_b)
    kvup = _matmul(c_kv, w_up_cat, "nn", BF16, name="kv_up")
    z_b = _matmul(hn_b, w_z, "nn", BF16, name="b_proj_z")
    cq_raw = _matmul(hn_b, w_cq, "nn", F32, name="b_proj_q")
    c_q = _rms_fwd(cq_raw, b_q_norm, BF16, name="b_q_norm")
    always = lambda j: True
    q_cat = _matmul(c_q, w_q_up_p, "nn", BF16, name="b_q_up", rope=(tabs_b, always),
                    out_scale=(B_SCALE * LOG2E, always))
    r3 = lambda a: a.reshape(bl, t, a.shape[-1])
    tabs_b3 = tuple(r3(tb) for tb in tabs_b)
    ypre_b, o_b, lse_b, lse_rows_b = _mla_fwd(r3(q_cat), r3(kvup), r3(k_rope), r3(z_b), tq)
    y_b = _matmul(ypre_b.reshape(n, B_WIDTH), w_out_b, "nn", F32, name="b_out")
    dh2, loss_part = _post_norm_loss(y_b, b_post_norm, h1, loss_target.reshape(n, d))

    dy_b, dg_b_post = _rms_bwd(y_b, b_post_norm, dh2, BF16, name="b_post_norm_bwd")
    dypre_b = _matmul(dy_b, w_out_b, "nt", BF16, name="b_out_dx")
    dw_out_b = _matmul(ypre_b.reshape(n, B_WIDTH), dy_b, "tn", F32, name="b_out_dw", tm=1024, tk=512)
    do_b, dz_b = _gate_bwd(dypre_b, o_b.reshape(n, B_WIDTH), z_b, 0, name="b_gate_bwd", with_delta=False)
    dq_cat, delta_rows_b = _mla_dq(r3(q_cat), r3(kvup), r3(k_rope), r3(do_b), o_b, lse_b, tabs_b3, tq)
    dq_cat = dq_cat.reshape(n, -1)
    dk_cat, dv_b = _mla_dkv(r3(q_cat), r3(kvup), r3(k_rope), r3(do_b), lse_rows_b, delta_rows_b, tq)
    dk_cat, dv_b = dk_cat.reshape(n, -1), dv_b.reshape(n, -1)
    dcq_n = _matmul(dq_cat, w_q_up_p, "nt", F32, name="b_q_up_dx")
    dw_q_up_p = _matmul(c_q, dq_cat, "tn", F32, name="b_q_up_dw", tm=1024, tk=512)
    dcq, dg_b_q = _rms_bwd(cq_raw, b_q_norm, dcq_n, BF16, name="b_q_norm_bwd")
    dhn_b = _matmul(dz_b, w_z, "nt", F32, name="b_proj_z_dx")
    dhn_b = _matmul(dcq, w_cq, "nt", F32, name="b_proj_q_dx", add=dhn_b)
    dw_z = _matmul(hn_b, dz_b, "tn", F32, name="b_proj_z_dw", tm=1024, tk=512)
    dw_cq = _matmul(hn_b, dcq, "tn", F32, name="b_proj_q_dw", tm=1024, tk=512)
    dckv_n = _matmul(dk_cat, w_up_k, "nt", F32, name="kv_up_k_dx")
    dckv_n = _matmul(dv_b, w_up_v, "nt", F32, name="kv_up_v_dx", add=dckv_n)
    dw_up_k = _matmul(c_kv, dk_cat, "tn", F32, name="kv_up_k_dw", tm=1024, tk=512)
    dw_up_v = _matmul(c_kv, dv_b, "tn", F32, name="kv_up_v_dw", tm=1024, tk=512)
    dckr, dg_lat = _kv_latent_bwd(dckv_n, ckr, g_lat, dk_cat, tabs_b)
    dhn_kv = _matmul(dckr, w_down_p, "nt", F32, name="kv_down_dx")
    dw_down_p = _matmul(hn_kv, dckr, "tn", F32, name="kv_down_dw", tm=1024, tk=512)
    dh1, dg_b_pre, dg_kvn = _rms_bwd_pair(h1, b_pre_norm, dhn_b, g_kvn, dhn_kv, dh2, name="h1_norms_bwd")

    dy_a, dg_a_post = _rms_bwd(y_a, g_a_post, dh1, BF16, name="a_post_norm_bwd")
    dypre_a = _matmul(dy_a, w_out_a, "nt", BF16, name="a_out_dx")
    dw_out_a = _matmul(ypre_a, dy_a, "tn", F32, name="a_out_dw", tm=1024, tk=512)
    do_a, dz_a, delta_a = _gate_bwd(dypre_a, om_a, z_a, 0, name="a_gate_bwd", with_delta=True)
    dw_cols = A_IN_WIDTH // N_CHIPS
    dw_tn = _tile(dw_cols, 512)
    dw_kwargs = dict(tm=1024, tn=dw_tn, tk=512, out_chunk_blocks=dw_cols // dw_tn)
    r_big = _matmul(hn_a, dz_a, "tn", F32, name="a_proj_dw_z", out_full=(N_CHIPS, d, dw_cols),
                    out_joff=z_blk_a * A_WIDTH // dw_tn, **dw_kwargs)
    dqkvs = []
    for g, dil in enumerate(A_DILATIONS):
        cm = lambda a: _class_major(a, bl, t, dil)
        swap = lambda a: jnp.swapaxes(a, 2, 3)
        lse_cm, delta_cm = cm(lse_a), cm(delta_a)
        tabs_g = tuple(tb.reshape(bl, dil, t // dil, LANES) for tb in tabs_cm[g])
        dqkv = _attn_a_bwd(qkv_cm[g], 0, cm(do_a), lse_cm, delta_cm, swap(lse_cm), swap(delta_cm),
                           tabs_g, qb, name=f"attn_a_bwd_{g}").reshape(n, 3 * A_WIDTH)
        dqkvs.append(dqkv)
        r_big = _matmul(hn_cm[g], dqkv, "tn", F32, name=f"a_proj_dw_{g}", out_into=r_big,
                        out_joff=3 * g * A_WIDTH // dw_tn, **dw_kwargs)
    r_outs = jnp.concatenate([dw_out_a.reshape(N_CHIPS, A_WIDTH // N_CHIPS, d),
                              dw_out_b.reshape(N_CHIPS, B_WIDTH // N_CHIPS, d)], axis=1)

    bulk = [r_big, r_outs]
    recv_b = _pair_send_other_half(bulk, name="reduce_pair_send")
    halves_b = [_add_my_half(s, p, core_arr, BF16, name=f"reduce_pair_add_{i}")
                for i, (s, p) in enumerate(zip(bulk, recv_b))]
    send_sems, recv_sems, sent_b, lands_b, token = _chip_exchange_start(halves_b, name="reduce_exchange_start")

    dhn_a = _matmul(dz_a, w_in_a, "nt", F32, name="a_proj_dx_z", b_koff=z_blk_a, after=token)
    dhn_more = []
    for g, dil in enumerate(A_DILATIONS):
        koff = 3 * g * A_WIDTH // _tile(3 * A_WIDTH, 1024)
        if dil == 1:
            dhn_a = _matmul(dqkvs[g], w_in_a, "nt", F32, name=f"a_proj_dx_{g}", add=dhn_a, b_koff=koff, after=token)
        else:
            part = _matmul(dqkvs[g], w_in_a, "nt", BF16, name=f"a_proj_dx_{g}", b_koff=koff, after=token)
            dhn_more.append(_natural(part.reshape(bl, dil, t // dil, d)))
    grad_x, dg_a_pre = _rms_bwd(h0, g_a_pre, dhn_a, F32, name="a_pre_norm_bwd", adds=(dh1,),
                                dy_more=tuple(dhn_more))

    dw_up = jnp.concatenate([dw_up_k.reshape(B_KV_LORA, B_HEADS, LANES)[:, :, :B_NOPE],
                             dw_up_v.reshape(B_KV_LORA, B_HEADS, B_VDIM)], axis=2).reshape(B_KV_LORA, -1)
    dw_q_up = dw_q_up_p.reshape(B_Q_LORA, B_HEADS, LANES)[:, :, :B_QK_DIM].reshape(B_Q_LORA, -1)
    dw_down = jnp.concatenate([dw_down_p[:, :B_KV_LORA], dw_down_p[:, B_KV_LORA + B_NOPE:B_KV_LORA + B_NOPE + B_ROPE]], axis=1)
    dw_in_b = jnp.concatenate([dw_cq, dw_z], axis=1)
    vec_rep = [dg_kvn.reshape(-1), dg_lat.reshape(-1), dg_b_pre.reshape(-1), dg_b_q.reshape(-1),
               dg_b_post.reshape(-1), loss_part.reshape(-1)]
    vec_shapes = [(dq4,), (dq4,)] + [v.shape for v in vec_rep]
    down_c = dw_down.reshape(N_CHIPS, dq4, -1)
    up_c = _to_chunks_cols(dw_up)
    inb_c = _to_chunks_cols(dw_in_b)
    qup_c = _to_chunks_cols(dw_q_up)
    small_chunks = []
    for k in range(N_CHIPS):
        vecs = [dg_a_pre.reshape(-1)[k * dq4:(k + 1) * dq4], dg_a_post.reshape(-1)[k * dq4:(k + 1) * dq4]] + vec_rep
        small_chunks.append(_pack_rows([down_c[k], up_c[k], inb_c[k], qup_c[k]] + vecs, 32))
    r_small = jnp.stack(small_chunks)

    recv_s = _pair_send_other_half([r_small], name="reduce_pair_send_small")
    halves_s = [_add_my_half(r_small, recv_s[0], core_arr, F32, name="reduce_pair_add_small")]
    parts_s = list(_chip_exchange(halves_s, name="reduce_exchange_small"))
    parts_b = _chip_exchange_wait(send_sems, recv_sems, sent_b, lands_b, grad_x, name="reduce_exchange_wait")
    sums = [_sum_chips(p, own, chip_arr, name=f"reduce_chip_sum_{i}")
            for i, (p, own) in enumerate(zip(parts_b + parts_s, sent_b + halves_s))]
    others = _pair_swap(sums, name="reduce_pair_swap")
    g_big, g_outs_r, g_small_r = [_join_halves(m, o, core_arr, name=f"reduce_join_{i}")
                                  for i, (m, o) in enumerate(zip(sums, others))]

    grads = {}
    grads["a_w_in"] = g_big
    grads["a_w_out"] = g_outs_r[:A_WIDTH // N_CHIPS]
    grads["b_w_out"] = g_outs_r[A_WIDTH // N_CHIPS:]
    small_out_shapes = [down_c.shape[1:], up_c.shape[1:], inb_c.shape[1:], qup_c.shape[1:]] + vec_shapes
    (grads["kv_w_down"], grads["kv_w_up"], grads["b_w_in"], grads["b_w_q_up"], grads["a_pre_norm"],
     grads["a_post_norm"], grads["kv_norm"], grads["kv_latent_norm"], grads["b_pre_norm"], grads["b_q_norm"],
     grads["b_post_norm"], loss_sum) = _unpack(g_small_r.reshape(-1), small_out_shapes)

    weights = dict(a_pre_norm=a_pre_norm, a_w_in=a_w_in, a_w_out=a_w_out, a_post_norm=a_post_norm, kv_norm=kv_norm,
                   kv_w_down=kv_w_down, kv_latent_norm=kv_latent_norm, kv_w_up=kv_w_up, b_pre_norm=b_pre_norm,
                   b_w_in=b_w_in, b_q_norm=b_q_norm, b_w_q_up=b_w_q_up, b_w_out=b_w_out, b_post_norm=b_post_norm)
    names = list(weights)
    out_g, out_d, out_m, out_v = [], [], [], []
    for i, nm in enumerate(names):
        w = weights[nm]
        two_d = (1, w.shape[0]) if w.ndim == 1 else (w.shape[-2], w.shape[-1])
        gw = grads[nm].reshape(two_d)
        dlt, new_m, new_v = _adamw(w.reshape(two_d), gw, moments[i].reshape(two_d),
                                   moments[len(names) + i].reshape(two_d), name=f"adamw_{nm}")
        out_g.append(gw.reshape(w.shape))
        out_d.append(dlt.reshape(w.shape))
        out_m.append(new_m.reshape(w.shape))
        out_v.append(new_v.reshape(w.shape))
    return (loss_sum.reshape(()), grad_x.reshape(bl, t, d), *out_g, *out_d, *out_m, *out_v)


def kernel(x, positions, a_pre_norm, a_w_in, a_w_out, a_post_norm, kv_norm, kv_w_down, kv_latent_norm, kv_w_up, b_pre_norm, b_w_in, b_q_norm, b_w_q_up, b_w_out, b_post_norm, loss_target, m_a_pre_norm, m_a_w_in, m_a_w_out, m_a_post_norm, m_kv_norm, m_kv_w_down, m_kv_latent_norm, m_kv_w_up, m_b_pre_norm, m_b_w_in, m_b_q_norm, m_b_w_q_up, m_b_w_out, m_b_post_norm, v_a_pre_norm, v_a_w_in, v_a_w_out, v_a_post_norm, v_kv_norm, v_kv_w_down, v_kv_latent_norm, v_kv_w_up, v_b_pre_norm, v_b_w_in, v_b_q_norm, v_b_w_q_up, v_b_w_out, v_b_post_norm):
    moments = (m_a_pre_norm, m_a_w_in, m_a_w_out, m_a_post_norm, m_kv_norm, m_kv_w_down, m_kv_latent_norm, m_kv_w_up,
               m_b_pre_norm, m_b_w_in, m_b_q_norm, m_b_w_q_up, m_b_w_out, m_b_post_norm,
               v_a_pre_norm, v_a_w_in, v_a_w_out, v_a_post_norm, v_kv_norm, v_kv_w_down, v_kv_latent_norm, v_kv_w_up,
               v_b_pre_norm, v_b_w_in, v_b_q_norm, v_b_w_q_up, v_b_w_out, v_b_post_norm)
    return _train_step(x, positions, a_pre_norm, a_w_in, a_w_out, a_post_norm, kv_norm, kv_w_down, kv_latent_norm,
                       kv_w_up, b_pre_norm, b_w_in, b_q_norm, b_w_q_up, b_w_out, b_post_norm, loss_target, moments)
```

```python
import math

import jax
import jax.numpy as jnp
from jax import lax
from jax.experimental import pallas as pl
from jax.experimental.pallas import tpu as pltpu

F32 = jnp.float32
BF16 = jnp.bfloat16
MESH = pl.DeviceIdType.MESH

NORM_EPS = 1e-6
NEG = -1e30
LANES = 128
VMEM_LIMIT = 56 * 1024 * 1024
LOG2E = math.log2(math.e)
LN2 = math.log(2.0)

A_GROUPS = 3
A_DILATIONS = (1, 4, 16)
A_HEADS = 8
A_HEAD_DIM = 128
A_WIDTH = A_HEADS * A_HEAD_DIM
A_ROPE_THETA = 500000.0
A_IN_WIDTH = A_GROUPS * 3 * A_WIDTH + A_WIDTH
A_SCALE = A_HEAD_DIM ** -0.5

B_HEADS = 16
B_NOPE = 64
B_ROPE = 32
B_QK_DIM = B_NOPE + B_ROPE
B_VDIM = 64
B_WIDTH = B_HEADS * B_VDIM
B_Q_LORA = 384
B_KV_LORA = 256
B_ROPE_THETA = 10000.0
B_SCALE = B_QK_DIM ** -0.5

ADAM_LR = 0.001
ADAM_B1 = 0.9
ADAM_B2 = 0.999
ADAM_EPS = 1e-08
ADAM_WD = 0.01
ADAM_STEP = 10

N_CHIPS = 4
PACK_COLS = 512


def _params(sem=None):
    return pltpu.CompilerParams(dimension_semantics=sem, vmem_limit_bytes=VMEM_LIMIT)


def _tile(n, want):
    t = min(n, want)
    assert n % t == 0, (n, want)
    return t


def _row_tile(n, want):
    for t in range(min(n, want), 0, -1):
        if n % t == 0 and (t % 16 == 0 or t == n):
            return t
    return n


def _rope_tables(positions, theta, lane0):
    half = 16
    inv_freq = 1.0 / (theta ** (jnp.arange(half, dtype=F32) * (2.0 / (2 * half))))
    n = positions.size
    per_row = LANES // half
    pos = jnp.repeat(positions.astype(F32).reshape(n // per_row, per_row), half, axis=1)
    ang = pos * jnp.tile(inv_freq, per_row)
    cos, sin = lax.optimization_barrier((jnp.cos(ang), jnp.sin(ang)))
    cos, sin = cos.reshape(n, half), sin.reshape(n, half)
    pre = jnp.zeros((n, lane0), F32)
    post = jnp.zeros((n, LANES - lane0 - 2 * half), F32)
    z16 = jnp.zeros((n, half), F32)
    c = jnp.concatenate([pre + 1.0, cos, cos, post + 1.0], axis=1)
    sa = jnp.concatenate([pre, -sin, z16, post], axis=1)
    sb = jnp.concatenate([pre, z16, sin, post], axis=1)
    return lax.optimization_barrier((c, sa, sb))


def _rope_apply(x, c, sa, sb, sign):
    k = x.shape[1] // LANES
    if k > 1:
        c, sa, sb = (jnp.concatenate([t] * k, axis=1) for t in (c, sa, sb))
    w = x.shape[1]
    up = pltpu.roll(x, w - 16, 1)
    dn = pltpu.roll(x, 16, 1)
    if sign > 0:
        return x * c + up * sa + dn * sb
    return x * c - up * sa - dn * sb


def _matmul(a, b, mode, out_dtype, *, name, tm=512, tn=1024, tk=1024, add=None, rope=None,
            out_scale=None, b_koff=0, b_cols=None, out_into=None, out_full=None, out_joff=0,
            out_chunk_blocks=None, after=None):
    if mode == "nn":
        m, k = a.shape
        n = b.shape[1]
    elif mode == "nt":
        m, k = a.shape
        n = b.shape[0]
    else:
        k, m = a.shape
        n = b.shape[1]
    b_j0 = 0
    if b_cols is not None:
        tn = _tile(n, tn)
        b_j0, n = b_cols[0], b_cols[1] * tn
    tm, tn, tk = _tile(m, tm), _tile(n, tn), _tile(k, tk)
    nk = k // tk
    if mode == "nn":
        a_spec = pl.BlockSpec((tm, tk), lambda j, i, kk: (i, kk))
        b_spec = pl.BlockSpec((tk, tn), lambda j, i, kk: (kk, j + b_j0))
        dims = (((1,), (0,)), ((), ()))
    elif mode == "nt":
        a_spec = pl.BlockSpec((tm, tk), lambda j, i, kk: (i, kk))
        b_spec = pl.BlockSpec((tn, tk), lambda j, i, kk: (j, kk + b_koff))
        dims = (((1,), (1,)), ((), ()))
    else:
        a_spec = pl.BlockSpec((tk, tm), lambda j, i, kk: (kk, i))
        b_spec = pl.BlockSpec((tk, tn), lambda j, i, kk: (kk, j))
        dims = (((0,), (0,)), ((), ()))
    operands = [a, b]
    in_specs = [a_spec, b_spec]
    if add is not None:
        operands.append(add)
        in_specs.append(pl.BlockSpec((tm, tn), lambda j, i, kk: (i, j)))
    if rope is not None:
        tables, rope_pred = rope
        for t in tables:
            operands.append(t)
            in_specs.append(pl.BlockSpec((tm, LANES), lambda j, i, kk: (i, 0)))
    aliases = {}
    if out_into is not None:
        aliases = {len(operands): 0}
        operands.append(out_into)
        in_specs.append(pl.BlockSpec(memory_space=pl.ANY))
        out_shape = jax.ShapeDtypeStruct(out_into.shape, out_into.dtype)
    elif out_full is not None:
        out_shape = jax.ShapeDtypeStruct(out_full, out_dtype)
    else:
        out_shape = jax.ShapeDtypeStruct((m, n), out_dtype)
    if after is not None:
        operands.append(after)
        in_specs.append(pl.BlockSpec(memory_space=pl.ANY))
    if out_chunk_blocks is not None:
        out_spec = pl.BlockSpec((None, tm, tn), lambda j, i, kk: ((j + out_joff) // out_chunk_blocks, i,
                                                                  (j + out_joff) % out_chunk_blocks))
    else:
        out_spec = pl.BlockSpec((tm, tn), lambda j, i, kk: (i, j + out_joff))

    def body(*refs):
        a_ref, b_ref = refs[0], refs[1]
        pos = 2
        add_ref = None
        if add is not None:
            add_ref = refs[pos]
            pos += 1
        tab_refs = None
        if rope is not None:
            tab_refs = refs[pos:pos + 3]
            pos += 3
        if out_into is not None:
            pos += 1
        if after is not None:
            pos += 1
        o_ref = refs[pos]
        acc_ref = refs[pos + 1] if nk > 1 else None

        def finish(res):
            if add_ref is not None:
                res = res + add_ref[...].astype(F32)
            if tab_refs is None:
                o_ref[...] = res.astype(o_ref.dtype)
                return
            j = pl.program_id(0)
            flag = rope_pred(j)
            roped = _rope_apply(res, tab_refs[0][...], tab_refs[1][...], tab_refs[2][...], 1)
            if out_scale is not None:
                value, scale_pred = out_scale
                use = scale_pred(j)
                roped = roped * (value if use is True else jnp.where(use, value, 1.0))
            if flag is True:
                o_ref[...] = roped.astype(o_ref.dtype)
                return

            @pl.when(flag)
            def _():
                o_ref[...] = roped.astype(o_ref.dtype)

            @pl.when(jnp.logical_not(flag))
            def _():
                o_ref[...] = res.astype(o_ref.dtype)

        part = lax.dot_general(a_ref[...].astype(BF16), b_ref[...].astype(BF16), dims,
                               preferred_element_type=F32)
        if nk == 1:
            finish(part)
            return
        kk = pl.program_id(2)

        @pl.when(kk == 0)
        def _():
            acc_ref[...] = part

        @pl.when(kk > 0)
        def _():
            acc_ref[...] += part

        @pl.when(kk == nk - 1)
        def _():
            finish(acc_ref[...])

    return pl.pallas_call(
        body, name=name, grid=(n // tn, m // tm, nk), in_specs=in_specs, out_specs=out_spec,
        out_shape=out_shape, input_output_aliases=aliases,
        scratch_shapes=[pltpu.VMEM((tm, tn), F32)] if nk > 1 else [],
        compiler_params=_params(("parallel", "parallel", "arbitrary")),
    )(*operands)


def _rms_fwd(x, g, out_dtype, *, name, add=None, tr=512):
    n, d = x.shape
    tr = _tile(n, tr)
    row = pl.BlockSpec((tr, d), lambda i: (i, 0))
    vec = pl.BlockSpec((1, d), lambda i: (0, 0))

    def body(*refs):
        x_ref, g_ref = refs[0], refs[1]
        o_ref = refs[-1]
        xv = x_ref[...].astype(F32)
        r = lax.rsqrt(jnp.mean(xv * xv, axis=-1, keepdims=True) + NORM_EPS)
        y = xv * r * g_ref[...]
        if add is not None:
            y = refs[2][...] + y
        o_ref[...] = y.astype(o_ref.dtype)

    ops = [x, g] + ([add] if add is not None else [])
    specs = [row, vec] + ([row] if add is not None else [])
    return pl.pallas_call(
        body, name=name, grid=(n // tr,), in_specs=specs, out_specs=row,
        out_shape=jax.ShapeDtypeStruct((n, d), out_dtype), compiler_params=_params(("parallel",)),
    )(*ops)


def _rms_bwd(x, g, dy, out_dtype, *, name, adds=(), dy_more=(), tr=512):
    n, d = x.shape
    tr = _tile(n, tr)
    steps = n // tr
    row = pl.BlockSpec((tr, d), lambda i: (i, 0))
    vec = pl.BlockSpec((1, d), lambda i: (0, 0))
    na = len(adds) + len(dy_more)

    def body(*refs):
        x_ref, g_ref, dy_ref = refs[:3]
        add_refs = refs[3:3 + len(adds)]
        more_refs = refs[3 + len(adds):3 + na]
        dx_ref, dg_ref, acc_ref = refs[3 + na:]
        i = pl.program_id(0)
        xv = x_ref[...].astype(F32)
        r = lax.rsqrt(jnp.mean(xv * xv, axis=-1, keepdims=True) + NORM_EPS)
        xh = xv * r
        dyv = dy_ref[...].astype(F32)
        for m_ref in more_refs:
            dyv = dyv + m_ref[...].astype(F32)
        part = (dyv * xh).reshape(tr // 8, 8, d).sum(axis=0)

        @pl.when(i == 0)
        def _():
            acc_ref[...] = part

        @pl.when(i > 0)
        def _():
            acc_ref[...] += part

        t = dyv * g_ref[...]
        dx = r * (t - xh * jnp.mean(t * xh, axis=-1, keepdims=True))
        for a_ref in add_refs:
            dx = dx + a_ref[...].astype(F32)
        dx_ref[...] = dx.astype(dx_ref.dtype)

        @pl.when(i == steps - 1)
        def _():
            dg_ref[...] = jnp.sum(acc_ref[...], axis=0, keepdims=True)

    return pl.pallas_call(
        body, name=name, grid=(steps,), in_specs=[row, vec, row] + [row] * na,
        out_specs=(row, vec),
        out_shape=(jax.ShapeDtypeStruct((n, d), out_dtype), jax.ShapeDtypeStruct((1, d), F32)),
        scratch_shapes=[pltpu.VMEM((8, d), F32)], compiler_params=_params(("arbitrary",)),
    )(x, g, dy, *adds, *dy_more)


def _rms(xv, g):
    return xv * lax.rsqrt(jnp.mean(xv * xv, axis=-1, keepdims=True) + NORM_EPS) * g


def _post_norm_block(y, g, h_in, next_gains, *, name, tr=512):
    n, d = y.shape
    tr = _tile(n, tr)
    nk = len(next_gains)
    row = pl.BlockSpec((tr, d), lambda i: (i, 0))
    vec = pl.BlockSpec((1, d), lambda i: (0, 0))

    def body(*refs):
        y_ref, g_ref, h_ref = refs[:3]
        gk_refs = refs[3:3 + nk]
        o_ref = refs[3 + nk]
        hn_refs = refs[4 + nk:]
        h = h_ref[...] + _rms(y_ref[...], g_ref[...])
        o_ref[...] = h
        for gk_ref, hn_ref in zip(gk_refs, hn_refs):
            hn_ref[...] = _rms(h, gk_ref[...]).astype(BF16)

    return pl.pallas_call(
        body, name=name, grid=(n // tr,), in_specs=[row, vec, row] + [vec] * nk,
        out_specs=(row,) * (1 + nk),
        out_shape=(jax.ShapeDtypeStruct((n, d), F32),) + (jax.ShapeDtypeStruct((n, d), BF16),) * nk,
        compiler_params=_params(("parallel",)),
    )(y, g, h_in, *next_gains)


def _post_norm_loss(y, g, h_in, target, *, tr=512):
    n, d = y.shape
    tr = _tile(n, tr)
    steps = n // tr
    row = pl.BlockSpec((tr, d), lambda i: (i, 0))

    def body(y_ref, g_ref, h_ref, t_ref, dh_ref, loss_ref, acc_ref):
        i = pl.program_id(0)
        e = h_ref[...] + _rms(y_ref[...], g_ref[...]) - t_ref[...]
        dh_ref[...] = e / d
        part = (e * e).reshape(tr // 8, 8, d).sum(axis=0)

        @pl.when(i == 0)
        def _():
            acc_ref[...] = part

        @pl.when(i > 0)
        def _():
            acc_ref[...] += part

        @pl.when(i == steps - 1)
        def _():
            s = jnp.sum(jnp.sum(acc_ref[...], axis=-1, keepdims=True), axis=0, keepdims=True)
            loss_ref[...] = 0.5 * s / d

    return pl.pallas_call(
        body, name="b_post_norm_loss", grid=(steps,),
        in_specs=[row, pl.BlockSpec((1, d), lambda i: (0, 0)), row, row],
        out_specs=(row, pl.BlockSpec((1, 1), lambda i: (0, 0))),
        out_shape=(jax.ShapeDtypeStruct((n, d), F32), jax.ShapeDtypeStruct((1, 1), F32)),
        scratch_shapes=[pltpu.VMEM((8, d), F32)], compiler_params=_params(("arbitrary",)),
    )(y, g, h_in, target)


def _rms_bwd_pair(x, g1, dy1, g2, dy2, add, *, name, tr=512):
    n, d = x.shape
    tr = _tile(n, tr)
    steps = n // tr
    row = pl.BlockSpec((tr, d), lambda i: (i, 0))
    vec = pl.BlockSpec((1, d), lambda i: (0, 0))

    def body(x_ref, g1_ref, d1_ref, g2_ref, d2_ref, add_ref, dx_ref, dg1_ref, dg2_ref, acc_ref):
        i = pl.program_id(0)
        xv = x_ref[...]
        r = lax.rsqrt(jnp.mean(xv * xv, axis=-1, keepdims=True) + NORM_EPS)
        xh = xv * r
        dx = add_ref[...]
        for k, (g_ref, d_ref) in enumerate(((g1_ref, d1_ref), (g2_ref, d2_ref))):
            dyv = d_ref[...].astype(F32)
            part = (dyv * xh).reshape(tr // 8, 8, d).sum(axis=0)

            @pl.when(i == 0)
            def _(part=part, k=k):
                acc_ref[k] = part

            @pl.when(i > 0)
            def _(part=part, k=k):
                acc_ref[k] += part

            t = dyv * g_ref[...]
            dx = dx + r * (t - xh * jnp.mean(t * xh, axis=-1, keepdims=True))
        dx_ref[...] = dx

        @pl.when(i == steps - 1)
        def _():
            dg1_ref[...] = jnp.sum(acc_ref[0], axis=0, keepdims=True)
            dg2_ref[...] = jnp.sum(acc_ref[1], axis=0, keepdims=True)

    return pl.pallas_call(
        body, name=name, grid=(steps,), in_specs=[row, vec, row, vec, row, row],
        out_specs=(row, vec, vec),
        out_shape=(jax.ShapeDtypeStruct((n, d), F32), jax.ShapeDtypeStruct((1, d), F32),
                   jax.ShapeDtypeStruct((1, d), F32)),
        scratch_shapes=[pltpu.VMEM((2, 8, d), F32)], compiler_params=_params(("arbitrary",)),
    )(x, g1, dy1, g2, dy2, add)


def _kv_latent_fwd(ckr, g_lat, tabs, *, tr=512):
    n = ckr.shape[0]
    tr = _tile(n, tr)
    lat = B_KV_LORA

    def body(c_ref, k_ref, g_ref, tc, tsa, tsb, ckv_ref, kr_ref):
        xv = c_ref[...]
        r = lax.rsqrt(jnp.mean(xv * xv, axis=-1, keepdims=True) + NORM_EPS)
        ckv_ref[...] = (xv * r * g_ref[...]).astype(BF16)
        kr_ref[...] = _rope_apply(k_ref[...], tc[...], tsa[...], tsb[...], 1).astype(BF16)

    tab = pl.BlockSpec((tr, LANES), lambda i: (i, 0))
    return pl.pallas_call(
        body, name="kv_latent_fwd", grid=(n // tr,),
        in_specs=[pl.BlockSpec((tr, lat), lambda i: (i, 0)),
                  pl.BlockSpec((tr, LANES), lambda i: (i, lat // LANES)),
                  pl.BlockSpec((1, lat), lambda i: (0, 0)), tab, tab, tab],
        out_specs=(pl.BlockSpec((tr, lat), lambda i: (i, 0)), tab),
        out_shape=(jax.ShapeDtypeStruct((n, lat), BF16), jax.ShapeDtypeStruct((n, LANES), BF16)),
        compiler_params=_params(("parallel",)),
    )(ckr, ckr, g_lat, *tabs)


def _kv_latent_bwd(dckv, ckr, g_lat, dk_cat, tabs, *, tr=512):
    n = ckr.shape[0]
    tr = _tile(n, tr)
    steps = n // tr
    lat = B_KV_LORA
    wk = dk_cat.shape[1]

    def body(d_ref, c_ref, g_ref, dk_ref, tc, tsa, tsb, o_ref, dg_ref, acc_ref):
        i = pl.program_id(0)
        xv = c_ref[...]
        r = lax.rsqrt(jnp.mean(xv * xv, axis=-1, keepdims=True) + NORM_EPS)
        xh = xv * r
        dyv = d_ref[...]
        part = (dyv * xh).reshape(tr // 8, 8, lat).sum(axis=0)

        @pl.when(i == 0)
        def _():
            acc_ref[...] = part

        @pl.when(i > 0)
        def _():
            acc_ref[...] += part

        t = dyv * g_ref[...]
        dx = r * (t - xh * jnp.mean(t * xh, axis=-1, keepdims=True))
        o_ref[:, 0:lat] = dx.astype(o_ref.dtype)
        dkr = dk_ref[:, 0:LANES].astype(F32)
        for h in range(1, wk // LANES):
            dkr = dkr + dk_ref[:, h * LANES:(h + 1) * LANES].astype(F32)
        o_ref[:, lat:lat + LANES] = _rope_apply(dkr, tc[...], tsa[...], tsb[...], -1).astype(o_ref.dtype)

        @pl.when(i == steps - 1)
        def _():
            dg_ref[...] = jnp.sum(acc_ref[...], axis=0, keepdims=True)

    tab = pl.BlockSpec((tr, LANES), lambda i: (i, 0))
    return pl.pallas_call(
        body, name="kv_latent_bwd", grid=(steps,),
        in_specs=[pl.BlockSpec((tr, lat), lambda i: (i, 0)), pl.BlockSpec((tr, lat), lambda i: (i, 0)),
                  pl.BlockSpec((1, lat), lambda i: (0, 0)), pl.BlockSpec((tr, wk), lambda i: (i, 0)),
                  tab, tab, tab],
        out_specs=(pl.BlockSpec((tr, lat + LANES), lambda i: (i, 0)), pl.BlockSpec((1, lat), lambda i: (0, 0))),
        out_shape=(jax.ShapeDtypeStruct((n, lat + LANES), BF16), jax.ShapeDtypeStruct((1, lat), F32)),
        scratch_shapes=[pltpu.VMEM((8, lat), F32)], compiler_params=_params(("arbitrary",)),
    )(dckv, ckr, g_lat, dk_cat, *tabs)


def _sigmoid(z):
    return 1.0 / (1.0 + jnp.exp(-z))


def _lane_place(cols, width):
    rows = cols[0].shape[0]
    lane = lax.broadcasted_iota(jnp.int32, (rows, width), 1)
    out = jnp.zeros((rows, width), F32)
    for h, col in enumerate(cols):
        out = jnp.where(lane == h, col, out)
    return out


def _merge_gate_fwd(outs, lses, proj, z_block, *, tr=256):
    n, w = outs[0].shape
    tr = _tile(n, tr)
    ng = len(outs)

    def body(*refs):
        o_refs = refs[:ng]
        l_refs = refs[ng:2 * ng]
        z_ref = refs[2 * ng]
        y_ref, om_ref, lse_ref = refs[2 * ng + 1:]
        ls = [r[...] for r in l_refs]
        mx = ls[0]
        for l in ls[1:]:
            mx = jnp.maximum(mx, l)
        ssum = jnp.exp2(ls[0] - mx)
        for l in ls[1:]:
            ssum = ssum + jnp.exp2(l - mx)
        tot = mx + jnp.log2(ssum)
        lse_ref[...] = tot
        ws = [jnp.exp2(l - tot) for l in ls]
        for h in range(A_HEADS):
            sl = slice(h * A_HEAD_DIM, (h + 1) * A_HEAD_DIM)
            o = ws[0][:, h:h + 1] * o_refs[0][:, sl]
            for gi in range(1, ng):
                o = o + ws[gi][:, h:h + 1] * o_refs[gi][:, sl]
            z = z_ref[:, sl].astype(F32)
            om_ref[:, sl] = o.astype(BF16)
            y_ref[:, sl] = (o * (z * _sigmoid(z))).astype(BF16)

    row = pl.BlockSpec((tr, w), lambda i: (i, 0))
    lrow = pl.BlockSpec((tr, A_HEADS), lambda i: (i, 0))
    return pl.pallas_call(
        body, name="merge_gate_fwd", grid=(n // tr,),
        in_specs=[row] * ng + [lrow] * ng + [pl.BlockSpec((tr, w), lambda i: (i, z_block))],
        out_specs=(row, row, lrow),
        out_shape=(jax.ShapeDtypeStruct((n, w), BF16), jax.ShapeDtypeStruct((n, w), BF16),
                   jax.ShapeDtypeStruct((n, A_HEADS), F32)),
        compiler_params=_params(("parallel",)),
    )(*outs, *lses, proj)


def _gate_bwd(dy, o, z_arr, z_block, *, name, with_delta, tr=256):
    n, w = dy.shape
    tr = _tile(n, tr)

    def body(*refs):
        dy_ref, o_ref, z_ref, do_ref, dz_ref = refs[:5]
        dyv = dy_ref[...].astype(F32)
        ov = o_ref[...].astype(F32)
        z = z_ref[...].astype(F32)
        sig = _sigmoid(z)
        do = dyv * (z * sig)
        do_ref[...] = do.astype(BF16)
        dz_ref[...] = (dyv * ov * (sig * (1.0 + z * (1.0 - sig)))).astype(BF16)
        if with_delta:
            prod = do * ov
            cols = [jnp.sum(prod[:, h * A_HEAD_DIM:(h + 1) * A_HEAD_DIM], axis=-1, keepdims=True)
                    for h in range(A_HEADS)]
            refs[5][...] = _lane_place(cols, A_HEADS)

    row = pl.BlockSpec((tr, w), lambda i: (i, 0))
    out_specs = [row, row]
    out_shape = [jax.ShapeDtypeStruct((n, w), BF16), jax.ShapeDtypeStruct((n, w), BF16)]
    if with_delta:
        out_specs.append(pl.BlockSpec((tr, A_HEADS), lambda i: (i, 0)))
        out_shape.append(jax.ShapeDtypeStruct((n, A_HEADS), F32))
    return pl.pallas_call(
        body, name=name, grid=(n // tr,),
        in_specs=[row, row, pl.BlockSpec((tr, w), lambda i: (i, z_block))],
        out_specs=tuple(out_specs), out_shape=tuple(out_shape), compiler_params=_params(("parallel",)),
    )(dy, o, z_arr)


def _loss_fwd_bwd(h, target, *, tr=512):
    n, d = h.shape
    tr = _tile(n, tr)
    steps = n // tr

    def body(h_ref, t_ref, dh_ref, loss_ref, acc_ref):
        i = pl.program_id(0)
        e = h_ref[...] - t_ref[...]
        dh_ref[...] = e / d
        part = (e * e).reshape(tr // 8, 8, d).sum(axis=0)

        @pl.when(i == 0)
        def _():
            acc_ref[...] = part

        @pl.when(i > 0)
        def _():
            acc_ref[...] += part

        @pl.when(i == steps - 1)
        def _():
            s = jnp.sum(jnp.sum(acc_ref[...], axis=-1, keepdims=True), axis=0, keepdims=True)
            loss_ref[...] = 0.5 * s / d

    row = pl.BlockSpec((tr, d), lambda i: (i, 0))
    return pl.pallas_call(
        body, name="loss", grid=(steps,), in_specs=[row, row],
        out_specs=(row, pl.BlockSpec((1, 1), lambda i: (0, 0))),
        out_shape=(jax.ShapeDtypeStruct((n, d), F32), jax.ShapeDtypeStruct((1, 1), F32)),
        scratch_shapes=[pltpu.VMEM((8, d), F32)], compiler_params=_params(("arbitrary",)),
    )(h, target)


def _dot_nt(a, b):
    return lax.dot_general(a, b, (((1,), (1,)), ((), ())), preferred_element_type=F32)


def _dot_nn(a, b):
    return lax.dot_general(a, b, (((1,), (0,)), ((), ())), preferred_element_type=F32)


def _attn_a_fwd(qkv, cb0, qb, out_dtype, *, name):
    bl, dil, ln, _ = qkv.shape
    nb = ln // qb
    hw = A_WIDTH
    heads = range(A_HEADS)
    sls = [slice(h * A_HEAD_DIM, (h + 1) * A_HEAD_DIM) for h in heads]

    def body(*refs):
        if nb > 1:
            q_ref, kc_ref, vc_ref, kp_ref, vp_ref, o_ref, lse_ref = refs
        else:
            q_ref, kc_ref, vc_ref, o_ref, lse_ref = refs
        i = pl.program_id(2)
        qi = lax.broadcasted_iota(jnp.int32, (qb, qb), 0)
        ki = lax.broadcasted_iota(jnp.int32, (qb, qb), 1)
        mask_c = ki <= qi
        mask_p = jnp.logical_and(ki >= qi, i >= 1)
        s_c = [jnp.where(mask_c, _dot_nt(q_ref[:, sls[h]], kc_ref[:, sls[h]]), NEG) for h in heads]
        m = [jnp.max(s_c[h], axis=-1, keepdims=True) for h in heads]
        if nb > 1:
            s_p = [jnp.where(mask_p, _dot_nt(q_ref[:, sls[h]], kp_ref[:, sls[h]]), NEG) for h in heads]
            m = [jnp.maximum(m[h], jnp.max(s_p[h], axis=-1, keepdims=True)) for h in heads]
        p_c = [jnp.exp2(s_c[h] - m[h]) for h in heads]
        l = [jnp.sum(p_c[h], axis=-1, keepdims=True) for h in heads]
        acc = [_dot_nn(p_c[h].astype(BF16), vc_ref[:, sls[h]]) for h in heads]
        if nb > 1:
            p_p = [jnp.exp2(s_p[h] - m[h]) for h in heads]
            l = [l[h] + jnp.sum(p_p[h], axis=-1, keepdims=True) for h in heads]
            acc = [acc[h] + _dot_nn(p_p[h].astype(BF16), vp_ref[:, sls[h]]) for h in heads]
        for h in heads:
            o_ref[:, sls[h]] = (acc[h] / l[h]).astype(o_ref.dtype)
        lse_ref[...] = _lane_place([m[h] + jnp.log2(l[h]) for h in heads], A_HEADS)

    def spec(off, prev):
        if prev:
            return pl.BlockSpec((None, None, qb, hw), lambda b, r, i: (b, r, jnp.maximum(i - 1, 0), cb0 + off))
        return pl.BlockSpec((None, None, qb, hw), lambda b, r, i: (b, r, i, cb0 + off))

    return pl.pallas_call(
        body, name=name, grid=(bl, dil, nb),
        in_specs=[spec(0, False), spec(1, False), spec(2, False)] + ([spec(1, True), spec(2, True)] if nb > 1 else []),
        out_specs=(pl.BlockSpec((None, None, qb, hw), lambda b, r, i: (b, r, i, 0)),
                   pl.BlockSpec((None, None, qb, A_HEADS), lambda b, r, i: (b, r, i, 0))),
        out_shape=(jax.ShapeDtypeStruct((bl, dil, ln, hw), out_dtype),
                   jax.ShapeDtypeStruct((bl, dil, ln, A_HEADS), F32)),
        compiler_params=_params(("parallel", "parallel", "arbitrary")),
    )(*([qkv] * (5 if nb > 1 else 3)))


def _attn_a_bwd(qkv, cb0, do, lse, delta, lse_t, delta_t, tabs, qb, *, name):
    bl, dil, ln, _ = qkv.shape
    nb = ln // qb
    hw = A_WIDTH

    def body(*refs):
        if nb > 1:
            (q_ref, kc_ref, vc_ref, do_ref, lse_ref, dl_ref, lt_ref, dt_ref, tc, tsa, tsb,
             qn_ref, kp_ref, vp_ref, don_ref, ltn_ref, dtn_ref, o_ref) = refs
        else:
            q_ref, kc_ref, vc_ref, do_ref, lse_ref, dl_ref, lt_ref, dt_ref, tc, tsa, tsb, o_ref = refs
        i = pl.program_id(2)
        row = lax.broadcasted_iota(jnp.int32, (qb, qb), 0)
        col = lax.broadcasted_iota(jnp.int32, (qb, qb), 1)
        m_qc = col <= row
        m_kc = row <= col
        m_qp = jnp.logical_and(col >= row, i >= 1)
        m_kn = jnp.logical_and(row >= col, i + 1 < nb)
        c, sa, sb = tc[...], tsa[...], tsb[...]
        heads = range(A_HEADS)
        sls = [slice(h * A_HEAD_DIM, (h + 1) * A_HEAD_DIM) for h in heads]
        q, kc = [q_ref[:, sl] for sl in sls], [kc_ref[:, sl] for sl in sls]
        vc, dov = [vc_ref[:, sl] for sl in sls], [do_ref[:, sl] for sl in sls]
        lse_c = [lse_ref[:, h:h + 1] for h in heads]
        dl_c = [dl_ref[:, h:h + 1] for h in heads]
        s = [_dot_nt(q[h], kc[h]) for h in heads]
        st = [_dot_nt(kc[h], q[h]) for h in heads]
        dp = [_dot_nt(dov[h], vc[h]) for h in heads]
        dpt = [_dot_nt(vc[h], dov[h]) for h in heads]
        p = [jnp.exp2(jnp.where(m_qc, s[h], NEG) - lse_c[h]) for h in heads]
        pt = [jnp.exp2(jnp.where(m_kc, st[h], NEG) - lt_ref[h:h + 1, :]) for h in heads]
        dq = [_dot_nn((p[h] * (dp[h] - dl_c[h])).astype(BF16), kc[h]) for h in heads]
        dk = [_dot_nn((pt[h] * (dpt[h] - dt_ref[h:h + 1, :])).astype(BF16), q[h]) for h in heads]
        dv = [_dot_nn(pt[h].astype(BF16), dov[h]) for h in heads]
        if nb > 1:
            kp, vp = [kp_ref[:, sl] for sl in sls], [vp_ref[:, sl] for sl in sls]
            qn, don = [qn_ref[:, sl] for sl in sls], [don_ref[:, sl] for sl in sls]
            s = [_dot_nt(q[h], kp[h]) for h in heads]
            st = [_dot_nt(kc[h], qn[h]) for h in heads]
            dp = [_dot_nt(dov[h], vp[h]) for h in heads]
            dpt = [_dot_nt(vc[h], don[h]) for h in heads]
            p = [jnp.exp2(jnp.where(m_qp, s[h], NEG) - lse_c[h]) for h in heads]
            pt = [jnp.exp2(jnp.where(m_kn, st[h], NEG) - ltn_ref[h:h + 1, :]) for h in heads]
            dq = [dq[h] + _dot_nn((p[h] * (dp[h] - dl_c[h])).astype(BF16), kp[h]) for h in heads]
            dk = [dk[h] + _dot_nn((pt[h] * (dpt[h] - dtn_ref[h:h + 1, :])).astype(BF16), qn[h]) for h in heads]
            dv = [dv[h] + _dot_nn(pt[h].astype(BF16), don[h]) for h in heads]
        for h in heads:
            o_ref[:, h * A_HEAD_DIM:(h + 1) * A_HEAD_DIM] = _rope_apply(dq[h] * A_SCALE, c, sa, sb, -1).astype(BF16)
            o_ref[:, hw + h * A_HEAD_DIM:hw + (h + 1) * A_HEAD_DIM] = _rope_apply(dk[h] * LN2, c, sa, sb, -1).astype(BF16)
            o_ref[:, 2 * hw + h * A_HEAD_DIM:2 * hw + (h + 1) * A_HEAD_DIM] = dv[h].astype(BF16)

    def cur(w, col):
        return pl.BlockSpec((None, None, qb, w), lambda b, r, i: (b, r, i, col))

    def prev(w, col):
        return pl.BlockSpec((None, None, qb, w), lambda b, r, i: (b, r, jnp.maximum(i - 1, 0), col))

    def nxt(w, col):
        return pl.BlockSpec((None, None, qb, w), lambda b, r, i: (b, r, jnp.minimum(i + 1, nb - 1), col))

    t_cur = pl.BlockSpec((None, None, A_HEADS, qb), lambda b, r, i: (b, r, 0, i))
    t_nxt = pl.BlockSpec((None, None, A_HEADS, qb), lambda b, r, i: (b, r, 0, jnp.minimum(i + 1, nb - 1)))
    in_specs = [cur(hw, cb0), cur(hw, cb0 + 1), cur(hw, cb0 + 2), cur(hw, 0), cur(A_HEADS, 0), cur(A_HEADS, 0),
                t_cur, t_cur, cur(LANES, 0), cur(LANES, 0), cur(LANES, 0)]
    operands = [qkv, qkv, qkv, do, lse, delta, lse_t, delta_t, *tabs]
    if nb > 1:
        in_specs += [nxt(hw, cb0), prev(hw, cb0 + 1), prev(hw, cb0 + 2), nxt(hw, 0), t_nxt, t_nxt]
        operands += [qkv, qkv, qkv, do, lse_t, delta_t]
    return pl.pallas_call(
        body, name=name, grid=(bl, dil, nb), in_specs=in_specs, out_specs=cur(3 * hw, 0),
        out_shape=jax.ShapeDtypeStruct((bl, dil, ln, 3 * hw), BF16),
        compiler_params=_params(("parallel", "parallel", "arbitrary")),
    )(*operands)


def _head_terms(do, o, lse, e):
    rows = do.shape[0]
    lane = lax.broadcasted_iota(jnp.int32, (rows, LANES), 1)
    mine = (lane < B_VDIM) if e == 0 else (lane >= B_VDIM)
    prod = do.astype(F32) * o.astype(F32)
    dl = jnp.sum(jnp.where(mine, prod, 0.0), axis=-1, keepdims=True)
    do_e = jnp.where(mine, do, jnp.zeros_like(do))
    return do_e, dl, lse[:, e * B_VDIM:e * B_VDIM + 1]


def _col_to_row(col, rows):
    return jnp.transpose(jnp.broadcast_to(col, (rows, LANES)))[0:1, :]


def _mla_fwd(q_cat, kvup, kr, z, tq):
    bl, t, _ = q_cat.shape
    nq = t // tq
    pairs = B_HEADS // 2
    v_blk0 = (B_HEADS * LANES) // LANES

    def body(q_ref, k_ref, v_ref, kr_ref, z_ref, y_ref, o_ref, lse_ref, lrow_ref, m_ref, acc_ref):
        qi = pl.program_id(2)
        qs = [q_ref[:, e * LANES:(e + 1) * LANES] for e in range(2)]
        row = lax.broadcasted_iota(jnp.int32, (tq, tq), 0)
        col = lax.broadcasted_iota(jnp.int32, (tq, tq), 1)
        tri = col <= row
        sum_lane = [B_VDIM, 0]

        for e in range(2):
            m_ref[e] = jnp.full((tq, LANES), NEG, F32)
            acc_ref[e] = jnp.zeros((tq, LANES), F32)

        def tile(k0, w, masked):
            lane = lax.broadcasted_iota(jnp.int32, (w, LANES), 1)
            first = lane < B_VDIM
            krv = kr_ref[pl.ds(k0, w), :]
            v = v_ref[pl.ds(k0, w), :]
            vs = [jnp.where(first, v, jnp.where(lane == B_VDIM, 1.0, 0.0).astype(BF16)),
                  jnp.where(first, jnp.where(lane == 0, 1.0, 0.0).astype(BF16), v)]
            ss = []
            for e in range(2):
                k = k_ref[pl.ds(k0, w), e * LANES:(e + 1) * LANES] + krv
                s = _dot_nt(qs[e], k)
                if masked:
                    r = lax.broadcasted_iota(jnp.int32, (tq, w), 0)
                    c = lax.broadcasted_iota(jnp.int32, (tq, w), 1)
                    s = jnp.where(c <= r + (w - tq), s, NEG)
                ss.append(s)
            for e in range(2):
                m_old = m_ref[e]
                m_new = jnp.maximum(m_old, jnp.max(ss[e], axis=-1, keepdims=True))
                p = jnp.exp2(ss[e] - jnp.concatenate([m_new] * (w // LANES), axis=1)).astype(BF16)
                m_ref[e] = m_new
                acc_ref[e] = jnp.exp2(m_old - m_new) * acc_ref[e] + _dot_nn(p, vs[e])

        def step(kb2, carry):
            tile(pl.multiple_of(kb2 * 2 * tq, 2 * tq), 2 * tq, False)
            return carry

        lax.fori_loop(0, qi // 2, step, 0)

        @pl.when(qi % 2 == 1)
        def _():
            tile(pl.multiple_of((qi - 1) * tq, tq), 2 * tq, True)

        @pl.when(qi % 2 == 0)
        def _():
            tile(pl.multiple_of(qi * tq, tq), tq, True)
        lane = lax.broadcasted_iota(jnp.int32, (tq, LANES), 1)
        first = lane < B_VDIM
        accs = [acc_ref[e] for e in range(2)]
        ls = [accs[e][:, sum_lane[e]:sum_lane[e] + 1] for e in range(2)]
        outs = [accs[e] / ls[e] for e in range(2)]
        lses = [m_ref[e] + jnp.log2(ls[e]) for e in range(2)]
        o = jnp.where(first, outs[0], outs[1])
        zv = z_ref[...].astype(F32)
        o_ref[...] = o.astype(BF16)
        y_ref[...] = (o * (zv * _sigmoid(zv))).astype(BF16)
        lse_ref[...] = jnp.where(first, lses[0], lses[1])
        for e in range(2):
            lrow_ref[e:e + 1, :] = jnp.transpose(lses[e])[0:1, :]

    blk = pl.BlockSpec((None, tq, LANES), lambda b, j, i: (b, i, j))
    return pl.pallas_call(
        body, name="mla_fwd", grid=(bl, pairs, nq),
        in_specs=[pl.BlockSpec((None, tq, 2 * LANES), lambda b, j, i: (b, i, j)),
                  pl.BlockSpec((None, t, 2 * LANES), lambda b, j, i: (b, 0, j)),
                  pl.BlockSpec((None, t, LANES), lambda b, j, i: (b, 0, v_blk0 + j)),
                  pl.BlockSpec((None, t, LANES), lambda b, j, i: (b, 0, 0)),
                  blk],
        out_specs=(blk, blk, blk, pl.BlockSpec((None, None, None, 2, tq), lambda b, j, i: (b, j, i, 0, 0))),
        out_shape=(jax.ShapeDtypeStruct((bl, t, B_WIDTH), BF16), jax.ShapeDtypeStruct((bl, t, B_WIDTH), BF16),
                   jax.ShapeDtypeStruct((bl, t, B_WIDTH), F32),
                   jax.ShapeDtypeStruct((bl, pairs, nq, 2, tq), F32)),
        scratch_shapes=[pltpu.VMEM((2, tq, LANES), F32), pltpu.VMEM((2, tq, LANES), F32)],
        compiler_params=_params(("parallel", "parallel", "arbitrary")),
    )(q_cat, kvup, kvup, kr, z)


def _mla_fwd_km(q_cat, kvup, kr, vt, z, tq):
    bl, t, _ = q_cat.shape
    nq = t // tq
    pairs = B_HEADS // 2

    def body(q_ref, k_ref, vt_ref, kr_ref, z_ref, y_ref, o_ref, lse_ref, lrow_ref, m_ref, acc_ref):
        qi = pl.program_id(2)
        qs = [q_ref[:, e * LANES:(e + 1) * LANES] for e in range(2)]
        krow = lax.broadcasted_iota(jnp.int32, (tq, tq), 0)
        qcol = lax.broadcasted_iota(jnp.int32, (tq, tq), 1)
        tri = krow <= qcol
        sub = lax.broadcasted_iota(jnp.int32, (LANES, tq), 0)
        first = sub < B_VDIM
        ones_row = [jnp.where(sub == B_VDIM, 1.0, 0.0).astype(BF16), jnp.where(sub == 0, 1.0, 0.0).astype(BF16)]
        sum_row = [B_VDIM, 0]
        for e in range(2):
            m_ref[e] = jnp.full((8, tq), NEG, F32)
            acc_ref[e] = jnp.zeros((LANES, tq), F32)

        def tile(kb, nblk, masked):
            w = nblk * tq
            k0 = pl.multiple_of(kb * tq, tq)
            krv = kr_ref[pl.ds(k0, w), :]
            sts = []
            for e in range(2):
                k = k_ref[pl.ds(k0, w), e * LANES:(e + 1) * LANES] + krv
                st = _dot_nt(k, qs[e])
                sts.append(jnp.where(tri, st, NEG) if masked else st)
            m_old = [m_ref[e] for e in range(2)]
            ms = [jnp.maximum(m_old[e], jnp.max(sts[e], axis=0, keepdims=True)) for e in range(2)]
            pts = [jnp.exp2(sts[e] - ms[e][0:1, :]).astype(BF16) for e in range(2)]
            for e in range(2):
                pv = None
                for i in range(nblk):
                    vtb = vt_ref[kb + i]
                    vte = jnp.where(first, vtb, ones_row[0]) if e == 0 else jnp.where(first, ones_row[1], vtb)
                    part = _dot_nn(vte, pts[e][i * tq:(i + 1) * tq, :])
                    pv = part if pv is None else pv + part
                m_ref[e] = ms[e]
                acc_ref[e] = jnp.exp2(m_old[e] - ms[e])[0:1, :] * acc_ref[e] + pv

        def step(kb2, carry):
            tile(2 * kb2, 2, False)
            return carry

        lax.fori_loop(0, qi // 2, step, 0)

        @pl.when(qi % 2 == 1)
        def _():
            tile(qi - 1, 1, False)

        tile(qi, 1, True)
        accs = [acc_ref[e] for e in range(2)]
        ls = [accs[e][sum_row[e]:sum_row[e] + 1, :] for e in range(2)]
        lses = [m_ref[e][0:1, :] + jnp.log2(ls[e]) for e in range(2)]
        o = jnp.transpose(jnp.where(first, accs[0] / ls[0], accs[1] / ls[1]))
        zv = z_ref[...].astype(F32)
        o_ref[...] = o.astype(BF16)
        y_ref[...] = (o * (zv * _sigmoid(zv))).astype(BF16)
        lse_ref[...] = jnp.transpose(jnp.where(first, jnp.broadcast_to(lses[0], (LANES, tq)),
                                               jnp.broadcast_to(lses[1], (LANES, tq))))
        for e in range(2):
            lrow_ref[e:e + 1, :] = lses[e]

    blk = pl.BlockSpec((None, tq, LANES), lambda b, j, i: (b, i, j))
    return pl.pallas_call(
        body, name="mla_fwd", grid=(bl, pairs, nq),
        in_specs=[pl.BlockSpec((None, tq, 2 * LANES), lambda b, j, i: (b, i, j)),
                  pl.BlockSpec((None, t, 2 * LANES), lambda b, j, i: (b, 0, j)),
                  pl.BlockSpec((None, nq, LANES, tq), lambda b, j, i: (b, 0, j, 0)),
                  pl.BlockSpec((None, t, LANES), lambda b, j, i: (b, 0, 0)),
                  blk],
        out_specs=(blk, blk, blk, pl.BlockSpec((None, None, None, 2, tq), lambda b, j, i: (b, j, i, 0, 0))),
        out_shape=(jax.ShapeDtypeStruct((bl, t, B_WIDTH), BF16), jax.ShapeDtypeStruct((bl, t, B_WIDTH), BF16),
                   jax.ShapeDtypeStruct((bl, t, B_WIDTH), F32),
                   jax.ShapeDtypeStruct((bl, pairs, nq, 2, tq), F32)),
        scratch_shapes=[pltpu.VMEM((2, 8, tq), F32), pltpu.VMEM((2, LANES, tq), F32)],
        compiler_params=_params(("parallel", "parallel", "arbitrary")),
    )(q_cat, kvup, vt, kr, z)


def _mla_dq(q_cat, kvup, kr, do, o, lse, tabs, tq):
    bl, t, _ = q_cat.shape
    nq = t // tq
    pairs = B_HEADS // 2
    v_blk0 = (B_HEADS * LANES) // LANES

    def body(q_ref, k_ref, v_ref, kr_ref, do_ref, o_ref, lse_ref, tc, tsa, tsb, dq_ref, drow_ref, acc_ref):
        qi = pl.program_id(2)
        dov, ov, lsev = do_ref[...], o_ref[...], lse_ref[...]
        qs = [q_ref[:, e * LANES:(e + 1) * LANES] for e in range(2)]
        terms = [_head_terms(dov, ov, lsev, e) for e in range(2)]
        row = lax.broadcasted_iota(jnp.int32, (tq, tq), 0)
        col = lax.broadcasted_iota(jnp.int32, (tq, tq), 1)
        tri = col <= row
        for e in range(2):
            acc_ref[e] = jnp.zeros((tq, LANES), F32)

        def tile(k0, w, masked):
            krv = kr_ref[pl.ds(k0, w), :]
            v = v_ref[pl.ds(k0, w), :]
            ks = [k_ref[pl.ds(k0, w), e * LANES:(e + 1) * LANES] + krv for e in range(2)]
            ss = [_dot_nt(qs[e], ks[e]) for e in range(2)]
            dps = [_dot_nt(terms[e][0], v) for e in range(2)]
            for e in range(2):
                s = ss[e]
                if masked:
                    r = lax.broadcasted_iota(jnp.int32, (tq, w), 0)
                    c = lax.broadcasted_iota(jnp.int32, (tq, w), 1)
                    s = jnp.where(c <= r + (w - tq), s, NEG)
                p = jnp.exp2(s - terms[e][2])
                ds = (p * (dps[e] - terms[e][1])).astype(BF16)
                acc_ref[e] += _dot_nn(ds, ks[e])

        def step(kb2, carry):
            tile(pl.multiple_of(kb2 * 2 * tq, 2 * tq), 2 * tq, False)
            return carry

        lax.fori_loop(0, qi // 2, step, 0)

        @pl.when(qi % 2 == 1)
        def _():
            tile(pl.multiple_of((qi - 1) * tq, tq), 2 * tq, True)

        @pl.when(qi % 2 == 0)
        def _():
            tile(pl.multiple_of(qi * tq, tq), tq, True)

        for e in range(2):
            dq_ref[:, e * LANES:(e + 1) * LANES] = _rope_apply(acc_ref[e] * B_SCALE, tc[...], tsa[...], tsb[...], -1).astype(BF16)
            drow_ref[e:e + 1, :] = _col_to_row(terms[e][1], tq)

    blk = pl.BlockSpec((None, tq, LANES), lambda b, j, i: (b, i, j))
    tab = pl.BlockSpec((None, tq, LANES), lambda b, j, i: (b, i, 0))
    qblk = pl.BlockSpec((None, tq, 2 * LANES), lambda b, j, i: (b, i, j))
    return pl.pallas_call(
        body, name="mla_dq", grid=(bl, pairs, nq),
        in_specs=[qblk,
                  pl.BlockSpec((None, t, 2 * LANES), lambda b, j, i: (b, 0, j)),
                  pl.BlockSpec((None, t, LANES), lambda b, j, i: (b, 0, v_blk0 + j)),
                  pl.BlockSpec((None, t, LANES), lambda b, j, i: (b, 0, 0)),
                  blk, blk, blk, tab, tab, tab],
        out_specs=(qblk, pl.BlockSpec((None, None, None, 2, tq), lambda b, j, i: (b, j, i, 0, 0))),
        out_shape=(jax.ShapeDtypeStruct((bl, t, B_HEADS * LANES), BF16),
                   jax.ShapeDtypeStruct((bl, pairs, nq, 2, tq), F32)),
        scratch_shapes=[pltpu.VMEM((2, tq, LANES), F32)],
        compiler_params=_params(("parallel", "parallel", "arbitrary")),
    )(q_cat, kvup, kvup, kr, do, o, lse, *tabs)


def _mla_dkv(q_cat, kvup, kr, do, lse_rows, delta_rows, tq):
    bl, t, _ = q_cat.shape
    nq = t // tq
    pairs = B_HEADS // 2
    v_blk0 = (B_HEADS * LANES) // LANES

    def body(q_ref, k_ref, v_ref, kr_ref, do_ref, lrow_ref, drow_ref, dk_ref, dv_ref, acc_ref):
        kb = pl.program_id(2)
        v = v_ref[...]
        krv = kr_ref[...]
        ks = [k_ref[:, e * LANES:(e + 1) * LANES] + krv for e in range(2)]
        krow = lax.broadcasted_iota(jnp.int32, (tq, tq), 0)
        qcol = lax.broadcasted_iota(jnp.int32, (tq, tq), 1)
        tri = krow <= qcol
        lane = lax.broadcasted_iota(jnp.int32, (tq, LANES), 1)
        mine = [lane < B_VDIM, lane >= B_VDIM]

        for e in range(3):
            acc_ref[e] = jnp.zeros((tq, LANES), F32)

        def tile(qb, nblk, masked):
            w = nblk * tq
            rows = pl.ds(pl.multiple_of(qb * tq, tq), w)
            dov = do_ref[rows, :]
            lane_w = lax.broadcasted_iota(jnp.int32, (w, LANES), 1)
            mine_w = [lane_w < B_VDIM, lane_w >= B_VDIM]
            qs = [q_ref[rows, e * LANES:(e + 1) * LANES] for e in range(2)]
            does = [jnp.where(mine_w[e], dov, jnp.zeros_like(dov)) for e in range(2)]
            sts = [_dot_nt(ks[e], qs[e]) for e in range(2)]
            dpts = [_dot_nt(v, does[e]) for e in range(2)]

            def rows_of(ref, e):
                return jnp.concatenate([ref[qb + i, e:e + 1, :] for i in range(nblk)], axis=1)

            pts = []
            for e in range(2):
                st = sts[e]
                if masked:
                    r = lax.broadcasted_iota(jnp.int32, (tq, w), 0)
                    c = lax.broadcasted_iota(jnp.int32, (tq, w), 1)
                    st = jnp.where(r <= c, st, NEG)
                pts.append(jnp.exp2(st - rows_of(lrow_ref, e)))
            acc_ref[2] += _dot_nn(pts[0].astype(BF16), does[0]) + _dot_nn(pts[1].astype(BF16), does[1])
            for e in range(2):
                dst = (pts[e] * (dpts[e] - rows_of(drow_ref, e))).astype(BF16)
                acc_ref[e] += _dot_nn(dst, qs[e])

        rest = nq - 1 - kb
        odd = rest % 2

        @pl.when(odd == 1)
        def _():
            tile(kb, 2, True)

        @pl.when(odd == 0)
        def _():
            tile(kb, 1, True)

        def step(i, carry):
            tile(kb + 1 + odd + 2 * i, 2, False)
            return carry

        lax.fori_loop(0, rest // 2, step, 0)
        dk_ref[:, 0:LANES] = (acc_ref[0] * LN2).astype(BF16)
        dk_ref[:, LANES:2 * LANES] = (acc_ref[1] * LN2).astype(BF16)
        dv_ref[...] = acc_ref[2].astype(BF16)

    full = pl.BlockSpec((None, t, LANES), lambda b, j, i: (b, 0, j))
    rows = pl.BlockSpec((None, None, nq, 2, tq), lambda b, j, i: (b, j, 0, 0, 0))
    kblk = pl.BlockSpec((None, tq, 2 * LANES), lambda b, j, i: (b, i, j))
    return pl.pallas_call(
        body, name="mla_dkv", grid=(bl, pairs, nq),
        in_specs=[pl.BlockSpec((None, t, 2 * LANES), lambda b, j, i: (b, 0, j)),
                  kblk,
                  pl.BlockSpec((None, tq, LANES), lambda b, j, i: (b, i, v_blk0 + j)),
                  pl.BlockSpec((None, tq, LANES), lambda b, j, i: (b, i, 0)),
                  full, rows, rows],
        out_specs=(kblk, pl.BlockSpec((None, tq, LANES), lambda b, j, i: (b, i, j))),
        out_shape=(jax.ShapeDtypeStruct((bl, t, B_HEADS * LANES), BF16),
                   jax.ShapeDtypeStruct((bl, t, B_WIDTH), BF16)),
        scratch_shapes=[pltpu.VMEM((3, tq, LANES), F32)],
        compiler_params=_params(("parallel", "parallel", "arbitrary")),
    )(q_cat, kvup, kvup, kr, do, lse_rows, delta_rows)


def _adamw(w, g, m, v, *, name):
    r, c = w.shape
    tr = _row_tile(r, 256)
    c1 = 1.0 - ADAM_B1
    c2 = 1.0 - ADAM_B2
    bc1 = 1.0 - ADAM_B1 ** ADAM_STEP
    bc2 = 1.0 - ADAM_B2 ** ADAM_STEP

    def body(w_ref, g_ref, m_ref, v_ref, d_ref, nm_ref, nv_ref):
        gv = g_ref[...]
        nm = ADAM_B1 * m_ref[...] + c1 * gv
        nv = ADAM_B2 * v_ref[...] + c2 * (gv * gv)
        nm_ref[...] = nm
        nv_ref[...] = nv
        d_ref[...] = -ADAM_LR * ((nm / bc1) / (jnp.sqrt(nv / bc2) + ADAM_EPS) + ADAM_WD * w_ref[...])

    blk = pl.BlockSpec((tr, c), lambda i: (i, 0))
    sds = jax.ShapeDtypeStruct((r, c), F32)
    return pl.pallas_call(
        body, name=name, grid=(r // tr,), in_specs=[blk] * 4, out_specs=(blk,) * 3,
        out_shape=(sds,) * 3, compiler_params=_params(("parallel",)),
    )(w, g, m, v)


def _add_my_half(stacked, other, core, out_dtype, *, name):
    nch, a, c = stacked.shape
    h = a // 2
    tr = _row_tile(h, 256)
    nblk = h // tr

    def body(core_ref, s_ref, p_ref, o_ref):
        o_ref[...] = (s_ref[...] + p_ref[...]).astype(o_ref.dtype)

    return pl.pallas_call(
        body, name=name,
        grid_spec=pltpu.PrefetchScalarGridSpec(
            num_scalar_prefetch=1, grid=(nch, nblk),
            in_specs=[pl.BlockSpec((None, tr, c), lambda k, i, cr: (k, cr[0] * nblk + i, 0)),
                      pl.BlockSpec((None, tr, c), lambda k, i, cr: (k, i, 0))],
            out_specs=pl.BlockSpec((None, tr, c), lambda k, i, cr: (k, i, 0))),
        out_shape=jax.ShapeDtypeStruct((nch, h, c), out_dtype),
        compiler_params=_params(("parallel", "parallel")),
    )(core, stacked, other)


def _sum_chips(parts, own, chip, *, name):
    nch, h, c = parts.shape
    tr = _row_tile(h, 256)

    def body(chip_ref, p_ref, own_ref, o_ref):
        me = chip_ref[0]

        def slot(k):
            return jnp.where(me == k, own_ref[k].astype(F32), p_ref[k].astype(F32))

        acc = slot(0) + slot(1)
        for k in range(2, nch):
            acc = acc + slot(k)
        o_ref[...] = acc

    blk = pl.BlockSpec((nch, tr, c), lambda i, cr: (0, i, 0))
    return pl.pallas_call(
        body, name=name,
        grid_spec=pltpu.PrefetchScalarGridSpec(
            num_scalar_prefetch=1, grid=(h // tr,), in_specs=[blk, blk],
            out_specs=pl.BlockSpec((tr, c), lambda i, cr: (i, 0))),
        out_shape=jax.ShapeDtypeStruct((h, c), F32), compiler_params=_params(("parallel",)),
    )(chip, parts, own)


def _join_halves(mine, other, core, *, name):
    h, c = mine.shape
    tr = _row_tile(h, 256)
    nblk = h // tr

    def body(core_ref, m_ref, s_ref, o_ref):
        is_mine = pl.program_id(0) // nblk == core_ref[0]

        @pl.when(is_mine)
        def _():
            o_ref[...] = m_ref[...]

        @pl.when(jnp.logical_not(is_mine))
        def _():
            o_ref[...] = s_ref[...]

    blk = pl.BlockSpec((tr, c), lambda i, cr: (i % nblk, 0))
    return pl.pallas_call(
        body, name=name,
        grid_spec=pltpu.PrefetchScalarGridSpec(
            num_scalar_prefetch=1, grid=(2 * nblk,), in_specs=[blk, blk],
            out_specs=pl.BlockSpec((tr, c), lambda i, cr: (i, 0))),
        out_shape=jax.ShapeDtypeStruct((2 * h, c), F32), compiler_params=_params(("arbitrary",)),
    )(core, mine, other)


def _place():
    x, y, c = lax.axis_index("x"), lax.axis_index("y"), lax.axis_index("c")
    chips = [(1 - x, y), (x, 1 - y), (1 - x, 1 - y)]
    return x, y, c, chips


def _remote(src, dst, send_sems, recv_sems, k, to):
    return pltpu.make_async_remote_copy(src_ref=src, dst_ref=dst, send_sem=send_sems.at[k],
                                        recv_sem=recv_sems.at[k], device_id=to, device_id_type=MESH)


def _hbm_call(body, name, ins, out_shapes, n_remote):
    any_spec = pl.BlockSpec(memory_space=pl.ANY)
    return pl.pallas_call(
        body, name=name, in_specs=[any_spec] * len(ins), out_specs=tuple([any_spec] * len(out_shapes)),
        out_shape=tuple(out_shapes),
        scratch_shapes=[pltpu.SemaphoreType.DMA((n_remote,)), pltpu.SemaphoreType.DMA((n_remote,))],
    )(*ins)


def _all_gather_chips(shards, *, name):
    n = len(shards)

    def body(*refs):
        ins, outs = refs[:n], refs[n:2 * n]
        send_sems, recv_sems = refs[2 * n:]
        x, y, c, chips = _place()
        me = 2 * x + y
        sent = []
        for s in range(n):
            h = ins[s].shape[0] // 2
            for j, (px, py) in enumerate(chips):
                cp = _remote(ins[s].at[pl.ds(c * h, h)], outs[s].at[me, pl.ds(c * h, h)],
                             send_sems, recv_sems, s * 6 + j, (px, py, c))
                cp.start()
                sent.append(cp)
        for s in range(n):
            h = ins[s].shape[0] // 2
            for j, (px, py) in enumerate(chips):
                slab = outs[s].at[2 * px + py, pl.ds(c * h, h)]
                _remote(slab, slab, send_sems, recv_sems, s * 6 + j, (px, py, c)).wait_recv()
                cp = _remote(slab, slab, send_sems, recv_sems, s * 6 + 3 + j, (x, y, 1 - c))
                cp.start()
                sent.append(cp)
        for s in range(n):
            h = ins[s].shape[0] // 2
            for j, (px, py) in enumerate(chips):
                slab = outs[s].at[2 * px + py, pl.ds((1 - c) * h, h)]
                _remote(slab, slab, send_sems, recv_sems, s * 6 + 3 + j, (x, y, 1 - c)).wait_recv()
        for cp in sent:
            cp.wait_send()

    out_shapes = [jax.ShapeDtypeStruct((N_CHIPS,) + s.shape, s.dtype) for s in shards]
    return _hbm_call(body, name, shards, out_shapes, 6 * n)


def _pair_send_other_half(stacked, *, name):
    n = len(stacked)

    def body(*refs):
        ins, outs = refs[:n], refs[n:2 * n]
        send_sems, recv_sems = refs[2 * n:]
        x, y, c, _chips = _place()
        sent = []
        for s in range(n):
            h = ins[s].shape[1] // 2
            cp = _remote(ins[s].at[:, pl.ds((1 - c) * h, h)], outs[s], send_sems, recv_sems, s, (x, y, 1 - c))
            cp.start()
            sent.append(cp)
        for cp in sent:
            cp.wait_recv()
        for cp in sent:
            cp.wait_send()

    out_shapes = [jax.ShapeDtypeStruct((s.shape[0], s.shape[1] // 2, s.shape[2]), s.dtype) for s in stacked]
    return _hbm_call(body, name, stacked, out_shapes, n)


def _chip_exchange(halves, *, name):
    n = len(halves)

    def body(*refs):
        ins, outs = refs[:n], refs[n:2 * n]
        send_sems, recv_sems = refs[2 * n:]
        x, y, c, chips = _place()
        me = 2 * x + y
        sent = []
        for s in range(n):
            for j, (px, py) in enumerate(chips):
                cp = _remote(ins[s].at[2 * px + py], outs[s].at[me], send_sems, recv_sems, s * 3 + j, (px, py, c))
                cp.start()
                sent.append(cp)
        for s in range(n):
            for j, (px, py) in enumerate(chips):
                slab = outs[s].at[2 * px + py]
                _remote(slab, slab, send_sems, recv_sems, s * 3 + j, (px, py, c)).wait_recv()
        for cp in sent:
            cp.wait_send()

    out_shapes = [jax.ShapeDtypeStruct(s.shape, s.dtype) for s in halves]
    return _hbm_call(body, name, halves, out_shapes, 3 * n)


def _chip_exchange_start(halves, *, name):
    n = len(halves)
    hbm = pl.BlockSpec(memory_space=pltpu.HBM)
    sem = pl.BlockSpec(memory_space=pltpu.SEMAPHORE)

    def body(*refs):
        ins, lands = refs[:n], refs[n:2 * n]
        send_sems, recv_sems = refs[2 * n], refs[2 * n + 1]
        token = refs[-1]
        x, y, c, chips = _place()
        me = 2 * x + y
        for s in range(n):
            for j, (px, py) in enumerate(chips):
                _remote(ins[s].at[2 * px + py], lands[s].at[me], send_sems, recv_sems, s * 3 + j, (px, py, c)).start()
        token[...] = jnp.zeros_like(token)

    slabs = [pltpu.HBM(s.shape, s.dtype) for s in halves]
    outs = pl.pallas_call(
        body, name=name,
        out_shape=(pltpu.SemaphoreType.DMA((3 * n,)), pltpu.SemaphoreType.DMA((3 * n,)), *slabs, *slabs,
                   jax.ShapeDtypeStruct((8, LANES), F32)),
        in_specs=[hbm] * (2 * n), out_specs=(sem, sem, *([hbm] * (2 * n)), pl.BlockSpec(memory_space=pltpu.VMEM)),
        input_output_aliases={i: 2 + i for i in range(2 * n)},
        compiler_params=pltpu.CompilerParams(has_side_effects=pltpu.SideEffectType.DATAFLOW_SIDE_EFFECTING),
    )(*[pltpu.with_memory_space_constraint(s, pltpu.HBM) for s in halves],
      *[pltpu.with_memory_space_constraint(lax.empty(s.shape, s.dtype), pltpu.HBM) for s in halves])
    return outs[0], outs[1], list(outs[2:2 + n]), list(outs[2 + n:2 + 2 * n]), outs[-1]


def _chip_exchange_wait(send_sems, recv_sems, sent, lands, after, *, name):
    n = len(sent)
    hbm = pl.BlockSpec(memory_space=pltpu.HBM)
    sem = pl.BlockSpec(memory_space=pltpu.SEMAPHORE)

    def body(*refs):
        ins, lands_in = refs[:n], refs[n:2 * n]
        send_sems, recv_sems = refs[2 * n], refs[2 * n + 1]
        x, y, c, chips = _place()
        me = 2 * x + y
        for s in range(n):
            for j, (px, py) in enumerate(chips):
                k = 2 * px + py
                _remote(ins[s].at[k], lands_in[s].at[me], send_sems, recv_sems, s * 3 + j, (px, py, c)).wait_send()
                _remote(ins[s].at[k], lands_in[s].at[k], send_sems, recv_sems, s * 3 + j, (px, py, c)).wait_recv()

    slabs = [pltpu.HBM(s.shape, s.dtype) for s in sent]
    outs = pl.pallas_call(
        body, name=name, out_shape=(*slabs, *slabs),
        in_specs=[hbm] * (2 * n) + [sem, sem, pl.BlockSpec(memory_space=pl.ANY)],
        out_specs=tuple([hbm] * (2 * n)), input_output_aliases={i: i for i in range(2 * n)},
        compiler_params=pltpu.CompilerParams(has_side_effects=pltpu.SideEffectType.DATAFLOW_SIDE_EFFECTING),
    )(*sent, *lands, send_sems, recv_sems, after)
    return list(outs[n:])


def _pair_swap(halves, *, name):
    n = len(halves)

    def body(*refs):
        ins, outs = refs[:n], refs[n:2 * n]
        send_sems, recv_sems = refs[2 * n:]
        x, y, c, _chips = _place()
        sent = []
        for s in range(n):
            cp = _remote(ins[s], outs[s], send_sems, recv_sems, s, (x, y, 1 - c))
            cp.start()
            sent.append(cp)
        for cp in sent:
            cp.wait_recv()
        for cp in sent:
            cp.wait_send()

    out_shapes = [jax.ShapeDtypeStruct(s.shape, s.dtype) for s in halves]
    return _hbm_call(body, name, halves, out_shapes, n)


def _pack_rows(parts, row_multiple):
    flat = jnp.concatenate([p.reshape(-1) for p in parts])
    quantum = row_multiple * PACK_COLS
    pad = (-flat.shape[0]) % quantum
    flat = jnp.pad(flat, (0, pad))
    return flat.reshape(-1, PACK_COLS)


def _unpack(flat, shapes):
    out, pos = [], 0
    for shp in shapes:
        size = math.prod(shp)
        out.append(flat[pos:pos + size].reshape(shp))
        pos += size
    return out


def _to_chunks_cols(full):
    r, c4 = full.shape
    return full.reshape(r, N_CHIPS, c4 // N_CHIPS).transpose(1, 0, 2)


def _from_chunks_cols(stacked):
    nch, r, c = stacked.shape
    return stacked.transpose(1, 0, 2).reshape(r, nch * c)


def _class_major(a, bl, t, dil):
    w = a.shape[-1]
    if dil == 1:
        return a.reshape(bl, 1, t, w)
    return a.reshape(bl, t // dil, dil, w).transpose(0, 2, 1, 3)


def _natural(a):
    bl, dil, ln, w = a.shape
    if dil == 1:
        return a.reshape(bl * ln, w)
    return a.transpose(0, 2, 1, 3).reshape(bl * ln * dil, w)


def _train_step(x, positions, a_pre_norm, a_w_in, a_w_out, a_post_norm, kv_norm, kv_w_down, kv_latent_norm,
                kv_w_up, b_pre_norm, b_w_in, b_q_norm, b_w_q_up, b_w_out, b_post_norm, loss_target, moments):
    bl, t, d = x.shape
    n = bl * t
    qb = t // A_DILATIONS[-1]
    tq = _tile(t, 256)
    dq4 = d // N_CHIPS
    chip = 2 * lax.axis_index("x") + lax.axis_index("y")
    chip_arr = chip.astype(jnp.int32).reshape(1)
    core_arr = lax.axis_index("c").astype(jnp.int32).reshape(1)

    w_in_a_s = a_w_in[0].astype(BF16)
    outs_s = jnp.concatenate([a_w_out[0], b_w_out[0]], axis=0).astype(BF16)
    small_shapes = [kv_w_down.shape, kv_w_up.shape, b_w_in[0].shape, b_w_q_up[0].shape]
    small_s = _pack_rows([kv_w_down, kv_w_up, b_w_in[0], b_w_q_up[0]], 32).astype(BF16)
    gains_s = jnp.pad(jnp.concatenate([a_pre_norm[0], a_post_norm[0]]), (0, 16 * LANES - 2 * dq4)).reshape(16, LANES)
    shards = [w_in_a_s, outs_s, small_s, gains_s]
    gathered = _all_gather_chips(shards, name="gather_weights")
    g_in_a, g_outs, g_small, g_gains = [lax.dynamic_update_index_in_dim(g, s, chip, 0)
                                        for g, s in zip(gathered, shards)]

    w_in_a = _from_chunks_cols(g_in_a)
    w_out_a = g_outs[:, :A_WIDTH // N_CHIPS].reshape(A_WIDTH, d)
    w_out_b = g_outs[:, A_WIDTH // N_CHIPS:].reshape(B_WIDTH, d)
    sm = [_unpack(g_small[k].reshape(-1), small_shapes) for k in range(N_CHIPS)]
    w_down = jnp.concatenate([sm[k][0] for k in range(N_CHIPS)], axis=0)
    w_up = jnp.concatenate([sm[k][1] for k in range(N_CHIPS)], axis=1)
    w_in_b = jnp.concatenate([sm[k][2] for k in range(N_CHIPS)], axis=1)
    w_q_up = jnp.concatenate([sm[k][3] for k in range(N_CHIPS)], axis=1)
    gflat = g_gains.reshape(N_CHIPS, -1)
    g_a_pre = gflat[:, :dq4].reshape(1, d)
    g_a_post = gflat[:, dq4:2 * dq4].reshape(1, d)

    w_up_h = w_up.reshape(B_KV_LORA, B_HEADS, B_NOPE + B_VDIM)
    w_up_k = jnp.pad(w_up_h[:, :, :B_NOPE], ((0, 0), (0, 0), (0, LANES - B_NOPE))).reshape(B_KV_LORA, B_HEADS * LANES)
    w_up_v = w_up_h[:, :, B_NOPE:].reshape(B_KV_LORA, B_WIDTH)
    w_up_cat = jnp.concatenate([w_up_k, w_up_v], axis=1)
    w_q_up_p = jnp.pad(w_q_up.reshape(B_Q_LORA, B_HEADS, B_QK_DIM),
                       ((0, 0), (0, 0), (0, LANES - B_QK_DIM))).reshape(B_Q_LORA, B_HEADS * LANES)
    zeros_d = lambda c: jnp.zeros((d, c), BF16)
    w_down_p = jnp.concatenate([w_down[:, :B_KV_LORA], zeros_d(B_NOPE), w_down[:, B_KV_LORA:],
                                zeros_d(LANES - B_NOPE - B_ROPE)], axis=1)
    w_cq = w_in_b[:, :B_Q_LORA]
    w_z = w_in_b[:, B_Q_LORA:]

    tabs_a = _rope_tables(positions, A_ROPE_THETA, 0)
    tabs_b = _rope_tables(positions, B_ROPE_THETA, B_NOPE)

    h0 = x.reshape(n, d)
    hn_a = _rms_fwd(h0, g_a_pre, BF16, name="a_pre_norm")
    is_qk = lambda j: j != 2
    is_q = lambda j: j == 0
    z_blk_a = 3 * A_GROUPS
    z_a = _matmul(hn_a, w_in_a, "nn", BF16, name="a_proj_z", b_cols=(z_blk_a, 1))
    o_groups, lse_groups, qkv_cm, hn_cm, tabs_cm = [], [], [], [], []
    for g, dil in enumerate(A_DILATIONS):
        flat = lambda a: _class_major(a, bl, t, dil).reshape(n, a.shape[-1])
        hn_g = hn_a if dil == 1 else flat(hn_a)
        tabs_g = tabs_a if dil == 1 else lax.optimization_barrier(tuple(flat(tb) for tb in tabs_a))
        proj_g = _matmul(hn_g, w_in_a, "nn", BF16, name=f"a_proj_{g}", rope=(tabs_g, is_qk),
                         out_scale=(A_SCALE * LOG2E, is_q), b_cols=(3 * g, 3))
        src = proj_g.reshape(bl, dil, t // dil, 3 * A_WIDTH)
        hn_cm.append(hn_g)
        tabs_cm.append(tabs_g)
        qkv_cm.append(src)
        o_g, lse_g = _attn_a_fwd(src, 0, qb, BF16, name=f"attn_a_fwd_{g}")
        o_groups.append(_natural(o_g))
        lse_groups.append(_natural(lse_g))
    ypre_a, om_a, lse_a = _merge_gate_fwd(o_groups, lse_groups, z_a, 0)
    y_a = _matmul(ypre_a, w_out_a, "nn", F32, name="a_out")
    g_kvn = kv_norm.reshape(1, d)
    g_lat = kv_latent_norm.reshape(1, B_KV_LORA)
    h1, hn_kv, hn_b = _post_norm_block(y_a, g_a_post, h0, [g_kvn, b_pre_norm], name="a_post_norm")

    ckr = _matmul(hn_kv, w_down_p, "nn", F32, name="kv_down")
    c_kv, k_rope = _kv_latent_fwd(ckr, g_lat, tabs_b)
    kvup = _matmul(c_kv, w_up_cat, "nn", BF16, name="kv_up")
    z_b = _matmul(hn_b, w_z, "nn", BF16, name="b_proj_z")
    cq_raw = _matmul(hn_b, w_cq, "nn", F32, name="b_proj_q")
    c_q = _rms_fwd(cq_raw, b_q_norm, BF16, name="b_q_norm")
    always = lambda j: True
    q_cat = _matmul(c_q, w_q_up_p, "nn", BF16, name="b_q_up", rope=(tabs_b, always),
                    out_scale=(B_SCALE * LOG2E, always))
    r3 = lambda a: a.reshape(bl, t, a.shape[-1])
    tabs_b3 = tuple(r3(tb) for tb in tabs_b)
    ypre_b, o_b, lse_b, lse_rows_b = _mla_fwd(r3(q_cat), r3(kvup), r3(k_rope), r3(z_b), tq)
    y_b = _matmul(ypre_b.reshape(n, B_WIDTH), w_out_b, "nn", F32, name="b_out")
    dh2, loss_part = _post_norm_loss(y_b, b_post_norm, h1, loss_target.reshape(n, d))

    dy_b, dg_b_post = _rms_bwd(y_b, b_post_norm, dh2, BF16, name="b_post_norm_bwd")
    dypre_b = _matmul(dy_b, w_out_b, "nt", BF16, name="b_out_dx")
    dw_out_b = _matmul(ypre_b.reshape(n, B_WIDTH), dy_b, "tn", F32, name="b_out_dw", tm=1024, tk=512)
    do_b, dz_b = _gate_bwd(dypre_b, o_b.reshape(n, B_WIDTH), z_b, 0, name="b_gate_bwd", with_delta=False)
    dq_cat, delta_rows_b = _mla_dq(r3(q_cat), r3(kvup), r3(k_rope), r3(do_b), o_b, lse_b, tabs_b3, tq)
    dq_cat = dq_cat.reshape(n, -1)
    dk_cat, dv_b = _mla_dkv(r3(q_cat), r3(kvup), r3(k_rope), r3(do_b), lse_rows_b, delta_rows_b, tq)
    dk_cat, dv_b = dk_cat.reshape(n, -1), dv_b.reshape(n, -1)
    dcq_n = _matmul(dq_cat, w_q_up_p, "nt", F32, name="b_q_up_dx")
    dw_q_up_p = _matmul(c_q, dq_cat, "tn", F32, name="b_q_up_dw", tm=1024, tk=512)
    dcq, dg_b_q = _rms_bwd(cq_raw, b_q_norm, dcq_n, BF16, name="b_q_norm_bwd")
    dhn_b = _matmul(dz_b, w_z, "nt", F32, name="b_proj_z_dx")
    dhn_b = _matmul(dcq, w_cq, "nt", F32, name="b_proj_q_dx", add=dhn_b)
    dw_z = _matmul(hn_b, dz_b, "tn", F32, name="b_proj_z_dw", tm=1024, tk=512)
    dw_cq = _matmul(hn_b, dcq, "tn", F32, name="b_proj_q_dw", tm=1024, tk=512)
    dckv_n = _matmul(dk_cat, w_up_k, "nt", F32, name="kv_up_k_dx")
    dckv_n = _matmul(dv_b, w_up_v, "nt", F32, name="kv_up_v_dx", add=dckv_n)
    dw_up_k = _matmul(c_kv, dk_cat, "tn", F32, name="kv_up_k_dw", tm=1024, tk=512)
    dw_up_v = _matmul(c_kv, dv_b, "tn", F32, name="kv_up_v_dw", tm=1024, tk=512)
    dckr, dg_lat = _kv_latent_bwd(dckv_n, ckr, g_lat, dk_cat, tabs_b)
    dhn_kv = _matmul(dckr, w_down_p, "nt", F32, name="kv_down_dx")
    dw_down_p = _matmul(hn_kv, dckr, "tn", F32, name="kv_down_dw", tm=1024, tk=512)
    dh1, dg_b_pre, dg_kvn = _rms_bwd_pair(h1, b_pre_norm, dhn_b, g_kvn, dhn_kv, dh2, name="h1_norms_bwd")

    dy_a, dg_a_post = _rms_bwd(y_a, g_a_post, dh1, BF16, name="a_post_norm_bwd")
    dypre_a = _matmul(dy_a, w_out_a, "nt", BF16, name="a_out_dx")
    dw_out_a = _matmul(ypre_a, dy_a, "tn", F32, name="a_out_dw", tm=1024, tk=512)
    do_a, dz_a, delta_a = _gate_bwd(dypre_a, om_a, z_a, 0, name="a_gate_bwd", with_delta=True)
    dw_cols = A_IN_WIDTH // N_CHIPS
    dw_tn = _tile(dw_cols, 512)
    dw_kwargs = dict(tm=1024, tn=dw_tn, tk=512, out_chunk_blocks=dw_cols // dw_tn)
    r_big = _matmul(hn_a, dz_a, "tn", F32, name="a_proj_dw_z", out_full=(N_CHIPS, d, dw_cols),
                    out_joff=z_blk_a * A_WIDTH // dw_tn, **dw_kwargs)
    dqkvs = []
    for g, dil in enumerate(A_DILATIONS):
        cm = lambda a: _class_major(a, bl, t, dil)
        swap = lambda a: jnp.swapaxes(a, 2, 3)
        lse_cm, delta_cm = cm(lse_a), cm(delta_a)
        tabs_g = tuple(tb.reshape(bl, dil, t // dil, LANES) for tb in tabs_cm[g])
        dqkv = _attn_a_bwd(qkv_cm[g], 0, cm(do_a), lse_cm, delta_cm, swap(lse_cm), swap(delta_cm),
                           tabs_g, qb, name=f"attn_a_bwd_{g}").reshape(n, 3 * A_WIDTH)
        dqkvs.append(dqkv)
        r_big = _matmul(hn_cm[g], dqkv, "tn", F32, name=f"a_proj_dw_{g}", out_into=r_big,
                        out_joff=3 * g * A_WIDTH // dw_tn, **dw_kwargs)
    r_outs = jnp.concatenate([dw_out_a.reshape(N_CHIPS, A_WIDTH // N_CHIPS, d),
                              dw_out_b.reshape(N_CHIPS, B_WIDTH // N_CHIPS, d)], axis=1)

    bulk = [r_big, r_outs]
    recv_b = _pair_send_other_half(bulk, name="reduce_pair_send")
    halves_b = [_add_my_half(s, p, core_arr, BF16, name=f"reduce_pair_add_{i}")
                for i, (s, p) in enumerate(zip(bulk, recv_b))]
    send_sems, recv_sems, sent_b, lands_b, token = _chip_exchange_start(halves_b, name="reduce_exchange_start")

    dhn_a = _matmul(dz_a, w_in_a, "nt", F32, name="a_proj_dx_z", b_koff=z_blk_a, after=token)
    dhn_more = []
    for g, dil in enumerate(A_DILATIONS):
        koff = 3 * g * A_WIDTH // _tile(3 * A_WIDTH, 1024)
        if dil == 1:
            dhn_a = _matmul(dqkvs[g], w_in_a, "nt", F32, name=f"a_proj_dx_{g}", add=dhn_a, b_koff=koff, after=token)
        else:
            part = _matmul(dqkvs[g], w_in_a, "nt", BF16, name=f"a_proj_dx_{g}", b_koff=koff, after=token)
            dhn_more.append(_natural(part.reshape(bl, dil, t // dil, d)))
    grad_x, dg_a_pre = _rms_bwd(h0, g_a_pre, dhn_a, F32, name="a_pre_norm_bwd", adds=(dh1,),
                                dy_more=tuple(dhn_more))

    dw_up = jnp.concatenate([dw_up_k.reshape(B_KV_LORA, B_HEADS, LANES)[:, :, :B_NOPE],
                             dw_up_v.reshape(B_KV_LORA, B_HEADS, B_VDIM)], axis=2).reshape(B_KV_LORA, -1)
    dw_q_up = dw_q_up_p.reshape(B_Q_LORA, B_HEADS, LANES)[:, :, :B_QK_DIM].reshape(B_Q_LORA, -1)
    dw_down = jnp.concatenate([dw_down_p[:, :B_KV_LORA], dw_down_p[:, B_KV_LORA + B_NOPE:B_KV_LORA + B_NOPE + B_ROPE]], axis=1)
    dw_in_b = jnp.concatenate([dw_cq, dw_z], axis=1)
    vec_rep = [dg_kvn.reshape(-1), dg_lat.reshape(-1), dg_b_pre.reshape(-1), dg_b_q.reshape(-1),
               dg_b_post.reshape(-1), loss_part.reshape(-1)]
    vec_shapes = [(dq4,), (dq4,)] + [v.shape for v in vec_rep]
    down_c = dw_down.reshape(N_CHIPS, dq4, -1)
    up_c = _to_chunks_cols(dw_up)
    inb_c = _to_chunks_cols(dw_in_b)
    qup_c = _to_chunks_cols(dw_q_up)
    small_chunks = []
    for k in range(N_CHIPS):
        vecs = [dg_a_pre.reshape(-1)[k * dq4:(k + 1) * dq4], dg_a_post.reshape(-1)[k * dq4:(k + 1) * dq4]] + vec_rep
        small_chunks.append(_pack_rows([down_c[k], up_c[k], inb_c[k], qup_c[k]] + vecs, 32))
    r_small = jnp.stack(small_chunks)

    recv_s = _pair_send_other_half([r_small], name="reduce_pair_send_small")
    halves_s = [_add_my_half(r_small, recv_s[0], core_arr, F32, name="reduce_pair_add_small")]
    parts_s = list(_chip_exchange(halves_s, name="reduce_exchange_small"))
    parts_b = _chip_exchange_wait(send_sems, recv_sems, sent_b, lands_b, grad_x, name="reduce_exchange_wait")
    sums = [_sum_chips(p, own, chip_arr, name=f"reduce_chip_sum_{i}")
            for i, (p, own) in enumerate(zip(parts_b + parts_s, sent_b + halves_s))]
    others = _pair_swap(sums, name="reduce_pair_swap")
    g_big, g_outs_r, g_small_r = [_join_halves(m, o, core_arr, name=f"reduce_join_{i}")
                                  for i, (m, o) in enumerate(zip(sums, others))]

    grads = {}
    grads["a_w_in"] = g_big
    grads["a_w_out"] = g_outs_r[:A_WIDTH // N_CHIPS]
    grads["b_w_out"] = g_outs_r[A_WIDTH // N_CHIPS:]
    small_out_shapes = [down_c.shape[1:], up_c.shape[1:], inb_c.shape[1:], qup_c.shape[1:]] + vec_shapes
    (grads["kv_w_down"], grads["kv_w_up"], grads["b_w_in"], grads["b_w_q_up"], grads["a_pre_norm"],
     grads["a_post_norm"], grads["kv_norm"], grads["kv_latent_norm"], grads["b_pre_norm"], grads["b_q_norm"],
     grads["b_post_norm"], loss_sum) = _unpack(g_small_r.reshape(-1), small_out_shapes)

    weights = dict(a_pre_norm=a_pre_norm, a_w_in=a_w_in, a_w_out=a_w_out, a_post_norm=a_post_norm, kv_norm=kv_norm,
                   kv_w_down=kv_w_down, kv_latent_norm=kv_latent_norm, kv_w_up=kv_w_up, b_pre_norm=b_pre_norm,
                   b_w_in=b_w_in, b_q_norm=b_q_norm, b_w_q_up=b_w_q_up, b_w_out=b_w_out, b_post_norm=b_post_norm)
    names = list(weights)
    out_g, out_d, out_m, out_v = [], [], [], []
    for i, nm in enumerate(names):
        w = weights[nm]
        two_d = (1, w.shape[0]) if w.ndim == 1 else (w.shape[-2], w.shape[-1])
        gw = grads[nm].reshape(two_d)
        dlt, new_m, new_v = _adamw(w.reshape(two_d), gw, moments[i].reshape(two_d),
                                   moments[len(names) + i].reshape(two_d), name=f"adamw_{nm}")
        out_g.append(gw.reshape(w.shape))
        out_d.append(dlt.reshape(w.shape))
        out_m.append(new_m.reshape(w.shape))
        out_v.append(new_v.reshape(w.shape))
    return (loss_sum.reshape(()), grad_x.reshape(bl, t, d), *out_g, *out_d, *out_m, *out_v)


def kernel(x, positions, a_pre_norm, a_w_in, a_w_out, a_post_norm, kv_norm, kv_w_down, kv_latent_norm, kv_w_up, b_pre_norm, b_w_in, b_q_norm, b_w_q_up, b_w_out, b_post_norm, loss_target, m_a_pre_norm, m_a_w_in, m_a_w_out, m_a_post_norm, m_kv_norm, m_kv_w_down, m_kv_latent_norm, m_kv_w_up, m_b_pre_norm, m_b_w_in, m_b_q_norm, m_b_w_q_up, m_b_w_out, m_b_post_norm, v_a_pre_norm, v_a_w_in, v_a_w_out, v_a_post_norm, v_kv_norm, v_kv_w_down, v_kv_latent_norm, v_kv_w_up, v_b_pre_norm, v_b_w_in, v_b_q_norm, v_b_w_q_up, v_b_w_out, v_b_post_norm):
    moments = (m_a_pre_norm, m_a_w_in, m_a_w_out, m_a_post_norm, m_kv_norm, m_kv_w_down, m_kv_latent_norm, m_kv_w_up,
               m_b_pre_norm, m_b_w_in, m_b_q_norm, m_b_w_q_up, m_b_w_out, m_b_post_norm,
               v_a_pre_norm, v_a_w_in, v_a_w_out, v_a_post_norm, v_kv_norm, v_kv_w_down, v_kv_latent_norm, v_kv_w_up,
               v_b_pre_norm, v_b_w_in, v_b_q_norm, v_b_w_q_up, v_b_w_out, v_b_post_norm)
    return _train_step(x, positions, a_pre_norm, a_w_in, a_w_out, a_post_norm, kv_norm, kv_w_down, kv_latent_norm,
                       kv_w_up, b_pre_norm, b_w_in, b_q_norm, b_w_q_up, b_w_out, b_post_norm, loss_target, moments)
```

```python
import math

import jax
import jax.numpy as jnp
from jax import lax
from jax.experimental import pallas as pl
from jax.experimental.pallas import tpu as pltpu

F32 = jnp.float32
BF16 = jnp.bfloat16
MESH = pl.DeviceIdType.MESH

NORM_EPS = 1e-6
NEG = -1e30
LANES = 128
VMEM_LIMIT = 56 * 1024 * 1024
LOG2E = math.log2(math.e)
LN2 = math.log(2.0)

A_GROUPS = 3
A_DILATIONS = (1, 4, 16)
A_HEADS = 8
A_HEAD_DIM = 128
A_WIDTH = A_HEADS * A_HEAD_DIM
A_ROPE_THETA = 500000.0
A_IN_WIDTH = A_GROUPS * 3 * A_WIDTH + A_WIDTH
A_SCALE = A_HEAD_DIM ** -0.5

B_HEADS = 16
B_NOPE = 64
B_ROPE = 32
B_QK_DIM = B_NOPE + B_ROPE
B_VDIM = 64
B_WIDTH = B_HEADS * B_VDIM
B_Q_LORA = 384
B_KV_LORA = 256
B_ROPE_THETA = 10000.0
B_SCALE = B_QK_DIM ** -0.5

ADAM_LR = 0.001
ADAM_B1 = 0.9
ADAM_B2 = 0.999
ADAM_EPS = 1e-08
ADAM_WD = 0.01
ADAM_STEP = 10

N_CHIPS = 4
PACK_COLS = 512


def _params(sem=None):
    return pltpu.CompilerParams(dimension_semantics=sem, vmem_limit_bytes=VMEM_LIMIT)


def _tile(n, want):
    t = min(n, want)
    assert n % t == 0, (n, want)
    return t


def _row_tile(n, want):
    for t in range(min(n, want), 0, -1):
        if n % t == 0 and (t % 16 == 0 or t == n):
            return t
    return n


def _rope_tables(positions, theta, lane0):
    half = 16
    inv_freq = 1.0 / (theta ** (jnp.arange(half, dtype=F32) * (2.0 / (2 * half))))
    n = positions.size
    per_row = LANES // half
    pos = jnp.repeat(positions.astype(F32).reshape(n // per_row, per_row), half, axis=1)
    ang = pos * jnp.tile(inv_freq, per_row)
    cos, sin = lax.optimization_barrier((jnp.cos(ang), jnp.sin(ang)))
    cos, sin = cos.reshape(n, half), sin.reshape(n, half)
    pre = jnp.zeros((n, lane0), F32)
    post = jnp.zeros((n, LANES - lane0 - 2 * half), F32)
    z16 = jnp.zeros((n, half), F32)
    c = jnp.concatenate([pre + 1.0, cos, cos, post + 1.0], axis=1)
    sa = jnp.concatenate([pre, -sin, z16, post], axis=1)
    sb = jnp.concatenate([pre, z16, sin, post], axis=1)
    return lax.optimization_barrier((c, sa, sb))


def _rope_apply(x, c, sa, sb, sign):
    k = x.shape[1] // LANES
    if k > 1:
        c, sa, sb = (jnp.concatenate([t] * k, axis=1) for t in (c, sa, sb))
    w = x.shape[1]
    up = pltpu.roll(x, w - 16, 1)
    dn = pltpu.roll(x, 16, 1)
    if sign > 0:
        return x * c + up * sa + dn * sb
    return x * c - up * sa - dn * sb


def _matmul(a, b, mode, out_dtype, *, name, tm=512, tn=1024, tk=1024, add=None, rope=None,
            out_scale=None, b_koff=0, b_cols=None, out_into=None, out_full=None, out_joff=0,
            out_chunk_blocks=None, after=None):
    if mode == "nn":
        m, k = a.shape
        n = b.shape[1]
    elif mode == "nt":
        m, k = a.shape
        n = b.shape[0]
    else:
        k, m = a.shape
        n = b.shape[1]
    b_j0 = 0
    if b_cols is not None:
        tn = _tile(n, tn)
        b_j0, n = b_cols[0], b_cols[1] * tn
    tm, tn, tk = _tile(m, tm), _tile(n, tn), _tile(k, tk)
    nk = k // tk
    if mode == "nn":
        a_spec = pl.BlockSpec((tm, tk), lambda j, i, kk: (i, kk))
        b_spec = pl.BlockSpec((tk, tn), lambda j, i, kk: (kk, j + b_j0))
        dims = (((1,), (0,)), ((), ()))
    elif mode == "nt":
        a_spec = pl.BlockSpec((tm, tk), lambda j, i, kk: (i, kk))
        b_spec = pl.BlockSpec((tn, tk), lambda j, i, kk: (j, kk + b_koff))
        dims = (((1,), (1,)), ((), ()))
    else:
        a_spec = pl.BlockSpec((tk, tm), lambda j, i, kk: (kk, i))
        b_spec = pl.BlockSpec((tk, tn), lambda j, i, kk: (kk, j))
        dims = (((0,), (0,)), ((), ()))
    operands = [a, b]
    in_specs = [a_spec, b_spec]
    if add is not None:
        operands.append(add)
        in_specs.append(pl.BlockSpec((tm, tn), lambda j, i, kk: (i, j)))
    if rope is not None:
        tables, rope_pred = rope
        for t in tables:
            operands.append(t)
            in_specs.append(pl.BlockSpec((tm, LANES), lambda j, i, kk: (i, 0)))
    aliases = {}
    if out_into is not None:
        aliases = {len(operands): 0}
        operands.append(out_into)
        in_specs.append(pl.BlockSpec(memory_space=pl.ANY))
        out_shape = jax.ShapeDtypeStruct(out_into.shape, out_into.dtype)
    elif out_full is not None:
        out_shape = jax.ShapeDtypeStruct(out_full, out_dtype)
    else:
        out_shape = jax.ShapeDtypeStruct((m, n), out_dtype)
    if after is not None:
        operands.append(after)
        in_specs.append(pl.BlockSpec(memory_space=pl.ANY))
    if out_chunk_blocks is not None:
        out_spec = pl.BlockSpec((None, tm, tn), lambda j, i, kk: ((j + out_joff) // out_chunk_blocks, i,
                                                                  (j + out_joff) % out_chunk_blocks))
    else:
        out_spec = pl.BlockSpec((tm, tn), lambda j, i, kk: (i, j + out_joff))

    def body(*refs):
        a_ref, b_ref = refs[0], refs[1]
        pos = 2
        add_ref = None
        if add is not None:
            add_ref = refs[pos]
            pos += 1
        tab_refs = None
        if rope is not None:
            tab_refs = refs[pos:pos + 3]
            pos += 3
        if out_into is not None:
            pos += 1
        if after is not None:
            pos += 1
        o_ref = refs[pos]
        acc_ref = refs[pos + 1] if nk > 1 else None

        def finish(res):
            if add_ref is not None:
                res = res + add_ref[...].astype(F32)
            if tab_refs is None:
                o_ref[...] = res.astype(o_ref.dtype)
                return
            j = pl.program_id(0)
            flag = rope_pred(j)
            roped = _rope_apply(res, tab_refs[0][...], tab_refs[1][...], tab_refs[2][...], 1)
            if out_scale is not None:
                value, scale_pred = out_scale
                use = scale_pred(j)
                roped = roped * (value if use is True else jnp.where(use, value, 1.0))
            if flag is True:
                o_ref[...] = roped.astype(o_ref.dtype)
                return

            @pl.when(flag)
            def _():
                o_ref[...] = roped.astype(o_ref.dtype)

            @pl.when(jnp.logical_not(flag))
            def _():
                o_ref[...] = res.astype(o_ref.dtype)

        part = lax.dot_general(a_ref[...].astype(BF16), b_ref[...].astype(BF16), dims,
                               preferred_element_type=F32)
        if nk == 1:
            finish(part)
            return
        kk = pl.program_id(2)

        @pl.when(kk == 0)
        def _():
            acc_ref[...] = part

        @pl.when(kk > 0)
        def _():
            acc_ref[...] += part

        @pl.when(kk == nk - 1)
        def _():
            finish(acc_ref[...])

    return pl.pallas_call(
        body, name=name, grid=(n // tn, m // tm, nk), in_specs=in_specs, out_specs=out_spec,
        out_shape=out_shape, input_output_aliases=aliases,
        scratch_shapes=[pltpu.VMEM((tm, tn), F32)] if nk > 1 else [],
        compiler_params=_params(("parallel", "parallel", "arbitrary")),
    )(*operands)


def _rms_fwd(x, g, out_dtype, *, name, add=None, tr=512):
    n, d = x.shape
    tr = _tile(n, tr)
    row = pl.BlockSpec((tr, d), lambda i: (i, 0))
    vec = pl.BlockSpec((1, d), lambda i: (0, 0))

    def body(*refs):
        x_ref, g_ref = refs[0], refs[1]
        o_ref = refs[-1]
        xv = x_ref[...].astype(F32)
        r = lax.rsqrt(jnp.mean(xv * xv, axis=-1, keepdims=True) + NORM_EPS)
        y = xv * r * g_ref[...]
        if add is not None:
            y = refs[2][...] + y
        o_ref[...] = y.astype(o_ref.dtype)

    ops = [x, g] + ([add] if add is not None else [])
    specs = [row, vec] + ([row] if add is not None else [])
    return pl.pallas_call(
        body, name=name, grid=(n // tr,), in_specs=specs, out_specs=row,
        out_shape=jax.ShapeDtypeStruct((n, d), out_dtype), compiler_params=_params(("parallel",)),
    )(*ops)


def _rms_bwd(x, g, dy, out_dtype, *, name, adds=(), dy_more=(), tr=512):
    n, d = x.shape
    tr = _tile(n, tr)
    steps = n // tr
    row = pl.BlockSpec((tr, d), lambda i: (i, 0))
    vec = pl.BlockSpec((1, d), lambda i: (0, 0))
    na = len(adds) + len(dy_more)

    def body(*refs):
        x_ref, g_ref, dy_ref = refs[:3]
        add_refs = refs[3:3 + len(adds)]
        more_refs = refs[3 + len(adds):3 + na]
        dx_ref, dg_ref, acc_ref = refs[3 + na:]
        i = pl.program_id(0)
        xv = x_ref[...].astype(F32)
        r = lax.rsqrt(jnp.mean(xv * xv, axis=-1, keepdims=True) + NORM_EPS)
        xh = xv * r
        dyv = dy_ref[...].astype(F32)
        for m_ref in more_refs:
            dyv = dyv + m_ref[...].astype(F32)
        part = (dyv * xh).reshape(tr // 8, 8, d).sum(axis=0)

        @pl.when(i == 0)
        def _():
            acc_ref[...] = part

        @pl.when(i > 0)
        def _():
            acc_ref[...] += part

        t = dyv * g_ref[...]
        dx = r * (t - xh * jnp.mean(t * xh, axis=-1, keepdims=True))
        for a_ref in add_refs:
            dx = dx + a_ref[...].astype(F32)
        dx_ref[...] = dx.astype(dx_ref.dtype)

        @pl.when(i == steps - 1)
        def _():
            dg_ref[...] = jnp.sum(acc_ref[...], axis=0, keepdims=True)

    return pl.pallas_call(
        body, name=name, grid=(steps,), in_specs=[row, vec, row] + [row] * na,
        out_specs=(row, vec),
        out_shape=(jax.ShapeDtypeStruct((n, d), out_dtype), jax.ShapeDtypeStruct((1, d), F32)),
        scratch_shapes=[pltpu.VMEM((8, d), F32)], compiler_params=_params(("arbitrary",)),
    )(x, g, dy, *adds, *dy_more)


def _rms(xv, g):
    return xv * lax.rsqrt(jnp.mean(xv * xv, axis=-1, keepdims=True) + NORM_EPS) * g


def _post_norm_block(y, g, h_in, next_gains, *, name, tr=512):
    n, d = y.shape
    tr = _tile(n, tr)
    nk = len(next_gains)
    row = pl.BlockSpec((tr, d), lambda i: (i, 0))
    vec = pl.BlockSpec((1, d), lambda i: (0, 0))

    def body(*refs):
        y_ref, g_ref, h_ref = refs[:3]
        gk_refs = refs[3:3 + nk]
        o_ref = refs[3 + nk]
        hn_refs = refs[4 + nk:]
        h = h_ref[...] + _rms(y_ref[...], g_ref[...])
        o_ref[...] = h
        for gk_ref, hn_ref in zip(gk_refs, hn_refs):
            hn_ref[...] = _rms(h, gk_ref[...]).astype(BF16)

    return pl.pallas_call(
        body, name=name, grid=(n // tr,), in_specs=[row, vec, row] + [vec] * nk,
        out_specs=(row,) * (1 + nk),
        out_shape=(jax.ShapeDtypeStruct((n, d), F32),) + (jax.ShapeDtypeStruct((n, d), BF16),) * nk,
        compiler_params=_params(("parallel",)),
    )(y, g, h_in, *next_gains)


def _post_norm_loss(y, g, h_in, target, *, tr=512):
    n, d = y.shape
    tr = _tile(n, tr)
    steps = n // tr
    row = pl.BlockSpec((tr, d), lambda i: (i, 0))

    def body(y_ref, g_ref, h_ref, t_ref, dh_ref, loss_ref, acc_ref):
        i = pl.program_id(0)
        e = h_ref[...] + _rms(y_ref[...], g_ref[...]) - t_ref[...]
        dh_ref[...] = e / d
        part = (e * e).reshape(tr // 8, 8, d).sum(axis=0)

        @pl.when(i == 0)
        def _():
            acc_ref[...] = part

        @pl.when(i > 0)
        def _():
            acc_ref[...] += part

        @pl.when(i == steps - 1)
        def _():
            s = jnp.sum(jnp.sum(acc_ref[...], axis=-1, keepdims=True), axis=0, keepdims=True)
            loss_ref[...] = 0.5 * s / d

    return pl.pallas_call(
        body, name="b_post_norm_loss", grid=(steps,),
        in_specs=[row, pl.BlockSpec((1, d), lambda i: (0, 0)), row, row],
        out_specs=(row, pl.BlockSpec((1, 1), lambda i: (0, 0))),
        out_shape=(jax.ShapeDtypeStruct((n, d), F32), jax.ShapeDtypeStruct((1, 1), F32)),
        scratch_shapes=[pltpu.VMEM((8, d), F32)], compiler_params=_params(("arbitrary",)),
    )(y, g, h_in, target)


def _rms_bwd_pair(x, g1, dy1, g2, dy2, add, *, name, tr=512):
    n, d = x.shape
    tr = _tile(n, tr)
    steps = n // tr
    row = pl.BlockSpec((tr, d), lambda i: (i, 0))
    vec = pl.BlockSpec((1, d), lambda i: (0, 0))

    def body(x_ref, g1_ref, d1_ref, g2_ref, d2_ref, add_ref, dx_ref, dg1_ref, dg2_ref, acc_ref):
        i = pl.program_id(0)
        xv = x_ref[...]
        r = lax.rsqrt(jnp.mean(xv * xv, axis=-1, keepdims=True) + NORM_EPS)
        xh = xv * r
        dx = add_ref[...]
        for k, (g_ref, d_ref) in enumerate(((g1_ref, d1_ref), (g2_ref, d2_ref))):
            dyv = d_ref[...].astype(F32)
            part = (dyv * xh).reshape(tr // 8, 8, d).sum(axis=0)

            @pl.when(i == 0)
            def _(part=part, k=k):
                acc_ref[k] = part

            @pl.when(i > 0)
            def _(part=part, k=k):
                acc_ref[k] += part

            t = dyv * g_ref[...]
            dx = dx + r * (t - xh * jnp.mean(t * xh, axis=-1, keepdims=True))
        dx_ref[...] = dx

        @pl.when(i == steps - 1)
        def _():
            dg1_ref[...] = jnp.sum(acc_ref[0], axis=0, keepdims=True)
            dg2_ref[...] = jnp.sum(acc_ref[1], axis=0, keepdims=True)

    return pl.pallas_call(
        body, name=name, grid=(steps,), in_specs=[row, vec, row, vec, row, row],
        out_specs=(row, vec, vec),
        out_shape=(jax.ShapeDtypeStruct((n, d), F32), jax.ShapeDtypeStruct((1, d), F32),
                   jax.ShapeDtypeStruct((1, d), F32)),
        scratch_shapes=[pltpu.VMEM((2, 8, d), F32)], compiler_params=_params(("arbitrary",)),
    )(x, g1, dy1, g2, dy2, add)


def _kv_latent_fwd(ckr, g_lat, tabs, *, tr=512):
    n = ckr.shape[0]
    tr = _tile(n, tr)
    lat = B_KV_LORA

    def body(c_ref, k_ref, g_ref, tc, tsa, tsb, ckv_ref, kr_ref):
        xv = c_ref[...]
        r = lax.rsqrt(jnp.mean(xv * xv, axis=-1, keepdims=True) + NORM_EPS)
        ckv_ref[...] = (xv * r * g_ref[...]).astype(BF16)
        kr_ref[...] = _rope_apply(k_ref[...], tc[...], tsa[...], tsb[...], 1).astype(BF16)

    tab = pl.BlockSpec((tr, LANES), lambda i: (i, 0))
    return pl.pallas_call(
        body, name="kv_latent_fwd", grid=(n // tr,),
        in_specs=[pl.BlockSpec((tr, lat), lambda i: (i, 0)),
                  pl.BlockSpec((tr, LANES), lambda i: (i, lat // LANES)),
                  pl.BlockSpec((1, lat), lambda i: (0, 0)), tab, tab, tab],
        out_specs=(pl.BlockSpec((tr, lat), lambda i: (i, 0)), tab),
        out_shape=(jax.ShapeDtypeStruct((n, lat), BF16), jax.ShapeDtypeStruct((n, LANES), BF16)),
        compiler_params=_params(("parallel",)),
    )(ckr, ckr, g_lat, *tabs)


def _kv_latent_bwd(dckv, ckr, g_lat, dk_cat, tabs, *, tr=512):
    n = ckr.shape[0]
    tr = _tile(n, tr)
    steps = n // tr
    lat = B_KV_LORA
    wk = dk_cat.shape[1]

    def body(d_ref, c_ref, g_ref, dk_ref, tc, tsa, tsb, o_ref, dg_ref, acc_ref):
        i = pl.program_id(0)
        xv = c_ref[...]
        r = lax.rsqrt(jnp.mean(xv * xv, axis=-1, keepdims=True) + NORM_EPS)
        xh = xv * r
        dyv = d_ref[...]
        part = (dyv * xh).reshape(tr // 8, 8, lat).sum(axis=0)

        @pl.when(i == 0)
        def _():
            acc_ref[...] = part

        @pl.when(i > 0)
        def _():
            acc_ref[...] += part

        t = dyv * g_ref[...]
        dx = r * (t - xh * jnp.mean(t * xh, axis=-1, keepdims=True))
        o_ref[:, 0:lat] = dx.astype(o_ref.dtype)
        dkr = dk_ref[:, 0:LANES].astype(F32)
        for h in range(1, wk // LANES):
            dkr = dkr + dk_ref[:, h * LANES:(h + 1) * LANES].astype(F32)
        o_ref[:, lat:lat + LANES] = _rope_apply(dkr, tc[...], tsa[...], tsb[...], -1).astype(o_ref.dtype)

        @pl.when(i == steps - 1)
        def _():
            dg_ref[...] = jnp.sum(acc_ref[...], axis=0, keepdims=True)

    tab = pl.BlockSpec((tr, LANES), lambda i: (i, 0))
    return pl.pallas_call(
        body, name="kv_latent_bwd", grid=(steps,),
        in_specs=[pl.BlockSpec((tr, lat), lambda i: (i, 0)), pl.BlockSpec((tr, lat), lambda i: (i, 0)),
                  pl.BlockSpec((1, lat), lambda i: (0, 0)), pl.BlockSpec((tr, wk), lambda i: (i, 0)),
                  tab, tab, tab],
        out_specs=(pl.BlockSpec((tr, lat + LANES), lambda i: (i, 0)), pl.BlockSpec((1, lat), lambda i: (0, 0))),
        out_shape=(jax.ShapeDtypeStruct((n, lat + LANES), BF16), jax.ShapeDtypeStruct((1, lat), F32)),
        scratch_shapes=[pltpu.VMEM((8, lat), F32)], compiler_params=_params(("arbitrary",)),
    )(dckv, ckr, g_lat, dk_cat, *tabs)


def _sigmoid(z):
    return 1.0 / (1.0 + jnp.exp(-z))


def _lane_place(cols, width):
    rows = cols[0].shape[0]
    lane = lax.broadcasted_iota(jnp.int32, (rows, width), 1)
    out = jnp.zeros((rows, width), F32)
    for h, col in enumerate(cols):
        out = jnp.where(lane == h, col, out)
    return out


def _merge_gate_fwd(outs, lses, proj, z_block, *, tr=256):
    n, w = outs[0].shape
    tr = _tile(n, tr)
    ng = len(outs)

    def body(*refs):
        o_refs = refs[:ng]
        l_refs = refs[ng:2 * ng]
        z_ref = refs[2 * ng]
        y_ref, om_ref, lse_ref = refs[2 * ng + 1:]
        ls = [r[...] for r in l_refs]
        mx = ls[0]
        for l in ls[1:]:
            mx = jnp.maximum(mx, l)
        ssum = jnp.exp2(ls[0] - mx)
        for l in ls[1:]:
            ssum = ssum + jnp.exp2(l - mx)
        tot = mx + jnp.log2(ssum)
        lse_ref[...] = tot
        ws = [jnp.exp2(l - tot) for l in ls]
        for h in range(A_HEADS):
            sl = slice(h * A_HEAD_DIM, (h + 1) * A_HEAD_DIM)
            o = ws[0][:, h:h + 1] * o_refs[0][:, sl]
            for gi in range(1, ng):
                o = o + ws[gi][:, h:h + 1] * o_refs[gi][:, sl]
            z = z_ref[:, sl].astype(F32)
            om_ref[:, sl] = o.astype(BF16)
            y_ref[:, sl] = (o * (z * _sigmoid(z))).astype(BF16)

    row = pl.BlockSpec((tr, w), lambda i: (i, 0))
    lrow = pl.BlockSpec((tr, A_HEADS), lambda i: (i, 0))
    return pl.pallas_call(
        body, name="merge_gate_fwd", grid=(n // tr,),
        in_specs=[row] * ng + [lrow] * ng + [pl.BlockSpec((tr, w), lambda i: (i, z_block))],
        out_specs=(row, row, lrow),
        out_shape=(jax.ShapeDtypeStruct((n, w), BF16), jax.ShapeDtypeStruct((n, w), BF16),
                   jax.ShapeDtypeStruct((n, A_HEADS), F32)),
        compiler_params=_params(("parallel",)),
    )(*outs, *lses, proj)


def _gate_bwd(dy, o, z_arr, z_block, *, name, with_delta, tr=256):
    n, w = dy.shape
    tr = _tile(n, tr)

    def body(*refs):
        dy_ref, o_ref, z_ref, do_ref, dz_ref = refs[:5]
        dyv = dy_ref[...].astype(F32)
        ov = o_ref[...].astype(F32)
        z = z_ref[...].astype(F32)
        sig = _sigmoid(z)
        do = dyv * (z * sig)
        do_ref[...] = do.astype(BF16)
        dz_ref[...] = (dyv * ov * (sig * (1.0 + z * (1.0 - sig)))).astype(BF16)
        if with_delta:
            prod = do * ov
            cols = [jnp.sum(prod[:, h * A_HEAD_DIM:(h + 1) * A_HEAD_DIM], axis=-1, keepdims=True)
                    for h in range(A_HEADS)]
            refs[5][...] = _lane_place(cols, A_HEADS)

    row = pl.BlockSpec((tr, w), lambda i: (i, 0))
    out_specs = [row, row]
    out_shape = [jax.ShapeDtypeStruct((n, w), BF16), jax.ShapeDtypeStruct((n, w), BF16)]
    if with_delta:
        out_specs.append(pl.BlockSpec((tr, A_HEADS), lambda i: (i, 0)))
        out_shape.append(jax.ShapeDtypeStruct((n, A_HEADS), F32))
    return pl.pallas_call(
        body, name=name, grid=(n // tr,),
        in_specs=[row, row, pl.BlockSpec((tr, w), lambda i: (i, z_block))],
        out_specs=tuple(out_specs), out_shape=tuple(out_shape), compiler_params=_params(("parallel",)),
    )(dy, o, z_arr)


def _dot_nt(a, b):
    return lax.dot_general(a, b, (((1,), (1,)), ((), ())), preferred_element_type=F32)


def _dot_nn(a, b):
    return lax.dot_general(a, b, (((1,), (0,)), ((), ())), preferred_element_type=F32)


def _attn_a_fwd(qkv, cb0, qb, out_dtype, *, name):
    bl, dil, ln, _ = qkv.shape
    nb = ln // qb
    hw = A_WIDTH
    heads = range(A_HEADS)
    sls = [slice(h * A_HEAD_DIM, (h + 1) * A_HEAD_DIM) for h in heads]

    def body(*refs):
        if nb > 1:
            q_ref, kc_ref, vc_ref, kp_ref, vp_ref, o_ref, lse_ref = refs
        else:
            q_ref, kc_ref, vc_ref, o_ref, lse_ref = refs
        i = pl.program_id(2)
        qi = lax.broadcasted_iota(jnp.int32, (qb, qb), 0)
        ki = lax.broadcasted_iota(jnp.int32, (qb, qb), 1)
        mask_c = ki <= qi
        mask_p = jnp.logical_and(ki >= qi, i >= 1)
        s_c = [jnp.where(mask_c, _dot_nt(q_ref[:, sls[h]], kc_ref[:, sls[h]]), NEG) for h in heads]
        m = [jnp.max(s_c[h], axis=-1, keepdims=True) for h in heads]
        if nb > 1:
            s_p = [jnp.where(mask_p, _dot_nt(q_ref[:, sls[h]], kp_ref[:, sls[h]]), NEG) for h in heads]
            m = [jnp.maximum(m[h], jnp.max(s_p[h], axis=-1, keepdims=True)) for h in heads]
        p_c = [jnp.exp2(s_c[h] - m[h]) for h in heads]
        l = [jnp.sum(p_c[h], axis=-1, keepdims=True) for h in heads]
        acc = [_dot_nn(p_c[h].astype(BF16), vc_ref[:, sls[h]]) for h in heads]
        if nb > 1:
            p_p = [jnp.exp2(s_p[h] - m[h]) for h in heads]
            l = [l[h] + jnp.sum(p_p[h], axis=-1, keepdims=True) for h in heads]
            acc = [acc[h] + _dot_nn(p_p[h].astype(BF16), vp_ref[:, sls[h]]) for h in heads]
        for h in heads:
            o_ref[:, sls[h]] = (acc[h] / l[h]).astype(o_ref.dtype)
        lse_ref[...] = _lane_place([m[h] + jnp.log2(l[h]) for h in heads], A_HEADS)

    def spec(off, prev):
        if prev:
            return pl.BlockSpec((None, None, qb, hw), lambda b, r, i: (b, r, jnp.maximum(i - 1, 0), cb0 + off))
        return pl.BlockSpec((None, None, qb, hw), lambda b, r, i: (b, r, i, cb0 + off))

    return pl.pallas_call(
        body, name=name, grid=(bl, dil, nb),
        in_specs=[spec(0, False), spec(1, False), spec(2, False)] + ([spec(1, True), spec(2, True)] if nb > 1 else []),
        out_specs=(pl.BlockSpec((None, None, qb, hw), lambda b, r, i: (b, r, i, 0)),
                   pl.BlockSpec((None, None, qb, A_HEADS), lambda b, r, i: (b, r, i, 0))),
        out_shape=(jax.ShapeDtypeStruct((bl, dil, ln, hw), out_dtype),
                   jax.ShapeDtypeStruct((bl, dil, ln, A_HEADS), F32)),
        compiler_params=_params(("parallel", "parallel", "arbitrary")),
    )(*([qkv] * (5 if nb > 1 else 3)))


def _attn_a_bwd(qkv, cb0, do, lse, delta, lse_t, delta_t, tabs, qb, *, name):
    bl, dil, ln, _ = qkv.shape
    nb = ln // qb
    hw = A_WIDTH

    def body(*refs):
        if nb > 1:
            (q_ref, kc_ref, vc_ref, do_ref, lse_ref, dl_ref, lt_ref, dt_ref, tc, tsa, tsb,
             qn_ref, kp_ref, vp_ref, don_ref, ltn_ref, dtn_ref, o_ref) = refs
        else:
            q_ref, kc_ref, vc_ref, do_ref, lse_ref, dl_ref, lt_ref, dt_ref, tc, tsa, tsb, o_ref = refs
        i = pl.program_id(2)
        row = lax.broadcasted_iota(jnp.int32, (qb, qb), 0)
        col = lax.broadcasted_iota(jnp.int32, (qb, qb), 1)
        m_qc = col <= row
        m_kc = row <= col
        m_qp = jnp.logical_and(col >= row, i >= 1)
        m_kn = jnp.logical_and(row >= col, i + 1 < nb)
        c, sa, sb = tc[...], tsa[...], tsb[...]
        heads = range(A_HEADS)
        sls = [slice(h * A_HEAD_DIM, (h + 1) * A_HEAD_DIM) for h in heads]
        q, kc = [q_ref[:, sl] for sl in sls], [kc_ref[:, sl] for sl in sls]
        vc, dov = [vc_ref[:, sl] for sl in sls], [do_ref[:, sl] for sl in sls]
        lse_c = [lse_ref[:, h:h + 1] for h in heads]
        dl_c = [dl_ref[:, h:h + 1] for h in heads]
        s = [_dot_nt(q[h], kc[h]) for h in heads]
        st = [_dot_nt(kc[h], q[h]) for h in heads]
        dp = [_dot_nt(dov[h], vc[h]) for h in heads]
        dpt = [_dot_nt(vc[h], dov[h]) for h in heads]
        p = [jnp.exp2(jnp.where(m_qc, s[h], NEG) - lse_c[h]) for h in heads]
        pt = [jnp.exp2(jnp.where(m_kc, st[h], NEG) - lt_ref[h:h + 1, :]) for h in heads]
        dq = [_dot_nn((p[h] * (dp[h] - dl_c[h])).astype(BF16), kc[h]) for h in heads]
        dk = [_dot_nn((pt[h] * (dpt[h] - dt_ref[h:h + 1, :])).astype(BF16), q[h]) for h in heads]
        dv = [_dot_nn(pt[h].astype(BF16), dov[h]) for h in heads]
        if nb > 1:
            kp, vp = [kp_ref[:, sl] for sl in sls], [vp_ref[:, sl] for sl in sls]
            qn, don = [qn_ref[:, sl] for sl in sls], [don_ref[:, sl] for sl in sls]
            s = [_dot_nt(q[h], kp[h]) for h in heads]
            st = [_dot_nt(kc[h], qn[h]) for h in heads]
            dp = [_dot_nt(dov[h], vp[h]) for h in heads]
            dpt = [_dot_nt(vc[h], don[h]) for h in heads]
            p = [jnp.exp2(jnp.where(m_qp, s[h], NEG) - lse_c[h]) for h in heads]
            pt = [jnp.exp2(jnp.where(m_kn, st[h], NEG) - ltn_ref[h:h + 1, :]) for h in heads]
            dq = [dq[h] + _dot_nn((p[h] * (dp[h] - dl_c[h])).astype(BF16), kp[h]) for h in heads]
            dk = [dk[h] + _dot_nn((pt[h] * (dpt[h] - dtn_ref[h:h + 1, :])).astype(BF16), qn[h]) for h in heads]
            dv = [dv[h] + _dot_nn(pt[h].astype(BF16), don[h]) for h in heads]
        for h in heads:
            o_ref[:, h * A_HEAD_DIM:(h + 1) * A_HEAD_DIM] = _rope_apply(dq[h] * A_SCALE, c, sa, sb, -1).astype(BF16)
            o_ref[:, hw + h * A_HEAD_DIM:hw + (h + 1) * A_HEAD_DIM] = _rope_apply(dk[h] * LN2, c, sa, sb, -1).astype(BF16)
            o_ref[:, 2 * hw + h * A_HEAD_DIM:2 * hw + (h + 1) * A_HEAD_DIM] = dv[h].astype(BF16)

    def cur(w, col):
        return pl.BlockSpec((None, None, qb, w), lambda b, r, i: (b, r, i, col))

    def prev(w, col):
        return pl.BlockSpec((None, None, qb, w), lambda b, r, i: (b, r, jnp.maximum(i - 1, 0), col))

    def nxt(w, col):
        return pl.BlockSpec((None, None, qb, w), lambda b, r, i: (b, r, jnp.minimum(i + 1, nb - 1), col))

    t_cur = pl.BlockSpec((None, None, A_HEADS, qb), lambda b, r, i: (b, r, 0, i))
    t_nxt = pl.BlockSpec((None, None, A_HEADS, qb), lambda b, r, i: (b, r, 0, jnp.minimum(i + 1, nb - 1)))
    in_specs = [cur(hw, cb0), cur(hw, cb0 + 1), cur(hw, cb0 + 2), cur(hw, 0), cur(A_HEADS, 0), cur(A_HEADS, 0),
                t_cur, t_cur, cur(LANES, 0), cur(LANES, 0), cur(LANES, 0)]
    operands = [qkv, qkv, qkv, do, lse, delta, lse_t, delta_t, *tabs]
    if nb > 1:
        in_specs += [nxt(hw, cb0), prev(hw, cb0 + 1), prev(hw, cb0 + 2), nxt(hw, 0), t_nxt, t_nxt]
        operands += [qkv, qkv, qkv, do, lse_t, delta_t]
    return pl.pallas_call(
        body, name=name, grid=(bl, dil, nb), in_specs=in_specs, out_specs=cur(3 * hw, 0),
        out_shape=jax.ShapeDtypeStruct((bl, dil, ln, 3 * hw), BF16),
        compiler_params=_params(("parallel", "parallel", "arbitrary")),
    )(*operands)


def _head_terms(do, o, lse, e):
    rows = do.shape[0]
    lane = lax.broadcasted_iota(jnp.int32, (rows, LANES), 1)
    mine = (lane < B_VDIM) if e == 0 else (lane >= B_VDIM)
    prod = do.astype(F32) * o.astype(F32)
    dl = jnp.sum(jnp.where(mine, prod, 0.0), axis=-1, keepdims=True)
    do_e = jnp.where(mine, do, jnp.zeros_like(do))
    return do_e, dl, lse[:, e * B_VDIM:e * B_VDIM + 1]


def _col_to_row(col, rows):
    return jnp.transpose(jnp.broadcast_to(col, (rows, LANES)))[0:1, :]


def _mla_fwd(q_cat, kvup, kr, z, tq):
    bl, t, _ = q_cat.shape
    nq = t // tq
    pairs = B_HEADS // 2
    v_blk0 = (B_HEADS * LANES) // LANES

    def body(q_ref, k_ref, v_ref, kr_ref, z_ref, y_ref, o_ref, lse_ref, lrow_ref, m_ref, acc_ref):
        qi = pl.program_id(2)
        qs = [q_ref[:, e * LANES:(e + 1) * LANES] for e in range(2)]
        row = lax.broadcasted_iota(jnp.int32, (tq, tq), 0)
        col = lax.broadcasted_iota(jnp.int32, (tq, tq), 1)
        tri = col <= row
        sum_lane = [B_VDIM, 0]

        for e in range(2):
            m_ref[e] = jnp.full((tq, LANES), NEG, F32)
            acc_ref[e] = jnp.zeros((tq, LANES), F32)

        def tile(k0, w, masked):
            lane = lax.broadcasted_iota(jnp.int32, (w, LANES), 1)
            first = lane < B_VDIM
            krv = kr_ref[pl.ds(k0, w), :]
            v = v_ref[pl.ds(k0, w), :]
            vs = [jnp.where(first, v, jnp.where(lane == B_VDIM, 1.0, 0.0).astype(BF16)),
                  jnp.where(first, jnp.where(lane == 0, 1.0, 0.0).astype(BF16), v)]
            ss = []
            for e in range(2):
                k = k_ref[pl.ds(k0, w), e * LANES:(e + 1) * LANES] + krv
                s = _dot_nt(qs[e], k)
                if masked:
                    r = lax.broadcasted_iota(jnp.int32, (tq, w), 0)
                    c = lax.broadcasted_iota(jnp.int32, (tq, w), 1)
                    s = jnp.where(c <= r + (w - tq), s, NEG)
                ss.append(s)
            for e in range(2):
                m_old = m_ref[e]
                m_new = jnp.maximum(m_old, jnp.max(ss[e], axis=-1, keepdims=True))
                p = jnp.exp2(ss[e] - jnp.concatenate([m_new] * (w // LANES), axis=1)).astype(BF16)
                m_ref[e] = m_new
                acc_ref[e] = jnp.exp2(m_old - m_new) * acc_ref[e] + _dot_nn(p, vs[e])

        def step(kb2, carry):
            tile(pl.multiple_of(kb2 * 2 * tq, 2 * tq), 2 * tq, False)
            return carry

        lax.fori_loop(0, qi // 2, step, 0)

        @pl.when(qi % 2 == 1)
        def _():
            tile(pl.multiple_of((qi - 1) * tq, tq), 2 * tq, True)

        @pl.when(qi % 2 == 0)
        def _():
            tile(pl.multiple_of(qi * tq, tq), tq, True)
        lane = lax.broadcasted_iota(jnp.int32, (tq, LANES), 1)
        first = lane < B_VDIM
        accs = [acc_ref[e] for e in range(2)]
        ls = [accs[e][:, sum_lane[e]:sum_lane[e] + 1] for e in range(2)]
        outs = [accs[e] / ls[e] for e in range(2)]
        lses = [m_ref[e] + jnp.log2(ls[e]) for e in range(2)]
        o = jnp.where(first, outs[0], outs[1])
        zv = z_ref[...].astype(F32)
        o_ref[...] = o.astype(BF16)
        y_ref[...] = (o * (zv * _sigmoid(zv))).astype(BF16)
        lse_ref[...] = jnp.where(first, lses[0], lses[1])
        for e in range(2):
            lrow_ref[e:e + 1, :] = jnp.transpose(lses[e])[0:1, :]

    blk = pl.BlockSpec((None, tq, LANES), lambda b, j, i: (b, i, j))
    return pl.pallas_call(
        body, name="mla_fwd", grid=(bl, pairs, nq),
        in_specs=[pl.BlockSpec((None, tq, 2 * LANES), lambda b, j, i: (b, i, j)),
                  pl.BlockSpec((None, t, 2 * LANES), lambda b, j, i: (b, 0, j)),
                  pl.BlockSpec((None, t, LANES), lambda b, j, i: (b, 0, v_blk0 + j)),
                  pl.BlockSpec((None, t, LANES), lambda b, j, i: (b, 0, 0)),
                  blk],
        out_specs=(blk, blk, blk, pl.BlockSpec((None, None, None, 2, tq), lambda b, j, i: (b, j, i, 0, 0))),
        out_shape=(jax.ShapeDtypeStruct((bl, t, B_WIDTH), BF16), jax.ShapeDtypeStruct((bl, t, B_WIDTH), BF16),
                   jax.ShapeDtypeStruct((bl, t, B_WIDTH), F32),
                   jax.ShapeDtypeStruct((bl, pairs, nq, 2, tq), F32)),
        scratch_shapes=[pltpu.VMEM((2, tq, LANES), F32), pltpu.VMEM((2, tq, LANES), F32)],
        compiler_params=_params(("parallel", "parallel", "arbitrary")),
    )(q_cat, kvup, kvup, kr, z)


def _mla_dq(q_cat, kvup, kr, do, o, lse, tabs, tq):
    bl, t, _ = q_cat.shape
    nq = t // tq
    pairs = B_HEADS // 2
    v_blk0 = (B_HEADS * LANES) // LANES

    def body(q_ref, k_ref, v_ref, kr_ref, do_ref, o_ref, lse_ref, tc, tsa, tsb, dq_ref, drow_ref, acc_ref):
        qi = pl.program_id(2)
        dov, ov, lsev = do_ref[...], o_ref[...], lse_ref[...]
        qs = [q_ref[:, e * LANES:(e + 1) * LANES] for e in range(2)]
        terms = [_head_terms(dov, ov, lsev, e) for e in range(2)]
        row = lax.broadcasted_iota(jnp.int32, (tq, tq), 0)
        col = lax.broadcasted_iota(jnp.int32, (tq, tq), 1)
        tri = col <= row
        for e in range(2):
            acc_ref[e] = jnp.zeros((tq, LANES), F32)

        def tile(k0, w, masked):
            krv = kr_ref[pl.ds(k0, w), :]
            v = v_ref[pl.ds(k0, w), :]
            ks = [k_ref[pl.ds(k0, w), e * LANES:(e + 1) * LANES] + krv for e in range(2)]
            ss = [_dot_nt(qs[e], ks[e]) for e in range(2)]
            dps = [_dot_nt(terms[e][0], v) for e in range(2)]
            for e in range(2):
                s = ss[e]
                if masked:
                    r = lax.broadcasted_iota(jnp.int32, (tq, w), 0)
                    c = lax.broadcasted_iota(jnp.int32, (tq, w), 1)
                    s = jnp.where(c <= r + (w - tq), s, NEG)
                p = jnp.exp2(s - terms[e][2])
                ds = (p * (dps[e] - terms[e][1])).astype(BF16)
                acc_ref[e] += _dot_nn(ds, ks[e])

        def step(kb2, carry):
            tile(pl.multiple_of(kb2 * 2 * tq, 2 * tq), 2 * tq, False)
            return carry

        lax.fori_loop(0, qi // 2, step, 0)

        @pl.when(qi % 2 == 1)
        def _():
            tile(pl.multiple_of((qi - 1) * tq, tq), 2 * tq, True)

        @pl.when(qi % 2 == 0)
        def _():
            tile(pl.multiple_of(qi * tq, tq), tq, True)

        for e in range(2):
            dq_ref[:, e * LANES:(e + 1) * LANES] = _rope_apply(acc_ref[e] * B_SCALE, tc[...], tsa[...], tsb[...], -1).astype(BF16)
            drow_ref[e:e + 1, :] = _col_to_row(terms[e][1], tq)

    blk = pl.BlockSpec((None, tq, LANES), lambda b, j, i: (b, i, j))
    tab = pl.BlockSpec((None, tq, LANES), lambda b, j, i: (b, i, 0))
    qblk = pl.BlockSpec((None, tq, 2 * LANES), lambda b, j, i: (b, i, j))
    return pl.pallas_call(
        body, name="mla_dq", grid=(bl, pairs, nq),
        in_specs=[qblk,
                  pl.BlockSpec((None, t, 2 * LANES), lambda b, j, i: (b, 0, j)),
                  pl.BlockSpec((None, t, LANES), lambda b, j, i: (b, 0, v_blk0 + j)),
                  pl.BlockSpec((None, t, LANES), lambda b, j, i: (b, 0, 0)),
                  blk, blk, blk, tab, tab, tab],
        out_specs=(qblk, pl.BlockSpec((None, None, None, 2, tq), lambda b, j, i: (b, j, i, 0, 0))),
        out_shape=(jax.ShapeDtypeStruct((bl, t, B_HEADS * LANES), BF16),
                   jax.ShapeDtypeStruct((bl, pairs, nq, 2, tq), F32)),
        scratch_shapes=[pltpu.VMEM((2, tq, LANES), F32)],
        compiler_params=_params(("parallel", "parallel", "arbitrary")),
    )(q_cat, kvup, kvup, kr, do, o, lse, *tabs)


def _mla_dkv(q_cat, kvup, kr, do, lse_rows, delta_rows, tq):
    bl, t, _ = q_cat.shape
    nq = t // tq
    pairs = B_HEADS // 2
    v_blk0 = (B_HEADS * LANES) // LANES

    def body(q_ref, k_ref, v_ref, kr_ref, do_ref, lrow_ref, drow_ref, dk_ref, dv_ref, acc_ref):
        kb = pl.program_id(2)
        v = v_ref[...]
        krv = kr_ref[...]
        ks = [k_ref[:, e * LANES:(e + 1) * LANES] + krv for e in range(2)]
        krow = lax.broadcasted_iota(jnp.int32, (tq, tq), 0)
        qcol = lax.broadcasted_iota(jnp.int32, (tq, tq), 1)
        tri = krow <= qcol
        lane = lax.broadcasted_iota(jnp.int32, (tq, LANES), 1)
        mine = [lane < B_VDIM, lane >= B_VDIM]

        for e in range(3):
            acc_ref[e] = jnp.zeros((tq, LANES), F32)

        def tile(qb, nblk, masked):
            w = nblk * tq
            rows = pl.ds(pl.multiple_of(qb * tq, tq), w)
            dov = do_ref[rows, :]
            lane_w = lax.broadcasted_iota(jnp.int32, (w, LANES), 1)
            mine_w = [lane_w < B_VDIM, lane_w >= B_VDIM]
            qs = [q_ref[rows, e * LANES:(e + 1) * LANES] for e in range(2)]
            does = [jnp.where(mine_w[e], dov, jnp.zeros_like(dov)) for e in range(2)]
            sts = [_dot_nt(ks[e], qs[e]) for e in range(2)]
            dpts = [_dot_nt(v, does[e]) for e in range(2)]

            def rows_of(ref, e):
                return jnp.concatenate([ref[qb + i, e:e + 1, :] for i in range(nblk)], axis=1)

            pts = []
            for e in range(2):
                st = sts[e]
                if masked:
                    r = lax.broadcasted_iota(jnp.int32, (tq, w), 0)
                    c = lax.broadcasted_iota(jnp.int32, (tq, w), 1)
                    st = jnp.where(r <= c, st, NEG)
                pts.append(jnp.exp2(st - rows_of(lrow_ref, e)))
            acc_ref[2] += _dot_nn(pts[0].astype(BF16), does[0]) + _dot_nn(pts[1].astype(BF16), does[1])
            for e in range(2):
                dst = (pts[e] * (dpts[e] - rows_of(drow_ref, e))).astype(BF16)
                acc_ref[e] += _dot_nn(dst, qs[e])

        rest = nq - 1 - kb
        odd = rest % 2

        @pl.when(odd == 1)
        def _():
            tile(kb, 2, True)

        @pl.when(odd == 0)
        def _():
            tile(kb, 1, True)

        def step(i, carry):
            tile(kb + 1 + odd + 2 * i, 2, False)
            return carry

        lax.fori_loop(0, rest // 2, step, 0)
        dk_ref[:, 0:LANES] = (acc_ref[0] * LN2).astype(BF16)
        dk_ref[:, LANES:2 * LANES] = (acc_ref[1] * LN2).astype(BF16)
        dv_ref[...] = acc_ref[2].astype(BF16)

    full = pl.BlockSpec((None, t, LANES), lambda b, j, i: (b, 0, j))
    rows = pl.BlockSpec((None, None, nq, 2, tq), lambda b, j, i: (b, j, 0, 0, 0))
    kblk = pl.BlockSpec((None, tq, 2 * LANES), lambda b, j, i: (b, i, j))
    return pl.pallas_call(
        body, name="mla_dkv", grid=(bl, pairs, nq),
        in_specs=[pl.BlockSpec((None, t, 2 * LANES), lambda b, j, i: (b, 0, j)),
                  kblk,
                  pl.BlockSpec((None, tq, LANES), lambda b, j, i: (b, i, v_blk0 + j)),
                  pl.BlockSpec((None, tq, LANES), lambda b, j, i: (b, i, 0)),
                  full, rows, rows],
        out_specs=(kblk, pl.BlockSpec((None, tq, LANES), lambda b, j, i: (b, i, j))),
        out_shape=(jax.ShapeDtypeStruct((bl, t, B_HEADS * LANES), BF16),
                   jax.ShapeDtypeStruct((bl, t, B_WIDTH), BF16)),
        scratch_shapes=[pltpu.VMEM((3, tq, LANES), F32)],
        compiler_params=_params(("parallel", "parallel", "arbitrary")),
    )(q_cat, kvup, kvup, kr, do, lse_rows, delta_rows)


def _adamw(w, g, m, v, *, name):
    r, c = w.shape
    tr = _row_tile(r, 256)
    c1 = 1.0 - ADAM_B1
    c2 = 1.0 - ADAM_B2
    bc1 = 1.0 - ADAM_B1 ** ADAM_STEP
    bc2 = 1.0 - ADAM_B2 ** ADAM_STEP

    def body(w_ref, g_ref, m_ref, v_ref, d_ref, nm_ref, nv_ref):
        gv = g_ref[...]
        nm = ADAM_B1 * m_ref[...] + c1 * gv
        nv = ADAM_B2 * v_ref[...] + c2 * (gv * gv)
        nm_ref[...] = nm
        nv_ref[...] = nv
        d_ref[...] = -ADAM_LR * ((nm / bc1) / (jnp.sqrt(nv / bc2) + ADAM_EPS) + ADAM_WD * w_ref[...])

    blk = pl.BlockSpec((tr, c), lambda i: (i, 0))
    sds = jax.ShapeDtypeStruct((r, c), F32)
    return pl.pallas_call(
        body, name=name, grid=(r // tr,), in_specs=[blk] * 4, out_specs=(blk,) * 3,
        out_shape=(sds,) * 3, compiler_params=_params(("parallel",)),
    )(w, g, m, v)


def _add_my_half(stacked, other, core, out_dtype, *, name):
    nch, a, c = stacked.shape
    h = a // 2
    tr = _row_tile(h, 256)
    nblk = h // tr

    def body(core_ref, s_ref, p_ref, o_ref):
        o_ref[...] = (s_ref[...] + p_ref[...]).astype(o_ref.dtype)

    return pl.pallas_call(
        body, name=name,
        grid_spec=pltpu.PrefetchScalarGridSpec(
            num_scalar_prefetch=1, grid=(nch, nblk),
            in_specs=[pl.BlockSpec((None, tr, c), lambda k, i, cr: (k, cr[0] * nblk + i, 0)),
                      pl.BlockSpec((None, tr, c), lambda k, i, cr: (k, i, 0))],
            out_specs=pl.BlockSpec((None, tr, c), lambda k, i, cr: (k, i, 0))),
        out_shape=jax.ShapeDtypeStruct((nch, h, c), out_dtype),
        compiler_params=_params(("parallel", "parallel")),
    )(core, stacked, other)


def _sum_chips(parts, own, chip, *, name):
    nch, h, c = parts.shape
    tr = _row_tile(h, 256)

    def body(chip_ref, p_ref, own_ref, o_ref):
        me = chip_ref[0]

        def slot(k):
            return jnp.where(me == k, own_ref[k].astype(F32), p_ref[k].astype(F32))

        acc = slot(0) + slot(1)
        for k in range(2, nch):
            acc = acc + slot(k)
        o_ref[...] = acc

    blk = pl.BlockSpec((nch, tr, c), lambda i, cr: (0, i, 0))
    return pl.pallas_call(
        body, name=name,
        grid_spec=pltpu.PrefetchScalarGridSpec(
            num_scalar_prefetch=1, grid=(h // tr,), in_specs=[blk, blk],
            out_specs=pl.BlockSpec((tr, c), lambda i, cr: (i, 0))),
        out_shape=jax.ShapeDtypeStruct((h, c), F32), compiler_params=_params(("parallel",)),
    )(chip, parts, own)


def _join_halves(mine, other, core, *, name):
    h, c = mine.shape
    tr = _row_tile(h, 256)
    nblk = h // tr

    def body(core_ref, m_ref, s_ref, o_ref):
        is_mine = pl.program_id(0) // nblk == core_ref[0]

        @pl.when(is_mine)
        def _():
            o_ref[...] = m_ref[...]

        @pl.when(jnp.logical_not(is_mine))
        def _():
            o_ref[...] = s_ref[...]

    blk = pl.BlockSpec((tr, c), lambda i, cr: (i % nblk, 0))
    return pl.pallas_call(
        body, name=name,
        grid_spec=pltpu.PrefetchScalarGridSpec(
            num_scalar_prefetch=1, grid=(2 * nblk,), in_specs=[blk, blk],
            out_specs=pl.BlockSpec((tr, c), lambda i, cr: (i, 0))),
        out_shape=jax.ShapeDtypeStruct((2 * h, c), F32), compiler_params=_params(("arbitrary",)),
    )(core, mine, other)


def _place():
    x, y, c = lax.axis_index("x"), lax.axis_index("y"), lax.axis_index("c")
    chips = [(1 - x, y), (x, 1 - y), (1 - x, 1 - y)]
    return x, y, c, chips


def _remote(src, dst, send_sems, recv_sems, k, to):
    return pltpu.make_async_remote_copy(src_ref=src, dst_ref=dst, send_sem=send_sems.at[k],
                                        recv_sem=recv_sems.at[k], device_id=to, device_id_type=MESH)


def _hbm_call(body, name, ins, out_shapes, n_remote):
    any_spec = pl.BlockSpec(memory_space=pl.ANY)
    return pl.pallas_call(
        body, name=name, in_specs=[any_spec] * len(ins), out_specs=tuple([any_spec] * len(out_shapes)),
        out_shape=tuple(out_shapes),
        scratch_shapes=[pltpu.SemaphoreType.DMA((n_remote,)), pltpu.SemaphoreType.DMA((n_remote,))],
    )(*ins)


def _all_gather_chips(shards, *, name):
    n = len(shards)

    def body(*refs):
        ins, outs = refs[:n], refs[n:2 * n]
        send_sems, recv_sems = refs[2 * n:]
        x, y, c, chips = _place()
        me = 2 * x + y
        sent = []
        for s in range(n):
            h = ins[s].shape[0] // 2
            for j, (px, py) in enumerate(chips):
                cp = _remote(ins[s].at[pl.ds(c * h, h)], outs[s].at[me, pl.ds(c * h, h)],
                             send_sems, recv_sems, s * 6 + j, (px, py, c))
                cp.start()
                sent.append(cp)
        for s in range(n):
            h = ins[s].shape[0] // 2
            for j, (px, py) in enumerate(chips):
                slab = outs[s].at[2 * px + py, pl.ds(c * h, h)]
                _remote(slab, slab, send_sems, recv_sems, s * 6 + j, (px, py, c)).wait_recv()
                cp = _remote(slab, slab, send_sems, recv_sems, s * 6 + 3 + j, (x, y, 1 - c))
                cp.start()
                sent.append(cp)
        for s in range(n):
            h = ins[s].shape[0] // 2
            for j, (px, py) in enumerate(chips):
                slab = outs[s].at[2 * px + py, pl.ds((1 - c) * h, h)]
                _remote(slab, slab, send_sems, recv_sems, s * 6 + 3 + j, (x, y, 1 - c)).wait_recv()
        for cp in sent:
            cp.wait_send()

    out_shapes = [jax.ShapeDtypeStruct((N_CHIPS,) + s.shape, s.dtype) for s in shards]
    return _hbm_call(body, name, shards, out_shapes, 6 * n)


def _pair_send_other_half(stacked, *, name):
    n = len(stacked)

    def body(*refs):
        ins, outs = refs[:n], refs[n:2 * n]
        send_sems, recv_sems = refs[2 * n:]
        x, y, c, _chips = _place()
        sent = []
        for s in range(n):
            h = ins[s].shape[1] // 2
            cp = _remote(ins[s].at[:, pl.ds((1 - c) * h, h)], outs[s], send_sems, recv_sems, s, (x, y, 1 - c))
            cp.start()
            sent.append(cp)
        for cp in sent:
            cp.wait_recv()
        for cp in sent:
            cp.wait_send()

    out_shapes = [jax.ShapeDtypeStruct((s.shape[0], s.shape[1] // 2, s.shape[2]), s.dtype) for s in stacked]
    return _hbm_call(body, name, stacked, out_shapes, n)


def _chip_exchange(halves, *, name):
    n = len(halves)

    def body(*refs):
        ins, outs = refs[:n], refs[n:2 * n]
        send_sems, recv_sems = refs[2 * n:]
        x, y, c, chips = _place()
        me = 2 * x + y
        sent = []
        for s in range(n):
            for j, (px, py) in enumerate(chips):
                cp = _remote(ins[s].at[2 * px + py], outs[s].at[me], send_sems, recv_sems, s * 3 + j, (px, py, c))
                cp.start()
                sent.append(cp)
        for s in range(n):
            for j, (px, py) in enumerate(chips):
                slab = outs[s].at[2 * px + py]
                _remote(slab, slab, send_sems, recv_sems, s * 3 + j, (px, py, c)).wait_recv()
        for cp in sent:
            cp.wait_send()

    out_shapes = [jax.ShapeDtypeStruct(s.shape, s.dtype) for s in halves]
    return _hbm_call(body, name, halves, out_shapes, 3 * n)


def _chip_exchange_start(halves, *, name):
    n = len(halves)
    hbm = pl.BlockSpec(memory_space=pltpu.HBM)
    sem = pl.BlockSpec(memory_space=pltpu.SEMAPHORE)

    def body(*refs):
        ins, lands = refs[:n], refs[n:2 * n]
        send_sems, recv_sems = refs[2 * n], refs[2 * n + 1]
        token = refs[-1]
        x, y, c, chips = _place()
        me = 2 * x + y
        for s in range(n):
            for j, (px, py) in enumerate(chips):
                _remote(ins[s].at[2 * px + py], lands[s].at[me], send_sems, recv_sems, s * 3 + j, (px, py, c)).start()
        token[...] = jnp.zeros_like(token)

    slabs = [pltpu.HBM(s.shape, s.dtype) for s in halves]
    outs = pl.pallas_call(
        body, name=name,
        out_shape=(pltpu.SemaphoreType.DMA((3 * n,)), pltpu.SemaphoreType.DMA((3 * n,)), *slabs, *slabs,
                   jax.ShapeDtypeStruct((8, LANES), F32)),
        in_specs=[hbm] * (2 * n), out_specs=(sem, sem, *([hbm] * (2 * n)), pl.BlockSpec(memory_space=pltpu.VMEM)),
        input_output_aliases={i: 2 + i for i in range(2 * n)},
        compiler_params=pltpu.CompilerParams(has_side_effects=pltpu.SideEffectType.DATAFLOW_SIDE_EFFECTING),
    )(*[pltpu.with_memory_space_constraint(s, pltpu.HBM) for s in halves],
      *[pltpu.with_memory_space_constraint(lax.empty(s.shape, s.dtype), pltpu.HBM) for s in halves])
    return outs[0], outs[1], list(outs[2:2 + n]), list(outs[2 + n:2 + 2 * n]), outs[-1]


def _chip_exchange_wait(send_sems, recv_sems, sent, lands, after, *, name):
    n = len(sent)
    hbm = pl.BlockSpec(memory_space=pltpu.HBM)
    sem = pl.BlockSpec(memory_space=pltpu.SEMAPHORE)

    def body(*refs):
        ins, lands_in = refs[:n], refs[n:2 * n]
        send_sems, recv_sems = refs[2 * n], refs[2 * n + 1]
        x, y, c, chips = _place()
        me = 2 * x + y
        for s in range(n):
            for j, (px, py) in enumerate(chips):
                k = 2 * px + py
                _remote(ins[s].at[k], lands_in[s].at[me], send_sems, recv_sems, s * 3 + j, (px, py, c)).wait_send()
                _remote(ins[s].at[k], lands_in[s].at[k], send_sems, recv_sems, s * 3 + j, (px, py, c)).wait_recv()

    slabs = [pltpu.HBM(s.shape, s.dtype) for s in sent]
    outs = pl.pallas_call(
        body, name=name, out_shape=(*slabs, *slabs),
        in_specs=[hbm] * (2 * n) + [sem, sem, pl.BlockSpec(memory_space=pl.ANY)],
        out_specs=tuple([hbm] * (2 * n)), input_output_aliases={i: i for i in range(2 * n)},
        compiler_params=pltpu.CompilerParams(has_side_effects=pltpu.SideEffectType.DATAFLOW_SIDE_EFFECTING),
    )(*sent, *lands, send_sems, recv_sems, after)
    return list(outs[n:])


def _pair_swap(halves, *, name):
    n = len(halves)

    def body(*refs):
        ins, outs = refs[:n], refs[n:2 * n]
        send_sems, recv_sems = refs[2 * n:]
        x, y, c, _chips = _place()
        sent = []
        for s in range(n):
            cp = _remote(ins[s], outs[s], send_sems, recv_sems, s, (x, y, 1 - c))
            cp.start()
            sent.append(cp)
        for cp in sent:
            cp.wait_recv()
        for cp in sent:
            cp.wait_send()

    out_shapes = [jax.ShapeDtypeStruct(s.shape, s.dtype) for s in halves]
    return _hbm_call(body, name, halves, out_shapes, n)


def _pack_rows(parts, row_multiple):
    flat = jnp.concatenate([p.reshape(-1) for p in parts])
    quantum = row_multiple * PACK_COLS
    pad = (-flat.shape[0]) % quantum
    flat = jnp.pad(flat, (0, pad))
    return flat.reshape(-1, PACK_COLS)


def _unpack(flat, shapes):
    out, pos = [], 0
    for shp in shapes:
        size = math.prod(shp)
        out.append(flat[pos:pos + size].reshape(shp))
        pos += size
    return out


def _to_chunks_cols(full):
    r, c4 = full.shape
    return full.reshape(r, N_CHIPS, c4 // N_CHIPS).transpose(1, 0, 2)


def _from_chunks_cols(stacked):
    nch, r, c = stacked.shape
    return stacked.transpose(1, 0, 2).reshape(r, nch * c)


def _class_major(a, bl, t, dil):
    w = a.shape[-1]
    if dil == 1:
        return a.reshape(bl, 1, t, w)
    return a.reshape(bl, t // dil, dil, w).transpose(0, 2, 1, 3)


def _natural(a):
    bl, dil, ln, w = a.shape
    if dil == 1:
        return a.reshape(bl * ln, w)
    return a.transpose(0, 2, 1, 3).reshape(bl * ln * dil, w)


def _train_step(x, positions, a_pre_norm, a_w_in, a_w_out, a_post_norm, kv_norm, kv_w_down, kv_latent_norm,
                kv_w_up, b_pre_norm, b_w_in, b_q_norm, b_w_q_up, b_w_out, b_post_norm, loss_target, moments):
    bl, t, d = x.shape
    n = bl * t
    qb = t // A_DILATIONS[-1]
    tq = _tile(t, 256)
    dq4 = d // N_CHIPS
    chip = 2 * lax.axis_index("x") + lax.axis_index("y")
    chip_arr = chip.astype(jnp.int32).reshape(1)
    core_arr = lax.axis_index("c").astype(jnp.int32).reshape(1)

    w_in_a_s = a_w_in[0].astype(BF16)
    outs_s = jnp.concatenate([a_w_out[0], b_w_out[0]], axis=0).astype(BF16)
    small_shapes = [kv_w_down.shape, kv_w_up.shape, b_w_in[0].shape, b_w_q_up[0].shape]
    small_s = _pack_rows([kv_w_down, kv_w_up, b_w_in[0], b_w_q_up[0]], 32).astype(BF16)
    gains_s = jnp.pad(jnp.concatenate([a_pre_norm[0], a_post_norm[0]]), (0, 16 * LANES - 2 * dq4)).reshape(16, LANES)
    shards = [w_in_a_s, outs_s, small_s, gains_s]
    gathered = _all_gather_chips(shards, name="gather_weights")
    g_in_a, g_outs, g_small, g_gains = [lax.dynamic_update_index_in_dim(g, s, chip, 0)
                                        for g, s in zip(gathered, shards)]

    w_in_a = _from_chunks_cols(g_in_a)
    w_out_a = g_outs[:, :A_WIDTH // N_CHIPS].reshape(A_WIDTH, d)
    w_out_b = g_outs[:, A_WIDTH // N_CHIPS:].reshape(B_WIDTH, d)
    sm = [_unpack(g_small[k].reshape(-1), small_shapes) for k in range(N_CHIPS)]
    w_down = jnp.concatenate([sm[k][0] for k in range(N_CHIPS)], axis=0)
    w_up = jnp.concatenate([sm[k][1] for k in range(N_CHIPS)], axis=1)
    w_in_b = jnp.concatenate([sm[k][2] for k in range(N_CHIPS)], axis=1)
    w_q_up = jnp.concatenate([sm[k][3] for k in range(N_CHIPS)], axis=1)
    gflat = g_gains.reshape(N_CHIPS, -1)
    g_a_pre = gflat[:, :dq4].reshape(1, d)
    g_a_post = gflat[:, dq4:2 * dq4].reshape(1, d)

    w_up_h = w_up.reshape(B_KV_LORA, B_HEADS, B_NOPE + B_VDIM)
    w_up_k = jnp.pad(w_up_h[:, :, :B_NOPE], ((0, 0), (0, 0), (0, LANES - B_NOPE))).reshape(B_KV_LORA, B_HEADS * LANES)
    w_up_v = w_up_h[:, :, B_NOPE:].reshape(B_KV_LORA, B_WIDTH)
    w_up_cat = jnp.concatenate([w_up_k, w_up_v], axis=1)
    w_q_up_p = jnp.pad(w_q_up.reshape(B_Q_LORA, B_HEADS, B_QK_DIM),
                       ((0, 0), (0, 0), (0, LANES - B_QK_DIM))).reshape(B_Q_LORA, B_HEADS * LANES)
    zeros_d = lambda c: jnp.zeros((d, c), BF16)
    w_down_p = jnp.concatenate([w_down[:, :B_KV_LORA], zeros_d(B_NOPE), w_down[:, B_KV_LORA:],
                                zeros_d(LANES - B_NOPE - B_ROPE)], axis=1)
    w_cq = w_in_b[:, :B_Q_LORA]
    w_z = w_in_b[:, B_Q_LORA:]

    tabs_a = _rope_tables(positions, A_ROPE_THETA, 0)
    tabs_b = _rope_tables(positions, B_ROPE_THETA, B_NOPE)

    h0 = x.reshape(n, d)
    hn_a = _rms_fwd(h0, g_a_pre, BF16, name="a_pre_norm")
    is_qk = lambda j: j != 2
    is_q = lambda j: j == 0
    z_blk_a = 3 * A_GROUPS
    z_a = _matmul(hn_a, w_in_a, "nn", BF16, name="a_proj_z", b_cols=(z_blk_a, 1))
    o_groups, lse_groups, qkv_cm, hn_cm, tabs_cm = [], [], [], [], []
    for g, dil in enumerate(A_DILATIONS):
        flat = lambda a: _class_major(a, bl, t, dil).reshape(n, a.shape[-1])
        hn_g = hn_a if dil == 1 else flat(hn_a)
        tabs_g = tabs_a if dil == 1 else lax.optimization_barrier(tuple(flat(tb) for tb in tabs_a))
        proj_g = _matmul(hn_g, w_in_a, "nn", BF16, name=f"a_proj_{g}", rope=(tabs_g, is_qk),
                         out_scale=(A_SCALE * LOG2E, is_q), b_cols=(3 * g, 3))
        src = proj_g.reshape(bl, dil, t // dil, 3 * A_WIDTH)
        hn_cm.append(hn_g)
        tabs_cm.append(tabs_g)
        qkv_cm.append(src)
        o_g, lse_g = _attn_a_fwd(src, 0, qb, BF16, name=f"attn_a_fwd_{g}")
        o_groups.append(_natural(o_g))
        lse_groups.append(_natural(lse_g))
    ypre_a, om_a, lse_a = _merge_gate_fwd(o_groups, lse_groups, z_a, 0)
    y_a = _matmul(ypre_a, w_out_a, "nn", F32, name="a_out")
    g_kvn = kv_norm.reshape(1, d)
    g_lat = kv_latent_norm.reshape(1, B_KV_LORA)
    h1, hn_kv, hn_b = _post_norm_block(y_a, g_a_post, h0, [g_kvn, b_pre_norm], name="a_post_norm")

    ckr = _matmul(hn_kv, w_down_p, "nn", F32, name="kv_down")
    c_kv, k_rope = _kv_latent_fwd(ckr, g_lat, tabs_b)
    kvup = _matmul(c_kv, w_up_cat, "nn", BF16, name="kv_up")
    z_b = _matmul(hn_b, w_z, "nn", BF16, name="b_proj_z")
    cq_raw = _matmul(hn_b, w_cq, "nn", F32, name="b_proj_q")
    c_q = _rms_fwd(cq_raw, b_q_norm, BF16, name="b_q_norm")
    always = lambda j: True
    q_cat = _matmul(c_q, w_q_up_p, "nn", BF16, name="b_q_up", rope=(tabs_b, always),
                    out_scale=(B_SCALE * LOG2E, always))
    r3 = lambda a: a.reshape(bl, t, a.shape[-1])
    tabs_b3 = tuple(r3(tb) for tb in tabs_b)
    ypre_b, o_b, lse_b, lse_rows_b = _mla_fwd(r3(q_cat), r3(kvup), r3(k_rope), r3(z_b), tq)
    y_b = _matmul(ypre_b.reshape(n, B_WIDTH), w_out_b, "nn", F32, name="b_out")
    dh2, loss_part = _post_norm_loss(y_b, b_post_norm, h1, loss_target.reshape(n, d))

    dy_b, dg_b_post = _rms_bwd(y_b, b_post_norm, dh2, BF16, name="b_post_norm_bwd")
    dypre_b = _matmul(dy_b, w_out_b, "nt", BF16, name="b_out_dx")
    dw_out_b = _matmul(ypre_b.reshape(n, B_WIDTH), dy_b, "tn", F32, name="b_out_dw", tm=1024, tk=2048)
    do_b, dz_b = _gate_bwd(dypre_b, o_b.reshape(n, B_WIDTH), z_b, 0, name="b_gate_bwd", with_delta=False)
    dq_cat, delta_rows_b = _mla_dq(r3(q_cat), r3(kvup), r3(k_rope), r3(do_b), o_b, lse_b, tabs_b3, tq)
    dq_cat = dq_cat.reshape(n, -1)
    dk_cat, dv_b = _mla_dkv(r3(q_cat), r3(kvup), r3(k_rope), r3(do_b), lse_rows_b, delta_rows_b, tq)
    dk_cat, dv_b = dk_cat.reshape(n, -1), dv_b.reshape(n, -1)
    dcq_n = _matmul(dq_cat, w_q_up_p, "nt", F32, name="b_q_up_dx")
    dw_q_up_p = _matmul(c_q, dq_cat, "tn", F32, name="b_q_up_dw", tm=1024, tk=2048)
    dcq, dg_b_q = _rms_bwd(cq_raw, b_q_norm, dcq_n, BF16, name="b_q_norm_bwd")
    dhn_b = _matmul(dz_b, w_z, "nt", F32, name="b_proj_z_dx")
    dhn_b = _matmul(dcq, w_cq, "nt", F32, name="b_proj_q_dx", add=dhn_b)
    dw_z = _matmul(hn_b, dz_b, "tn", F32, name="b_proj_z_dw", tm=1024, tk=2048)
    dw_cq = _matmul(hn_b, dcq, "tn", F32, name="b_proj_q_dw", tm=1024, tk=2048)
    dckv_n = _matmul(dk_cat, w_up_k, "nt", F32, name="kv_up_k_dx")
    dckv_n = _matmul(dv_b, w_up_v, "nt", F32, name="kv_up_v_dx", add=dckv_n)
    dw_up_k = _matmul(c_kv, dk_cat, "tn", F32, name="kv_up_k_dw", tm=1024, tk=2048)
    dw_up_v = _matmul(c_kv, dv_b, "tn", F32, name="kv_up_v_dw", tm=1024, tk=2048)
    dckr, dg_lat = _kv_latent_bwd(dckv_n, ckr, g_lat, dk_cat, tabs_b)
    dhn_kv = _matmul(dckr, w_down_p, "nt", F32, name="kv_down_dx")
    dw_down_p = _matmul(hn_kv, dckr, "tn", F32, name="kv_down_dw", tm=1024, tk=2048)
    dh1, dg_b_pre, dg_kvn = _rms_bwd_pair(h1, b_pre_norm, dhn_b, g_kvn, dhn_kv, dh2, name="h1_norms_bwd")

    dy_a, dg_a_post = _rms_bwd(y_a, g_a_post, dh1, BF16, name="a_post_norm_bwd")
    dypre_a = _matmul(dy_a, w_out_a, "nt", BF16, name="a_out_dx")
    dw_out_a = _matmul(ypre_a, dy_a, "tn", F32, name="a_out_dw", tm=1024, tk=2048)
    do_a, dz_a, delta_a = _gate_bwd(dypre_a, om_a, z_a, 0, name="a_gate_bwd", with_delta=True)
    dw_cols = A_IN_WIDTH // N_CHIPS
    dw_tn = _tile(dw_cols, 512)
    dw_kwargs = dict(tm=1024, tn=dw_tn, tk=2048, out_chunk_blocks=dw_cols // dw_tn)
    r_big = _matmul(hn_a, dz_a, "tn", F32, name="a_proj_dw_z", out_full=(N_CHIPS, d, dw_cols),
                    out_joff=z_blk_a * A_WIDTH // dw_tn, **dw_kwargs)
    dqkvs = []
    for g, dil in enumerate(A_DILATIONS):
        cm = lambda a: _class_major(a, bl, t, dil)
        swap = lambda a: jnp.swapaxes(a, 2, 3)
        lse_cm, delta_cm = cm(lse_a), cm(delta_a)
        tabs_g = tuple(tb.reshape(bl, dil, t // dil, LANES) for tb in tabs_cm[g])
        dqkv = _attn_a_bwd(qkv_cm[g], 0, cm(do_a), lse_cm, delta_cm, swap(lse_cm), swap(delta_cm),
                           tabs_g, qb, name=f"attn_a_bwd_{g}").reshape(n, 3 * A_WIDTH)
        dqkvs.append(dqkv)
        r_big = _matmul(hn_cm[g], dqkv, "tn", F32, name=f"a_proj_dw_{g}", out_into=r_big,
                        out_joff=3 * g * A_WIDTH // dw_tn, **dw_kwargs)
    r_outs = jnp.concatenate([dw_out_a.reshape(N_CHIPS, A_WIDTH // N_CHIPS, d),
                              dw_out_b.reshape(N_CHIPS, B_WIDTH // N_CHIPS, d)], axis=1)

    bulk = [r_big, r_outs]
    recv_b = _pair_send_other_half(bulk, name="reduce_pair_send")
    halves_b = [_add_my_half(s, p, core_arr, BF16, name=f"reduce_pair_add_{i}")
                for i, (s, p) in enumerate(zip(bulk, recv_b))]
    send_sems, recv_sems, sent_b, lands_b, token = _chip_exchange_start(halves_b, name="reduce_exchange_start")

    dhn_a = _matmul(dz_a, w_in_a, "nt", F32, name="a_proj_dx_z", b_koff=z_blk_a, after=token)
    dhn_more = []
    for g, dil in enumerate(A_DILATIONS):
        tk_dx = 3 * A_WIDTH
        if dil == 1:
            dhn_a = _matmul(dqkvs[g], w_in_a, "nt", F32, name=f"a_proj_dx_{g}", add=dhn_a, tk=tk_dx, b_koff=g,
                            after=token)
        else:
            part = _matmul(dqkvs[g], w_in_a, "nt", BF16, name=f"a_proj_dx_{g}", tk=tk_dx, b_koff=g, after=token)
            dhn_more.append(_natural(part.reshape(bl, dil, t // dil, d)))
    grad_x, dg_a_pre = _rms_bwd(h0, g_a_pre, dhn_a, F32, name="a_pre_norm_bwd", adds=(dh1,),
                                dy_more=tuple(dhn_more))

    dw_up = jnp.concatenate([dw_up_k.reshape(B_KV_LORA, B_HEADS, LANES)[:, :, :B_NOPE],
                             dw_up_v.reshape(B_KV_LORA, B_HEADS, B_VDIM)], axis=2).reshape(B_KV_LORA, -1)
    dw_q_up = dw_q_up_p.reshape(B_Q_LORA, B_HEADS, LANES)[:, :, :B_QK_DIM].reshape(B_Q_LORA, -1)
    dw_down = jnp.concatenate([dw_down_p[:, :B_KV_LORA], dw_down_p[:, B_KV_LORA + B_NOPE:B_KV_LORA + B_NOPE + B_ROPE]], axis=1)
    dw_in_b = jnp.concatenate([dw_cq, dw_z], axis=1)
    vec_rep = [dg_kvn.reshape(-1), dg_lat.reshape(-1), dg_b_pre.reshape(-1), dg_b_q.reshape(-1),
               dg_b_post.reshape(-1), loss_part.reshape(-1)]
    vec_shapes = [(dq4,), (dq4,)] + [v.shape for v in vec_rep]
    down_c = dw_down.reshape(N_CHIPS, dq4, -1)
    up_c = _to_chunks_cols(dw_up)
    inb_c = _to_chunks_cols(dw_in_b)
    qup_c = _to_chunks_cols(dw_q_up)
    small_chunks = []
    for k in range(N_CHIPS):
        vecs = [dg_a_pre.reshape(-1)[k * dq4:(k + 1) * dq4], dg_a_post.reshape(-1)[k * dq4:(k + 1) * dq4]] + vec_rep
        small_chunks.append(_pack_rows([down_c[k], up_c[k], inb_c[k], qup_c[k]] + vecs, 32))
    r_small = jnp.stack(small_chunks)

    recv_s = _pair_send_other_half([r_small], name="reduce_pair_send_small")
    halves_s = [_add_my_half(r_small, recv_s[0], core_arr, F32, name="reduce_pair_add_small")]
    parts_s = list(_chip_exchange(halves_s, name="reduce_exchange_small"))
    parts_b = _chip_exchange_wait(send_sems, recv_sems, sent_b, lands_b, grad_x, name="reduce_exchange_wait")
    sums = [_sum_chips(p, own, chip_arr, name=f"reduce_chip_sum_{i}")
            for i, (p, own) in enumerate(zip(parts_b + parts_s, sent_b + halves_s))]
    others = _pair_swap(sums, name="reduce_pair_swap")
    g_big, g_outs_r, g_small_r = [_join_halves(m, o, core_arr, name=f"reduce_join_{i}")
                                  for i, (m, o) in enumerate(zip(sums, others))]

    grads = {}
    grads["a_w_in"] = g_big
    grads["a_w_out"] = g_outs_r[:A_WIDTH // N_CHIPS]
    grads["b_w_out"] = g_outs_r[A_WIDTH // N_CHIPS:]
    small_out_shapes = [down_c.shape[1:], up_c.shape[1:], inb_c.shape[1:], qup_c.shape[1:]] + vec_shapes
    (grads["kv_w_down"], grads["kv_w_up"], grads["b_w_in"], grads["b_w_q_up"], grads["a_pre_norm"],
     grads["a_post_norm"], grads["kv_norm"], grads["kv_latent_norm"], grads["b_pre_norm"], grads["b_q_norm"],
     grads["b_post_norm"], loss_sum) = _unpack(g_small_r.reshape(-1), small_out_shapes)

    weights = dict(a_pre_norm=a_pre_norm, a_w_in=a_w_in, a_w_out=a_w_out, a_post_norm=a_post_norm, kv_norm=kv_norm,
                   kv_w_down=kv_w_down, kv_latent_norm=kv_latent_norm, kv_w_up=kv_w_up, b_pre_norm=b_pre_norm,
                   b_w_in=b_w_in, b_q_norm=b_q_norm, b_w_q_up=b_w_q_up, b_w_out=b_w_out, b_post_norm=b_post_norm)
    names = list(weights)
    out_g, out_d, out_m, out_v = [], [], [], []
    for i, nm in enumerate(names):
        w = weights[nm]
        two_d = (1, w.shape[0]) if w.ndim == 1 else (w.shape[-2], w.shape[-1])
        gw = grads[nm].reshape(two_d)
        dlt, new_m, new_v = _adamw(w.reshape(two_d), gw, moments[i].reshape(two_d),
                                   moments[len(names) + i].reshape(two_d), name=f"adamw_{nm}")
        out_g.append(gw.reshape(w.shape))
        out_d.append(dlt.reshape(w.shape))
        out_m.append(new_m.reshape(w.shape))
        out_v.append(new_v.reshape(w.shape))
    return (loss_sum.reshape(()), grad_x.reshape(bl, t, d), *out_g, *out_d, *out_m, *out_v)


def kernel(x, positions, a_pre_norm, a_w_in, a_w_out, a_post_norm, kv_norm, kv_w_down, kv_latent_norm, kv_w_up, b_pre_norm, b_w_in, b_q_norm, b_w_q_up, b_w_out, b_post_norm, loss_target, m_a_pre_norm, m_a_w_in, m_a_w_out, m_a_post_norm, m_kv_norm, m_kv_w_down, m_kv_latent_norm, m_kv_w_up, m_b_pre_norm, m_b_w_in, m_b_q_norm, m_b_w_q_up, m_b_w_out, m_b_post_norm, v_a_pre_norm, v_a_w_in, v_a_w_out, v_a_post_norm, v_kv_norm, v_kv_w_down, v_kv_latent_norm, v_kv_w_up, v_b_pre_norm, v_b_w_in, v_b_q_norm, v_b_w_q_up, v_b_w_out, v_b_post_norm):
    moments = (m_a_pre_norm, m_a_w_in, m_a_w_out, m_a_post_norm, m_kv_norm, m_kv_w_down, m_kv_latent_norm, m_kv_w_up,
               m_b_pre_norm, m_b_w_in, m_b_q_norm, m_b_w_q_up, m_b_w_out, m_b_post_norm,
               v_a_pre_norm, v_a_w_in, v_a_w_out, v_a_post_norm, v_kv_norm, v_kv_w_down, v_kv_latent_norm, v_kv_w_up,
               v_b_pre_norm, v_b_w_in, v_b_q_norm, v_b_w_q_up, v_b_w_out, v_b_post_norm)
    return _train_step(x, positions, a_pre_norm, a_w_in, a_w_out, a_post_norm, kv_norm, kv_w_down, kv_latent_norm,
                       kv_w_up, b_pre_norm, b_w_in, b_q_norm, b_w_q_up, b_w_out, b_post_norm, loss_target, moments)
```

```python
import math

import jax
import jax.numpy as jnp
from jax import lax
from jax.experimental import pallas as pl
from jax.experimental.pallas import tpu as pltpu

F32 = jnp.float32
BF16 = jnp.bfloat16
MESH = pl.DeviceIdType.MESH

NORM_EPS = 1e-6
NEG = -1e30
LANES = 128
VMEM_LIMIT = 56 * 1024 * 1024
LOG2E = math.log2(math.e)
LN2 = math.log(2.0)

A_GROUPS = 3
A_DILATIONS = (1, 4, 16)
A_HEADS = 8
A_HEAD_DIM = 128
A_WIDTH = A_HEADS * A_HEAD_DIM
A_ROPE_THETA = 500000.0
A_IN_WIDTH = A_GROUPS * 3 * A_WIDTH + A_WIDTH
A_SCALE = A_HEAD_DIM ** -0.5

B_HEADS = 16
B_NOPE = 64
B_ROPE = 32
B_QK_DIM = B_NOPE + B_ROPE
B_VDIM = 64
B_WIDTH = B_HEADS * B_VDIM
B_Q_LORA = 384
B_KV_LORA = 256
B_ROPE_THETA = 10000.0
B_SCALE = B_QK_DIM ** -0.5

ADAM_LR = 0.001
ADAM_B1 = 0.9
ADAM_B2 = 0.999
ADAM_EPS = 1e-08
ADAM_WD = 0.01
ADAM_STEP = 10

N_CHIPS = 4
PACK_COLS = 512


def _params(sem=None):
    return pltpu.CompilerParams(dimension_semantics=sem, vmem_limit_bytes=VMEM_LIMIT)


def _tile(n, want):
    t = min(n, want)
    assert n % t == 0, (n, want)
    return t


def _row_tile(n, want):
    for t in range(min(n, want), 0, -1):
        if n % t == 0 and (t % 16 == 0 or t == n):
            return t
    return n


def _rope_tables(positions, theta, lane0):
    half = 16
    inv_freq = 1.0 / (theta ** (jnp.arange(half, dtype=F32) * (2.0 / (2 * half))))
    n = positions.size
    per_row = LANES // half
    pos = jnp.repeat(positions.astype(F32).reshape(n // per_row, per_row), half, axis=1)
    ang = pos * jnp.tile(inv_freq, per_row)
    cos, sin = lax.optimization_barrier((jnp.cos(ang), jnp.sin(ang)))
    cos, sin = cos.reshape(n, half), sin.reshape(n, half)
    pre = jnp.zeros((n, lane0), F32)
    post = jnp.zeros((n, LANES - lane0 - 2 * half), F32)
    z16 = jnp.zeros((n, half), F32)
    c = jnp.concatenate([pre + 1.0, cos, cos, post + 1.0], axis=1)
    sa = jnp.concatenate([pre, -sin, z16, post], axis=1)
    sb = jnp.concatenate([pre, z16, sin, post], axis=1)
    return lax.optimization_barrier((c, sa, sb))


def _rope_apply(x, c, sa, sb, sign):
    k = x.shape[1] // LANES
    if k > 1:
        c, sa, sb = (jnp.concatenate([t] * k, axis=1) for t in (c, sa, sb))
    w = x.shape[1]
    up = pltpu.roll(x, w - 16, 1)
    dn = pltpu.roll(x, 16, 1)
    if sign > 0:
        return x * c + up * sa + dn * sb
    return x * c - up * sa - dn * sb


def _matmul(a, b, mode, out_dtype, *, name, tm=None, tn=1024, tk=None, add=None, rope=None,
            out_scale=None, b_koff=0, b_cols=None, out_into=None, out_full=None, out_joff=0,
            out_chunk_blocks=None, after=None):
    if mode == "nn":
        m, k = a.shape
        n = b.shape[1]
    elif mode == "nt":
        m, k = a.shape
        n = b.shape[0]
    else:
        k, m = a.shape
        n = b.shape[1]
    b_j0 = 0
    if b_cols is not None:
        tn = _tile(n, tn)
        b_j0, n = b_cols[0], b_cols[1] * tn
    if tm is None:
        tm = 512 if (rope is not None or mode == "nt") else (2048 if k <= 512 else 1024)
    if tk is None:
        tk = 3072 if mode == "nt" else 1024
    tm, tn, tk = _tile(m, tm), _tile(n, tn), _tile(k, tk)
    nk = k // tk
    if mode == "nn":
        a_spec = pl.BlockSpec((tm, tk), lambda j, i, kk: (i, kk))
        b_spec = pl.BlockSpec((tk, tn), lambda j, i, kk: (kk, j + b_j0))
        dims = (((1,), (0,)), ((), ()))
    elif mode == "nt":
        a_spec = pl.BlockSpec((tm, tk), lambda j, i, kk: (i, kk))
        b_spec = pl.BlockSpec((tn, tk), lambda j, i, kk: (j, kk + b_koff))
        dims = (((1,), (1,)), ((), ()))
    else:
        a_spec = pl.BlockSpec((tk, tm), lambda j, i, kk: (kk, i))
        b_spec = pl.BlockSpec((tk, tn), lambda j, i, kk: (kk, j))
        dims = (((0,), (0,)), ((), ()))
    operands = [a, b]
    in_specs = [a_spec, b_spec]
    if add is not None:
        operands.append(add)
        in_specs.append(pl.BlockSpec((tm, tn), lambda j, i, kk: (i, j)))
    if rope is not None:
        tables, rope_pred = rope
        for t in tables:
            operands.append(t)
            in_specs.append(pl.BlockSpec((tm, LANES), lambda j, i, kk: (i, 0)))
    aliases = {}
    if out_into is not None:
        aliases = {len(operands): 0}
        operands.append(out_into)
        in_specs.append(pl.BlockSpec(memory_space=pl.ANY))
        out_shape = jax.ShapeDtypeStruct(out_into.shape, out_into.dtype)
    elif out_full is not None:
        out_shape = jax.ShapeDtypeStruct(out_full, out_dtype)
    else:
        out_shape = jax.ShapeDtypeStruct((m, n), out_dtype)
    if after is not None:
        operands.append(after)
        in_specs.append(pl.BlockSpec(memory_space=pl.ANY))
    if out_chunk_blocks is not None:
        out_spec = pl.BlockSpec((None, tm, tn), lambda j, i, kk: ((j + out_joff) // out_chunk_blocks, i,
                                                                  (j + out_joff) % out_chunk_blocks))
    else:
        out_spec = pl.BlockSpec((tm, tn), lambda j, i, kk: (i, j + out_joff))

    def body(*refs):
        a_ref, b_ref = refs[0], refs[1]
        pos = 2
        add_ref = None
        if add is not None:
            add_ref = refs[pos]
            pos += 1
        tab_refs = None
        if rope is not None:
            tab_refs = refs[pos:pos + 3]
            pos += 3
        if out_into is not None:
            pos += 1
        if after is not None:
            pos += 1
        o_ref = refs[pos]
        acc_ref = refs[pos + 1] if nk > 1 else None

        def finish(res):
            if add_ref is not None:
                res = res + add_ref[...].astype(F32)
            if tab_refs is None:
                o_ref[...] = res.astype(o_ref.dtype)
                return
            j = pl.program_id(0)
            flag = rope_pred(j)

            def rotated():
                roped = _rope_apply(res, tab_refs[0][...], tab_refs[1][...], tab_refs[2][...], 1)
                if out_scale is not None:
                    value, scale_pred = out_scale
                    use = scale_pred(j)
                    roped = roped * (value if use is True else jnp.where(use, value, 1.0))
                o_ref[...] = roped.astype(o_ref.dtype)

            if flag is True:
                rotated()
                return
            pl.when(flag)(rotated)

            @pl.when(jnp.logical_not(flag))
            def _():
                o_ref[...] = res.astype(o_ref.dtype)

        part = lax.dot_general(a_ref[...].astype(BF16), b_ref[...].astype(BF16), dims,
                               preferred_element_type=F32)
        if nk == 1:
            finish(part)
            return
        kk = pl.program_id(2)

        @pl.when(kk == 0)
        def _():
            acc_ref[...] = part

        @pl.when(kk > 0)
        def _():
            acc_ref[...] += part

        @pl.when(kk == nk - 1)
        def _():
            finish(acc_ref[...])

    return pl.pallas_call(
        body, name=name, grid=(n // tn, m // tm, nk), in_specs=in_specs, out_specs=out_spec,
        out_shape=out_shape, input_output_aliases=aliases,
        scratch_shapes=[pltpu.VMEM((tm, tn), F32)] if nk > 1 else [],
        compiler_params=_params(("parallel", "parallel", "arbitrary")),
    )(*operands)


def _rms_fwd(x, g, out_dtype, *, name, add=None, tr=512):
    n, d = x.shape
    tr = _tile(n, tr)
    row = pl.BlockSpec((tr, d), lambda i: (i, 0))
    vec = pl.BlockSpec((1, d), lambda i: (0, 0))

    def body(*refs):
        x_ref, g_ref = refs[0], refs[1]
        o_ref = refs[-1]
        xv = x_ref[...].astype(F32)
        r = lax.rsqrt(jnp.mean(xv * xv, axis=-1, keepdims=True) + NORM_EPS)
        y = xv * r * g_ref[...]
        if add is not None:
            y = refs[2][...] + y
        o_ref[...] = y.astype(o_ref.dtype)

    ops = [x, g] + ([add] if add is not None else [])
    specs = [row, vec] + ([row] if add is not None else [])
    return pl.pallas_call(
        body, name=name, grid=(n // tr,), in_specs=specs, out_specs=row,
        out_shape=jax.ShapeDtypeStruct((n, d), out_dtype), compiler_params=_params(("parallel",)),
    )(*ops)


def _rms_bwd(x, g, dy, out_dtype, *, name, adds=(), dy_more=(), tr=512):
    n, d = x.shape
    tr = _tile(n, tr)
    steps = n // tr
    row = pl.BlockSpec((tr, d), lambda i: (i, 0))
    vec = pl.BlockSpec((1, d), lambda i: (0, 0))
    na = len(adds) + len(dy_more)

    def body(*refs):
        x_ref, g_ref, dy_ref = refs[:3]
        add_refs = refs[3:3 + len(adds)]
        more_refs = refs[3 + len(adds):3 + na]
        dx_ref, dg_ref, acc_ref = refs[3 + na:]
        i = pl.program_id(0)
        xv = x_ref[...].astype(F32)
        r = lax.rsqrt(jnp.mean(xv * xv, axis=-1, keepdims=True) + NORM_EPS)
        xh = xv * r
        dyv = dy_ref[...].astype(F32)
        for m_ref in more_refs:
            dyv = dyv + m_ref[...].astype(F32)
        part = (dyv * xh).reshape(tr // 8, 8, d).sum(axis=0)

        @pl.when(i == 0)
        def _():
            acc_ref[...] = part

        @pl.when(i > 0)
        def _():
            acc_ref[...] += part

        t = dyv * g_ref[...]
        dx = r * (t - xh * jnp.mean(t * xh, axis=-1, keepdims=True))
        for a_ref in add_refs:
            dx = dx + a_ref[...].astype(F32)
        dx_ref[...] = dx.astype(dx_ref.dtype)

        @pl.when(i == steps - 1)
        def _():
            dg_ref[...] = jnp.sum(acc_ref[...], axis=0, keepdims=True)

    return pl.pallas_call(
        body, name=name, grid=(steps,), in_specs=[row, vec, row] + [row] * na,
        out_specs=(row, vec),
        out_shape=(jax.ShapeDtypeStruct((n, d), out_dtype), jax.ShapeDtypeStruct((1, d), F32)),
        scratch_shapes=[pltpu.VMEM((8, d), F32)], compiler_params=_params(("arbitrary",)),
    )(x, g, dy, *adds, *dy_more)


def _rms(xv, g):
    return xv * lax.rsqrt(jnp.mean(xv * xv, axis=-1, keepdims=True) + NORM_EPS) * g


def _post_norm_block(y, g, h_in, next_gains, *, name, tr=512):
    n, d = y.shape
    tr = _tile(n, tr)
    nk = len(next_gains)
    row = pl.BlockSpec((tr, d), lambda i: (i, 0))
    vec = pl.BlockSpec((1, d), lambda i: (0, 0))

    def body(*refs):
        y_ref, g_ref, h_ref = refs[:3]
        gk_refs = refs[3:3 + nk]
        o_ref = refs[3 + nk]
        hn_refs = refs[4 + nk:]
        h = h_ref[...] + _rms(y_ref[...], g_ref[...])
        o_ref[...] = h
        for gk_ref, hn_ref in zip(gk_refs, hn_refs):
            hn_ref[...] = _rms(h, gk_ref[...]).astype(BF16)

    return pl.pallas_call(
        body, name=name, grid=(n // tr,), in_specs=[row, vec, row] + [vec] * nk,
        out_specs=(row,) * (1 + nk),
        out_shape=(jax.ShapeDtypeStruct((n, d), F32),) + (jax.ShapeDtypeStruct((n, d), BF16),) * nk,
        compiler_params=_params(("parallel",)),
    )(y, g, h_in, *next_gains)


def _post_norm_loss(y, g, h_in, target, *, tr=512):
    n, d = y.shape
    tr = _tile(n, tr)
    steps = n // tr
    row = pl.BlockSpec((tr, d), lambda i: (i, 0))

    def body(y_ref, g_ref, h_ref, t_ref, dh_ref, loss_ref, acc_ref):
        i = pl.program_id(0)
        e = h_ref[...] + _rms(y_ref[...], g_ref[...]) - t_ref[...]
        dh_ref[...] = e / d
        part = (e * e).reshape(tr // 8, 8, d).sum(axis=0)

        @pl.when(i == 0)
        def _():
            acc_ref[...] = part

        @pl.when(i > 0)
        def _():
            acc_ref[...] += part

        @pl.when(i == steps - 1)
        def _():
            s = jnp.sum(jnp.sum(acc_ref[...], axis=-1, keepdims=True), axis=0, keepdims=True)
            loss_ref[...] = 0.5 * s / d

    return pl.pallas_call(
        body, name="b_post_norm_loss", grid=(steps,),
        in_specs=[row, pl.BlockSpec((1, d), lambda i: (0, 0)), row, row],
        out_specs=(row, pl.BlockSpec((1, 1), lambda i: (0, 0))),
        out_shape=(jax.ShapeDtypeStruct((n, d), F32), jax.ShapeDtypeStruct((1, 1), F32)),
        scratch_shapes=[pltpu.VMEM((8, d), F32)], compiler_params=_params(("arbitrary",)),
    )(y, g, h_in, target)


def _rms_bwd_pair(x, g1, dy1, g2, dy2, add, *, name, tr=512):
    n, d = x.shape
    tr = _tile(n, tr)
    steps = n // tr
    row = pl.BlockSpec((tr, d), lambda i: (i, 0))
    vec = pl.BlockSpec((1, d), lambda i: (0, 0))

    def body(x_ref, g1_ref, d1_ref, g2_ref, d2_ref, add_ref, dx_ref, dg1_ref, dg2_ref, acc_ref):
        i = pl.program_id(0)
        xv = x_ref[...]
        r = lax.rsqrt(jnp.mean(xv * xv, axis=-1, keepdims=True) + NORM_EPS)
        xh = xv * r
        dx = add_ref[...]
        for k, (g_ref, d_ref) in enumerate(((g1_ref, d1_ref), (g2_ref, d2_ref))):
            dyv = d_ref[...].astype(F32)
            part = (dyv * xh).reshape(tr // 8, 8, d).sum(axis=0)

            @pl.when(i == 0)
            def _(part=part, k=k):
                acc_ref[k] = part

            @pl.when(i > 0)
            def _(part=part, k=k):
                acc_ref[k] += part

            t = dyv * g_ref[...]
            dx = dx + r * (t - xh * jnp.mean(t * xh, axis=-1, keepdims=True))
        dx_ref[...] = dx

        @pl.when(i == steps - 1)
        def _():
            dg1_ref[...] = jnp.sum(acc_ref[0], axis=0, keepdims=True)
            dg2_ref[...] = jnp.sum(acc_ref[1], axis=0, keepdims=True)

    return pl.pallas_call(
        body, name=name, grid=(steps,), in_specs=[row, vec, row, vec, row, row],
        out_specs=(row, vec, vec),
        out_shape=(jax.ShapeDtypeStruct((n, d), F32), jax.ShapeDtypeStruct((1, d), F32),
                   jax.ShapeDtypeStruct((1, d), F32)),
        scratch_shapes=[pltpu.VMEM((2, 8, d), F32)], compiler_params=_params(("arbitrary",)),
    )(x, g1, dy1, g2, dy2, add)


def _kv_latent_fwd(ckr, g_lat, tabs, *, tr=512):
    n = ckr.shape[0]
    tr = _tile(n, tr)
    lat = B_KV_LORA

    def body(c_ref, k_ref, g_ref, tc, tsa, tsb, ckv_ref, kr_ref):
        xv = c_ref[...]
        r = lax.rsqrt(jnp.mean(xv * xv, axis=-1, keepdims=True) + NORM_EPS)
        ckv_ref[...] = (xv * r * g_ref[...]).astype(BF16)
        kr_ref[...] = _rope_apply(k_ref[...], tc[...], tsa[...], tsb[...], 1).astype(BF16)

    tab = pl.BlockSpec((tr, LANES), lambda i: (i, 0))
    return pl.pallas_call(
        body, name="kv_latent_fwd", grid=(n // tr,),
        in_specs=[pl.BlockSpec((tr, lat), lambda i: (i, 0)),
                  pl.BlockSpec((tr, LANES), lambda i: (i, lat // LANES)),
                  pl.BlockSpec((1, lat), lambda i: (0, 0)), tab, tab, tab],
        out_specs=(pl.BlockSpec((tr, lat), lambda i: (i, 0)), tab),
        out_shape=(jax.ShapeDtypeStruct((n, lat), BF16), jax.ShapeDtypeStruct((n, LANES), BF16)),
        compiler_params=_params(("parallel",)),
    )(ckr, ckr, g_lat, *tabs)


def _kv_latent_bwd(dckv, ckr, g_lat, dk_cat, tabs, *, tr=512):
    n = ckr.shape[0]
    tr = _tile(n, tr)
    steps = n // tr
    lat = B_KV_LORA
    wk = dk_cat.shape[1]

    def body(d_ref, c_ref, g_ref, dk_ref, tc, tsa, tsb, o_ref, dg_ref, acc_ref):
        i = pl.program_id(0)
        xv = c_ref[...]
        r = lax.rsqrt(jnp.mean(xv * xv, axis=-1, keepdims=True) + NORM_EPS)
        xh = xv * r
        dyv = d_ref[...]
        part = (dyv * xh).reshape(tr // 8, 8, lat).sum(axis=0)

        @pl.when(i == 0)
        def _():
            acc_ref[...] = part

        @pl.when(i > 0)
        def _():
            acc_ref[...] += part

        t = dyv * g_ref[...]
        dx = r * (t - xh * jnp.mean(t * xh, axis=-1, keepdims=True))
        o_ref[:, 0:lat] = dx.astype(o_ref.dtype)
        dkr = dk_ref[:, 0:LANES].astype(F32)
        for h in range(1, wk // LANES):
            dkr = dkr + dk_ref[:, h * LANES:(h + 1) * LANES].astype(F32)
        o_ref[:, lat:lat + LANES] = _rope_apply(dkr, tc[...], tsa[...], tsb[...], -1).astype(o_ref.dtype)

        @pl.when(i == steps - 1)
        def _():
            dg_ref[...] = jnp.sum(acc_ref[...], axis=0, keepdims=True)

    tab = pl.BlockSpec((tr, LANES), lambda i: (i, 0))
    return pl.pallas_call(
        body, name="kv_latent_bwd", grid=(steps,),
        in_specs=[pl.BlockSpec((tr, lat), lambda i: (i, 0)), pl.BlockSpec((tr, lat), lambda i: (i, 0)),
                  pl.BlockSpec((1, lat), lambda i: (0, 0)), pl.BlockSpec((tr, wk), lambda i: (i, 0)),
                  tab, tab, tab],
        out_specs=(pl.BlockSpec((tr, lat + LANES), lambda i: (i, 0)), pl.BlockSpec((1, lat), lambda i: (0, 0))),
        out_shape=(jax.ShapeDtypeStruct((n, lat + LANES), BF16), jax.ShapeDtypeStruct((1, lat), F32)),
        scratch_shapes=[pltpu.VMEM((8, lat), F32)], compiler_params=_params(("arbitrary",)),
    )(dckv, ckr, g_lat, dk_cat, *tabs)


def _sigmoid(z):
    return 1.0 / (1.0 + jnp.exp(-z))


def _lane_place(cols, width):
    rows = cols[0].shape[0]
    lane = lax.broadcasted_iota(jnp.int32, (rows, width), 1)
    out = jnp.zeros((rows, width), F32)
    for h, col in enumerate(cols):
        out = jnp.where(lane == h, col, out)
    return out


def _merge_gate_fwd(outs, lses, proj, z_block, *, tr=256):
    n, w = outs[0].shape
    tr = _tile(n, tr)
    ng = len(outs)

    def body(*refs):
        o_refs = refs[:ng]
        l_refs = refs[ng:2 * ng]
        z_ref = refs[2 * ng]
        y_ref, om_ref, lse_ref = refs[2 * ng + 1:]
        ls = [r[...] for r in l_refs]
        mx = ls[0]
        for l in ls[1:]:
            mx = jnp.maximum(mx, l)
        ssum = jnp.exp2(ls[0] - mx)
        for l in ls[1:]:
            ssum = ssum + jnp.exp2(l - mx)
        tot = mx + jnp.log2(ssum)
        lse_ref[...] = tot
        ws = [jnp.exp2(l - tot) for l in ls]
        for h in range(A_HEADS):
            sl = slice(h * A_HEAD_DIM, (h + 1) * A_HEAD_DIM)
            o = ws[0][:, h:h + 1] * o_refs[0][:, sl]
            for gi in range(1, ng):
                o = o + ws[gi][:, h:h + 1] * o_refs[gi][:, sl]
            z = z_ref[:, sl].astype(F32)
            om_ref[:, sl] = o.astype(BF16)
            y_ref[:, sl] = (o * (z * _sigmoid(z))).astype(BF16)

    row = pl.BlockSpec((tr, w), lambda i: (i, 0))
    lrow = pl.BlockSpec((tr, A_HEADS), lambda i: (i, 0))
    return pl.pallas_call(
        body, name="merge_gate_fwd", grid=(n // tr,),
        in_specs=[row] * ng + [lrow] * ng + [pl.BlockSpec((tr, w), lambda i: (i, z_block))],
        out_specs=(row, row, lrow),
        out_shape=(jax.ShapeDtypeStruct((n, w), BF16), jax.ShapeDtypeStruct((n, w), BF16),
                   jax.ShapeDtypeStruct((n, A_HEADS), F32)),
        compiler_params=_params(("parallel",)),
    )(*outs, *lses, proj)


def _gate_bwd(dy, o, z_arr, z_block, *, name, with_delta, tr=256):
    n, w = dy.shape
    tr = _tile(n, tr)

    def body(*refs):
        dy_ref, o_ref, z_ref, do_ref, dz_ref = refs[:5]
        dyv = dy_ref[...].astype(F32)
        ov = o_ref[...].astype(F32)
        z = z_ref[...].astype(F32)
        sig = _sigmoid(z)
        do = dyv * (z * sig)
        do_ref[...] = do.astype(BF16)
        dz_ref[...] = (dyv * ov * (sig * (1.0 + z * (1.0 - sig)))).astype(BF16)
        if with_delta:
            prod = do * ov
            cols = [jnp.sum(prod[:, h * A_HEAD_DIM:(h + 1) * A_HEAD_DIM], axis=-1, keepdims=True)
                    for h in range(A_HEADS)]
            refs[5][...] = _lane_place(cols, A_HEADS)

    row = pl.BlockSpec((tr, w), lambda i: (i, 0))
    out_specs = [row, row]
    out_shape = [jax.ShapeDtypeStruct((n, w), BF16), jax.ShapeDtypeStruct((n, w), BF16)]
    if with_delta:
        out_specs.append(pl.BlockSpec((tr, A_HEADS), lambda i: (i, 0)))
        out_shape.append(jax.ShapeDtypeStruct((n, A_HEADS), F32))
    return pl.pallas_call(
        body, name=name, grid=(n // tr,),
        in_specs=[row, row, pl.BlockSpec((tr, w), lambda i: (i, z_block))],
        out_specs=tuple(out_specs), out_shape=tuple(out_shape), compiler_params=_params(("parallel",)),
    )(dy, o, z_arr)


def _dot_nt(a, b):
    return lax.dot_general(a, b, (((1,), (1,)), ((), ())), preferred_element_type=F32)


def _dot_nn(a, b):
    return lax.dot_general(a, b, (((1,), (0,)), ((), ())), preferred_element_type=F32)


def _attn_a_fwd(qkv, cb0, qb, out_dtype, *, name):
    bl, dil, ln, _ = qkv.shape
    nb = ln // qb
    hw = A_WIDTH
    heads = range(A_HEADS)
    sls = [slice(h * A_HEAD_DIM, (h + 1) * A_HEAD_DIM) for h in heads]

    def body(*refs):
        if nb > 1:
            q_ref, kc_ref, vc_ref, kp_ref, vp_ref, o_ref, lse_ref = refs
        else:
            q_ref, kc_ref, vc_ref, o_ref, lse_ref = refs
        i = pl.program_id(2)
        qi = lax.broadcasted_iota(jnp.int32, (qb, qb), 0)
        ki = lax.broadcasted_iota(jnp.int32, (qb, qb), 1)
        mask_c = ki <= qi
        mask_p = jnp.logical_and(ki >= qi, i >= 1)
        s_c = [jnp.where(mask_c, _dot_nt(q_ref[:, sls[h]], kc_ref[:, sls[h]]), NEG) for h in heads]
        m = [jnp.max(s_c[h], axis=-1, keepdims=True) for h in heads]
        if nb > 1:
            s_p = [jnp.where(mask_p, _dot_nt(q_ref[:, sls[h]], kp_ref[:, sls[h]]), NEG) for h in heads]
            m = [jnp.maximum(m[h], jnp.max(s_p[h], axis=-1, keepdims=True)) for h in heads]
        p_c = [jnp.exp2(s_c[h] - m[h]) for h in heads]
        l = [jnp.sum(p_c[h], axis=-1, keepdims=True) for h in heads]
        acc = [_dot_nn(p_c[h].astype(BF16), vc_ref[:, sls[h]]) for h in heads]
        if nb > 1:
            p_p = [jnp.exp2(s_p[h] - m[h]) for h in heads]
            l = [l[h] + jnp.sum(p_p[h], axis=-1, keepdims=True) for h in heads]
            acc = [acc[h] + _dot_nn(p_p[h].astype(BF16), vp_ref[:, sls[h]]) for h in heads]
        for h in heads:
            o_ref[:, sls[h]] = (acc[h] / l[h]).astype(o_ref.dtype)
        lse_ref[...] = _lane_place([m[h] + jnp.log2(l[h]) for h in heads], A_HEADS)

    def spec(off, prev):
        if prev:
            return pl.BlockSpec((None, None, qb, hw), lambda b, r, i: (b, r, jnp.maximum(i - 1, 0), cb0 + off))
        return pl.BlockSpec((None, None, qb, hw), lambda b, r, i: (b, r, i, cb0 + off))

    return pl.pallas_call(
        body, name=name, grid=(bl, dil, nb),
        in_specs=[spec(0, False), spec(1, False), spec(2, False)] + ([spec(1, True), spec(2, True)] if nb > 1 else []),
        out_specs=(pl.BlockSpec((None, None, qb, hw), lambda b, r, i: (b, r, i, 0)),
                   pl.BlockSpec((None, None, qb, A_HEADS), lambda b, r, i: (b, r, i, 0))),
        out_shape=(jax.ShapeDtypeStruct((bl, dil, ln, hw), out_dtype),
                   jax.ShapeDtypeStruct((bl, dil, ln, A_HEADS), F32)),
        compiler_params=_params(("parallel", "parallel", "arbitrary")),
    )(*([qkv] * (5 if nb > 1 else 3)))


def _attn_a_bwd(qkv, cb0, do, lse, delta, lse_t, delta_t, tabs, qb, *, name):
    bl, dil, ln, _ = qkv.shape
    nb = ln // qb
    hw = A_WIDTH

    def body(*refs):
        if nb > 1:
            (q_ref, kc_ref, vc_ref, do_ref, lse_ref, dl_ref, lt_ref, dt_ref, tc, tsa, tsb,
             qn_ref, kp_ref, vp_ref, don_ref, ltn_ref, dtn_ref, o_ref) = refs
        else:
            q_ref, kc_ref, vc_ref, do_ref, lse_ref, dl_ref, lt_ref, dt_ref, tc, tsa, tsb, o_ref = refs
        i = pl.program_id(2)
        row = lax.broadcasted_iota(jnp.int32, (qb, qb), 0)
        col = lax.broadcasted_iota(jnp.int32, (qb, qb), 1)
        m_qc = col <= row
        m_kc = row <= col
        m_qp = jnp.logical_and(col >= row, i >= 1)
        m_kn = jnp.logical_and(row >= col, i + 1 < nb)
        c, sa, sb = tc[...], tsa[...], tsb[...]
        heads = range(A_HEADS)
        sls = [slice(h * A_HEAD_DIM, (h + 1) * A_HEAD_DIM) for h in heads]
        q, kc = [q_ref[:, sl] for sl in sls], [kc_ref[:, sl] for sl in sls]
        vc, dov = [vc_ref[:, sl] for sl in sls], [do_ref[:, sl] for sl in sls]
        lse_c = [lse_ref[:, h:h + 1] for h in heads]
        dl_c = [dl_ref[:, h:h + 1] for h in heads]
        s = [_dot_nt(q[h], kc[h]) for h in heads]
        st = [_dot_nt(kc[h], q[h]) for h in heads]
        dp = [_dot_nt(dov[h], vc[h]) for h in heads]
        dpt = [_dot_nt(vc[h], dov[h]) for h in heads]
        p = [jnp.exp2(jnp.where(m_qc, s[h], NEG) - lse_c[h]) for h in heads]
        pt = [jnp.exp2(jnp.where(m_kc, st[h], NEG) - lt_ref[h:h + 1, :]) for h in heads]
        dq = [_dot_nn((p[h] * (dp[h] - dl_c[h])).astype(BF16), kc[h]) for h in heads]
        dk = [_dot_nn((pt[h] * (dpt[h] - dt_ref[h:h + 1, :])).astype(BF16), q[h]) for h in heads]
        dv = [_dot_nn(pt[h].astype(BF16), dov[h]) for h in heads]
        if nb > 1:
            kp, vp = [kp_ref[:, sl] for sl in sls], [vp_ref[:, sl] for sl in sls]
            qn, don = [qn_ref[:, sl] for sl in sls], [don_ref[:, sl] for sl in sls]
            s = [_dot_nt(q[h], kp[h]) for h in heads]
            st = [_dot_nt(kc[h], qn[h]) for h in heads]
            dp = [_dot_nt(dov[h], vp[h]) for h in heads]
            dpt = [_dot_nt(vc[h], don[h]) for h in heads]
            p = [jnp.exp2(jnp.where(m_qp, s[h], NEG) - lse_c[h]) for h in heads]
            pt = [jnp.exp2(jnp.where(m_kn, st[h], NEG) - ltn_ref[h:h + 1, :]) for h in heads]
            dq = [dq[h] + _dot_nn((p[h] * (dp[h] - dl_c[h])).astype(BF16), kp[h]) for h in heads]
            dk = [dk[h] + _dot_nn((pt[h] * (dpt[h] - dtn_ref[h:h + 1, :])).astype(BF16), qn[h]) for h in heads]
            dv = [dv[h] + _dot_nn(pt[h].astype(BF16), don[h]) for h in heads]
        for h in heads:
            o_ref[:, h * A_HEAD_DIM:(h + 1) * A_HEAD_DIM] = _rope_apply(dq[h] * A_SCALE, c, sa, sb, -1).astype(BF16)
            o_ref[:, hw + h * A_HEAD_DIM:hw + (h + 1) * A_HEAD_DIM] = _rope_apply(dk[h] * LN2, c, sa, sb, -1).astype(BF16)
            o_ref[:, 2 * hw + h * A_HEAD_DIM:2 * hw + (h + 1) * A_HEAD_DIM] = dv[h].astype(BF16)

    def cur(w, col):
        return pl.BlockSpec((None, None, qb, w), lambda b, r, i: (b, r, i, col))

    def prev(w, col):
        return pl.BlockSpec((None, None, qb, w), lambda b, r, i: (b, r, jnp.maximum(i - 1, 0), col))

    def nxt(w, col):
        return pl.BlockSpec((None, None, qb, w), lambda b, r, i: (b, r, jnp.minimum(i + 1, nb - 1), col))

    t_cur = pl.BlockSpec((None, None, A_HEADS, qb), lambda b, r, i: (b, r, 0, i))
    t_nxt = pl.BlockSpec((None, None, A_HEADS, qb), lambda b, r, i: (b, r, 0, jnp.minimum(i + 1, nb - 1)))
    in_specs = [cur(hw, cb0), cur(hw, cb0 + 1), cur(hw, cb0 + 2), cur(hw, 0), cur(A_HEADS, 0), cur(A_HEADS, 0),
                t_cur, t_cur, cur(LANES, 0), cur(LANES, 0), cur(LANES, 0)]
    operands = [qkv, qkv, qkv, do, lse, delta, lse_t, delta_t, *tabs]
    if nb > 1:
        in_specs += [nxt(hw, cb0), prev(hw, cb0 + 1), prev(hw, cb0 + 2), nxt(hw, 0), t_nxt, t_nxt]
        operands += [qkv, qkv, qkv, do, lse_t, delta_t]
    return pl.pallas_call(
        body, name=name, grid=(bl, dil, nb), in_specs=in_specs, out_specs=cur(3 * hw, 0),
        out_shape=jax.ShapeDtypeStruct((bl, dil, ln, 3 * hw), BF16),
        compiler_params=_params(("parallel", "parallel", "arbitrary")),
    )(*operands)


def _head_terms(do, o, lse, e):
    rows = do.shape[0]
    lane = lax.broadcasted_iota(jnp.int32, (rows, LANES), 1)
    mine = (lane < B_VDIM) if e == 0 else (lane >= B_VDIM)
    prod = do.astype(F32) * o.astype(F32)
    dl = jnp.sum(jnp.where(mine, prod, 0.0), axis=-1, keepdims=True)
    do_e = jnp.where(mine, do, jnp.zeros_like(do))
    return do_e, dl, lse[:, e * B_VDIM:e * B_VDIM + 1]


def _col_to_row(col, rows):
    return jnp.transpose(jnp.broadcast_to(col, (rows, LANES)))[0:1, :]


def _mla_fwd(q_cat, kvup, kr, z, tq):
    bl, t, _ = q_cat.shape
    nq = t // tq
    pairs = B_HEADS // 2
    v_blk0 = (B_HEADS * LANES) // LANES

    def body(q_ref, k_ref, v_ref, kr_ref, z_ref, y_ref, o_ref, lse_ref, lrow_ref, m_ref, acc_ref):
        qi = pl.program_id(2)
        qs = [q_ref[:, e * LANES:(e + 1) * LANES] for e in range(2)]
        row = lax.broadcasted_iota(jnp.int32, (tq, tq), 0)
        col = lax.broadcasted_iota(jnp.int32, (tq, tq), 1)
        tri = col <= row
        sum_lane = [B_VDIM, 0]

        for e in range(2):
            m_ref[e] = jnp.full((tq, LANES), NEG, F32)
            acc_ref[e] = jnp.zeros((tq, LANES), F32)

        def tile(k0, w, masked):
            lane = lax.broadcasted_iota(jnp.int32, (w, LANES), 1)
            first = lane < B_VDIM
            krv = kr_ref[pl.ds(k0, w), :]
            v = v_ref[pl.ds(k0, w), :]
            vs = [jnp.where(first, v, jnp.where(lane == B_VDIM, 1.0, 0.0).astype(BF16)),
                  jnp.where(first, jnp.where(lane == 0, 1.0, 0.0).astype(BF16), v)]
            ss = []
            for e in range(2):
                k = k_ref[pl.ds(k0, w), e * LANES:(e + 1) * LANES] + krv
                s = _dot_nt(qs[e], k)
                if masked:
                    r = lax.broadcasted_iota(jnp.int32, (tq, w), 0)
                    c = lax.broadcasted_iota(jnp.int32, (tq, w), 1)
                    s = jnp.where(c <= r + (w - tq), s, NEG)
                ss.append(s)
            for e in range(2):
                m_old = m_ref[e]
                m_new = jnp.maximum(m_old, jnp.max(ss[e], axis=-1, keepdims=True))
                p = jnp.exp2(ss[e] - jnp.concatenate([m_new] * (w // LANES), axis=1)).astype(BF16)
                m_ref[e] = m_new
                acc_ref[e] = jnp.exp2(m_old - m_new) * acc_ref[e] + _dot_nn(p, vs[e])

        def step(kb2, carry):
            tile(pl.multiple_of(kb2 * 2 * tq, 2 * tq), 2 * tq, False)
            return carry

        lax.fori_loop(0, qi // 2, step, 0)

        @pl.when(qi % 2 == 1)
        def _():
            tile(pl.multiple_of((qi - 1) * tq, tq), 2 * tq, True)

        @pl.when(qi % 2 == 0)
        def _():
            tile(pl.multiple_of(qi * tq, tq), tq, True)
        lane = lax.broadcasted_iota(jnp.int32, (tq, LANES), 1)
        first = lane < B_VDIM
        accs = [acc_ref[e] for e in range(2)]
        ls = [accs[e][:, sum_lane[e]:sum_lane[e] + 1] for e in range(2)]
        outs = [accs[e] / ls[e] for e in range(2)]
        lses = [m_ref[e] + jnp.log2(ls[e]) for e in range(2)]
        o = jnp.where(first, outs[0], outs[1])
        zv = z_ref[...].astype(F32)
        o_ref[...] = o.astype(BF16)
        y_ref[...] = (o * (zv * _sigmoid(zv))).astype(BF16)
        lse_ref[...] = jnp.where(first, lses[0], lses[1])
        for e in range(2):
            lrow_ref[e:e + 1, :] = jnp.transpose(lses[e])[0:1, :]

    blk = pl.BlockSpec((None, tq, LANES), lambda b, j, i: (b, i, j))
    return pl.pallas_call(
        body, name="mla_fwd", grid=(bl, pairs, nq),
        in_specs=[pl.BlockSpec((None, tq, 2 * LANES), lambda b, j, i: (b, i, j)),
                  pl.BlockSpec((None, t, 2 * LANES), lambda b, j, i: (b, 0, j)),
                  pl.BlockSpec((None, t, LANES), lambda b, j, i: (b, 0, v_blk0 + j)),
                  pl.BlockSpec((None, t, LANES), lambda b, j, i: (b, 0, 0)),
                  blk],
        out_specs=(blk, blk, blk, pl.BlockSpec((None, None, None, 2, tq), lambda b, j, i: (b, j, i, 0, 0))),
        out_shape=(jax.ShapeDtypeStruct((bl, t, B_WIDTH), BF16), jax.ShapeDtypeStruct((bl, t, B_WIDTH), BF16),
                   jax.ShapeDtypeStruct((bl, t, B_WIDTH), F32),
                   jax.ShapeDtypeStruct((bl, pairs, nq, 2, tq), F32)),
        scratch_shapes=[pltpu.VMEM((2, tq, LANES), F32), pltpu.VMEM((2, tq, LANES), F32)],
        compiler_params=_params(("parallel", "parallel", "arbitrary")),
    )(q_cat, kvup, kvup, kr, z)


def _mla_dq(q_cat, kvup, kr, do, o, lse, tabs, tq):
    bl, t, _ = q_cat.shape
    nq = t // tq
    pairs = B_HEADS // 2
    v_blk0 = (B_HEADS * LANES) // LANES

    def body(q_ref, k_ref, v_ref, kr_ref, do_ref, o_ref, lse_ref, tc, tsa, tsb, dq_ref, drow_ref, acc_ref):
        qi = pl.program_id(2)
        dov, ov, lsev = do_ref[...], o_ref[...], lse_ref[...]
        qs = [q_ref[:, e * LANES:(e + 1) * LANES] for e in range(2)]
        terms = [_head_terms(dov, ov, lsev, e) for e in range(2)]
        row = lax.broadcasted_iota(jnp.int32, (tq, tq), 0)
        col = lax.broadcasted_iota(jnp.int32, (tq, tq), 1)
        tri = col <= row
        for e in range(2):
            acc_ref[e] = jnp.zeros((tq, LANES), F32)

        def tile(k0, w, masked):
            krv = kr_ref[pl.ds(k0, w), :]
            v = v_ref[pl.ds(k0, w), :]
            ks = [k_ref[pl.ds(k0, w), e * LANES:(e + 1) * LANES] + krv for e in range(2)]
            ss = [_dot_nt(qs[e], ks[e]) for e in range(2)]
            dps = [_dot_nt(terms[e][0], v) for e in range(2)]
            for e in range(2):
                s = ss[e]
                if masked:
                    r = lax.broadcasted_iota(jnp.int32, (tq, w), 0)
                    c = lax.broadcasted_iota(jnp.int32, (tq, w), 1)
                    s = jnp.where(c <= r + (w - tq), s, NEG)
                p = jnp.exp2(s - terms[e][2])
                ds = (p * (dps[e] - terms[e][1])).astype(BF16)
                acc_ref[e] += _dot_nn(ds, ks[e])

        def step(kb2, carry):
            tile(pl.multiple_of(kb2 * 2 * tq, 2 * tq), 2 * tq, False)
            return carry

        lax.fori_loop(0, qi // 2, step, 0)

        @pl.when(qi % 2 == 1)
        def _():
            tile(pl.multiple_of((qi - 1) * tq, tq), 2 * tq, True)

        @pl.when(qi % 2 == 0)
        def _():
            tile(pl.multiple_of(qi * tq, tq), tq, True)

        for e in range(2):
            dq_ref[:, e * LANES:(e + 1) * LANES] = _rope_apply(acc_ref[e] * B_SCALE, tc[...], tsa[...], tsb[...], -1).astype(BF16)
            drow_ref[e:e + 1, :] = _col_to_row(terms[e][1], tq)

    blk = pl.BlockSpec((None, tq, LANES), lambda b, j, i: (b, i, j))
    tab = pl.BlockSpec((None, tq, LANES), lambda b, j, i: (b, i, 0))
    qblk = pl.BlockSpec((None, tq, 2 * LANES), lambda b, j, i: (b, i, j))
    return pl.pallas_call(
        body, name="mla_dq", grid=(bl, pairs, nq),
        in_specs=[qblk,
                  pl.BlockSpec((None, t, 2 * LANES), lambda b, j, i: (b, 0, j)),
                  pl.BlockSpec((None, t, LANES), lambda b, j, i: (b, 0, v_blk0 + j)),
                  pl.BlockSpec((None, t, LANES), lambda b, j, i: (b, 0, 0)),
                  blk, blk, blk, tab, tab, tab],
        out_specs=(qblk, pl.BlockSpec((None, None, None, 2, tq), lambda b, j, i: (b, j, i, 0, 0))),
        out_shape=(jax.ShapeDtypeStruct((bl, t, B_HEADS * LANES), BF16),
                   jax.ShapeDtypeStruct((bl, pairs, nq, 2, tq), F32)),
        scratch_shapes=[pltpu.VMEM((2, tq, LANES), F32)],
        compiler_params=_params(("parallel", "parallel", "arbitrary")),
    )(q_cat, kvup, kvup, kr, do, o, lse, *tabs)


def _mla_dkv(q_cat, kvup, kr, do, lse_rows, delta_rows, tq):
    bl, t, _ = q_cat.shape
    nq = t // tq
    pairs = B_HEADS // 2
    v_blk0 = (B_HEADS * LANES) // LANES

    def body(q_ref, k_ref, v_ref, kr_ref, do_ref, lrow_ref, drow_ref, dk_ref, dv_ref, acc_ref):
        kb = pl.program_id(2)
        v = v_ref[...]
        krv = kr_ref[...]
        ks = [k_ref[:, e * LANES:(e + 1) * LANES] + krv for e in range(2)]
        krow = lax.broadcasted_iota(jnp.int32, (tq, tq), 0)
        qcol = lax.broadcasted_iota(jnp.int32, (tq, tq), 1)
        tri = krow <= qcol
        lane = lax.broadcasted_iota(jnp.int32, (tq, LANES), 1)
        mine = [lane < B_VDIM, lane >= B_VDIM]

        for e in range(3):
            acc_ref[e] = jnp.zeros((tq, LANES), F32)

        def tile(qb, nblk, masked):
            w = nblk * tq
            rows = pl.ds(pl.multiple_of(qb * tq, tq), w)
            dov = do_ref[rows, :]
            lane_w = lax.broadcasted_iota(jnp.int32, (w, LANES), 1)
            mine_w = [lane_w < B_VDIM, lane_w >= B_VDIM]
            qs = [q_ref[rows, e * LANES:(e + 1) * LANES] for e in range(2)]
            does = [jnp.where(mine_w[e], dov, jnp.zeros_like(dov)) for e in range(2)]
            sts = [_dot_nt(ks[e], qs[e]) for e in range(2)]
            dpts = [_dot_nt(v, does[e]) for e in range(2)]

            def rows_of(ref, e):
                return jnp.concatenate([ref[qb + i, e:e + 1, :] for i in range(nblk)], axis=1)

            pts = []
            for e in range(2):
                st = sts[e]
                if masked:
                    r = lax.broadcasted_iota(jnp.int32, (tq, w), 0)
                    c = lax.broadcasted_iota(jnp.int32, (tq, w), 1)
                    st = jnp.where(r <= c, st, NEG)
                pts.append(jnp.exp2(st - rows_of(lrow_ref, e)))
            acc_ref[2] += _dot_nn(pts[0].astype(BF16), does[0]) + _dot_nn(pts[1].astype(BF16), does[1])
            for e in range(2):
                dst = (pts[e] * (dpts[e] - rows_of(drow_ref, e))).astype(BF16)
                acc_ref[e] += _dot_nn(dst, qs[e])

        rest = nq - 1 - kb
        odd = rest % 2

        @pl.when(odd == 1)
        def _():
            tile(kb, 2, True)

        @pl.when(odd == 0)
        def _():
            tile(kb, 1, True)

        def step(i, carry):
            tile(kb + 1 + odd + 2 * i, 2, False)
            return carry

        lax.fori_loop(0, rest // 2, step, 0)
        dk_ref[:, 0:LANES] = (acc_ref[0] * LN2).astype(BF16)
        dk_ref[:, LANES:2 * LANES] = (acc_ref[1] * LN2).astype(BF16)
        dv_ref[...] = acc_ref[2].astype(BF16)

    full = pl.BlockSpec((None, t, LANES), lambda b, j, i: (b, 0, j))
    rows = pl.BlockSpec((None, None, nq, 2, tq), lambda b, j, i: (b, j, 0, 0, 0))
    kblk = pl.BlockSpec((None, tq, 2 * LANES), lambda b, j, i: (b, i, j))
    return pl.pallas_call(
        body, name="mla_dkv", grid=(bl, pairs, nq),
        in_specs=[pl.BlockSpec((None, t, 2 * LANES), lambda b, j, i: (b, 0, j)),
                  kblk,
                  pl.BlockSpec((None, tq, LANES), lambda b, j, i: (b, i, v_blk0 + j)),
                  pl.BlockSpec((None, tq, LANES), lambda b, j, i: (b, i, 0)),
                  full, rows, rows],
        out_specs=(kblk, pl.BlockSpec((None, tq, LANES), lambda b, j, i: (b, i, j))),
        out_shape=(jax.ShapeDtypeStruct((bl, t, B_HEADS * LANES), BF16),
                   jax.ShapeDtypeStruct((bl, t, B_WIDTH), BF16)),
        scratch_shapes=[pltpu.VMEM((3, tq, LANES), F32)],
        compiler_params=_params(("parallel", "parallel", "arbitrary")),
    )(q_cat, kvup, kvup, kr, do, lse_rows, delta_rows)


def _adamw(w, g, m, v, *, name):
    r, c = w.shape
    tr = _row_tile(r, 256)
    c1 = 1.0 - ADAM_B1
    c2 = 1.0 - ADAM_B2
    bc1 = 1.0 - ADAM_B1 ** ADAM_STEP
    bc2 = 1.0 - ADAM_B2 ** ADAM_STEP

    def body(w_ref, g_ref, m_ref, v_ref, d_ref, nm_ref, nv_ref):
        gv = g_ref[...]
        nm = ADAM_B1 * m_ref[...] + c1 * gv
        nv = ADAM_B2 * v_ref[...] + c2 * (gv * gv)
        nm_ref[...] = nm
        nv_ref[...] = nv
        d_ref[...] = -ADAM_LR * ((nm / bc1) / (jnp.sqrt(nv / bc2) + ADAM_EPS) + ADAM_WD * w_ref[...])

    blk = pl.BlockSpec((tr, c), lambda i: (i, 0))
    sds = jax.ShapeDtypeStruct((r, c), F32)
    return pl.pallas_call(
        body, name=name, grid=(r // tr,), in_specs=[blk] * 4, out_specs=(blk,) * 3,
        out_shape=(sds,) * 3, compiler_params=_params(("parallel",)),
    )(w, g, m, v)


def _add_my_half(stacked, other, core, out_dtype, *, name):
    nch, a, c = stacked.shape
    h = a // 2
    tr = _row_tile(h, 256)
    nblk = h // tr

    def body(core_ref, s_ref, p_ref, o_ref):
        o_ref[...] = (s_ref[...] + p_ref[...]).astype(o_ref.dtype)

    return pl.pallas_call(
        body, name=name,
        grid_spec=pltpu.PrefetchScalarGridSpec(
            num_scalar_prefetch=1, grid=(nch, nblk),
            in_specs=[pl.BlockSpec((None, tr, c), lambda k, i, cr: (k, cr[0] * nblk + i, 0)),
                      pl.BlockSpec((None, tr, c), lambda k, i, cr: (k, i, 0))],
            out_specs=pl.BlockSpec((None, tr, c), lambda k, i, cr: (k, i, 0))),
        out_shape=jax.ShapeDtypeStruct((nch, h, c), out_dtype),
        compiler_params=_params(("parallel", "parallel")),
    )(core, stacked, other)


def _sum_chips(parts, own, chip, *, name):
    nch, h, c = parts.shape
    tr = _row_tile(h, 256)

    def body(chip_ref, p_ref, own_ref, o_ref):
        me = chip_ref[0]

        def slot(k):
            return jnp.where(me == k, own_ref[k].astype(F32), p_ref[k].astype(F32))

        acc = slot(0) + slot(1)
        for k in range(2, nch):
            acc = acc + slot(k)
        o_ref[...] = acc

    blk = pl.BlockSpec((nch, tr, c), lambda i, cr: (0, i, 0))
    return pl.pallas_call(
        body, name=name,
        grid_spec=pltpu.PrefetchScalarGridSpec(
            num_scalar_prefetch=1, grid=(h // tr,), in_specs=[blk, blk],
            out_specs=pl.BlockSpec((tr, c), lambda i, cr: (i, 0))),
        out_shape=jax.ShapeDtypeStruct((h, c), F32), compiler_params=_params(("parallel",)),
    )(chip, parts, own)


def _join_halves(mine, other, core, *, name):
    h, c = mine.shape
    tr = _row_tile(h, 256)
    nblk = h // tr

    def body(core_ref, m_ref, s_ref, o_ref):
        is_mine = pl.program_id(0) // nblk == core_ref[0]

        @pl.when(is_mine)
        def _():
            o_ref[...] = m_ref[...]

        @pl.when(jnp.logical_not(is_mine))
        def _():
            o_ref[...] = s_ref[...]

    blk = pl.BlockSpec((tr, c), lambda i, cr: (i % nblk, 0))
    return pl.pallas_call(
        body, name=name,
        grid_spec=pltpu.PrefetchScalarGridSpec(
            num_scalar_prefetch=1, grid=(2 * nblk,), in_specs=[blk, blk],
            out_specs=pl.BlockSpec((tr, c), lambda i, cr: (i, 0))),
        out_shape=jax.ShapeDtypeStruct((2 * h, c), F32), compiler_params=_params(("arbitrary",)),
    )(core, mine, other)


def _place():
    x, y, c = lax.axis_index("x"), lax.axis_index("y"), lax.axis_index("c")
    chips = [(1 - x, y), (x, 1 - y), (1 - x, 1 - y)]
    return x, y, c, chips


def _remote(src, dst, send_sems, recv_sems, k, to):
    return pltpu.make_async_remote_copy(src_ref=src, dst_ref=dst, send_sem=send_sems.at[k],
                                        recv_sem=recv_sems.at[k], device_id=to, device_id_type=MESH)


def _hbm_call(body, name, ins, out_shapes, n_remote):
    any_spec = pl.BlockSpec(memory_space=pl.ANY)
    return pl.pallas_call(
        body, name=name, in_specs=[any_spec] * len(ins), out_specs=tuple([any_spec] * len(out_shapes)),
        out_shape=tuple(out_shapes),
        scratch_shapes=[pltpu.SemaphoreType.DMA((n_remote,)), pltpu.SemaphoreType.DMA((n_remote,))],
    )(*ins)


def _all_gather_chips(shards, *, name):
    n = len(shards)

    def body(*refs):
        ins, outs = refs[:n], refs[n:2 * n]
        send_sems, recv_sems = refs[2 * n:]
        x, y, c, chips = _place()
        me = 2 * x + y
        sent = []
        for s in range(n):
            h = ins[s].shape[0] // 2
            for j, (px, py) in enumerate(chips):
                cp = _remote(ins[s].at[pl.ds(c * h, h)], outs[s].at[me, pl.ds(c * h, h)],
                             send_sems, recv_sems, s * 6 + j, (px, py, c))
                cp.start()
                sent.append(cp)
        for s in range(n):
            h = ins[s].shape[0] // 2
            for j, (px, py) in enumerate(chips):
                slab = outs[s].at[2 * px + py, pl.ds(c * h, h)]
                _remote(slab, slab, send_sems, recv_sems, s * 6 + j, (px, py, c)).wait_recv()
                cp = _remote(slab, slab, send_sems, recv_sems, s * 6 + 3 + j, (x, y, 1 - c))
                cp.start()
                sent.append(cp)
        for s in range(n):
            h = ins[s].shape[0] // 2
            for j, (px, py) in enumerate(chips):
                slab = outs[s].at[2 * px + py, pl.ds((1 - c) * h, h)]
                _remote(slab, slab, send_sems, recv_sems, s * 6 + 3 + j, (x, y, 1 - c)).wait_recv()
        for cp in sent:
            cp.wait_send()

    out_shapes = [jax.ShapeDtypeStruct((N_CHIPS,) + s.shape, s.dtype) for s in shards]
    return _hbm_call(body, name, shards, out_shapes, 6 * n)


def _pair_send_other_half(stacked, *, name):
    n = len(stacked)

    def body(*refs):
        ins, outs = refs[:n], refs[n:2 * n]
        send_sems, recv_sems = refs[2 * n:]
        x, y, c, _chips = _place()
        sent = []
        for s in range(n):
            h = ins[s].shape[1] // 2
            cp = _remote(ins[s].at[:, pl.ds((1 - c) * h, h)], outs[s], send_sems, recv_sems, s, (x, y, 1 - c))
            cp.start()
            sent.append(cp)
        for cp in sent:
            cp.wait_recv()
        for cp in sent:
            cp.wait_send()

    out_shapes = [jax.ShapeDtypeStruct((s.shape[0], s.shape[1] // 2, s.shape[2]), s.dtype) for s in stacked]
    return _hbm_call(body, name, stacked, out_shapes, n)


def _chip_exchange(halves, *, name):
    n = len(halves)

    def body(*refs):
        ins, outs = refs[:n], refs[n:2 * n]
        send_sems, recv_sems = refs[2 * n:]
        x, y, c, chips = _place()
        me = 2 * x + y
        sent = []
        for s in range(n):
            for j, (px, py) in enumerate(chips):
                cp = _remote(ins[s].at[2 * px + py], outs[s].at[me], send_sems, recv_sems, s * 3 + j, (px, py, c))
                cp.start()
                sent.append(cp)
        for s in range(n):
            for j, (px, py) in enumerate(chips):
                slab = outs[s].at[2 * px + py]
                _remote(slab, slab, send_sems, recv_sems, s * 3 + j, (px, py, c)).wait_recv()
        for cp in sent:
            cp.wait_send()

    out_shapes = [jax.ShapeDtypeStruct(s.shape, s.dtype) for s in halves]
    return _hbm_call(body, name, halves, out_shapes, 3 * n)


def _chip_exchange_start(halves, *, name):
    n = len(halves)
    hbm = pl.BlockSpec(memory_space=pltpu.HBM)
    sem = pl.BlockSpec(memory_space=pltpu.SEMAPHORE)

    def body(*refs):
        ins, lands = refs[:n], refs[n:2 * n]
        send_sems, recv_sems = refs[2 * n], refs[2 * n + 1]
        token = refs[-1]
        x, y, c, chips = _place()
        me = 2 * x + y
        for s in range(n):
            for j, (px, py) in enumerate(chips):
                _remote(ins[s].at[2 * px + py], lands[s].at[me], send_sems, recv_sems, s * 3 + j, (px, py, c)).start()
        token[...] = jnp.zeros_like(token)

    slabs = [pltpu.HBM(s.shape, s.dtype) for s in halves]
    outs = pl.pallas_call(
        body, name=name,
        out_shape=(pltpu.SemaphoreType.DMA((3 * n,)), pltpu.SemaphoreType.DMA((3 * n,)), *slabs, *slabs,
                   jax.ShapeDtypeStruct((8, LANES), F32)),
        in_specs=[hbm] * (2 * n), out_specs=(sem, sem, *([hbm] * (2 * n)), pl.BlockSpec(memory_space=pltpu.VMEM)),
        input_output_aliases={i: 2 + i for i in range(2 * n)},
        compiler_params=pltpu.CompilerParams(has_side_effects=pltpu.SideEffectType.DATAFLOW_SIDE_EFFECTING),
    )(*[pltpu.with_memory_space_constraint(s, pltpu.HBM) for s in halves],
      *[pltpu.with_memory_space_constraint(lax.empty(s.shape, s.dtype), pltpu.HBM) for s in halves])
    return outs[0], outs[1], list(outs[2:2 + n]), list(outs[2 + n:2 + 2 * n]), outs[-1]


def _chip_exchange_wait(send_sems, recv_sems, sent, lands, after, *, name):
    n = len(sent)
    hbm = pl.BlockSpec(memory_space=pltpu.HBM)
    sem = pl.BlockSpec(memory_space=pltpu.SEMAPHORE)

    def body(*refs):
        ins, lands_in = refs[:n], refs[n:2 * n]
        send_sems, recv_sems = refs[2 * n], refs[2 * n + 1]
        x, y, c, chips = _place()
        me = 2 * x + y
        for s in range(n):
            for j, (px, py) in enumerate(chips):
                k = 2 * px + py
                _remote(ins[s].at[k], lands_in[s].at[me], send_sems, recv_sems, s * 3 + j, (px, py, c)).wait_send()
                _remote(ins[s].at[k], lands_in[s].at[k], send_sems, recv_sems, s * 3 + j, (px, py, c)).wait_recv()

    slabs = [pltpu.HBM(s.shape, s.dtype) for s in sent]
    outs = pl.pallas_call(
        body, name=name, out_shape=(*slabs, *slabs),
        in_specs=[hbm] * (2 * n) + [sem, sem, pl.BlockSpec(memory_space=pl.ANY)],
        out_specs=tuple([hbm] * (2 * n)), input_output_aliases={i: i for i in range(2 * n)},
        compiler_params=pltpu.CompilerParams(has_side_effects=pltpu.SideEffectType.DATAFLOW_SIDE_EFFECTING),
    )(*sent, *lands, send_sems, recv_sems, after)
    return list(outs[n:])


def _pair_swap(halves, *, name):
    n = len(halves)

    def body(*refs):
        ins, outs = refs[:n], refs[n:2 * n]
        send_sems, recv_sems = refs[2 * n:]
        x, y, c, _chips = _place()
        sent = []
        for s in range(n):
            cp = _remote(ins[s], outs[s], send_sems, recv_sems, s, (x, y, 1 - c))
            cp.start()
            sent.append(cp)
        for cp in sent:
            cp.wait_recv()
        for cp in sent:
            cp.wait_send()

    out_shapes = [jax.ShapeDtypeStruct(s.shape, s.dtype) for s in halves]
    return _hbm_call(body, name, halves, out_shapes, n)


def _pack_rows(parts, row_multiple):
    flat = jnp.concatenate([p.reshape(-1) for p in parts])
    quantum = row_multiple * PACK_COLS
    pad = (-flat.shape[0]) % quantum
    flat = jnp.pad(flat, (0, pad))
    return flat.reshape(-1, PACK_COLS)


def _unpack(flat, shapes):
    out, pos = [], 0
    for shp in shapes:
        size = math.prod(shp)
        out.append(flat[pos:pos + size].reshape(shp))
        pos += size
    return out


def _to_chunks_cols(full):
    r, c4 = full.shape
    return full.reshape(r, N_CHIPS, c4 // N_CHIPS).transpose(1, 0, 2)


def _from_chunks_cols(stacked):
    nch, r, c = stacked.shape
    return stacked.transpose(1, 0, 2).reshape(r, nch * c)


def _class_major(a, bl, t, dil):
    w = a.shape[-1]
    if dil == 1:
        return a.reshape(bl, 1, t, w)
    return a.reshape(bl, t // dil, dil, w).transpose(0, 2, 1, 3)


def _natural(a):
    bl, dil, ln, w = a.shape
    if dil == 1:
        return a.reshape(bl * ln, w)
    return a.transpose(0, 2, 1, 3).reshape(bl * ln * dil, w)


def _train_step(x, positions, a_pre_norm, a_w_in, a_w_out, a_post_norm, kv_norm, kv_w_down, kv_latent_norm,
                kv_w_up, b_pre_norm, b_w_in, b_q_norm, b_w_q_up, b_w_out, b_post_norm, loss_target, moments):
    bl, t, d = x.shape
    n = bl * t
    qb = t // A_DILATIONS[-1]
    tq = _tile(t, 256)
    dq4 = d // N_CHIPS
    chip = 2 * lax.axis_index("x") + lax.axis_index("y")
    chip_arr = chip.astype(jnp.int32).reshape(1)
    core_arr = lax.axis_index("c").astype(jnp.int32).reshape(1)

    w_in_a_s = a_w_in[0].astype(BF16)
    outs_s = jnp.concatenate([a_w_out[0], b_w_out[0]], axis=0).astype(BF16)
    small_shapes = [kv_w_down.shape, kv_w_up.shape, b_w_in[0].shape, b_w_q_up[0].shape]
    small_s = _pack_rows([kv_w_down, kv_w_up, b_w_in[0], b_w_q_up[0]], 32).astype(BF16)
    gains_s = jnp.pad(jnp.concatenate([a_pre_norm[0], a_post_norm[0]]), (0, 16 * LANES - 2 * dq4)).reshape(16, LANES)
    shards = [w_in_a_s, outs_s, small_s, gains_s]
    gathered = _all_gather_chips(shards, name="gather_weights")
    g_in_a, g_outs, g_small, g_gains = [lax.dynamic_update_index_in_dim(g, s, chip, 0)
                                        for g, s in zip(gathered, shards)]

    w_in_a = _from_chunks_cols(g_in_a)
    w_out_a = g_outs[:, :A_WIDTH // N_CHIPS].reshape(A_WIDTH, d)
    w_out_b = g_outs[:, A_WIDTH // N_CHIPS:].reshape(B_WIDTH, d)
    sm = [_unpack(g_small[k].reshape(-1), small_shapes) for k in range(N_CHIPS)]
    w_down = jnp.concatenate([sm[k][0] for k in range(N_CHIPS)], axis=0)
    w_up = jnp.concatenate([sm[k][1] for k in range(N_CHIPS)], axis=1)
    w_in_b = jnp.concatenate([sm[k][2] for k in range(N_CHIPS)], axis=1)
    w_q_up = jnp.concatenate([sm[k][3] for k in range(N_CHIPS)], axis=1)
    gflat = g_gains.reshape(N_CHIPS, -1)
    g_a_pre = gflat[:, :dq4].reshape(1, d)
    g_a_post = gflat[:, dq4:2 * dq4].reshape(1, d)

    w_up_h = w_up.reshape(B_KV_LORA, B_HEADS, B_NOPE + B_VDIM)
    w_up_k = jnp.pad(w_up_h[:, :, :B_NOPE], ((0, 0), (0, 0), (0, LANES - B_NOPE))).reshape(B_KV_LORA, B_HEADS * LANES)
    w_up_v = w_up_h[:, :, B_NOPE:].reshape(B_KV_LORA, B_WIDTH)
    w_up_cat = jnp.concatenate([w_up_k, w_up_v], axis=1)
    w_q_up_p = jnp.pad(w_q_up.reshape(B_Q_LORA, B_HEADS, B_QK_DIM),
                       ((0, 0), (0, 0), (0, LANES - B_QK_DIM))).reshape(B_Q_LORA, B_HEADS * LANES)
    zeros_d = lambda c: jnp.zeros((d, c), BF16)
    w_down_p = jnp.concatenate([w_down[:, :B_KV_LORA], zeros_d(B_NOPE), w_down[:, B_KV_LORA:],
                                zeros_d(LANES - B_NOPE - B_ROPE)], axis=1)
    w_cq = w_in_b[:, :B_Q_LORA]
    w_z = w_in_b[:, B_Q_LORA:]

    tabs_a = _rope_tables(positions, A_ROPE_THETA, 0)
    tabs_b = _rope_tables(positions, B_ROPE_THETA, B_NOPE)

    h0 = x.reshape(n, d)
    hn_a = _rms_fwd(h0, g_a_pre, BF16, name="a_pre_norm")
    is_qk = lambda j: j != 2
    is_q = lambda j: j == 0
    z_blk_a = 3 * A_GROUPS
    z_a = _matmul(hn_a, w_in_a, "nn", BF16, name="a_proj_z", b_cols=(z_blk_a, 1))
    o_groups, lse_groups, qkv_cm, hn_cm, tabs_cm = [], [], [], [], []
    for g, dil in enumerate(A_DILATIONS):
        flat = lambda a: _class_major(a, bl, t, dil).reshape(n, a.shape[-1])
        hn_g = hn_a if dil == 1 else flat(hn_a)
        tabs_g = tabs_a if dil == 1 else lax.optimization_barrier(tuple(flat(tb) for tb in tabs_a))
        proj_g = _matmul(hn_g, w_in_a, "nn", BF16, name=f"a_proj_{g}", rope=(tabs_g, is_qk),
                         out_scale=(A_SCALE * LOG2E, is_q), b_cols=(3 * g, 3))
        src = proj_g.reshape(bl, dil, t // dil, 3 * A_WIDTH)
        hn_cm.append(hn_g)
        tabs_cm.append(tabs_g)
        qkv_cm.append(src)
        o_g, lse_g = _attn_a_fwd(src, 0, qb, BF16, name=f"attn_a_fwd_{g}")
        o_groups.append(_natural(o_g))
        lse_groups.append(_natural(lse_g))
    ypre_a, om_a, lse_a = _merge_gate_fwd(o_groups, lse_groups, z_a, 0)
    y_a = _matmul(ypre_a, w_out_a, "nn", F32, name="a_out")
    g_kvn = kv_norm.reshape(1, d)
    g_lat = kv_latent_norm.reshape(1, B_KV_LORA)
    h1, hn_kv, hn_b = _post_norm_block(y_a, g_a_post, h0, [g_kvn, b_pre_norm], name="a_post_norm")

    ckr = _matmul(hn_kv, w_down_p, "nn", F32, name="kv_down")
    c_kv, k_rope = _kv_latent_fwd(ckr, g_lat, tabs_b)
    kvup = _matmul(c_kv, w_up_cat, "nn", BF16, name="kv_up")
    z_b = _matmul(hn_b, w_z, "nn", BF16, name="b_proj_z")
    cq_raw = _matmul(hn_b, w_cq, "nn", F32, name="b_proj_q")
    c_q = _rms_fwd(cq_raw, b_q_norm, BF16, name="b_q_norm")
    always = lambda j: True
    q_cat = _matmul(c_q, w_q_up_p, "nn", BF16, name="b_q_up", rope=(tabs_b, always),
                    out_scale=(B_SCALE * LOG2E, always))
    r3 = lambda a: a.reshape(bl, t, a.shape[-1])
    tabs_b3 = tuple(r3(tb) for tb in tabs_b)
    ypre_b, o_b, lse_b, lse_rows_b = _mla_fwd(r3(q_cat), r3(kvup), r3(k_rope), r3(z_b), tq)
    y_b = _matmul(ypre_b.reshape(n, B_WIDTH), w_out_b, "nn", F32, name="b_out")
    dh2, loss_part = _post_norm_loss(y_b, b_post_norm, h1, loss_target.reshape(n, d))

    dy_b, dg_b_post = _rms_bwd(y_b, b_post_norm, dh2, BF16, name="b_post_norm_bwd")
    dypre_b = _matmul(dy_b, w_out_b, "nt", BF16, name="b_out_dx")
    dw_out_b = _matmul(ypre_b.reshape(n, B_WIDTH), dy_b, "tn", F32, name="b_out_dw", tm=1024, tk=2048)
    do_b, dz_b = _gate_bwd(dypre_b, o_b.reshape(n, B_WIDTH), z_b, 0, name="b_gate_bwd", with_delta=False)
    dq_cat, delta_rows_b = _mla_dq(r3(q_cat), r3(kvup), r3(k_rope), r3(do_b), o_b, lse_b, tabs_b3, tq)
    dq_cat = dq_cat.reshape(n, -1)
    dk_cat, dv_b = _mla_dkv(r3(q_cat), r3(kvup), r3(k_rope), r3(do_b), lse_rows_b, delta_rows_b, tq)
    dk_cat, dv_b = dk_cat.reshape(n, -1), dv_b.reshape(n, -1)
    dcq_n = _matmul(dq_cat, w_q_up_p, "nt", F32, name="b_q_up_dx")
    dw_q_up_p = _matmul(c_q, dq_cat, "tn", F32, name="b_q_up_dw", tm=1024, tk=2048)
    dcq, dg_b_q = _rms_bwd(cq_raw, b_q_norm, dcq_n, BF16, name="b_q_norm_bwd")
    dhn_b = _matmul(dz_b, w_z, "nt", F32, name="b_proj_z_dx")
    dhn_b = _matmul(dcq, w_cq, "nt", F32, name="b_proj_q_dx", add=dhn_b)
    dw_z = _matmul(hn_b, dz_b, "tn", F32, name="b_proj_z_dw", tm=1024, tk=2048)
    dw_cq = _matmul(hn_b, dcq, "tn", F32, name="b_proj_q_dw", tm=1024, tk=2048)
    dckv_n = _matmul(dk_cat, w_up_k, "nt", F32, name="kv_up_k_dx")
    dckv_n = _matmul(dv_b, w_up_v, "nt", F32, name="kv_up_v_dx", add=dckv_n)
    dw_up_k = _matmul(c_kv, dk_cat, "tn", F32, name="kv_up_k_dw", tm=1024, tk=2048)
    dw_up_v = _matmul(c_kv, dv_b, "tn", F32, name="kv_up_v_dw", tm=1024, tk=2048)
    dckr, dg_lat = _kv_latent_bwd(dckv_n, ckr, g_lat, dk_cat, tabs_b)
    dhn_kv = _matmul(dckr, w_down_p, "nt", F32, name="kv_down_dx")
    dw_down_p = _matmul(hn_kv, dckr, "tn", F32, name="kv_down_dw", tm=1024, tk=2048)
    dh1, dg_b_pre, dg_kvn = _rms_bwd_pair(h1, b_pre_norm, dhn_b, g_kvn, dhn_kv, dh2, name="h1_norms_bwd")

    dy_a, dg_a_post = _rms_bwd(y_a, g_a_post, dh1, BF16, name="a_post_norm_bwd")
    dypre_a = _matmul(dy_a, w_out_a, "nt", BF16, name="a_out_dx")
    dw_out_a = _matmul(ypre_a, dy_a, "tn", F32, name="a_out_dw", tm=1024, tk=2048)
    do_a, dz_a, delta_a = _gate_bwd(dypre_a, om_a, z_a, 0, name="a_gate_bwd", with_delta=True)
    dw_cols = A_IN_WIDTH // N_CHIPS
    dw_tn = _tile(dw_cols, 512)
    dw_kwargs = dict(tm=1024, tn=dw_tn, tk=2048, out_chunk_blocks=dw_cols // dw_tn)
    r_big = _matmul(hn_a, dz_a, "tn", F32, name="a_proj_dw_z", out_full=(N_CHIPS, d, dw_cols),
                    out_joff=z_blk_a * A_WIDTH // dw_tn, **dw_kwargs)
    dqkvs = []
    for g, dil in enumerate(A_DILATIONS):
        cm = lambda a: _class_major(a, bl, t, dil)
        swap = lambda a: jnp.swapaxes(a, 2, 3)
        lse_cm, delta_cm = cm(lse_a), cm(delta_a)
        tabs_g = tuple(tb.reshape(bl, dil, t // dil, LANES) for tb in tabs_cm[g])
        dqkv = _attn_a_bwd(qkv_cm[g], 0, cm(do_a), lse_cm, delta_cm, swap(lse_cm), swap(delta_cm),
                           tabs_g, qb, name=f"attn_a_bwd_{g}").reshape(n, 3 * A_WIDTH)
        dqkvs.append(dqkv)
        r_big = _matmul(hn_cm[g], dqkv, "tn", F32, name=f"a_proj_dw_{g}", out_into=r_big,
                        out_joff=3 * g * A_WIDTH // dw_tn, **dw_kwargs)
    r_outs = jnp.concatenate([dw_out_a.reshape(N_CHIPS, A_WIDTH // N_CHIPS, d),
                              dw_out_b.reshape(N_CHIPS, B_WIDTH // N_CHIPS, d)], axis=1)

    bulk = [r_big, r_outs]
    recv_b = _pair_send_other_half(bulk, name="reduce_pair_send")
    halves_b = [_add_my_half(s, p, core_arr, BF16, name=f"reduce_pair_add_{i}")
                for i, (s, p) in enumerate(zip(bulk, recv_b))]
    send_sems, recv_sems, sent_b, lands_b, token = _chip_exchange_start(halves_b, name="reduce_exchange_start")

    dhn_a = _matmul(dz_a, w_in_a, "nt", F32, name="a_proj_dx_z", b_koff=z_blk_a, after=token)
    dhn_more = []
    for g, dil in enumerate(A_DILATIONS):
        tk_dx = 3 * A_WIDTH
        if dil == 1:
            dhn_a = _matmul(dqkvs[g], w_in_a, "nt", F32, name=f"a_proj_dx_{g}", add=dhn_a, tk=tk_dx, b_koff=g,
                            after=token)
        else:
            part = _matmul(dqkvs[g], w_in_a, "nt", BF16, name=f"a_proj_dx_{g}", tk=tk_dx, b_koff=g, after=token)
            dhn_more.append(_natural(part.reshape(bl, dil, t // dil, d)))
    grad_x, dg_a_pre = _rms_bwd(h0, g_a_pre, dhn_a, F32, name="a_pre_norm_bwd", adds=(dh1,),
                                dy_more=tuple(dhn_more))

    dw_up = jnp.concatenate([dw_up_k.reshape(B_KV_LORA, B_HEADS, LANES)[:, :, :B_NOPE],
                             dw_up_v.reshape(B_KV_LORA, B_HEADS, B_VDIM)], axis=2).reshape(B_KV_LORA, -1)
    dw_q_up = dw_q_up_p.reshape(B_Q_LORA, B_HEADS, LANES)[:, :, :B_QK_DIM].reshape(B_Q_LORA, -1)
    dw_down = jnp.concatenate([dw_down_p[:, :B_KV_LORA], dw_down_p[:, B_KV_LORA + B_NOPE:B_KV_LORA + B_NOPE + B_ROPE]], axis=1)
    dw_in_b = jnp.concatenate([dw_cq, dw_z], axis=1)
    vec_rep = [dg_kvn.reshape(-1), dg_lat.reshape(-1), dg_b_pre.reshape(-1), dg_b_q.reshape(-1),
               dg_b_post.reshape(-1), loss_part.reshape(-1)]
    vec_shapes = [(dq4,), (dq4,)] + [v.shape for v in vec_rep]
    down_c = dw_down.reshape(N_CHIPS, dq4, -1)
    up_c = _to_chunks_cols(dw_up)
    inb_c = _to_chunks_cols(dw_in_b)
    qup_c = _to_chunks_cols(dw_q_up)
    small_chunks = []
    for k in range(N_CHIPS):
        vecs = [dg_a_pre.reshape(-1)[k * dq4:(k + 1) * dq4], dg_a_post.reshape(-1)[k * dq4:(k + 1) * dq4]] + vec_rep
        small_chunks.append(_pack_rows([down_c[k], up_c[k], inb_c[k], qup_c[k]] + vecs, 32))
    r_small = jnp.stack(small_chunks)

    recv_s = _pair_send_other_half([r_small], name="reduce_pair_send_small")
    halves_s = [_add_my_half(r_small, recv_s[0], core_arr, F32, name="reduce_pair_add_small")]
    parts_s = list(_chip_exchange(halves_s, name="reduce_exchange_small"))
    parts_b = _chip_exchange_wait(send_sems, recv_sems, sent_b, lands_b, grad_x, name="reduce_exchange_wait")
    sums = [_sum_chips(p, own, chip_arr, name=f"reduce_chip_sum_{i}")
            for i, (p, own) in enumerate(zip(parts_b + parts_s, sent_b + halves_s))]
    others = _pair_swap(sums, name="reduce_pair_swap")
    g_big, g_outs_r, g_small_r = [_join_halves(m, o, core_arr, name=f"reduce_join_{i}")
                                  for i, (m, o) in enumerate(zip(sums, others))]

    grads = {}
    grads["a_w_in"] = g_big
    grads["a_w_out"] = g_outs_r[:A_WIDTH // N_CHIPS]
    grads["b_w_out"] = g_outs_r[A_WIDTH // N_CHIPS:]
    small_out_shapes = [down_c.shape[1:], up_c.shape[1:], inb_c.shape[1:], qup_c.shape[1:]] + vec_shapes
    (grads["kv_w_down"], grads["kv_w_up"], grads["b_w_in"], grads["b_w_q_up"], grads["a_pre_norm"],
     grads["a_post_norm"], grads["kv_norm"], grads["kv_latent_norm"], grads["b_pre_norm"], grads["b_q_norm"],
     grads["b_post_norm"], loss_sum) = _unpack(g_small_r.reshape(-1), small_out_shapes)

    weights = dict(a_pre_norm=a_pre_norm, a_w_in=a_w_in, a_w_out=a_w_out, a_post_norm=a_post_norm, kv_norm=kv_norm,
                   kv_w_down=kv_w_down, kv_latent_norm=kv_latent_norm, kv_w_up=kv_w_up, b_pre_norm=b_pre_norm,
                   b_w_in=b_w_in, b_q_norm=b_q_norm, b_w_q_up=b_w_q_up, b_w_out=b_w_out, b_post_norm=b_post_norm)
    names = list(weights)
    out_g, out_d, out_m, out_v = [], [], [], []
    for i, nm in enumerate(names):
        w = weights[nm]
        two_d = (1, w.shape[0]) if w.ndim == 1 else (w.shape[-2], w.shape[-1])
        gw = grads[nm].reshape(two_d)
        dlt, new_m, new_v = _adamw(w.reshape(two_d), gw, moments[i].reshape(two_d),
                                   moments[len(names) + i].reshape(two_d), name=f"adamw_{nm}")
        out_g.append(gw.reshape(w.shape))
        out_d.append(dlt.reshape(w.shape))
        out_m.append(new_m.reshape(w.shape))
        out_v.append(new_v.reshape(w.shape))
    return (loss_sum.reshape(()), grad_x.reshape(bl, t, d), *out_g, *out_d, *out_m, *out_v)


def kernel(x, positions, a_pre_norm, a_w_in, a_w_out, a_post_norm, kv_norm, kv_w_down, kv_latent_norm, kv_w_up, b_pre_norm, b_w_in, b_q_norm, b_w_q_up, b_w_out, b_post_norm, loss_target, m_a_pre_norm, m_a_w_in, m_a_w_out, m_a_post_norm, m_kv_norm, m_kv_w_down, m_kv_latent_norm, m_kv_w_up, m_b_pre_norm, m_b_w_in, m_b_q_norm, m_b_w_q_up, m_b_w_out, m_b_post_norm, v_a_pre_norm, v_a_w_in, v_a_w_out, v_a_post_norm, v_kv_norm, v_kv_w_down, v_kv_latent_norm, v_kv_w_up, v_b_pre_norm, v_b_w_in, v_b_q_norm, v_b_w_q_up, v_b_w_out, v_b_post_norm):
    moments = (m_a_pre_norm, m_a_w_in, m_a_w_out, m_a_post_norm, m_kv_norm, m_kv_w_down, m_kv_latent_norm, m_kv_w_up,
               m_b_pre_norm, m_b_w_in, m_b_q_norm, m_b_w_q_up, m_b_w_out, m_b_post_norm,
               v_a_pre_norm, v_a_w_in, v_a_w_out, v_a_post_norm, v_kv_norm, v_kv_w_down, v_kv_latent_norm, v_kv_w_up,
               v_b_pre_norm, v_b_w_in, v_b_q_norm, v_b_w_q_up, v_b_w_out, v_b_post_norm)
    return _train_step(x, positions, a_pre_norm, a_w_in, a_w_out, a_post_norm, kv_norm, kv_w_down, kv_latent_norm,
                       kv_w_up, b_pre_norm, b_w_in, b_q_norm, b_w_q_up, b_w_out, b_post_norm, loss_target, moments)
```

```python
import math

import jax
import jax.numpy as jnp
from jax import lax
from jax.experimental import pallas as pl
from jax.experimental.pallas import tpu as pltpu

F32 = jnp.float32
BF16 = jnp.bfloat16
MESH = pl.DeviceIdType.MESH

NORM_EPS = 1e-6
NEG = -1e30
LANES = 128
VMEM_LIMIT = 56 * 1024 * 1024
LOG2E = math.log2(math.e)
LN2 = math.log(2.0)

A_GROUPS = 3
A_DILATIONS = (1, 4, 16)
A_HEADS = 8
A_HEAD_DIM = 128
A_WIDTH = A_HEADS * A_HEAD_DIM
A_ROPE_THETA = 500000.0
A_IN_WIDTH = A_GROUPS * 3 * A_WIDTH + A_WIDTH
A_SCALE = A_HEAD_DIM ** -0.5

B_HEADS = 16
B_NOPE = 64
B_ROPE = 32
B_QK_DIM = B_NOPE + B_ROPE
B_VDIM = 64
B_WIDTH = B_HEADS * B_VDIM
B_Q_LORA = 384
B_KV_LORA = 256
B_ROPE_THETA = 10000.0
B_SCALE = B_QK_DIM ** -0.5

ADAM_LR = 0.001
ADAM_B1 = 0.9
ADAM_B2 = 0.999
ADAM_EPS = 1e-08
ADAM_WD = 0.01
ADAM_STEP = 10

N_CHIPS = 4
PACK_COLS = 512


def _params(sem=None):
    return pltpu.CompilerParams(dimension_semantics=sem, vmem_limit_bytes=VMEM_LIMIT)


def _tile(n, want):
    t = min(n, want)
    assert n % t == 0, (n, want)
    return t


def _row_tile(n, want):
    for t in range(min(n, want), 0, -1):
        if n % t == 0 and (t % 16 == 0 or t == n):
            return t
    return n


def _rope_tables(positions, theta, lane0):
    half = 16
    inv_freq = 1.0 / (theta ** (jnp.arange(half, dtype=F32) * (2.0 / (2 * half))))
    n = positions.size
    per_row = LANES // half
    pos = jnp.repeat(positions.astype(F32).reshape(n // per_row, per_row), half, axis=1)
    ang = pos * jnp.tile(inv_freq, per_row)
    cos, sin = lax.optimization_barrier((jnp.cos(ang), jnp.sin(ang)))
    cos, sin = cos.reshape(n, half), sin.reshape(n, half)
    pre = jnp.zeros((n, lane0), F32)
    post = jnp.zeros((n, LANES - lane0 - 2 * half), F32)
    z16 = jnp.zeros((n, half), F32)
    c = jnp.concatenate([pre + 1.0, cos, cos, post + 1.0], axis=1)
    sa = jnp.concatenate([pre, -sin, z16, post], axis=1)
    sb = jnp.concatenate([pre, z16, sin, post], axis=1)
    return lax.optimization_barrier((c, sa, sb))


def _rope_apply(x, c, sa, sb, sign):
    k = x.shape[1] // LANES
    if k > 1:
        c, sa, sb = (jnp.concatenate([t] * k, axis=1) for t in (c, sa, sb))
    w = x.shape[1]
    up = pltpu.roll(x, w - 16, 1)
    dn = pltpu.roll(x, 16, 1)
    if sign > 0:
        return x * c + up * sa + dn * sb
    return x * c - up * sa - dn * sb


def _matmul(a, b, mode, out_dtype, *, name, tm=None, tn=1024, tk=None, add=None, rope=None,
            out_scale=None, b_koff=0, b_cols=None, out_into=None, out_full=None, out_joff=0,
            out_chunk_blocks=None, after=None):
    if mode == "nn":
        m, k = a.shape
        n = b.shape[1]
    elif mode == "nt":
        m, k = a.shape
        n = b.shape[0]
    else:
        k, m = a.shape
        n = b.shape[1]
    b_j0 = 0
    if b_cols is not None:
        tn = _tile(n, tn)
        b_j0, n = b_cols[0], b_cols[1] * tn
    if tm is None:
        tm = 512 if (rope is not None or mode == "nt") else (2048 if k <= 512 else 1024)
    if tk is None:
        tk = 3072 if mode == "nt" else 1024
    tm, tn, tk = _tile(m, tm), _tile(n, tn), _tile(k, tk)
    nk = k // tk
    if mode == "nn":
        a_spec = pl.BlockSpec((tm, tk), lambda j, i, kk: (i, kk))
        b_spec = pl.BlockSpec((tk, tn), lambda j, i, kk: (kk, j + b_j0))
        dims = (((1,), (0,)), ((), ()))
    elif mode == "nt":
        a_spec = pl.BlockSpec((tm, tk), lambda j, i, kk: (i, kk))
        b_spec = pl.BlockSpec((tn, tk), lambda j, i, kk: (j, kk + b_koff))
        dims = (((1,), (1,)), ((), ()))
    else:
        a_spec = pl.BlockSpec((tk, tm), lambda j, i, kk: (kk, i))
        b_spec = pl.BlockSpec((tk, tn), lambda j, i, kk: (kk, j))
        dims = (((0,), (0,)), ((), ()))
    operands = [a, b]
    in_specs = [a_spec, b_spec]
    if add is not None:
        operands.append(add)
        in_specs.append(pl.BlockSpec((tm, tn), lambda j, i, kk: (i, j)))
    if rope is not None:
        tables, rope_pred = rope
        for t in tables:
            operands.append(t)
            in_specs.append(pl.BlockSpec((tm, LANES), lambda j, i, kk: (i, 0)))
    aliases = {}
    if out_into is not None:
        aliases = {len(operands): 0}
        operands.append(out_into)
        in_specs.append(pl.BlockSpec(memory_space=pl.ANY))
        out_shape = jax.ShapeDtypeStruct(out_into.shape, out_into.dtype)
    elif out_full is not None:
        out_shape = jax.ShapeDtypeStruct(out_full, out_dtype)
    else:
        out_shape = jax.ShapeDtypeStruct((m, n), out_dtype)
    if after is not None:
        operands.append(after)
        in_specs.append(pl.BlockSpec(memory_space=pl.ANY))
    if out_chunk_blocks is not None:
        out_spec = pl.BlockSpec((None, tm, tn), lambda j, i, kk: ((j + out_joff) // out_chunk_blocks, i,
                                                                  (j + out_joff) % out_chunk_blocks))
    else:
        out_spec = pl.BlockSpec((tm, tn), lambda j, i, kk: (i, j + out_joff))

    def body(*refs):
        a_ref, b_ref = refs[0], refs[1]
        pos = 2
        add_ref = None
        if add is not None:
            add_ref = refs[pos]
            pos += 1
        tab_refs = None
        if rope is not None:
            tab_refs = refs[pos:pos + 3]
            pos += 3
        if out_into is not None:
            pos += 1
        if after is not None:
            pos += 1
        o_ref = refs[pos]
        acc_ref = refs[pos + 1] if nk > 1 else None

        def finish(res):
            if add_ref is not None:
                res = res + add_ref[...].astype(F32)
            if tab_refs is None:
                o_ref[...] = res.astype(o_ref.dtype)
                return
            j = pl.program_id(0)
            flag = rope_pred(j)

            roped = _rope_apply(res, tab_refs[0][...], tab_refs[1][...], tab_refs[2][...], 1)
            if out_scale is not None:
                value, scale_pred = out_scale
                use = scale_pred(j)
                roped = roped * (value if use is True else jnp.where(use, value, 1.0))
            if flag is True:
                o_ref[...] = roped.astype(o_ref.dtype)
                return

            @pl.when(flag)
            def _():
                o_ref[...] = roped.astype(o_ref.dtype)

            @pl.when(jnp.logical_not(flag))
            def _():
                o_ref[...] = res.astype(o_ref.dtype)

        part = lax.dot_general(a_ref[...].astype(BF16), b_ref[...].astype(BF16), dims,
                               preferred_element_type=F32)
        if nk == 1:
            finish(part)
            return
        kk = pl.program_id(2)

        @pl.when(kk == 0)
        def _():
            acc_ref[...] = part

        @pl.when(kk > 0)
        def _():
            acc_ref[...] += part

        @pl.when(kk == nk - 1)
        def _():
            finish(acc_ref[...])

    return pl.pallas_call(
        body, name=name, grid=(n // tn, m // tm, nk), in_specs=in_specs, out_specs=out_spec,
        out_shape=out_shape, input_output_aliases=aliases,
        scratch_shapes=[pltpu.VMEM((tm, tn), F32)] if nk > 1 else [],
        compiler_params=_params(("parallel", "parallel", "arbitrary")),
    )(*operands)


def _rms_fwd(x, g, out_dtype, *, name, add=None, tr=512):
    n, d = x.shape
    tr = _tile(n, tr)
    row = pl.BlockSpec((tr, d), lambda i: (i, 0))
    vec = pl.BlockSpec((1, d), lambda i: (0, 0))

    def body(*refs):
        x_ref, g_ref = refs[0], refs[1]
        o_ref = refs[-1]
        xv = x_ref[...].astype(F32)
        r = lax.rsqrt(jnp.mean(xv * xv, axis=-1, keepdims=True) + NORM_EPS)
        y = xv * r * g_ref[...]
        if add is not None:
            y = refs[2][...] + y
        o_ref[...] = y.astype(o_ref.dtype)

    ops = [x, g] + ([add] if add is not None else [])
    specs = [row, vec] + ([row] if add is not None else [])
    return pl.pallas_call(
        body, name=name, grid=(n // tr,), in_specs=specs, out_specs=row,
        out_shape=jax.ShapeDtypeStruct((n, d), out_dtype), compiler_params=_params(("parallel",)),
    )(*ops)


def _rms_bwd(x, g, dy, out_dtype, *, name, adds=(), dy_more=(), tr=512):
    n, d = x.shape
    tr = _tile(n, tr)
    steps = n // tr
    row = pl.BlockSpec((tr, d), lambda i: (i, 0))
    vec = pl.BlockSpec((1, d), lambda i: (0, 0))
    na = len(adds) + len(dy_more)

    def body(*refs):
        x_ref, g_ref, dy_ref = refs[:3]
        add_refs = refs[3:3 + len(adds)]
        more_refs = refs[3 + len(adds):3 + na]
        dx_ref, dg_ref, acc_ref = refs[3 + na:]
        i = pl.program_id(0)
        xv = x_ref[...].astype(F32)
        r = lax.rsqrt(jnp.mean(xv * xv, axis=-1, keepdims=True) + NORM_EPS)
        xh = xv * r
        dyv = dy_ref[...].astype(F32)
        for m_ref in more_refs:
            dyv = dyv + m_ref[...].astype(F32)
        part = (dyv * xh).reshape(tr // 8, 8, d).sum(axis=0)

        @pl.when(i == 0)
        def _():
            acc_ref[...] = part

        @pl.when(i > 0)
        def _():
            acc_ref[...] += part

        t = dyv * g_ref[...]
        dx = r * (t - xh * jnp.mean(t * xh, axis=-1, keepdims=True))
        for a_ref in add_refs:
            dx = dx + a_ref[...].astype(F32)
        dx_ref[...] = dx.astype(dx_ref.dtype)

        @pl.when(i == steps - 1)
        def _():
            dg_ref[...] = jnp.sum(acc_ref[...], axis=0, keepdims=True)

    return pl.pallas_call(
        body, name=name, grid=(steps,), in_specs=[row, vec, row] + [row] * na,
        out_specs=(row, vec),
        out_shape=(jax.ShapeDtypeStruct((n, d), out_dtype), jax.ShapeDtypeStruct((1, d), F32)),
        scratch_shapes=[pltpu.VMEM((8, d), F32)], compiler_params=_params(("arbitrary",)),
    )(x, g, dy, *adds, *dy_more)


def _rms(xv, g):
    return xv * lax.rsqrt(jnp.mean(xv * xv, axis=-1, keepdims=True) + NORM_EPS) * g


def _post_norm_block(y, g, h_in, next_gains, *, name, tr=512):
    n, d = y.shape
    tr = _tile(n, tr)
    nk = len(next_gains)
    row = pl.BlockSpec((tr, d), lambda i: (i, 0))
    vec = pl.BlockSpec((1, d), lambda i: (0, 0))

    def body(*refs):
        y_ref, g_ref, h_ref = refs[:3]
        gk_refs = refs[3:3 + nk]
        o_ref = refs[3 + nk]
        hn_refs = refs[4 + nk:]
        h = h_ref[...] + _rms(y_ref[...], g_ref[...])
        o_ref[...] = h
        for gk_ref, hn_ref in zip(gk_refs, hn_refs):
            hn_ref[...] = _rms(h, gk_ref[...]).astype(BF16)

    return pl.pallas_call(
        body, name=name, grid=(n // tr,), in_specs=[row, vec, row] + [vec] * nk,
        out_specs=(row,) * (1 + nk),
        out_shape=(jax.ShapeDtypeStruct((n, d), F32),) + (jax.ShapeDtypeStruct((n, d), BF16),) * nk,
        compiler_params=_params(("parallel",)),
    )(y, g, h_in, *next_gains)


def _post_norm_loss(y, g, h_in, target, *, tr=512):
    n, d = y.shape
    tr = _tile(n, tr)
    steps = n // tr
    row = pl.BlockSpec((tr, d), lambda i: (i, 0))

    def body(y_ref, g_ref, h_ref, t_ref, dh_ref, loss_ref, acc_ref):
        i = pl.program_id(0)
        e = h_ref[...] + _rms(y_ref[...], g_ref[...]) - t_ref[...]
        dh_ref[...] = e / d
        part = (e * e).reshape(tr // 8, 8, d).sum(axis=0)

        @pl.when(i == 0)
        def _():
            acc_ref[...] = part

        @pl.when(i > 0)
        def _():
            acc_ref[...] += part

        @pl.when(i == steps - 1)
        def _():
            s = jnp.sum(jnp.sum(acc_ref[...], axis=-1, keepdims=True), axis=0, keepdims=True)
            loss_ref[...] = 0.5 * s / d

    return pl.pallas_call(
        body, name="b_post_norm_loss", grid=(steps,),
        in_specs=[row, pl.BlockSpec((1, d), lambda i: (0, 0)), row, row],
        out_specs=(row, pl.BlockSpec((1, 1), lambda i: (0, 0))),
        out_shape=(jax.ShapeDtypeStruct((n, d), F32), jax.ShapeDtypeStruct((1, 1), F32)),
        scratch_shapes=[pltpu.VMEM((8, d), F32)], compiler_params=_params(("arbitrary",)),
    )(y, g, h_in, target)


def _rms_bwd_pair(x, g1, dy1, g2, dy2, add, *, name, tr=512):
    n, d = x.shape
    tr = _tile(n, tr)
    steps = n // tr
    row = pl.BlockSpec((tr, d), lambda i: (i, 0))
    vec = pl.BlockSpec((1, d), lambda i: (0, 0))

    def body(x_ref, g1_ref, d1_ref, g2_ref, d2_ref, add_ref, dx_ref, dg1_ref, dg2_ref, acc_ref):
        i = pl.program_id(0)
        xv = x_ref[...]
        r = lax.rsqrt(jnp.mean(xv * xv, axis=-1, keepdims=True) + NORM_EPS)
        xh = xv * r
        dx = add_ref[...]
        for k, (g_ref, d_ref) in enumerate(((g1_ref, d1_ref), (g2_ref, d2_ref))):
            dyv = d_ref[...].astype(F32)
            part = (dyv * xh).reshape(tr // 8, 8, d).sum(axis=0)

            @pl.when(i == 0)
            def _(part=part, k=k):
                acc_ref[k] = part

            @pl.when(i > 0)
            def _(part=part, k=k):
                acc_ref[k] += part

            t = dyv * g_ref[...]
            dx = dx + r * (t - xh * jnp.mean(t * xh, axis=-1, keepdims=True))
        dx_ref[...] = dx

        @pl.when(i == steps - 1)
        def _():
            dg1_ref[...] = jnp.sum(acc_ref[0], axis=0, keepdims=True)
            dg2_ref[...] = jnp.sum(acc_ref[1], axis=0, keepdims=True)

    return pl.pallas_call(
        body, name=name, grid=(steps,), in_specs=[row, vec, row, vec, row, row],
        out_specs=(row, vec, vec),
        out_shape=(jax.ShapeDtypeStruct((n, d), F32), jax.ShapeDtypeStruct((1, d), F32),
                   jax.ShapeDtypeStruct((1, d), F32)),
        scratch_shapes=[pltpu.VMEM((2, 8, d), F32)], compiler_params=_params(("arbitrary",)),
    )(x, g1, dy1, g2, dy2, add)


def _kv_latent_fwd(ckr, g_lat, tabs, *, tr=512):
    n = ckr.shape[0]
    tr = _tile(n, tr)
    lat = B_KV_LORA

    def body(c_ref, k_ref, g_ref, tc, tsa, tsb, ckv_ref, kr_ref):
        xv = c_ref[...]
        r = lax.rsqrt(jnp.mean(xv * xv, axis=-1, keepdims=True) + NORM_EPS)
        ckv_ref[...] = (xv * r * g_ref[...]).astype(BF16)
        kr_ref[...] = _rope_apply(k_ref[...], tc[...], tsa[...], tsb[...], 1).astype(BF16)

    tab = pl.BlockSpec((tr, LANES), lambda i: (i, 0))
    return pl.pallas_call(
        body, name="kv_latent_fwd", grid=(n // tr,),
        in_specs=[pl.BlockSpec((tr, lat), lambda i: (i, 0)),
                  pl.BlockSpec((tr, LANES), lambda i: (i, lat // LANES)),
                  pl.BlockSpec((1, lat), lambda i: (0, 0)), tab, tab, tab],
        out_specs=(pl.BlockSpec((tr, lat), lambda i: (i, 0)), tab),
        out_shape=(jax.ShapeDtypeStruct((n, lat), BF16), jax.ShapeDtypeStruct((n, LANES), BF16)),
        compiler_params=_params(("parallel",)),
    )(ckr, ckr, g_lat, *tabs)


def _kv_latent_bwd(dckv, ckr, g_lat, dk_cat, tabs, *, tr=512):
    n = ckr.shape[0]
    tr = _tile(n, tr)
    steps = n // tr
    lat = B_KV_LORA
    wk = dk_cat.shape[1]

    def body(d_ref, c_ref, g_ref, dk_ref, tc, tsa, tsb, o_ref, dg_ref, acc_ref):
        i = pl.program_id(0)
        xv = c_ref[...]
        r = lax.rsqrt(jnp.mean(xv * xv, axis=-1, keepdims=True) + NORM_EPS)
        xh = xv * r
        dyv = d_ref[...]
        part = (dyv * xh).reshape(tr // 8, 8, lat).sum(axis=0)

        @pl.when(i == 0)
        def _():
            acc_ref[...] = part

        @pl.when(i > 0)
        def _():
            acc_ref[...] += part

        t = dyv * g_ref[...]
        dx = r * (t - xh * jnp.mean(t * xh, axis=-1, keepdims=True))
        o_ref[:, 0:lat] = dx.astype(o_ref.dtype)
        dkr = dk_ref[:, 0:LANES].astype(F32)
        for h in range(1, wk // LANES):
            dkr = dkr + dk_ref[:, h * LANES:(h + 1) * LANES].astype(F32)
        o_ref[:, lat:lat + LANES] = _rope_apply(dkr, tc[...], tsa[...], tsb[...], -1).astype(o_ref.dtype)

        @pl.when(i == steps - 1)
        def _():
            dg_ref[...] = jnp.sum(acc_ref[...], axis=0, keepdims=True)

    tab = pl.BlockSpec((tr, LANES), lambda i: (i, 0))
    return pl.pallas_call(
        body, name="kv_latent_bwd", grid=(steps,),
        in_specs=[pl.BlockSpec((tr, lat), lambda i: (i, 0)), pl.BlockSpec((tr, lat), lambda i: (i, 0)),
                  pl.BlockSpec((1, lat), lambda i: (0, 0)), pl.BlockSpec((tr, wk), lambda i: (i, 0)),
                  tab, tab, tab],
        out_specs=(pl.BlockSpec((tr, lat + LANES), lambda i: (i, 0)), pl.BlockSpec((1, lat), lambda i: (0, 0))),
        out_shape=(jax.ShapeDtypeStruct((n, lat + LANES), BF16), jax.ShapeDtypeStruct((1, lat), F32)),
        scratch_shapes=[pltpu.VMEM((8, lat), F32)], compiler_params=_params(("arbitrary",)),
    )(dckv, ckr, g_lat, dk_cat, *tabs)


def _sigmoid(z):
    return 1.0 / (1.0 + jnp.exp(-z))


def _lane_place(cols, width):
    rows = cols[0].shape[0]
    lane = lax.broadcasted_iota(jnp.int32, (rows, width), 1)
    out = jnp.zeros((rows, width), F32)
    for h, col in enumerate(cols):
        out = jnp.where(lane == h, col, out)
    return out


def _merge_gate_fwd(outs, lses, proj, z_block, *, tr=512):
    n, w = outs[0].shape
    tr = _tile(n, tr)
    ng = len(outs)

    def body(*refs):
        o_refs = refs[:ng]
        l_refs = refs[ng:2 * ng]
        z_ref = refs[2 * ng]
        y_ref, om_ref, lse_ref = refs[2 * ng + 1:]
        ls = [r[...] for r in l_refs]
        mx = ls[0]
        for l in ls[1:]:
            mx = jnp.maximum(mx, l)
        ssum = jnp.exp2(ls[0] - mx)
        for l in ls[1:]:
            ssum = ssum + jnp.exp2(l - mx)
        tot = mx + jnp.log2(ssum)
        lse_ref[...] = tot
        ws = [jnp.exp2(l - tot) for l in ls]
        for h in range(A_HEADS):
            sl = slice(h * A_HEAD_DIM, (h + 1) * A_HEAD_DIM)
            o = ws[0][:, h:h + 1] * o_refs[0][:, sl]
            for gi in range(1, ng):
                o = o + ws[gi][:, h:h + 1] * o_refs[gi][:, sl]
            z = z_ref[:, sl].astype(F32)
            om_ref[:, sl] = o.astype(BF16)
            y_ref[:, sl] = (o * (z * _sigmoid(z))).astype(BF16)

    row = pl.BlockSpec((tr, w), lambda i: (i, 0))
    lrow = pl.BlockSpec((tr, A_HEADS), lambda i: (i, 0))
    return pl.pallas_call(
        body, name="merge_gate_fwd", grid=(n // tr,),
        in_specs=[row] * ng + [lrow] * ng + [pl.BlockSpec((tr, w), lambda i: (i, z_block))],
        out_specs=(row, row, lrow),
        out_shape=(jax.ShapeDtypeStruct((n, w), BF16), jax.ShapeDtypeStruct((n, w), BF16),
                   jax.ShapeDtypeStruct((n, A_HEADS), F32)),
        compiler_params=_params(("parallel",)),
    )(*outs, *lses, proj)


def _gate_bwd(dy, o, z_arr, z_block, *, name, with_delta, tr=512):
    n, w = dy.shape
    tr = _tile(n, tr)

    def body(*refs):
        dy_ref, o_ref, z_ref, do_ref, dz_ref = refs[:5]
        dyv = dy_ref[...].astype(F32)
        ov = o_ref[...].astype(F32)
        z = z_ref[...].astype(F32)
        sig = _sigmoid(z)
        do = dyv * (z * sig)
        do_ref[...] = do.astype(BF16)
        dz_ref[...] = (dyv * ov * (sig * (1.0 + z * (1.0 - sig)))).astype(BF16)
        if with_delta:
            prod = do * ov
            cols = [jnp.sum(prod[:, h * A_HEAD_DIM:(h + 1) * A_HEAD_DIM], axis=-1, keepdims=True)
                    for h in range(A_HEADS)]
            refs[5][...] = _lane_place(cols, A_HEADS)

    row = pl.BlockSpec((tr, w), lambda i: (i, 0))
    out_specs = [row, row]
    out_shape = [jax.ShapeDtypeStruct((n, w), BF16), jax.ShapeDtypeStruct((n, w), BF16)]
    if with_delta:
        out_specs.append(pl.BlockSpec((tr, A_HEADS), lambda i: (i, 0)))
        out_shape.append(jax.ShapeDtypeStruct((n, A_HEADS), F32))
    return pl.pallas_call(
        body, name=name, grid=(n // tr,),
        in_specs=[row, row, pl.BlockSpec((tr, w), lambda i: (i, z_block))],
        out_specs=tuple(out_specs), out_shape=tuple(out_shape), compiler_params=_params(("parallel",)),
    )(dy, o, z_arr)


def _dot_nt(a, b):
    return lax.dot_general(a, b, (((1,), (1,)), ((), ())), preferred_element_type=F32)


def _dot_nn(a, b):
    return lax.dot_general(a, b, (((1,), (0,)), ((), ())), preferred_element_type=F32)


def _attn_a_fwd(qkv, cb0, qb, out_dtype, *, name):
    bl, dil, ln, _ = qkv.shape
    nb = ln // qb
    hw = A_WIDTH
    heads = range(A_HEADS)
    sls = [slice(h * A_HEAD_DIM, (h + 1) * A_HEAD_DIM) for h in heads]

    def body(*refs):
        if nb > 1:
            q_ref, kc_ref, vc_ref, kp_ref, vp_ref, o_ref, lse_ref = refs
        else:
            q_ref, kc_ref, vc_ref, o_ref, lse_ref = refs
        i = pl.program_id(2)
        qi = lax.broadcasted_iota(jnp.int32, (qb, qb), 0)
        ki = lax.broadcasted_iota(jnp.int32, (qb, qb), 1)
        mask_c = ki <= qi
        mask_p = jnp.logical_and(ki >= qi, i >= 1)
        s_c = [jnp.where(mask_c, _dot_nt(q_ref[:, sls[h]], kc_ref[:, sls[h]]), NEG) for h in heads]
        m = [jnp.max(s_c[h], axis=-1, keepdims=True) for h in heads]
        if nb > 1:
            s_p = [jnp.where(mask_p, _dot_nt(q_ref[:, sls[h]], kp_ref[:, sls[h]]), NEG) for h in heads]
            m = [jnp.maximum(m[h], jnp.max(s_p[h], axis=-1, keepdims=True)) for h in heads]
        p_c = [jnp.exp2(s_c[h] - m[h]) for h in heads]
        l = [jnp.sum(p_c[h], axis=-1, keepdims=True) for h in heads]
        acc = [_dot_nn(p_c[h].astype(BF16), vc_ref[:, sls[h]]) for h in heads]
        if nb > 1:
            p_p = [jnp.exp2(s_p[h] - m[h]) for h in heads]
            l = [l[h] + jnp.sum(p_p[h], axis=-1, keepdims=True) for h in heads]
            acc = [acc[h] + _dot_nn(p_p[h].astype(BF16), vp_ref[:, sls[h]]) for h in heads]
        for h in heads:
            o_ref[:, sls[h]] = (acc[h] / l[h]).astype(o_ref.dtype)
        lse_ref[...] = _lane_place([m[h] + jnp.log2(l[h]) for h in heads], A_HEADS)

    def spec(off, prev):
        if prev:
            return pl.BlockSpec((None, None, qb, hw), lambda b, r, i: (b, r, jnp.maximum(i - 1, 0), cb0 + off))
        return pl.BlockSpec((None, None, qb, hw), lambda b, r, i: (b, r, i, cb0 + off))

    return pl.pallas_call(
        body, name=name, grid=(bl, dil, nb),
        in_specs=[spec(0, False), spec(1, False), spec(2, False)] + ([spec(1, True), spec(2, True)] if nb > 1 else []),
        out_specs=(pl.BlockSpec((None, None, qb, hw), lambda b, r, i: (b, r, i, 0)),
                   pl.BlockSpec((None, None, qb, A_HEADS), lambda b, r, i: (b, r, i, 0))),
        out_shape=(jax.ShapeDtypeStruct((bl, dil, ln, hw), out_dtype),
                   jax.ShapeDtypeStruct((bl, dil, ln, A_HEADS), F32)),
        compiler_params=_params(("parallel", "parallel", "arbitrary")),
    )(*([qkv] * (5 if nb > 1 else 3)))


def _attn_a_bwd(qkv, cb0, do, lse, delta, lse_t, delta_t, tabs, qb, *, name):
    bl, dil, ln, _ = qkv.shape
    nb = ln // qb
    hw = A_WIDTH

    def body(*refs):
        if nb > 1:
            (q_ref, kc_ref, vc_ref, do_ref, lse_ref, dl_ref, lt_ref, dt_ref, tc, tsa, tsb,
             qn_ref, kp_ref, vp_ref, don_ref, ltn_ref, dtn_ref, o_ref) = refs
        else:
            q_ref, kc_ref, vc_ref, do_ref, lse_ref, dl_ref, lt_ref, dt_ref, tc, tsa, tsb, o_ref = refs
        i = pl.program_id(2)
        row = lax.broadcasted_iota(jnp.int32, (qb, qb), 0)
        col = lax.broadcasted_iota(jnp.int32, (qb, qb), 1)
        m_qc = col <= row
        m_kc = row <= col
        m_qp = jnp.logical_and(col >= row, i >= 1)
        m_kn = jnp.logical_and(row >= col, i + 1 < nb)
        c, sa, sb = tc[...], tsa[...], tsb[...]
        heads = range(A_HEADS)
        sls = [slice(h * A_HEAD_DIM, (h + 1) * A_HEAD_DIM) for h in heads]
        q, kc = [q_ref[:, sl] for sl in sls], [kc_ref[:, sl] for sl in sls]
        vc, dov = [vc_ref[:, sl] for sl in sls], [do_ref[:, sl] for sl in sls]
        lse_c = [lse_ref[:, h:h + 1] for h in heads]
        dl_c = [dl_ref[:, h:h + 1] for h in heads]
        s = [_dot_nt(q[h], kc[h]) for h in heads]
        st = [_dot_nt(kc[h], q[h]) for h in heads]
        dp = [_dot_nt(dov[h], vc[h]) for h in heads]
        dpt = [_dot_nt(vc[h], dov[h]) for h in heads]
        p = [jnp.exp2(jnp.where(m_qc, s[h], NEG) - lse_c[h]) for h in heads]
        pt = [jnp.exp2(jnp.where(m_kc, st[h], NEG) - lt_ref[h:h + 1, :]) for h in heads]
        dq = [_dot_nn((p[h] * (dp[h] - dl_c[h])).astype(BF16), kc[h]) for h in heads]
        dk = [_dot_nn((pt[h] * (dpt[h] - dt_ref[h:h + 1, :])).astype(BF16), q[h]) for h in heads]
        dv = [_dot_nn(pt[h].astype(BF16), dov[h]) for h in heads]
        if nb > 1:
            kp, vp = [kp_ref[:, sl] for sl in sls], [vp_ref[:, sl] for sl in sls]
            qn, don = [qn_ref[:, sl] for sl in sls], [don_ref[:, sl] for sl in sls]
            s = [_dot_nt(q[h], kp[h]) for h in heads]
            st = [_dot_nt(kc[h], qn[h]) for h in heads]
            dp = [_dot_nt(dov[h], vp[h]) for h in heads]
            dpt = [_dot_nt(vc[h], don[h]) for h in heads]
            p = [jnp.exp2(jnp.where(m_qp, s[h], NEG) - lse_c[h]) for h in heads]
            pt = [jnp.exp2(jnp.where(m_kn, st[h], NEG) - ltn_ref[h:h + 1, :]) for h in heads]
            dq = [dq[h] + _dot_nn((p[h] * (dp[h] - dl_c[h])).astype(BF16), kp[h]) for h in heads]
            dk = [dk[h] + _dot_nn((pt[h] * (dpt[h] - dtn_ref[h:h + 1, :])).astype(BF16), qn[h]) for h in heads]
            dv = [dv[h] + _dot_nn(pt[h].astype(BF16), don[h]) for h in heads]
        for h in heads:
            o_ref[:, h * A_HEAD_DIM:(h + 1) * A_HEAD_DIM] = _rope_apply(dq[h] * A_SCALE, c, sa, sb, -1).astype(BF16)
            o_ref[:, hw + h * A_HEAD_DIM:hw + (h + 1) * A_HEAD_DIM] = _rope_apply(dk[h] * LN2, c, sa, sb, -1).astype(BF16)
            o_ref[:, 2 * hw + h * A_HEAD_DIM:2 * hw + (h + 1) * A_HEAD_DIM] = dv[h].astype(BF16)

    def cur(w, col):
        return pl.BlockSpec((None, None, qb, w), lambda b, r, i: (b, r, i, col))

    def prev(w, col):
        return pl.BlockSpec((None, None, qb, w), lambda b, r, i: (b, r, jnp.maximum(i - 1, 0), col))

    def nxt(w, col):
        return pl.BlockSpec((None, None, qb, w), lambda b, r, i: (b, r, jnp.minimum(i + 1, nb - 1), col))

    t_cur = pl.BlockSpec((None, None, A_HEADS, qb), lambda b, r, i: (b, r, 0, i))
    t_nxt = pl.BlockSpec((None, None, A_HEADS, qb), lambda b, r, i: (b, r, 0, jnp.minimum(i + 1, nb - 1)))
    in_specs = [cur(hw, cb0), cur(hw, cb0 + 1), cur(hw, cb0 + 2), cur(hw, 0), cur(A_HEADS, 0), cur(A_HEADS, 0),
                t_cur, t_cur, cur(LANES, 0), cur(LANES, 0), cur(LANES, 0)]
    operands = [qkv, qkv, qkv, do, lse, delta, lse_t, delta_t, *tabs]
    if nb > 1:
        in_specs += [nxt(hw, cb0), prev(hw, cb0 + 1), prev(hw, cb0 + 2), nxt(hw, 0), t_nxt, t_nxt]
        operands += [qkv, qkv, qkv, do, lse_t, delta_t]
    return pl.pallas_call(
        body, name=name, grid=(bl, dil, nb), in_specs=in_specs, out_specs=cur(3 * hw, 0),
        out_shape=jax.ShapeDtypeStruct((bl, dil, ln, 3 * hw), BF16),
        compiler_params=_params(("parallel", "parallel", "arbitrary")),
    )(*operands)


def _head_terms(do, o, lse, e):
    rows = do.shape[0]
    lane = lax.broadcasted_iota(jnp.int32, (rows, LANES), 1)
    mine = (lane < B_VDIM) if e == 0 else (lane >= B_VDIM)
    prod = do.astype(F32) * o.astype(F32)
    dl = jnp.sum(jnp.where(mine, prod, 0.0), axis=-1, keepdims=True)
    do_e = jnp.where(mine, do, jnp.zeros_like(do))
    return do_e, dl, lse[:, e * B_VDIM:e * B_VDIM + 1]


def _col_to_row(col, rows):
    return jnp.transpose(jnp.broadcast_to(col, (rows, LANES)))[0:1, :]


def _mla_fwd(q_cat, kvup, kr, z, tq):
    bl, t, _ = q_cat.shape
    nq = t // tq
    pairs = B_HEADS // 2
    v_blk0 = (B_HEADS * LANES) // LANES

    def body(q_ref, k_ref, v_ref, kr_ref, z_ref, y_ref, o_ref, lse_ref, lrow_ref, m_ref, acc_ref):
        qi = pl.program_id(2)
        qs = [q_ref[:, e * LANES:(e + 1) * LANES] for e in range(2)]
        row = lax.broadcasted_iota(jnp.int32, (tq, tq), 0)
        col = lax.broadcasted_iota(jnp.int32, (tq, tq), 1)
        tri = col <= row
        sum_lane = [B_VDIM, 0]

        for e in range(2):
            m_ref[e] = jnp.full((tq, LANES), NEG, F32)
            acc_ref[e] = jnp.zeros((tq, LANES), F32)

        def tile(k0, w, masked):
            lane = lax.broadcasted_iota(jnp.int32, (w, LANES), 1)
            first = lane < B_VDIM
            krv = kr_ref[pl.ds(k0, w), :]
            v = v_ref[pl.ds(k0, w), :]
            vs = [jnp.where(first, v, jnp.where(lane == B_VDIM, 1.0, 0.0).astype(BF16)),
                  jnp.where(first, jnp.where(lane == 0, 1.0, 0.0).astype(BF16), v)]
            ss = []
            for e in range(2):
                k = k_ref[pl.ds(k0, w), e * LANES:(e + 1) * LANES] + krv
                s = _dot_nt(qs[e], k)
                if masked:
                    r = lax.broadcasted_iota(jnp.int32, (tq, w), 0)
                    c = lax.broadcasted_iota(jnp.int32, (tq, w), 1)
                    s = jnp.where(c <= r + (w - tq), s, NEG)
                ss.append(s)
            for e in range(2):
                m_old = m_ref[e]
                m_new = jnp.maximum(m_old, jnp.max(ss[e], axis=-1, keepdims=True))
                p = jnp.exp2(ss[e] - jnp.concatenate([m_new] * (w // LANES), axis=1)).astype(BF16)
                m_ref[e] = m_new
                acc_ref[e] = jnp.exp2(m_old - m_new) * acc_ref[e] + _dot_nn(p, vs[e])

        def step(kb2, carry):
            tile(pl.multiple_of(kb2 * 2 * tq, 2 * tq), 2 * tq, False)
            return carry

        lax.fori_loop(0, qi // 2, step, 0)

        @pl.when(qi % 2 == 1)
        def _():
            tile(pl.multiple_of((qi - 1) * tq, tq), 2 * tq, True)

        @pl.when(qi % 2 == 0)
        def _():
            tile(pl.multiple_of(qi * tq, tq), tq, True)
        lane = lax.broadcasted_iota(jnp.int32, (tq, LANES), 1)
        first = lane < B_VDIM
        accs = [acc_ref[e] for e in range(2)]
        ls = [accs[e][:, sum_lane[e]:sum_lane[e] + 1] for e in range(2)]
        outs = [accs[e] / ls[e] for e in range(2)]
        lses = [m_ref[e] + jnp.log2(ls[e]) for e in range(2)]
        o = jnp.where(first, outs[0], outs[1])
        zv = z_ref[...].astype(F32)
        o_ref[...] = o.astype(BF16)
        y_ref[...] = (o * (zv * _sigmoid(zv))).astype(BF16)
        lse_ref[...] = jnp.where(first, lses[0], lses[1])
        for e in range(2):
            lrow_ref[e:e + 1, :] = jnp.transpose(lses[e])[0:1, :]

    blk = pl.BlockSpec((None, tq, LANES), lambda b, j, i: (b, i, j))
    return pl.pallas_call(
        body, name="mla_fwd", grid=(bl, pairs, nq),
        in_specs=[pl.BlockSpec((None, tq, 2 * LANES), lambda b, j, i: (b, i, j)),
                  pl.BlockSpec((None, t, 2 * LANES), lambda b, j, i: (b, 0, j)),
                  pl.BlockSpec((None, t, LANES), lambda b, j, i: (b, 0, v_blk0 + j)),
                  pl.BlockSpec((None, t, LANES), lambda b, j, i: (b, 0, 0)),
                  blk],
        out_specs=(blk, blk, blk, pl.BlockSpec((None, None, None, 2, tq), lambda b, j, i: (b, j, i, 0, 0))),
        out_shape=(jax.ShapeDtypeStruct((bl, t, B_WIDTH), BF16), jax.ShapeDtypeStruct((bl, t, B_WIDTH), BF16),
                   jax.ShapeDtypeStruct((bl, t, B_WIDTH), F32),
                   jax.ShapeDtypeStruct((bl, pairs, nq, 2, tq), F32)),
        scratch_shapes=[pltpu.VMEM((2, tq, LANES), F32), pltpu.VMEM((2, tq, LANES), F32)],
        compiler_params=_params(("parallel", "parallel", "arbitrary")),
    )(q_cat, kvup, kvup, kr, z)


def _mla_dq(q_cat, kvup, kr, do, o, lse, tabs, tq):
    bl, t, _ = q_cat.shape
    nq = t // tq
    pairs = B_HEADS // 2
    v_blk0 = (B_HEADS * LANES) // LANES

    def body(q_ref, k_ref, v_ref, kr_ref, do_ref, o_ref, lse_ref, tc, tsa, tsb, dq_ref, drow_ref, acc_ref):
        qi = pl.program_id(2)
        dov, ov, lsev = do_ref[...], o_ref[...], lse_ref[...]
        qs = [q_ref[:, e * LANES:(e + 1) * LANES] for e in range(2)]
        terms = [_head_terms(dov, ov, lsev, e) for e in range(2)]
        row = lax.broadcasted_iota(jnp.int32, (tq, tq), 0)
        col = lax.broadcasted_iota(jnp.int32, (tq, tq), 1)
        tri = col <= row
        for e in range(2):
            acc_ref[e] = jnp.zeros((tq, LANES), F32)

        def tile(k0, w, masked):
            krv = kr_ref[pl.ds(k0, w), :]
            v = v_ref[pl.ds(k0, w), :]
            ks = [k_ref[pl.ds(k0, w), e * LANES:(e + 1) * LANES] + krv for e in range(2)]
            ss = [_dot_nt(qs[e], ks[e]) for e in range(2)]
            dps = [_dot_nt(terms[e][0], v) for e in range(2)]
            for e in range(2):
                s = ss[e]
                if masked:
                    r = lax.broadcasted_iota(jnp.int32, (tq, w), 0)
                    c = lax.broadcasted_iota(jnp.int32, (tq, w), 1)
                    s = jnp.where(c <= r + (w - tq), s, NEG)
                p = jnp.exp2(s - terms[e][2])
                ds = (p * (dps[e] - terms[e][1])).astype(BF16)
                acc_ref[e] += _dot_nn(ds, ks[e])

        def step(kb2, carry):
            tile(pl.multiple_of(kb2 * 2 * tq, 2 * tq), 2 * tq, False)
            return carry

        lax.fori_loop(0, qi // 2, step, 0)

        @pl.when(qi % 2 == 1)
        def _():
            tile(pl.multiple_of((qi - 1) * tq, tq), 2 * tq, True)

        @pl.when(qi % 2 == 0)
        def _():
            tile(pl.multiple_of(qi * tq, tq), tq, True)

        for e in range(2):
            dq_ref[:, e * LANES:(e + 1) * LANES] = _rope_apply(acc_ref[e] * B_SCALE, tc[...], tsa[...], tsb[...], -1).astype(BF16)
            drow_ref[e:e + 1, :] = _col_to_row(terms[e][1], tq)

    blk = pl.BlockSpec((None, tq, LANES), lambda b, j, i: (b, i, j))
    tab = pl.BlockSpec((None, tq, LANES), lambda b, j, i: (b, i, 0))
    qblk = pl.BlockSpec((None, tq, 2 * LANES), lambda b, j, i: (b, i, j))
    return pl.pallas_call(
        body, name="mla_dq", grid=(bl, pairs, nq),
        in_specs=[qblk,
                  pl.BlockSpec((None, t, 2 * LANES), lambda b, j, i: (b, 0, j)),
                  pl.BlockSpec((None, t, LANES), lambda b, j, i: (b, 0, v_blk0 + j)),
                  pl.BlockSpec((None, t, LANES), lambda b, j, i: (b, 0, 0)),
                  blk, blk, blk, tab, tab, tab],
        out_specs=(qblk, pl.BlockSpec((None, None, None, 2, tq), lambda b, j, i: (b, j, i, 0, 0))),
        out_shape=(jax.ShapeDtypeStruct((bl, t, B_HEADS * LANES), BF16),
                   jax.ShapeDtypeStruct((bl, pairs, nq, 2, tq), F32)),
        scratch_shapes=[pltpu.VMEM((2, tq, LANES), F32)],
        compiler_params=_params(("parallel", "parallel", "arbitrary")),
    )(q_cat, kvup, kvup, kr, do, o, lse, *tabs)


def _mla_dkv(q_cat, kvup, kr, do, lse_rows, delta_rows, tq):
    bl, t, _ = q_cat.shape
    nq = t // tq
    pairs = B_HEADS // 2
    v_blk0 = (B_HEADS * LANES) // LANES

    def body(q_ref, k_ref, v_ref, kr_ref, do_ref, lrow_ref, drow_ref, dk_ref, dv_ref, acc_ref):
        kb = pl.program_id(2)
        v = v_ref[...]
        krv = kr_ref[...]
        ks = [k_ref[:, e * LANES:(e + 1) * LANES] + krv for e in range(2)]
        krow = lax.broadcasted_iota(jnp.int32, (tq, tq), 0)
        qcol = lax.broadcasted_iota(jnp.int32, (tq, tq), 1)
        tri = krow <= qcol
        lane = lax.broadcasted_iota(jnp.int32, (tq, LANES), 1)
        mine = [lane < B_VDIM, lane >= B_VDIM]

        for e in range(3):
            acc_ref[e] = jnp.zeros((tq, LANES), F32)

        def tile(qb, nblk, masked):
            w = nblk * tq
            rows = pl.ds(pl.multiple_of(qb * tq, tq), w)
            dov = do_ref[rows, :]
            lane_w = lax.broadcasted_iota(jnp.int32, (w, LANES), 1)
            mine_w = [lane_w < B_VDIM, lane_w >= B_VDIM]
            qs = [q_ref[rows, e * LANES:(e + 1) * LANES] for e in range(2)]
            does = [jnp.where(mine_w[e], dov, jnp.zeros_like(dov)) for e in range(2)]
            sts = [_dot_nt(ks[e], qs[e]) for e in range(2)]
            dpts = [_dot_nt(v, does[e]) for e in range(2)]

            def rows_of(ref, e):
                return jnp.concatenate([ref[qb + i, e:e + 1, :] for i in range(nblk)], axis=1)

            pts = []
            for e in range(2):
                st = sts[e]
                if masked:
                    r = lax.broadcasted_iota(jnp.int32, (tq, w), 0)
                    c = lax.broadcasted_iota(jnp.int32, (tq, w), 1)
                    st = jnp.where(r <= c, st, NEG)
                pts.append(jnp.exp2(st - rows_of(lrow_ref, e)))
            acc_ref[2] += _dot_nn(pts[0].astype(BF16), does[0]) + _dot_nn(pts[1].astype(BF16), does[1])
            for e in range(2):
                dst = (pts[e] * (dpts[e] - rows_of(drow_ref, e))).astype(BF16)
                acc_ref[e] += _dot_nn(dst, qs[e])

        rest = nq - 1 - kb
        odd = rest % 2

        @pl.when(odd == 1)
        def _():
            tile(kb, 2, True)

        @pl.when(odd == 0)
        def _():
            tile(kb, 1, True)

        def step(i, carry):
            tile(kb + 1 + odd + 2 * i, 2, False)
            return carry

        lax.fori_loop(0, rest // 2, step, 0)
        dk_ref[:, 0:LANES] = (acc_ref[0] * LN2).astype(BF16)
        dk_ref[:, LANES:2 * LANES] = (acc_ref[1] * LN2).astype(BF16)
        dv_ref[...] = acc_ref[2].astype(BF16)

    full = pl.BlockSpec((None, t, LANES), lambda b, j, i: (b, 0, j))
    rows = pl.BlockSpec((None, None, nq, 2, tq), lambda b, j, i: (b, j, 0, 0, 0))
    kblk = pl.BlockSpec((None, tq, 2 * LANES), lambda b, j, i: (b, i, j))
    return pl.pallas_call(
        body, name="mla_dkv", grid=(bl, pairs, nq),
        in_specs=[pl.BlockSpec((None, t, 2 * LANES), lambda b, j, i: (b, 0, j)),
                  kblk,
                  pl.BlockSpec((None, tq, LANES), lambda b, j, i: (b, i, v_blk0 + j)),
                  pl.BlockSpec((None, tq, LANES), lambda b, j, i: (b, i, 0)),
                  full, rows, rows],
        out_specs=(kblk, pl.BlockSpec((None, tq, LANES), lambda b, j, i: (b, i, j))),
        out_shape=(jax.ShapeDtypeStruct((bl, t, B_HEADS * LANES), BF16),
                   jax.ShapeDtypeStruct((bl, t, B_WIDTH), BF16)),
        scratch_shapes=[pltpu.VMEM((3, tq, LANES), F32)],
        compiler_params=_params(("parallel", "parallel", "arbitrary")),
    )(q_cat, kvup, kvup, kr, do, lse_rows, delta_rows)


def _adamw(w, g, m, v, *, name):
    r, c = w.shape
    tr = _row_tile(r, 256)
    c1 = 1.0 - ADAM_B1
    c2 = 1.0 - ADAM_B2
    bc1 = 1.0 - ADAM_B1 ** ADAM_STEP
    bc2 = 1.0 - ADAM_B2 ** ADAM_STEP

    def body(w_ref, g_ref, m_ref, v_ref, d_ref, nm_ref, nv_ref):
        gv = g_ref[...]
        nm = ADAM_B1 * m_ref[...] + c1 * gv
        nv = ADAM_B2 * v_ref[...] + c2 * (gv * gv)
        nm_ref[...] = nm
        nv_ref[...] = nv
        d_ref[...] = -ADAM_LR * ((nm / bc1) / (jnp.sqrt(nv / bc2) + ADAM_EPS) + ADAM_WD * w_ref[...])

    blk = pl.BlockSpec((tr, c), lambda i: (i, 0))
    sds = jax.ShapeDtypeStruct((r, c), F32)
    return pl.pallas_call(
        body, name=name, grid=(r // tr,), in_specs=[blk] * 4, out_specs=(blk,) * 3,
        out_shape=(sds,) * 3, compiler_params=_params(("parallel",)),
    )(w, g, m, v)


def _add_my_half(stacked, other, core, out_dtype, *, name):
    nch, a, c = stacked.shape
    h = a // 2
    tr = _row_tile(h, 256)
    nblk = h // tr

    def body(core_ref, s_ref, p_ref, o_ref):
        o_ref[...] = (s_ref[...] + p_ref[...]).astype(o_ref.dtype)

    return pl.pallas_call(
        body, name=name,
        grid_spec=pltpu.PrefetchScalarGridSpec(
            num_scalar_prefetch=1, grid=(nch, nblk),
            in_specs=[pl.BlockSpec((None, tr, c), lambda k, i, cr: (k, cr[0] * nblk + i, 0)),
                      pl.BlockSpec((None, tr, c), lambda k, i, cr: (k, i, 0))],
            out_specs=pl.BlockSpec((None, tr, c), lambda k, i, cr: (k, i, 0))),
        out_shape=jax.ShapeDtypeStruct((nch, h, c), out_dtype),
        compiler_params=_params(("parallel", "parallel")),
    )(core, stacked, other)


def _sum_chips(parts, own, chip, *, name):
    nch, h, c = parts.shape
    tr = _row_tile(h, 256)

    def body(chip_ref, p_ref, own_ref, o_ref):
        me = chip_ref[0]

        def slot(k):
            return jnp.where(me == k, own_ref[k].astype(F32), p_ref[k].astype(F32))

        acc = slot(0) + slot(1)
        for k in range(2, nch):
            acc = acc + slot(k)
        o_ref[...] = acc

    blk = pl.BlockSpec((nch, tr, c), lambda i, cr: (0, i, 0))
    return pl.pallas_call(
        body, name=name,
        grid_spec=pltpu.PrefetchScalarGridSpec(
            num_scalar_prefetch=1, grid=(h // tr,), in_specs=[blk, blk],
            out_specs=pl.BlockSpec((tr, c), lambda i, cr: (i, 0))),
        out_shape=jax.ShapeDtypeStruct((h, c), F32), compiler_params=_params(("parallel",)),
    )(chip, parts, own)


def _join_halves(mine, other, core, *, name):
    h, c = mine.shape
    tr = _row_tile(h, 256)
    nblk = h // tr

    def body(core_ref, m_ref, s_ref, o_ref):
        is_mine = pl.program_id(0) // nblk == core_ref[0]

        @pl.when(is_mine)
        def _():
            o_ref[...] = m_ref[...]

        @pl.when(jnp.logical_not(is_mine))
        def _():
            o_ref[...] = s_ref[...]

    blk = pl.BlockSpec((tr, c), lambda i, cr: (i % nblk, 0))
    return pl.pallas_call(
        body, name=name,
        grid_spec=pltpu.PrefetchScalarGridSpec(
            num_scalar_prefetch=1, grid=(2 * nblk,), in_specs=[blk, blk],
            out_specs=pl.BlockSpec((tr, c), lambda i, cr: (i, 0))),
        out_shape=jax.ShapeDtypeStruct((2 * h, c), F32), compiler_params=_params(("arbitrary",)),
    )(core, mine, other)


def _place():
    x, y, c = lax.axis_index("x"), lax.axis_index("y"), lax.axis_index("c")
    chips = [(1 - x, y), (x, 1 - y), (1 - x, 1 - y)]
    return x, y, c, chips


def _remote(src, dst, send_sems, recv_sems, k, to):
    return pltpu.make_async_remote_copy(src_ref=src, dst_ref=dst, send_sem=send_sems.at[k],
                                        recv_sem=recv_sems.at[k], device_id=to, device_id_type=MESH)


def _hbm_call(body, name, ins, out_shapes, n_remote):
    any_spec = pl.BlockSpec(memory_space=pl.ANY)
    return pl.pallas_call(
        body, name=name, in_specs=[any_spec] * len(ins), out_specs=tuple([any_spec] * len(out_shapes)),
        out_shape=tuple(out_shapes),
        scratch_shapes=[pltpu.SemaphoreType.DMA((n_remote,)), pltpu.SemaphoreType.DMA((n_remote,))],
    )(*ins)


def _all_gather_chips(shards, *, name):
    n = len(shards)

    def body(*refs):
        ins, outs = refs[:n], refs[n:2 * n]
        send_sems, recv_sems = refs[2 * n:]
        x, y, c, chips = _place()
        me = 2 * x + y
        sent = []
        for s in range(n):
            h = ins[s].shape[0] // 2
            for j, (px, py) in enumerate(chips):
                cp = _remote(ins[s].at[pl.ds(c * h, h)], outs[s].at[me, pl.ds(c * h, h)],
                             send_sems, recv_sems, s * 6 + j, (px, py, c))
                cp.start()
                sent.append(cp)
        for s in range(n):
            h = ins[s].shape[0] // 2
            for j, (px, py) in enumerate(chips):
                slab = outs[s].at[2 * px + py, pl.ds(c * h, h)]
                _remote(slab, slab, send_sems, recv_sems, s * 6 + j, (px, py, c)).wait_recv()
                cp = _remote(slab, slab, send_sems, recv_sems, s * 6 + 3 + j, (x, y, 1 - c))
                cp.start()
                sent.append(cp)
        for s in range(n):
            h = ins[s].shape[0] // 2
            for j, (px, py) in enumerate(chips):
                slab = outs[s].at[2 * px + py, pl.ds((1 - c) * h, h)]
                _remote(slab, slab, send_sems, recv_sems, s * 6 + 3 + j, (x, y, 1 - c)).wait_recv()
        for cp in sent:
            cp.wait_send()

    out_shapes = [jax.ShapeDtypeStruct((N_CHIPS,) + s.shape, s.dtype) for s in shards]
    return _hbm_call(body, name, shards, out_shapes, 6 * n)


def _pair_send_other_half(stacked, *, name):
    n = len(stacked)

    def body(*refs):
        ins, outs = refs[:n], refs[n:2 * n]
        send_sems, recv_sems = refs[2 * n:]
        x, y, c, _chips = _place()
        sent = []
        for s in range(n):
            h = ins[s].shape[1] // 2
            cp = _remote(ins[s].at[:, pl.ds((1 - c) * h, h)], outs[s], send_sems, recv_sems, s, (x, y, 1 - c))
            cp.start()
            sent.append(cp)
        for cp in sent:
            cp.wait_recv()
        for cp in sent:
            cp.wait_send()

    out_shapes = [jax.ShapeDtypeStruct((s.shape[0], s.shape[1] // 2, s.shape[2]), s.dtype) for s in stacked]
    return _hbm_call(body, name, stacked, out_shapes, n)


def _chip_exchange(halves, *, name):
    n = len(halves)

    def body(*refs):
        ins, outs = refs[:n], refs[n:2 * n]
        send_sems, recv_sems = refs[2 * n:]
        x, y, c, chips = _place()
        me = 2 * x + y
        sent = []
        for s in range(n):
            for j, (px, py) in enumerate(chips):
                cp = _remote(ins[s].at[2 * px + py], outs[s].at[me], send_sems, recv_sems, s * 3 + j, (px, py, c))
                cp.start()
                sent.append(cp)
        for s in range(n):
            for j, (px, py) in enumerate(chips):
                slab = outs[s].at[2 * px + py]
                _remote(slab, slab, send_sems, recv_sems, s * 3 + j, (px, py, c)).wait_recv()
        for cp in sent:
            cp.wait_send()

    out_shapes = [jax.ShapeDtypeStruct(s.shape, s.dtype) for s in halves]
    return _hbm_call(body, name, halves, out_shapes, 3 * n)


def _chip_exchange_start(halves, *, name):
    n = len(halves)
    hbm = pl.BlockSpec(memory_space=pltpu.HBM)
    sem = pl.BlockSpec(memory_space=pltpu.SEMAPHORE)

    def body(*refs):
        ins, lands = refs[:n], refs[n:2 * n]
        send_sems, recv_sems = refs[2 * n], refs[2 * n + 1]
        token = refs[-1]
        x, y, c, chips = _place()
        me = 2 * x + y
        for s in range(n):
            for j, (px, py) in enumerate(chips):
                _remote(ins[s].at[2 * px + py], lands[s].at[me], send_sems, recv_sems, s * 3 + j, (px, py, c)).start()
        token[...] = jnp.zeros_like(token)

    slabs = [pltpu.HBM(s.shape, s.dtype) for s in halves]
    outs = pl.pallas_call(
        body, name=name,
        out_shape=(pltpu.SemaphoreType.DMA((3 * n,)), pltpu.SemaphoreType.DMA((3 * n,)), *slabs, *slabs,
                   jax.ShapeDtypeStruct((8, LANES), F32)),
        in_specs=[hbm] * (2 * n), out_specs=(sem, sem, *([hbm] * (2 * n)), pl.BlockSpec(memory_space=pltpu.VMEM)),
        input_output_aliases={i: 2 + i for i in range(2 * n)},
        compiler_params=pltpu.CompilerParams(has_side_effects=pltpu.SideEffectType.DATAFLOW_SIDE_EFFECTING),
    )(*[pltpu.with_memory_space_constraint(s, pltpu.HBM) for s in halves],
      *[pltpu.with_memory_space_constraint(lax.empty(s.shape, s.dtype), pltpu.HBM) for s in halves])
    return outs[0], outs[1], list(outs[2:2 + n]), list(outs[2 + n:2 + 2 * n]), outs[-1]


def _chip_exchange_wait(send_sems, recv_sems, sent, lands, after, *, name):
    n = len(sent)
    hbm = pl.BlockSpec(memory_space=pltpu.HBM)
    sem = pl.BlockSpec(memory_space=pltpu.SEMAPHORE)

    def body(*refs):
        ins, lands_in = refs[:n], refs[n:2 * n]
        send_sems, recv_sems = refs[2 * n], refs[2 * n + 1]
        x, y, c, chips = _place()
        me = 2 * x + y
        for s in range(n):
            for j, (px, py) in enumerate(chips):
                k = 2 * px + py
                _remote(ins[s].at[k], lands_in[s].at[me], send_sems, recv_sems, s * 3 + j, (px, py, c)).wait_send()
                _remote(ins[s].at[k], lands_in[s].at[k], send_sems, recv_sems, s * 3 + j, (px, py, c)).wait_recv()

    slabs = [pltpu.HBM(s.shape, s.dtype) for s in sent]
    outs = pl.pallas_call(
        body, name=name, out_shape=(*slabs, *slabs),
        in_specs=[hbm] * (2 * n) + [sem, sem, pl.BlockSpec(memory_space=pl.ANY)],
        out_specs=tuple([hbm] * (2 * n)), input_output_aliases={i: i for i in range(2 * n)},
        compiler_params=pltpu.CompilerParams(has_side_effects=pltpu.SideEffectType.DATAFLOW_SIDE_EFFECTING),
    )(*sent, *lands, send_sems, recv_sems, after)
    return list(outs[n:])


def _pair_swap(halves, *, name):
    n = len(halves)

    def body(*refs):
        ins, outs = refs[:n], refs[n:2 * n]
        send_sems, recv_sems = refs[2 * n:]
        x, y, c, _chips = _place()
        sent = []
        for s in range(n):
            cp = _remote(ins[s], outs[s], send_sems, recv_sems, s, (x, y, 1 - c))
            cp.start()
            sent.append(cp)
        for cp in sent:
            cp.wait_recv()
        for cp in sent:
            cp.wait_send()

    out_shapes = [jax.ShapeDtypeStruct(s.shape, s.dtype) for s in halves]
    return _hbm_call(body, name, halves, out_shapes, n)


def _pack_rows(parts, row_multiple):
    flat = jnp.concatenate([p.reshape(-1) for p in parts])
    quantum = row_multiple * PACK_COLS
    pad = (-flat.shape[0]) % quantum
    flat = jnp.pad(flat, (0, pad))
    return flat.reshape(-1, PACK_COLS)


def _unpack(flat, shapes):
    out, pos = [], 0
    for shp in shapes:
        size = math.prod(shp)
        out.append(flat[pos:pos + size].reshape(shp))
        pos += size
    return out


def _to_chunks_cols(full):
    r, c4 = full.shape
    return full.reshape(r, N_CHIPS, c4 // N_CHIPS).transpose(1, 0, 2)


def _from_chunks_cols(stacked):
    nch, r, c = stacked.shape
    return stacked.transpose(1, 0, 2).reshape(r, nch * c)


def _class_major(a, bl, t, dil):
    w = a.shape[-1]
    if dil == 1:
        return a.reshape(bl, 1, t, w)
    return a.reshape(bl, t // dil, dil, w).transpose(0, 2, 1, 3)


def _natural(a):
    bl, dil, ln, w = a.shape
    if dil == 1:
        return a.reshape(bl * ln, w)
    return a.transpose(0, 2, 1, 3).reshape(bl * ln * dil, w)


def _train_step(x, positions, a_pre_norm, a_w_in, a_w_out, a_post_norm, kv_norm, kv_w_down, kv_latent_norm,
                kv_w_up, b_pre_norm, b_w_in, b_q_norm, b_w_q_up, b_w_out, b_post_norm, loss_target, moments):
    bl, t, d = x.shape
    n = bl * t
    qb = t // A_DILATIONS[-1]
    tq = _tile(t, 256)
    dq4 = d // N_CHIPS
    chip = 2 * lax.axis_index("x") + lax.axis_index("y")
    chip_arr = chip.astype(jnp.int32).reshape(1)
    core_arr = lax.axis_index("c").astype(jnp.int32).reshape(1)

    w_in_a_s = a_w_in[0].astype(BF16)
    outs_s = jnp.concatenate([a_w_out[0], b_w_out[0]], axis=0).astype(BF16)
    small_shapes = [kv_w_down.shape, kv_w_up.shape, b_w_in[0].shape, b_w_q_up[0].shape]
    small_s = _pack_rows([kv_w_down, kv_w_up, b_w_in[0], b_w_q_up[0]], 32).astype(BF16)
    gains_s = jnp.pad(jnp.concatenate([a_pre_norm[0], a_post_norm[0]]), (0, 16 * LANES - 2 * dq4)).reshape(16, LANES)
    shards = [w_in_a_s, outs_s, small_s, gains_s]
    gathered = _all_gather_chips(shards, name="gather_weights")
    g_in_a, g_outs, g_small, g_gains = [lax.dynamic_update_index_in_dim(g, s, chip, 0)
                                        for g, s in zip(gathered, shards)]

    w_in_a = _from_chunks_cols(g_in_a)
    w_out_a = g_outs[:, :A_WIDTH // N_CHIPS].reshape(A_WIDTH, d)
    w_out_b = g_outs[:, A_WIDTH // N_CHIPS:].reshape(B_WIDTH, d)
    sm = [_unpack(g_small[k].reshape(-1), small_shapes) for k in range(N_CHIPS)]
    w_down = jnp.concatenate([sm[k][0] for k in range(N_CHIPS)], axis=0)
    w_up = jnp.concatenate([sm[k][1] for k in range(N_CHIPS)], axis=1)
    w_in_b = jnp.concatenate([sm[k][2] for k in range(N_CHIPS)], axis=1)
    w_q_up = jnp.concatenate([sm[k][3] for k in range(N_CHIPS)], axis=1)
    gflat = g_gains.reshape(N_CHIPS, -1)
    g_a_pre = gflat[:, :dq4].reshape(1, d)
    g_a_post = gflat[:, dq4:2 * dq4].reshape(1, d)

    w_up_h = w_up.reshape(B_KV_LORA, B_HEADS, B_NOPE + B_VDIM)
    w_up_k = jnp.pad(w_up_h[:, :, :B_NOPE], ((0, 0), (0, 0), (0, LANES - B_NOPE))).reshape(B_KV_LORA, B_HEADS * LANES)
    w_up_v = w_up_h[:, :, B_NOPE:].reshape(B_KV_LORA, B_WIDTH)
    w_up_cat = jnp.concatenate([w_up_k, w_up_v], axis=1)
    w_q_up_p = jnp.pad(w_q_up.reshape(B_Q_LORA, B_HEADS, B_QK_DIM),
                       ((0, 0), (0, 0), (0, LANES - B_QK_DIM))).reshape(B_Q_LORA, B_HEADS * LANES)
    zeros_d = lambda c: jnp.zeros((d, c), BF16)
    w_down_p = jnp.concatenate([w_down[:, :B_KV_LORA], zeros_d(B_NOPE), w_down[:, B_KV_LORA:],
                                zeros_d(LANES - B_NOPE - B_ROPE)], axis=1)
    w_cq = w_in_b[:, :B_Q_LORA]
    w_z = w_in_b[:, B_Q_LORA:]

    tabs_a = _rope_tables(positions, A_ROPE_THETA, 0)
    tabs_b = _rope_tables(positions, B_ROPE_THETA, B_NOPE)

    h0 = x.reshape(n, d)
    hn_a = _rms_fwd(h0, g_a_pre, BF16, name="a_pre_norm")
    is_qk = lambda j: j != 2
    is_q = lambda j: j == 0
    z_blk_a = 3 * A_GROUPS
    z_a = _matmul(hn_a, w_in_a, "nn", BF16, name="a_proj_z", b_cols=(z_blk_a, 1))
    o_groups, lse_groups, qkv_cm, hn_cm, tabs_cm = [], [], [], [], []
    for g, dil in enumerate(A_DILATIONS):
        flat = lambda a: _class_major(a, bl, t, dil).reshape(n, a.shape[-1])
        hn_g = hn_a if dil == 1 else flat(hn_a)
        tabs_g = tabs_a if dil == 1 else lax.optimization_barrier(tuple(flat(tb) for tb in tabs_a))
        proj_g = _matmul(hn_g, w_in_a, "nn", BF16, name=f"a_proj_{g}", rope=(tabs_g, is_qk),
                         out_scale=(A_SCALE * LOG2E, is_q), b_cols=(3 * g, 3))
        src = proj_g.reshape(bl, dil, t // dil, 3 * A_WIDTH)
        hn_cm.append(hn_g)
        tabs_cm.append(tabs_g)
        qkv_cm.append(src)
        o_g, lse_g = _attn_a_fwd(src, 0, qb, BF16, name=f"attn_a_fwd_{g}")
        o_groups.append(_natural(o_g))
        lse_groups.append(_natural(lse_g))
    ypre_a, om_a, lse_a = _merge_gate_fwd(o_groups, lse_groups, z_a, 0)
    y_a = _matmul(ypre_a, w_out_a, "nn", F32, name="a_out")
    g_kvn = kv_norm.reshape(1, d)
    g_lat = kv_latent_norm.reshape(1, B_KV_LORA)
    h1, hn_kv, hn_b = _post_norm_block(y_a, g_a_post, h0, [g_kvn, b_pre_norm], name="a_post_norm")

    ckr = _matmul(hn_kv, w_down_p, "nn", F32, name="kv_down")
    c_kv, k_rope = _kv_latent_fwd(ckr, g_lat, tabs_b)
    kvup = _matmul(c_kv, w_up_cat, "nn", BF16, name="kv_up")
    z_b = _matmul(hn_b, w_z, "nn", BF16, name="b_proj_z")
    cq_raw = _matmul(hn_b, w_cq, "nn", F32, name="b_proj_q")
    c_q = _rms_fwd(cq_raw, b_q_norm, BF16, name="b_q_norm")
    always = lambda j: True
    q_cat = _matmul(c_q, w_q_up_p, "nn", BF16, name="b_q_up", rope=(tabs_b, always),
                    out_scale=(B_SCALE * LOG2E, always))
    r3 = lambda a: a.reshape(bl, t, a.shape[-1])
    tabs_b3 = tuple(r3(tb) for tb in tabs_b)
    ypre_b, o_b, lse_b, lse_rows_b = _mla_fwd(r3(q_cat), r3(kvup), r3(k_rope), r3(z_b), tq)
    y_b = _matmul(ypre_b.reshape(n, B_WIDTH), w_out_b, "nn", F32, name="b_out")
    dh2, loss_part = _post_norm_loss(y_b, b_post_norm, h1, loss_target.reshape(n, d))

    dy_b, dg_b_post = _rms_bwd(y_b, b_post_norm, dh2, BF16, name="b_post_norm_bwd")
    dypre_b = _matmul(dy_b, w_out_b, "nt", BF16, name="b_out_dx")
    dw_out_b = _matmul(ypre_b.reshape(n, B_WIDTH), dy_b, "tn", F32, name="b_out_dw", tm=1024, tk=2048)
    do_b, dz_b = _gate_bwd(dypre_b, o_b.reshape(n, B_WIDTH), z_b, 0, name="b_gate_bwd", with_delta=False)
    dq_cat, delta_rows_b = _mla_dq(r3(q_cat), r3(kvup), r3(k_rope), r3(do_b), o_b, lse_b, tabs_b3, tq)
    dq_cat = dq_cat.reshape(n, -1)
    dk_cat, dv_b = _mla_dkv(r3(q_cat), r3(kvup), r3(k_rope), r3(do_b), lse_rows_b, delta_rows_b, tq)
    dk_cat, dv_b = dk_cat.reshape(n, -1), dv_b.reshape(n, -1)
    dcq_n = _matmul(dq_cat, w_q_up_p, "nt", F32, name="b_q_up_dx")
    dw_q_up_p = _matmul(c_q, dq_cat, "tn", F32, name="b_q_up_dw", tm=1024, tk=2048)
    dcq, dg_b_q = _rms_bwd(cq_raw, b_q_norm, dcq_n, BF16, name="b_q_norm_bwd")
    dhn_b = _matmul(dz_b, w_z, "nt", F32, name="b_proj_z_dx")
    dhn_b = _matmul(dcq, w_cq, "nt", F32, name="b_proj_q_dx", add=dhn_b)
    dw_z = _matmul(hn_b, dz_b, "tn", F32, name="b_proj_z_dw", tm=1024, tk=2048)
    dw_cq = _matmul(hn_b, dcq, "tn", F32, name="b_proj_q_dw", tm=1024, tk=2048)
    dckv_n = _matmul(dk_cat, w_up_k, "nt", F32, name="kv_up_k_dx")
    dckv_n = _matmul(dv_b, w_up_v, "nt", F32, name="kv_up_v_dx", add=dckv_n)
    dw_up_k = _matmul(c_kv, dk_cat, "tn", F32, name="kv_up_k_dw", tm=1024, tk=2048)
    dw_up_v = _matmul(c_kv, dv_b, "tn", F32, name="kv_up_v_dw", tm=1024, tk=2048)
    dckr, dg_lat = _kv_latent_bwd(dckv_n, ckr, g_lat, dk_cat, tabs_b)
    dhn_kv = _matmul(dckr, w_down_p, "nt", F32, name="kv_down_dx")
    dw_down_p = _matmul(hn_kv, dckr, "tn", F32, name="kv_down_dw", tm=1024, tk=2048)
    dh1, dg_b_pre, dg_kvn = _rms_bwd_pair(h1, b_pre_norm, dhn_b, g_kvn, dhn_kv, dh2, name="h1_norms_bwd")

    dy_a, dg_a_post = _rms_bwd(y_a, g_a_post, dh1, BF16, name="a_post_norm_bwd")
    dypre_a = _matmul(dy_a, w_out_a, "nt", BF16, name="a_out_dx")
    dw_out_a = _matmul(ypre_a, dy_a, "tn", F32, name="a_out_dw", tm=1024, tk=2048)
    do_a, dz_a, delta_a = _gate_bwd(dypre_a, om_a, z_a, 0, name="a_gate_bwd", with_delta=True)
    dw_cols = A_IN_WIDTH // N_CHIPS
    dw_tn = _tile(dw_cols, 512)
    dw_kwargs = dict(tm=1024, tn=dw_tn, tk=2048, out_chunk_blocks=dw_cols // dw_tn)
    r_big = _matmul(hn_a, dz_a, "tn", F32, name="a_proj_dw_z", out_full=(N_CHIPS, d, dw_cols),
                    out_joff=z_blk_a * A_WIDTH // dw_tn, **dw_kwargs)
    dqkvs = []
    for g, dil in enumerate(A_DILATIONS):
        cm = lambda a: _class_major(a, bl, t, dil)
        swap = lambda a: jnp.swapaxes(a, 2, 3)
        lse_cm, delta_cm = cm(lse_a), cm(delta_a)
        tabs_g = tuple(tb.reshape(bl, dil, t // dil, LANES) for tb in tabs_cm[g])
        dqkv = _attn_a_bwd(qkv_cm[g], 0, cm(do_a), lse_cm, delta_cm, swap(lse_cm), swap(delta_cm),
                           tabs_g, qb, name=f"attn_a_bwd_{g}").reshape(n, 3 * A_WIDTH)
        dqkvs.append(dqkv)
        r_big = _matmul(hn_cm[g], dqkv, "tn", F32, name=f"a_proj_dw_{g}", out_into=r_big,
                        out_joff=3 * g * A_WIDTH // dw_tn, **dw_kwargs)
    r_outs = jnp.concatenate([dw_out_a.reshape(N_CHIPS, A_WIDTH // N_CHIPS, d),
                              dw_out_b.reshape(N_CHIPS, B_WIDTH // N_CHIPS, d)], axis=1)

    bulk = [r_big, r_outs]
    recv_b = _pair_send_other_half(bulk, name="reduce_pair_send")
    halves_b = [_add_my_half(s, p, core_arr, BF16, name=f"reduce_pair_add_{i}")
                for i, (s, p) in enumerate(zip(bulk, recv_b))]
    send_sems, recv_sems, sent_b, lands_b, token = _chip_exchange_start(halves_b, name="reduce_exchange_start")

    dhn_a = _matmul(dz_a, w_in_a, "nt", F32, name="a_proj_dx_z", b_koff=z_blk_a, after=token)
    dhn_more = []
    for g, dil in enumerate(A_DILATIONS):
        tk_dx = 3 * A_WIDTH
        if dil == 1:
            dhn_a = _matmul(dqkvs[g], w_in_a, "nt", F32, name=f"a_proj_dx_{g}", add=dhn_a, tk=tk_dx, b_koff=g,
                            after=token)
        else:
            part = _matmul(dqkvs[g], w_in_a, "nt", BF16, name=f"a_proj_dx_{g}", tk=tk_dx, b_koff=g, after=token)
            dhn_more.append(_natural(part.reshape(bl, dil, t // dil, d)))
    grad_x, dg_a_pre = _rms_bwd(h0, g_a_pre, dhn_a, F32, name="a_pre_norm_bwd", adds=(dh1,),
                                dy_more=tuple(dhn_more))

    dw_up = jnp.concatenate([dw_up_k.reshape(B_KV_LORA, B_HEADS, LANES)[:, :, :B_NOPE],
                             dw_up_v.reshape(B_KV_LORA, B_HEADS, B_VDIM)], axis=2).reshape(B_KV_LORA, -1)
    dw_q_up = dw_q_up_p.reshape(B_Q_LORA, B_HEADS, LANES)[:, :, :B_QK_DIM].reshape(B_Q_LORA, -1)
    dw_down = jnp.concatenate([dw_down_p[:, :B_KV_LORA], dw_down_p[:, B_KV_LORA + B_NOPE:B_KV_LORA + B_NOPE + B_ROPE]], axis=1)
    dw_in_b = jnp.concatenate([dw_cq, dw_z], axis=1)
    vec_rep = [dg_kvn.reshape(-1), dg_lat.reshape(-1), dg_b_pre.reshape(-1), dg_b_q.reshape(-1),
               dg_b_post.reshape(-1), loss_part.reshape(-1)]
    vec_shapes = [(dq4,), (dq4,)] + [v.shape for v in vec_rep]
    down_c = dw_down.reshape(N_CHIPS, dq4, -1)
    up_c = _to_chunks_cols(dw_up)
    inb_c = _to_chunks_cols(dw_in_b)
    qup_c = _to_chunks_cols(dw_q_up)
    small_chunks = []
    for k in range(N_CHIPS):
        vecs = [dg_a_pre.reshape(-1)[k * dq4:(k + 1) * dq4], dg_a_post.reshape(-1)[k * dq4:(k + 1) * dq4]] + vec_rep
        small_chunks.append(_pack_rows([down_c[k], up_c[k], inb_c[k], qup_c[k]] + vecs, 32))
    r_small = jnp.stack(small_chunks)

    recv_s = _pair_send_other_half([r_small], name="reduce_pair_send_small")
    halves_s = [_add_my_half(r_small, recv_s[0], core_arr, F32, name="reduce_pair_add_small")]
    parts_s = list(_chip_exchange(halves_s, name="reduce_exchange_small"))
    parts_b = _chip_exchange_wait(send_sems, recv_sems, sent_b, lands_b, grad_x, name="reduce_exchange_wait")
    sums = [_sum_chips(p, own, chip_arr, name=f"reduce_chip_sum_{i}")
            for i, (p, own) in enumerate(zip(parts_b + parts_s, sent_b + halves_s))]
    others = _pair_swap(sums, name="reduce_pair_swap")
    g_big, g_outs_r, g_small_r = [_join_halves(m, o, core_arr, name=f"reduce_join_{i}")
                                  for i, (m, o) in enumerate(zip(sums, others))]

    grads = {}
    grads["a_w_in"] = g_big
    grads["a_w_out"] = g_outs_r[:A_WIDTH // N_CHIPS]
    grads["b_w_out"] = g_outs_r[A_WIDTH // N_CHIPS:]
    small_out_shapes = [down_c.shape[1:], up_c.shape[1:], inb_c.shape[1:], qup_c.shape[1:]] + vec_shapes
    (grads["kv_w_down"], grads["kv_w_up"], grads["b_w_in"], grads["b_w_q_up"], grads["a_pre_norm"],
     grads["a_post_norm"], grads["kv_norm"], grads["kv_latent_norm"], grads["b_pre_norm"], grads["b_q_norm"],
     grads["b_post_norm"], loss_sum) = _unpack(g_small_r.reshape(-1), small_out_shapes)

    weights = dict(a_pre_norm=a_pre_norm, a_w_in=a_w_in, a_w_out=a_w_out, a_post_norm=a_post_norm, kv_norm=kv_norm,
                   kv_w_down=kv_w_down, kv_latent_norm=kv_latent_norm, kv_w_up=kv_w_up, b_pre_norm=b_pre_norm,
                   b_w_in=b_w_in, b_q_norm=b_q_norm, b_w_q_up=b_w_q_up, b_w_out=b_w_out, b_post_norm=b_post_norm)
    names = list(weights)
    out_g, out_d, out_m, out_v = [], [], [], []
    for i, nm in enumerate(names):
        w = weights[nm]
        two_d = (1, w.shape[0]) if w.ndim == 1 else (w.shape[-2], w.shape[-1])
        gw = grads[nm].reshape(two_d)
        dlt, new_m, new_v = _adamw(w.reshape(two_d), gw, moments[i].reshape(two_d),
                                   moments[len(names) + i].reshape(two_d), name=f"adamw_{nm}")
        out_g.append(gw.reshape(w.shape))
        out_d.append(dlt.reshape(w.shape))
        out_m.append(new_m.reshape(w.shape))
        out_v.append(new_v.reshape(w.shape))
    return (loss_sum.reshape(()), grad_x.reshape(bl, t, d), *out_g, *out_d, *out_m, *out_v)


def kernel(x, positions, a_pre_norm, a_w_in, a_w_out, a_post_norm, kv_norm, kv_w_down, kv_latent_norm, kv_w_up, b_pre_norm, b_w_in, b_q_norm, b_w_q_up, b_w_out, b_post_norm, loss_target, m_a_pre_norm, m_a_w_in, m_a_w_out, m_a_post_norm, m_kv_norm, m_kv_w_down, m_kv_latent_norm, m_kv_w_up, m_b_pre_norm, m_b_w_in, m_b_q_norm, m_b_w_q_up, m_b_w_out, m_b_post_norm, v_a_pre_norm, v_a_w_in, v_a_w_out, v_a_post_norm, v_kv_norm, v_kv_w_down, v_kv_latent_norm, v_kv_w_up, v_b_pre_norm, v_b_w_in, v_b_q_norm, v_b_w_q_up, v_b_w_out, v_b_post_norm):
    moments = (m_a_pre_norm, m_a_w_in, m_a_w_out, m_a_post_norm, m_kv_norm, m_kv_w_down, m_kv_latent_norm, m_kv_w_up,
               m_b_pre_norm, m_b_w_in, m_b_q_norm, m_b_w_q_up, m_b_w_out, m_b_post_norm,
               v_a_pre_norm, v_a_w_in, v_a_w_out, v_a_post_norm, v_kv_norm, v_kv_w_down, v_kv_latent_norm, v_kv_w_up,
               v_b_pre_norm, v_b_w_in, v_b_q_norm, v_b_w_q_up, v_b_w_out, v_b_post_norm)
    return _train_step(x, positions, a_pre_norm, a_w_in, a_w_out, a_post_norm, kv_norm, kv_w_down, kv_latent_norm,
                       kv_w_up, b_pre_norm, b_w_in, b_q_norm, b_w_q_up, b_w_out, b_post_norm, loss_target, moments)
```

```python
import math

import jax
import jax.numpy as jnp
from jax import lax
from jax.experimental import pallas as pl
from jax.experimental.pallas import tpu as pltpu

F32 = jnp.float32
BF16 = jnp.bfloat16
MESH = pl.DeviceIdType.MESH

NORM_EPS = 1e-6
NEG = -1e30
LANES = 128
VMEM_LIMIT = 56 * 1024 * 1024
LOG2E = math.log2(math.e)
LN2 = math.log(2.0)

A_GROUPS = 3
A_DILATIONS = (1, 4, 16)
A_HEADS = 8
A_HEAD_DIM = 128
A_WIDTH = A_HEADS * A_HEAD_DIM
A_ROPE_THETA = 500000.0
A_IN_WIDTH = A_GROUPS * 3 * A_WIDTH + A_WIDTH
A_SCALE = A_HEAD_DIM ** -0.5

B_HEADS = 16
B_NOPE = 64
B_ROPE = 32
B_QK_DIM = B_NOPE + B_ROPE
B_VDIM = 64
B_WIDTH = B_HEADS * B_VDIM
B_Q_LORA = 384
B_KV_LORA = 256
B_ROPE_THETA = 10000.0
B_SCALE = B_QK_DIM ** -0.5

ADAM_LR = 0.001
ADAM_B1 = 0.9
ADAM_B2 = 0.999
ADAM_EPS = 1e-08
ADAM_WD = 0.01
ADAM_STEP = 10

N_CHIPS = 4
PACK_COLS = 512


def _params(sem=None):
    return pltpu.CompilerParams(dimension_semantics=sem, vmem_limit_bytes=VMEM_LIMIT)


def _tile(n, want):
    t = min(n, want)
    assert n % t == 0, (n, want)
    return t


def _row_tile(n, want):
    for t in range(min(n, want), 0, -1):
        if n % t == 0 and (t % 16 == 0 or t == n):
            return t
    return n


def _rope_tables(positions, theta, lane0):
    half = 16
    inv_freq = 1.0 / (theta ** (jnp.arange(half, dtype=F32) * (2.0 / (2 * half))))
    n = positions.size
    per_row = LANES // half
    pos = jnp.repeat(positions.astype(F32).reshape(n // per_row, per_row), half, axis=1)
    ang = pos * jnp.tile(inv_freq, per_row)
    cos, sin = lax.optimization_barrier((jnp.cos(ang), jnp.sin(ang)))
    cos, sin = cos.reshape(n, half), sin.reshape(n, half)
    pre = jnp.zeros((n, lane0), F32)
    post = jnp.zeros((n, LANES - lane0 - 2 * half), F32)
    z16 = jnp.zeros((n, half), F32)
    c = jnp.concatenate([pre + 1.0, cos, cos, post + 1.0], axis=1)
    sa = jnp.concatenate([pre, -sin, z16, post], axis=1)
    sb = jnp.concatenate([pre, z16, sin, post], axis=1)
    return lax.optimization_barrier((c, sa, sb))


def _rope_apply(x, c, sa, sb, sign):
    k = x.shape[1] // LANES
    if k > 1:
        c, sa, sb = (jnp.concatenate([t] * k, axis=1) for t in (c, sa, sb))
    w = x.shape[1]
    up = pltpu.roll(x, w - 16, 1)
    dn = pltpu.roll(x, 16, 1)
    if sign > 0:
        return x * c + up * sa + dn * sb
    return x * c - up * sa - dn * sb


def _matmul(a, b, mode, out_dtype, *, name, tm=None, tn=1024, tk=None, add=None, rope=None,
            out_scale=None, b_koff=0, b_cols=None, out_into=None, out_full=None, out_joff=0,
            out_chunk_blocks=None, after=None):
    if mode == "nn":
        m, k = a.shape
        n = b.shape[1]
    elif mode == "nt":
        m, k = a.shape
        n = b.shape[0]
    else:
        k, m = a.shape
        n = b.shape[1]
    b_j0 = 0
    if b_cols is not None:
        tn = _tile(n, tn)
        b_j0, n = b_cols[0], b_cols[1] * tn
    if tm is None:
        tm = 512 if (rope is not None or mode == "nt") else (2048 if k <= 512 else 1024)
    if tk is None:
        tk = 3072 if mode == "nt" else 1024
    tm, tn, tk = _tile(m, tm), _tile(n, tn), _tile(k, tk)
    nk = k // tk
    if mode == "nn":
        a_spec = pl.BlockSpec((tm, tk), lambda j, i, kk: (i, kk))
        b_spec = pl.BlockSpec((tk, tn), lambda j, i, kk: (kk, j + b_j0))
        dims = (((1,), (0,)), ((), ()))
    elif mode == "nt":
        a_spec = pl.BlockSpec((tm, tk), lambda j, i, kk: (i, kk))
        b_spec = pl.BlockSpec((tn, tk), lambda j, i, kk: (j, kk + b_koff))
        dims = (((1,), (1,)), ((), ()))
    else:
        a_spec = pl.BlockSpec((tk, tm), lambda j, i, kk: (kk, i))
        b_spec = pl.BlockSpec((tk, tn), lambda j, i, kk: (kk, j))
        dims = (((0,), (0,)), ((), ()))
    operands = [a, b]
    in_specs = [a_spec, b_spec]
    if add is not None:
        operands.append(add)
        in_specs.append(pl.BlockSpec((tm, tn), lambda j, i, kk: (i, j)))
    if rope is not None:
        tables, rope_pred = rope
        for t in tables:
            operands.append(t)
            in_specs.append(pl.BlockSpec((tm, LANES), lambda j, i, kk: (i, 0)))
    aliases = {}
    if out_into is not None:
        aliases = {len(operands): 0}
        operands.append(out_into)
        in_specs.append(pl.BlockSpec(memory_space=pl.ANY))
        out_shape = jax.ShapeDtypeStruct(out_into.shape, out_into.dtype)
    elif out_full is not None:
        out_shape = jax.ShapeDtypeStruct(out_full, out_dtype)
    else:
        out_shape = jax.ShapeDtypeStruct((m, n), out_dtype)
    if after is not None:
        operands.append(after)
        in_specs.append(pl.BlockSpec(memory_space=pl.ANY))
    if out_chunk_blocks is not None:
        out_spec = pl.BlockSpec((None, tm, tn), lambda j, i, kk: ((j + out_joff) // out_chunk_blocks, i,
                                                                  (j + out_joff) % out_chunk_blocks))
    else:
        out_spec = pl.BlockSpec((tm, tn), lambda j, i, kk: (i, j + out_joff))

    def body(*refs):
        a_ref, b_ref = refs[0], refs[1]
        pos = 2
        add_ref = None
        if add is not None:
            add_ref = refs[pos]
            pos += 1
        tab_refs = None
        if rope is not None:
            tab_refs = refs[pos:pos + 3]
            pos += 3
        if out_into is not None:
            pos += 1
        if after is not None:
            pos += 1
        o_ref = refs[pos]
        acc_ref = refs[pos + 1] if nk > 1 else None

        def finish(res):
            if add_ref is not None:
                res = res + add_ref[...].astype(F32)
            if tab_refs is None:
                o_ref[...] = res.astype(o_ref.dtype)
                return
            j = pl.program_id(0)
            flag = rope_pred(j)

            roped = _rope_apply(res, tab_refs[0][...], tab_refs[1][...], tab_refs[2][...], 1)
            if out_scale is not None:
                value, scale_pred = out_scale
                use = scale_pred(j)
                roped = roped * (value if use is True else jnp.where(use, value, 1.0))
            if flag is True:
                o_ref[...] = roped.astype(o_ref.dtype)
                return

            @pl.when(flag)
            def _():
                o_ref[...] = roped.astype(o_ref.dtype)

            @pl.when(jnp.logical_not(flag))
            def _():
                o_ref[...] = res.astype(o_ref.dtype)

        part = lax.dot_general(a_ref[...].astype(BF16), b_ref[...].astype(BF16), dims,
                               preferred_element_type=F32)
        if nk == 1:
            finish(part)
            return
        kk = pl.program_id(2)

        @pl.when(kk == 0)
        def _():
            acc_ref[...] = part

        @pl.when(kk > 0)
        def _():
            acc_ref[...] += part

        @pl.when(kk == nk - 1)
        def _():
            finish(acc_ref[...])

    return pl.pallas_call(
        body, name=name, grid=(n // tn, m // tm, nk), in_specs=in_specs, out_specs=out_spec,
        out_shape=out_shape, input_output_aliases=aliases,
        scratch_shapes=[pltpu.VMEM((tm, tn), F32)] if nk > 1 else [],
        compiler_params=_params(("parallel", "parallel", "arbitrary")),
    )(*operands)


def _rms_fwd(x, g, out_dtype, *, name, add=None, tr=512):
    n, d = x.shape
    tr = _tile(n, tr)
    row = pl.BlockSpec((tr, d), lambda i: (i, 0))
    vec = pl.BlockSpec((1, d), lambda i: (0, 0))

    def body(*refs):
        x_ref, g_ref = refs[0], refs[1]
        o_ref = refs[-1]
        xv = x_ref[...].astype(F32)
        r = lax.rsqrt(jnp.mean(xv * xv, axis=-1, keepdims=True) + NORM_EPS)
        y = xv * r * g_ref[...]
        if add is not None:
            y = refs[2][...] + y
        o_ref[...] = y.astype(o_ref.dtype)

    ops = [x, g] + ([add] if add is not None else [])
    specs = [row, vec] + ([row] if add is not None else [])
    return pl.pallas_call(
        body, name=name, grid=(n // tr,), in_specs=specs, out_specs=row,
        out_shape=jax.ShapeDtypeStruct((n, d), out_dtype), compiler_params=_params(("parallel",)),
    )(*ops)


def _rms_bwd(x, g, dy, out_dtype, *, name, adds=(), dy_more=(), tr=512):
    n, d = x.shape
    tr = _tile(n, tr)
    steps = n // tr
    row = pl.BlockSpec((tr, d), lambda i: (i, 0))
    vec = pl.BlockSpec((1, d), lambda i: (0, 0))
    na = len(adds) + len(dy_more)

    def body(*refs):
        x_ref, g_ref, dy_ref = refs[:3]
        add_refs = refs[3:3 + len(adds)]
        more_refs = refs[3 + len(adds):3 + na]
        dx_ref, dg_ref, acc_ref = refs[3 + na:]
        i = pl.program_id(0)
        xv = x_ref[...].astype(F32)
        r = lax.rsqrt(jnp.mean(xv * xv, axis=-1, keepdims=True) + NORM_EPS)
        xh = xv * r
        dyv = dy_ref[...].astype(F32)
        for m_ref in more_refs:
            dyv = dyv + m_ref[...].astype(F32)
        part = (dyv * xh).reshape(tr // 8, 8, d).sum(axis=0)

        @pl.when(i == 0)
        def _():
            acc_ref[...] = part

        @pl.when(i > 0)
        def _():
            acc_ref[...] += part

        t = dyv * g_ref[...]
        dx = r * (t - xh * jnp.mean(t * xh, axis=-1, keepdims=True))
        for a_ref in add_refs:
            dx = dx + a_ref[...].astype(F32)
        dx_ref[...] = dx.astype(dx_ref.dtype)

        @pl.when(i == steps - 1)
        def _():
            dg_ref[...] = jnp.sum(acc_ref[...], axis=0, keepdims=True)

    return pl.pallas_call(
        body, name=name, grid=(steps,), in_specs=[row, vec, row] + [row] * na,
        out_specs=(row, vec),
        out_shape=(jax.ShapeDtypeStruct((n, d), out_dtype), jax.ShapeDtypeStruct((1, d), F32)),
        scratch_shapes=[pltpu.VMEM((8, d), F32)], compiler_params=_params(("arbitrary",)),
    )(x, g, dy, *adds, *dy_more)


def _rms(xv, g):
    return xv * lax.rsqrt(jnp.mean(xv * xv, axis=-1, keepdims=True) + NORM_EPS) * g


def _post_norm_block(y, g, h_in, next_gains, *, name, tr=1024):
    n, d = y.shape
    tr = _tile(n, tr)
    nk = len(next_gains)
    row = pl.BlockSpec((tr, d), lambda i: (i, 0))
    vec = pl.BlockSpec((1, d), lambda i: (0, 0))

    def body(*refs):
        y_ref, g_ref, h_ref = refs[:3]
        gk_refs = refs[3:3 + nk]
        o_ref = refs[3 + nk]
        hn_refs = refs[4 + nk:]
        h = h_ref[...] + _rms(y_ref[...], g_ref[...])
        o_ref[...] = h
        for gk_ref, hn_ref in zip(gk_refs, hn_refs):
            hn_ref[...] = _rms(h, gk_ref[...]).astype(BF16)

    return pl.pallas_call(
        body, name=name, grid=(n // tr,), in_specs=[row, vec, row] + [vec] * nk,
        out_specs=(row,) * (1 + nk),
        out_shape=(jax.ShapeDtypeStruct((n, d), F32),) + (jax.ShapeDtypeStruct((n, d), BF16),) * nk,
        compiler_params=_params(("parallel",)),
    )(y, g, h_in, *next_gains)


def _post_norm_loss(y, g, h_in, target, *, tr=1024):
    n, d = y.shape
    tr = _tile(n, tr)
    steps = n // tr
    row = pl.BlockSpec((tr, d), lambda i: (i, 0))

    def body(y_ref, g_ref, h_ref, t_ref, dh_ref, loss_ref, acc_ref):
        i = pl.program_id(0)
        e = h_ref[...] + _rms(y_ref[...], g_ref[...]) - t_ref[...]
        dh_ref[...] = e / d
        part = (e * e).reshape(tr // 8, 8, d).sum(axis=0)

        @pl.when(i == 0)
        def _():
            acc_ref[...] = part

        @pl.when(i > 0)
        def _():
            acc_ref[...] += part

        @pl.when(i == steps - 1)
        def _():
            s = jnp.sum(jnp.sum(acc_ref[...], axis=-1, keepdims=True), axis=0, keepdims=True)
            loss_ref[...] = 0.5 * s / d

    return pl.pallas_call(
        body, name="b_post_norm_loss", grid=(steps,),
        in_specs=[row, pl.BlockSpec((1, d), lambda i: (0, 0)), row, row],
        out_specs=(row, pl.BlockSpec((1, 1), lambda i: (0, 0))),
        out_shape=(jax.ShapeDtypeStruct((n, d), F32), jax.ShapeDtypeStruct((1, 1), F32)),
        scratch_shapes=[pltpu.VMEM((8, d), F32)], compiler_params=_params(("arbitrary",)),
    )(y, g, h_in, target)


def _rms_bwd_pair(x, g1, dy1, g2, dy2, add, *, name, tr=512):
    n, d = x.shape
    tr = _tile(n, tr)
    steps = n // tr
    row = pl.BlockSpec((tr, d), lambda i: (i, 0))
    vec = pl.BlockSpec((1, d), lambda i: (0, 0))

    def body(x_ref, g1_ref, d1_ref, g2_ref, d2_ref, add_ref, dx_ref, dg1_ref, dg2_ref, acc_ref):
        i = pl.program_id(0)
        xv = x_ref[...]
        r = lax.rsqrt(jnp.mean(xv * xv, axis=-1, keepdims=True) + NORM_EPS)
        xh = xv * r
        dx = add_ref[...]
        for k, (g_ref, d_ref) in enumerate(((g1_ref, d1_ref), (g2_ref, d2_ref))):
            dyv = d_ref[...].astype(F32)
            part = (dyv * xh).reshape(tr // 8, 8, d).sum(axis=0)

            @pl.when(i == 0)
            def _(part=part, k=k):
                acc_ref[k] = part

            @pl.when(i > 0)
            def _(part=part, k=k):
                acc_ref[k] += part

            t = dyv * g_ref[...]
            dx = dx + r * (t - xh * jnp.mean(t * xh, axis=-1, keepdims=True))
        dx_ref[...] = dx

        @pl.when(i == steps - 1)
        def _():
            dg1_ref[...] = jnp.sum(acc_ref[0], axis=0, keepdims=True)
            dg2_ref[...] = jnp.sum(acc_ref[1], axis=0, keepdims=True)

    return pl.pallas_call(
        body, name=name, grid=(steps,), in_specs=[row, vec, row, vec, row, row],
        out_specs=(row, vec, vec),
        out_shape=(jax.ShapeDtypeStruct((n, d), F32), jax.ShapeDtypeStruct((1, d), F32),
                   jax.ShapeDtypeStruct((1, d), F32)),
        scratch_shapes=[pltpu.VMEM((2, 8, d), F32)], compiler_params=_params(("arbitrary",)),
    )(x, g1, dy1, g2, dy2, add)


def _kv_latent_fwd(ckr, g_lat, tabs, *, tr=512):
    n = ckr.shape[0]
    tr = _tile(n, tr)
    lat = B_KV_LORA

    def body(c_ref, k_ref, g_ref, tc, tsa, tsb, ckv_ref, kr_ref):
        xv = c_ref[...]
        r = lax.rsqrt(jnp.mean(xv * xv, axis=-1, keepdims=True) + NORM_EPS)
        ckv_ref[...] = (xv * r * g_ref[...]).astype(BF16)
        kr_ref[...] = _rope_apply(k_ref[...], tc[...], tsa[...], tsb[...], 1).astype(BF16)

    tab = pl.BlockSpec((tr, LANES), lambda i: (i, 0))
    return pl.pallas_call(
        body, name="kv_latent_fwd", grid=(n // tr,),
        in_specs=[pl.BlockSpec((tr, lat), lambda i: (i, 0)),
                  pl.BlockSpec((tr, LANES), lambda i: (i, lat // LANES)),
                  pl.BlockSpec((1, lat), lambda i: (0, 0)), tab, tab, tab],
        out_specs=(pl.BlockSpec((tr, lat), lambda i: (i, 0)), tab),
        out_shape=(jax.ShapeDtypeStruct((n, lat), BF16), jax.ShapeDtypeStruct((n, LANES), BF16)),
        compiler_params=_params(("parallel",)),
    )(ckr, ckr, g_lat, *tabs)


def _kv_latent_bwd(dckv, ckr, g_lat, dk_cat, tabs, *, tr=512):
    n = ckr.shape[0]
    tr = _tile(n, tr)
    steps = n // tr
    lat = B_KV_LORA
    wk = dk_cat.shape[1]

    def body(d_ref, c_ref, g_ref, dk_ref, tc, tsa, tsb, o_ref, dg_ref, acc_ref):
        i = pl.program_id(0)
        xv = c_ref[...]
        r = lax.rsqrt(jnp.mean(xv * xv, axis=-1, keepdims=True) + NORM_EPS)
        xh = xv * r
        dyv = d_ref[...]
        part = (dyv * xh).reshape(tr // 8, 8, lat).sum(axis=0)

        @pl.when(i == 0)
        def _():
            acc_ref[...] = part

        @pl.when(i > 0)
        def _():
            acc_ref[...] += part

        t = dyv * g_ref[...]
        dx = r * (t - xh * jnp.mean(t * xh, axis=-1, keepdims=True))
        o_ref[:, 0:lat] = dx.astype(o_ref.dtype)
        dkr = dk_ref[:, 0:LANES].astype(F32)
        for h in range(1, wk // LANES):
            dkr = dkr + dk_ref[:, h * LANES:(h + 1) * LANES].astype(F32)
        o_ref[:, lat:lat + LANES] = _rope_apply(dkr, tc[...], tsa[...], tsb[...], -1).astype(o_ref.dtype)

        @pl.when(i == steps - 1)
        def _():
            dg_ref[...] = jnp.sum(acc_ref[...], axis=0, keepdims=True)

    tab = pl.BlockSpec((tr, LANES), lambda i: (i, 0))
    return pl.pallas_call(
        body, name="kv_latent_bwd", grid=(steps,),
        in_specs=[pl.BlockSpec((tr, lat), lambda i: (i, 0)), pl.BlockSpec((tr, lat), lambda i: (i, 0)),
                  pl.BlockSpec((1, lat), lambda i: (0, 0)), pl.BlockSpec((tr, wk), lambda i: (i, 0)),
                  tab, tab, tab],
        out_specs=(pl.BlockSpec((tr, lat + LANES), lambda i: (i, 0)), pl.BlockSpec((1, lat), lambda i: (0, 0))),
        out_shape=(jax.ShapeDtypeStruct((n, lat + LANES), BF16), jax.ShapeDtypeStruct((1, lat), F32)),
        scratch_shapes=[pltpu.VMEM((8, lat), F32)], compiler_params=_params(("arbitrary",)),
    )(dckv, ckr, g_lat, dk_cat, *tabs)


def _sigmoid(z):
    return 1.0 / (1.0 + jnp.exp(-z))


def _lane_place(cols, width):
    rows = cols[0].shape[0]
    lane = lax.broadcasted_iota(jnp.int32, (rows, width), 1)
    out = jnp.zeros((rows, width), F32)
    for h, col in enumerate(cols):
        out = jnp.where(lane == h, col, out)
    return out


def _merge_gate_fwd(outs, lses, proj, z_block, *, tr=512):
    n, w = outs[0].shape
    tr = _tile(n, tr)
    ng = len(outs)

    def body(*refs):
        o_refs = refs[:ng]
        l_refs = refs[ng:2 * ng]
        z_ref = refs[2 * ng]
        y_ref, om_ref, lse_ref = refs[2 * ng + 1:]
        ls = [r[...] for r in l_refs]
        mx = ls[0]
        for l in ls[1:]:
            mx = jnp.maximum(mx, l)
        ssum = jnp.exp2(ls[0] - mx)
        for l in ls[1:]:
            ssum = ssum + jnp.exp2(l - mx)
        tot = mx + jnp.log2(ssum)
        lse_ref[...] = tot
        ws = [jnp.exp2(l - tot) for l in ls]
        for h in range(A_HEADS):
            sl = slice(h * A_HEAD_DIM, (h + 1) * A_HEAD_DIM)
            o = ws[0][:, h:h + 1] * o_refs[0][:, sl]
            for gi in range(1, ng):
                o = o + ws[gi][:, h:h + 1] * o_refs[gi][:, sl]
            z = z_ref[:, sl].astype(F32)
            om_ref[:, sl] = o.astype(BF16)
            y_ref[:, sl] = (o * (z * _sigmoid(z))).astype(BF16)

    row = pl.BlockSpec((tr, w), lambda i: (i, 0))
    lrow = pl.BlockSpec((tr, A_HEADS), lambda i: (i, 0))
    return pl.pallas_call(
        body, name="merge_gate_fwd", grid=(n // tr,),
        in_specs=[row] * ng + [lrow] * ng + [pl.BlockSpec((tr, w), lambda i: (i, z_block))],
        out_specs=(row, row, lrow),
        out_shape=(jax.ShapeDtypeStruct((n, w), BF16), jax.ShapeDtypeStruct((n, w), BF16),
                   jax.ShapeDtypeStruct((n, A_HEADS), F32)),
        compiler_params=_params(("parallel",)),
    )(*outs, *lses, proj)


def _gate_bwd(dy, o, z_arr, z_block, *, name, with_delta, tr=512):
    n, w = dy.shape
    tr = _tile(n, tr)

    def body(*refs):
        dy_ref, o_ref, z_ref, do_ref, dz_ref = refs[:5]
        dyv = dy_ref[...].astype(F32)
        ov = o_ref[...].astype(F32)
        z = z_ref[...].astype(F32)
        sig = _sigmoid(z)
        do = dyv * (z * sig)
        do_ref[...] = do.astype(BF16)
        dz_ref[...] = (dyv * ov * (sig * (1.0 + z * (1.0 - sig)))).astype(BF16)
        if with_delta:
            prod = do * ov
            cols = [jnp.sum(prod[:, h * A_HEAD_DIM:(h + 1) * A_HEAD_DIM], axis=-1, keepdims=True)
                    for h in range(A_HEADS)]
            refs[5][...] = _lane_place(cols, A_HEADS)

    row = pl.BlockSpec((tr, w), lambda i: (i, 0))
    out_specs = [row, row]
    out_shape = [jax.ShapeDtypeStruct((n, w), BF16), jax.ShapeDtypeStruct((n, w), BF16)]
    if with_delta:
        out_specs.append(pl.BlockSpec((tr, A_HEADS), lambda i: (i, 0)))
        out_shape.append(jax.ShapeDtypeStruct((n, A_HEADS), F32))
    return pl.pallas_call(
        body, name=name, grid=(n // tr,),
        in_specs=[row, row, pl.BlockSpec((tr, w), lambda i: (i, z_block))],
        out_specs=tuple(out_specs), out_shape=tuple(out_shape), compiler_params=_params(("parallel",)),
    )(dy, o, z_arr)


def _dot_nt(a, b):
    return lax.dot_general(a, b, (((1,), (1,)), ((), ())), preferred_element_type=F32)


def _dot_nn(a, b):
    return lax.dot_general(a, b, (((1,), (0,)), ((), ())), preferred_element_type=F32)


def _attn_a_fwd(qkv, cb0, qb, out_dtype, *, name):
    bl, dil, ln, _ = qkv.shape
    nb = ln // qb
    hw = A_WIDTH
    heads = range(A_HEADS)
    sls = [slice(h * A_HEAD_DIM, (h + 1) * A_HEAD_DIM) for h in heads]

    def body(*refs):
        if nb > 1:
            q_ref, kc_ref, vc_ref, kp_ref, vp_ref, o_ref, lse_ref = refs
        else:
            q_ref, kc_ref, vc_ref, o_ref, lse_ref = refs
        i = pl.program_id(2)
        qi = lax.broadcasted_iota(jnp.int32, (qb, qb), 0)
        ki = lax.broadcasted_iota(jnp.int32, (qb, qb), 1)
        mask_c = ki <= qi
        mask_p = jnp.logical_and(ki >= qi, i >= 1)
        s_c = [jnp.where(mask_c, _dot_nt(q_ref[:, sls[h]], kc_ref[:, sls[h]]), NEG) for h in heads]
        m = [jnp.max(s_c[h], axis=-1, keepdims=True) for h in heads]
        if nb > 1:
            s_p = [jnp.where(mask_p, _dot_nt(q_ref[:, sls[h]], kp_ref[:, sls[h]]), NEG) for h in heads]
            m = [jnp.maximum(m[h], jnp.max(s_p[h], axis=-1, keepdims=True)) for h in heads]
        p_c = [jnp.exp2(s_c[h] - m[h]) for h in heads]
        l = [jnp.sum(p_c[h], axis=-1, keepdims=True) for h in heads]
        acc = [_dot_nn(p_c[h].astype(BF16), vc_ref[:, sls[h]]) for h in heads]
        if nb > 1:
            p_p = [jnp.exp2(s_p[h] - m[h]) for h in heads]
            l = [l[h] + jnp.sum(p_p[h], axis=-1, keepdims=True) for h in heads]
            acc = [acc[h] + _dot_nn(p_p[h].astype(BF16), vp_ref[:, sls[h]]) for h in heads]
        for h in heads:
            o_ref[:, sls[h]] = (acc[h] / l[h]).astype(o_ref.dtype)
        lse_ref[...] = _lane_place([m[h] + jnp.log2(l[h]) for h in heads], A_HEADS)

    def spec(off, prev):
        if prev:
            return pl.BlockSpec((None, None, qb, hw), lambda b, r, i: (b, r, jnp.maximum(i - 1, 0), cb0 + off))
        return pl.BlockSpec((None, None, qb, hw), lambda b, r, i: (b, r, i, cb0 + off))

    return pl.pallas_call(
        body, name=name, grid=(bl, dil, nb),
        in_specs=[spec(0, False), spec(1, False), spec(2, False)] + ([spec(1, True), spec(2, True)] if nb > 1 else []),
        out_specs=(pl.BlockSpec((None, None, qb, hw), lambda b, r, i: (b, r, i, 0)),
                   pl.BlockSpec((None, None, qb, A_HEADS), lambda b, r, i: (b, r, i, 0))),
        out_shape=(jax.ShapeDtypeStruct((bl, dil, ln, hw), out_dtype),
                   jax.ShapeDtypeStruct((bl, dil, ln, A_HEADS), F32)),
        compiler_params=_params(("parallel", "parallel", "arbitrary")),
    )(*([qkv] * (5 if nb > 1 else 3)))


def _attn_a_bwd(qkv, cb0, do, lse, delta, lse_t, delta_t, tabs, qb, *, name):
    bl, dil, ln, _ = qkv.shape
    nb = ln // qb
    hw = A_WIDTH

    def body(*refs):
        if nb > 1:
            (q_ref, kc_ref, vc_ref, do_ref, lse_ref, dl_ref, lt_ref, dt_ref, tc, tsa, tsb,
             qn_ref, kp_ref, vp_ref, don_ref, ltn_ref, dtn_ref, o_ref) = refs
        else:
            q_ref, kc_ref, vc_ref, do_ref, lse_ref, dl_ref, lt_ref, dt_ref, tc, tsa, tsb, o_ref = refs
        i = pl.program_id(2)
        row = lax.broadcasted_iota(jnp.int32, (qb, qb), 0)
        col = lax.broadcasted_iota(jnp.int32, (qb, qb), 1)
        m_qc = col <= row
        m_kc = row <= col
        m_qp = jnp.logical_and(col >= row, i >= 1)
        m_kn = jnp.logical_and(row >= col, i + 1 < nb)
        c, sa, sb = tc[...], tsa[...], tsb[...]
        heads = range(A_HEADS)
        sls = [slice(h * A_HEAD_DIM, (h + 1) * A_HEAD_DIM) for h in heads]
        q, kc = [q_ref[:, sl] for sl in sls], [kc_ref[:, sl] for sl in sls]
        vc, dov = [vc_ref[:, sl] for sl in sls], [do_ref[:, sl] for sl in sls]
        lse_c = [lse_ref[:, h:h + 1] for h in heads]
        dl_c = [dl_ref[:, h:h + 1] for h in heads]
        s = [_dot_nt(q[h], kc[h]) for h in heads]
        st = [_dot_nt(kc[h], q[h]) for h in heads]
        dp = [_dot_nt(dov[h], vc[h]) for h in heads]
        dpt = [_dot_nt(vc[h], dov[h]) for h in heads]
        p = [jnp.exp2(jnp.where(m_qc, s[h], NEG) - lse_c[h]) for h in heads]
        pt = [jnp.exp2(jnp.where(m_kc, st[h], NEG) - lt_ref[h:h + 1, :]) for h in heads]
        dq = [_dot_nn((p[h] * (dp[h] - dl_c[h])).astype(BF16), kc[h]) for h in heads]
        dk = [_dot_nn((pt[h] * (dpt[h] - dt_ref[h:h + 1, :])).astype(BF16), q[h]) for h in heads]
        dv = [_dot_nn(pt[h].astype(BF16), dov[h]) for h in heads]
        if nb > 1:
            kp, vp = [kp_ref[:, sl] for sl in sls], [vp_ref[:, sl] for sl in sls]
            qn, don = [qn_ref[:, sl] for sl in sls], [don_ref[:, sl] for sl in sls]
            s = [_dot_nt(q[h], kp[h]) for h in heads]
            st = [_dot_nt(kc[h], qn[h]) for h in heads]
            dp = [_dot_nt(dov[h], vp[h]) for h in heads]
            dpt = [_dot_nt(vc[h], don[h]) for h in heads]
            p = [jnp.exp2(jnp.where(m_qp, s[h], NEG) - lse_c[h]) for h in heads]
            pt = [jnp.exp2(jnp.where(m_kn, st[h], NEG) - ltn_ref[h:h + 1, :]) for h in heads]
            dq = [dq[h] + _dot_nn((p[h] * (dp[h] - dl_c[h])).astype(BF16), kp[h]) for h in heads]
            dk = [dk[h] + _dot_nn((pt[h] * (dpt[h] - dtn_ref[h:h + 1, :])).astype(BF16), qn[h]) for h in heads]
            dv = [dv[h] + _dot_nn(pt[h].astype(BF16), don[h]) for h in heads]
        for h in heads:
            o_ref[:, h * A_HEAD_DIM:(h + 1) * A_HEAD_DIM] = _rope_apply(dq[h] * A_SCALE, c, sa, sb, -1).astype(BF16)
            o_ref[:, hw + h * A_HEAD_DIM:hw + (h + 1) * A_HEAD_DIM] = _rope_apply(dk[h] * LN2, c, sa, sb, -1).astype(BF16)
            o_ref[:, 2 * hw + h * A_HEAD_DIM:2 * hw + (h + 1) * A_HEAD_DIM] = dv[h].astype(BF16)

    def cur(w, col):
        return pl.BlockSpec((None, None, qb, w), lambda b, r, i: (b, r, i, col))

    def prev(w, col):
        return pl.BlockSpec((None, None, qb, w), lambda b, r, i: (b, r, jnp.maximum(i - 1, 0), col))

    def nxt(w, col):
        return pl.BlockSpec((None, None, qb, w), lambda b, r, i: (b, r, jnp.minimum(i + 1, nb - 1), col))

    t_cur = pl.BlockSpec((None, None, A_HEADS, qb), lambda b, r, i: (b, r, 0, i))
    t_nxt = pl.BlockSpec((None, None, A_HEADS, qb), lambda b, r, i: (b, r, 0, jnp.minimum(i + 1, nb - 1)))
    in_specs = [cur(hw, cb0), cur(hw, cb0 + 1), cur(hw, cb0 + 2), cur(hw, 0), cur(A_HEADS, 0), cur(A_HEADS, 0),
                t_cur, t_cur, cur(LANES, 0), cur(LANES, 0), cur(LANES, 0)]
    operands = [qkv, qkv, qkv, do, lse, delta, lse_t, delta_t, *tabs]
    if nb > 1:
        in_specs += [nxt(hw, cb0), prev(hw, cb0 + 1), prev(hw, cb0 + 2), nxt(hw, 0), t_nxt, t_nxt]
        operands += [qkv, qkv, qkv, do, lse_t, delta_t]
    return pl.pallas_call(
        body, name=name, grid=(bl, dil, nb), in_specs=in_specs, out_specs=cur(3 * hw, 0),
        out_shape=jax.ShapeDtypeStruct((bl, dil, ln, 3 * hw), BF16),
        compiler_params=_params(("parallel", "parallel", "arbitrary")),
    )(*operands)


def _head_terms(do, o, lse, e):
    rows = do.shape[0]
    lane = lax.broadcasted_iota(jnp.int32, (rows, LANES), 1)
    mine = (lane < B_VDIM) if e == 0 else (lane >= B_VDIM)
    prod = do.astype(F32) * o.astype(F32)
    dl = jnp.sum(jnp.where(mine, prod, 0.0), axis=-1, keepdims=True)
    do_e = jnp.where(mine, do, jnp.zeros_like(do))
    return do_e, dl, lse[:, e * B_VDIM:e * B_VDIM + 1]


def _col_to_row(col, rows):
    return jnp.transpose(jnp.broadcast_to(col, (rows, LANES)))[0:1, :]


def _mla_fwd(q_cat, kvup, kr, z, tq):
    bl, t, _ = q_cat.shape
    nq = t // tq
    pairs = B_HEADS // 2
    v_blk0 = (B_HEADS * LANES) // LANES

    def body(q_ref, k_ref, v_ref, kr_ref, z_ref, y_ref, o_ref, lse_ref, lrow_ref, m_ref, acc_ref):
        qi = pl.program_id(2)
        qs = [q_ref[:, e * LANES:(e + 1) * LANES] for e in range(2)]
        row = lax.broadcasted_iota(jnp.int32, (tq, tq), 0)
        col = lax.broadcasted_iota(jnp.int32, (tq, tq), 1)
        tri = col <= row
        sum_lane = [B_VDIM, 0]

        for e in range(2):
            m_ref[e] = jnp.full((tq, LANES), NEG, F32)
            acc_ref[e] = jnp.zeros((tq, LANES), F32)

        def tile(k0, w, masked):
            lane = lax.broadcasted_iota(jnp.int32, (w, LANES), 1)
            first = lane < B_VDIM
            krv = kr_ref[pl.ds(k0, w), :]
            v = v_ref[pl.ds(k0, w), :]
            vs = [jnp.where(first, v, jnp.where(lane == B_VDIM, 1.0, 0.0).astype(BF16)),
                  jnp.where(first, jnp.where(lane == 0, 1.0, 0.0).astype(BF16), v)]
            ss = []
            for e in range(2):
                k = k_ref[pl.ds(k0, w), e * LANES:(e + 1) * LANES] + krv
                s = _dot_nt(qs[e], k)
                if masked:
                    r = lax.broadcasted_iota(jnp.int32, (tq, w), 0)
                    c = lax.broadcasted_iota(jnp.int32, (tq, w), 1)
                    s = jnp.where(c <= r + (w - tq), s, NEG)
                ss.append(s)
            for e in range(2):
                m_old = m_ref[e]
                m_new = jnp.maximum(m_old, jnp.max(ss[e], axis=-1, keepdims=True))
                p = jnp.exp2(ss[e] - jnp.concatenate([m_new] * (w // LANES), axis=1)).astype(BF16)
                m_ref[e] = m_new
                acc_ref[e] = jnp.exp2(m_old - m_new) * acc_ref[e] + _dot_nn(p, vs[e])

        def step(kb2, carry):
            tile(pl.multiple_of(kb2 * 2 * tq, 2 * tq), 2 * tq, False)
            return carry

        lax.fori_loop(0, qi // 2, step, 0)

        @pl.when(qi % 2 == 1)
        def _():
            tile(pl.multiple_of((qi - 1) * tq, tq), 2 * tq, True)

        @pl.when(qi % 2 == 0)
        def _():
            tile(pl.multiple_of(qi * tq, tq), tq, True)
        lane = lax.broadcasted_iota(jnp.int32, (tq, LANES), 1)
        first = lane < B_VDIM
        accs = [acc_ref[e] for e in range(2)]
        ls = [accs[e][:, sum_lane[e]:sum_lane[e] + 1] for e in range(2)]
        outs = [accs[e] / ls[e] for e in range(2)]
        lses = [m_ref[e] + jnp.log2(ls[e]) for e in range(2)]
        o = jnp.where(first, outs[0], outs[1])
        zv = z_ref[...].astype(F32)
        o_ref[...] = o.astype(BF16)
        y_ref[...] = (o * (zv * _sigmoid(zv))).astype(BF16)
        lse_ref[...] = jnp.where(first, lses[0], lses[1])
        for e in range(2):
            lrow_ref[e:e + 1, :] = jnp.transpose(lses[e])[0:1, :]

    blk = pl.BlockSpec((None, tq, LANES), lambda b, j, i: (b, i, j))
    return pl.pallas_call(
        body, name="mla_fwd", grid=(bl, pairs, nq),
        in_specs=[pl.BlockSpec((None, tq, 2 * LANES), lambda b, j, i: (b, i, j)),
                  pl.BlockSpec((None, t, 2 * LANES), lambda b, j, i: (b, 0, j)),
                  pl.BlockSpec((None, t, LANES), lambda b, j, i: (b, 0, v_blk0 + j)),
                  pl.BlockSpec((None, t, LANES), lambda b, j, i: (b, 0, 0)),
                  blk],
        out_specs=(blk, blk, blk, pl.BlockSpec((None, None, None, 2, tq), lambda b, j, i: (b, j, i, 0, 0))),
        out_shape=(jax.ShapeDtypeStruct((bl, t, B_WIDTH), BF16), jax.ShapeDtypeStruct((bl, t, B_WIDTH), BF16),
                   jax.ShapeDtypeStruct((bl, t, B_WIDTH), F32),
                   jax.ShapeDtypeStruct((bl, pairs, nq, 2, tq), F32)),
        scratch_shapes=[pltpu.VMEM((2, tq, LANES), F32), pltpu.VMEM((2, tq, LANES), F32)],
        compiler_params=_params(("parallel", "parallel", "arbitrary")),
    )(q_cat, kvup, kvup, kr, z)


def _mla_dq(q_cat, kvup, kr, do, o, lse, tabs, tq):
    bl, t, _ = q_cat.shape
    nq = t // tq
    pairs = B_HEADS // 2
    v_blk0 = (B_HEADS * LANES) // LANES

    def body(q_ref, k_ref, v_ref, kr_ref, do_ref, o_ref, lse_ref, tc, tsa, tsb, dq_ref, drow_ref, acc_ref):
        qi = pl.program_id(2)
        dov, ov, lsev = do_ref[...], o_ref[...], lse_ref[...]
        qs = [q_ref[:, e * LANES:(e + 1) * LANES] for e in range(2)]
        terms = [_head_terms(dov, ov, lsev, e) for e in range(2)]
        row = lax.broadcasted_iota(jnp.int32, (tq, tq), 0)
        col = lax.broadcasted_iota(jnp.int32, (tq, tq), 1)
        tri = col <= row
        for e in range(2):
            acc_ref[e] = jnp.zeros((tq, LANES), F32)

        def tile(k0, w, masked):
            krv = kr_ref[pl.ds(k0, w), :]
            v = v_ref[pl.ds(k0, w), :]
            ks = [k_ref[pl.ds(k0, w), e * LANES:(e + 1) * LANES] + krv for e in range(2)]
            ss = [_dot_nt(qs[e], ks[e]) for e in range(2)]
            dps = [_dot_nt(terms[e][0], v) for e in range(2)]
            for e in range(2):
                s = ss[e]
                if masked:
                    r = lax.broadcasted_iota(jnp.int32, (tq, w), 0)
                    c = lax.broadcasted_iota(jnp.int32, (tq, w), 1)
                    s = jnp.where(c <= r + (w - tq), s, NEG)
                p = jnp.exp2(s - terms[e][2])
                ds = (p * (dps[e] - terms[e][1])).astype(BF16)
                acc_ref[e] += _dot_nn(ds, ks[e])

        def step(kb2, carry):
            tile(pl.multiple_of(kb2 * 2 * tq, 2 * tq), 2 * tq, False)
            return carry

        lax.fori_loop(0, qi // 2, step, 0)

        @pl.when(qi % 2 == 1)
        def _():
            tile(pl.multiple_of((qi - 1) * tq, tq), 2 * tq, True)

        @pl.when(qi % 2 == 0)
        def _():
            tile(pl.multiple_of(qi * tq, tq), tq, True)

        for e in range(2):
            dq_ref[:, e * LANES:(e + 1) * LANES] = _rope_apply(acc_ref[e] * B_SCALE, tc[...], tsa[...], tsb[...], -1).astype(BF16)
            drow_ref[e:e + 1, :] = _col_to_row(terms[e][1], tq)

    blk = pl.BlockSpec((None, tq, LANES), lambda b, j, i: (b, i, j))
    tab = pl.BlockSpec((None, tq, LANES), lambda b, j, i: (b, i, 0))
    qblk = pl.BlockSpec((None, tq, 2 * LANES), lambda b, j, i: (b, i, j))
    return pl.pallas_call(
        body, name="mla_dq", grid=(bl, pairs, nq),
        in_specs=[qblk,
                  pl.BlockSpec((None, t, 2 * LANES), lambda b, j, i: (b, 0, j)),
                  pl.BlockSpec((None, t, LANES), lambda b, j, i: (b, 0, v_blk0 + j)),
                  pl.BlockSpec((None, t, LANES), lambda b, j, i: (b, 0, 0)),
                  blk, blk, blk, tab, tab, tab],
        out_specs=(qblk, pl.BlockSpec((None, None, None, 2, tq), lambda b, j, i: (b, j, i, 0, 0))),
        out_shape=(jax.ShapeDtypeStruct((bl, t, B_HEADS * LANES), BF16),
                   jax.ShapeDtypeStruct((bl, pairs, nq, 2, tq), F32)),
        scratch_shapes=[pltpu.VMEM((2, tq, LANES), F32)],
        compiler_params=_params(("parallel", "parallel", "arbitrary")),
    )(q_cat, kvup, kvup, kr, do, o, lse, *tabs)


def _mla_dkv(q_cat, kvup, kr, do, lse_rows, delta_rows, tq):
    bl, t, _ = q_cat.shape
    nq = t // tq
    pairs = B_HEADS // 2
    v_blk0 = (B_HEADS * LANES) // LANES

    def body(q_ref, k_ref, v_ref, kr_ref, do_ref, lrow_ref, drow_ref, dk_ref, dv_ref, acc_ref):
        kb = pl.program_id(2)
        v = v_ref[...]
        krv = kr_ref[...]
        ks = [k_ref[:, e * LANES:(e + 1) * LANES] + krv for e in range(2)]
        krow = lax.broadcasted_iota(jnp.int32, (tq, tq), 0)
        qcol = lax.broadcasted_iota(jnp.int32, (tq, tq), 1)
        tri = krow <= qcol
        lane = lax.broadcasted_iota(jnp.int32, (tq, LANES), 1)
        mine = [lane < B_VDIM, lane >= B_VDIM]

        for e in range(3):
            acc_ref[e] = jnp.zeros((tq, LANES), F32)

        def tile(qb, nblk, masked):
            w = nblk * tq
            rows = pl.ds(pl.multiple_of(qb * tq, tq), w)
            dov = do_ref[rows, :]
            lane_w = lax.broadcasted_iota(jnp.int32, (w, LANES), 1)
            mine_w = [lane_w < B_VDIM, lane_w >= B_VDIM]
            qs = [q_ref[rows, e * LANES:(e + 1) * LANES] for e in range(2)]
            does = [jnp.where(mine_w[e], dov, jnp.zeros_like(dov)) for e in range(2)]
            sts = [_dot_nt(ks[e], qs[e]) for e in range(2)]
            dpts = [_dot_nt(v, does[e]) for e in range(2)]

            def rows_of(ref, e):
                return jnp.concatenate([ref[qb + i, e:e + 1, :] for i in range(nblk)], axis=1)

            pts = []
            for e in range(2):
                st = sts[e]
                if masked:
                    r = lax.broadcasted_iota(jnp.int32, (tq, w), 0)
                    c = lax.broadcasted_iota(jnp.int32, (tq, w), 1)
                    st = jnp.where(r <= c, st, NEG)
                pts.append(jnp.exp2(st - rows_of(lrow_ref, e)))
            acc_ref[2] += _dot_nn(pts[0].astype(BF16), does[0]) + _dot_nn(pts[1].astype(BF16), does[1])
            for e in range(2):
                dst = (pts[e] * (dpts[e] - rows_of(drow_ref, e))).astype(BF16)
                acc_ref[e] += _dot_nn(dst, qs[e])

        rest = nq - 1 - kb
        odd = rest % 2

        @pl.when(odd == 1)
        def _():
            tile(kb, 2, True)

        @pl.when(odd == 0)
        def _():
            tile(kb, 1, True)

        def step(i, carry):
            tile(kb + 1 + odd + 2 * i, 2, False)
            return carry

        lax.fori_loop(0, rest // 2, step, 0)
        dk_ref[:, 0:LANES] = (acc_ref[0] * LN2).astype(BF16)
        dk_ref[:, LANES:2 * LANES] = (acc_ref[1] * LN2).astype(BF16)
        dv_ref[...] = acc_ref[2].astype(BF16)

    full = pl.BlockSpec((None, t, LANES), lambda b, j, i: (b, 0, j))
    rows = pl.BlockSpec((None, None, nq, 2, tq), lambda b, j, i: (b, j, 0, 0, 0))
    kblk = pl.BlockSpec((None, tq, 2 * LANES), lambda b, j, i: (b, i, j))
    return pl.pallas_call(
        body, name="mla_dkv", grid=(bl, pairs, nq),
        in_specs=[pl.BlockSpec((None, t, 2 * LANES), lambda b, j, i: (b, 0, j)),
                  kblk,
                  pl.BlockSpec((None, tq, LANES), lambda b, j, i: (b, i, v_blk0 + j)),
                  pl.BlockSpec((None, tq, LANES), lambda b, j, i: (b, i, 0)),
                  full, rows, rows],
        out_specs=(kblk, pl.BlockSpec((None, tq, LANES), lambda b, j, i: (b, i, j))),
        out_shape=(jax.ShapeDtypeStruct((bl, t, B_HEADS * LANES), BF16),
                   jax.ShapeDtypeStruct((bl, t, B_WIDTH), BF16)),
        scratch_shapes=[pltpu.VMEM((3, tq, LANES), F32)],
        compiler_params=_params(("parallel", "parallel", "arbitrary")),
    )(q_cat, kvup, kvup, kr, do, lse_rows, delta_rows)


def _adamw(w, g, m, v, *, name):
    r, c = w.shape
    tr = _row_tile(r, 256)
    c1 = 1.0 - ADAM_B1
    c2 = 1.0 - ADAM_B2
    bc1 = 1.0 - ADAM_B1 ** ADAM_STEP
    bc2 = 1.0 - ADAM_B2 ** ADAM_STEP

    def body(w_ref, g_ref, m_ref, v_ref, d_ref, nm_ref, nv_ref):
        gv = g_ref[...]
        nm = ADAM_B1 * m_ref[...] + c1 * gv
        nv = ADAM_B2 * v_ref[...] + c2 * (gv * gv)
        nm_ref[...] = nm
        nv_ref[...] = nv
        d_ref[...] = -ADAM_LR * ((nm / bc1) / (jnp.sqrt(nv / bc2) + ADAM_EPS) + ADAM_WD * w_ref[...])

    blk = pl.BlockSpec((tr, c), lambda i: (i, 0))
    sds = jax.ShapeDtypeStruct((r, c), F32)
    return pl.pallas_call(
        body, name=name, grid=(r // tr,), in_specs=[blk] * 4, out_specs=(blk,) * 3,
        out_shape=(sds,) * 3, compiler_params=_params(("parallel",)),
    )(w, g, m, v)


def _add_my_half(stacked, other, core, out_dtype, *, name):
    nch, a, c = stacked.shape
    h = a // 2
    tr = _row_tile(h, 256)
    nblk = h // tr

    def body(core_ref, s_ref, p_ref, o_ref):
        o_ref[...] = (s_ref[...] + p_ref[...]).astype(o_ref.dtype)

    return pl.pallas_call(
        body, name=name,
        grid_spec=pltpu.PrefetchScalarGridSpec(
            num_scalar_prefetch=1, grid=(nch, nblk),
            in_specs=[pl.BlockSpec((None, tr, c), lambda k, i, cr: (k, cr[0] * nblk + i, 0)),
                      pl.BlockSpec((None, tr, c), lambda k, i, cr: (k, i, 0))],
            out_specs=pl.BlockSpec((None, tr, c), lambda k, i, cr: (k, i, 0))),
        out_shape=jax.ShapeDtypeStruct((nch, h, c), out_dtype),
        compiler_params=_params(("parallel", "parallel")),
    )(core, stacked, other)


def _sum_chips(parts, own, chip, *, name):
    nch, h, c = parts.shape
    tr = _row_tile(h, 256)

    def body(chip_ref, p_ref, own_ref, o_ref):
        me = chip_ref[0]

        def slot(k):
            return jnp.where(me == k, own_ref[k].astype(F32), p_ref[k].astype(F32))

        acc = slot(0) + slot(1)
        for k in range(2, nch):
            acc = acc + slot(k)
        o_ref[...] = acc

    blk = pl.BlockSpec((nch, tr, c), lambda i, cr: (0, i, 0))
    return pl.pallas_call(
        body, name=name,
        grid_spec=pltpu.PrefetchScalarGridSpec(
            num_scalar_prefetch=1, grid=(h // tr,), in_specs=[blk, blk],
            out_specs=pl.BlockSpec((tr, c), lambda i, cr: (i, 0))),
        out_shape=jax.ShapeDtypeStruct((h, c), F32), compiler_params=_params(("parallel",)),
    )(chip, parts, own)


def _join_halves(mine, other, core, *, name):
    h, c = mine.shape
    tr = _row_tile(h, 256)
    nblk = h // tr

    def body(core_ref, m_ref, s_ref, o_ref):
        is_mine = pl.program_id(0) // nblk == core_ref[0]

        @pl.when(is_mine)
        def _():
            o_ref[...] = m_ref[...]

        @pl.when(jnp.logical_not(is_mine))
        def _():
            o_ref[...] = s_ref[...]

    blk = pl.BlockSpec((tr, c), lambda i, cr: (i % nblk, 0))
    return pl.pallas_call(
        body, name=name,
        grid_spec=pltpu.PrefetchScalarGridSpec(
            num_scalar_prefetch=1, grid=(2 * nblk,), in_specs=[blk, blk],
            out_specs=pl.BlockSpec((tr, c), lambda i, cr: (i, 0))),
        out_shape=jax.ShapeDtypeStruct((2 * h, c), F32), compiler_params=_params(("arbitrary",)),
    )(core, mine, other)


def _place():
    x, y, c = lax.axis_index("x"), lax.axis_index("y"), lax.axis_index("c")
    chips = [(1 - x, y), (x, 1 - y), (1 - x, 1 - y)]
    return x, y, c, chips


def _remote(src, dst, send_sems, recv_sems, k, to):
    return pltpu.make_async_remote_copy(src_ref=src, dst_ref=dst, send_sem=send_sems.at[k],
                                        recv_sem=recv_sems.at[k], device_id=to, device_id_type=MESH)


def _hbm_call(body, name, ins, out_shapes, n_remote):
    any_spec = pl.BlockSpec(memory_space=pl.ANY)
    return pl.pallas_call(
        body, name=name, in_specs=[any_spec] * len(ins), out_specs=tuple([any_spec] * len(out_shapes)),
        out_shape=tuple(out_shapes),
        scratch_shapes=[pltpu.SemaphoreType.DMA((n_remote,)), pltpu.SemaphoreType.DMA((n_remote,))],
    )(*ins)


def _all_gather_chips(shards, *, name):
    n = len(shards)

    def body(*refs):
        ins, outs = refs[:n], refs[n:2 * n]
        send_sems, recv_sems = refs[2 * n:]
        x, y, c, chips = _place()
        me = 2 * x + y
        sent = []
        for s in range(n):
            h = ins[s].shape[0] // 2
            for j, (px, py) in enumerate(chips):
                cp = _remote(ins[s].at[pl.ds(c * h, h)], outs[s].at[me, pl.ds(c * h, h)],
                             send_sems, recv_sems, s * 6 + j, (px, py, c))
                cp.start()
                sent.append(cp)
        for s in range(n):
            h = ins[s].shape[0] // 2
            for j, (px, py) in enumerate(chips):
                slab = outs[s].at[2 * px + py, pl.ds(c * h, h)]
                _remote(slab, slab, send_sems, recv_sems, s * 6 + j, (px, py, c)).wait_recv()
                cp = _remote(slab, slab, send_sems, recv_sems, s * 6 + 3 + j, (x, y, 1 - c))
                cp.start()
                sent.append(cp)
        for s in range(n):
            h = ins[s].shape[0] // 2
            for j, (px, py) in enumerate(chips):
                slab = outs[s].at[2 * px + py, pl.ds((1 - c) * h, h)]
                _remote(slab, slab, send_sems, recv_sems, s * 6 + 3 + j, (x, y, 1 - c)).wait_recv()
        for cp in sent:
            cp.wait_send()

    out_shapes = [jax.ShapeDtypeStruct((N_CHIPS,) + s.shape, s.dtype) for s in shards]
    return _hbm_call(body, name, shards, out_shapes, 6 * n)


def _pair_send_other_half(stacked, *, name):
    n = len(stacked)

    def body(*refs):
        ins, outs = refs[:n], refs[n:2 * n]
        send_sems, recv_sems = refs[2 * n:]
        x, y, c, _chips = _place()
        sent = []
        for s in range(n):
            h = ins[s].shape[1] // 2
            cp = _remote(ins[s].at[:, pl.ds((1 - c) * h, h)], outs[s], send_sems, recv_sems, s, (x, y, 1 - c))
            cp.start()
            sent.append(cp)
        for cp in sent:
            cp.wait_recv()
        for cp in sent:
            cp.wait_send()

    out_shapes = [jax.ShapeDtypeStruct((s.shape[0], s.shape[1] // 2, s.shape[2]), s.dtype) for s in stacked]
    return _hbm_call(body, name, stacked, out_shapes, n)


def _chip_exchange(halves, *, name):
    n = len(halves)

    def body(*refs):
        ins, outs = refs[:n], refs[n:2 * n]
        send_sems, recv_sems = refs[2 * n:]
        x, y, c, chips = _place()
        me = 2 * x + y
        sent = []
        for s in range(n):
            for j, (px, py) in enumerate(chips):
                cp = _remote(ins[s].at[2 * px + py], outs[s].at[me], send_sems, recv_sems, s * 3 + j, (px, py, c))
                cp.start()
                sent.append(cp)
        for s in range(n):
            for j, (px, py) in enumerate(chips):
                slab = outs[s].at[2 * px + py]
                _remote(slab, slab, send_sems, recv_sems, s * 3 + j, (px, py, c)).wait_recv()
        for cp in sent:
            cp.wait_send()

    out_shapes = [jax.ShapeDtypeStruct(s.shape, s.dtype) for s in halves]
    return _hbm_call(body, name, halves, out_shapes, 3 * n)


def _chip_exchange_start(halves, *, name):
    n = len(halves)
    hbm = pl.BlockSpec(memory_space=pltpu.HBM)
    sem = pl.BlockSpec(memory_space=pltpu.SEMAPHORE)

    def body(*refs):
        ins, lands = refs[:n], refs[n:2 * n]
        send_sems, recv_sems = refs[2 * n], refs[2 * n + 1]
        token = refs[-1]
        x, y, c, chips = _place()
        me = 2 * x + y
        for s in range(n):
            for j, (px, py) in enumerate(chips):
                _remote(ins[s].at[2 * px + py], lands[s].at[me], send_sems, recv_sems, s * 3 + j, (px, py, c)).start()
        token[...] = jnp.zeros_like(token)

    slabs = [pltpu.HBM(s.shape, s.dtype) for s in halves]
    outs = pl.pallas_call(
        body, name=name,
        out_shape=(pltpu.SemaphoreType.DMA((3 * n,)), pltpu.SemaphoreType.DMA((3 * n,)), *slabs, *slabs,
                   jax.ShapeDtypeStruct((8, LANES), F32)),
        in_specs=[hbm] * (2 * n), out_specs=(sem, sem, *([hbm] * (2 * n)), pl.BlockSpec(memory_space=pltpu.VMEM)),
        input_output_aliases={i: 2 + i for i in range(2 * n)},
        compiler_params=pltpu.CompilerParams(has_side_effects=pltpu.SideEffectType.DATAFLOW_SIDE_EFFECTING),
    )(*[pltpu.with_memory_space_constraint(s, pltpu.HBM) for s in halves],
      *[pltpu.with_memory_space_constraint(lax.empty(s.shape, s.dtype), pltpu.HBM) for s in halves])
    return outs[0], outs[1], list(outs[2:2 + n]), list(outs[2 + n:2 + 2 * n]), outs[-1]


def _chip_exchange_wait(send_sems, recv_sems, sent, lands, after, *, name):
    n = len(sent)
    hbm = pl.BlockSpec(memory_space=pltpu.HBM)
    sem = pl.BlockSpec(memory_space=pltpu.SEMAPHORE)

    def body(*refs):
        ins, lands_in = refs[:n], refs[n:2 * n]
        send_sems, recv_sems = refs[2 * n], refs[2 * n + 1]
        x, y, c, chips = _place()
        me = 2 * x + y
        for s in range(n):
            for j, (px, py) in enumerate(chips):
                k = 2 * px + py
                _remote(ins[s].at[k], lands_in[s].at[me], send_sems, recv_sems, s * 3 + j, (px, py, c)).wait_send()
                _remote(ins[s].at[k], lands_in[s].at[k], send_sems, recv_sems, s * 3 + j, (px, py, c)).wait_recv()

    slabs = [pltpu.HBM(s.shape, s.dtype) for s in sent]
    outs = pl.pallas_call(
        body, name=name, out_shape=(*slabs, *slabs),
        in_specs=[hbm] * (2 * n) + [sem, sem, pl.BlockSpec(memory_space=pl.ANY)],
        out_specs=tuple([hbm] * (2 * n)), input_output_aliases={i: i for i in range(2 * n)},
        compiler_params=pltpu.CompilerParams(has_side_effects=pltpu.SideEffectType.DATAFLOW_SIDE_EFFECTING),
    )(*sent, *lands, send_sems, recv_sems, after)
    return list(outs[n:])


def _pair_swap(halves, *, name):
    n = len(halves)

    def body(*refs):
        ins, outs = refs[:n], refs[n:2 * n]
        send_sems, recv_sems = refs[2 * n:]
        x, y, c, _chips = _place()
        sent = []
        for s in range(n):
            cp = _remote(ins[s], outs[s], send_sems, recv_sems, s, (x, y, 1 - c))
            cp.start()
            sent.append(cp)
        for cp in sent:
            cp.wait_recv()
        for cp in sent:
            cp.wait_send()

    out_shapes = [jax.ShapeDtypeStruct(s.shape, s.dtype) for s in halves]
    return _hbm_call(body, name, halves, out_shapes, n)


def _pack_rows(parts, row_multiple):
    flat = jnp.concatenate([p.reshape(-1) for p in parts])
    quantum = row_multiple * PACK_COLS
    pad = (-flat.shape[0]) % quantum
    flat = jnp.pad(flat, (0, pad))
    return flat.reshape(-1, PACK_COLS)


def _unpack(flat, shapes):
    out, pos = [], 0
    for shp in shapes:
        size = math.prod(shp)
        out.append(flat[pos:pos + size].reshape(shp))
        pos += size
    return out


def _to_chunks_cols(full):
    r, c4 = full.shape
    return full.reshape(r, N_CHIPS, c4 // N_CHIPS).transpose(1, 0, 2)


def _from_chunks_cols(stacked):
    nch, r, c = stacked.shape
    return stacked.transpose(1, 0, 2).reshape(r, nch * c)


def _class_major(a, bl, t, dil):
    w = a.shape[-1]
    if dil == 1:
        return a.reshape(bl, 1, t, w)
    return a.reshape(bl, t // dil, dil, w).transpose(0, 2, 1, 3)


def _natural(a):
    bl, dil, ln, w = a.shape
    if dil == 1:
        return a.reshape(bl * ln, w)
    return a.transpose(0, 2, 1, 3).reshape(bl * ln * dil, w)


def _train_step(x, positions, a_pre_norm, a_w_in, a_w_out, a_post_norm, kv_norm, kv_w_down, kv_latent_norm,
                kv_w_up, b_pre_norm, b_w_in, b_q_norm, b_w_q_up, b_w_out, b_post_norm, loss_target, moments):
    bl, t, d = x.shape
    n = bl * t
    qb = t // A_DILATIONS[-1]
    tq = _tile(t, 256)
    dq4 = d // N_CHIPS
    chip = 2 * lax.axis_index("x") + lax.axis_index("y")
    chip_arr = chip.astype(jnp.int32).reshape(1)
    core_arr = lax.axis_index("c").astype(jnp.int32).reshape(1)

    w_in_a_s = a_w_in[0].astype(BF16)
    outs_s = jnp.concatenate([a_w_out[0], b_w_out[0]], axis=0).astype(BF16)
    small_shapes = [kv_w_down.shape, kv_w_up.shape, b_w_in[0].shape, b_w_q_up[0].shape]
    small_s = _pack_rows([kv_w_down, kv_w_up, b_w_in[0], b_w_q_up[0]], 32).astype(BF16)
    gains_s = jnp.pad(jnp.concatenate([a_pre_norm[0], a_post_norm[0]]), (0, 16 * LANES - 2 * dq4)).reshape(16, LANES)
    shards = [w_in_a_s, outs_s, small_s, gains_s]
    gathered = _all_gather_chips(shards, name="gather_weights")
    g_in_a, g_outs, g_small, g_gains = [lax.dynamic_update_index_in_dim(g, s, chip, 0)
                                        for g, s in zip(gathered, shards)]

    w_in_a = _from_chunks_cols(g_in_a)
    w_out_a = g_outs[:, :A_WIDTH // N_CHIPS].reshape(A_WIDTH, d)
    w_out_b = g_outs[:, A_WIDTH // N_CHIPS:].reshape(B_WIDTH, d)
    sm = [_unpack(g_small[k].reshape(-1), small_shapes) for k in range(N_CHIPS)]
    w_down = jnp.concatenate([sm[k][0] for k in range(N_CHIPS)], axis=0)
    w_up = jnp.concatenate([sm[k][1] for k in range(N_CHIPS)], axis=1)
    w_in_b = jnp.concatenate([sm[k][2] for k in range(N_CHIPS)], axis=1)
    w_q_up = jnp.concatenate([sm[k][3] for k in range(N_CHIPS)], axis=1)
    gflat = g_gains.reshape(N_CHIPS, -1)
    g_a_pre = gflat[:, :dq4].reshape(1, d)
    g_a_post = gflat[:, dq4:2 * dq4].reshape(1, d)

    w_up_h = w_up.reshape(B_KV_LORA, B_HEADS, B_NOPE + B_VDIM)
    w_up_k = jnp.pad(w_up_h[:, :, :B_NOPE], ((0, 0), (0, 0), (0, LANES - B_NOPE))).reshape(B_KV_LORA, B_HEADS * LANES)
    w_up_v = w_up_h[:, :, B_NOPE:].reshape(B_KV_LORA, B_WIDTH)
    w_up_cat = jnp.concatenate([w_up_k, w_up_v], axis=1)
    w_q_up_p = jnp.pad(w_q_up.reshape(B_Q_LORA, B_HEADS, B_QK_DIM),
                       ((0, 0), (0, 0), (0, LANES - B_QK_DIM))).reshape(B_Q_LORA, B_HEADS * LANES)
    zeros_d = lambda c: jnp.zeros((d, c), BF16)
    w_down_p = jnp.concatenate([w_down[:, :B_KV_LORA], zeros_d(B_NOPE), w_down[:, B_KV_LORA:],
                                zeros_d(LANES - B_NOPE - B_ROPE)], axis=1)
    w_cq = w_in_b[:, :B_Q_LORA]
    w_z = w_in_b[:, B_Q_LORA:]

    tabs_a = _rope_tables(positions, A_ROPE_THETA, 0)
    tabs_b = _rope_tables(positions, B_ROPE_THETA, B_NOPE)

    h0 = x.reshape(n, d)
    hn_a = _rms_fwd(h0, g_a_pre, BF16, name="a_pre_norm", tr=1024)
    is_qk = lambda j: j != 2
    is_q = lambda j: j == 0
    z_blk_a = 3 * A_GROUPS
    z_a = _matmul(hn_a, w_in_a, "nn", BF16, name="a_proj_z", b_cols=(z_blk_a, 1))
    o_groups, lse_groups, qkv_cm, hn_cm, tabs_cm = [], [], [], [], []
    for g, dil in enumerate(A_DILATIONS):
        flat = lambda a: _class_major(a, bl, t, dil).reshape(n, a.shape[-1])
        hn_g = hn_a if dil == 1 else flat(hn_a)
        tabs_g = tabs_a if dil == 1 else lax.optimization_barrier(tuple(flat(tb) for tb in tabs_a))
        proj_g = _matmul(hn_g, w_in_a, "nn", BF16, name=f"a_proj_{g}", rope=(tabs_g, is_qk),
                         out_scale=(A_SCALE * LOG2E, is_q), b_cols=(3 * g, 3))
        src = proj_g.reshape(bl, dil, t // dil, 3 * A_WIDTH)
        hn_cm.append(hn_g)
        tabs_cm.append(tabs_g)
        qkv_cm.append(src)
        o_g, lse_g = _attn_a_fwd(src, 0, qb, BF16, name=f"attn_a_fwd_{g}")
        o_groups.append(_natural(o_g))
        lse_groups.append(_natural(lse_g))
    ypre_a, om_a, lse_a = _merge_gate_fwd(o_groups, lse_groups, z_a, 0)
    y_a = _matmul(ypre_a, w_out_a, "nn", F32, name="a_out")
    g_kvn = kv_norm.reshape(1, d)
    g_lat = kv_latent_norm.reshape(1, B_KV_LORA)
    h1, hn_kv, hn_b = _post_norm_block(y_a, g_a_post, h0, [g_kvn, b_pre_norm], name="a_post_norm")

    ckr = _matmul(hn_kv, w_down_p, "nn", F32, name="kv_down")
    c_kv, k_rope = _kv_latent_fwd(ckr, g_lat, tabs_b)
    kvup = _matmul(c_kv, w_up_cat, "nn", BF16, name="kv_up")
    z_b = _matmul(hn_b, w_z, "nn", BF16, name="b_proj_z")
    cq_raw = _matmul(hn_b, w_cq, "nn", F32, name="b_proj_q")
    c_q = _rms_fwd(cq_raw, b_q_norm, BF16, name="b_q_norm", tr=1024)
    always = lambda j: True
    q_cat = _matmul(c_q, w_q_up_p, "nn", BF16, name="b_q_up", rope=(tabs_b, always),
                    out_scale=(B_SCALE * LOG2E, always))
    r3 = lambda a: a.reshape(bl, t, a.shape[-1])
    tabs_b3 = tuple(r3(tb) for tb in tabs_b)
    ypre_b, o_b, lse_b, lse_rows_b = _mla_fwd(r3(q_cat), r3(kvup), r3(k_rope), r3(z_b), tq)
    y_b = _matmul(ypre_b.reshape(n, B_WIDTH), w_out_b, "nn", F32, name="b_out")
    dh2, loss_part = _post_norm_loss(y_b, b_post_norm, h1, loss_target.reshape(n, d))

    dy_b, dg_b_post = _rms_bwd(y_b, b_post_norm, dh2, BF16, name="b_post_norm_bwd", tr=1024)
    dypre_b = _matmul(dy_b, w_out_b, "nt", BF16, name="b_out_dx")
    dw_out_b = _matmul(ypre_b.reshape(n, B_WIDTH), dy_b, "tn", F32, name="b_out_dw", tm=1024, tk=2048)
    do_b, dz_b = _gate_bwd(dypre_b, o_b.reshape(n, B_WIDTH), z_b, 0, name="b_gate_bwd", with_delta=False)
    dq_cat, delta_rows_b = _mla_dq(r3(q_cat), r3(kvup), r3(k_rope), r3(do_b), o_b, lse_b, tabs_b3, tq)
    dq_cat = dq_cat.reshape(n, -1)
    dk_cat, dv_b = _mla_dkv(r3(q_cat), r3(kvup), r3(k_rope), r3(do_b), lse_rows_b, delta_rows_b, tq)
    dk_cat, dv_b = dk_cat.reshape(n, -1), dv_b.reshape(n, -1)
    dcq_n = _matmul(dq_cat, w_q_up_p, "nt", F32, name="b_q_up_dx")
    dw_q_up_p = _matmul(c_q, dq_cat, "tn", F32, name="b_q_up_dw", tm=1024, tk=2048)
    dcq, dg_b_q = _rms_bwd(cq_raw, b_q_norm, dcq_n, BF16, name="b_q_norm_bwd", tr=1024)
    dhn_b = _matmul(dz_b, w_z, "nt", F32, name="b_proj_z_dx")
    dhn_b = _matmul(dcq, w_cq, "nt", F32, name="b_proj_q_dx", add=dhn_b)
    dw_z = _matmul(hn_b, dz_b, "tn", F32, name="b_proj_z_dw", tm=1024, tk=2048)
    dw_cq = _matmul(hn_b, dcq, "tn", F32, name="b_proj_q_dw", tm=1024, tk=2048)
    dckv_n = _matmul(dk_cat, w_up_k, "nt", F32, name="kv_up_k_dx")
    dckv_n = _matmul(dv_b, w_up_v, "nt", F32, name="kv_up_v_dx", add=dckv_n)
    dw_up_k = _matmul(c_kv, dk_cat, "tn", F32, name="kv_up_k_dw", tm=1024, tk=2048)
    dw_up_v = _matmul(c_kv, dv_b, "tn", F32, name="kv_up_v_dw", tm=1024, tk=2048)
    dckr, dg_lat = _kv_latent_bwd(dckv_n, ckr, g_lat, dk_cat, tabs_b)
    dhn_kv = _matmul(dckr, w_down_p, "nt", F32, name="kv_down_dx")
    dw_down_p = _matmul(hn_kv, dckr, "tn", F32, name="kv_down_dw", tm=1024, tk=2048)
    dh1, dg_b_pre, dg_kvn = _rms_bwd_pair(h1, b_pre_norm, dhn_b, g_kvn, dhn_kv, dh2, name="h1_norms_bwd")

    dy_a, dg_a_post = _rms_bwd(y_a, g_a_post, dh1, BF16, name="a_post_norm_bwd", tr=1024)
    dypre_a = _matmul(dy_a, w_out_a, "nt", BF16, name="a_out_dx")
    dw_out_a = _matmul(ypre_a, dy_a, "tn", F32, name="a_out_dw", tm=1024, tk=2048)
    do_a, dz_a, delta_a = _gate_bwd(dypre_a, om_a, z_a, 0, name="a_gate_bwd", with_delta=True)
    dw_cols = A_IN_WIDTH // N_CHIPS
    dw_tn = _tile(dw_cols, 512)
    dw_kwargs = dict(tm=1024, tn=dw_tn, tk=4096, out_chunk_blocks=dw_cols // dw_tn)
    r_big = _matmul(hn_a, dz_a, "tn", F32, name="a_proj_dw_z", out_full=(N_CHIPS, d, dw_cols),
                    out_joff=z_blk_a * A_WIDTH // dw_tn, **dw_kwargs)
    dqkvs = []
    for g, dil in enumerate(A_DILATIONS):
        cm = lambda a: _class_major(a, bl, t, dil)
        swap = lambda a: jnp.swapaxes(a, 2, 3)
        lse_cm, delta_cm = cm(lse_a), cm(delta_a)
        tabs_g = tuple(tb.reshape(bl, dil, t // dil, LANES) for tb in tabs_cm[g])
        dqkv = _attn_a_bwd(qkv_cm[g], 0, cm(do_a), lse_cm, delta_cm, swap(lse_cm), swap(delta_cm),
                           tabs_g, qb, name=f"attn_a_bwd_{g}").reshape(n, 3 * A_WIDTH)
        dqkvs.append(dqkv)
        r_big = _matmul(hn_cm[g], dqkv, "tn", F32, name=f"a_proj_dw_{g}", out_into=r_big,
                        out_joff=3 * g * A_WIDTH // dw_tn, **dw_kwargs)
    r_outs = jnp.concatenate([dw_out_a.reshape(N_CHIPS, A_WIDTH // N_CHIPS, d),
                              dw_out_b.reshape(N_CHIPS, B_WIDTH // N_CHIPS, d)], axis=1)

    bulk = [r_big, r_outs]
    recv_b = _pair_send_other_half(bulk, name="reduce_pair_send")
    halves_b = [_add_my_half(s, p, core_arr, BF16, name=f"reduce_pair_add_{i}")
                for i, (s, p) in enumerate(zip(bulk, recv_b))]
    send_sems, recv_sems, sent_b, lands_b, token = _chip_exchange_start(halves_b, name="reduce_exchange_start")

    dhn_a = _matmul(dz_a, w_in_a, "nt", F32, name="a_proj_dx_z", b_koff=z_blk_a, after=token)
    dhn_more = []
    for g, dil in enumerate(A_DILATIONS):
        tk_dx = 3 * A_WIDTH
        if dil == 1:
            dhn_a = _matmul(dqkvs[g], w_in_a, "nt", F32, name=f"a_proj_dx_{g}", add=dhn_a, tk=tk_dx, b_koff=g,
                            after=token)
        else:
            part = _matmul(dqkvs[g], w_in_a, "nt", BF16, name=f"a_proj_dx_{g}", tk=tk_dx, b_koff=g, after=token)
            dhn_more.append(_natural(part.reshape(bl, dil, t // dil, d)))
    grad_x, dg_a_pre = _rms_bwd(h0, g_a_pre, dhn_a, F32, name="a_pre_norm_bwd", adds=(dh1,),
                                dy_more=tuple(dhn_more))

    dw_up = jnp.concatenate([dw_up_k.reshape(B_KV_LORA, B_HEADS, LANES)[:, :, :B_NOPE],
                             dw_up_v.reshape(B_KV_LORA, B_HEADS, B_VDIM)], axis=2).reshape(B_KV_LORA, -1)
    dw_q_up = dw_q_up_p.reshape(B_Q_LORA, B_HEADS, LANES)[:, :, :B_QK_DIM].reshape(B_Q_LORA, -1)
    dw_down = jnp.concatenate([dw_down_p[:, :B_KV_LORA], dw_down_p[:, B_KV_LORA + B_NOPE:B_KV_LORA + B_NOPE + B_ROPE]], axis=1)
    dw_in_b = jnp.concatenate([dw_cq, dw_z], axis=1)
    vec_rep = [dg_kvn.reshape(-1), dg_lat.reshape(-1), dg_b_pre.reshape(-1), dg_b_q.reshape(-1),
               dg_b_post.reshape(-1), loss_part.reshape(-1)]
    vec_shapes = [(dq4,), (dq4,)] + [v.shape for v in vec_rep]
    down_c = dw_down.reshape(N_CHIPS, dq4, -1)
    up_c = _to_chunks_cols(dw_up)
    inb_c = _to_chunks_cols(dw_in_b)
    qup_c = _to_chunks_cols(dw_q_up)
    small_chunks = []
    for k in range(N_CHIPS):
        vecs = [dg_a_pre.reshape(-1)[k * dq4:(k + 1) * dq4], dg_a_post.reshape(-1)[k * dq4:(k + 1) * dq4]] + vec_rep
        small_chunks.append(_pack_rows([down_c[k], up_c[k], inb_c[k], qup_c[k]] + vecs, 32))
    r_small = jnp.stack(small_chunks)

    recv_s = _pair_send_other_half([r_small], name="reduce_pair_send_small")
    halves_s = [_add_my_half(r_small, recv_s[0], core_arr, F32, name="reduce_pair_add_small")]
    parts_s = list(_chip_exchange(halves_s, name="reduce_exchange_small"))
    parts_b = _chip_exchange_wait(send_sems, recv_sems, sent_b, lands_b, grad_x, name="reduce_exchange_wait")
    sums = [_sum_chips(p, own, chip_arr, name=f"reduce_chip_sum_{i}")
            for i, (p, own) in enumerate(zip(parts_b + parts_s, sent_b + halves_s))]
    others = _pair_swap(sums, name="reduce_pair_swap")
    g_big, g_outs_r, g_small_r = [_join_halves(m, o, core_arr, name=f"reduce_join_{i}")
                                  for i, (m, o) in enumerate(zip(sums, others))]

    grads = {}
    grads["a_w_in"] = g_big
    grads["a_w_out"] = g_outs_r[:A_WIDTH // N_CHIPS]
    grads["b_w_out"] = g_outs_r[A_WIDTH // N_CHIPS:]
    small_out_shapes = [down_c.shape[1:], up_c.shape[1:], inb_c.shape[1:], qup_c.shape[1:]] + vec_shapes
    (grads["kv_w_down"], grads["kv_w_up"], grads["b_w_in"], grads["b_w_q_up"], grads["a_pre_norm"],
     grads["a_post_norm"], grads["kv_norm"], grads["kv_latent_norm"], grads["b_pre_norm"], grads["b_q_norm"],
     grads["b_post_norm"], loss_sum) = _unpack(g_small_r.reshape(-1), small_out_shapes)

    weights = dict(a_pre_norm=a_pre_norm, a_w_in=a_w_in, a_w_out=a_w_out, a_post_norm=a_post_norm, kv_norm=kv_norm,
                   kv_w_down=kv_w_down, kv_latent_norm=kv_latent_norm, kv_w_up=kv_w_up, b_pre_norm=b_pre_norm,
                   b_w_in=b_w_in, b_q_norm=b_q_norm, b_w_q_up=b_w_q_up, b_w_out=b_w_out, b_post_norm=b_post_norm)
    names = list(weights)
    out_g, out_d, out_m, out_v = [], [], [], []
    for i, nm in enumerate(names):
        w = weights[nm]
        two_d = (1, w.shape[0]) if w.ndim == 1 else (w.shape[-2], w.shape[-1])
        gw = grads[nm].reshape(two_d)
        dlt, new_m, new_v = _adamw(w.reshape(two_d), gw, moments[i].reshape(two_d),
                                   moments[len(names) + i].reshape(two_d), name=f"adamw_{nm}")
        out_g.append(gw.reshape(w.shape))
        out_d.append(dlt.reshape(w.shape))
        out_m.append(new_m.reshape(w.shape))
        out_v.append(new_v.reshape(w.shape))
    return (loss_sum.reshape(()), grad_x.reshape(bl, t, d), *out_g, *out_d, *out_m, *out_v)


def kernel(x, positions, a_pre_norm, a_w_in, a_w_out, a_post_norm, kv_norm, kv_w_down, kv_latent_norm, kv_w_up, b_pre_norm, b_w_in, b_q_norm, b_w_q_up, b_w_out, b_post_norm, loss_target, m_a_pre_norm, m_a_w_in, m_a_w_out, m_a_post_norm, m_kv_norm, m_kv_w_down, m_kv_latent_norm, m_kv_w_up, m_b_pre_norm, m_b_w_in, m_b_q_norm, m_b_w_q_up, m_b_w_out, m_b_post_norm, v_a_pre_norm, v_a_w_in, v_a_w_out, v_a_post_norm, v_kv_norm, v_kv_w_down, v_kv_latent_norm, v_kv_w_up, v_b_pre_norm, v_b_w_in, v_b_q_norm, v_b_w_q_up, v_b_w_out, v_b_post_norm):
    moments = (m_a_pre_norm, m_a_w_in, m_a_w_out, m_a_post_norm, m_kv_norm, m_kv_w_down, m_kv_latent_norm, m_kv_w_up,
               m_b_pre_norm, m_b_w_in, m_b_q_norm, m_b_w_q_up, m_b_w_out, m_b_post_norm,
               v_a_pre_norm, v_a_w_in, v_a_w_out, v_a_post_norm, v_kv_norm, v_kv_w_down, v_kv_latent_norm, v_kv_w_up,
               v_b_pre_norm, v_b_w_in, v_b_q_norm, v_b_w_q_up, v_b_w_out, v_b_post_norm)
    return _train_step(x, positions, a_pre_norm, a_w_in, a_w_out, a_post_norm, kv_norm, kv_w_down, kv_latent_norm,
                       kv_w_up, b_pre_norm, b_w_in, b_q_norm, b_w_q_up, b_w_out, b_post_norm, loss_target, moments)
```

```python
import math

import jax
import jax.numpy as jnp
from jax import lax
from jax.experimental import pallas as pl
from jax.experimental.pallas import tpu as pltpu

F32 = jnp.float32
BF16 = jnp.bfloat16
MESH = pl.DeviceIdType.MESH

NORM_EPS = 1e-6
NEG = -1e30
LANES = 128
VMEM_LIMIT = 56 * 1024 * 1024
LOG2E = math.log2(math.e)
LN2 = math.log(2.0)

A_GROUPS = 3
A_DILATIONS = (1, 4, 16)
A_HEADS = 8
A_HEAD_DIM = 128
A_WIDTH = A_HEADS * A_HEAD_DIM
A_ROPE_THETA = 500000.0
A_IN_WIDTH = A_GROUPS * 3 * A_WIDTH + A_WIDTH
A_SCALE = A_HEAD_DIM ** -0.5

B_HEADS = 16
B_NOPE = 64
B_ROPE = 32
B_QK_DIM = B_NOPE + B_ROPE
B_VDIM = 64
B_WIDTH = B_HEADS * B_VDIM
B_Q_LORA = 384
B_KV_LORA = 256
B_ROPE_THETA = 10000.0
B_SCALE = B_QK_DIM ** -0.5

ADAM_LR = 0.001
ADAM_B1 = 0.9
ADAM_B2 = 0.999
ADAM_EPS = 1e-08
ADAM_WD = 0.01
ADAM_STEP = 10

N_CHIPS = 4
PACK_COLS = 512


def _params(sem=None):
    return pltpu.CompilerParams(dimension_semantics=sem, vmem_limit_bytes=VMEM_LIMIT)


def _tile(n, want):
    t = min(n, want)
    assert n % t == 0, (n, want)
    return t


def _row_tile(n, want):
    for t in range(min(n, want), 0, -1):
        if n % t == 0 and (t % 16 == 0 or t == n):
            return t
    return n


def _rope_tables(positions, theta, lane0):
    half = 16
    inv_freq = 1.0 / (theta ** (jnp.arange(half, dtype=F32) * (2.0 / (2 * half))))
    n = positions.size
    per_row = LANES // half
    pos = jnp.repeat(positions.astype(F32).reshape(n // per_row, per_row), half, axis=1)
    ang = pos * jnp.tile(inv_freq, per_row)
    cos, sin = lax.optimization_barrier((jnp.cos(ang), jnp.sin(ang)))
    cos, sin = cos.reshape(n, half), sin.reshape(n, half)
    pre = jnp.zeros((n, lane0), F32)
    post = jnp.zeros((n, LANES - lane0 - 2 * half), F32)
    z16 = jnp.zeros((n, half), F32)
    c = jnp.concatenate([pre + 1.0, cos, cos, post + 1.0], axis=1)
    sa = jnp.concatenate([pre, -sin, z16, post], axis=1)
    sb = jnp.concatenate([pre, z16, sin, post], axis=1)
    return lax.optimization_barrier((c, sa, sb))


def _rope_apply(x, c, sa, sb, sign):
    k = x.shape[1] // LANES
    if k > 1:
        c, sa, sb = (jnp.concatenate([t] * k, axis=1) for t in (c, sa, sb))
    w = x.shape[1]
    up = pltpu.roll(x, w - 16, 1)
    dn = pltpu.roll(x, 16, 1)
    if sign > 0:
        return x * c + up * sa + dn * sb
    return x * c - up * sa - dn * sb


def _matmul(a, b, mode, out_dtype, *, name, tm=None, tn=1024, tk=None, add=None, rope=None,
            out_scale=None, b_koff=0, b_cols=None, out_into=None, out_full=None, out_joff=0,
            out_chunk_blocks=None, after=None):
    if mode == "nn":
        m, k = a.shape
        n = b.shape[1]
    elif mode == "nt":
        m, k = a.shape
        n = b.shape[0]
    else:
        k, m = a.shape
        n = b.shape[1]
    b_j0 = 0
    if b_cols is not None:
        tn = _tile(n, tn)
        b_j0, n = b_cols[0], b_cols[1] * tn
    if tm is None:
        tm = 512 if mode == "nt" else (2048 if (k <= 512 and rope is None) else 1024)
    if tk is None:
        tk = 3072 if mode == "nt" else 1024
    tm, tn, tk = _tile(m, tm), _tile(n, tn), _tile(k, tk)
    nk = k // tk
    if mode == "nn":
        a_spec = pl.BlockSpec((tm, tk), lambda j, i, kk: (i, kk))
        b_spec = pl.BlockSpec((tk, tn), lambda j, i, kk: (kk, j + b_j0))
        dims = (((1,), (0,)), ((), ()))
    elif mode == "nt":
        a_spec = pl.BlockSpec((tm, tk), lambda j, i, kk: (i, kk))
        b_spec = pl.BlockSpec((tn, tk), lambda j, i, kk: (j, kk + b_koff))
        dims = (((1,), (1,)), ((), ()))
    else:
        a_spec = pl.BlockSpec((tk, tm), lambda j, i, kk: (kk, i))
        b_spec = pl.BlockSpec((tk, tn), lambda j, i, kk: (kk, j))
        dims = (((0,), (0,)), ((), ()))
    operands = [a, b]
    in_specs = [a_spec, b_spec]
    if add is not None:
        operands.append(add)
        in_specs.append(pl.BlockSpec((tm, tn), lambda j, i, kk: (i, j)))
    if rope is not None:
        tables, rope_pred = rope
        for t in tables:
            operands.append(t)
            in_specs.append(pl.BlockSpec((tm, LANES), lambda j, i, kk: (i, 0)))
    aliases = {}
    if out_into is not None:
        aliases = {len(operands): 0}
        operands.append(out_into)
        in_specs.append(pl.BlockSpec(memory_space=pl.ANY))
        out_shape = jax.ShapeDtypeStruct(out_into.shape, out_into.dtype)
    elif out_full is not None:
        out_shape = jax.ShapeDtypeStruct(out_full, out_dtype)
    else:
        out_shape = jax.ShapeDtypeStruct((m, n), out_dtype)
    if after is not None:
        operands.append(after)
        in_specs.append(pl.BlockSpec(memory_space=pl.ANY))
    if out_chunk_blocks is not None:
        out_spec = pl.BlockSpec((None, tm, tn), lambda j, i, kk: ((j + out_joff) // out_chunk_blocks, i,
                                                                  (j + out_joff) % out_chunk_blocks))
    else:
        out_spec = pl.BlockSpec((tm, tn), lambda j, i, kk: (i, j + out_joff))

    def body(*refs):
        a_ref, b_ref = refs[0], refs[1]
        pos = 2
        add_ref = None
        if add is not None:
            add_ref = refs[pos]
            pos += 1
        tab_refs = None
        if rope is not None:
            tab_refs = refs[pos:pos + 3]
            pos += 3
        if out_into is not None:
            pos += 1
        if after is not None:
            pos += 1
        o_ref = refs[pos]
        acc_ref = refs[pos + 1] if nk > 1 else None

        def finish(res):
            if add_ref is not None:
                res = res + add_ref[...].astype(F32)
            if tab_refs is None:
                o_ref[...] = res.astype(o_ref.dtype)
                return
            j = pl.program_id(0)
            flag = rope_pred(j)

            roped = _rope_apply(res, tab_refs[0][...], tab_refs[1][...], tab_refs[2][...], 1)
            if out_scale is not None:
                value, scale_pred = out_scale
                use = scale_pred(j)
                roped = roped * (value if use is True else jnp.where(use, value, 1.0))
            if flag is True:
                o_ref[...] = roped.astype(o_ref.dtype)
                return

            @pl.when(flag)
            def _():
                o_ref[...] = roped.astype(o_ref.dtype)

            @pl.when(jnp.logical_not(flag))
            def _():
                o_ref[...] = res.astype(o_ref.dtype)

        part = lax.dot_general(a_ref[...].astype(BF16), b_ref[...].astype(BF16), dims,
                               preferred_element_type=F32)
        if nk == 1:
            finish(part)
            return
        kk = pl.program_id(2)

        @pl.when(kk == 0)
        def _():
            acc_ref[...] = part

        @pl.when(kk > 0)
        def _():
            acc_ref[...] += part

        @pl.when(kk == nk - 1)
        def _():
            finish(acc_ref[...])

    return pl.pallas_call(
        body, name=name, grid=(n // tn, m // tm, nk), in_specs=in_specs, out_specs=out_spec,
        out_shape=out_shape, input_output_aliases=aliases,
        scratch_shapes=[pltpu.VMEM((tm, tn), F32)] if nk > 1 else [],
        compiler_params=_params(("parallel", "parallel", "arbitrary")),
    )(*operands)


def _rms_fwd(x, g, out_dtype, *, name, add=None, tr=512):
    n, d = x.shape
    tr = _tile(n, tr)
    row = pl.BlockSpec((tr, d), lambda i: (i, 0))
    vec = pl.BlockSpec((1, d), lambda i: (0, 0))

    def body(*refs):
        x_ref, g_ref = refs[0], refs[1]
        o_ref = refs[-1]
        xv = x_ref[...].astype(F32)
        r = lax.rsqrt(jnp.mean(xv * xv, axis=-1, keepdims=True) + NORM_EPS)
        y = xv * r * g_ref[...]
        if add is not None:
            y = refs[2][...] + y
        o_ref[...] = y.astype(o_ref.dtype)

    ops = [x, g] + ([add] if add is not None else [])
    specs = [row, vec] + ([row] if add is not None else [])
    return pl.pallas_call(
        body, name=name, grid=(n // tr,), in_specs=specs, out_specs=row,
        out_shape=jax.ShapeDtypeStruct((n, d), out_dtype), compiler_params=_params(("parallel",)),
    )(*ops)


def _rms_bwd(x, g, dy, out_dtype, *, name, adds=(), dy_more=(), tr=512):
    n, d = x.shape
    tr = _tile(n, tr)
    steps = n // tr
    row = pl.BlockSpec((tr, d), lambda i: (i, 0))
    vec = pl.BlockSpec((1, d), lambda i: (0, 0))
    na = len(adds) + len(dy_more)

    def body(*refs):
        x_ref, g_ref, dy_ref = refs[:3]
        add_refs = refs[3:3 + len(adds)]
        more_refs = refs[3 + len(adds):3 + na]
        dx_ref, dg_ref, acc_ref = refs[3 + na:]
        i = pl.program_id(0)
        xv = x_ref[...].astype(F32)
        r = lax.rsqrt(jnp.mean(xv * xv, axis=-1, keepdims=True) + NORM_EPS)
        xh = xv * r
        dyv = dy_ref[...].astype(F32)
        for m_ref in more_refs:
            dyv = dyv + m_ref[...].astype(F32)
        part = (dyv * xh).reshape(tr // 8, 8, d).sum(axis=0)

        @pl.when(i == 0)
        def _():
            acc_ref[...] = part

        @pl.when(i > 0)
        def _():
            acc_ref[...] += part

        t = dyv * g_ref[...]
        dx = r * (t - xh * jnp.mean(t * xh, axis=-1, keepdims=True))
        for a_ref in add_refs:
            dx = dx + a_ref[...].astype(F32)
        dx_ref[...] = dx.astype(dx_ref.dtype)

        @pl.when(i == steps - 1)
        def _():
            dg_ref[...] = jnp.sum(acc_ref[...], axis=0, keepdims=True)

    return pl.pallas_call(
        body, name=name, grid=(steps,), in_specs=[row, vec, row] + [row] * na,
        out_specs=(row, vec),
        out_shape=(jax.ShapeDtypeStruct((n, d), out_dtype), jax.ShapeDtypeStruct((1, d), F32)),
        scratch_shapes=[pltpu.VMEM((8, d), F32)], compiler_params=_params(("arbitrary",)),
    )(x, g, dy, *adds, *dy_more)


def _rms(xv, g):
    return xv * lax.rsqrt(jnp.mean(xv * xv, axis=-1, keepdims=True) + NORM_EPS) * g


def _post_norm_block(y, g, h_in, next_gains, *, name, tr=1024):
    n, d = y.shape
    tr = _tile(n, tr)
    nk = len(next_gains)
    row = pl.BlockSpec((tr, d), lambda i: (i, 0))
    vec = pl.BlockSpec((1, d), lambda i: (0, 0))

    def body(*refs):
        y_ref, g_ref, h_ref = refs[:3]
        gk_refs = refs[3:3 + nk]
        o_ref = refs[3 + nk]
        hn_refs = refs[4 + nk:]
        h = h_ref[...] + _rms(y_ref[...], g_ref[...])
        o_ref[...] = h
        for gk_ref, hn_ref in zip(gk_refs, hn_refs):
            hn_ref[...] = _rms(h, gk_ref[...]).astype(BF16)

    return pl.pallas_call(
        body, name=name, grid=(n // tr,), in_specs=[row, vec, row] + [vec] * nk,
        out_specs=(row,) * (1 + nk),
        out_shape=(jax.ShapeDtypeStruct((n, d), F32),) + (jax.ShapeDtypeStruct((n, d), BF16),) * nk,
        compiler_params=_params(("parallel",)),
    )(y, g, h_in, *next_gains)


def _post_norm_loss(y, g, h_in, target, *, tr=1024):
    n, d = y.shape
    tr = _tile(n, tr)
    steps = n // tr
    row = pl.BlockSpec((tr, d), lambda i: (i, 0))

    def body(y_ref, g_ref, h_ref, t_ref, dh_ref, loss_ref, acc_ref):
        i = pl.program_id(0)
        e = h_ref[...] + _rms(y_ref[...], g_ref[...]) - t_ref[...]
        dh_ref[...] = e / d
        part = (e * e).reshape(tr // 8, 8, d).sum(axis=0)

        @pl.when(i == 0)
        def _():
            acc_ref[...] = part

        @pl.when(i > 0)
        def _():
            acc_ref[...] += part

        @pl.when(i == steps - 1)
        def _():
            s = jnp.sum(jnp.sum(acc_ref[...], axis=-1, keepdims=True), axis=0, keepdims=True)
            loss_ref[...] = 0.5 * s / d

    return pl.pallas_call(
        body, name="b_post_norm_loss", grid=(steps,),
        in_specs=[row, pl.BlockSpec((1, d), lambda i: (0, 0)), row, row],
        out_specs=(row, pl.BlockSpec((1, 1), lambda i: (0, 0))),
        out_shape=(jax.ShapeDtypeStruct((n, d), F32), jax.ShapeDtypeStruct((1, 1), F32)),
        scratch_shapes=[pltpu.VMEM((8, d), F32)], compiler_params=_params(("arbitrary",)),
    )(y, g, h_in, target)


def _rms_bwd_pair(x, g1, dy1, g2, dy2, add, *, name, tr=512):
    n, d = x.shape
    tr = _tile(n, tr)
    steps = n // tr
    row = pl.BlockSpec((tr, d), lambda i: (i, 0))
    vec = pl.BlockSpec((1, d), lambda i: (0, 0))

    def body(x_ref, g1_ref, d1_ref, g2_ref, d2_ref, add_ref, dx_ref, dg1_ref, dg2_ref, acc_ref):
        i = pl.program_id(0)
        xv = x_ref[...]
        r = lax.rsqrt(jnp.mean(xv * xv, axis=-1, keepdims=True) + NORM_EPS)
        xh = xv * r
        dx = add_ref[...]
        for k, (g_ref, d_ref) in enumerate(((g1_ref, d1_ref), (g2_ref, d2_ref))):
            dyv = d_ref[...].astype(F32)
            part = (dyv * xh).reshape(tr // 8, 8, d).sum(axis=0)

            @pl.when(i == 0)
            def _(part=part, k=k):
                acc_ref[k] = part

            @pl.when(i > 0)
            def _(part=part, k=k):
                acc_ref[k] += part

            t = dyv * g_ref[...]
            dx = dx + r * (t - xh * jnp.mean(t * xh, axis=-1, keepdims=True))
        dx_ref[...] = dx

        @pl.when(i == steps - 1)
        def _():
            dg1_ref[...] = jnp.sum(acc_ref[0], axis=0, keepdims=True)
            dg2_ref[...] = jnp.sum(acc_ref[1], axis=0, keepdims=True)

    return pl.pallas_call(
        body, name=name, grid=(steps,), in_specs=[row, vec, row, vec, row, row],
        out_specs=(row, vec, vec),
        out_shape=(jax.ShapeDtypeStruct((n, d), F32), jax.ShapeDtypeStruct((1, d), F32),
                   jax.ShapeDtypeStruct((1, d), F32)),
        scratch_shapes=[pltpu.VMEM((2, 8, d), F32)], compiler_params=_params(("arbitrary",)),
    )(x, g1, dy1, g2, dy2, add)


def _kv_latent_fwd(ckr, g_lat, tabs, *, tr=512):
    n = ckr.shape[0]
    tr = _tile(n, tr)
    lat = B_KV_LORA

    def body(c_ref, k_ref, g_ref, tc, tsa, tsb, ckv_ref, kr_ref):
        xv = c_ref[...]
        r = lax.rsqrt(jnp.mean(xv * xv, axis=-1, keepdims=True) + NORM_EPS)
        ckv_ref[...] = (xv * r * g_ref[...]).astype(BF16)
        kr_ref[...] = _rope_apply(k_ref[...], tc[...], tsa[...], tsb[...], 1).astype(BF16)

    tab = pl.BlockSpec((tr, LANES), lambda i: (i, 0))
    return pl.pallas_call(
        body, name="kv_latent_fwd", grid=(n // tr,),
        in_specs=[pl.BlockSpec((tr, lat), lambda i: (i, 0)),
                  pl.BlockSpec((tr, LANES), lambda i: (i, lat // LANES)),
                  pl.BlockSpec((1, lat), lambda i: (0, 0)), tab, tab, tab],
        out_specs=(pl.BlockSpec((tr, lat), lambda i: (i, 0)), tab),
        out_shape=(jax.ShapeDtypeStruct((n, lat), BF16), jax.ShapeDtypeStruct((n, LANES), BF16)),
        compiler_params=_params(("parallel",)),
    )(ckr, ckr, g_lat, *tabs)


def _kv_latent_bwd(dckv, ckr, g_lat, dk_cat, tabs, *, tr=512):
    n = ckr.shape[0]
    tr = _tile(n, tr)
    steps = n // tr
    lat = B_KV_LORA
    wk = dk_cat.shape[1]

    def body(d_ref, c_ref, g_ref, dk_ref, tc, tsa, tsb, o_ref, dg_ref, acc_ref):
        i = pl.program_id(0)
        xv = c_ref[...]
        r = lax.rsqrt(jnp.mean(xv * xv, axis=-1, keepdims=True) + NORM_EPS)
        xh = xv * r
        dyv = d_ref[...]
        part = (dyv * xh).reshape(tr // 8, 8, lat).sum(axis=0)

        @pl.when(i == 0)
        def _():
            acc_ref[...] = part

        @pl.when(i > 0)
        def _():
            acc_ref[...] += part

        t = dyv * g_ref[...]
        dx = r * (t - xh * jnp.mean(t * xh, axis=-1, keepdims=True))
        o_ref[:, 0:lat] = dx.astype(o_ref.dtype)
        dkr = dk_ref[:, 0:LANES].astype(F32)
        for h in range(1, wk // LANES):
            dkr = dkr + dk_ref[:, h * LANES:(h + 1) * LANES].astype(F32)
        o_ref[:, lat:lat + LANES] = _rope_apply(dkr, tc[...], tsa[...], tsb[...], -1).astype(o_ref.dtype)

        @pl.when(i == steps - 1)
        def _():
            dg_ref[...] = jnp.sum(acc_ref[...], axis=0, keepdims=True)

    tab = pl.BlockSpec((tr, LANES), lambda i: (i, 0))
    return pl.pallas_call(
        body, name="kv_latent_bwd", grid=(steps,),
        in_specs=[pl.BlockSpec((tr, lat), lambda i: (i, 0)), pl.BlockSpec((tr, lat), lambda i: (i, 0)),
                  pl.BlockSpec((1, lat), lambda i: (0, 0)), pl.BlockSpec((tr, wk), lambda i: (i, 0)),
                  tab, tab, tab],
        out_specs=(pl.BlockSpec((tr, lat + LANES), lambda i: (i, 0)), pl.BlockSpec((1, lat), lambda i: (0, 0))),
        out_shape=(jax.ShapeDtypeStruct((n, lat + LANES), BF16), jax.ShapeDtypeStruct((1, lat), F32)),
        scratch_shapes=[pltpu.VMEM((8, lat), F32)], compiler_params=_params(("arbitrary",)),
    )(dckv, ckr, g_lat, dk_cat, *tabs)


def _sigmoid(z):
    return 1.0 / (1.0 + jnp.exp(-z))


def _lane_place(cols, width):
    rows = cols[0].shape[0]
    lane = lax.broadcasted_iota(jnp.int32, (rows, width), 1)
    out = jnp.zeros((rows, width), F32)
    for h, col in enumerate(cols):
        out = jnp.where(lane == h, col, out)
    return out


def _merge_gate_fwd(outs, lses, proj, z_block, *, tr=512):
    n, w = outs[0].shape
    tr = _tile(n, tr)
    ng = len(outs)

    def body(*refs):
        o_refs = refs[:ng]
        l_refs = refs[ng:2 * ng]
        z_ref = refs[2 * ng]
        y_ref, om_ref, lse_ref = refs[2 * ng + 1:]
        ls = [r[...] for r in l_refs]
        mx = ls[0]
        for l in ls[1:]:
            mx = jnp.maximum(mx, l)
        ssum = jnp.exp2(ls[0] - mx)
        for l in ls[1:]:
            ssum = ssum + jnp.exp2(l - mx)
        tot = mx + jnp.log2(ssum)
        lse_ref[...] = tot
        ws = [jnp.exp2(l - tot) for l in ls]
        for h in range(A_HEADS):
            sl = slice(h * A_HEAD_DIM, (h + 1) * A_HEAD_DIM)
            o = ws[0][:, h:h + 1] * o_refs[0][:, sl]
            for gi in range(1, ng):
                o = o + ws[gi][:, h:h + 1] * o_refs[gi][:, sl]
            z = z_ref[:, sl].astype(F32)
            om_ref[:, sl] = o.astype(BF16)
            y_ref[:, sl] = (o * (z * _sigmoid(z))).astype(BF16)

    row = pl.BlockSpec((tr, w), lambda i: (i, 0))
    lrow = pl.BlockSpec((tr, A_HEADS), lambda i: (i, 0))
    return pl.pallas_call(
        body, name="merge_gate_fwd", grid=(n // tr,),
        in_specs=[row] * ng + [lrow] * ng + [pl.BlockSpec((tr, w), lambda i: (i, z_block))],
        out_specs=(row, row, lrow),
        out_shape=(jax.ShapeDtypeStruct((n, w), BF16), jax.ShapeDtypeStruct((n, w), BF16),
                   jax.ShapeDtypeStruct((n, A_HEADS), F32)),
        compiler_params=_params(("parallel",)),
    )(*outs, *lses, proj)


def _gate_bwd(dy, o, z_arr, z_block, *, name, with_delta, tr=512):
    n, w = dy.shape
    tr = _tile(n, tr)

    def body(*refs):
        dy_ref, o_ref, z_ref, do_ref, dz_ref = refs[:5]
        dyv = dy_ref[...].astype(F32)
        ov = o_ref[...].astype(F32)
        z = z_ref[...].astype(F32)
        sig = _sigmoid(z)
        do = dyv * (z * sig)
        do_ref[...] = do.astype(BF16)
        dz_ref[...] = (dyv * ov * (sig * (1.0 + z * (1.0 - sig)))).astype(BF16)
        if with_delta:
            prod = do * ov
            cols = [jnp.sum(prod[:, h * A_HEAD_DIM:(h + 1) * A_HEAD_DIM], axis=-1, keepdims=True)
                    for h in range(A_HEADS)]
            refs[5][...] = _lane_place(cols, A_HEADS)

    row = pl.BlockSpec((tr, w), lambda i: (i, 0))
    out_specs = [row, row]
    out_shape = [jax.ShapeDtypeStruct((n, w), BF16), jax.ShapeDtypeStruct((n, w), BF16)]
    if with_delta:
        out_specs.append(pl.BlockSpec((tr, A_HEADS), lambda i: (i, 0)))
        out_shape.append(jax.ShapeDtypeStruct((n, A_HEADS), F32))
    return pl.pallas_call(
        body, name=name, grid=(n // tr,),
        in_specs=[row, row, pl.BlockSpec((tr, w), lambda i: (i, z_block))],
        out_specs=tuple(out_specs), out_shape=tuple(out_shape), compiler_params=_params(("parallel",)),
    )(dy, o, z_arr)


def _dot_nt(a, b):
    return lax.dot_general(a, b, (((1,), (1,)), ((), ())), preferred_element_type=F32)


def _dot_nn(a, b):
    return lax.dot_general(a, b, (((1,), (0,)), ((), ())), preferred_element_type=F32)


def _attn_a_fwd(qkv, cb0, qb, out_dtype, *, name):
    bl, dil, ln, _ = qkv.shape
    nb = ln // qb
    hw = A_WIDTH
    heads = range(A_HEADS)
    sls = [slice(h * A_HEAD_DIM, (h + 1) * A_HEAD_DIM) for h in heads]

    def body(*refs):
        if nb > 1:
            q_ref, kc_ref, vc_ref, kp_ref, vp_ref, o_ref, lse_ref = refs
        else:
            q_ref, kc_ref, vc_ref, o_ref, lse_ref = refs
        i = pl.program_id(2)
        qi = lax.broadcasted_iota(jnp.int32, (qb, qb), 0)
        ki = lax.broadcasted_iota(jnp.int32, (qb, qb), 1)
        mask_c = ki <= qi
        mask_p = jnp.logical_and(ki >= qi, i >= 1)
        s_c = [jnp.where(mask_c, _dot_nt(q_ref[:, sls[h]], kc_ref[:, sls[h]]), NEG) for h in heads]
        m = [jnp.max(s_c[h], axis=-1, keepdims=True) for h in heads]
        if nb > 1:
            s_p = [jnp.where(mask_p, _dot_nt(q_ref[:, sls[h]], kp_ref[:, sls[h]]), NEG) for h in heads]
            m = [jnp.maximum(m[h], jnp.max(s_p[h], axis=-1, keepdims=True)) for h in heads]
        p_c = [jnp.exp2(s_c[h] - m[h]) for h in heads]
        l = [jnp.sum(p_c[h], axis=-1, keepdims=True) for h in heads]
        acc = [_dot_nn(p_c[h].astype(BF16), vc_ref[:, sls[h]]) for h in heads]
        if nb > 1:
            p_p = [jnp.exp2(s_p[h] - m[h]) for h in heads]
            l = [l[h] + jnp.sum(p_p[h], axis=-1, keepdims=True) for h in heads]
            acc = [acc[h] + _dot_nn(p_p[h].astype(BF16), vp_ref[:, sls[h]]) for h in heads]
        for h in heads:
            o_ref[:, sls[h]] = (acc[h] / l[h]).astype(o_ref.dtype)
        lse_ref[...] = _lane_place([m[h] + jnp.log2(l[h]) for h in heads], A_HEADS)

    def spec(off, prev):
        if prev:
            return pl.BlockSpec((None, None, qb, hw), lambda b, r, i: (b, r, jnp.maximum(i - 1, 0), cb0 + off))
        return pl.BlockSpec((None, None, qb, hw), lambda b, r, i: (b, r, i, cb0 + off))

    return pl.pallas_call(
        body, name=name, grid=(bl, dil, nb),
        in_specs=[spec(0, False), spec(1, False), spec(2, False)] + ([spec(1, True), spec(2, True)] if nb > 1 else []),
        out_specs=(pl.BlockSpec((None, None, qb, hw), lambda b, r, i: (b, r, i, 0)),
                   pl.BlockSpec((None, None, qb, A_HEADS), lambda b, r, i: (b, r, i, 0))),
        out_shape=(jax.ShapeDtypeStruct((bl, dil, ln, hw), out_dtype),
                   jax.ShapeDtypeStruct((bl, dil, ln, A_HEADS), F32)),
        compiler_params=_params(("parallel", "parallel", "arbitrary")),
    )(*([qkv] * (5 if nb > 1 else 3)))


def _attn_a_bwd(qkv, cb0, do, lse, delta, lse_t, delta_t, tabs, qb, *, name):
    bl, dil, ln, _ = qkv.shape
    nb = ln // qb
    hw = A_WIDTH

    def body(*refs):
        if nb > 1:
            (q_ref, kc_ref, vc_ref, do_ref, lse_ref, dl_ref, lt_ref, dt_ref, tc, tsa, tsb,
             qn_ref, kp_ref, vp_ref, don_ref, ltn_ref, dtn_ref, o_ref) = refs
        else:
            q_ref, kc_ref, vc_ref, do_ref, lse_ref, dl_ref, lt_ref, dt_ref, tc, tsa, tsb, o_ref = refs
        i = pl.program_id(2)
        row = lax.broadcasted_iota(jnp.int32, (qb, qb), 0)
        col = lax.broadcasted_iota(jnp.int32, (qb, qb), 1)
        m_qc = col <= row
        m_kc = row <= col
        m_qp = jnp.logical_and(col >= row, i >= 1)
        m_kn = jnp.logical_and(row >= col, i + 1 < nb)
        c, sa, sb = tc[...], tsa[...], tsb[...]
        heads = range(A_HEADS)
        sls = [slice(h * A_HEAD_DIM, (h + 1) * A_HEAD_DIM) for h in heads]
        q, kc = [q_ref[:, sl] for sl in sls], [kc_ref[:, sl] for sl in sls]
        vc, dov = [vc_ref[:, sl] for sl in sls], [do_ref[:, sl] for sl in sls]
        lse_c = [lse_ref[:, h:h + 1] for h in heads]
        dl_c = [dl_ref[:, h:h + 1] for h in heads]
        s = [_dot_nt(q[h], kc[h]) for h in heads]
        st = [_dot_nt(kc[h], q[h]) for h in heads]
        dp = [_dot_nt(dov[h], vc[h]) for h in heads]
        dpt = [_dot_nt(vc[h], dov[h]) for h in heads]
        p = [jnp.exp2(jnp.where(m_qc, s[h], NEG) - lse_c[h]) for h in heads]
        pt = [jnp.exp2(jnp.where(m_kc, st[h], NEG) - lt_ref[h:h + 1, :]) for h in heads]
        dq = [_dot_nn((p[h] * (dp[h] - dl_c[h])).astype(BF16), kc[h]) for h in heads]
        dk = [_dot_nn((pt[h] * (dpt[h] - dt_ref[h:h + 1, :])).astype(BF16), q[h]) for h in heads]
        dv = [_dot_nn(pt[h].astype(BF16), dov[h]) for h in heads]
        if nb > 1:
            kp, vp = [kp_ref[:, sl] for sl in sls], [vp_ref[:, sl] for sl in sls]
            qn, don = [qn_ref[:, sl] for sl in sls], [don_ref[:, sl] for sl in sls]
            s = [_dot_nt(q[h], kp[h]) for h in heads]
            st = [_dot_nt(kc[h], qn[h]) for h in heads]
            dp = [_dot_nt(dov[h], vp[h]) for h in heads]
            dpt = [_dot_nt(vc[h], don[h]) for h in heads]
            p = [jnp.exp2(jnp.where(m_qp, s[h], NEG) - lse_c[h]) for h in heads]
            pt = [jnp.exp2(jnp.where(m_kn, st[h], NEG) - ltn_ref[h:h + 1, :]) for h in heads]
            dq = [dq[h] + _dot_nn((p[h] * (dp[h] - dl_c[h])).astype(BF16), kp[h]) for h in heads]
            dk = [dk[h] + _dot_nn((pt[h] * (dpt[h] - dtn_ref[h:h + 1, :])).astype(BF16), qn[h]) for h in heads]
            dv = [dv[h] + _dot_nn(pt[h].astype(BF16), don[h]) for h in heads]
        for h in heads:
            o_ref[:, h * A_HEAD_DIM:(h + 1) * A_HEAD_DIM] = _rope_apply(dq[h] * A_SCALE, c, sa, sb, -1).astype(BF16)
            o_ref[:, hw + h * A_HEAD_DIM:hw + (h + 1) * A_HEAD_DIM] = _rope_apply(dk[h] * LN2, c, sa, sb, -1).astype(BF16)
            o_ref[:, 2 * hw + h * A_HEAD_DIM:2 * hw + (h + 1) * A_HEAD_DIM] = dv[h].astype(BF16)

    def cur(w, col):
        return pl.BlockSpec((None, None, qb, w), lambda b, r, i: (b, r, i, col))

    def prev(w, col):
        return pl.BlockSpec((None, None, qb, w), lambda b, r, i: (b, r, jnp.maximum(i - 1, 0), col))

    def nxt(w, col):
        return pl.BlockSpec((None, None, qb, w), lambda b, r, i: (b, r, jnp.minimum(i + 1, nb - 1), col))

    t_cur = pl.BlockSpec((None, None, A_HEADS, qb), lambda b, r, i: (b, r, 0, i))
    t_nxt = pl.BlockSpec((None, None, A_HEADS, qb), lambda b, r, i: (b, r, 0, jnp.minimum(i + 1, nb - 1)))
    in_specs = [cur(hw, cb0), cur(hw, cb0 + 1), cur(hw, cb0 + 2), cur(hw, 0), cur(A_HEADS, 0), cur(A_HEADS, 0),
                t_cur, t_cur, cur(LANES, 0), cur(LANES, 0), cur(LANES, 0)]
    operands = [qkv, qkv, qkv, do, lse, delta, lse_t, delta_t, *tabs]
    if nb > 1:
        in_specs += [nxt(hw, cb0), prev(hw, cb0 + 1), prev(hw, cb0 + 2), nxt(hw, 0), t_nxt, t_nxt]
        operands += [qkv, qkv, qkv, do, lse_t, delta_t]
    return pl.pallas_call(
        body, name=name, grid=(bl, dil, nb), in_specs=in_specs, out_specs=cur(3 * hw, 0),
        out_shape=jax.ShapeDtypeStruct((bl, dil, ln, 3 * hw), BF16),
        compiler_params=_params(("parallel", "parallel", "arbitrary")),
    )(*operands)


def _head_terms(do, o, lse, e):
    rows = do.shape[0]
    lane = lax.broadcasted_iota(jnp.int32, (rows, LANES), 1)
    mine = (lane < B_VDIM) if e == 0 else (lane >= B_VDIM)
    prod = do.astype(F32) * o.astype(F32)
    dl = jnp.sum(jnp.where(mine, prod, 0.0), axis=-1, keepdims=True)
    do_e = jnp.where(mine, do, jnp.zeros_like(do))
    return do_e, dl, lse[:, e * B_VDIM:e * B_VDIM + 1]


def _col_to_row(col, rows):
    return jnp.transpose(jnp.broadcast_to(col, (rows, LANES)))[0:1, :]


def _mla_fwd(q_cat, kvup, kr, z, tq):
    bl, t, _ = q_cat.shape
    nq = t // tq
    pairs = B_HEADS // 2
    v_blk0 = (B_HEADS * LANES) // LANES

    def body(q_ref, k_ref, v_ref, kr_ref, z_ref, y_ref, o_ref, lse_ref, lrow_ref, m_ref, acc_ref):
        qi = pl.program_id(2)
        qs = [q_ref[:, e * LANES:(e + 1) * LANES] for e in range(2)]
        row = lax.broadcasted_iota(jnp.int32, (tq, tq), 0)
        col = lax.broadcasted_iota(jnp.int32, (tq, tq), 1)
        tri = col <= row
        sum_lane = [B_VDIM, 0]

        for e in range(2):
            m_ref[e] = jnp.full((tq, LANES), NEG, F32)
            acc_ref[e] = jnp.zeros((tq, LANES), F32)

        def tile(k0, w, masked):
            lane = lax.broadcasted_iota(jnp.int32, (w, LANES), 1)
            first = lane < B_VDIM
            krv = kr_ref[pl.ds(k0, w), :]
            v = v_ref[pl.ds(k0, w), :]
            vs = [jnp.where(first, v, jnp.where(lane == B_VDIM, 1.0, 0.0).astype(BF16)),
                  jnp.where(first, jnp.where(lane == 0, 1.0, 0.0).astype(BF16), v)]
            ss = []
            for e in range(2):
                k = k_ref[pl.ds(k0, w), e * LANES:(e + 1) * LANES] + krv
                s = _dot_nt(qs[e], k)
                if masked:
                    r = lax.broadcasted_iota(jnp.int32, (tq, w), 0)
                    c = lax.broadcasted_iota(jnp.int32, (tq, w), 1)
                    s = jnp.where(c <= r + (w - tq), s, NEG)
                ss.append(s)
            for e in range(2):
                m_old = m_ref[e]
                m_new = jnp.maximum(m_old, jnp.max(ss[e], axis=-1, keepdims=True))
                p = jnp.exp2(ss[e] - jnp.concatenate([m_new] * (w // LANES), axis=1)).astype(BF16)
                m_ref[e] = m_new
                acc_ref[e] = jnp.exp2(m_old - m_new) * acc_ref[e] + _dot_nn(p, vs[e])

        def step(kb2, carry):
            tile(pl.multiple_of(kb2 * 2 * tq, 2 * tq), 2 * tq, False)
            return carry

        lax.fori_loop(0, qi // 2, step, 0)

        @pl.when(qi % 2 == 1)
        def _():
            tile(pl.multiple_of((qi - 1) * tq, tq), 2 * tq, True)

        @pl.when(qi % 2 == 0)
        def _():
            tile(pl.multiple_of(qi * tq, tq), tq, True)
        lane = lax.broadcasted_iota(jnp.int32, (tq, LANES), 1)
        first = lane < B_VDIM
        accs = [acc_ref[e] for e in range(2)]
        ls = [accs[e][:, sum_lane[e]:sum_lane[e] + 1] for e in range(2)]
        outs = [accs[e] / ls[e] for e in range(2)]
        lses = [m_ref[e] + jnp.log2(ls[e]) for e in range(2)]
        o = jnp.where(first, outs[0], outs[1])
        zv = z_ref[...].astype(F32)
        o_ref[...] = o.astype(BF16)
        y_ref[...] = (o * (zv * _sigmoid(zv))).astype(BF16)
        lse_ref[...] = jnp.where(first, lses[0], lses[1])
        for e in range(2):
            lrow_ref[e:e + 1, :] = jnp.transpose(lses[e])[0:1, :]

    blk = pl.BlockSpec((None, tq, LANES), lambda b, j, i: (b, i, j))
    return pl.pallas_call(
        body, name="mla_fwd", grid=(bl, pairs, nq),
        in_specs=[pl.BlockSpec((None, tq, 2 * LANES), lambda b, j, i: (b, i, j)),
                  pl.BlockSpec((None, t, 2 * LANES), lambda b, j, i: (b, 0, j)),
                  pl.BlockSpec((None, t, LANES), lambda b, j, i: (b, 0, v_blk0 + j)),
                  pl.BlockSpec((None, t, LANES), lambda b, j, i: (b, 0, 0)),
                  blk],
        out_specs=(blk, blk, blk, pl.BlockSpec((None, None, None, 2, tq), lambda b, j, i: (b, j, i, 0, 0))),
        out_shape=(jax.ShapeDtypeStruct((bl, t, B_WIDTH), BF16), jax.ShapeDtypeStruct((bl, t, B_WIDTH), BF16),
                   jax.ShapeDtypeStruct((bl, t, B_WIDTH), F32),
                   jax.ShapeDtypeStruct((bl, pairs, nq, 2, tq), F32)),
        scratch_shapes=[pltpu.VMEM((2, tq, LANES), F32), pltpu.VMEM((2, tq, LANES), F32)],
        compiler_params=_params(("parallel", "parallel", "arbitrary")),
    )(q_cat, kvup, kvup, kr, z)


def _mla_dq(q_cat, kvup, kr, do, o, lse, tabs, tq):
    bl, t, _ = q_cat.shape
    nq = t // tq
    pairs = B_HEADS // 2
    v_blk0 = (B_HEADS * LANES) // LANES

    def body(q_ref, k_ref, v_ref, kr_ref, do_ref, o_ref, lse_ref, tc, tsa, tsb, dq_ref, drow_ref, acc_ref):
        qi = pl.program_id(2)
        dov, ov, lsev = do_ref[...], o_ref[...], lse_ref[...]
        qs = [q_ref[:, e * LANES:(e + 1) * LANES] for e in range(2)]
        terms = [_head_terms(dov, ov, lsev, e) for e in range(2)]
        row = lax.broadcasted_iota(jnp.int32, (tq, tq), 0)
        col = lax.broadcasted_iota(jnp.int32, (tq, tq), 1)
        tri = col <= row
        for e in range(2):
            acc_ref[e] = jnp.zeros((tq, LANES), F32)

        def tile(k0, w, masked):
            krv = kr_ref[pl.ds(k0, w), :]
            v = v_ref[pl.ds(k0, w), :]
            ks = [k_ref[pl.ds(k0, w), e * LANES:(e + 1) * LANES] + krv for e in range(2)]
            ss = [_dot_nt(qs[e], ks[e]) for e in range(2)]
            dps = [_dot_nt(terms[e][0], v) for e in range(2)]
            for e in range(2):
                s = ss[e]
                if masked:
                    r = lax.broadcasted_iota(jnp.int32, (tq, w), 0)
                    c = lax.broadcasted_iota(jnp.int32, (tq, w), 1)
                    s = jnp.where(c <= r + (w - tq), s, NEG)
                p = jnp.exp2(s - terms[e][2])
                ds = (p * (dps[e] - terms[e][1])).astype(BF16)
                acc_ref[e] += _dot_nn(ds, ks[e])

        def step(kb2, carry):
            tile(pl.multiple_of(kb2 * 2 * tq, 2 * tq), 2 * tq, False)
            return carry

        lax.fori_loop(0, qi // 2, step, 0)

        @pl.when(qi % 2 == 1)
        def _():
            tile(pl.multiple_of((qi - 1) * tq, tq), 2 * tq, True)

        @pl.when(qi % 2 == 0)
        def _():
            tile(pl.multiple_of(qi * tq, tq), tq, True)

        for e in range(2):
            dq_ref[:, e * LANES:(e + 1) * LANES] = _rope_apply(acc_ref[e] * B_SCALE, tc[...], tsa[...], tsb[...], -1).astype(BF16)
            drow_ref[e:e + 1, :] = _col_to_row(terms[e][1], tq)

    blk = pl.BlockSpec((None, tq, LANES), lambda b, j, i: (b, i, j))
    tab = pl.BlockSpec((None, tq, LANES), lambda b, j, i: (b, i, 0))
    qblk = pl.BlockSpec((None, tq, 2 * LANES), lambda b, j, i: (b, i, j))
    return pl.pallas_call(
        body, name="mla_dq", grid=(bl, pairs, nq),
        in_specs=[qblk,
                  pl.BlockSpec((None, t, 2 * LANES), lambda b, j, i: (b, 0, j)),
                  pl.BlockSpec((None, t, LANES), lambda b, j, i: (b, 0, v_blk0 + j)),
                  pl.BlockSpec((None, t, LANES), lambda b, j, i: (b, 0, 0)),
                  blk, blk, blk, tab, tab, tab],
        out_specs=(qblk, pl.BlockSpec((None, None, None, 2, tq), lambda b, j, i: (b, j, i, 0, 0))),
        out_shape=(jax.ShapeDtypeStruct((bl, t, B_HEADS * LANES), BF16),
                   jax.ShapeDtypeStruct((bl, pairs, nq, 2, tq), F32)),
        scratch_shapes=[pltpu.VMEM((2, tq, LANES), F32)],
        compiler_params=_params(("parallel", "parallel", "arbitrary")),
    )(q_cat, kvup, kvup, kr, do, o, lse, *tabs)


def _mla_dkv(q_cat, kvup, kr, do, lse_rows, delta_rows, tq):
    bl, t, _ = q_cat.shape
    nq = t // tq
    pairs = B_HEADS // 2
    v_blk0 = (B_HEADS * LANES) // LANES

    def body(q_ref, k_ref, v_ref, kr_ref, do_ref, lrow_ref, drow_ref, dk_ref, dv_ref, acc_ref):
        kb = pl.program_id(2)
        v = v_ref[...]
        krv = kr_ref[...]
        ks = [k_ref[:, e * LANES:(e + 1) * LANES] + krv for e in range(2)]
        krow = lax.broadcasted_iota(jnp.int32, (tq, tq), 0)
        qcol = lax.broadcasted_iota(jnp.int32, (tq, tq), 1)
        tri = krow <= qcol
        lane = lax.broadcasted_iota(jnp.int32, (tq, LANES), 1)
        mine = [lane < B_VDIM, lane >= B_VDIM]

        for e in range(3):
            acc_ref[e] = jnp.zeros((tq, LANES), F32)

        def tile(qb, nblk, masked):
            w = nblk * tq
            rows = pl.ds(pl.multiple_of(qb * tq, tq), w)
            dov = do_ref[rows, :]
            lane_w = lax.broadcasted_iota(jnp.int32, (w, LANES), 1)
            mine_w = [lane_w < B_VDIM, lane_w >= B_VDIM]
            qs = [q_ref[rows, e * LANES:(e + 1) * LANES] for e in range(2)]
            does = [jnp.where(mine_w[e], dov, jnp.zeros_like(dov)) for e in range(2)]
            sts = [_dot_nt(ks[e], qs[e]) for e in range(2)]
            dpts = [_dot_nt(v, does[e]) for e in range(2)]

            def rows_of(ref, e):
                return jnp.concatenate([ref[qb + i, e:e + 1, :] for i in range(nblk)], axis=1)

            pts = []
            for e in range(2):
                st = sts[e]
                if masked:
                    r = lax.broadcasted_iota(jnp.int32, (tq, w), 0)
                    c = lax.broadcasted_iota(jnp.int32, (tq, w), 1)
                    st = jnp.where(r <= c, st, NEG)
                pts.append(jnp.exp2(st - rows_of(lrow_ref, e)))
            acc_ref[2] += _dot_nn(pts[0].astype(BF16), does[0]) + _dot_nn(pts[1].astype(BF16), does[1])
            for e in range(2):
                dst = (pts[e] * (dpts[e] - rows_of(drow_ref, e))).astype(BF16)
                acc_ref[e] += _dot_nn(dst, qs[e])

        rest = nq - 1 - kb
        odd = rest % 2

        @pl.when(odd == 1)
        def _():
            tile(kb, 2, True)

        @pl.when(odd == 0)
        def _():
            tile(kb, 1, True)

        def step(i, carry):
            tile(kb + 1 + odd + 2 * i, 2, False)
            return carry

        lax.fori_loop(0, rest // 2, step, 0)
        dk_ref[:, 0:LANES] = (acc_ref[0] * LN2).astype(BF16)
        dk_ref[:, LANES:2 * LANES] = (acc_ref[1] * LN2).astype(BF16)
        dv_ref[...] = acc_ref[2].astype(BF16)

    full = pl.BlockSpec((None, t, LANES), lambda b, j, i: (b, 0, j))
    rows = pl.BlockSpec((None, None, nq, 2, tq), lambda b, j, i: (b, j, 0, 0, 0))
    kblk = pl.BlockSpec((None, tq, 2 * LANES), lambda b, j, i: (b, i, j))
    return pl.pallas_call(
        body, name="mla_dkv", grid=(bl, pairs, nq),
        in_specs=[pl.BlockSpec((None, t, 2 * LANES), lambda b, j, i: (b, 0, j)),
                  kblk,
                  pl.BlockSpec((None, tq, LANES), lambda b, j, i: (b, i, v_blk0 + j)),
                  pl.BlockSpec((None, tq, LANES), lambda b, j, i: (b, i, 0)),
                  full, rows, rows],
        out_specs=(kblk, pl.BlockSpec((None, tq, LANES), lambda b, j, i: (b, i, j))),
        out_shape=(jax.ShapeDtypeStruct((bl, t, B_HEADS * LANES), BF16),
                   jax.ShapeDtypeStruct((bl, t, B_WIDTH), BF16)),
        scratch_shapes=[pltpu.VMEM((3, tq, LANES), F32)],
        compiler_params=_params(("parallel", "parallel", "arbitrary")),
    )(q_cat, kvup, kvup, kr, do, lse_rows, delta_rows)


def _adamw(w, g, m, v, *, name):
    r, c = w.shape
    tr = _row_tile(r, 256)
    c1 = 1.0 - ADAM_B1
    c2 = 1.0 - ADAM_B2
    bc1 = 1.0 - ADAM_B1 ** ADAM_STEP
    bc2 = 1.0 - ADAM_B2 ** ADAM_STEP

    def body(w_ref, g_ref, m_ref, v_ref, d_ref, nm_ref, nv_ref):
        gv = g_ref[...]
        nm = ADAM_B1 * m_ref[...] + c1 * gv
        nv = ADAM_B2 * v_ref[...] + c2 * (gv * gv)
        nm_ref[...] = nm
        nv_ref[...] = nv
        d_ref[...] = -ADAM_LR * ((nm / bc1) / (jnp.sqrt(nv / bc2) + ADAM_EPS) + ADAM_WD * w_ref[...])

    blk = pl.BlockSpec((tr, c), lambda i: (i, 0))
    sds = jax.ShapeDtypeStruct((r, c), F32)
    return pl.pallas_call(
        body, name=name, grid=(r // tr,), in_specs=[blk] * 4, out_specs=(blk,) * 3,
        out_shape=(sds,) * 3, compiler_params=_params(("parallel",)),
    )(w, g, m, v)


def _add_my_half(stacked, other, core, out_dtype, *, name):
    nch, a, c = stacked.shape
    h = a // 2
    tr = _row_tile(h, 256)
    nblk = h // tr

    def body(core_ref, s_ref, p_ref, o_ref):
        o_ref[...] = (s_ref[...] + p_ref[...]).astype(o_ref.dtype)

    return pl.pallas_call(
        body, name=name,
        grid_spec=pltpu.PrefetchScalarGridSpec(
            num_scalar_prefetch=1, grid=(nch, nblk),
            in_specs=[pl.BlockSpec((None, tr, c), lambda k, i, cr: (k, cr[0] * nblk + i, 0)),
                      pl.BlockSpec((None, tr, c), lambda k, i, cr: (k, i, 0))],
            out_specs=pl.BlockSpec((None, tr, c), lambda k, i, cr: (k, i, 0))),
        out_shape=jax.ShapeDtypeStruct((nch, h, c), out_dtype),
        compiler_params=_params(("parallel", "parallel")),
    )(core, stacked, other)


def _sum_chips(parts, own, chip, *, name):
    nch, h, c = parts.shape
    tr = _row_tile(h, 256)

    def body(chip_ref, p_ref, own_ref, o_ref):
        me = chip_ref[0]

        def slot(k):
            return jnp.where(me == k, own_ref[k].astype(F32), p_ref[k].astype(F32))

        acc = slot(0) + slot(1)
        for k in range(2, nch):
            acc = acc + slot(k)
        o_ref[...] = acc

    blk = pl.BlockSpec((nch, tr, c), lambda i, cr: (0, i, 0))
    return pl.pallas_call(
        body, name=name,
        grid_spec=pltpu.PrefetchScalarGridSpec(
            num_scalar_prefetch=1, grid=(h // tr,), in_specs=[blk, blk],
            out_specs=pl.BlockSpec((tr, c), lambda i, cr: (i, 0))),
        out_shape=jax.ShapeDtypeStruct((h, c), F32), compiler_params=_params(("parallel",)),
    )(chip, parts, own)


def _join_halves(mine, other, core, *, name):
    h, c = mine.shape
    tr = _row_tile(h, 256)
    nblk = h // tr

    def body(core_ref, m_ref, s_ref, o_ref):
        is_mine = pl.program_id(0) // nblk == core_ref[0]

        @pl.when(is_mine)
        def _():
            o_ref[...] = m_ref[...]

        @pl.when(jnp.logical_not(is_mine))
        def _():
            o_ref[...] = s_ref[...]

    blk = pl.BlockSpec((tr, c), lambda i, cr: (i % nblk, 0))
    return pl.pallas_call(
        body, name=name,
        grid_spec=pltpu.PrefetchScalarGridSpec(
            num_scalar_prefetch=1, grid=(2 * nblk,), in_specs=[blk, blk],
            out_specs=pl.BlockSpec((tr, c), lambda i, cr: (i, 0))),
        out_shape=jax.ShapeDtypeStruct((2 * h, c), F32), compiler_params=_params(("arbitrary",)),
    )(core, mine, other)


def _place():
    x, y, c = lax.axis_index("x"), lax.axis_index("y"), lax.axis_index("c")
    chips = [(1 - x, y), (x, 1 - y), (1 - x, 1 - y)]
    return x, y, c, chips


def _remote(src, dst, send_sems, recv_sems, k, to):
    return pltpu.make_async_remote_copy(src_ref=src, dst_ref=dst, send_sem=send_sems.at[k],
                                        recv_sem=recv_sems.at[k], device_id=to, device_id_type=MESH)


def _hbm_call(body, name, ins, out_shapes, n_remote):
    any_spec = pl.BlockSpec(memory_space=pl.ANY)
    return pl.pallas_call(
        body, name=name, in_specs=[any_spec] * len(ins), out_specs=tuple([any_spec] * len(out_shapes)),
        out_shape=tuple(out_shapes),
        scratch_shapes=[pltpu.SemaphoreType.DMA((n_remote,)), pltpu.SemaphoreType.DMA((n_remote,))],
    )(*ins)


def _all_gather_chips(shards, *, name):
    n = len(shards)

    def body(*refs):
        ins, outs = refs[:n], refs[n:2 * n]
        send_sems, recv_sems = refs[2 * n:]
        x, y, c, chips = _place()
        me = 2 * x + y
        sent = []
        for s in range(n):
            h = ins[s].shape[0] // 2
            for j, (px, py) in enumerate(chips):
                cp = _remote(ins[s].at[pl.ds(c * h, h)], outs[s].at[me, pl.ds(c * h, h)],
                             send_sems, recv_sems, s * 6 + j, (px, py, c))
                cp.start()
                sent.append(cp)
        for s in range(n):
            h = ins[s].shape[0] // 2
            for j, (px, py) in enumerate(chips):
                slab = outs[s].at[2 * px + py, pl.ds(c * h, h)]
                _remote(slab, slab, send_sems, recv_sems, s * 6 + j, (px, py, c)).wait_recv()
                cp = _remote(slab, slab, send_sems, recv_sems, s * 6 + 3 + j, (x, y, 1 - c))
                cp.start()
                sent.append(cp)
        for s in range(n):
            h = ins[s].shape[0] // 2
            for j, (px, py) in enumerate(chips):
                slab = outs[s].at[2 * px + py, pl.ds((1 - c) * h, h)]
                _remote(slab, slab, send_sems, recv_sems, s * 6 + 3 + j, (x, y, 1 - c)).wait_recv()
        for cp in sent:
            cp.wait_send()

    out_shapes = [jax.ShapeDtypeStruct((N_CHIPS,) + s.shape, s.dtype) for s in shards]
    return _hbm_call(body, name, shards, out_shapes, 6 * n)


def _pair_send_other_half(stacked, *, name):
    n = len(stacked)

    def body(*refs):
        ins, outs = refs[:n], refs[n:2 * n]
        send_sems, recv_sems = refs[2 * n:]
        x, y, c, _chips = _place()
        sent = []
        for s in range(n):
            h = ins[s].shape[1] // 2
            cp = _remote(ins[s].at[:, pl.ds((1 - c) * h, h)], outs[s], send_sems, recv_sems, s, (x, y, 1 - c))
            cp.start()
            sent.append(cp)
        for cp in sent:
            cp.wait_recv()
        for cp in sent:
            cp.wait_send()

    out_shapes = [jax.ShapeDtypeStruct((s.shape[0], s.shape[1] // 2, s.shape[2]), s.dtype) for s in stacked]
    return _hbm_call(body, name, stacked, out_shapes, n)


def _chip_exchange(halves, *, name):
    n = len(halves)

    def body(*refs):
        ins, outs = refs[:n], refs[n:2 * n]
        send_sems, recv_sems = refs[2 * n:]
        x, y, c, chips = _place()
        me = 2 * x + y
        sent = []
        for s in range(n):
            for j, (px, py) in enumerate(chips):
                cp = _remote(ins[s].at[2 * px + py], outs[s].at[me], send_sems, recv_sems, s * 3 + j, (px, py, c))
                cp.start()
                sent.append(cp)
        for s in range(n):
            for j, (px, py) in enumerate(chips):
                slab = outs[s].at[2 * px + py]
                _remote(slab, slab, send_sems, recv_sems, s * 3 + j, (px, py, c)).wait_recv()
        for cp in sent:
            cp.wait_send()

    out_shapes = [jax.ShapeDtypeStruct(s.shape, s.dtype) for s in halves]
    return _hbm_call(body, name, halves, out_shapes, 3 * n)


def _chip_exchange_start(halves, *, name):
    n = len(halves)
    hbm = pl.BlockSpec(memory_space=pltpu.HBM)
    sem = pl.BlockSpec(memory_space=pltpu.SEMAPHORE)

    def body(*refs):
        ins, lands = refs[:n], refs[n:2 * n]
        send_sems, recv_sems = refs[2 * n], refs[2 * n + 1]
        token = refs[-1]
        x, y, c, chips = _place()
        me = 2 * x + y
        for s in range(n):
            for j, (px, py) in enumerate(chips):
                _remote(ins[s].at[2 * px + py], lands[s].at[me], send_sems, recv_sems, s * 3 + j, (px, py, c)).start()
        token[...] = jnp.zeros_like(token)

    slabs = [pltpu.HBM(s.shape, s.dtype) for s in halves]
    outs = pl.pallas_call(
        body, name=name,
        out_shape=(pltpu.SemaphoreType.DMA((3 * n,)), pltpu.SemaphoreType.DMA((3 * n,)), *slabs, *slabs,
                   jax.ShapeDtypeStruct((8, LANES), F32)),
        in_specs=[hbm] * (2 * n), out_specs=(sem, sem, *([hbm] * (2 * n)), pl.BlockSpec(memory_space=pltpu.VMEM)),
        input_output_aliases={i: 2 + i for i in range(2 * n)},
        compiler_params=pltpu.CompilerParams(has_side_effects=pltpu.SideEffectType.DATAFLOW_SIDE_EFFECTING),
    )(*[pltpu.with_memory_space_constraint(s, pltpu.HBM) for s in halves],
      *[pltpu.with_memory_space_constraint(lax.empty(s.shape, s.dtype), pltpu.HBM) for s in halves])
    return outs[0], outs[1], list(outs[2:2 + n]), list(outs[2 + n:2 + 2 * n]), outs[-1]


def _chip_exchange_wait(send_sems, recv_sems, sent, lands, after, *, name):
    n = len(sent)
    hbm = pl.BlockSpec(memory_space=pltpu.HBM)
    sem = pl.BlockSpec(memory_space=pltpu.SEMAPHORE)

    def body(*refs):
        ins, lands_in = refs[:n], refs[n:2 * n]
        send_sems, recv_sems = refs[2 * n], refs[2 * n + 1]
        x, y, c, chips = _place()
        me = 2 * x + y
        for s in range(n):
            for j, (px, py) in enumerate(chips):
                k = 2 * px + py
                _remote(ins[s].at[k], lands_in[s].at[me], send_sems, recv_sems, s * 3 + j, (px, py, c)).wait_send()
                _remote(ins[s].at[k], lands_in[s].at[k], send_sems, recv_sems, s * 3 + j, (px, py, c)).wait_recv()

    slabs = [pltpu.HBM(s.shape, s.dtype) for s in sent]
    outs = pl.pallas_call(
        body, name=name, out_shape=(*slabs, *slabs),
        in_specs=[hbm] * (2 * n) + [sem, sem, pl.BlockSpec(memory_space=pl.ANY)],
        out_specs=tuple([hbm] * (2 * n)), input_output_aliases={i: i for i in range(2 * n)},
        compiler_params=pltpu.CompilerParams(has_side_effects=pltpu.SideEffectType.DATAFLOW_SIDE_EFFECTING),
    )(*sent, *lands, send_sems, recv_sems, after)
    return list(outs[n:])


def _pair_swap(halves, *, name):
    n = len(halves)

    def body(*refs):
        ins, outs = refs[:n], refs[n:2 * n]
        send_sems, recv_sems = refs[2 * n:]
        x, y, c, _chips = _place()
        sent = []
        for s in range(n):
            cp = _remote(ins[s], outs[s], send_sems, recv_sems, s, (x, y, 1 - c))
            cp.start()
            sent.append(cp)
        for cp in sent:
            cp.wait_recv()
        for cp in sent:
            cp.wait_send()

    out_shapes = [jax.ShapeDtypeStruct(s.shape, s.dtype) for s in halves]
    return _hbm_call(body, name, halves, out_shapes, n)


def _pack_rows(parts, row_multiple):
    flat = jnp.concatenate([p.reshape(-1) for p in parts])
    quantum = row_multiple * PACK_COLS
    pad = (-flat.shape[0]) % quantum
    flat = jnp.pad(flat, (0, pad))
    return flat.reshape(-1, PACK_COLS)


def _unpack(flat, shapes):
    out, pos = [], 0
    for shp in shapes:
        size = math.prod(shp)
        out.append(flat[pos:pos + size].reshape(shp))
        pos += size
    return out


def _to_chunks_cols(full):
    r, c4 = full.shape
    return full.reshape(r, N_CHIPS, c4 // N_CHIPS).transpose(1, 0, 2)


def _from_chunks_cols(stacked):
    nch, r, c = stacked.shape
    return stacked.transpose(1, 0, 2).reshape(r, nch * c)


def _class_major(a, bl, t, dil):
    w = a.shape[-1]
    if dil == 1:
        return a.reshape(bl, 1, t, w)
    return a.reshape(bl, t // dil, dil, w).transpose(0, 2, 1, 3)


def _natural(a):
    bl, dil, ln, w = a.shape
    if dil == 1:
        return a.reshape(bl * ln, w)
    return a.transpose(0, 2, 1, 3).reshape(bl * ln * dil, w)


def _train_step(x, positions, a_pre_norm, a_w_in, a_w_out, a_post_norm, kv_norm, kv_w_down, kv_latent_norm,
                kv_w_up, b_pre_norm, b_w_in, b_q_norm, b_w_q_up, b_w_out, b_post_norm, loss_target, moments):
    bl, t, d = x.shape
    n = bl * t
    qb = t // A_DILATIONS[-1]
    tq = _tile(t, 256)
    dq4 = d // N_CHIPS
    chip = 2 * lax.axis_index("x") + lax.axis_index("y")
    chip_arr = chip.astype(jnp.int32).reshape(1)
    core_arr = lax.axis_index("c").astype(jnp.int32).reshape(1)

    w_in_a_s = a_w_in[0].astype(BF16)
    outs_s = jnp.concatenate([a_w_out[0], b_w_out[0]], axis=0).astype(BF16)
    small_shapes = [kv_w_down.shape, kv_w_up.shape, b_w_in[0].shape, b_w_q_up[0].shape]
    small_s = _pack_rows([kv_w_down, kv_w_up, b_w_in[0], b_w_q_up[0]], 32).astype(BF16)
    gains_s = jnp.pad(jnp.concatenate([a_pre_norm[0], a_post_norm[0]]), (0, 16 * LANES - 2 * dq4)).reshape(16, LANES)
    shards = [w_in_a_s, outs_s, small_s, gains_s]
    gathered = _all_gather_chips(shards, name="gather_weights")
    g_in_a, g_outs, g_small, g_gains = [lax.dynamic_update_index_in_dim(g, s, chip, 0)
                                        for g, s in zip(gathered, shards)]

    w_in_a = _from_chunks_cols(g_in_a)
    w_out_a = g_outs[:, :A_WIDTH // N_CHIPS].reshape(A_WIDTH, d)
    w_out_b = g_outs[:, A_WIDTH // N_CHIPS:].reshape(B_WIDTH, d)
    sm = [_unpack(g_small[k].reshape(-1), small_shapes) for k in range(N_CHIPS)]
    w_down = jnp.concatenate([sm[k][0] for k in range(N_CHIPS)], axis=0)
    w_up = jnp.concatenate([sm[k][1] for k in range(N_CHIPS)], axis=1)
    w_in_b = jnp.concatenate([sm[k][2] for k in range(N_CHIPS)], axis=1)
    w_q_up = jnp.concatenate([sm[k][3] for k in range(N_CHIPS)], axis=1)
    gflat = g_gains.reshape(N_CHIPS, -1)
    g_a_pre = gflat[:, :dq4].reshape(1, d)
    g_a_post = gflat[:, dq4:2 * dq4].reshape(1, d)

    w_up_h = w_up.reshape(B_KV_LORA, B_HEADS, B_NOPE + B_VDIM)
    w_up_k = jnp.pad(w_up_h[:, :, :B_NOPE], ((0, 0), (0, 0), (0, LANES - B_NOPE))).reshape(B_KV_LORA, B_HEADS * LANES)
    w_up_v = w_up_h[:, :, B_NOPE:].reshape(B_KV_LORA, B_WIDTH)
    w_up_cat = jnp.concatenate([w_up_k, w_up_v], axis=1)
    w_q_up_p = jnp.pad(w_q_up.reshape(B_Q_LORA, B_HEADS, B_QK_DIM),
                       ((0, 0), (0, 0), (0, LANES - B_QK_DIM))).reshape(B_Q_LORA, B_HEADS * LANES)
    zeros_d = lambda c: jnp.zeros((d, c), BF16)
    w_down_p = jnp.concatenate([w_down[:, :B_KV_LORA], zeros_d(B_NOPE), w_down[:, B_KV_LORA:],
                                zeros_d(LANES - B_NOPE - B_ROPE)], axis=1)
    w_cq = w_in_b[:, :B_Q_LORA]
    w_z = w_in_b[:, B_Q_LORA:]

    tabs_a = _rope_tables(positions, A_ROPE_THETA, 0)
    tabs_b = _rope_tables(positions, B_ROPE_THETA, B_NOPE)

    h0 = x.reshape(n, d)
    hn_a = _rms_fwd(h0, g_a_pre, BF16, name="a_pre_norm", tr=1024)
    is_qk = lambda j: j != 2
    is_q = lambda j: j == 0
    z_blk_a = 3 * A_GROUPS
    z_a = _matmul(hn_a, w_in_a, "nn", BF16, name="a_proj_z", b_cols=(z_blk_a, 1))
    o_groups, lse_groups, qkv_cm, hn_cm, tabs_cm = [], [], [], [], []
    for g, dil in enumerate(A_DILATIONS):
        flat = lambda a: _class_major(a, bl, t, dil).reshape(n, a.shape[-1])
        hn_g = hn_a if dil == 1 else flat(hn_a)
        tabs_g = tabs_a if dil == 1 else lax.optimization_barrier(tuple(flat(tb) for tb in tabs_a))
        proj_g = _matmul(hn_g, w_in_a, "nn", BF16, name=f"a_proj_{g}", rope=(tabs_g, is_qk),
                         out_scale=(A_SCALE * LOG2E, is_q), b_cols=(3 * g, 3))
        src = proj_g.reshape(bl, dil, t // dil, 3 * A_WIDTH)
        hn_cm.append(hn_g)
        tabs_cm.append(tabs_g)
        qkv_cm.append(src)
        o_g, lse_g = _attn_a_fwd(src, 0, qb, BF16, name=f"attn_a_fwd_{g}")
        o_groups.append(_natural(o_g))
        lse_groups.append(_natural(lse_g))
    ypre_a, om_a, lse_a = _merge_gate_fwd(o_groups, lse_groups, z_a, 0)
    y_a = _matmul(ypre_a, w_out_a, "nn", F32, name="a_out")
    g_kvn = kv_norm.reshape(1, d)
    g_lat = kv_latent_norm.reshape(1, B_KV_LORA)
    h1, hn_kv, hn_b = _post_norm_block(y_a, g_a_post, h0, [g_kvn, b_pre_norm], name="a_post_norm")

    ckr = _matmul(hn_kv, w_down_p, "nn", F32, name="kv_down")
    c_kv, k_rope = _kv_latent_fwd(ckr, g_lat, tabs_b)
    kvup = _matmul(c_kv, w_up_cat, "nn", BF16, name="kv_up")
    z_b = _matmul(hn_b, w_z, "nn", BF16, name="b_proj_z")
    cq_raw = _matmul(hn_b, w_cq, "nn", F32, name="b_proj_q")
    c_q = _rms_fwd(cq_raw, b_q_norm, BF16, name="b_q_norm", tr=1024)
    always = lambda j: True
    q_cat = _matmul(c_q, w_q_up_p, "nn", BF16, name="b_q_up", rope=(tabs_b, always),
                    out_scale=(B_SCALE * LOG2E, always))
    r3 = lambda a: a.reshape(bl, t, a.shape[-1])
    tabs_b3 = tuple(r3(tb) for tb in tabs_b)
    ypre_b, o_b, lse_b, lse_rows_b = _mla_fwd(r3(q_cat), r3(kvup), r3(k_rope), r3(z_b), tq)
    y_b = _matmul(ypre_b.reshape(n, B_WIDTH), w_out_b, "nn", F32, name="b_out")
    dh2, loss_part = _post_norm_loss(y_b, b_post_norm, h1, loss_target.reshape(n, d))

    dy_b, dg_b_post = _rms_bwd(y_b, b_post_norm, dh2, BF16, name="b_post_norm_bwd", tr=1024)
    dypre_b = _matmul(dy_b, w_out_b, "nt", BF16, name="b_out_dx")
    dw_out_b = _matmul(ypre_b.reshape(n, B_WIDTH), dy_b, "tn", F32, name="b_out_dw", tm=1024, tk=2048)
    do_b, dz_b = _gate_bwd(dypre_b, o_b.reshape(n, B_WIDTH), z_b, 0, name="b_gate_bwd", with_delta=False)
    dq_cat, delta_rows_b = _mla_dq(r3(q_cat), r3(kvup), r3(k_rope), r3(do_b), o_b, lse_b, tabs_b3, tq)
    dq_cat = dq_cat.reshape(n, -1)
    dk_cat, dv_b = _mla_dkv(r3(q_cat), r3(kvup), r3(k_rope), r3(do_b), lse_rows_b, delta_rows_b, tq)
    dk_cat, dv_b = dk_cat.reshape(n, -1), dv_b.reshape(n, -1)
    dcq_n = _matmul(dq_cat, w_q_up_p, "nt", F32, name="b_q_up_dx")
    dw_q_up_p = _matmul(c_q, dq_cat, "tn", F32, name="b_q_up_dw", tm=1024, tk=2048)
    dcq, dg_b_q = _rms_bwd(cq_raw, b_q_norm, dcq_n, BF16, name="b_q_norm_bwd", tr=1024)
    dhn_b = _matmul(dz_b, w_z, "nt", F32, name="b_proj_z_dx")
    dhn_b = _matmul(dcq, w_cq, "nt", F32, name="b_proj_q_dx", add=dhn_b)
    dw_z = _matmul(hn_b, dz_b, "tn", F32, name="b_proj_z_dw", tm=1024, tk=2048)
    dw_cq = _matmul(hn_b, dcq, "tn", F32, name="b_proj_q_dw", tm=1024, tk=2048)
    dckv_n = _matmul(dk_cat, w_up_k, "nt", F32, name="kv_up_k_dx")
    dckv_n = _matmul(dv_b, w_up_v, "nt", F32, name="kv_up_v_dx", add=dckv_n)
    dw_up_k = _matmul(c_kv, dk_cat, "tn", F32, name="kv_up_k_dw", tm=1024, tk=2048)
    dw_up_v = _matmul(c_kv, dv_b, "tn", F32, name="kv_up_v_dw", tm=1024, tk=2048)
    dckr, dg_lat = _kv_latent_bwd(dckv_n, ckr, g_lat, dk_cat, tabs_b)
    dhn_kv = _matmul(dckr, w_down_p, "nt", F32, name="kv_down_dx")
    dw_down_p = _matmul(hn_kv, dckr, "tn", F32, name="kv_down_dw", tm=1024, tk=2048)
    dh1, dg_b_pre, dg_kvn = _rms_bwd_pair(h1, b_pre_norm, dhn_b, g_kvn, dhn_kv, dh2, name="h1_norms_bwd")

    dy_a, dg_a_post = _rms_bwd(y_a, g_a_post, dh1, BF16, name="a_post_norm_bwd", tr=1024)
    dypre_a = _matmul(dy_a, w_out_a, "nt", BF16, name="a_out_dx")
    dw_out_a = _matmul(ypre_a, dy_a, "tn", F32, name="a_out_dw", tm=1024, tk=2048)
    do_a, dz_a, delta_a = _gate_bwd(dypre_a, om_a, z_a, 0, name="a_gate_bwd", with_delta=True)
    dw_cols = A_IN_WIDTH // N_CHIPS
    dw_tn = _tile(dw_cols, 512)
    dw_kwargs = dict(tm=1024, tn=dw_tn, tk=4096, out_chunk_blocks=dw_cols // dw_tn)
    r_big = _matmul(hn_a, dz_a, "tn", F32, name="a_proj_dw_z", out_full=(N_CHIPS, d, dw_cols),
                    out_joff=z_blk_a * A_WIDTH // dw_tn, **dw_kwargs)
    dqkvs = []
    for g, dil in enumerate(A_DILATIONS):
        cm = lambda a: _class_major(a, bl, t, dil)
        swap = lambda a: jnp.swapaxes(a, 2, 3)
        lse_cm, delta_cm = cm(lse_a), cm(delta_a)
        tabs_g = tuple(tb.reshape(bl, dil, t // dil, LANES) for tb in tabs_cm[g])
        dqkv = _attn_a_bwd(qkv_cm[g], 0, cm(do_a), lse_cm, delta_cm, swap(lse_cm), swap(delta_cm),
                           tabs_g, qb, name=f"attn_a_bwd_{g}").reshape(n, 3 * A_WIDTH)
        dqkvs.append(dqkv)
        r_big = _matmul(hn_cm[g], dqkv, "tn", F32, name=f"a_proj_dw_{g}", out_into=r_big,
                        out_joff=3 * g * A_WIDTH // dw_tn, **dw_kwargs)
    r_outs = jnp.concatenate([dw_out_a.reshape(N_CHIPS, A_WIDTH // N_CHIPS, d),
                              dw_out_b.reshape(N_CHIPS, B_WIDTH // N_CHIPS, d)], axis=1)

    bulk = [r_big, r_outs]
    recv_b = _pair_send_other_half(bulk, name="reduce_pair_send")
    halves_b = [_add_my_half(s, p, core_arr, BF16, name=f"reduce_pair_add_{i}")
                for i, (s, p) in enumerate(zip(bulk, recv_b))]
    send_sems, recv_sems, sent_b, lands_b, token = _chip_exchange_start(halves_b, name="reduce_exchange_start")

    dhn_a = _matmul(dz_a, w_in_a, "nt", F32, name="a_proj_dx_z", b_koff=z_blk_a, after=token)
    dhn_more = []
    for g, dil in enumerate(A_DILATIONS):
        tk_dx = 3 * A_WIDTH
        if dil == 1:
            dhn_a = _matmul(dqkvs[g], w_in_a, "nt", F32, name=f"a_proj_dx_{g}", add=dhn_a, tk=tk_dx, b_koff=g,
                            after=token)
        else:
            part = _matmul(dqkvs[g], w_in_a, "nt", BF16, name=f"a_proj_dx_{g}", tk=tk_dx, b_koff=g, after=token)
            dhn_more.append(_natural(part.reshape(bl, dil, t // dil, d)))
    grad_x, dg_a_pre = _rms_bwd(h0, g_a_pre, dhn_a, F32, name="a_pre_norm_bwd", adds=(dh1,),
                                dy_more=tuple(dhn_more))

    dw_up = jnp.concatenate([dw_up_k.reshape(B_KV_LORA, B_HEADS, LANES)[:, :, :B_NOPE],
                             dw_up_v.reshape(B_KV_LORA, B_HEADS, B_VDIM)], axis=2).reshape(B_KV_LORA, -1)
    dw_q_up = dw_q_up_p.reshape(B_Q_LORA, B_HEADS, LANES)[:, :, :B_QK_DIM].reshape(B_Q_LORA, -1)
    dw_down = jnp.concatenate([dw_down_p[:, :B_KV_LORA], dw_down_p[:, B_KV_LORA + B_NOPE:B_KV_LORA + B_NOPE + B_ROPE]], axis=1)
    dw_in_b = jnp.concatenate([dw_cq, dw_z], axis=1)
    vec_rep = [dg_kvn.reshape(-1), dg_lat.reshape(-1), dg_b_pre.reshape(-1), dg_b_q.reshape(-1),
               dg_b_post.reshape(-1), loss_part.reshape(-1)]
    vec_shapes = [(dq4,), (dq4,)] + [v.shape for v in vec_rep]
    down_c = dw_down.reshape(N_CHIPS, dq4, -1)
    up_c = _to_chunks_cols(dw_up)
    inb_c = _to_chunks_cols(dw_in_b)
    qup_c = _to_chunks_cols(dw_q_up)
    small_chunks = []
    for k in range(N_CHIPS):
        vecs = [dg_a_pre.reshape(-1)[k * dq4:(k + 1) * dq4], dg_a_post.reshape(-1)[k * dq4:(k + 1) * dq4]] + vec_rep
        small_chunks.append(_pack_rows([down_c[k], up_c[k], inb_c[k], qup_c[k]] + vecs, 32))
    r_small = jnp.stack(small_chunks)

    recv_s = _pair_send_other_half([r_small], name="reduce_pair_send_small")
    halves_s = [_add_my_half(r_small, recv_s[0], core_arr, F32, name="reduce_pair_add_small")]
    parts_s = list(_chip_exchange(halves_s, name="reduce_exchange_small"))
    parts_b = _chip_exchange_wait(send_sems, recv_sems, sent_b, lands_b, grad_x, name="reduce_exchange_wait")
    sums = [_sum_chips(p, own, chip_arr, name=f"reduce_chip_sum_{i}")
            for i, (p, own) in enumerate(zip(parts_b + parts_s, sent_b + halves_s))]
    others = _pair_swap(sums, name="reduce_pair_swap")
    g_big, g_outs_r, g_small_r = [_join_halves(m, o, core_arr, name=f"reduce_join_{i}")
                                  for i, (m, o) in enumerate(zip(sums, others))]

    grads = {}
    grads["a_w_in"] = g_big
    grads["a_w_out"] = g_outs_r[:A_WIDTH // N_CHIPS]
    grads["b_w_out"] = g_outs_r[A_WIDTH // N_CHIPS:]
    small_out_shapes = [down_c.shape[1:], up_c.shape[1:], inb_c.shape[1:], qup_c.shape[1:]] + vec_shapes
    (grads["kv_w_down"], grads["kv_w_up"], grads["b_w_in"], grads["b_w_q_up"], grads["a_pre_norm"],
     grads["a_post_norm"], grads["kv_norm"], grads["kv_latent_norm"], grads["b_pre_norm"], grads["b_q_norm"],
     grads["b_post_norm"], loss_sum) = _unpack(g_small_r.reshape(-1), small_out_shapes)

    weights = dict(a_pre_norm=a_pre_norm, a_w_in=a_w_in, a_w_out=a_w_out, a_post_norm=a_post_norm, kv_norm=kv_norm,
                   kv_w_down=kv_w_down, kv_latent_norm=kv_latent_norm, kv_w_up=kv_w_up, b_pre_norm=b_pre_norm,
                   b_w_in=b_w_in, b_q_norm=b_q_norm, b_w_q_up=b_w_q_up, b_w_out=b_w_out, b_post_norm=b_post_norm)
    names = list(weights)
    out_g, out_d, out_m, out_v = [], [], [], []
    for i, nm in enumerate(names):
        w = weights[nm]
        two_d = (1, w.shape[0]) if w.ndim == 1 else (w.shape[-2], w.shape[-1])
        gw = grads[nm].reshape(two_d)
        dlt, new_m, new_v = _adamw(w.reshape(two_d), gw, moments[i].reshape(two_d),
                                   moments[len(names) + i].reshape(two_d), name=f"adamw_{nm}")
        out_g.append(gw.reshape(w.shape))
        out_d.append(dlt.reshape(w.shape))
        out_m.append(new_m.reshape(w.shape))
        out_v.append(new_v.reshape(w.shape))
    return (loss_sum.reshape(()), grad_x.reshape(bl, t, d), *out_g, *out_d, *out_m, *out_v)


def kernel(x, positions, a_pre_norm, a_w_in, a_w_out, a_post_norm, kv_norm, kv_w_down, kv_latent_norm, kv_w_up, b_pre_norm, b_w_in, b_q_norm, b_w_q_up, b_w_out, b_post_norm, loss_target, m_a_pre_norm, m_a_w_in, m_a_w_out, m_a_post_norm, m_kv_norm, m_kv_w_down, m_kv_latent_norm, m_kv_w_up, m_b_pre_norm, m_b_w_in, m_b_q_norm, m_b_w_q_up, m_b_w_out, m_b_post_norm, v_a_pre_norm, v_a_w_in, v_a_w_out, v_a_post_norm, v_kv_norm, v_kv_w_down, v_kv_latent_norm, v_kv_w_up, v_b_pre_norm, v_b_w_in, v_b_q_norm, v_b_w_q_up, v_b_w_out, v_b_post_norm):
    moments = (m_a_pre_norm, m_a_w_in, m_a_w_out, m_a_post_norm, m_kv_norm, m_kv_w_down, m_kv_latent_norm, m_kv_w_up,
               m_b_pre_norm, m_b_w_in, m_b_q_norm, m_b_w_q_up, m_b_w_out, m_b_post_norm,
               v_a_pre_norm, v_a_w_in, v_a_w_out, v_a_post_norm, v_kv_norm, v_kv_w_down, v_kv_latent_norm, v_kv_w_up,
               v_b_pre_norm, v_b_w_in, v_b_q_norm, v_b_w_q_up, v_b_w_out, v_b_post_norm)
    return _train_step(x, positions, a_pre_norm, a_w_in, a_w_out, a_post_norm, kv_norm, kv_w_down, kv_latent_norm,
                       kv_w_up, b_pre_norm, b_w_in, b_q_norm, b_w_q_up, b_w_out, b_post_norm, loss_target, moments)
```

```python
import math

import jax
import jax.numpy as jnp
from jax import lax
from jax.experimental import pallas as pl
from jax.experimental.pallas import tpu as pltpu

F32 = jnp.float32
BF16 = jnp.bfloat16
MESH = pl.DeviceIdType.MESH

NORM_EPS = 1e-6
NEG = -1e30
LANES = 128
VMEM_LIMIT = 56 * 1024 * 1024
LOG2E = math.log2(math.e)
LN2 = math.log(2.0)

A_GROUPS = 3
A_DILATIONS = (1, 4, 16)
A_HEADS = 8
A_HEAD_DIM = 128
A_WIDTH = A_HEADS * A_HEAD_DIM
A_ROPE_THETA = 500000.0
A_IN_WIDTH = A_GROUPS * 3 * A_WIDTH + A_WIDTH
A_SCALE = A_HEAD_DIM ** -0.5

B_HEADS = 16
B_NOPE = 64
B_ROPE = 32
B_QK_DIM = B_NOPE + B_ROPE
B_VDIM = 64
B_WIDTH = B_HEADS * B_VDIM
B_Q_LORA = 384
B_KV_LORA = 256
B_ROPE_THETA = 10000.0
B_SCALE = B_QK_DIM ** -0.5

ADAM_LR = 0.001
ADAM_B1 = 0.9
ADAM_B2 = 0.999
ADAM_EPS = 1e-08
ADAM_WD = 0.01
ADAM_STEP = 10

N_CHIPS = 4
PACK_COLS = 512


def _params(sem=None):
    return pltpu.CompilerParams(dimension_semantics=sem, vmem_limit_bytes=VMEM_LIMIT)


def _tile(n, want):
    t = min(n, want)
    assert n % t == 0, (n, want)
    return t


def _row_tile(n, want):
    for t in range(min(n, want), 0, -1):
        if n % t == 0 and (t % 16 == 0 or t == n):
            return t
    return n


def _rope_tables(positions, theta, lane0):
    half = 16
    inv_freq = 1.0 / (theta ** (jnp.arange(half, dtype=F32) * (2.0 / (2 * half))))
    n = positions.size
    per_row = LANES // half
    pos = jnp.repeat(positions.astype(F32).reshape(n // per_row, per_row), half, axis=1)
    ang = pos * jnp.tile(inv_freq, per_row)
    cos, sin = lax.optimization_barrier((jnp.cos(ang), jnp.sin(ang)))
    cos, sin = cos.reshape(n, half), sin.reshape(n, half)
    pre = jnp.zeros((n, lane0), F32)
    post = jnp.zeros((n, LANES - lane0 - 2 * half), F32)
    z16 = jnp.zeros((n, half), F32)
    c = jnp.concatenate([pre + 1.0, cos, cos, post + 1.0], axis=1)
    sa = jnp.concatenate([pre, -sin, z16, post], axis=1)
    sb = jnp.concatenate([pre, z16, sin, post], axis=1)
    return lax.optimization_barrier((c, sa, sb))


def _rope_apply(x, c, sa, sb, sign):
    k = x.shape[1] // LANES
    if k > 1:
        c, sa, sb = (jnp.concatenate([t] * k, axis=1) for t in (c, sa, sb))
    w = x.shape[1]
    up = pltpu.roll(x, w - 16, 1)
    dn = pltpu.roll(x, 16, 1)
    if sign > 0:
        return x * c + up * sa + dn * sb
    return x * c - up * sa - dn * sb


def _matmul(a, b, mode, out_dtype, *, name, tm=None, tn=1024, tk=None, add=None, rope=None,
            out_scale=None, b_koff=0, b_cols=None, out_into=None, out_full=None, out_joff=0,
            out_chunk_blocks=None, after=None):
    if mode == "nn":
        m, k = a.shape
        n = b.shape[1]
    elif mode == "nt":
        m, k = a.shape
        n = b.shape[0]
    else:
        k, m = a.shape
        n = b.shape[1]
    b_j0 = 0
    if b_cols is not None:
        tn = _tile(n, tn)
        b_j0, n = b_cols[0], b_cols[1] * tn
    if tm is None:
        if mode == "nt":
            tm = 512 if k > 2048 else 1024
        else:
            tm = 2048 if (k <= 512 and rope is None) else 1024
    if tk is None:
        tk = 3072 if mode == "nt" else 1024
    tm, tn, tk = _tile(m, tm), _tile(n, tn), _tile(k, tk)
    nk = k // tk
    if mode == "nn":
        a_spec = pl.BlockSpec((tm, tk), lambda j, i, kk: (i, kk))
        b_spec = pl.BlockSpec((tk, tn), lambda j, i, kk: (kk, j + b_j0))
        dims = (((1,), (0,)), ((), ()))
    elif mode == "nt":
        a_spec = pl.BlockSpec((tm, tk), lambda j, i, kk: (i, kk))
        b_spec = pl.BlockSpec((tn, tk), lambda j, i, kk: (j, kk + b_koff))
        dims = (((1,), (1,)), ((), ()))
    else:
        a_spec = pl.BlockSpec((tk, tm), lambda j, i, kk: (kk, i))
        b_spec = pl.BlockSpec((tk, tn), lambda j, i, kk: (kk, j))
        dims = (((0,), (0,)), ((), ()))
    operands = [a, b]
    in_specs = [a_spec, b_spec]
    if add is not None:
        operands.append(add)
        in_specs.append(pl.BlockSpec((tm, tn), lambda j, i, kk: (i, j)))
    if rope is not None:
        tables, rope_pred = rope
        for t in tables:
            operands.append(t)
            in_specs.append(pl.BlockSpec((tm, LANES), lambda j, i, kk: (i, 0)))
    aliases = {}
    if out_into is not None:
        aliases = {len(operands): 0}
        operands.append(out_into)
        in_specs.append(pl.BlockSpec(memory_space=pl.ANY))
        out_shape = jax.ShapeDtypeStruct(out_into.shape, out_into.dtype)
    elif out_full is not None:
        out_shape = jax.ShapeDtypeStruct(out_full, out_dtype)
    else:
        out_shape = jax.ShapeDtypeStruct((m, n), out_dtype)
    if after is not None:
        operands.append(after)
        in_specs.append(pl.BlockSpec(memory_space=pl.ANY))
    if out_chunk_blocks is not None:
        out_spec = pl.BlockSpec((None, tm, tn), lambda j, i, kk: ((j + out_joff) // out_chunk_blocks, i,
                                                                  (j + out_joff) % out_chunk_blocks))
    else:
        out_spec = pl.BlockSpec((tm, tn), lambda j, i, kk: (i, j + out_joff))

    def body(*refs):
        a_ref, b_ref = refs[0], refs[1]
        pos = 2
        add_ref = None
        if add is not None:
            add_ref = refs[pos]
            pos += 1
        tab_refs = None
        if rope is not None:
            tab_refs = refs[pos:pos + 3]
            pos += 3
        if out_into is not None:
            pos += 1
        if after is not None:
            pos += 1
        o_ref = refs[pos]
        acc_ref = refs[pos + 1] if nk > 1 else None

        def finish(res):
            if add_ref is not None:
                res = res + add_ref[...].astype(F32)
            if tab_refs is None:
                o_ref[...] = res.astype(o_ref.dtype)
                return
            j = pl.program_id(0)
            flag = rope_pred(j)

            roped = _rope_apply(res, tab_refs[0][...], tab_refs[1][...], tab_refs[2][...], 1)
            if out_scale is not None:
                value, scale_pred = out_scale
                use = scale_pred(j)
                roped = roped * (value if use is True else jnp.where(use, value, 1.0))
            if flag is True:
                o_ref[...] = roped.astype(o_ref.dtype)
                return

            @pl.when(flag)
            def _():
                o_ref[...] = roped.astype(o_ref.dtype)

            @pl.when(jnp.logical_not(flag))
            def _():
                o_ref[...] = res.astype(o_ref.dtype)

        part = lax.dot_general(a_ref[...].astype(BF16), b_ref[...].astype(BF16), dims,
                               preferred_element_type=F32)
        if nk == 1:
            finish(part)
            return
        kk = pl.program_id(2)

        @pl.when(kk == 0)
        def _():
            acc_ref[...] = part

        @pl.when(kk > 0)
        def _():
            acc_ref[...] += part

        @pl.when(kk == nk - 1)
        def _():
            finish(acc_ref[...])

    return pl.pallas_call(
        body, name=name, grid=(n // tn, m // tm, nk), in_specs=in_specs, out_specs=out_spec,
        out_shape=out_shape, input_output_aliases=aliases,
        scratch_shapes=[pltpu.VMEM((tm, tn), F32)] if nk > 1 else [],
        compiler_params=_params(("parallel", "parallel", "arbitrary")),
    )(*operands)


def _rms_fwd(x, g, out_dtype, *, name, add=None, tr=512):
    n, d = x.shape
    tr = _tile(n, tr)
    row = pl.BlockSpec((tr, d), lambda i: (i, 0))
    vec = pl.BlockSpec((1, d), lambda i: (0, 0))

    def body(*refs):
        x_ref, g_ref = refs[0], refs[1]
        o_ref = refs[-1]
        xv = x_ref[...].astype(F32)
        r = lax.rsqrt(jnp.mean(xv * xv, axis=-1, keepdims=True) + NORM_EPS)
        y = xv * r * g_ref[...]
        if add is not None:
            y = refs[2][...] + y
        o_ref[...] = y.astype(o_ref.dtype)

    ops = [x, g] + ([add] if add is not None else [])
    specs = [row, vec] + ([row] if add is not None else [])
    return pl.pallas_call(
        body, name=name, grid=(n // tr,), in_specs=specs, out_specs=row,
        out_shape=jax.ShapeDtypeStruct((n, d), out_dtype), compiler_params=_params(("parallel",)),
    )(*ops)


def _rms_bwd(x, g, dy, out_dtype, *, name, adds=(), dy_more=(), tr=512):
    n, d = x.shape
    tr = _tile(n, tr)
    steps = n // tr
    row = pl.BlockSpec((tr, d), lambda i: (i, 0))
    vec = pl.BlockSpec((1, d), lambda i: (0, 0))
    na = len(adds) + len(dy_more)

    def body(*refs):
        x_ref, g_ref, dy_ref = refs[:3]
        add_refs = refs[3:3 + len(adds)]
        more_refs = refs[3 + len(adds):3 + na]
        dx_ref, dg_ref, acc_ref = refs[3 + na:]
        i = pl.program_id(0)
        xv = x_ref[...].astype(F32)
        r = lax.rsqrt(jnp.mean(xv * xv, axis=-1, keepdims=True) + NORM_EPS)
        xh = xv * r
        dyv = dy_ref[...].astype(F32)
        for m_ref in more_refs:
            dyv = dyv + m_ref[...].astype(F32)
        part = (dyv * xh).reshape(tr // 8, 8, d).sum(axis=0)

        @pl.when(i == 0)
        def _():
            acc_ref[...] = part

        @pl.when(i > 0)
        def _():
            acc_ref[...] += part

        t = dyv * g_ref[...]
        dx = r * (t - xh * jnp.mean(t * xh, axis=-1, keepdims=True))
        for a_ref in add_refs:
            dx = dx + a_ref[...].astype(F32)
        dx_ref[...] = dx.astype(dx_ref.dtype)

        @pl.when(i == steps - 1)
        def _():
            dg_ref[...] = jnp.sum(acc_ref[...], axis=0, keepdims=True)

    return pl.pallas_call(
        body, name=name, grid=(steps,), in_specs=[row, vec, row] + [row] * na,
        out_specs=(row, vec),
        out_shape=(jax.ShapeDtypeStruct((n, d), out_dtype), jax.ShapeDtypeStruct((1, d), F32)),
        scratch_shapes=[pltpu.VMEM((8, d), F32)], compiler_params=_params(("arbitrary",)),
    )(x, g, dy, *adds, *dy_more)


def _rms(xv, g):
    return xv * lax.rsqrt(jnp.mean(xv * xv, axis=-1, keepdims=True) + NORM_EPS) * g


def _post_norm_block(y, g, h_in, next_gains, *, name, tr=1024):
    n, d = y.shape
    tr = _tile(n, tr)
    nk = len(next_gains)
    row = pl.BlockSpec((tr, d), lambda i: (i, 0))
    vec = pl.BlockSpec((1, d), lambda i: (0, 0))

    def body(*refs):
        y_ref, g_ref, h_ref = refs[:3]
        gk_refs = refs[3:3 + nk]
        o_ref = refs[3 + nk]
        hn_refs = refs[4 + nk:]
        h = h_ref[...] + _rms(y_ref[...], g_ref[...])
        o_ref[...] = h
        for gk_ref, hn_ref in zip(gk_refs, hn_refs):
            hn_ref[...] = _rms(h, gk_ref[...]).astype(BF16)

    return pl.pallas_call(
        body, name=name, grid=(n // tr,), in_specs=[row, vec, row] + [vec] * nk,
        out_specs=(row,) * (1 + nk),
        out_shape=(jax.ShapeDtypeStruct((n, d), F32),) + (jax.ShapeDtypeStruct((n, d), BF16),) * nk,
        compiler_params=_params(("parallel",)),
    )(y, g, h_in, *next_gains)


def _post_norm_loss(y, g, h_in, target, *, tr=1024):
    n, d = y.shape
    tr = _tile(n, tr)
    steps = n // tr
    row = pl.BlockSpec((tr, d), lambda i: (i, 0))

    def body(y_ref, g_ref, h_ref, t_ref, dh_ref, loss_ref, acc_ref):
        i = pl.program_id(0)
        e = h_ref[...] + _rms(y_ref[...], g_ref[...]) - t_ref[...]
        dh_ref[...] = e / d
        part = (e * e).reshape(tr // 8, 8, d).sum(axis=0)

        @pl.when(i == 0)
        def _():
            acc_ref[...] = part

        @pl.when(i > 0)
        def _():
            acc_ref[...] += part

        @pl.when(i == steps - 1)
        def _():
            s = jnp.sum(jnp.sum(acc_ref[...], axis=-1, keepdims=True), axis=0, keepdims=True)
            loss_ref[...] = 0.5 * s / d

    return pl.pallas_call(
        body, name="b_post_norm_loss", grid=(steps,),
        in_specs=[row, pl.BlockSpec((1, d), lambda i: (0, 0)), row, row],
        out_specs=(row, pl.BlockSpec((1, 1), lambda i: (0, 0))),
        out_shape=(jax.ShapeDtypeStruct((n, d), F32), jax.ShapeDtypeStruct((1, 1), F32)),
        scratch_shapes=[pltpu.VMEM((8, d), F32)], compiler_params=_params(("arbitrary",)),
    )(y, g, h_in, target)


def _rms_bwd_pair(x, g1, dy1, g2, dy2, add, *, name, tr=512):
    n, d = x.shape
    tr = _tile(n, tr)
    steps = n // tr
    row = pl.BlockSpec((tr, d), lambda i: (i, 0))
    vec = pl.BlockSpec((1, d), lambda i: (0, 0))

    def body(x_ref, g1_ref, d1_ref, g2_ref, d2_ref, add_ref, dx_ref, dg1_ref, dg2_ref, acc_ref):
        i = pl.program_id(0)
        xv = x_ref[...]
        r = lax.rsqrt(jnp.mean(xv * xv, axis=-1, keepdims=True) + NORM_EPS)
        xh = xv * r
        dx = add_ref[...]
        for k, (g_ref, d_ref) in enumerate(((g1_ref, d1_ref), (g2_ref, d2_ref))):
            dyv = d_ref[...].astype(F32)
            part = (dyv * xh).reshape(tr // 8, 8, d).sum(axis=0)

            @pl.when(i == 0)
            def _(part=part, k=k):
                acc_ref[k] = part

            @pl.when(i > 0)
            def _(part=part, k=k):
                acc_ref[k] += part

            t = dyv * g_ref[...]
            dx = dx + r * (t - xh * jnp.mean(t * xh, axis=-1, keepdims=True))
        dx_ref[...] = dx

        @pl.when(i == steps - 1)
        def _():
            dg1_ref[...] = jnp.sum(acc_ref[0], axis=0, keepdims=True)
            dg2_ref[...] = jnp.sum(acc_ref[1], axis=0, keepdims=True)

    return pl.pallas_call(
        body, name=name, grid=(steps,), in_specs=[row, vec, row, vec, row, row],
        out_specs=(row, vec, vec),
        out_shape=(jax.ShapeDtypeStruct((n, d), F32), jax.ShapeDtypeStruct((1, d), F32),
                   jax.ShapeDtypeStruct((1, d), F32)),
        scratch_shapes=[pltpu.VMEM((2, 8, d), F32)], compiler_params=_params(("arbitrary",)),
    )(x, g1, dy1, g2, dy2, add)


def _kv_latent_fwd(ckr, g_lat, tabs, *, tr=512):
    n = ckr.shape[0]
    tr = _tile(n, tr)
    lat = B_KV_LORA

    def body(c_ref, k_ref, g_ref, tc, tsa, tsb, ckv_ref, kr_ref):
        xv = c_ref[...]
        r = lax.rsqrt(jnp.mean(xv * xv, axis=-1, keepdims=True) + NORM_EPS)
        ckv_ref[...] = (xv * r * g_ref[...]).astype(BF16)
        kr_ref[...] = _rope_apply(k_ref[...], tc[...], tsa[...], tsb[...], 1).astype(BF16)

    tab = pl.BlockSpec((tr, LANES), lambda i: (i, 0))
    return pl.pallas_call(
        body, name="kv_latent_fwd", grid=(n // tr,),
        in_specs=[pl.BlockSpec((tr, lat), lambda i: (i, 0)),
                  pl.BlockSpec((tr, LANES), lambda i: (i, lat // LANES)),
                  pl.BlockSpec((1, lat), lambda i: (0, 0)), tab, tab, tab],
        out_specs=(pl.BlockSpec((tr, lat), lambda i: (i, 0)), tab),
        out_shape=(jax.ShapeDtypeStruct((n, lat), BF16), jax.ShapeDtypeStruct((n, LANES), BF16)),
        compiler_params=_params(("parallel",)),
    )(ckr, ckr, g_lat, *tabs)


def _kv_latent_bwd(dckv, ckr, g_lat, dk_cat, tabs, *, tr=512):
    n = ckr.shape[0]
    tr = _tile(n, tr)
    steps = n // tr
    lat = B_KV_LORA
    wk = dk_cat.shape[1]

    def body(d_ref, c_ref, g_ref, dk_ref, tc, tsa, tsb, o_ref, dg_ref, acc_ref):
        i = pl.program_id(0)
        xv = c_ref[...]
        r = lax.rsqrt(jnp.mean(xv * xv, axis=-1, keepdims=True) + NORM_EPS)
        xh = xv * r
        dyv = d_ref[...]
        part = (dyv * xh).reshape(tr // 8, 8, lat).sum(axis=0)

        @pl.when(i == 0)
        def _():
            acc_ref[...] = part

        @pl.when(i > 0)
        def _():
            acc_ref[...] += part

        t = dyv * g_ref[...]
        dx = r * (t - xh * jnp.mean(t * xh, axis=-1, keepdims=True))
        o_ref[:, 0:lat] = dx.astype(o_ref.dtype)
        dkr = dk_ref[:, 0:LANES].astype(F32)
        for h in range(1, wk // LANES):
            dkr = dkr + dk_ref[:, h * LANES:(h + 1) * LANES].astype(F32)
        o_ref[:, lat:lat + LANES] = _rope_apply(dkr, tc[...], tsa[...], tsb[...], -1).astype(o_ref.dtype)

        @pl.when(i == steps - 1)
        def _():
            dg_ref[...] = jnp.sum(acc_ref[...], axis=0, keepdims=True)

    tab = pl.BlockSpec((tr, LANES), lambda i: (i, 0))
    return pl.pallas_call(
        body, name="kv_latent_bwd", grid=(steps,),
        in_specs=[pl.BlockSpec((tr, lat), lambda i: (i, 0)), pl.BlockSpec((tr, lat), lambda i: (i, 0)),
                  pl.BlockSpec((1, lat), lambda i: (0, 0)), pl.BlockSpec((tr, wk), lambda i: (i, 0)),
                  tab, tab, tab],
        out_specs=(pl.BlockSpec((tr, lat + LANES), lambda i: (i, 0)), pl.BlockSpec((1, lat), lambda i: (0, 0))),
        out_shape=(jax.ShapeDtypeStruct((n, lat + LANES), BF16), jax.ShapeDtypeStruct((1, lat), F32)),
        scratch_shapes=[pltpu.VMEM((8, lat), F32)], compiler_params=_params(("arbitrary",)),
    )(dckv, ckr, g_lat, dk_cat, *tabs)


def _sigmoid(z):
    return 1.0 / (1.0 + jnp.exp(-z))


def _lane_place(cols, width):
    rows = cols[0].shape[0]
    lane = lax.broadcasted_iota(jnp.int32, (rows, width), 1)
    out = jnp.zeros((rows, width), F32)
    for h, col in enumerate(cols):
        out = jnp.where(lane == h, col, out)
    return out


def _merge_gate_fwd(outs, lses, proj, z_block, *, tr=1024):
    n, w = outs[0].shape
    tr = _tile(n, tr)
    ng = len(outs)

    def body(*refs):
        o_refs = refs[:ng]
        l_refs = refs[ng:2 * ng]
        z_ref = refs[2 * ng]
        y_ref, om_ref, lse_ref = refs[2 * ng + 1:]
        ls = [r[...] for r in l_refs]
        mx = ls[0]
        for l in ls[1:]:
            mx = jnp.maximum(mx, l)
        ssum = jnp.exp2(ls[0] - mx)
        for l in ls[1:]:
            ssum = ssum + jnp.exp2(l - mx)
        tot = mx + jnp.log2(ssum)
        lse_ref[...] = tot
        ws = [jnp.exp2(l - tot) for l in ls]
        for h in range(A_HEADS):
            sl = slice(h * A_HEAD_DIM, (h + 1) * A_HEAD_DIM)
            o = ws[0][:, h:h + 1] * o_refs[0][:, sl]
            for gi in range(1, ng):
                o = o + ws[gi][:, h:h + 1] * o_refs[gi][:, sl]
            z = z_ref[:, sl].astype(F32)
            om_ref[:, sl] = o.astype(BF16)
            y_ref[:, sl] = (o * (z * _sigmoid(z))).astype(BF16)

    row = pl.BlockSpec((tr, w), lambda i: (i, 0))
    lrow = pl.BlockSpec((tr, A_HEADS), lambda i: (i, 0))
    return pl.pallas_call(
        body, name="merge_gate_fwd", grid=(n // tr,),
        in_specs=[row] * ng + [lrow] * ng + [pl.BlockSpec((tr, w), lambda i: (i, z_block))],
        out_specs=(row, row, lrow),
        out_shape=(jax.ShapeDtypeStruct((n, w), BF16), jax.ShapeDtypeStruct((n, w), BF16),
                   jax.ShapeDtypeStruct((n, A_HEADS), F32)),
        compiler_params=_params(("parallel",)),
    )(*outs, *lses, proj)


def _gate_bwd(dy, o, z_arr, z_block, *, name, with_delta, tr=512):
    n, w = dy.shape
    tr = _tile(n, tr)

    def body(*refs):
        dy_ref, o_ref, z_ref, do_ref, dz_ref = refs[:5]
        dyv = dy_ref[...].astype(F32)
        ov = o_ref[...].astype(F32)
        z = z_ref[...].astype(F32)
        sig = _sigmoid(z)
        do = dyv * (z * sig)
        do_ref[...] = do.astype(BF16)
        dz_ref[...] = (dyv * ov * (sig * (1.0 + z * (1.0 - sig)))).astype(BF16)
        if with_delta:
            prod = do * ov
            cols = [jnp.sum(prod[:, h * A_HEAD_DIM:(h + 1) * A_HEAD_DIM], axis=-1, keepdims=True)
                    for h in range(A_HEADS)]
            refs[5][...] = _lane_place(cols, A_HEADS)

    row = pl.BlockSpec((tr, w), lambda i: (i, 0))
    out_specs = [row, row]
    out_shape = [jax.ShapeDtypeStruct((n, w), BF16), jax.ShapeDtypeStruct((n, w), BF16)]
    if with_delta:
        out_specs.append(pl.BlockSpec((tr, A_HEADS), lambda i: (i, 0)))
        out_shape.append(jax.ShapeDtypeStruct((n, A_HEADS), F32))
    return pl.pallas_call(
        body, name=name, grid=(n // tr,),
        in_specs=[row, row, pl.BlockSpec((tr, w), lambda i: (i, z_block))],
        out_specs=tuple(out_specs), out_shape=tuple(out_shape), compiler_params=_params(("parallel",)),
    )(dy, o, z_arr)


def _dot_nt(a, b):
    return lax.dot_general(a, b, (((1,), (1,)), ((), ())), preferred_element_type=F32)


def _dot_nn(a, b):
    return lax.dot_general(a, b, (((1,), (0,)), ((), ())), preferred_element_type=F32)


def _attn_a_fwd(qkv, cb0, qb, out_dtype, *, name):
    bl, dil, ln, _ = qkv.shape
    nb = ln // qb
    hw = A_WIDTH
    heads = range(A_HEADS)
    sls = [slice(h * A_HEAD_DIM, (h + 1) * A_HEAD_DIM) for h in heads]

    def body(*refs):
        if nb > 1:
            q_ref, kc_ref, vc_ref, kp_ref, vp_ref, o_ref, lse_ref = refs
        else:
            q_ref, kc_ref, vc_ref, o_ref, lse_ref = refs
        i = pl.program_id(2)
        qi = lax.broadcasted_iota(jnp.int32, (qb, qb), 0)
        ki = lax.broadcasted_iota(jnp.int32, (qb, qb), 1)
        mask_c = ki <= qi
        mask_p = jnp.logical_and(ki >= qi, i >= 1)
        s_c = [jnp.where(mask_c, _dot_nt(q_ref[:, sls[h]], kc_ref[:, sls[h]]), NEG) for h in heads]
        m = [jnp.max(s_c[h], axis=-1, keepdims=True) for h in heads]
        if nb > 1:
            s_p = [jnp.where(mask_p, _dot_nt(q_ref[:, sls[h]], kp_ref[:, sls[h]]), NEG) for h in heads]
            m = [jnp.maximum(m[h], jnp.max(s_p[h], axis=-1, keepdims=True)) for h in heads]
        p_c = [jnp.exp2(s_c[h] - m[h]) for h in heads]
        l = [jnp.sum(p_c[h], axis=-1, keepdims=True) for h in heads]
        acc = [_dot_nn(p_c[h].astype(BF16), vc_ref[:, sls[h]]) for h in heads]
        if nb > 1:
            p_p = [jnp.exp2(s_p[h] - m[h]) for h in heads]
            l = [l[h] + jnp.sum(p_p[h], axis=-1, keepdims=True) for h in heads]
            acc = [acc[h] + _dot_nn(p_p[h].astype(BF16), vp_ref[:, sls[h]]) for h in heads]
        for h in heads:
            o_ref[:, sls[h]] = (acc[h] / l[h]).astype(o_ref.dtype)
        lse_ref[...] = _lane_place([m[h] + jnp.log2(l[h]) for h in heads], A_HEADS)

    def spec(off, prev):
        if prev:
            return pl.BlockSpec((None, None, qb, hw), lambda b, r, i: (b, r, jnp.maximum(i - 1, 0), cb0 + off))
        return pl.BlockSpec((None, None, qb, hw), lambda b, r, i: (b, r, i, cb0 + off))

    return pl.pallas_call(
        body, name=name, grid=(bl, dil, nb),
        in_specs=[spec(0, False), spec(1, False), spec(2, False)] + ([spec(1, True), spec(2, True)] if nb > 1 else []),
        out_specs=(pl.BlockSpec((None, None, qb, hw), lambda b, r, i: (b, r, i, 0)),
                   pl.BlockSpec((None, None, qb, A_HEADS), lambda b, r, i: (b, r, i, 0))),
        out_shape=(jax.ShapeDtypeStruct((bl, dil, ln, hw), out_dtype),
                   jax.ShapeDtypeStruct((bl, dil, ln, A_HEADS), F32)),
        compiler_params=_params(("parallel", "parallel", "arbitrary")),
    )(*([qkv] * (5 if nb > 1 else 3)))


def _attn_a_bwd(qkv, cb0, do, lse, delta, lse_t, delta_t, tabs, qb, *, name):
    bl, dil, ln, _ = qkv.shape
    nb = ln // qb
    hw = A_WIDTH

    def body(*refs):
        if nb > 1:
            (q_ref, kc_ref, vc_ref, do_ref, lse_ref, dl_ref, lt_ref, dt_ref, tc, tsa, tsb,
             qn_ref, kp_ref, vp_ref, don_ref, ltn_ref, dtn_ref, o_ref) = refs
        else:
            q_ref, kc_ref, vc_ref, do_ref, lse_ref, dl_ref, lt_ref, dt_ref, tc, tsa, tsb, o_ref = refs
        i = pl.program_id(2)
        row = lax.broadcasted_iota(jnp.int32, (qb, qb), 0)
        col = lax.broadcasted_iota(jnp.int32, (qb, qb), 1)
        m_qc = col <= row
        m_kc = row <= col
        m_qp = jnp.logical_and(col >= row, i >= 1)
        m_kn = jnp.logical_and(row >= col, i + 1 < nb)
        c, sa, sb = tc[...], tsa[...], tsb[...]
        heads = range(A_HEADS)
        sls = [slice(h * A_HEAD_DIM, (h + 1) * A_HEAD_DIM) for h in heads]
        q, kc = [q_ref[:, sl] for sl in sls], [kc_ref[:, sl] for sl in sls]
        vc, dov = [vc_ref[:, sl] for sl in sls], [do_ref[:, sl] for sl in sls]
        lse_c = [lse_ref[:, h:h + 1] for h in heads]
        dl_c = [dl_ref[:, h:h + 1] for h in heads]
        s = [_dot_nt(q[h], kc[h]) for h in heads]
        st = [_dot_nt(kc[h], q[h]) for h in heads]
        dp = [_dot_nt(dov[h], vc[h]) for h in heads]
        dpt = [_dot_nt(vc[h], dov[h]) for h in heads]
        p = [jnp.exp2(jnp.where(m_qc, s[h], NEG) - lse_c[h]) for h in heads]
        pt = [jnp.exp2(jnp.where(m_kc, st[h], NEG) - lt_ref[h:h + 1, :]) for h in heads]
        dq = [_dot_nn((p[h] * (dp[h] - dl_c[h])).astype(BF16), kc[h]) for h in heads]
        dk = [_dot_nn((pt[h] * (dpt[h] - dt_ref[h:h + 1, :])).astype(BF16), q[h]) for h in heads]
        dv = [_dot_nn(pt[h].astype(BF16), dov[h]) for h in heads]
        if nb > 1:
            kp, vp = [kp_ref[:, sl] for sl in sls], [vp_ref[:, sl] for sl in sls]
            qn, don = [qn_ref[:, sl] for sl in sls], [don_ref[:, sl] for sl in sls]
            s = [_dot_nt(q[h], kp[h]) for h in heads]
            st = [_dot_nt(kc[h], qn[h]) for h in heads]
            dp = [_dot_nt(dov[h], vp[h]) for h in heads]
            dpt = [_dot_nt(vc[h], don[h]) for h in heads]
            p = [jnp.exp2(jnp.where(m_qp, s[h], NEG) - lse_c[h]) for h in heads]
            pt = [jnp.exp2(jnp.where(m_kn, st[h], NEG) - ltn_ref[h:h + 1, :]) for h in heads]
            dq = [dq[h] + _dot_nn((p[h] * (dp[h] - dl_c[h])).astype(BF16), kp[h]) for h in heads]
            dk = [dk[h] + _dot_nn((pt[h] * (dpt[h] - dtn_ref[h:h + 1, :])).astype(BF16), qn[h]) for h in heads]
            dv = [dv[h] + _dot_nn(pt[h].astype(BF16), don[h]) for h in heads]
        for h in heads:
            o_ref[:, h * A_HEAD_DIM:(h + 1) * A_HEAD_DIM] = _rope_apply(dq[h] * A_SCALE, c, sa, sb, -1).astype(BF16)
            o_ref[:, hw + h * A_HEAD_DIM:hw + (h + 1) * A_HEAD_DIM] = _rope_apply(dk[h] * LN2, c, sa, sb, -1).astype(BF16)
            o_ref[:, 2 * hw + h * A_HEAD_DIM:2 * hw + (h + 1) * A_HEAD_DIM] = dv[h].astype(BF16)

    def cur(w, col):
        return pl.BlockSpec((None, None, qb, w), lambda b, r, i: (b, r, i, col))

    def prev(w, col):
        return pl.BlockSpec((None, None, qb, w), lambda b, r, i: (b, r, jnp.maximum(i - 1, 0), col))

    def nxt(w, col):
        return pl.BlockSpec((None, None, qb, w), lambda b, r, i: (b, r, jnp.minimum(i + 1, nb - 1), col))

    t_cur = pl.BlockSpec((None, None, A_HEADS, qb), lambda b, r, i: (b, r, 0, i))
    t_nxt = pl.BlockSpec((None, None, A_HEADS, qb), lambda b, r, i: (b, r, 0, jnp.minimum(i + 1, nb - 1)))
    in_specs = [cur(hw, cb0), cur(hw, cb0 + 1), cur(hw, cb0 + 2), cur(hw, 0), cur(A_HEADS, 0), cur(A_HEADS, 0),
                t_cur, t_cur, cur(LANES, 0), cur(LANES, 0), cur(LANES, 0)]
    operands = [qkv, qkv, qkv, do, lse, delta, lse_t, delta_t, *tabs]
    if nb > 1:
        in_specs += [nxt(hw, cb0), prev(hw, cb0 + 1), prev(hw, cb0 + 2), nxt(hw, 0), t_nxt, t_nxt]
        operands += [qkv, qkv, qkv, do, lse_t, delta_t]
    return pl.pallas_call(
        body, name=name, grid=(bl, dil, nb), in_specs=in_specs, out_specs=cur(3 * hw, 0),
        out_shape=jax.ShapeDtypeStruct((bl, dil, ln, 3 * hw), BF16),
        compiler_params=_params(("parallel", "parallel", "arbitrary")),
    )(*operands)


def _head_terms(do, o, lse, e):
    rows = do.shape[0]
    lane = lax.broadcasted_iota(jnp.int32, (rows, LANES), 1)
    mine = (lane < B_VDIM) if e == 0 else (lane >= B_VDIM)
    prod = do.astype(F32) * o.astype(F32)
    dl = jnp.sum(jnp.where(mine, prod, 0.0), axis=-1, keepdims=True)
    do_e = jnp.where(mine, do, jnp.zeros_like(do))
    return do_e, dl, lse[:, e * B_VDIM:e * B_VDIM + 1]


def _col_to_row(col, rows):
    return jnp.transpose(jnp.broadcast_to(col, (rows, LANES)))[0:1, :]


def _mla_fwd(q_cat, kvup, kr, z, tq):
    bl, t, _ = q_cat.shape
    nq = t // tq
    pairs = B_HEADS // 2
    v_blk0 = (B_HEADS * LANES) // LANES

    def body(q_ref, k_ref, v_ref, kr_ref, z_ref, y_ref, o_ref, lse_ref, lrow_ref, m_ref, acc_ref):
        qi = pl.program_id(2)
        qs = [q_ref[:, e * LANES:(e + 1) * LANES] for e in range(2)]
        row = lax.broadcasted_iota(jnp.int32, (tq, tq), 0)
        col = lax.broadcasted_iota(jnp.int32, (tq, tq), 1)
        tri = col <= row
        sum_lane = [B_VDIM, 0]

        for e in range(2):
            m_ref[e] = jnp.full((tq, LANES), NEG, F32)
            acc_ref[e] = jnp.zeros((tq, LANES), F32)

        def tile(k0, w, masked):
            lane = lax.broadcasted_iota(jnp.int32, (w, LANES), 1)
            first = lane < B_VDIM
            krv = kr_ref[pl.ds(k0, w), :]
            v = v_ref[pl.ds(k0, w), :]
            vs = [jnp.where(first, v, jnp.where(lane == B_VDIM, 1.0, 0.0).astype(BF16)),
                  jnp.where(first, jnp.where(lane == 0, 1.0, 0.0).astype(BF16), v)]
            ss = []
            for e in range(2):
                k = k_ref[pl.ds(k0, w), e * LANES:(e + 1) * LANES] + krv
                s = _dot_nt(qs[e], k)
                if masked:
                    r = lax.broadcasted_iota(jnp.int32, (tq, w), 0)
                    c = lax.broadcasted_iota(jnp.int32, (tq, w), 1)
                    s = jnp.where(c <= r + (w - tq), s, NEG)
                ss.append(s)
            for e in range(2):
                m_old = m_ref[e]
                m_new = jnp.maximum(m_old, jnp.max(ss[e], axis=-1, keepdims=True))
                p = jnp.exp2(ss[e] - jnp.concatenate([m_new] * (w // LANES), axis=1)).astype(BF16)
                m_ref[e] = m_new
                acc_ref[e] = jnp.exp2(m_old - m_new) * acc_ref[e] + _dot_nn(p, vs[e])

        def step(kb2, carry):
            tile(pl.multiple_of(kb2 * 2 * tq, 2 * tq), 2 * tq, False)
            return carry

        lax.fori_loop(0, qi // 2, step, 0)

        @pl.when(qi % 2 == 1)
        def _():
            tile(pl.multiple_of((qi - 1) * tq, tq), 2 * tq, True)

        @pl.when(qi % 2 == 0)
        def _():
            tile(pl.multiple_of(qi * tq, tq), tq, True)
        lane = lax.broadcasted_iota(jnp.int32, (tq, LANES), 1)
        first = lane < B_VDIM
        accs = [acc_ref[e] for e in range(2)]
        ls = [accs[e][:, sum_lane[e]:sum_lane[e] + 1] for e in range(2)]
        outs = [accs[e] / ls[e] for e in range(2)]
        lses = [m_ref[e] + jnp.log2(ls[e]) for e in range(2)]
        o = jnp.where(first, outs[0], outs[1])
        zv = z_ref[...].astype(F32)
        o_ref[...] = o.astype(BF16)
        y_ref[...] = (o * (zv * _sigmoid(zv))).astype(BF16)
        lse_ref[...] = jnp.where(first, lses[0], lses[1])
        for e in range(2):
            lrow_ref[e:e + 1, :] = jnp.transpose(lses[e])[0:1, :]

    blk = pl.BlockSpec((None, tq, LANES), lambda b, j, i: (b, i, j))
    return pl.pallas_call(
        body, name="mla_fwd", grid=(bl, pairs, nq),
        in_specs=[pl.BlockSpec((None, tq, 2 * LANES), lambda b, j, i: (b, i, j)),
                  pl.BlockSpec((None, t, 2 * LANES), lambda b, j, i: (b, 0, j)),
                  pl.BlockSpec((None, t, LANES), lambda b, j, i: (b, 0, v_blk0 + j)),
                  pl.BlockSpec((None, t, LANES), lambda b, j, i: (b, 0, 0)),
                  blk],
        out_specs=(blk, blk, blk, pl.BlockSpec((None, None, None, 2, tq), lambda b, j, i: (b, j, i, 0, 0))),
        out_shape=(jax.ShapeDtypeStruct((bl, t, B_WIDTH), BF16), jax.ShapeDtypeStruct((bl, t, B_WIDTH), BF16),
                   jax.ShapeDtypeStruct((bl, t, B_WIDTH), F32),
                   jax.ShapeDtypeStruct((bl, pairs, nq, 2, tq), F32)),
        scratch_shapes=[pltpu.VMEM((2, tq, LANES), F32), pltpu.VMEM((2, tq, LANES), F32)],
        compiler_params=_params(("parallel", "parallel", "arbitrary")),
    )(q_cat, kvup, kvup, kr, z)


def _mla_dq(q_cat, kvup, kr, do, o, lse, tabs, tq):
    bl, t, _ = q_cat.shape
    nq = t // tq
    pairs = B_HEADS // 2
    v_blk0 = (B_HEADS * LANES) // LANES

    def body(q_ref, k_ref, v_ref, kr_ref, do_ref, o_ref, lse_ref, tc, tsa, tsb, dq_ref, drow_ref, acc_ref):
        qi = pl.program_id(2)
        dov, ov, lsev = do_ref[...], o_ref[...], lse_ref[...]
        qs = [q_ref[:, e * LANES:(e + 1) * LANES] for e in range(2)]
        terms = [_head_terms(dov, ov, lsev, e) for e in range(2)]
        row = lax.broadcasted_iota(jnp.int32, (tq, tq), 0)
        col = lax.broadcasted_iota(jnp.int32, (tq, tq), 1)
        tri = col <= row
        for e in range(2):
            acc_ref[e] = jnp.zeros((tq, LANES), F32)

        def tile(k0, w, masked):
            krv = kr_ref[pl.ds(k0, w), :]
            v = v_ref[pl.ds(k0, w), :]
            ks = [k_ref[pl.ds(k0, w), e * LANES:(e + 1) * LANES] + krv for e in range(2)]
            ss = [_dot_nt(qs[e], ks[e]) for e in range(2)]
            dps = [_dot_nt(terms[e][0], v) for e in range(2)]
            for e in range(2):
                s = ss[e]
                if masked:
                    r = lax.broadcasted_iota(jnp.int32, (tq, w), 0)
                    c = lax.broadcasted_iota(jnp.int32, (tq, w), 1)
                    s = jnp.where(c <= r + (w - tq), s, NEG)
                p = jnp.exp2(s - terms[e][2])
                ds = (p * (dps[e] - terms[e][1])).astype(BF16)
                acc_ref[e] += _dot_nn(ds, ks[e])

        def step(kb2, carry):
            tile(pl.multiple_of(kb2 * 2 * tq, 2 * tq), 2 * tq, False)
            return carry

        lax.fori_loop(0, qi // 2, step, 0)

        @pl.when(qi % 2 == 1)
        def _():
            tile(pl.multiple_of((qi - 1) * tq, tq), 2 * tq, True)

        @pl.when(qi % 2 == 0)
        def _():
            tile(pl.multiple_of(qi * tq, tq), tq, True)

        for e in range(2):
            dq_ref[:, e * LANES:(e + 1) * LANES] = _rope_apply(acc_ref[e] * B_SCALE, tc[...], tsa[...], tsb[...], -1).astype(BF16)
            drow_ref[e:e + 1, :] = _col_to_row(terms[e][1], tq)

    blk = pl.BlockSpec((None, tq, LANES), lambda b, j, i: (b, i, j))
    tab = pl.BlockSpec((None, tq, LANES), lambda b, j, i: (b, i, 0))
    qblk = pl.BlockSpec((None, tq, 2 * LANES), lambda b, j, i: (b, i, j))
    return pl.pallas_call(
        body, name="mla_dq", grid=(bl, pairs, nq),
        in_specs=[qblk,
                  pl.BlockSpec((None, t, 2 * LANES), lambda b, j, i: (b, 0, j)),
                  pl.BlockSpec((None, t, LANES), lambda b, j, i: (b, 0, v_blk0 + j)),
                  pl.BlockSpec((None, t, LANES), lambda b, j, i: (b, 0, 0)),
                  blk, blk, blk, tab, tab, tab],
        out_specs=(qblk, pl.BlockSpec((None, None, None, 2, tq), lambda b, j, i: (b, j, i, 0, 0))),
        out_shape=(jax.ShapeDtypeStruct((bl, t, B_HEADS * LANES), BF16),
                   jax.ShapeDtypeStruct((bl, pairs, nq, 2, tq), F32)),
        scratch_shapes=[pltpu.VMEM((2, tq, LANES), F32)],
        compiler_params=_params(("parallel", "parallel", "arbitrary")),
    )(q_cat, kvup, kvup, kr, do, o, lse, *tabs)


def _mla_dkv(q_cat, kvup, kr, do, lse_rows, delta_rows, tq):
    bl, t, _ = q_cat.shape
    nq = t // tq
    pairs = B_HEADS // 2
    v_blk0 = (B_HEADS * LANES) // LANES

    def body(q_ref, k_ref, v_ref, kr_ref, do_ref, lrow_ref, drow_ref, dk_ref, dv_ref, acc_ref):
        kb = pl.program_id(2)
        v = v_ref[...]
        krv = kr_ref[...]
        ks = [k_ref[:, e * LANES:(e + 1) * LANES] + krv for e in range(2)]
        krow = lax.broadcasted_iota(jnp.int32, (tq, tq), 0)
        qcol = lax.broadcasted_iota(jnp.int32, (tq, tq), 1)
        tri = krow <= qcol
        lane = lax.broadcasted_iota(jnp.int32, (tq, LANES), 1)
        mine = [lane < B_VDIM, lane >= B_VDIM]

        for e in range(3):
            acc_ref[e] = jnp.zeros((tq, LANES), F32)

        def tile(qb, nblk, masked):
            w = nblk * tq
            rows = pl.ds(pl.multiple_of(qb * tq, tq), w)
            dov = do_ref[rows, :]
            lane_w = lax.broadcasted_iota(jnp.int32, (w, LANES), 1)
            mine_w = [lane_w < B_VDIM, lane_w >= B_VDIM]
            qs = [q_ref[rows, e * LANES:(e + 1) * LANES] for e in range(2)]
            does = [jnp.where(mine_w[e], dov, jnp.zeros_like(dov)) for e in range(2)]
            sts = [_dot_nt(ks[e], qs[e]) for e in range(2)]
            dpts = [_dot_nt(v, does[e]) for e in range(2)]

            def rows_of(ref, e):
                return jnp.concatenate([ref[qb + i, e:e + 1, :] for i in range(nblk)], axis=1)

            pts = []
            for e in range(2):
                st = sts[e]
                if masked:
                    r = lax.broadcasted_iota(jnp.int32, (tq, w), 0)
                    c = lax.broadcasted_iota(jnp.int32, (tq, w), 1)
                    st = jnp.where(r <= c, st, NEG)
                pts.append(jnp.exp2(st - rows_of(lrow_ref, e)))
            acc_ref[2] += _dot_nn(pts[0].astype(BF16), does[0]) + _dot_nn(pts[1].astype(BF16), does[1])
            for e in range(2):
                dst = (pts[e] * (dpts[e] - rows_of(drow_ref, e))).astype(BF16)
                acc_ref[e] += _dot_nn(dst, qs[e])

        rest = nq - 1 - kb
        odd = rest % 2

        @pl.when(odd == 1)
        def _():
            tile(kb, 2, True)

        @pl.when(odd == 0)
        def _():
            tile(kb, 1, True)

        def step(i, carry):
            tile(kb + 1 + odd + 2 * i, 2, False)
            return carry

        lax.fori_loop(0, rest // 2, step, 0)
        dk_ref[:, 0:LANES] = (acc_ref[0] * LN2).astype(BF16)
        dk_ref[:, LANES:2 * LANES] = (acc_ref[1] * LN2).astype(BF16)
        dv_ref[...] = acc_ref[2].astype(BF16)

    full = pl.BlockSpec((None, t, LANES), lambda b, j, i: (b, 0, j))
    rows = pl.BlockSpec((None, None, nq, 2, tq), lambda b, j, i: (b, j, 0, 0, 0))
    kblk = pl.BlockSpec((None, tq, 2 * LANES), lambda b, j, i: (b, i, j))
    return pl.pallas_call(
        body, name="mla_dkv", grid=(bl, pairs, nq),
        in_specs=[pl.BlockSpec((None, t, 2 * LANES), lambda b, j, i: (b, 0, j)),
                  kblk,
                  pl.BlockSpec((None, tq, LANES), lambda b, j, i: (b, i, v_blk0 + j)),
                  pl.BlockSpec((None, tq, LANES), lambda b, j, i: (b, i, 0)),
                  full, rows, rows],
        out_specs=(kblk, pl.BlockSpec((None, tq, LANES), lambda b, j, i: (b, i, j))),
        out_shape=(jax.ShapeDtypeStruct((bl, t, B_HEADS * LANES), BF16),
                   jax.ShapeDtypeStruct((bl, t, B_WIDTH), BF16)),
        scratch_shapes=[pltpu.VMEM((3, tq, LANES), F32)],
        compiler_params=_params(("parallel", "parallel", "arbitrary")),
    )(q_cat, kvup, kvup, kr, do, lse_rows, delta_rows)


def _adamw(w, g, m, v, *, name):
    r, c = w.shape
    tr = _row_tile(r, 256)
    c1 = 1.0 - ADAM_B1
    c2 = 1.0 - ADAM_B2
    bc1 = 1.0 - ADAM_B1 ** ADAM_STEP
    bc2 = 1.0 - ADAM_B2 ** ADAM_STEP

    def body(w_ref, g_ref, m_ref, v_ref, d_ref, nm_ref, nv_ref):
        gv = g_ref[...]
        nm = ADAM_B1 * m_ref[...] + c1 * gv
        nv = ADAM_B2 * v_ref[...] + c2 * (gv * gv)
        nm_ref[...] = nm
        nv_ref[...] = nv
        d_ref[...] = -ADAM_LR * ((nm / bc1) / (jnp.sqrt(nv / bc2) + ADAM_EPS) + ADAM_WD * w_ref[...])

    blk = pl.BlockSpec((tr, c), lambda i: (i, 0))
    sds = jax.ShapeDtypeStruct((r, c), F32)
    return pl.pallas_call(
        body, name=name, grid=(r // tr,), in_specs=[blk] * 4, out_specs=(blk,) * 3,
        out_shape=(sds,) * 3, compiler_params=_params(("parallel",)),
    )(w, g, m, v)


def _add_my_half(stacked, other, core, out_dtype, *, name):
    nch, a, c = stacked.shape
    h = a // 2
    tr = _row_tile(h, 256)
    nblk = h // tr

    def body(core_ref, s_ref, p_ref, o_ref):
        o_ref[...] = (s_ref[...] + p_ref[...]).astype(o_ref.dtype)

    return pl.pallas_call(
        body, name=name,
        grid_spec=pltpu.PrefetchScalarGridSpec(
            num_scalar_prefetch=1, grid=(nch, nblk),
            in_specs=[pl.BlockSpec((None, tr, c), lambda k, i, cr: (k, cr[0] * nblk + i, 0)),
                      pl.BlockSpec((None, tr, c), lambda k, i, cr: (k, i, 0))],
            out_specs=pl.BlockSpec((None, tr, c), lambda k, i, cr: (k, i, 0))),
        out_shape=jax.ShapeDtypeStruct((nch, h, c), out_dtype),
        compiler_params=_params(("parallel", "parallel")),
    )(core, stacked, other)


def _sum_chips(parts, own, chip, *, name):
    nch, h, c = parts.shape
    tr = _row_tile(h, 256)

    def body(chip_ref, p_ref, own_ref, o_ref):
        me = chip_ref[0]

        def slot(k):
            return jnp.where(me == k, own_ref[k].astype(F32), p_ref[k].astype(F32))

        acc = slot(0) + slot(1)
        for k in range(2, nch):
            acc = acc + slot(k)
        o_ref[...] = acc

    blk = pl.BlockSpec((nch, tr, c), lambda i, cr: (0, i, 0))
    return pl.pallas_call(
        body, name=name,
        grid_spec=pltpu.PrefetchScalarGridSpec(
            num_scalar_prefetch=1, grid=(h // tr,), in_specs=[blk, blk],
            out_specs=pl.BlockSpec((tr, c), lambda i, cr: (i, 0))),
        out_shape=jax.ShapeDtypeStruct((h, c), F32), compiler_params=_params(("parallel",)),
    )(chip, parts, own)


def _join_halves(mine, other, core, *, name):
    h, c = mine.shape
    tr = _row_tile(h, 256)
    nblk = h // tr

    def body(core_ref, m_ref, s_ref, o_ref):
        is_mine = pl.program_id(0) // nblk == core_ref[0]

        @pl.when(is_mine)
        def _():
            o_ref[...] = m_ref[...]

        @pl.when(jnp.logical_not(is_mine))
        def _():
            o_ref[...] = s_ref[...]

    blk = pl.BlockSpec((tr, c), lambda i, cr: (i % nblk, 0))
    return pl.pallas_call(
        body, name=name,
        grid_spec=pltpu.PrefetchScalarGridSpec(
            num_scalar_prefetch=1, grid=(2 * nblk,), in_specs=[blk, blk],
            out_specs=pl.BlockSpec((tr, c), lambda i, cr: (i, 0))),
        out_shape=jax.ShapeDtypeStruct((2 * h, c), F32), compiler_params=_params(("arbitrary",)),
    )(core, mine, other)


def _place():
    x, y, c = lax.axis_index("x"), lax.axis_index("y"), lax.axis_index("c")
    chips = [(1 - x, y), (x, 1 - y), (1 - x, 1 - y)]
    return x, y, c, chips


def _remote(src, dst, send_sems, recv_sems, k, to):
    return pltpu.make_async_remote_copy(src_ref=src, dst_ref=dst, send_sem=send_sems.at[k],
                                        recv_sem=recv_sems.at[k], device_id=to, device_id_type=MESH)


def _hbm_call(body, name, ins, out_shapes, n_remote):
    any_spec = pl.BlockSpec(memory_space=pl.ANY)
    return pl.pallas_call(
        body, name=name, in_specs=[any_spec] * len(ins), out_specs=tuple([any_spec] * len(out_shapes)),
        out_shape=tuple(out_shapes),
        scratch_shapes=[pltpu.SemaphoreType.DMA((n_remote,)), pltpu.SemaphoreType.DMA((n_remote,))],
    )(*ins)


def _all_gather_chips(shards, *, name):
    n = len(shards)

    def body(*refs):
        ins, outs = refs[:n], refs[n:2 * n]
        send_sems, recv_sems = refs[2 * n:]
        x, y, c, chips = _place()
        me = 2 * x + y
        sent = []
        for s in range(n):
            h = ins[s].shape[0] // 2
            for j, (px, py) in enumerate(chips):
                cp = _remote(ins[s].at[pl.ds(c * h, h)], outs[s].at[me, pl.ds(c * h, h)],
                             send_sems, recv_sems, s * 6 + j, (px, py, c))
                cp.start()
                sent.append(cp)
        for s in range(n):
            h = ins[s].shape[0] // 2
            for j, (px, py) in enumerate(chips):
                slab = outs[s].at[2 * px + py, pl.ds(c * h, h)]
                _remote(slab, slab, send_sems, recv_sems, s * 6 + j, (px, py, c)).wait_recv()
                cp = _remote(slab, slab, send_sems, recv_sems, s * 6 + 3 + j, (x, y, 1 - c))
                cp.start()
                sent.append(cp)
        for s in range(n):
            h = ins[s].shape[0] // 2
            for j, (px, py) in enumerate(chips):
                slab = outs[s].at[2 * px + py, pl.ds((1 - c) * h, h)]
                _remote(slab, slab, send_sems, recv_sems, s * 6 + 3 + j, (x, y, 1 - c)).wait_recv()
        for cp in sent:
            cp.wait_send()

    out_shapes = [jax.ShapeDtypeStruct((N_CHIPS,) + s.shape, s.dtype) for s in shards]
    return _hbm_call(body, name, shards, out_shapes, 6 * n)


def _pair_send_other_half(stacked, *, name):
    n = len(stacked)

    def body(*refs):
        ins, outs = refs[:n], refs[n:2 * n]
        send_sems, recv_sems = refs[2 * n:]
        x, y, c, _chips = _place()
        sent = []
        for s in range(n):
            h = ins[s].shape[1] // 2
            cp = _remote(ins[s].at[:, pl.ds((1 - c) * h, h)], outs[s], send_sems, recv_sems, s, (x, y, 1 - c))
            cp.start()
            sent.append(cp)
        for cp in sent:
            cp.wait_recv()
        for cp in sent:
            cp.wait_send()

    out_shapes = [jax.ShapeDtypeStruct((s.shape[0], s.shape[1] // 2, s.shape[2]), s.dtype) for s in stacked]
    return _hbm_call(body, name, stacked, out_shapes, n)


def _chip_exchange(halves, *, name):
    n = len(halves)

    def body(*refs):
        ins, outs = refs[:n], refs[n:2 * n]
        send_sems, recv_sems = refs[2 * n:]
        x, y, c, chips = _place()
        me = 2 * x + y
        sent = []
        for s in range(n):
            for j, (px, py) in enumerate(chips):
                cp = _remote(ins[s].at[2 * px + py], outs[s].at[me], send_sems, recv_sems, s * 3 + j, (px, py, c))
                cp.start()
                sent.append(cp)
        for s in range(n):
            for j, (px, py) in enumerate(chips):
                slab = outs[s].at[2 * px + py]
                _remote(slab, slab, send_sems, recv_sems, s * 3 + j, (px, py, c)).wait_recv()
        for cp in sent:
            cp.wait_send()

    out_shapes = [jax.ShapeDtypeStruct(s.shape, s.dtype) for s in halves]
    return _hbm_call(body, name, halves, out_shapes, 3 * n)


def _chip_exchange_start(halves, *, name):
    n = len(halves)
    hbm = pl.BlockSpec(memory_space=pltpu.HBM)
    sem = pl.BlockSpec(memory_space=pltpu.SEMAPHORE)

    def body(*refs):
        ins, lands = refs[:n], refs[n:2 * n]
        send_sems, recv_sems = refs[2 * n], refs[2 * n + 1]
        token = refs[-1]
        x, y, c, chips = _place()
        me = 2 * x + y
        for s in range(n):
            for j, (px, py) in enumerate(chips):
                _remote(ins[s].at[2 * px + py], lands[s].at[me], send_sems, recv_sems, s * 3 + j, (px, py, c)).start()
        token[...] = jnp.zeros_like(token)

    slabs = [pltpu.HBM(s.shape, s.dtype) for s in halves]
    outs = pl.pallas_call(
        body, name=name,
        out_shape=(pltpu.SemaphoreType.DMA((3 * n,)), pltpu.SemaphoreType.DMA((3 * n,)), *slabs, *slabs,
                   jax.ShapeDtypeStruct((8, LANES), F32)),
        in_specs=[hbm] * (2 * n), out_specs=(sem, sem, *([hbm] * (2 * n)), pl.BlockSpec(memory_space=pltpu.VMEM)),
        input_output_aliases={i: 2 + i for i in range(2 * n)},
        compiler_params=pltpu.CompilerParams(has_side_effects=pltpu.SideEffectType.DATAFLOW_SIDE_EFFECTING),
    )(*[pltpu.with_memory_space_constraint(s, pltpu.HBM) for s in halves],
      *[pltpu.with_memory_space_constraint(lax.empty(s.shape, s.dtype), pltpu.HBM) for s in halves])
    return outs[0], outs[1], list(outs[2:2 + n]), list(outs[2 + n:2 + 2 * n]), outs[-1]


def _chip_exchange_wait(send_sems, recv_sems, sent, lands, after, *, name):
    n = len(sent)
    hbm = pl.BlockSpec(memory_space=pltpu.HBM)
    sem = pl.BlockSpec(memory_space=pltpu.SEMAPHORE)

    def body(*refs):
        ins, lands_in = refs[:n], refs[n:2 * n]
        send_sems, recv_sems = refs[2 * n], refs[2 * n + 1]
        x, y, c, chips = _place()
        me = 2 * x + y
        for s in range(n):
            for j, (px, py) in enumerate(chips):
                k = 2 * px + py
                _remote(ins[s].at[k], lands_in[s].at[me], send_sems, recv_sems, s * 3 + j, (px, py, c)).wait_send()
                _remote(ins[s].at[k], lands_in[s].at[k], send_sems, recv_sems, s * 3 + j, (px, py, c)).wait_recv()

    slabs = [pltpu.HBM(s.shape, s.dtype) for s in sent]
    outs = pl.pallas_call(
        body, name=name, out_shape=(*slabs, *slabs),
        in_specs=[hbm] * (2 * n) + [sem, sem, pl.BlockSpec(memory_space=pl.ANY)],
        out_specs=tuple([hbm] * (2 * n)), input_output_aliases={i: i for i in range(2 * n)},
        compiler_params=pltpu.CompilerParams(has_side_effects=pltpu.SideEffectType.DATAFLOW_SIDE_EFFECTING),
    )(*sent, *lands, send_sems, recv_sems, after)
    return list(outs[n:])


def _pair_swap(halves, *, name):
    n = len(halves)

    def body(*refs):
        ins, outs = refs[:n], refs[n:2 * n]
        send_sems, recv_sems = refs[2 * n:]
        x, y, c, _chips = _place()
        sent = []
        for s in range(n):
            cp = _remote(ins[s], outs[s], send_sems, recv_sems, s, (x, y, 1 - c))
            cp.start()
            sent.append(cp)
        for cp in sent:
            cp.wait_recv()
        for cp in sent:
            cp.wait_send()

    out_shapes = [jax.ShapeDtypeStruct(s.shape, s.dtype) for s in halves]
    return _hbm_call(body, name, halves, out_shapes, n)


def _pack_rows(parts, row_multiple):
    flat = jnp.concatenate([p.reshape(-1) for p in parts])
    quantum = row_multiple * PACK_COLS
    pad = (-flat.shape[0]) % quantum
    flat = jnp.pad(flat, (0, pad))
    return flat.reshape(-1, PACK_COLS)


def _unpack(flat, shapes):
    out, pos = [], 0
    for shp in shapes:
        size = math.prod(shp)
        out.append(flat[pos:pos + size].reshape(shp))
        pos += size
    return out


def _to_chunks_cols(full):
    r, c4 = full.shape
    return full.reshape(r, N_CHIPS, c4 // N_CHIPS).transpose(1, 0, 2)


def _from_chunks_cols(stacked):
    nch, r, c = stacked.shape
    return stacked.transpose(1, 0, 2).reshape(r, nch * c)


def _class_major(a, bl, t, dil):
    w = a.shape[-1]
    if dil == 1:
        return a.reshape(bl, 1, t, w)
    return a.reshape(bl, t // dil, dil, w).transpose(0, 2, 1, 3)


def _natural(a):
    bl, dil, ln, w = a.shape
    if dil == 1:
        return a.reshape(bl * ln, w)
    return a.transpose(0, 2, 1, 3).reshape(bl * ln * dil, w)


def _train_step(x, positions, a_pre_norm, a_w_in, a_w_out, a_post_norm, kv_norm, kv_w_down, kv_latent_norm,
                kv_w_up, b_pre_norm, b_w_in, b_q_norm, b_w_q_up, b_w_out, b_post_norm, loss_target, moments):
    bl, t, d = x.shape
    n = bl * t
    qb = t // A_DILATIONS[-1]
    tq = _tile(t, 256)
    dq4 = d // N_CHIPS
    chip = 2 * lax.axis_index("x") + lax.axis_index("y")
    chip_arr = chip.astype(jnp.int32).reshape(1)
    core_arr = lax.axis_index("c").astype(jnp.int32).reshape(1)

    w_in_a_s = a_w_in[0].astype(BF16)
    outs_s = jnp.concatenate([a_w_out[0], b_w_out[0]], axis=0).astype(BF16)
    small_shapes = [kv_w_down.shape, kv_w_up.shape, b_w_in[0].shape, b_w_q_up[0].shape]
    small_s = _pack_rows([kv_w_down, kv_w_up, b_w_in[0], b_w_q_up[0]], 32).astype(BF16)
    gains_s = jnp.pad(jnp.concatenate([a_pre_norm[0], a_post_norm[0]]), (0, 16 * LANES - 2 * dq4)).reshape(16, LANES)
    shards = [w_in_a_s, outs_s, small_s, gains_s]
    gathered = _all_gather_chips(shards, name="gather_weights")
    g_in_a, g_outs, g_small, g_gains = [lax.dynamic_update_index_in_dim(g, s, chip, 0)
                                        for g, s in zip(gathered, shards)]

    w_in_a = _from_chunks_cols(g_in_a)
    w_out_a = g_outs[:, :A_WIDTH // N_CHIPS].reshape(A_WIDTH, d)
    w_out_b = g_outs[:, A_WIDTH // N_CHIPS:].reshape(B_WIDTH, d)
    sm = [_unpack(g_small[k].reshape(-1), small_shapes) for k in range(N_CHIPS)]
    w_down = jnp.concatenate([sm[k][0] for k in range(N_CHIPS)], axis=0)
    w_up = jnp.concatenate([sm[k][1] for k in range(N_CHIPS)], axis=1)
    w_in_b = jnp.concatenate([sm[k][2] for k in range(N_CHIPS)], axis=1)
    w_q_up = jnp.concatenate([sm[k][3] for k in range(N_CHIPS)], axis=1)
    gflat = g_gains.reshape(N_CHIPS, -1)
    g_a_pre = gflat[:, :dq4].reshape(1, d)
    g_a_post = gflat[:, dq4:2 * dq4].reshape(1, d)

    w_up_h = w_up.reshape(B_KV_LORA, B_HEADS, B_NOPE + B_VDIM)
    w_up_k = jnp.pad(w_up_h[:, :, :B_NOPE], ((0, 0), (0, 0), (0, LANES - B_NOPE))).reshape(B_KV_LORA, B_HEADS * LANES)
    w_up_v = w_up_h[:, :, B_NOPE:].reshape(B_KV_LORA, B_WIDTH)
    w_up_cat = jnp.concatenate([w_up_k, w_up_v], axis=1)
    w_q_up_p = jnp.pad(w_q_up.reshape(B_Q_LORA, B_HEADS, B_QK_DIM),
                       ((0, 0), (0, 0), (0, LANES - B_QK_DIM))).reshape(B_Q_LORA, B_HEADS * LANES)
    zeros_d = lambda c: jnp.zeros((d, c), BF16)
    w_down_p = jnp.concatenate([w_down[:, :B_KV_LORA], zeros_d(B_NOPE), w_down[:, B_KV_LORA:],
                                zeros_d(LANES - B_NOPE - B_ROPE)], axis=1)
    w_cq = w_in_b[:, :B_Q_LORA]
    w_z = w_in_b[:, B_Q_LORA:]

    tabs_a = _rope_tables(positions, A_ROPE_THETA, 0)
    tabs_b = _rope_tables(positions, B_ROPE_THETA, B_NOPE)

    h0 = x.reshape(n, d)
    hn_a = _rms_fwd(h0, g_a_pre, BF16, name="a_pre_norm", tr=1024)
    is_qk = lambda j: j != 2
    is_q = lambda j: j == 0
    z_blk_a = 3 * A_GROUPS
    z_a = _matmul(hn_a, w_in_a, "nn", BF16, name="a_proj_z", b_cols=(z_blk_a, 1))
    o_groups, lse_groups, qkv_cm, hn_cm, tabs_cm = [], [], [], [], []
    for g, dil in enumerate(A_DILATIONS):
        flat = lambda a: _class_major(a, bl, t, dil).reshape(n, a.shape[-1])
        hn_g = hn_a if dil == 1 else flat(hn_a)
        tabs_g = tabs_a if dil == 1 else lax.optimization_barrier(tuple(flat(tb) for tb in tabs_a))
        proj_g = _matmul(hn_g, w_in_a, "nn", BF16, name=f"a_proj_{g}", rope=(tabs_g, is_qk),
                         out_scale=(A_SCALE * LOG2E, is_q), b_cols=(3 * g, 3))
        src = proj_g.reshape(bl, dil, t // dil, 3 * A_WIDTH)
        hn_cm.append(hn_g)
        tabs_cm.append(tabs_g)
        qkv_cm.append(src)
        o_g, lse_g = _attn_a_fwd(src, 0, qb, BF16, name=f"attn_a_fwd_{g}")
        o_groups.append(_natural(o_g))
        lse_groups.append(_natural(lse_g))
    ypre_a, om_a, lse_a = _merge_gate_fwd(o_groups, lse_groups, z_a, 0)
    y_a = _matmul(ypre_a, w_out_a, "nn", F32, name="a_out")
    g_kvn = kv_norm.reshape(1, d)
    g_lat = kv_latent_norm.reshape(1, B_KV_LORA)
    h1, hn_kv, hn_b = _post_norm_block(y_a, g_a_post, h0, [g_kvn, b_pre_norm], name="a_post_norm")

    ckr = _matmul(hn_kv, w_down_p, "nn", F32, name="kv_down")
    c_kv, k_rope = _kv_latent_fwd(ckr, g_lat, tabs_b)
    kvup = _matmul(c_kv, w_up_cat, "nn", BF16, name="kv_up")
    z_b = _matmul(hn_b, w_z, "nn", BF16, name="b_proj_z")
    cq_raw = _matmul(hn_b, w_cq, "nn", F32, name="b_proj_q")
    c_q = _rms_fwd(cq_raw, b_q_norm, BF16, name="b_q_norm", tr=1024)
    always = lambda j: True
    q_cat = _matmul(c_q, w_q_up_p, "nn", BF16, name="b_q_up", rope=(tabs_b, always),
                    out_scale=(B_SCALE * LOG2E, always))
    r3 = lambda a: a.reshape(bl, t, a.shape[-1])
    tabs_b3 = tuple(r3(tb) for tb in tabs_b)
    ypre_b, o_b, lse_b, lse_rows_b = _mla_fwd(r3(q_cat), r3(kvup), r3(k_rope), r3(z_b), tq)
    y_b = _matmul(ypre_b.reshape(n, B_WIDTH), w_out_b, "nn", F32, name="b_out")
    dh2, loss_part = _post_norm_loss(y_b, b_post_norm, h1, loss_target.reshape(n, d))

    dy_b, dg_b_post = _rms_bwd(y_b, b_post_norm, dh2, BF16, name="b_post_norm_bwd", tr=1024)
    dypre_b = _matmul(dy_b, w_out_b, "nt", BF16, name="b_out_dx")
    dw_out_b = _matmul(ypre_b.reshape(n, B_WIDTH), dy_b, "tn", F32, name="b_out_dw", tm=1024, tk=2048)
    do_b, dz_b = _gate_bwd(dypre_b, o_b.reshape(n, B_WIDTH), z_b, 0, name="b_gate_bwd", with_delta=False)
    dq_cat, delta_rows_b = _mla_dq(r3(q_cat), r3(kvup), r3(k_rope), r3(do_b), o_b, lse_b, tabs_b3, tq)
    dq_cat = dq_cat.reshape(n, -1)
    dk_cat, dv_b = _mla_dkv(r3(q_cat), r3(kvup), r3(k_rope), r3(do_b), lse_rows_b, delta_rows_b, tq)
    dk_cat, dv_b = dk_cat.reshape(n, -1), dv_b.reshape(n, -1)
    dcq_n = _matmul(dq_cat, w_q_up_p, "nt", F32, name="b_q_up_dx")
    dw_q_up_p = _matmul(c_q, dq_cat, "tn", F32, name="b_q_up_dw", tm=1024, tk=2048)
    dcq, dg_b_q = _rms_bwd(cq_raw, b_q_norm, dcq_n, BF16, name="b_q_norm_bwd", tr=1024)
    dhn_b = _matmul(dz_b, w_z, "nt", F32, name="b_proj_z_dx")
    dhn_b = _matmul(dcq, w_cq, "nt", F32, name="b_proj_q_dx", add=dhn_b)
    dw_z = _matmul(hn_b, dz_b, "tn", F32, name="b_proj_z_dw", tm=1024, tk=2048)
    dw_cq = _matmul(hn_b, dcq, "tn", F32, name="b_proj_q_dw", tm=1024, tk=2048)
    dckv_n = _matmul(dk_cat, w_up_k, "nt", F32, name="kv_up_k_dx")
    dckv_n = _matmul(dv_b, w_up_v, "nt", F32, name="kv_up_v_dx", add=dckv_n)
    dw_up_k = _matmul(c_kv, dk_cat, "tn", F32, name="kv_up_k_dw", tm=1024, tk=2048)
    dw_up_v = _matmul(c_kv, dv_b, "tn", F32, name="kv_up_v_dw", tm=1024, tk=2048)
    dckr, dg_lat = _kv_latent_bwd(dckv_n, ckr, g_lat, dk_cat, tabs_b)
    dhn_kv = _matmul(dckr, w_down_p, "nt", F32, name="kv_down_dx")
    dw_down_p = _matmul(hn_kv, dckr, "tn", F32, name="kv_down_dw", tm=1024, tk=2048)
    dh1, dg_b_pre, dg_kvn = _rms_bwd_pair(h1, b_pre_norm, dhn_b, g_kvn, dhn_kv, dh2, name="h1_norms_bwd")

    dy_a, dg_a_post = _rms_bwd(y_a, g_a_post, dh1, BF16, name="a_post_norm_bwd", tr=1024)
    dypre_a = _matmul(dy_a, w_out_a, "nt", BF16, name="a_out_dx")
    dw_out_a = _matmul(ypre_a, dy_a, "tn", F32, name="a_out_dw", tm=1024, tk=2048)
    do_a, dz_a, delta_a = _gate_bwd(dypre_a, om_a, z_a, 0, name="a_gate_bwd", with_delta=True)
    dw_cols = A_IN_WIDTH // N_CHIPS
    dw_tn = _tile(dw_cols, 512)
    dw_kwargs = dict(tm=1024, tn=dw_tn, tk=4096, out_chunk_blocks=dw_cols // dw_tn)
    r_big = _matmul(hn_a, dz_a, "tn", F32, name="a_proj_dw_z", out_full=(N_CHIPS, d, dw_cols),
                    out_joff=z_blk_a * A_WIDTH // dw_tn, **dw_kwargs)
    dqkvs = []
    for g, dil in enumerate(A_DILATIONS):
        cm = lambda a: _class_major(a, bl, t, dil)
        swap = lambda a: jnp.swapaxes(a, 2, 3)
        lse_cm, delta_cm = cm(lse_a), cm(delta_a)
        tabs_g = tuple(tb.reshape(bl, dil, t // dil, LANES) for tb in tabs_cm[g])
        dqkv = _attn_a_bwd(qkv_cm[g], 0, cm(do_a), lse_cm, delta_cm, swap(lse_cm), swap(delta_cm),
                           tabs_g, qb, name=f"attn_a_bwd_{g}").reshape(n, 3 * A_WIDTH)
        dqkvs.append(dqkv)
        r_big = _matmul(hn_cm[g], dqkv, "tn", F32, name=f"a_proj_dw_{g}", out_into=r_big,
                        out_joff=3 * g * A_WIDTH // dw_tn, **dw_kwargs)
    r_outs = jnp.concatenate([dw_out_a.reshape(N_CHIPS, A_WIDTH // N_CHIPS, d),
                              dw_out_b.reshape(N_CHIPS, B_WIDTH // N_CHIPS, d)], axis=1)

    bulk = [r_big, r_outs]
    recv_b = _pair_send_other_half(bulk, name="reduce_pair_send")
    halves_b = [_add_my_half(s, p, core_arr, BF16, name=f"reduce_pair_add_{i}")
                for i, (s, p) in enumerate(zip(bulk, recv_b))]
    send_sems, recv_sems, sent_b, lands_b, token = _chip_exchange_start(halves_b, name="reduce_exchange_start")

    dhn_a = _matmul(dz_a, w_in_a, "nt", F32, name="a_proj_dx_z", b_koff=z_blk_a, after=token)
    dhn_more = []
    for g, dil in enumerate(A_DILATIONS):
        tk_dx = 3 * A_WIDTH
        if dil == 1:
            dhn_a = _matmul(dqkvs[g], w_in_a, "nt", F32, name=f"a_proj_dx_{g}", add=dhn_a, tk=tk_dx, b_koff=g,
                            after=token)
        else:
            part = _matmul(dqkvs[g], w_in_a, "nt", BF16, name=f"a_proj_dx_{g}", tk=tk_dx, b_koff=g, after=token)
            dhn_more.append(_natural(part.reshape(bl, dil, t // dil, d)))
    grad_x, dg_a_pre = _rms_bwd(h0, g_a_pre, dhn_a, F32, name="a_pre_norm_bwd", adds=(dh1,),
                                dy_more=tuple(dhn_more))

    dw_up = jnp.concatenate([dw_up_k.reshape(B_KV_LORA, B_HEADS, LANES)[:, :, :B_NOPE],
                             dw_up_v.reshape(B_KV_LORA, B_HEADS, B_VDIM)], axis=2).reshape(B_KV_LORA, -1)
    dw_q_up = dw_q_up_p.reshape(B_Q_LORA, B_HEADS, LANES)[:, :, :B_QK_DIM].reshape(B_Q_LORA, -1)
    dw_down = jnp.concatenate([dw_down_p[:, :B_KV_LORA], dw_down_p[:, B_KV_LORA + B_NOPE:B_KV_LORA + B_NOPE + B_ROPE]], axis=1)
    dw_in_b = jnp.concatenate([dw_cq, dw_z], axis=1)
    vec_rep = [dg_kvn.reshape(-1), dg_lat.reshape(-1), dg_b_pre.reshape(-1), dg_b_q.reshape(-1),
               dg_b_post.reshape(-1), loss_part.reshape(-1)]
    vec_shapes = [(dq4,), (dq4,)] + [v.shape for v in vec_rep]
    down_c = dw_down.reshape(N_CHIPS, dq4, -1)
    up_c = _to_chunks_cols(dw_up)
    inb_c = _to_chunks_cols(dw_in_b)
    qup_c = _to_chunks_cols(dw_q_up)
    small_chunks = []
    for k in range(N_CHIPS):
        vecs = [dg_a_pre.reshape(-1)[k * dq4:(k + 1) * dq4], dg_a_post.reshape(-1)[k * dq4:(k + 1) * dq4]] + vec_rep
        small_chunks.append(_pack_rows([down_c[k], up_c[k], inb_c[k], qup_c[k]] + vecs, 32))
    r_small = jnp.stack(small_chunks)

    recv_s = _pair_send_other_half([r_small], name="reduce_pair_send_small")
    halves_s = [_add_my_half(r_small, recv_s[0], core_arr, F32, name="reduce_pair_add_small")]
    parts_s = list(_chip_exchange(halves_s, name="reduce_exchange_small"))
    parts_b = _chip_exchange_wait(send_sems, recv_sems, sent_b, lands_b, grad_x, name="reduce_exchange_wait")
    sums = [_sum_chips(p, own, chip_arr, name=f"reduce_chip_sum_{i}")
            for i, (p, own) in enumerate(zip(parts_b + parts_s, sent_b + halves_s))]
    others = _pair_swap(sums, name="reduce_pair_swap")
    g_big, g_outs_r, g_small_r = [_join_halves(m, o, core_arr, name=f"reduce_join_{i}")
                                  for i, (m, o) in enumerate(zip(sums, others))]

    grads = {}
    grads["a_w_in"] = g_big
    grads["a_w_out"] = g_outs_r[:A_WIDTH // N_CHIPS]
    grads["b_w_out"] = g_outs_r[A_WIDTH // N_CHIPS:]
    small_out_shapes = [down_c.shape[1:], up_c.shape[1:], inb_c.shape[1:], qup_c.shape[1:]] + vec_shapes
    (grads["kv_w_down"], grads["kv_w_up"], grads["b_w_in"], grads["b_w_q_up"], grads["a_pre_norm"],
     grads["a_post_norm"], grads["kv_norm"], grads["kv_latent_norm"], grads["b_pre_norm"], grads["b_q_norm"],
     grads["b_post_norm"], loss_sum) = _unpack(g_small_r.reshape(-1), small_out_shapes)

    weights = dict(a_pre_norm=a_pre_norm, a_w_in=a_w_in, a_w_out=a_w_out, a_post_norm=a_post_norm, kv_norm=kv_norm,
                   kv_w_down=kv_w_down, kv_latent_norm=kv_latent_norm, kv_w_up=kv_w_up, b_pre_norm=b_pre_norm,
                   b_w_in=b_w_in, b_q_norm=b_q_norm, b_w_q_up=b_w_q_up, b_w_out=b_w_out, b_post_norm=b_post_norm)
    names = list(weights)
    out_g, out_d, out_m, out_v = [], [], [], []
    for i, nm in enumerate(names):
        w = weights[nm]
        two_d = (1, w.shape[0]) if w.ndim == 1 else (w.shape[-2], w.shape[-1])
        gw = grads[nm].reshape(two_d)
        dlt, new_m, new_v = _adamw(w.reshape(two_d), gw, moments[i].reshape(two_d),
                                   moments[len(names) + i].reshape(two_d), name=f"adamw_{nm}")
        out_g.append(gw.reshape(w.shape))
        out_d.append(dlt.reshape(w.shape))
        out_m.append(new_m.reshape(w.shape))
        out_v.append(new_v.reshape(w.shape))
    return (loss_sum.reshape(()), grad_x.reshape(bl, t, d), *out_g, *out_d, *out_m, *out_v)


def kernel(x, positions, a_pre_norm, a_w_in, a_w_out, a_post_norm, kv_norm, kv_w_down, kv_latent_norm, kv_w_up, b_pre_norm, b_w_in, b_q_norm, b_w_q_up, b_w_out, b_post_norm, loss_target, m_a_pre_norm, m_a_w_in, m_a_w_out, m_a_post_norm, m_kv_norm, m_kv_w_down, m_kv_latent_norm, m_kv_w_up, m_b_pre_norm, m_b_w_in, m_b_q_norm, m_b_w_q_up, m_b_w_out, m_b_post_norm, v_a_pre_norm, v_a_w_in, v_a_w_out, v_a_post_norm, v_kv_norm, v_kv_w_down, v_kv_latent_norm, v_kv_w_up, v_b_pre_norm, v_b_w_in, v_b_q_norm, v_b_w_q_up, v_b_w_out, v_b_post_norm):
    moments = (m_a_pre_norm, m_a_w_in, m_a_w_out, m_a_post_norm, m_kv_norm, m_kv_w_down, m_kv_latent_norm, m_kv_w_up,
               m_b_pre_norm, m_b_w_in, m_b_q_norm, m_b_w_q_up, m_b_w_out, m_b_post_norm,
               v_a_pre_norm, v_a_w_in, v_a_w_out, v_a_post_norm, v_kv_norm, v_kv_w_down, v_kv_latent_norm, v_kv_w_up,
               v_b_pre_norm, v_b_w_in, v_b_q_norm, v_b_w_q_up, v_b_w_out, v_b_post_norm)
    return _train_step(x, positions, a_pre_norm, a_w_in, a_w_out, a_post_norm, kv_norm, kv_w_down, kv_latent_norm,
                       kv_w_up, b_pre_norm, b_w_in, b_q_norm, b_w_q_up, b_w_out, b_post_norm, loss_target, moments)
```

```python
import math

import jax
import jax.numpy as jnp
from jax import lax
from jax.experimental import pallas as pl
from jax.experimental.pallas import tpu as pltpu

F32 = jnp.float32
BF16 = jnp.bfloat16
MESH = pl.DeviceIdType.MESH

NORM_EPS = 1e-6
NEG = -1e30
LANES = 128
VMEM_LIMIT = 56 * 1024 * 1024
LOG2E = math.log2(math.e)
LN2 = math.log(2.0)

A_GROUPS = 3
A_DILATIONS = (1, 4, 16)
A_HEADS = 8
A_HEAD_DIM = 128
A_WIDTH = A_HEADS * A_HEAD_DIM
A_ROPE_THETA = 500000.0
A_IN_WIDTH = A_GROUPS * 3 * A_WIDTH + A_WIDTH
A_SCALE = A_HEAD_DIM ** -0.5

B_HEADS = 16
B_NOPE = 64
B_ROPE = 32
B_QK_DIM = B_NOPE + B_ROPE
B_VDIM = 64
B_WIDTH = B_HEADS * B_VDIM
B_Q_LORA = 384
B_KV_LORA = 256
B_ROPE_THETA = 10000.0
B_SCALE = B_QK_DIM ** -0.5

ADAM_LR = 0.001
ADAM_B1 = 0.9
ADAM_B2 = 0.999
ADAM_EPS = 1e-08
ADAM_WD = 0.01
ADAM_STEP = 10

N_CHIPS = 4
PACK_COLS = 512


def _params(sem=None):
    return pltpu.CompilerParams(dimension_semantics=sem, vmem_limit_bytes=VMEM_LIMIT)


def _tile(n, want):
    t = min(n, want)
    assert n % t == 0, (n, want)
    return t


def _row_tile(n, want):
    for t in range(min(n, want), 0, -1):
        if n % t == 0 and (t % 16 == 0 or t == n):
            return t
    return n


def _rope_tables(positions, theta, lane0):
    half = 16
    inv_freq = 1.0 / (theta ** (jnp.arange(half, dtype=F32) * (2.0 / (2 * half))))
    n = positions.size
    per_row = LANES // half
    pos = jnp.repeat(positions.astype(F32).reshape(n // per_row, per_row), half, axis=1)
    ang = pos * jnp.tile(inv_freq, per_row)
    cos, sin = lax.optimization_barrier((jnp.cos(ang), jnp.sin(ang)))
    cos, sin = cos.reshape(n, half), sin.reshape(n, half)
    pre = jnp.zeros((n, lane0), F32)
    post = jnp.zeros((n, LANES - lane0 - 2 * half), F32)
    z16 = jnp.zeros((n, half), F32)
    c = jnp.concatenate([pre + 1.0, cos, cos, post + 1.0], axis=1)
    sa = jnp.concatenate([pre, -sin, z16, post], axis=1)
    sb = jnp.concatenate([pre, z16, sin, post], axis=1)
    return lax.optimization_barrier((c, sa, sb))


def _rope_apply(x, c, sa, sb, sign):
    k = x.shape[1] // LANES
    if k > 1:
        c, sa, sb = (jnp.concatenate([t] * k, axis=1) for t in (c, sa, sb))
    w = x.shape[1]
    up = pltpu.roll(x, w - 16, 1)
    dn = pltpu.roll(x, 16, 1)
    if sign > 0:
        return x * c + up * sa + dn * sb
    return x * c - up * sa - dn * sb


def _matmul(a, b, mode, out_dtype, *, name, tm=None, tn=1024, tk=None, add=None, rope=None,
            out_scale=None, b_koff=0, b_cols=None, out_into=None, out_full=None, out_joff=0,
            out_chunk_blocks=None, after=None):
    if mode == "nn":
        m, k = a.shape
        n = b.shape[1]
    elif mode == "nt":
        m, k = a.shape
        n = b.shape[0]
    else:
        k, m = a.shape
        n = b.shape[1]
    b_j0 = 0
    if b_cols is not None:
        tn = _tile(n, tn)
        b_j0, n = b_cols[0], b_cols[1] * tn
    if tm is None:
        if mode == "nt":
            tm = 512 if k > 2048 else 1024
        else:
            tm = 2048 if (k <= 512 and rope is None) else 1024
    if tk is None:
        tk = 3072 if mode == "nt" else 1024
    tm, tn, tk = _tile(m, tm), _tile(n, tn), _tile(k, tk)
    nk = k // tk
    if mode == "nn":
        a_spec = pl.BlockSpec((tm, tk), lambda j, i, kk: (i, kk))
        b_spec = pl.BlockSpec((tk, tn), lambda j, i, kk: (kk, j + b_j0))
        dims = (((1,), (0,)), ((), ()))
    elif mode == "nt":
        a_spec = pl.BlockSpec((tm, tk), lambda j, i, kk: (i, kk))
        b_spec = pl.BlockSpec((tn, tk), lambda j, i, kk: (j, kk + b_koff))
        dims = (((1,), (1,)), ((), ()))
    else:
        a_spec = pl.BlockSpec((tk, tm), lambda j, i, kk: (kk, i))
        b_spec = pl.BlockSpec((tk, tn), lambda j, i, kk: (kk, j))
        dims = (((0,), (0,)), ((), ()))
    operands = [a, b]
    in_specs = [a_spec, b_spec]
    if add is not None:
        operands.append(add)
        in_specs.append(pl.BlockSpec((tm, tn), lambda j, i, kk: (i, j)))
    if rope is not None:
        tables, rope_pred = rope
        for t in tables:
            operands.append(t)
            in_specs.append(pl.BlockSpec((tm, LANES), lambda j, i, kk: (i, 0)))
    aliases = {}
    if out_into is not None:
        aliases = {len(operands): 0}
        operands.append(out_into)
        in_specs.append(pl.BlockSpec(memory_space=pl.ANY))
        out_shape = jax.ShapeDtypeStruct(out_into.shape, out_into.dtype)
    elif out_full is not None:
        out_shape = jax.ShapeDtypeStruct(out_full, out_dtype)
    else:
        out_shape = jax.ShapeDtypeStruct((m, n), out_dtype)
    if after is not None:
        operands.append(after)
        in_specs.append(pl.BlockSpec(memory_space=pl.ANY))
    if out_chunk_blocks is not None:
        out_spec = pl.BlockSpec((None, tm, tn), lambda j, i, kk: ((j + out_joff) // out_chunk_blocks, i,
                                                                  (j + out_joff) % out_chunk_blocks))
    else:
        out_spec = pl.BlockSpec((tm, tn), lambda j, i, kk: (i, j + out_joff))

    def body(*refs):
        a_ref, b_ref = refs[0], refs[1]
        pos = 2
        add_ref = None
        if add is not None:
            add_ref = refs[pos]
            pos += 1
        tab_refs = None
        if rope is not None:
            tab_refs = refs[pos:pos + 3]
            pos += 3
        if out_into is not None:
            pos += 1
        if after is not None:
            pos += 1
        o_ref = refs[pos]
        acc_ref = refs[pos + 1] if nk > 1 else None

        def finish(res):
            if add_ref is not None:
                res = res + add_ref[...].astype(F32)
            if tab_refs is None:
                o_ref[...] = res.astype(o_ref.dtype)
                return
            j = pl.program_id(0)
            flag = rope_pred(j)

            roped = _rope_apply(res, tab_refs[0][...], tab_refs[1][...], tab_refs[2][...], 1)
            if out_scale is not None:
                value, scale_pred = out_scale
                use = scale_pred(j)
                roped = roped * (value if use is True else jnp.where(use, value, 1.0))
            if flag is True:
                o_ref[...] = roped.astype(o_ref.dtype)
                return

            @pl.when(flag)
            def _():
                o_ref[...] = roped.astype(o_ref.dtype)

            @pl.when(jnp.logical_not(flag))
            def _():
                o_ref[...] = res.astype(o_ref.dtype)

        part = lax.dot_general(a_ref[...].astype(BF16), b_ref[...].astype(BF16), dims,
                               preferred_element_type=F32)
        if nk == 1:
            finish(part)
            return
        kk = pl.program_id(2)

        @pl.when(kk == 0)
        def _():
            acc_ref[...] = part

        @pl.when(kk > 0)
        def _():
            acc_ref[...] += part

        @pl.when(kk == nk - 1)
        def _():
            finish(acc_ref[...])

    return pl.pallas_call(
        body, name=name, grid=(n // tn, m // tm, nk), in_specs=in_specs, out_specs=out_spec,
        out_shape=out_shape, input_output_aliases=aliases,
        scratch_shapes=[pltpu.VMEM((tm, tn), F32)] if nk > 1 else [],
        compiler_params=_params(("parallel", "parallel", "arbitrary")),
    )(*operands)


def _rms_fwd(x, g, out_dtype, *, name, add=None, tr=512):
    n, d = x.shape
    tr = _tile(n, tr)
    row = pl.BlockSpec((tr, d), lambda i: (i, 0))
    vec = pl.BlockSpec((1, d), lambda i: (0, 0))

    def body(*refs):
        x_ref, g_ref = refs[0], refs[1]
        o_ref = refs[-1]
        xv = x_ref[...].astype(F32)
        r = lax.rsqrt(jnp.mean(xv * xv, axis=-1, keepdims=True) + NORM_EPS)
        y = xv * r * g_ref[...]
        if add is not None:
            y = refs[2][...] + y
        o_ref[...] = y.astype(o_ref.dtype)

    ops = [x, g] + ([add] if add is not None else [])
    specs = [row, vec] + ([row] if add is not None else [])
    return pl.pallas_call(
        body, name=name, grid=(n // tr,), in_specs=specs, out_specs=row,
        out_shape=jax.ShapeDtypeStruct((n, d), out_dtype), compiler_params=_params(("parallel",)),
    )(*ops)


def _rms_bwd(x, g, dy, out_dtype, *, name, adds=(), dy_more=(), tr=512):
    n, d = x.shape
    tr = _tile(n, tr)
    steps = n // tr
    row = pl.BlockSpec((tr, d), lambda i: (i, 0))
    vec = pl.BlockSpec((1, d), lambda i: (0, 0))
    na = len(adds) + len(dy_more)

    def body(*refs):
        x_ref, g_ref, dy_ref = refs[:3]
        add_refs = refs[3:3 + len(adds)]
        more_refs = refs[3 + len(adds):3 + na]
        dx_ref, dg_ref, acc_ref = refs[3 + na:]
        i = pl.program_id(0)
        xv = x_ref[...].astype(F32)
        r = lax.rsqrt(jnp.mean(xv * xv, axis=-1, keepdims=True) + NORM_EPS)
        xh = xv * r
        dyv = dy_ref[...].astype(F32)
        for m_ref in more_refs:
            dyv = dyv + m_ref[...].astype(F32)
        part = (dyv * xh).reshape(tr // 8, 8, d).sum(axis=0)

        @pl.when(i == 0)
        def _():
            acc_ref[...] = part

        @pl.when(i > 0)
        def _():
            acc_ref[...] += part

        t = dyv * g_ref[...]
        dx = r * (t - xh * jnp.mean(t * xh, axis=-1, keepdims=True))
        for a_ref in add_refs:
            dx = dx + a_ref[...].astype(F32)
        dx_ref[...] = dx.astype(dx_ref.dtype)

        @pl.when(i == steps - 1)
        def _():
            dg_ref[...] = jnp.sum(acc_ref[...], axis=0, keepdims=True)

    return pl.pallas_call(
        body, name=name, grid=(steps,), in_specs=[row, vec, row] + [row] * na,
        out_specs=(row, vec),
        out_shape=(jax.ShapeDtypeStruct((n, d), out_dtype), jax.ShapeDtypeStruct((1, d), F32)),
        scratch_shapes=[pltpu.VMEM((8, d), F32)], compiler_params=_params(("arbitrary",)),
    )(x, g, dy, *adds, *dy_more)


def _rms(xv, g):
    return xv * lax.rsqrt(jnp.mean(xv * xv, axis=-1, keepdims=True) + NORM_EPS) * g


def _post_norm_block(y, g, h_in, next_gains, *, name, tr=1024):
    n, d = y.shape
    tr = _tile(n, tr)
    nk = len(next_gains)
    row = pl.BlockSpec((tr, d), lambda i: (i, 0))
    vec = pl.BlockSpec((1, d), lambda i: (0, 0))

    def body(*refs):
        y_ref, g_ref, h_ref = refs[:3]
        gk_refs = refs[3:3 + nk]
        o_ref = refs[3 + nk]
        hn_refs = refs[4 + nk:]
        h = h_ref[...] + _rms(y_ref[...], g_ref[...])
        o_ref[...] = h
        for gk_ref, hn_ref in zip(gk_refs, hn_refs):
            hn_ref[...] = _rms(h, gk_ref[...]).astype(BF16)

    return pl.pallas_call(
        body, name=name, grid=(n // tr,), in_specs=[row, vec, row] + [vec] * nk,
        out_specs=(row,) * (1 + nk),
        out_shape=(jax.ShapeDtypeStruct((n, d), F32),) + (jax.ShapeDtypeStruct((n, d), BF16),) * nk,
        compiler_params=_params(("parallel",)),
    )(y, g, h_in, *next_gains)


def _post_norm_loss(y, g, h_in, target, *, tr=1024):
    n, d = y.shape
    tr = _tile(n, tr)
    steps = n // tr
    row = pl.BlockSpec((tr, d), lambda i: (i, 0))

    def body(y_ref, g_ref, h_ref, t_ref, dh_ref, loss_ref, acc_ref):
        i = pl.program_id(0)
        e = h_ref[...] + _rms(y_ref[...], g_ref[...]) - t_ref[...]
        dh_ref[...] = e / d
        part = (e * e).reshape(tr // 8, 8, d).sum(axis=0)

        @pl.when(i == 0)
        def _():
            acc_ref[...] = part

        @pl.when(i > 0)
        def _():
            acc_ref[...] += part

        @pl.when(i == steps - 1)
        def _():
            s = jnp.sum(jnp.sum(acc_ref[...], axis=-1, keepdims=True), axis=0, keepdims=True)
            loss_ref[...] = 0.5 * s / d

    return pl.pallas_call(
        body, name="b_post_norm_loss", grid=(steps,),
        in_specs=[row, pl.BlockSpec((1, d), lambda i: (0, 0)), row, row],
        out_specs=(row, pl.BlockSpec((1, 1), lambda i: (0, 0))),
        out_shape=(jax.ShapeDtypeStruct((n, d), F32), jax.ShapeDtypeStruct((1, 1), F32)),
        scratch_shapes=[pltpu.VMEM((8, d), F32)], compiler_params=_params(("arbitrary",)),
    )(y, g, h_in, target)


def _rms_bwd_pair(x, g1, dy1, g2, dy2, add, *, name, tr=512):
    n, d = x.shape
    tr = _tile(n, tr)
    steps = n // tr
    row = pl.BlockSpec((tr, d), lambda i: (i, 0))
    vec = pl.BlockSpec((1, d), lambda i: (0, 0))

    def body(x_ref, g1_ref, d1_ref, g2_ref, d2_ref, add_ref, dx_ref, dg1_ref, dg2_ref, acc_ref):
        i = pl.program_id(0)
        xv = x_ref[...]
        r = lax.rsqrt(jnp.mean(xv * xv, axis=-1, keepdims=True) + NORM_EPS)
        xh = xv * r
        dx = add_ref[...]
        for k, (g_ref, d_ref) in enumerate(((g1_ref, d1_ref), (g2_ref, d2_ref))):
            dyv = d_ref[...].astype(F32)
            part = (dyv * xh).reshape(tr // 8, 8, d).sum(axis=0)

            @pl.when(i == 0)
            def _(part=part, k=k):
                acc_ref[k] = part

            @pl.when(i > 0)
            def _(part=part, k=k):
                acc_ref[k] += part

            t = dyv * g_ref[...]
            dx = dx + r * (t - xh * jnp.mean(t * xh, axis=-1, keepdims=True))
        dx_ref[...] = dx

        @pl.when(i == steps - 1)
        def _():
            dg1_ref[...] = jnp.sum(acc_ref[0], axis=0, keepdims=True)
            dg2_ref[...] = jnp.sum(acc_ref[1], axis=0, keepdims=True)

    return pl.pallas_call(
        body, name=name, grid=(steps,), in_specs=[row, vec, row, vec, row, row],
        out_specs=(row, vec, vec),
        out_shape=(jax.ShapeDtypeStruct((n, d), F32), jax.ShapeDtypeStruct((1, d), F32),
                   jax.ShapeDtypeStruct((1, d), F32)),
        scratch_shapes=[pltpu.VMEM((2, 8, d), F32)], compiler_params=_params(("arbitrary",)),
    )(x, g1, dy1, g2, dy2, add)


def _kv_latent_fwd(ckr, g_lat, tabs, *, tr=512):
    n = ckr.shape[0]
    tr = _tile(n, tr)
    lat = B_KV_LORA

    def body(c_ref, k_ref, g_ref, tc, tsa, tsb, ckv_ref, kr_ref):
        xv = c_ref[...]
        r = lax.rsqrt(jnp.mean(xv * xv, axis=-1, keepdims=True) + NORM_EPS)
        ckv_ref[...] = (xv * r * g_ref[...]).astype(BF16)
        kr_ref[...] = _rope_apply(k_ref[...], tc[...], tsa[...], tsb[...], 1).astype(BF16)

    tab = pl.BlockSpec((tr, LANES), lambda i: (i, 0))
    return pl.pallas_call(
        body, name="kv_latent_fwd", grid=(n // tr,),
        in_specs=[pl.BlockSpec((tr, lat), lambda i: (i, 0)),
                  pl.BlockSpec((tr, LANES), lambda i: (i, lat // LANES)),
                  pl.BlockSpec((1, lat), lambda i: (0, 0)), tab, tab, tab],
        out_specs=(pl.BlockSpec((tr, lat), lambda i: (i, 0)), tab),
        out_shape=(jax.ShapeDtypeStruct((n, lat), BF16), jax.ShapeDtypeStruct((n, LANES), BF16)),
        compiler_params=_params(("parallel",)),
    )(ckr, ckr, g_lat, *tabs)


def _kv_latent_bwd(dckv, ckr, g_lat, dk_cat, tabs, *, tr=512):
    n = ckr.shape[0]
    tr = _tile(n, tr)
    steps = n // tr
    lat = B_KV_LORA
    wk = dk_cat.shape[1]

    def body(d_ref, c_ref, g_ref, dk_ref, tc, tsa, tsb, o_ref, dg_ref, acc_ref):
        i = pl.program_id(0)
        xv = c_ref[...]
        r = lax.rsqrt(jnp.mean(xv * xv, axis=-1, keepdims=True) + NORM_EPS)
        xh = xv * r
        dyv = d_ref[...]
        part = (dyv * xh).reshape(tr // 8, 8, lat).sum(axis=0)

        @pl.when(i == 0)
        def _():
            acc_ref[...] = part

        @pl.when(i > 0)
        def _():
            acc_ref[...] += part

        t = dyv * g_ref[...]
        dx = r * (t - xh * jnp.mean(t * xh, axis=-1, keepdims=True))
        o_ref[:, 0:lat] = dx.astype(o_ref.dtype)
        dkr = dk_ref[:, 0:LANES].astype(F32)
        for h in range(1, wk // LANES):
            dkr = dkr + dk_ref[:, h * LANES:(h + 1) * LANES].astype(F32)
        o_ref[:, lat:lat + LANES] = _rope_apply(dkr, tc[...], tsa[...], tsb[...], -1).astype(o_ref.dtype)

        @pl.when(i == steps - 1)
        def _():
            dg_ref[...] = jnp.sum(acc_ref[...], axis=0, keepdims=True)

    tab = pl.BlockSpec((tr, LANES), lambda i: (i, 0))
    return pl.pallas_call(
        body, name="kv_latent_bwd", grid=(steps,),
        in_specs=[pl.BlockSpec((tr, lat), lambda i: (i, 0)), pl.BlockSpec((tr, lat), lambda i: (i, 0)),
                  pl.BlockSpec((1, lat), lambda i: (0, 0)), pl.BlockSpec((tr, wk), lambda i: (i, 0)),
                  tab, tab, tab],
        out_specs=(pl.BlockSpec((tr, lat + LANES), lambda i: (i, 0)), pl.BlockSpec((1, lat), lambda i: (0, 0))),
        out_shape=(jax.ShapeDtypeStruct((n, lat + LANES), BF16), jax.ShapeDtypeStruct((1, lat), F32)),
        scratch_shapes=[pltpu.VMEM((8, lat), F32)], compiler_params=_params(("arbitrary",)),
    )(dckv, ckr, g_lat, dk_cat, *tabs)


def _sigmoid(z):
    return 1.0 / (1.0 + jnp.exp(-z))


def _lane_place(cols, width):
    rows = cols[0].shape[0]
    lane = lax.broadcasted_iota(jnp.int32, (rows, width), 1)
    out = jnp.zeros((rows, width), F32)
    for h, col in enumerate(cols):
        out = jnp.where(lane == h, col, out)
    return out


def _merge_gate_fwd(outs, lses, proj, z_block, *, tr=1024):
    n, w = outs[0].shape
    tr = _tile(n, tr)
    ng = len(outs)

    def body(*refs):
        o_refs = refs[:ng]
        l_refs = refs[ng:2 * ng]
        z_ref = refs[2 * ng]
        y_ref, om_ref, lse_ref = refs[2 * ng + 1:]
        ls = [r[...] for r in l_refs]
        mx = ls[0]
        for l in ls[1:]:
            mx = jnp.maximum(mx, l)
        ssum = jnp.exp2(ls[0] - mx)
        for l in ls[1:]:
            ssum = ssum + jnp.exp2(l - mx)
        tot = mx + jnp.log2(ssum)
        lse_ref[...] = tot
        ws = [jnp.exp2(l - tot) for l in ls]
        for h in range(A_HEADS):
            sl = slice(h * A_HEAD_DIM, (h + 1) * A_HEAD_DIM)
            o = ws[0][:, h:h + 1] * o_refs[0][:, sl]
            for gi in range(1, ng):
                o = o + ws[gi][:, h:h + 1] * o_refs[gi][:, sl]
            z = z_ref[:, sl].astype(F32)
            om_ref[:, sl] = o.astype(BF16)
            y_ref[:, sl] = (o * (z * _sigmoid(z))).astype(BF16)

    row = pl.BlockSpec((tr, w), lambda i: (i, 0))
    lrow = pl.BlockSpec((tr, A_HEADS), lambda i: (i, 0))
    return pl.pallas_call(
        body, name="merge_gate_fwd", grid=(n // tr,),
        in_specs=[row] * ng + [lrow] * ng + [pl.BlockSpec((tr, w), lambda i: (i, z_block))],
        out_specs=(row, row, lrow),
        out_shape=(jax.ShapeDtypeStruct((n, w), BF16), jax.ShapeDtypeStruct((n, w), BF16),
                   jax.ShapeDtypeStruct((n, A_HEADS), F32)),
        compiler_params=_params(("parallel",)),
    )(*outs, *lses, proj)


def _gate_bwd(dy, o, z_arr, z_block, *, name, with_delta, tr=1024):
    n, w = dy.shape
    tr = _tile(n, tr)

    def body(*refs):
        dy_ref, o_ref, z_ref, do_ref, dz_ref = refs[:5]
        dyv = dy_ref[...].astype(F32)
        ov = o_ref[...].astype(F32)
        z = z_ref[...].astype(F32)
        sig = _sigmoid(z)
        do = dyv * (z * sig)
        do_ref[...] = do.astype(BF16)
        dz_ref[...] = (dyv * ov * (sig * (1.0 + z * (1.0 - sig)))).astype(BF16)
        if with_delta:
            prod = do * ov
            cols = [jnp.sum(prod[:, h * A_HEAD_DIM:(h + 1) * A_HEAD_DIM], axis=-1, keepdims=True)
                    for h in range(A_HEADS)]
            refs[5][...] = _lane_place(cols, A_HEADS)

    row = pl.BlockSpec((tr, w), lambda i: (i, 0))
    out_specs = [row, row]
    out_shape = [jax.ShapeDtypeStruct((n, w), BF16), jax.ShapeDtypeStruct((n, w), BF16)]
    if with_delta:
        out_specs.append(pl.BlockSpec((tr, A_HEADS), lambda i: (i, 0)))
        out_shape.append(jax.ShapeDtypeStruct((n, A_HEADS), F32))
    return pl.pallas_call(
        body, name=name, grid=(n // tr,),
        in_specs=[row, row, pl.BlockSpec((tr, w), lambda i: (i, z_block))],
        out_specs=tuple(out_specs), out_shape=tuple(out_shape), compiler_params=_params(("parallel",)),
    )(dy, o, z_arr)


def _dot_nt(a, b):
    return lax.dot_general(a, b, (((1,), (1,)), ((), ())), preferred_element_type=F32)


def _dot_nn(a, b):
    return lax.dot_general(a, b, (((1,), (0,)), ((), ())), preferred_element_type=F32)


def _attn_a_fwd(qkv, cb0, qb, out_dtype, *, name):
    bl, dil, ln, _ = qkv.shape
    nb = ln // qb
    hw = A_WIDTH
    heads = range(A_HEADS)
    sls = [slice(h * A_HEAD_DIM, (h + 1) * A_HEAD_DIM) for h in heads]

    def body(*refs):
        if nb > 1:
            q_ref, kc_ref, vc_ref, kp_ref, vp_ref, o_ref, lse_ref = refs
        else:
            q_ref, kc_ref, vc_ref, o_ref, lse_ref = refs
        i = pl.program_id(2)
        qi = lax.broadcasted_iota(jnp.int32, (qb, qb), 0)
        ki = lax.broadcasted_iota(jnp.int32, (qb, qb), 1)
        mask_c = ki <= qi
        mask_p = jnp.logical_and(ki >= qi, i >= 1)
        s_c = [jnp.where(mask_c, _dot_nt(q_ref[:, sls[h]], kc_ref[:, sls[h]]), NEG) for h in heads]
        m = [jnp.max(s_c[h], axis=-1, keepdims=True) for h in heads]
        if nb > 1:
            s_p = [jnp.where(mask_p, _dot_nt(q_ref[:, sls[h]], kp_ref[:, sls[h]]), NEG) for h in heads]
            m = [jnp.maximum(m[h], jnp.max(s_p[h], axis=-1, keepdims=True)) for h in heads]
        p_c = [jnp.exp2(s_c[h] - m[h]) for h in heads]
        l = [jnp.sum(p_c[h], axis=-1, keepdims=True) for h in heads]
        acc = [_dot_nn(p_c[h].astype(BF16), vc_ref[:, sls[h]]) for h in heads]
        if nb > 1:
            p_p = [jnp.exp2(s_p[h] - m[h]) for h in heads]
            l = [l[h] + jnp.sum(p_p[h], axis=-1, keepdims=True) for h in heads]
            acc = [acc[h] + _dot_nn(p_p[h].astype(BF16), vp_ref[:, sls[h]]) for h in heads]
        for h in heads:
            o_ref[:, sls[h]] = (acc[h] / l[h]).astype(o_ref.dtype)
        lse_ref[...] = _lane_place([m[h] + jnp.log2(l[h]) for h in heads], A_HEADS)

    def spec(off, prev):
        if prev:
            return pl.BlockSpec((None, None, qb, hw), lambda b, r, i: (b, r, jnp.maximum(i - 1, 0), cb0 + off))
        return pl.BlockSpec((None, None, qb, hw), lambda b, r, i: (b, r, i, cb0 + off))

    return pl.pallas_call(
        body, name=name, grid=(bl, dil, nb),
        in_specs=[spec(0, False), spec(1, False), spec(2, False)] + ([spec(1, True), spec(2, True)] if nb > 1 else []),
        out_specs=(pl.BlockSpec((None, None, qb, hw), lambda b, r, i: (b, r, i, 0)),
                   pl.BlockSpec((None, None, qb, A_HEADS), lambda b, r, i: (b, r, i, 0))),
        out_shape=(jax.ShapeDtypeStruct((bl, dil, ln, hw), out_dtype),
                   jax.ShapeDtypeStruct((bl, dil, ln, A_HEADS), F32)),
        compiler_params=_params(("parallel", "parallel", "arbitrary")),
    )(*([qkv] * (5 if nb > 1 else 3)))


def _attn_a_bwd(qkv, cb0, do, lse, delta, lse_t, delta_t, tabs, qb, *, name):
    bl, dil, ln, _ = qkv.shape
    nb = ln // qb
    hw = A_WIDTH

    def body(*refs):
        if nb > 1:
            (q_ref, kc_ref, vc_ref, do_ref, lse_ref, dl_ref, lt_ref, dt_ref, tc, tsa, tsb,
             qn_ref, kp_ref, vp_ref, don_ref, ltn_ref, dtn_ref, o_ref) = refs
        else:
            q_ref, kc_ref, vc_ref, do_ref, lse_ref, dl_ref, lt_ref, dt_ref, tc, tsa, tsb, o_ref = refs
        i = pl.program_id(2)
        row = lax.broadcasted_iota(jnp.int32, (qb, qb), 0)
        col = lax.broadcasted_iota(jnp.int32, (qb, qb), 1)
        m_qc = col <= row
        m_kc = row <= col
        m_qp = jnp.logical_and(col >= row, i >= 1)
        m_kn = jnp.logical_and(row >= col, i + 1 < nb)
        c, sa, sb = tc[...], tsa[...], tsb[...]
        heads = range(A_HEADS)
        sls = [slice(h * A_HEAD_DIM, (h + 1) * A_HEAD_DIM) for h in heads]
        q, kc = [q_ref[:, sl] for sl in sls], [kc_ref[:, sl] for sl in sls]
        vc, dov = [vc_ref[:, sl] for sl in sls], [do_ref[:, sl] for sl in sls]
        lse_c = [lse_ref[:, h:h + 1] for h in heads]
        dl_c = [dl_ref[:, h:h + 1] for h in heads]
        s = [_dot_nt(q[h], kc[h]) for h in heads]
        st = [_dot_nt(kc[h], q[h]) for h in heads]
        dp = [_dot_nt(dov[h], vc[h]) for h in heads]
        dpt = [_dot_nt(vc[h], dov[h]) for h in heads]
        p = [jnp.exp2(jnp.where(m_qc, s[h], NEG) - lse_c[h]) for h in heads]
        pt = [jnp.exp2(jnp.where(m_kc, st[h], NEG) - lt_ref[h:h + 1, :]) for h in heads]
        dq = [_dot_nn((p[h] * (dp[h] - dl_c[h])).astype(BF16), kc[h]) for h in heads]
        dk = [_dot_nn((pt[h] * (dpt[h] - dt_ref[h:h + 1, :])).astype(BF16), q[h]) for h in heads]
        dv = [_dot_nn(pt[h].astype(BF16), dov[h]) for h in heads]
        if nb > 1:
            kp, vp = [kp_ref[:, sl] for sl in sls], [vp_ref[:, sl] for sl in sls]
            qn, don = [qn_ref[:, sl] for sl in sls], [don_ref[:, sl] for sl in sls]
            s = [_dot_nt(q[h], kp[h]) for h in heads]
            st = [_dot_nt(kc[h], qn[h]) for h in heads]
            dp = [_dot_nt(dov[h], vp[h]) for h in heads]
            dpt = [_dot_nt(vc[h], don[h]) for h in heads]
            p = [jnp.exp2(jnp.where(m_qp, s[h], NEG) - lse_c[h]) for h in heads]
            pt = [jnp.exp2(jnp.where(m_kn, st[h], NEG) - ltn_ref[h:h + 1, :]) for h in heads]
            dq = [dq[h] + _dot_nn((p[h] * (dp[h] - dl_c[h])).astype(BF16), kp[h]) for h in heads]
            dk = [dk[h] + _dot_nn((pt[h] * (dpt[h] - dtn_ref[h:h + 1, :])).astype(BF16), qn[h]) for h in heads]
            dv = [dv[h] + _dot_nn(pt[h].astype(BF16), don[h]) for h in heads]
        for h in heads:
            o_ref[:, h * A_HEAD_DIM:(h + 1) * A_HEAD_DIM] = _rope_apply(dq[h] * A_SCALE, c, sa, sb, -1).astype(BF16)
            o_ref[:, hw + h * A_HEAD_DIM:hw + (h + 1) * A_HEAD_DIM] = _rope_apply(dk[h] * LN2, c, sa, sb, -1).astype(BF16)
            o_ref[:, 2 * hw + h * A_HEAD_DIM:2 * hw + (h + 1) * A_HEAD_DIM] = dv[h].astype(BF16)

    def cur(w, col):
        return pl.BlockSpec((None, None, qb, w), lambda b, r, i: (b, r, i, col))

    def prev(w, col):
        return pl.BlockSpec((None, None, qb, w), lambda b, r, i: (b, r, jnp.maximum(i - 1, 0), col))

    def nxt(w, col):
        return pl.BlockSpec((None, None, qb, w), lambda b, r, i: (b, r, jnp.minimum(i + 1, nb - 1), col))

    t_cur = pl.BlockSpec((None, None, A_HEADS, qb), lambda b, r, i: (b, r, 0, i))
    t_nxt = pl.BlockSpec((None, None, A_HEADS, qb), lambda b, r, i: (b, r, 0, jnp.minimum(i + 1, nb - 1)))
    in_specs = [cur(hw, cb0), cur(hw, cb0 + 1), cur(hw, cb0 + 2), cur(hw, 0), cur(A_HEADS, 0), cur(A_HEADS, 0),
                t_cur, t_cur, cur(LANES, 0), cur(LANES, 0), cur(LANES, 0)]
    operands = [qkv, qkv, qkv, do, lse, delta, lse_t, delta_t, *tabs]
    if nb > 1:
        in_specs += [nxt(hw, cb0), prev(hw, cb0 + 1), prev(hw, cb0 + 2), nxt(hw, 0), t_nxt, t_nxt]
        operands += [qkv, qkv, qkv, do, lse_t, delta_t]
    return pl.pallas_call(
        body, name=name, grid=(bl, dil, nb), in_specs=in_specs, out_specs=cur(3 * hw, 0),
        out_shape=jax.ShapeDtypeStruct((bl, dil, ln, 3 * hw), BF16),
        compiler_params=_params(("parallel", "parallel", "arbitrary")),
    )(*operands)


def _head_terms(do, o, lse, e):
    rows = do.shape[0]
    lane = lax.broadcasted_iota(jnp.int32, (rows, LANES), 1)
    mine = (lane < B_VDIM) if e == 0 else (lane >= B_VDIM)
    prod = do.astype(F32) * o.astype(F32)
    dl = jnp.sum(jnp.where(mine, prod, 0.0), axis=-1, keepdims=True)
    do_e = jnp.where(mine, do, jnp.zeros_like(do))
    return do_e, dl, lse[:, e * B_VDIM:e * B_VDIM + 1]


def _col_to_row(col, rows):
    return jnp.transpose(jnp.broadcast_to(col, (rows, LANES)))[0:1, :]


def _mla_fwd(q_cat, kvup, kr, z, tq):
    bl, t, _ = q_cat.shape
    nq = t // tq
    pairs = B_HEADS // 2
    v_blk0 = (B_HEADS * LANES) // LANES

    def body(q_ref, k_ref, v_ref, kr_ref, z_ref, y_ref, o_ref, lse_ref, lrow_ref, m_ref, acc_ref):
        qi = pl.program_id(2)
        qs = [q_ref[:, e * LANES:(e + 1) * LANES] for e in range(2)]
        row = lax.broadcasted_iota(jnp.int32, (tq, tq), 0)
        col = lax.broadcasted_iota(jnp.int32, (tq, tq), 1)
        tri = col <= row
        sum_lane = [B_VDIM, 0]

        for e in range(2):
            m_ref[e] = jnp.full((tq, LANES), NEG, F32)
            acc_ref[e] = jnp.zeros((tq, LANES), F32)

        def tile(k0, w, masked):
            lane = lax.broadcasted_iota(jnp.int32, (w, LANES), 1)
            first = lane < B_VDIM
            krv = kr_ref[pl.ds(k0, w), :]
            v = v_ref[pl.ds(k0, w), :]
            vs = [jnp.where(first, v, jnp.where(lane == B_VDIM, 1.0, 0.0).astype(BF16)),
                  jnp.where(first, jnp.where(lane == 0, 1.0, 0.0).astype(BF16), v)]
            ss = []
            for e in range(2):
                k = k_ref[pl.ds(k0, w), e * LANES:(e + 1) * LANES] + krv
                s = _dot_nt(qs[e], k)
                if masked:
                    r = lax.broadcasted_iota(jnp.int32, (tq, w), 0)
                    c = lax.broadcasted_iota(jnp.int32, (tq, w), 1)
                    s = jnp.where(c <= r + (w - tq), s, NEG)
                ss.append(s)
            for e in range(2):
                m_old = m_ref[e]
                m_new = jnp.maximum(m_old, jnp.max(ss[e], axis=-1, keepdims=True))
                p = jnp.exp2(ss[e] - jnp.concatenate([m_new] * (w // LANES), axis=1)).astype(BF16)
                m_ref[e] = m_new
                acc_ref[e] = jnp.exp2(m_old - m_new) * acc_ref[e] + _dot_nn(p, vs[e])

        def step(kb2, carry):
            tile(pl.multiple_of(kb2 * 2 * tq, 2 * tq), 2 * tq, False)
            return carry

        lax.fori_loop(0, qi // 2, step, 0)

        @pl.when(qi % 2 == 1)
        def _():
            tile(pl.multiple_of((qi - 1) * tq, tq), 2 * tq, True)

        @pl.when(qi % 2 == 0)
        def _():
            tile(pl.multiple_of(qi * tq, tq), tq, True)
        lane = lax.broadcasted_iota(jnp.int32, (tq, LANES), 1)
        first = lane < B_VDIM
        accs = [acc_ref[e] for e in range(2)]
        ls = [accs[e][:, sum_lane[e]:sum_lane[e] + 1] for e in range(2)]
        outs = [accs[e] / ls[e] for e in range(2)]
        lses = [m_ref[e] + jnp.log2(ls[e]) for e in range(2)]
        o = jnp.where(first, outs[0], outs[1])
        zv = z_ref[...].astype(F32)
        o_ref[...] = o.astype(BF16)
        y_ref[...] = (o * (zv * _sigmoid(zv))).astype(BF16)
        lse_ref[...] = jnp.where(first, lses[0], lses[1])
        for e in range(2):
            lrow_ref[e:e + 1, :] = jnp.transpose(lses[e])[0:1, :]

    blk = pl.BlockSpec((None, tq, LANES), lambda b, j, i: (b, i, j))
    return pl.pallas_call(
        body, name="mla_fwd", grid=(bl, pairs, nq),
        in_specs=[pl.BlockSpec((None, tq, 2 * LANES), lambda b, j, i: (b, i, j)),
                  pl.BlockSpec((None, t, 2 * LANES), lambda b, j, i: (b, 0, j)),
                  pl.BlockSpec((None, t, LANES), lambda b, j, i: (b, 0, v_blk0 + j)),
                  pl.BlockSpec((None, t, LANES), lambda b, j, i: (b, 0, 0)),
                  blk],
        out_specs=(blk, blk, blk, pl.BlockSpec((None, None, None, 2, tq), lambda b, j, i: (b, j, i, 0, 0))),
        out_shape=(jax.ShapeDtypeStruct((bl, t, B_WIDTH), BF16), jax.ShapeDtypeStruct((bl, t, B_WIDTH), BF16),
                   jax.ShapeDtypeStruct((bl, t, B_WIDTH), F32),
                   jax.ShapeDtypeStruct((bl, pairs, nq, 2, tq), F32)),
        scratch_shapes=[pltpu.VMEM((2, tq, LANES), F32), pltpu.VMEM((2, tq, LANES), F32)],
        compiler_params=_params(("parallel", "parallel", "arbitrary")),
    )(q_cat, kvup, kvup, kr, z)


def _mla_dq(q_cat, kvup, kr, do, o, lse, tabs, tq):
    bl, t, _ = q_cat.shape
    nq = t // tq
    pairs = B_HEADS // 2
    v_blk0 = (B_HEADS * LANES) // LANES

    def body(q_ref, k_ref, v_ref, kr_ref, do_ref, o_ref, lse_ref, tc, tsa, tsb, dq_ref, drow_ref, acc_ref):
        qi = pl.program_id(2)
        dov, ov, lsev = do_ref[...], o_ref[...], lse_ref[...]
        qs = [q_ref[:, e * LANES:(e + 1) * LANES] for e in range(2)]
        terms = [_head_terms(dov, ov, lsev, e) for e in range(2)]
        row = lax.broadcasted_iota(jnp.int32, (tq, tq), 0)
        col = lax.broadcasted_iota(jnp.int32, (tq, tq), 1)
        tri = col <= row
        for e in range(2):
            acc_ref[e] = jnp.zeros((tq, LANES), F32)

        def tile(k0, w, masked):
            krv = kr_ref[pl.ds(k0, w), :]
            v = v_ref[pl.ds(k0, w), :]
            ks = [k_ref[pl.ds(k0, w), e * LANES:(e + 1) * LANES] + krv for e in range(2)]
            ss = [_dot_nt(qs[e], ks[e]) for e in range(2)]
            dps = [_dot_nt(terms[e][0], v) for e in range(2)]
            for e in range(2):
                s = ss[e]
                if masked:
                    r = lax.broadcasted_iota(jnp.int32, (tq, w), 0)
                    c = lax.broadcasted_iota(jnp.int32, (tq, w), 1)
                    s = jnp.where(c <= r + (w - tq), s, NEG)
                p = jnp.exp2(s - terms[e][2])
                ds = (p * (dps[e] - terms[e][1])).astype(BF16)
                acc_ref[e] += _dot_nn(ds, ks[e])

        def step(kb2, carry):
            tile(pl.multiple_of(kb2 * 2 * tq, 2 * tq), 2 * tq, False)
            return carry

        lax.fori_loop(0, qi // 2, step, 0)

        @pl.when(qi % 2 == 1)
        def _():
            tile(pl.multiple_of((qi - 1) * tq, tq), 2 * tq, True)

        @pl.when(qi % 2 == 0)
        def _():
            tile(pl.multiple_of(qi * tq, tq), tq, True)

        for e in range(2):
            dq_ref[:, e * LANES:(e + 1) * LANES] = _rope_apply(acc_ref[e] * B_SCALE, tc[...], tsa[...], tsb[...], -1).astype(BF16)
            drow_ref[e:e + 1, :] = _col_to_row(terms[e][1], tq)

    blk = pl.BlockSpec((None, tq, LANES), lambda b, j, i: (b, i, j))
    tab = pl.BlockSpec((None, tq, LANES), lambda b, j, i: (b, i, 0))
    qblk = pl.BlockSpec((None, tq, 2 * LANES), lambda b, j, i: (b, i, j))
    return pl.pallas_call(
        body, name="mla_dq", grid=(bl, pairs, nq),
        in_specs=[qblk,
                  pl.BlockSpec((None, t, 2 * LANES), lambda b, j, i: (b, 0, j)),
                  pl.BlockSpec((None, t, LANES), lambda b, j, i: (b, 0, v_blk0 + j)),
                  pl.BlockSpec((None, t, LANES), lambda b, j, i: (b, 0, 0)),
                  blk, blk, blk, tab, tab, tab],
        out_specs=(qblk, pl.BlockSpec((None, None, None, 2, tq), lambda b, j, i: (b, j, i, 0, 0))),
        out_shape=(jax.ShapeDtypeStruct((bl, t, B_HEADS * LANES), BF16),
                   jax.ShapeDtypeStruct((bl, pairs, nq, 2, tq), F32)),
        scratch_shapes=[pltpu.VMEM((2, tq, LANES), F32)],
        compiler_params=_params(("parallel", "parallel", "arbitrary")),
    )(q_cat, kvup, kvup, kr, do, o, lse, *tabs)


def _mla_dkv(q_cat, kvup, kr, do, lse_rows, delta_rows, tq):
    bl, t, _ = q_cat.shape
    nq = t // tq
    pairs = B_HEADS // 2
    v_blk0 = (B_HEADS * LANES) // LANES

    def body(q_ref, k_ref, v_ref, kr_ref, do_ref, lrow_ref, drow_ref, dk_ref, dv_ref, acc_ref):
        kb = pl.program_id(2)
        v = v_ref[...]
        krv = kr_ref[...]
        ks = [k_ref[:, e * LANES:(e + 1) * LANES] + krv for e in range(2)]
        krow = lax.broadcasted_iota(jnp.int32, (tq, tq), 0)
        qcol = lax.broadcasted_iota(jnp.int32, (tq, tq), 1)
        tri = krow <= qcol
        lane = lax.broadcasted_iota(jnp.int32, (tq, LANES), 1)
        mine = [lane < B_VDIM, lane >= B_VDIM]

        for e in range(3):
            acc_ref[e] = jnp.zeros((tq, LANES), F32)

        def tile(qb, nblk, masked):
            w = nblk * tq
            rows = pl.ds(pl.multiple_of(qb * tq, tq), w)
            dov = do_ref[rows, :]
            lane_w = lax.broadcasted_iota(jnp.int32, (w, LANES), 1)
            mine_w = [lane_w < B_VDIM, lane_w >= B_VDIM]
            qs = [q_ref[rows, e * LANES:(e + 1) * LANES] for e in range(2)]
            does = [jnp.where(mine_w[e], dov, jnp.zeros_like(dov)) for e in range(2)]
            sts = [_dot_nt(ks[e], qs[e]) for e in range(2)]
            dpts = [_dot_nt(v, does[e]) for e in range(2)]

            def rows_of(ref, e):
                return jnp.concatenate([ref[qb + i, e:e + 1, :] for i in range(nblk)], axis=1)

            pts = []
            for e in range(2):
                st = sts[e]
                if masked:
                    r = lax.broadcasted_iota(jnp.int32, (tq, w), 0)
                    c = lax.broadcasted_iota(jnp.int32, (tq, w), 1)
                    st = jnp.where(r <= c, st, NEG)
                pts.append(jnp.exp2(st - rows_of(lrow_ref, e)))
            acc_ref[2] += _dot_nn(pts[0].astype(BF16), does[0]) + _dot_nn(pts[1].astype(BF16), does[1])
            for e in range(2):
                dst = (pts[e] * (dpts[e] - rows_of(drow_ref, e))).astype(BF16)
                acc_ref[e] += _dot_nn(dst, qs[e])

        rest = nq - 1 - kb
        odd = rest % 2

        @pl.when(odd == 1)
        def _():
            tile(kb, 2, True)

        @pl.when(odd == 0)
        def _():
            tile(kb, 1, True)

        def step(i, carry):
            tile(kb + 1 + odd + 2 * i, 2, False)
            return carry

        lax.fori_loop(0, rest // 2, step, 0)
        dk_ref[:, 0:LANES] = (acc_ref[0] * LN2).astype(BF16)
        dk_ref[:, LANES:2 * LANES] = (acc_ref[1] * LN2).astype(BF16)
        dv_ref[...] = acc_ref[2].astype(BF16)

    full = pl.BlockSpec((None, t, LANES), lambda b, j, i: (b, 0, j))
    rows = pl.BlockSpec((None, None, nq, 2, tq), lambda b, j, i: (b, j, 0, 0, 0))
    kblk = pl.BlockSpec((None, tq, 2 * LANES), lambda b, j, i: (b, i, j))
    return pl.pallas_call(
        body, name="mla_dkv", grid=(bl, pairs, nq),
        in_specs=[pl.BlockSpec((None, t, 2 * LANES), lambda b, j, i: (b, 0, j)),
                  kblk,
                  pl.BlockSpec((None, tq, LANES), lambda b, j, i: (b, i, v_blk0 + j)),
                  pl.BlockSpec((None, tq, LANES), lambda b, j, i: (b, i, 0)),
                  full, rows, rows],
        out_specs=(kblk, pl.BlockSpec((None, tq, LANES), lambda b, j, i: (b, i, j))),
        out_shape=(jax.ShapeDtypeStruct((bl, t, B_HEADS * LANES), BF16),
                   jax.ShapeDtypeStruct((bl, t, B_WIDTH), BF16)),
        scratch_shapes=[pltpu.VMEM((3, tq, LANES), F32)],
        compiler_params=_params(("parallel", "parallel", "arbitrary")),
    )(q_cat, kvup, kvup, kr, do, lse_rows, delta_rows)


def _adamw(w, g, m, v, *, name):
    r, c = w.shape
    tr = _row_tile(r, 256)
    c1 = 1.0 - ADAM_B1
    c2 = 1.0 - ADAM_B2
    bc1 = 1.0 - ADAM_B1 ** ADAM_STEP
    bc2 = 1.0 - ADAM_B2 ** ADAM_STEP

    def body(w_ref, g_ref, m_ref, v_ref, d_ref, nm_ref, nv_ref):
        gv = g_ref[...]
        nm = ADAM_B1 * m_ref[...] + c1 * gv
        nv = ADAM_B2 * v_ref[...] + c2 * (gv * gv)
        nm_ref[...] = nm
        nv_ref[...] = nv
        d_ref[...] = -ADAM_LR * ((nm / bc1) / (jnp.sqrt(nv / bc2) + ADAM_EPS) + ADAM_WD * w_ref[...])

    blk = pl.BlockSpec((tr, c), lambda i: (i, 0))
    sds = jax.ShapeDtypeStruct((r, c), F32)
    return pl.pallas_call(
        body, name=name, grid=(r // tr,), in_specs=[blk] * 4, out_specs=(blk,) * 3,
        out_shape=(sds,) * 3, compiler_params=_params(("parallel",)),
    )(w, g, m, v)


def _add_my_half(stacked, other, core, out_dtype, *, name):
    nch, a, c = stacked.shape
    h = a // 2
    tr = _row_tile(h, 256)
    nblk = h // tr

    def body(core_ref, s_ref, p_ref, o_ref):
        o_ref[...] = (s_ref[...] + p_ref[...]).astype(o_ref.dtype)

    return pl.pallas_call(
        body, name=name,
        grid_spec=pltpu.PrefetchScalarGridSpec(
            num_scalar_prefetch=1, grid=(nch, nblk),
            in_specs=[pl.BlockSpec((None, tr, c), lambda k, i, cr: (k, cr[0] * nblk + i, 0)),
                      pl.BlockSpec((None, tr, c), lambda k, i, cr: (k, i, 0))],
            out_specs=pl.BlockSpec((None, tr, c), lambda k, i, cr: (k, i, 0))),
        out_shape=jax.ShapeDtypeStruct((nch, h, c), out_dtype),
        compiler_params=_params(("parallel", "parallel")),
    )(core, stacked, other)


def _sum_chips(parts, own, chip, *, name):
    nch, h, c = parts.shape
    tr = _row_tile(h, 256)

    def body(chip_ref, p_ref, own_ref, o_ref):
        me = chip_ref[0]

        def slot(k):
            return jnp.where(me == k, own_ref[k].astype(F32), p_ref[k].astype(F32))

        acc = slot(0) + slot(1)
        for k in range(2, nch):
            acc = acc + slot(k)
        o_ref[...] = acc

    blk = pl.BlockSpec((nch, tr, c), lambda i, cr: (0, i, 0))
    return pl.pallas_call(
        body, name=name,
        grid_spec=pltpu.PrefetchScalarGridSpec(
            num_scalar_prefetch=1, grid=(h // tr,), in_specs=[blk, blk],
            out_specs=pl.BlockSpec((tr, c), lambda i, cr: (i, 0))),
        out_shape=jax.ShapeDtypeStruct((h, c), F32), compiler_params=_params(("parallel",)),
    )(chip, parts, own)


def _join_halves(mine, other, core, *, name):
    h, c = mine.shape
    tr = _row_tile(h, 256)
    nblk = h // tr

    def body(core_ref, m_ref, s_ref, o_ref):
        is_mine = pl.program_id(0) // nblk == core_ref[0]

        @pl.when(is_mine)
        def _():
            o_ref[...] = m_ref[...]

        @pl.when(jnp.logical_not(is_mine))
        def _():
            o_ref[...] = s_ref[...]

    blk = pl.BlockSpec((tr, c), lambda i, cr: (i % nblk, 0))
    return pl.pallas_call(
        body, name=name,
        grid_spec=pltpu.PrefetchScalarGridSpec(
            num_scalar_prefetch=1, grid=(2 * nblk,), in_specs=[blk, blk],
            out_specs=pl.BlockSpec((tr, c), lambda i, cr: (i, 0))),
        out_shape=jax.ShapeDtypeStruct((2 * h, c), F32), compiler_params=_params(("arbitrary",)),
    )(core, mine, other)


def _place():
    x, y, c = lax.axis_index("x"), lax.axis_index("y"), lax.axis_index("c")
    chips = [(1 - x, y), (x, 1 - y), (1 - x, 1 - y)]
    return x, y, c, chips


def _remote(src, dst, send_sems, recv_sems, k, to):
    return pltpu.make_async_remote_copy(src_ref=src, dst_ref=dst, send_sem=send_sems.at[k],
                                        recv_sem=recv_sems.at[k], device_id=to, device_id_type=MESH)


def _hbm_call(body, name, ins, out_shapes, n_remote):
    any_spec = pl.BlockSpec(memory_space=pl.ANY)
    return pl.pallas_call(
        body, name=name, in_specs=[any_spec] * len(ins), out_specs=tuple([any_spec] * len(out_shapes)),
        out_shape=tuple(out_shapes),
        scratch_shapes=[pltpu.SemaphoreType.DMA((n_remote,)), pltpu.SemaphoreType.DMA((n_remote,))],
    )(*ins)


def _all_gather_chips(shards, *, name):
    n = len(shards)

    def body(*refs):
        ins, outs = refs[:n], refs[n:2 * n]
        send_sems, recv_sems = refs[2 * n:]
        x, y, c, chips = _place()
        me = 2 * x + y
        sent = []
        for s in range(n):
            h = ins[s].shape[0] // 2
            for j, (px, py) in enumerate(chips):
                cp = _remote(ins[s].at[pl.ds(c * h, h)], outs[s].at[me, pl.ds(c * h, h)],
                             send_sems, recv_sems, s * 6 + j, (px, py, c))
                cp.start()
                sent.append(cp)
        for s in range(n):
            h = ins[s].shape[0] // 2
            for j, (px, py) in enumerate(chips):
                slab = outs[s].at[2 * px + py, pl.ds(c * h, h)]
                _remote(slab, slab, send_sems, recv_sems, s * 6 + j, (px, py, c)).wait_recv()
                cp = _remote(slab, slab, send_sems, recv_sems, s * 6 + 3 + j, (x, y, 1 - c))
                cp.start()
                sent.append(cp)
        for s in range(n):
            h = ins[s].shape[0] // 2
            for j, (px, py) in enumerate(chips):
                slab = outs[s].at[2 * px + py, pl.ds((1 - c) * h, h)]
                _remote(slab, slab, send_sems, recv_sems, s * 6 + 3 + j, (x, y, 1 - c)).wait_recv()
        for cp in sent:
            cp.wait_send()

    out_shapes = [jax.ShapeDtypeStruct((N_CHIPS,) + s.shape, s.dtype) for s in shards]
    return _hbm_call(body, name, shards, out_shapes, 6 * n)


def _pair_send_other_half(stacked, *, name):
    n = len(stacked)

    def body(*refs):
        ins, outs = refs[:n], refs[n:2 * n]
        send_sems, recv_sems = refs[2 * n:]
        x, y, c, _chips = _place()
        sent = []
        for s in range(n):
            h = ins[s].shape[1] // 2
            cp = _remote(ins[s].at[:, pl.ds((1 - c) * h, h)], outs[s], send_sems, recv_sems, s, (x, y, 1 - c))
            cp.start()
            sent.append(cp)
        for cp in sent:
            cp.wait_recv()
        for cp in sent:
            cp.wait_send()

    out_shapes = [jax.ShapeDtypeStruct((s.shape[0], s.shape[1] // 2, s.shape[2]), s.dtype) for s in stacked]
    return _hbm_call(body, name, stacked, out_shapes, n)


def _chip_exchange(halves, *, name):
    n = len(halves)

    def body(*refs):
        ins, outs = refs[:n], refs[n:2 * n]
        send_sems, recv_sems = refs[2 * n:]
        x, y, c, chips = _place()
        me = 2 * x + y
        sent = []
        for s in range(n):
            for j, (px, py) in enumerate(chips):
                cp = _remote(ins[s].at[2 * px + py], outs[s].at[me], send_sems, recv_sems, s * 3 + j, (px, py, c))
                cp.start()
                sent.append(cp)
        for s in range(n):
            for j, (px, py) in enumerate(chips):
                slab = outs[s].at[2 * px + py]
                _remote(slab, slab, send_sems, recv_sems, s * 3 + j, (px, py, c)).wait_recv()
        for cp in sent:
            cp.wait_send()

    out_shapes = [jax.ShapeDtypeStruct(s.shape, s.dtype) for s in halves]
    return _hbm_call(body, name, halves, out_shapes, 3 * n)


def _chip_exchange_start(halves, *, name):
    n = len(halves)
    hbm = pl.BlockSpec(memory_space=pltpu.HBM)
    sem = pl.BlockSpec(memory_space=pltpu.SEMAPHORE)

    def body(*refs):
        ins, lands = refs[:n], refs[n:2 * n]
        send_sems, recv_sems = refs[2 * n], refs[2 * n + 1]
        token = refs[-1]
        x, y, c, chips = _place()
        me = 2 * x + y
        for s in range(n):
            for j, (px, py) in enumerate(chips):
                _remote(ins[s].at[2 * px + py], lands[s].at[me], send_sems, recv_sems, s * 3 + j, (px, py, c)).start()
        token[...] = jnp.zeros_like(token)

    slabs = [pltpu.HBM(s.shape, s.dtype) for s in halves]
    outs = pl.pallas_call(
        body, name=name,
        out_shape=(pltpu.SemaphoreType.DMA((3 * n,)), pltpu.SemaphoreType.DMA((3 * n,)), *slabs, *slabs,
                   jax.ShapeDtypeStruct((8, LANES), F32)),
        in_specs=[hbm] * (2 * n), out_specs=(sem, sem, *([hbm] * (2 * n)), pl.BlockSpec(memory_space=pltpu.VMEM)),
        input_output_aliases={i: 2 + i for i in range(2 * n)},
        compiler_params=pltpu.CompilerParams(has_side_effects=pltpu.SideEffectType.DATAFLOW_SIDE_EFFECTING),
    )(*[pltpu.with_memory_space_constraint(s, pltpu.HBM) for s in halves],
      *[pltpu.with_memory_space_constraint(lax.empty(s.shape, s.dtype), pltpu.HBM) for s in halves])
    return outs[0], outs[1], list(outs[2:2 + n]), list(outs[2 + n:2 + 2 * n]), outs[-1]


def _chip_exchange_wait(send_sems, recv_sems, sent, lands, after, *, name):
    n = len(sent)
    hbm = pl.BlockSpec(memory_space=pltpu.HBM)
    sem = pl.BlockSpec(memory_space=pltpu.SEMAPHORE)

    def body(*refs):
        ins, lands_in = refs[:n], refs[n:2 * n]
        send_sems, recv_sems = refs[2 * n], refs[2 * n + 1]
        x, y, c, chips = _place()
        me = 2 * x + y
        for s in range(n):
            for j, (px, py) in enumerate(chips):
                k = 2 * px + py
                _remote(ins[s].at[k], lands_in[s].at[me], send_sems, recv_sems, s * 3 + j, (px, py, c)).wait_send()
                _remote(ins[s].at[k], lands_in[s].at[k], send_sems, recv_sems, s * 3 + j, (px, py, c)).wait_recv()

    slabs = [pltpu.HBM(s.shape, s.dtype) for s in sent]
    outs = pl.pallas_call(
        body, name=name, out_shape=(*slabs, *slabs),
        in_specs=[hbm] * (2 * n) + [sem, sem, pl.BlockSpec(memory_space=pl.ANY)],
        out_specs=tuple([hbm] * (2 * n)), input_output_aliases={i: i for i in range(2 * n)},
        compiler_params=pltpu.CompilerParams(has_side_effects=pltpu.SideEffectType.DATAFLOW_SIDE_EFFECTING),
    )(*sent, *lands, send_sems, recv_sems, after)
    return list(outs[n:])


def _pair_swap(halves, *, name):
    n = len(halves)

    def body(*refs):
        ins, outs = refs[:n], refs[n:2 * n]
        send_sems, recv_sems = refs[2 * n:]
        x, y, c, _chips = _place()
        sent = []
        for s in range(n):
            cp = _remote(ins[s], outs[s], send_sems, recv_sems, s, (x, y, 1 - c))
            cp.start()
            sent.append(cp)
        for cp in sent:
            cp.wait_recv()
        for cp in sent:
            cp.wait_send()

    out_shapes = [jax.ShapeDtypeStruct(s.shape, s.dtype) for s in halves]
    return _hbm_call(body, name, halves, out_shapes, n)


def _pack_rows(parts, row_multiple):
    flat = jnp.concatenate([p.reshape(-1) for p in parts])
    quantum = row_multiple * PACK_COLS
    pad = (-flat.shape[0]) % quantum
    flat = jnp.pad(flat, (0, pad))
    return flat.reshape(-1, PACK_COLS)


def _unpack(flat, shapes):
    out, pos = [], 0
    for shp in shapes:
        size = math.prod(shp)
        out.append(flat[pos:pos + size].reshape(shp))
        pos += size
    return out


def _to_chunks_cols(full):
    r, c4 = full.shape
    return full.reshape(r, N_CHIPS, c4 // N_CHIPS).transpose(1, 0, 2)


def _from_chunks_cols(stacked):
    nch, r, c = stacked.shape
    return stacked.transpose(1, 0, 2).reshape(r, nch * c)


def _class_major(a, bl, t, dil):
    w = a.shape[-1]
    if dil == 1:
        return a.reshape(bl, 1, t, w)
    return a.reshape(bl, t // dil, dil, w).transpose(0, 2, 1, 3)


def _natural(a):
    bl, dil, ln, w = a.shape
    if dil == 1:
        return a.reshape(bl * ln, w)
    return a.transpose(0, 2, 1, 3).reshape(bl * ln * dil, w)


def _train_step(x, positions, a_pre_norm, a_w_in, a_w_out, a_post_norm, kv_norm, kv_w_down, kv_latent_norm,
                kv_w_up, b_pre_norm, b_w_in, b_q_norm, b_w_q_up, b_w_out, b_post_norm, loss_target, moments):
    bl, t, d = x.shape
    n = bl * t
    qb = t // A_DILATIONS[-1]
    tq = _tile(t, 256)
    dq4 = d // N_CHIPS
    chip = 2 * lax.axis_index("x") + lax.axis_index("y")
    chip_arr = chip.astype(jnp.int32).reshape(1)
    core_arr = lax.axis_index("c").astype(jnp.int32).reshape(1)

    w_in_a_s = a_w_in[0].astype(BF16)
    outs_s = jnp.concatenate([a_w_out[0], b_w_out[0]], axis=0).astype(BF16)
    small_shapes = [kv_w_down.shape, kv_w_up.shape, b_w_in[0].shape, b_w_q_up[0].shape]
    small_s = _pack_rows([kv_w_down, kv_w_up, b_w_in[0], b_w_q_up[0]], 32).astype(BF16)
    gains_s = jnp.pad(jnp.concatenate([a_pre_norm[0], a_post_norm[0]]), (0, 16 * LANES - 2 * dq4)).reshape(16, LANES)
    shards = [w_in_a_s, outs_s, small_s, gains_s]
    gathered = _all_gather_chips(shards, name="gather_weights")
    g_in_a, g_outs, g_small, g_gains = [lax.dynamic_update_index_in_dim(g, s, chip, 0)
                                        for g, s in zip(gathered, shards)]

    w_in_a = _from_chunks_cols(g_in_a)
    w_out_a = g_outs[:, :A_WIDTH // N_CHIPS].reshape(A_WIDTH, d)
    w_out_b = g_outs[:, A_WIDTH // N_CHIPS:].reshape(B_WIDTH, d)
    sm = [_unpack(g_small[k].reshape(-1), small_shapes) for k in range(N_CHIPS)]
    w_down = jnp.concatenate([sm[k][0] for k in range(N_CHIPS)], axis=0)
    w_up = jnp.concatenate([sm[k][1] for k in range(N_CHIPS)], axis=1)
    w_in_b = jnp.concatenate([sm[k][2] for k in range(N_CHIPS)], axis=1)
    w_q_up = jnp.concatenate([sm[k][3] for k in range(N_CHIPS)], axis=1)
    gflat = g_gains.reshape(N_CHIPS, -1)
    g_a_pre = gflat[:, :dq4].reshape(1, d)
    g_a_post = gflat[:, dq4:2 * dq4].reshape(1, d)

    w_up_h = w_up.reshape(B_KV_LORA, B_HEADS, B_NOPE + B_VDIM)
    w_up_k = jnp.pad(w_up_h[:, :, :B_NOPE], ((0, 0), (0, 0), (0, LANES - B_NOPE))).reshape(B_KV_LORA, B_HEADS * LANES)
    w_up_v = w_up_h[:, :, B_NOPE:].reshape(B_KV_LORA, B_WIDTH)
    w_up_cat = jnp.concatenate([w_up_k, w_up_v], axis=1)
    w_q_up_p = jnp.pad(w_q_up.reshape(B_Q_LORA, B_HEADS, B_QK_DIM),
                       ((0, 0), (0, 0), (0, LANES - B_QK_DIM))).reshape(B_Q_LORA, B_HEADS * LANES)
    zeros_d = lambda c: jnp.zeros((d, c), BF16)
    w_down_p = jnp.concatenate([w_down[:, :B_KV_LORA], zeros_d(B_NOPE), w_down[:, B_KV_LORA:],
                                zeros_d(LANES - B_NOPE - B_ROPE)], axis=1)
    w_cq = w_in_b[:, :B_Q_LORA]
    w_z = w_in_b[:, B_Q_LORA:]

    tabs_a = _rope_tables(positions, A_ROPE_THETA, 0)
    tabs_b = _rope_tables(positions, B_ROPE_THETA, B_NOPE)

    h0 = x.reshape(n, d)
    hn_a = _rms_fwd(h0, g_a_pre, BF16, name="a_pre_norm", tr=1024)
    is_qk = lambda j: j != 2
    is_q = lambda j: j == 0
    z_blk_a = 3 * A_GROUPS
    z_a = _matmul(hn_a, w_in_a, "nn", BF16, name="a_proj_z", b_cols=(z_blk_a, 1))
    o_groups, lse_groups, qkv_cm, hn_cm, tabs_cm = [], [], [], [], []
    for g, dil in enumerate(A_DILATIONS):
        flat = lambda a: _class_major(a, bl, t, dil).reshape(n, a.shape[-1])
        hn_g = hn_a if dil == 1 else flat(hn_a)
        tabs_g = tabs_a if dil == 1 else lax.optimization_barrier(tuple(flat(tb) for tb in tabs_a))
        proj_g = _matmul(hn_g, w_in_a, "nn", BF16, name=f"a_proj_{g}", rope=(tabs_g, is_qk),
                         out_scale=(A_SCALE * LOG2E, is_q), b_cols=(3 * g, 3))
        src = proj_g.reshape(bl, dil, t // dil, 3 * A_WIDTH)
        hn_cm.append(hn_g)
        tabs_cm.append(tabs_g)
        qkv_cm.append(src)
        o_g, lse_g = _attn_a_fwd(src, 0, qb, BF16, name=f"attn_a_fwd_{g}")
        o_groups.append(_natural(o_g))
        lse_groups.append(_natural(lse_g))
    ypre_a, om_a, lse_a = _merge_gate_fwd(o_groups, lse_groups, z_a, 0)
    y_a = _matmul(ypre_a, w_out_a, "nn", F32, name="a_out")
    g_kvn = kv_norm.reshape(1, d)
    g_lat = kv_latent_norm.reshape(1, B_KV_LORA)
    h1, hn_kv, hn_b = _post_norm_block(y_a, g_a_post, h0, [g_kvn, b_pre_norm], name="a_post_norm")

    ckr = _matmul(hn_kv, w_down_p, "nn", F32, name="kv_down")
    c_kv, k_rope = _kv_latent_fwd(ckr, g_lat, tabs_b)
    kvup = _matmul(c_kv, w_up_cat, "nn", BF16, name="kv_up")
    z_b = _matmul(hn_b, w_z, "nn", BF16, name="b_proj_z")
    cq_raw = _matmul(hn_b, w_cq, "nn", F32, name="b_proj_q")
    c_q = _rms_fwd(cq_raw, b_q_norm, BF16, name="b_q_norm", tr=1024)
    always = lambda j: True
    q_cat = _matmul(c_q, w_q_up_p, "nn", BF16, name="b_q_up", rope=(tabs_b, always),
                    out_scale=(B_SCALE * LOG2E, always))
    r3 = lambda a: a.reshape(bl, t, a.shape[-1])
    tabs_b3 = tuple(r3(tb) for tb in tabs_b)
    ypre_b, o_b, lse_b, lse_rows_b = _mla_fwd(r3(q_cat), r3(kvup), r3(k_rope), r3(z_b), tq)
    y_b = _matmul(ypre_b.reshape(n, B_WIDTH), w_out_b, "nn", F32, name="b_out")
    dh2, loss_part = _post_norm_loss(y_b, b_post_norm, h1, loss_target.reshape(n, d))

    dy_b, dg_b_post = _rms_bwd(y_b, b_post_norm, dh2, BF16, name="b_post_norm_bwd", tr=1024)
    dypre_b = _matmul(dy_b, w_out_b, "nt", BF16, name="b_out_dx")
    dw_out_b = _matmul(ypre_b.reshape(n, B_WIDTH), dy_b, "tn", F32, name="b_out_dw", tm=1024, tk=4096)
    do_b, dz_b = _gate_bwd(dypre_b, o_b.reshape(n, B_WIDTH), z_b, 0, name="b_gate_bwd", with_delta=False)
    dq_cat, delta_rows_b = _mla_dq(r3(q_cat), r3(kvup), r3(k_rope), r3(do_b), o_b, lse_b, tabs_b3, tq)
    dq_cat = dq_cat.reshape(n, -1)
    dk_cat, dv_b = _mla_dkv(r3(q_cat), r3(kvup), r3(k_rope), r3(do_b), lse_rows_b, delta_rows_b, tq)
    dk_cat, dv_b = dk_cat.reshape(n, -1), dv_b.reshape(n, -1)
    dcq_n = _matmul(dq_cat, w_q_up_p, "nt", F32, name="b_q_up_dx")
    dw_q_up_p = _matmul(c_q, dq_cat, "tn", F32, name="b_q_up_dw", tm=1024, tk=4096)
    dcq, dg_b_q = _rms_bwd(cq_raw, b_q_norm, dcq_n, BF16, name="b_q_norm_bwd", tr=1024)
    dhn_b = _matmul(dz_b, w_z, "nt", F32, name="b_proj_z_dx")
    dhn_b = _matmul(dcq, w_cq, "nt", F32, name="b_proj_q_dx", add=dhn_b)
    dw_z = _matmul(hn_b, dz_b, "tn", F32, name="b_proj_z_dw", tm=1024, tk=4096)
    dw_cq = _matmul(hn_b, dcq, "tn", F32, name="b_proj_q_dw", tm=1024, tk=4096)
    dckv_n = _matmul(dk_cat, w_up_k, "nt", F32, name="kv_up_k_dx")
    dckv_n = _matmul(dv_b, w_up_v, "nt", F32, name="kv_up_v_dx", add=dckv_n)
    dw_up_k = _matmul(c_kv, dk_cat, "tn", F32, name="kv_up_k_dw", tm=1024, tk=4096)
    dw_up_v = _matmul(c_kv, dv_b, "tn", F32, name="kv_up_v_dw", tm=1024, tk=4096)
    dckr, dg_lat = _kv_latent_bwd(dckv_n, ckr, g_lat, dk_cat, tabs_b)
    dhn_kv = _matmul(dckr, w_down_p, "nt", F32, name="kv_down_dx")
    dw_down_p = _matmul(hn_kv, dckr, "tn", F32, name="kv_down_dw", tm=1024, tk=4096)
    dh1, dg_b_pre, dg_kvn = _rms_bwd_pair(h1, b_pre_norm, dhn_b, g_kvn, dhn_kv, dh2, name="h1_norms_bwd")

    dy_a, dg_a_post = _rms_bwd(y_a, g_a_post, dh1, BF16, name="a_post_norm_bwd", tr=1024)
    dypre_a = _matmul(dy_a, w_out_a, "nt", BF16, name="a_out_dx")
    dw_out_a = _matmul(ypre_a, dy_a, "tn", F32, name="a_out_dw", tm=1024, tk=4096)
    do_a, dz_a, delta_a = _gate_bwd(dypre_a, om_a, z_a, 0, name="a_gate_bwd", with_delta=True)
    dw_cols = A_IN_WIDTH // N_CHIPS
    dw_tn = _tile(dw_cols, 512)
    dw_kwargs = dict(tm=1024, tn=dw_tn, tk=4096, out_chunk_blocks=dw_cols // dw_tn)
    r_big = _matmul(hn_a, dz_a, "tn", F32, name="a_proj_dw_z", out_full=(N_CHIPS, d, dw_cols),
                    out_joff=z_blk_a * A_WIDTH // dw_tn, **dw_kwargs)
    dqkvs = []
    for g, dil in enumerate(A_DILATIONS):
        cm = lambda a: _class_major(a, bl, t, dil)
        swap = lambda a: jnp.swapaxes(a, 2, 3)
        lse_cm, delta_cm = cm(lse_a), cm(delta_a)
        tabs_g = tuple(tb.reshape(bl, dil, t // dil, LANES) for tb in tabs_cm[g])
        dqkv = _attn_a_bwd(qkv_cm[g], 0, cm(do_a), lse_cm, delta_cm, swap(lse_cm), swap(delta_cm),
                           tabs_g, qb, name=f"attn_a_bwd_{g}").reshape(n, 3 * A_WIDTH)
        dqkvs.append(dqkv)
        r_big = _matmul(hn_cm[g], dqkv, "tn", F32, name=f"a_proj_dw_{g}", out_into=r_big,
                        out_joff=3 * g * A_WIDTH // dw_tn, **dw_kwargs)
    r_outs = jnp.concatenate([dw_out_a.reshape(N_CHIPS, A_WIDTH // N_CHIPS, d),
                              dw_out_b.reshape(N_CHIPS, B_WIDTH // N_CHIPS, d)], axis=1)

    bulk = [r_big, r_outs]
    recv_b = _pair_send_other_half(bulk, name="reduce_pair_send")
    halves_b = [_add_my_half(s, p, core_arr, BF16, name=f"reduce_pair_add_{i}")
                for i, (s, p) in enumerate(zip(bulk, recv_b))]
    send_sems, recv_sems, sent_b, lands_b, token = _chip_exchange_start(halves_b, name="reduce_exchange_start")

    dhn_a = _matmul(dz_a, w_in_a, "nt", F32, name="a_proj_dx_z", b_koff=z_blk_a, after=token)
    dhn_more = []
    for g, dil in enumerate(A_DILATIONS):
        tk_dx = 3 * A_WIDTH
        if dil == 1:
            dhn_a = _matmul(dqkvs[g], w_in_a, "nt", F32, name=f"a_proj_dx_{g}", add=dhn_a, tk=tk_dx, b_koff=g,
                            after=token)
        else:
            part = _matmul(dqkvs[g], w_in_a, "nt", BF16, name=f"a_proj_dx_{g}", tk=tk_dx, b_koff=g, after=token)
            dhn_more.append(_natural(part.reshape(bl, dil, t // dil, d)))
    grad_x, dg_a_pre = _rms_bwd(h0, g_a_pre, dhn_a, F32, name="a_pre_norm_bwd", adds=(dh1,),
                                dy_more=tuple(dhn_more))

    dw_up = jnp.concatenate([dw_up_k.reshape(B_KV_LORA, B_HEADS, LANES)[:, :, :B_NOPE],
                             dw_up_v.reshape(B_KV_LORA, B_HEADS, B_VDIM)], axis=2).reshape(B_KV_LORA, -1)
    dw_q_up = dw_q_up_p.reshape(B_Q_LORA, B_HEADS, LANES)[:, :, :B_QK_DIM].reshape(B_Q_LORA, -1)
    dw_down = jnp.concatenate([dw_down_p[:, :B_KV_LORA], dw_down_p[:, B_KV_LORA + B_NOPE:B_KV_LORA + B_NOPE + B_ROPE]], axis=1)
    dw_in_b = jnp.concatenate([dw_cq, dw_z], axis=1)
    vec_rep = [dg_kvn.reshape(-1), dg_lat.reshape(-1), dg_b_pre.reshape(-1), dg_b_q.reshape(-1),
               dg_b_post.reshape(-1), loss_part.reshape(-1)]
    vec_shapes = [(dq4,), (dq4,)] + [v.shape for v in vec_rep]
    down_c = dw_down.reshape(N_CHIPS, dq4, -1)
    up_c = _to_chunks_cols(dw_up)
    inb_c = _to_chunks_cols(dw_in_b)
    qup_c = _to_chunks_cols(dw_q_up)
    small_chunks = []
    for k in range(N_CHIPS):
        vecs = [dg_a_pre.reshape(-1)[k * dq4:(k + 1) * dq4], dg_a_post.reshape(-1)[k * dq4:(k + 1) * dq4]] + vec_rep
        small_chunks.append(_pack_rows([down_c[k], up_c[k], inb_c[k], qup_c[k]] + vecs, 32))
    r_small = jnp.stack(small_chunks)

    recv_s = _pair_send_other_half([r_small], name="reduce_pair_send_small")
    halves_s = [_add_my_half(r_small, recv_s[0], core_arr, F32, name="reduce_pair_add_small")]
    parts_s = list(_chip_exchange(halves_s, name="reduce_exchange_small"))
    parts_b = _chip_exchange_wait(send_sems, recv_sems, sent_b, lands_b, grad_x, name="reduce_exchange_wait")
    sums = [_sum_chips(p, own, chip_arr, name=f"reduce_chip_sum_{i}")
            for i, (p, own) in enumerate(zip(parts_b + parts_s, sent_b + halves_s))]
    others = _pair_swap(sums, name="reduce_pair_swap")
    g_big, g_outs_r, g_small_r = [_join_halves(m, o, core_arr, name=f"reduce_join_{i}")
                                  for i, (m, o) in enumerate(zip(sums, others))]

    grads = {}
    grads["a_w_in"] = g_big
    grads["a_w_out"] = g_outs_r[:A_WIDTH // N_CHIPS]
    grads["b_w_out"] = g_outs_r[A_WIDTH // N_CHIPS:]
    small_out_shapes = [down_c.shape[1:], up_c.shape[1:], inb_c.shape[1:], qup_c.shape[1:]] + vec_shapes
    (grads["kv_w_down"], grads["kv_w_up"], grads["b_w_in"], grads["b_w_q_up"], grads["a_pre_norm"],
     grads["a_post_norm"], grads["kv_norm"], grads["kv_latent_norm"], grads["b_pre_norm"], grads["b_q_norm"],
     grads["b_post_norm"], loss_sum) = _unpack(g_small_r.reshape(-1), small_out_shapes)

    weights = dict(a_pre_norm=a_pre_norm, a_w_in=a_w_in, a_w_out=a_w_out, a_post_norm=a_post_norm, kv_norm=kv_norm,
                   kv_w_down=kv_w_down, kv_latent_norm=kv_latent_norm, kv_w_up=kv_w_up, b_pre_norm=b_pre_norm,
                   b_w_in=b_w_in, b_q_norm=b_q_norm, b_w_q_up=b_w_q_up, b_w_out=b_w_out, b_post_norm=b_post_norm)
    names = list(weights)
    out_g, out_d, out_m, out_v = [], [], [], []
    for i, nm in enumerate(names):
        w = weights[nm]
        two_d = (1, w.shape[0]) if w.ndim == 1 else (w.shape[-2], w.shape[-1])
        gw = grads[nm].reshape(two_d)
        dlt, new_m, new_v = _adamw(w.reshape(two_d), gw, moments[i].reshape(two_d),
                                   moments[len(names) + i].reshape(two_d), name=f"adamw_{nm}")
        out_g.append(gw.reshape(w.shape))
        out_d.append(dlt.reshape(w.shape))
        out_m.append(new_m.reshape(w.shape))
        out_v.append(new_v.reshape(w.shape))
    return (loss_sum.reshape(()), grad_x.reshape(bl, t, d), *out_g, *out_d, *out_m, *out_v)


def kernel(x, positions, a_pre_norm, a_w_in, a_w_out, a_post_norm, kv_norm, kv_w_down, kv_latent_norm, kv_w_up, b_pre_norm, b_w_in, b_q_norm, b_w_q_up, b_w_out, b_post_norm, loss_target, m_a_pre_norm, m_a_w_in, m_a_w_out, m_a_post_norm, m_kv_norm, m_kv_w_down, m_kv_latent_norm, m_kv_w_up, m_b_pre_norm, m_b_w_in, m_b_q_norm, m_b_w_q_up, m_b_w_out, m_b_post_norm, v_a_pre_norm, v_a_w_in, v_a_w_out, v_a_post_norm, v_kv_norm, v_kv_w_down, v_kv_latent_norm, v_kv_w_up, v_b_pre_norm, v_b_w_in, v_b_q_norm, v_b_w_q_up, v_b_w_out, v_b_post_norm):
    moments = (m_a_pre_norm, m_a_w_in, m_a_w_out, m_a_post_norm, m_kv_norm, m_kv_w_down, m_kv_latent_norm, m_kv_w_up,
               m_b_pre_norm, m_b_w_in, m_b_q_norm, m_b_w_q_up, m_b_w_out, m_b_post_norm,
               v_a_pre_norm, v_a_w_in, v_a_w_out, v_a_post_norm, v_kv_norm, v_kv_w_down, v_kv_latent_norm, v_kv_w_up,
               v_b_pre_norm, v_b_w_in, v_b_q_norm, v_b_w_q_up, v_b_w_out, v_b_post_norm)
    return _train_step(x, positions, a_pre_norm, a_w_in, a_w_out, a_post_norm, kv_norm, kv_w_down, kv_latent_norm,
                       kv_w_up, b_pre_norm, b_w_in, b_q_norm, b_w_q_up, b_w_out, b_post_norm, loss_target, moments)
```

```python
import math

import jax
import jax.numpy as jnp
from jax import lax
from jax.experimental import pallas as pl
from jax.experimental.pallas import tpu as pltpu

F32 = jnp.float32
BF16 = jnp.bfloat16
MESH = pl.DeviceIdType.MESH

NORM_EPS = 1e-6
NEG = -1e30
LANES = 128
VMEM_LIMIT = 56 * 1024 * 1024
LOG2E = math.log2(math.e)
LN2 = math.log(2.0)

A_GROUPS = 3
A_DILATIONS = (1, 4, 16)
A_HEADS = 8
A_HEAD_DIM = 128
A_WIDTH = A_HEADS * A_HEAD_DIM
A_ROPE_THETA = 500000.0
A_IN_WIDTH = A_GROUPS * 3 * A_WIDTH + A_WIDTH
A_SCALE = A_HEAD_DIM ** -0.5

B_HEADS = 16
B_NOPE = 64
B_ROPE = 32
B_QK_DIM = B_NOPE + B_ROPE
B_VDIM = 64
B_WIDTH = B_HEADS * B_VDIM
B_Q_LORA = 384
B_KV_LORA = 256
B_ROPE_THETA = 10000.0
B_SCALE = B_QK_DIM ** -0.5

ADAM_LR = 0.001
ADAM_B1 = 0.9
ADAM_B2 = 0.999
ADAM_EPS = 1e-08
ADAM_WD = 0.01
ADAM_STEP = 10

N_CHIPS = 4
PACK_COLS = 512


def _params(sem=None):
    return pltpu.CompilerParams(dimension_semantics=sem, vmem_limit_bytes=VMEM_LIMIT)


def _tile(n, want):
    t = min(n, want)
    assert n % t == 0, (n, want)
    return t


def _row_tile(n, want):
    for t in range(min(n, want), 0, -1):
        if n % t == 0 and (t % 16 == 0 or t == n):
            return t
    return n


def _rope_tables(positions, theta, lane0):
    half = 16
    inv_freq = 1.0 / (theta ** (jnp.arange(half, dtype=F32) * (2.0 / (2 * half))))
    n = positions.size
    per_row = LANES // half
    pos = jnp.repeat(positions.astype(F32).reshape(n // per_row, per_row), half, axis=1)
    ang = pos * jnp.tile(inv_freq, per_row)
    cos, sin = lax.optimization_barrier((jnp.cos(ang), jnp.sin(ang)))
    cos, sin = cos.reshape(n, half), sin.reshape(n, half)
    pre = jnp.zeros((n, lane0), F32)
    post = jnp.zeros((n, LANES - lane0 - 2 * half), F32)
    z16 = jnp.zeros((n, half), F32)
    c = jnp.concatenate([pre + 1.0, cos, cos, post + 1.0], axis=1)
    sa = jnp.concatenate([pre, -sin, z16, post], axis=1)
    sb = jnp.concatenate([pre, z16, sin, post], axis=1)
    return lax.optimization_barrier((c, sa, sb))


def _rope_apply(x, c, sa, sb, sign):
    k = x.shape[1] // LANES
    if k > 1:
        c, sa, sb = (jnp.concatenate([t] * k, axis=1) for t in (c, sa, sb))
    w = x.shape[1]
    up = pltpu.roll(x, w - 16, 1)
    dn = pltpu.roll(x, 16, 1)
    if sign > 0:
        return x * c + up * sa + dn * sb
    return x * c - up * sa - dn * sb


def _matmul(a, b, mode, out_dtype, *, name, tm=None, tn=1024, tk=None, add=None, rope=None,
            out_scale=None, b_koff=0, b_cols=None, out_into=None, out_full=None, out_joff=0,
            out_chunk_blocks=None, after=None):
    if mode == "nn":
        m, k = a.shape
        n = b.shape[1]
    elif mode == "nt":
        m, k = a.shape
        n = b.shape[0]
    else:
        k, m = a.shape
        n = b.shape[1]
    b_j0 = 0
    if b_cols is not None:
        tn = _tile(n, tn)
        b_j0, n = b_cols[0], b_cols[1] * tn
    if tm is None:
        if mode == "nt":
            tm = 1024
        else:
            tm = 2048 if (k <= 512 and rope is None) else 1024
    if tk is None:
        tk = 3072 if mode == "nt" else 1024
    tm, tn, tk = _tile(m, tm), _tile(n, tn), _tile(k, tk)
    nk = k // tk
    if mode == "nn":
        a_spec = pl.BlockSpec((tm, tk), lambda j, i, kk: (i, kk))
        b_spec = pl.BlockSpec((tk, tn), lambda j, i, kk: (kk, j + b_j0))
        dims = (((1,), (0,)), ((), ()))
    elif mode == "nt":
        a_spec = pl.BlockSpec((tm, tk), lambda j, i, kk: (i, kk))
        b_spec = pl.BlockSpec((tn, tk), lambda j, i, kk: (j, kk + b_koff))
        dims = (((1,), (1,)), ((), ()))
    else:
        a_spec = pl.BlockSpec((tk, tm), lambda j, i, kk: (kk, i))
        b_spec = pl.BlockSpec((tk, tn), lambda j, i, kk: (kk, j))
        dims = (((0,), (0,)), ((), ()))
    operands = [a, b]
    in_specs = [a_spec, b_spec]
    if add is not None:
        operands.append(add)
        in_specs.append(pl.BlockSpec((tm, tn), lambda j, i, kk: (i, j)))
    if rope is not None:
        tables, rope_pred = rope
        for t in tables:
            operands.append(t)
            in_specs.append(pl.BlockSpec((tm, LANES), lambda j, i, kk: (i, 0)))
    aliases = {}
    if out_into is not None:
        aliases = {len(operands): 0}
        operands.append(out_into)
        in_specs.append(pl.BlockSpec(memory_space=pl.ANY))
        out_shape = jax.ShapeDtypeStruct(out_into.shape, out_into.dtype)
    elif out_full is not None:
        out_shape = jax.ShapeDtypeStruct(out_full, out_dtype)
    else:
        out_shape = jax.ShapeDtypeStruct((m, n), out_dtype)
    if after is not None:
        operands.append(after)
        in_specs.append(pl.BlockSpec(memory_space=pl.ANY))
    if out_chunk_blocks is not None:
        out_spec = pl.BlockSpec((None, tm, tn), lambda j, i, kk: ((j + out_joff) // out_chunk_blocks, i,
                                                                  (j + out_joff) % out_chunk_blocks))
    else:
        out_spec = pl.BlockSpec((tm, tn), lambda j, i, kk: (i, j + out_joff))

    def body(*refs):
        a_ref, b_ref = refs[0], refs[1]
        pos = 2
        add_ref = None
        if add is not None:
            add_ref = refs[pos]
            pos += 1
        tab_refs = None
        if rope is not None:
            tab_refs = refs[pos:pos + 3]
            pos += 3
        if out_into is not None:
            pos += 1
        if after is not None:
            pos += 1
        o_ref = refs[pos]
        acc_ref = refs[pos + 1] if nk > 1 else None

        def finish(res):
            if add_ref is not None:
                res = res + add_ref[...].astype(F32)
            if tab_refs is None:
                o_ref[...] = res.astype(o_ref.dtype)
                return
            j = pl.program_id(0)
            flag = rope_pred(j)

            roped = _rope_apply(res, tab_refs[0][...], tab_refs[1][...], tab_refs[2][...], 1)
            if out_scale is not None:
                value, scale_pred = out_scale
                use = scale_pred(j)
                roped = roped * (value if use is True else jnp.where(use, value, 1.0))
            if flag is True:
                o_ref[...] = roped.astype(o_ref.dtype)
                return

            @pl.when(flag)
            def _():
                o_ref[...] = roped.astype(o_ref.dtype)

            @pl.when(jnp.logical_not(flag))
            def _():
                o_ref[...] = res.astype(o_ref.dtype)

        part = lax.dot_general(a_ref[...].astype(BF16), b_ref[...].astype(BF16), dims,
                               preferred_element_type=F32)
        if nk == 1:
            finish(part)
            return
        kk = pl.program_id(2)

        @pl.when(kk == 0)
        def _():
            acc_ref[...] = part

        @pl.when(kk > 0)
        def _():
            acc_ref[...] += part

        @pl.when(kk == nk - 1)
        def _():
            finish(acc_ref[...])

    return pl.pallas_call(
        body, name=name, grid=(n // tn, m // tm, nk), in_specs=in_specs, out_specs=out_spec,
        out_shape=out_shape, input_output_aliases=aliases,
        scratch_shapes=[pltpu.VMEM((tm, tn), F32)] if nk > 1 else [],
        compiler_params=_params(("parallel", "parallel", "arbitrary")),
    )(*operands)


def _rms_fwd(x, g, out_dtype, *, name, add=None, tr=512):
    n, d = x.shape
    tr = _tile(n, tr)
    row = pl.BlockSpec((tr, d), lambda i: (i, 0))
    vec = pl.BlockSpec((1, d), lambda i: (0, 0))

    def body(*refs):
        x_ref, g_ref = refs[0], refs[1]
        o_ref = refs[-1]
        xv = x_ref[...].astype(F32)
        r = lax.rsqrt(jnp.mean(xv * xv, axis=-1, keepdims=True) + NORM_EPS)
        y = xv * r * g_ref[...]
        if add is not None:
            y = refs[2][...] + y
        o_ref[...] = y.astype(o_ref.dtype)

    ops = [x, g] + ([add] if add is not None else [])
    specs = [row, vec] + ([row] if add is not None else [])
    return pl.pallas_call(
        body, name=name, grid=(n // tr,), in_specs=specs, out_specs=row,
        out_shape=jax.ShapeDtypeStruct((n, d), out_dtype), compiler_params=_params(("parallel",)),
    )(*ops)


def _rms_bwd(x, g, dy, out_dtype, *, name, adds=(), dy_more=(), tr=512):
    n, d = x.shape
    tr = _tile(n, tr)
    steps = n // tr
    row = pl.BlockSpec((tr, d), lambda i: (i, 0))
    vec = pl.BlockSpec((1, d), lambda i: (0, 0))
    na = len(adds) + len(dy_more)

    def body(*refs):
        x_ref, g_ref, dy_ref = refs[:3]
        add_refs = refs[3:3 + len(adds)]
        more_refs = refs[3 + len(adds):3 + na]
        dx_ref, dg_ref, acc_ref = refs[3 + na:]
        i = pl.program_id(0)
        xv = x_ref[...].astype(F32)
        r = lax.rsqrt(jnp.mean(xv * xv, axis=-1, keepdims=True) + NORM_EPS)
        xh = xv * r
        dyv = dy_ref[...].astype(F32)
        for m_ref in more_refs:
            dyv = dyv + m_ref[...].astype(F32)
        part = (dyv * xh).reshape(tr // 8, 8, d).sum(axis=0)

        @pl.when(i == 0)
        def _():
            acc_ref[...] = part

        @pl.when(i > 0)
        def _():
            acc_ref[...] += part

        t = dyv * g_ref[...]
        dx = r * (t - xh * jnp.mean(t * xh, axis=-1, keepdims=True))
        for a_ref in add_refs:
            dx = dx + a_ref[...].astype(F32)
        dx_ref[...] = dx.astype(dx_ref.dtype)

        @pl.when(i == steps - 1)
        def _():
            dg_ref[...] = jnp.sum(acc_ref[...], axis=0, keepdims=True)

    return pl.pallas_call(
        body, name=name, grid=(steps,), in_specs=[row, vec, row] + [row] * na,
        out_specs=(row, vec),
        out_shape=(jax.ShapeDtypeStruct((n, d), out_dtype), jax.ShapeDtypeStruct((1, d), F32)),
        scratch_shapes=[pltpu.VMEM((8, d), F32)], compiler_params=_params(("arbitrary",)),
    )(x, g, dy, *adds, *dy_more)


def _rms(xv, g):
    return xv * lax.rsqrt(jnp.mean(xv * xv, axis=-1, keepdims=True) + NORM_EPS) * g


def _post_norm_block(y, g, h_in, next_gains, *, name, tr=1024):
    n, d = y.shape
    tr = _tile(n, tr)
    nk = len(next_gains)
    row = pl.BlockSpec((tr, d), lambda i: (i, 0))
    vec = pl.BlockSpec((1, d), lambda i: (0, 0))

    def body(*refs):
        y_ref, g_ref, h_ref = refs[:3]
        gk_refs = refs[3:3 + nk]
        o_ref = refs[3 + nk]
        hn_refs = refs[4 + nk:]
        h = h_ref[...] + _rms(y_ref[...], g_ref[...])
        o_ref[...] = h
        for gk_ref, hn_ref in zip(gk_refs, hn_refs):
            hn_ref[...] = _rms(h, gk_ref[...]).astype(BF16)

    return pl.pallas_call(
        body, name=name, grid=(n // tr,), in_specs=[row, vec, row] + [vec] * nk,
        out_specs=(row,) * (1 + nk),
        out_shape=(jax.ShapeDtypeStruct((n, d), F32),) + (jax.ShapeDtypeStruct((n, d), BF16),) * nk,
        compiler_params=_params(("parallel",)),
    )(y, g, h_in, *next_gains)


def _post_norm_loss(y, g, h_in, target, *, tr=1024):
    n, d = y.shape
    tr = _tile(n, tr)
    steps = n // tr
    row = pl.BlockSpec((tr, d), lambda i: (i, 0))

    def body(y_ref, g_ref, h_ref, t_ref, dh_ref, loss_ref, acc_ref):
        i = pl.program_id(0)
        e = h_ref[...] + _rms(y_ref[...], g_ref[...]) - t_ref[...]
        dh_ref[...] = e / d
        part = (e * e).reshape(tr // 8, 8, d).sum(axis=0)

        @pl.when(i == 0)
        def _():
            acc_ref[...] = part

        @pl.when(i > 0)
        def _():
            acc_ref[...] += part

        @pl.when(i == steps - 1)
        def _():
            s = jnp.sum(jnp.sum(acc_ref[...], axis=-1, keepdims=True), axis=0, keepdims=True)
            loss_ref[...] = 0.5 * s / d

    return pl.pallas_call(
        body, name="b_post_norm_loss", grid=(steps,),
        in_specs=[row, pl.BlockSpec((1, d), lambda i: (0, 0)), row, row],
        out_specs=(row, pl.BlockSpec((1, 1), lambda i: (0, 0))),
        out_shape=(jax.ShapeDtypeStruct((n, d), F32), jax.ShapeDtypeStruct((1, 1), F32)),
        scratch_shapes=[pltpu.VMEM((8, d), F32)], compiler_params=_params(("arbitrary",)),
    )(y, g, h_in, target)


def _rms_bwd_pair(x, g1, dy1, g2, dy2, add, *, name, tr=512):
    n, d = x.shape
    tr = _tile(n, tr)
    steps = n // tr
    row = pl.BlockSpec((tr, d), lambda i: (i, 0))
    vec = pl.BlockSpec((1, d), lambda i: (0, 0))

    def body(x_ref, g1_ref, d1_ref, g2_ref, d2_ref, add_ref, dx_ref, dg1_ref, dg2_ref, acc_ref):
        i = pl.program_id(0)
        xv = x_ref[...]
        r = lax.rsqrt(jnp.mean(xv * xv, axis=-1, keepdims=True) + NORM_EPS)
        xh = xv * r
        dx = add_ref[...]
        for k, (g_ref, d_ref) in enumerate(((g1_ref, d1_ref), (g2_ref, d2_ref))):
            dyv = d_ref[...].astype(F32)
            part = (dyv * xh).reshape(tr // 8, 8, d).sum(axis=0)

            @pl.when(i == 0)
            def _(part=part, k=k):
                acc_ref[k] = part

            @pl.when(i > 0)
            def _(part=part, k=k):
                acc_ref[k] += part

            t = dyv * g_ref[...]
            dx = dx + r * (t - xh * jnp.mean(t * xh, axis=-1, keepdims=True))
        dx_ref[...] = dx

        @pl.when(i == steps - 1)
        def _():
            dg1_ref[...] = jnp.sum(acc_ref[0], axis=0, keepdims=True)
            dg2_ref[...] = jnp.sum(acc_ref[1], axis=0, keepdims=True)

    return pl.pallas_call(
        body, name=name, grid=(steps,), in_specs=[row, vec, row, vec, row, row],
        out_specs=(row, vec, vec),
        out_shape=(jax.ShapeDtypeStruct((n, d), F32), jax.ShapeDtypeStruct((1, d), F32),
                   jax.ShapeDtypeStruct((1, d), F32)),
        scratch_shapes=[pltpu.VMEM((2, 8, d), F32)], compiler_params=_params(("arbitrary",)),
    )(x, g1, dy1, g2, dy2, add)


def _kv_latent_fwd(ckr, g_lat, tabs, *, tr=512):
    n = ckr.shape[0]
    tr = _tile(n, tr)
    lat = B_KV_LORA

    def body(c_ref, k_ref, g_ref, tc, tsa, tsb, ckv_ref, kr_ref):
        xv = c_ref[...]
        r = lax.rsqrt(jnp.mean(xv * xv, axis=-1, keepdims=True) + NORM_EPS)
        ckv_ref[...] = (xv * r * g_ref[...]).astype(BF16)
        kr_ref[...] = _rope_apply(k_ref[...], tc[...], tsa[...], tsb[...], 1).astype(BF16)

    tab = pl.BlockSpec((tr, LANES), lambda i: (i, 0))
    return pl.pallas_call(
        body, name="kv_latent_fwd", grid=(n // tr,),
        in_specs=[pl.BlockSpec((tr, lat), lambda i: (i, 0)),
                  pl.BlockSpec((tr, LANES), lambda i: (i, lat // LANES)),
                  pl.BlockSpec((1, lat), lambda i: (0, 0)), tab, tab, tab],
        out_specs=(pl.BlockSpec((tr, lat), lambda i: (i, 0)), tab),
        out_shape=(jax.ShapeDtypeStruct((n, lat), BF16), jax.ShapeDtypeStruct((n, LANES), BF16)),
        compiler_params=_params(("parallel",)),
    )(ckr, ckr, g_lat, *tabs)


def _kv_latent_bwd(dckv, ckr, g_lat, dk_cat, tabs, *, tr=512):
    n = ckr.shape[0]
    tr = _tile(n, tr)
    steps = n // tr
    lat = B_KV_LORA
    wk = dk_cat.shape[1]

    def body(d_ref, c_ref, g_ref, dk_ref, tc, tsa, tsb, o_ref, dg_ref, acc_ref):
        i = pl.program_id(0)
        xv = c_ref[...]
        r = lax.rsqrt(jnp.mean(xv * xv, axis=-1, keepdims=True) + NORM_EPS)
        xh = xv * r
        dyv = d_ref[...]
        part = (dyv * xh).reshape(tr // 8, 8, lat).sum(axis=0)

        @pl.when(i == 0)
        def _():
            acc_ref[...] = part

        @pl.when(i > 0)
        def _():
            acc_ref[...] += part

        t = dyv * g_ref[...]
        dx = r * (t - xh * jnp.mean(t * xh, axis=-1, keepdims=True))
        o_ref[:, 0:lat] = dx.astype(o_ref.dtype)
        dkr = dk_ref[:, 0:LANES].astype(F32)
        for h in range(1, wk // LANES):
            dkr = dkr + dk_ref[:, h * LANES:(h + 1) * LANES].astype(F32)
        o_ref[:, lat:lat + LANES] = _rope_apply(dkr, tc[...], tsa[...], tsb[...], -1).astype(o_ref.dtype)

        @pl.when(i == steps - 1)
        def _():
            dg_ref[...] = jnp.sum(acc_ref[...], axis=0, keepdims=True)

    tab = pl.BlockSpec((tr, LANES), lambda i: (i, 0))
    return pl.pallas_call(
        body, name="kv_latent_bwd", grid=(steps,),
        in_specs=[pl.BlockSpec((tr, lat), lambda i: (i, 0)), pl.BlockSpec((tr, lat), lambda i: (i, 0)),
                  pl.BlockSpec((1, lat), lambda i: (0, 0)), pl.BlockSpec((tr, wk), lambda i: (i, 0)),
                  tab, tab, tab],
        out_specs=(pl.BlockSpec((tr, lat + LANES), lambda i: (i, 0)), pl.BlockSpec((1, lat), lambda i: (0, 0))),
        out_shape=(jax.ShapeDtypeStruct((n, lat + LANES), BF16), jax.ShapeDtypeStruct((1, lat), F32)),
        scratch_shapes=[pltpu.VMEM((8, lat), F32)], compiler_params=_params(("arbitrary",)),
    )(dckv, ckr, g_lat, dk_cat, *tabs)


def _sigmoid(z):
    return 1.0 / (1.0 + jnp.exp(-z))


def _lane_place(cols, width):
    rows = cols[0].shape[0]
    lane = lax.broadcasted_iota(jnp.int32, (rows, width), 1)
    out = jnp.zeros((rows, width), F32)
    for h, col in enumerate(cols):
        out = jnp.where(lane == h, col, out)
    return out


def _merge_gate_fwd(outs, lses, proj, z_block, *, tr=1024):
    n, w = outs[0].shape
    tr = _tile(n, tr)
    ng = len(outs)

    def body(*refs):
        o_refs = refs[:ng]
        l_refs = refs[ng:2 * ng]
        z_ref = refs[2 * ng]
        y_ref, om_ref, lse_ref = refs[2 * ng + 1:]
        ls = [r[...] for r in l_refs]
        mx = ls[0]
        for l in ls[1:]:
            mx = jnp.maximum(mx, l)
        ssum = jnp.exp2(ls[0] - mx)
        for l in ls[1:]:
            ssum = ssum + jnp.exp2(l - mx)
        tot = mx + jnp.log2(ssum)
        lse_ref[...] = tot
        ws = [jnp.exp2(l - tot) for l in ls]
        for h in range(A_HEADS):
            sl = slice(h * A_HEAD_DIM, (h + 1) * A_HEAD_DIM)
            o = ws[0][:, h:h + 1] * o_refs[0][:, sl]
            for gi in range(1, ng):
                o = o + ws[gi][:, h:h + 1] * o_refs[gi][:, sl]
            z = z_ref[:, sl].astype(F32)
            om_ref[:, sl] = o.astype(BF16)
            y_ref[:, sl] = (o * (z * _sigmoid(z))).astype(BF16)

    row = pl.BlockSpec((tr, w), lambda i: (i, 0))
    lrow = pl.BlockSpec((tr, A_HEADS), lambda i: (i, 0))
    return pl.pallas_call(
        body, name="merge_gate_fwd", grid=(n // tr,),
        in_specs=[row] * ng + [lrow] * ng + [pl.BlockSpec((tr, w), lambda i: (i, z_block))],
        out_specs=(row, row, lrow),
        out_shape=(jax.ShapeDtypeStruct((n, w), BF16), jax.ShapeDtypeStruct((n, w), BF16),
                   jax.ShapeDtypeStruct((n, A_HEADS), F32)),
        compiler_params=_params(("parallel",)),
    )(*outs, *lses, proj)


def _gate_bwd(dy, o, z_arr, z_block, *, name, with_delta, tr=512):
    n, w = dy.shape
    tr = _tile(n, tr)

    def body(*refs):
        dy_ref, o_ref, z_ref, do_ref, dz_ref = refs[:5]
        dyv = dy_ref[...].astype(F32)
        ov = o_ref[...].astype(F32)
        z = z_ref[...].astype(F32)
        sig = _sigmoid(z)
        do = dyv * (z * sig)
        do_ref[...] = do.astype(BF16)
        dz_ref[...] = (dyv * ov * (sig * (1.0 + z * (1.0 - sig)))).astype(BF16)
        if with_delta:
            prod = do * ov
            cols = [jnp.sum(prod[:, h * A_HEAD_DIM:(h + 1) * A_HEAD_DIM], axis=-1, keepdims=True)
                    for h in range(A_HEADS)]
            refs[5][...] = _lane_place(cols, A_HEADS)

    row = pl.BlockSpec((tr, w), lambda i: (i, 0))
    out_specs = [row, row]
    out_shape = [jax.ShapeDtypeStruct((n, w), BF16), jax.ShapeDtypeStruct((n, w), BF16)]
    if with_delta:
        out_specs.append(pl.BlockSpec((tr, A_HEADS), lambda i: (i, 0)))
        out_shape.append(jax.ShapeDtypeStruct((n, A_HEADS), F32))
    return pl.pallas_call(
        body, name=name, grid=(n // tr,),
        in_specs=[row, row, pl.BlockSpec((tr, w), lambda i: (i, z_block))],
        out_specs=tuple(out_specs), out_shape=tuple(out_shape), compiler_params=_params(("parallel",)),
    )(dy, o, z_arr)


def _dot_nt(a, b):
    return lax.dot_general(a, b, (((1,), (1,)), ((), ())), preferred_element_type=F32)


def _dot_nn(a, b):
    return lax.dot_general(a, b, (((1,), (0,)), ((), ())), preferred_element_type=F32)


def _attn_a_fwd(qkv, cb0, qb, out_dtype, *, name):
    bl, dil, ln, _ = qkv.shape
    nb = ln // qb
    hw = A_WIDTH
    heads = range(A_HEADS)
    sls = [slice(h * A_HEAD_DIM, (h + 1) * A_HEAD_DIM) for h in heads]

    def body(*refs):
        if nb > 1:
            q_ref, kc_ref, vc_ref, kp_ref, vp_ref, o_ref, lse_ref = refs
        else:
            q_ref, kc_ref, vc_ref, o_ref, lse_ref = refs
        i = pl.program_id(2)
        qi = lax.broadcasted_iota(jnp.int32, (qb, qb), 0)
        ki = lax.broadcasted_iota(jnp.int32, (qb, qb), 1)
        mask_c = ki <= qi
        mask_p = jnp.logical_and(ki >= qi, i >= 1)
        s_c = [jnp.where(mask_c, _dot_nt(q_ref[:, sls[h]], kc_ref[:, sls[h]]), NEG) for h in heads]
        m = [jnp.max(s_c[h], axis=-1, keepdims=True) for h in heads]
        if nb > 1:
            s_p = [jnp.where(mask_p, _dot_nt(q_ref[:, sls[h]], kp_ref[:, sls[h]]), NEG) for h in heads]
            m = [jnp.maximum(m[h], jnp.max(s_p[h], axis=-1, keepdims=True)) for h in heads]
        p_c = [jnp.exp2(s_c[h] - m[h]) for h in heads]
        l = [jnp.sum(p_c[h], axis=-1, keepdims=True) for h in heads]
        acc = [_dot_nn(p_c[h].astype(BF16), vc_ref[:, sls[h]]) for h in heads]
        if nb > 1:
            p_p = [jnp.exp2(s_p[h] - m[h]) for h in heads]
            l = [l[h] + jnp.sum(p_p[h], axis=-1, keepdims=True) for h in heads]
            acc = [acc[h] + _dot_nn(p_p[h].astype(BF16), vp_ref[:, sls[h]]) for h in heads]
        for h in heads:
            o_ref[:, sls[h]] = (acc[h] / l[h]).astype(o_ref.dtype)
        lse_ref[...] = _lane_place([m[h] + jnp.log2(l[h]) for h in heads], A_HEADS)

    def spec(off, prev):
        if prev:
            return pl.BlockSpec((None, None, qb, hw), lambda b, r, i: (b, r, jnp.maximum(i - 1, 0), cb0 + off))
        return pl.BlockSpec((None, None, qb, hw), lambda b, r, i: (b, r, i, cb0 + off))

    return pl.pallas_call(
        body, name=name, grid=(bl, dil, nb),
        in_specs=[spec(0, False), spec(1, False), spec(2, False)] + ([spec(1, True), spec(2, True)] if nb > 1 else []),
        out_specs=(pl.BlockSpec((None, None, qb, hw), lambda b, r, i: (b, r, i, 0)),
                   pl.BlockSpec((None, None, qb, A_HEADS), lambda b, r, i: (b, r, i, 0))),
        out_shape=(jax.ShapeDtypeStruct((bl, dil, ln, hw), out_dtype),
                   jax.ShapeDtypeStruct((bl, dil, ln, A_HEADS), F32)),
        compiler_params=_params(("parallel", "parallel", "arbitrary")),
    )(*([qkv] * (5 if nb > 1 else 3)))


def _attn_a_bwd(qkv, cb0, do, lse, delta, lse_t, delta_t, tabs, qb, *, name):
    bl, dil, ln, _ = qkv.shape
    nb = ln // qb
    hw = A_WIDTH

    def body(*refs):
        if nb > 1:
            (q_ref, kc_ref, vc_ref, do_ref, lse_ref, dl_ref, lt_ref, dt_ref, tc, tsa, tsb,
             qn_ref, kp_ref, vp_ref, don_ref, ltn_ref, dtn_ref, o_ref) = refs
        else:
            q_ref, kc_ref, vc_ref, do_ref, lse_ref, dl_ref, lt_ref, dt_ref, tc, tsa, tsb, o_ref = refs
        i = pl.program_id(2)
        row = lax.broadcasted_iota(jnp.int32, (qb, qb), 0)
        col = lax.broadcasted_iota(jnp.int32, (qb, qb), 1)
        m_qc = col <= row
        m_kc = row <= col
        m_qp = jnp.logical_and(col >= row, i >= 1)
        m_kn = jnp.logical_and(row >= col, i + 1 < nb)
        c, sa, sb = tc[...], tsa[...], tsb[...]
        heads = range(A_HEADS)
        sls = [slice(h * A_HEAD_DIM, (h + 1) * A_HEAD_DIM) for h in heads]
        q, kc = [q_ref[:, sl] for sl in sls], [kc_ref[:, sl] for sl in sls]
        vc, dov = [vc_ref[:, sl] for sl in sls], [do_ref[:, sl] for sl in sls]
        lse_c = [lse_ref[:, h:h + 1] for h in heads]
        dl_c = [dl_ref[:, h:h + 1] for h in heads]
        s = [_dot_nt(q[h], kc[h]) for h in heads]
        st = [_dot_nt(kc[h], q[h]) for h in heads]
        dp = [_dot_nt(dov[h], vc[h]) for h in heads]
        dpt = [_dot_nt(vc[h], dov[h]) for h in heads]
        p = [jnp.exp2(jnp.where(m_qc, s[h], NEG) - lse_c[h]) for h in heads]
        pt = [jnp.exp2(jnp.where(m_kc, st[h], NEG) - lt_ref[h:h + 1, :]) for h in heads]
        dq = [_dot_nn((p[h] * (dp[h] - dl_c[h])).astype(BF16), kc[h]) for h in heads]
        dk = [_dot_nn((pt[h] * (dpt[h] - dt_ref[h:h + 1, :])).astype(BF16), q[h]) for h in heads]
        dv = [_dot_nn(pt[h].astype(BF16), dov[h]) for h in heads]
        if nb > 1:
            kp, vp = [kp_ref[:, sl] for sl in sls], [vp_ref[:, sl] for sl in sls]
            qn, don = [qn_ref[:, sl] for sl in sls], [don_ref[:, sl] for sl in sls]
            s = [_dot_nt(q[h], kp[h]) for h in heads]
            st = [_dot_nt(kc[h], qn[h]) for h in heads]
            dp = [_dot_nt(dov[h], vp[h]) for h in heads]
            dpt = [_dot_nt(vc[h], don[h]) for h in heads]
            p = [jnp.exp2(jnp.where(m_qp, s[h], NEG) - lse_c[h]) for h in heads]
            pt = [jnp.exp2(jnp.where(m_kn, st[h], NEG) - ltn_ref[h:h + 1, :]) for h in heads]
            dq = [dq[h] + _dot_nn((p[h] * (dp[h] - dl_c[h])).astype(BF16), kp[h]) for h in heads]
            dk = [dk[h] + _dot_nn((pt[h] * (dpt[h] - dtn_ref[h:h + 1, :])).astype(BF16), qn[h]) for h in heads]
            dv = [dv[h] + _dot_nn(pt[h].astype(BF16), don[h]) for h in heads]
        for h in heads:
            o_ref[:, h * A_HEAD_DIM:(h + 1) * A_HEAD_DIM] = _rope_apply(dq[h] * A_SCALE, c, sa, sb, -1).astype(BF16)
            o_ref[:, hw + h * A_HEAD_DIM:hw + (h + 1) * A_HEAD_DIM] = _rope_apply(dk[h] * LN2, c, sa, sb, -1).astype(BF16)
            o_ref[:, 2 * hw + h * A_HEAD_DIM:2 * hw + (h + 1) * A_HEAD_DIM] = dv[h].astype(BF16)

    def cur(w, col):
        return pl.BlockSpec((None, None, qb, w), lambda b, r, i: (b, r, i, col))

    def prev(w, col):
        return pl.BlockSpec((None, None, qb, w), lambda b, r, i: (b, r, jnp.maximum(i - 1, 0), col))

    def nxt(w, col):
        return pl.BlockSpec((None, None, qb, w), lambda b, r, i: (b, r, jnp.minimum(i + 1, nb - 1), col))

    t_cur = pl.BlockSpec((None, None, A_HEADS, qb), lambda b, r, i: (b, r, 0, i))
    t_nxt = pl.BlockSpec((None, None, A_HEADS, qb), lambda b, r, i: (b, r, 0, jnp.minimum(i + 1, nb - 1)))
    in_specs = [cur(hw, cb0), cur(hw, cb0 + 1), cur(hw, cb0 + 2), cur(hw, 0), cur(A_HEADS, 0), cur(A_HEADS, 0),
                t_cur, t_cur, cur(LANES, 0), cur(LANES, 0), cur(LANES, 0)]
    operands = [qkv, qkv, qkv, do, lse, delta, lse_t, delta_t, *tabs]
    if nb > 1:
        in_specs += [nxt(hw, cb0), prev(hw, cb0 + 1), prev(hw, cb0 + 2), nxt(hw, 0), t_nxt, t_nxt]
        operands += [qkv, qkv, qkv, do, lse_t, delta_t]
    return pl.pallas_call(
        body, name=name, grid=(bl, dil, nb), in_specs=in_specs, out_specs=cur(3 * hw, 0),
        out_shape=jax.ShapeDtypeStruct((bl, dil, ln, 3 * hw), BF16),
        compiler_params=_params(("parallel", "parallel", "arbitrary")),
    )(*operands)


def _head_terms(do, o, lse, e):
    rows = do.shape[0]
    lane = lax.broadcasted_iota(jnp.int32, (rows, LANES), 1)
    mine = (lane < B_VDIM) if e == 0 else (lane >= B_VDIM)
    prod = do.astype(F32) * o.astype(F32)
    dl = jnp.sum(jnp.where(mine, prod, 0.0), axis=-1, keepdims=True)
    do_e = jnp.where(mine, do, jnp.zeros_like(do))
    return do_e, dl, lse[:, e * B_VDIM:e * B_VDIM + 1]


def _col_to_row(col, rows):
    return jnp.transpose(jnp.broadcast_to(col, (rows, LANES)))[0:1, :]


def _mla_fwd(q_cat, kvup, kr, z, tq):
    bl, t, _ = q_cat.shape
    nq = t // tq
    pairs = B_HEADS // 2
    v_blk0 = (B_HEADS * LANES) // LANES

    def body(q_ref, k_ref, v_ref, kr_ref, z_ref, y_ref, o_ref, lse_ref, lrow_ref, m_ref, acc_ref):
        qi = pl.program_id(2)
        qs = [q_ref[:, e * LANES:(e + 1) * LANES] for e in range(2)]
        row = lax.broadcasted_iota(jnp.int32, (tq, tq), 0)
        col = lax.broadcasted_iota(jnp.int32, (tq, tq), 1)
        tri = col <= row
        sum_lane = [B_VDIM, 0]

        for e in range(2):
            m_ref[e] = jnp.full((tq, LANES), NEG, F32)
            acc_ref[e] = jnp.zeros((tq, LANES), F32)

        def tile(k0, w, masked):
            lane = lax.broadcasted_iota(jnp.int32, (w, LANES), 1)
            first = lane < B_VDIM
            krv = kr_ref[pl.ds(k0, w), :]
            v = v_ref[pl.ds(k0, w), :]
            vs = [jnp.where(first, v, jnp.where(lane == B_VDIM, 1.0, 0.0).astype(BF16)),
                  jnp.where(first, jnp.where(lane == 0, 1.0, 0.0).astype(BF16), v)]
            ss = []
            for e in range(2):
                k = k_ref[pl.ds(k0, w), e * LANES:(e + 1) * LANES] + krv
                s = _dot_nt(qs[e], k)
                if masked:
                    r = lax.broadcasted_iota(jnp.int32, (tq, w), 0)
                    c = lax.broadcasted_iota(jnp.int32, (tq, w), 1)
                    s = jnp.where(c <= r + (w - tq), s, NEG)
                ss.append(s)
            for e in range(2):
                m_old = m_ref[e]
                m_new = jnp.maximum(m_old, jnp.max(ss[e], axis=-1, keepdims=True))
                p = jnp.exp2(ss[e] - jnp.concatenate([m_new] * (w // LANES), axis=1)).astype(BF16)
                m_ref[e] = m_new
                acc_ref[e] = jnp.exp2(m_old - m_new) * acc_ref[e] + _dot_nn(p, vs[e])

        def step(kb2, carry):
            tile(pl.multiple_of(kb2 * 2 * tq, 2 * tq), 2 * tq, False)
            return carry

        lax.fori_loop(0, qi // 2, step, 0)

        @pl.when(qi % 2 == 1)
        def _():
            tile(pl.multiple_of((qi - 1) * tq, tq), 2 * tq, True)

        @pl.when(qi % 2 == 0)
        def _():
            tile(pl.multiple_of(qi * tq, tq), tq, True)
        lane = lax.broadcasted_iota(jnp.int32, (tq, LANES), 1)
        first = lane < B_VDIM
        accs = [acc_ref[e] for e in range(2)]
        ls = [accs[e][:, sum_lane[e]:sum_lane[e] + 1] for e in range(2)]
        outs = [accs[e] / ls[e] for e in range(2)]
        lses = [m_ref[e] + jnp.log2(ls[e]) for e in range(2)]
        o = jnp.where(first, outs[0], outs[1])
        zv = z_ref[...].astype(F32)
        o_ref[...] = o.astype(BF16)
        y_ref[...] = (o * (zv * _sigmoid(zv))).astype(BF16)
        lse_ref[...] = jnp.where(first, lses[0], lses[1])
        for e in range(2):
            lrow_ref[e:e + 1, :] = jnp.transpose(lses[e])[0:1, :]

    blk = pl.BlockSpec((None, tq, LANES), lambda b, j, i: (b, i, j))
    return pl.pallas_call(
        body, name="mla_fwd", grid=(bl, pairs, nq),
        in_specs=[pl.BlockSpec((None, tq, 2 * LANES), lambda b, j, i: (b, i, j)),
                  pl.BlockSpec((None, t, 2 * LANES), lambda b, j, i: (b, 0, j)),
                  pl.BlockSpec((None, t, LANES), lambda b, j, i: (b, 0, v_blk0 + j)),
                  pl.BlockSpec((None, t, LANES), lambda b, j, i: (b, 0, 0)),
                  blk],
        out_specs=(blk, blk, blk, pl.BlockSpec((None, None, None, 2, tq), lambda b, j, i: (b, j, i, 0, 0))),
        out_shape=(jax.ShapeDtypeStruct((bl, t, B_WIDTH), BF16), jax.ShapeDtypeStruct((bl, t, B_WIDTH), BF16),
                   jax.ShapeDtypeStruct((bl, t, B_WIDTH), F32),
                   jax.ShapeDtypeStruct((bl, pairs, nq, 2, tq), F32)),
        scratch_shapes=[pltpu.VMEM((2, tq, LANES), F32), pltpu.VMEM((2, tq, LANES), F32)],
        compiler_params=_params(("parallel", "parallel", "arbitrary")),
    )(q_cat, kvup, kvup, kr, z)


def _mla_dq(q_cat, kvup, kr, do, o, lse, tabs, tq):
    bl, t, _ = q_cat.shape
    nq = t // tq
    pairs = B_HEADS // 2
    v_blk0 = (B_HEADS * LANES) // LANES

    def body(q_ref, k_ref, v_ref, kr_ref, do_ref, o_ref, lse_ref, tc, tsa, tsb, dq_ref, drow_ref, acc_ref):
        qi = pl.program_id(2)
        dov, ov, lsev = do_ref[...], o_ref[...], lse_ref[...]
        qs = [q_ref[:, e * LANES:(e + 1) * LANES] for e in range(2)]
        terms = [_head_terms(dov, ov, lsev, e) for e in range(2)]
        row = lax.broadcasted_iota(jnp.int32, (tq, tq), 0)
        col = lax.broadcasted_iota(jnp.int32, (tq, tq), 1)
        tri = col <= row
        for e in range(2):
            acc_ref[e] = jnp.zeros((tq, LANES), F32)

        def tile(k0, w, masked):
            krv = kr_ref[pl.ds(k0, w), :]
            v = v_ref[pl.ds(k0, w), :]
            ks = [k_ref[pl.ds(k0, w), e * LANES:(e + 1) * LANES] + krv for e in range(2)]
            ss = [_dot_nt(qs[e], ks[e]) for e in range(2)]
            dps = [_dot_nt(terms[e][0], v) for e in range(2)]
            for e in range(2):
                s = ss[e]
                if masked:
                    r = lax.broadcasted_iota(jnp.int32, (tq, w), 0)
                    c = lax.broadcasted_iota(jnp.int32, (tq, w), 1)
                    s = jnp.where(c <= r + (w - tq), s, NEG)
                p = jnp.exp2(s - terms[e][2])
                ds = (p * (dps[e] - terms[e][1])).astype(BF16)
                acc_ref[e] += _dot_nn(ds, ks[e])

        def step(kb2, carry):
            tile(pl.multiple_of(kb2 * 2 * tq, 2 * tq), 2 * tq, False)
            return carry

        lax.fori_loop(0, qi // 2, step, 0)

        @pl.when(qi % 2 == 1)
        def _():
            tile(pl.multiple_of((qi - 1) * tq, tq), 2 * tq, True)

        @pl.when(qi % 2 == 0)
        def _():
            tile(pl.multiple_of(qi * tq, tq), tq, True)

        for e in range(2):
            dq_ref[:, e * LANES:(e + 1) * LANES] = _rope_apply(acc_ref[e] * B_SCALE, tc[...], tsa[...], tsb[...], -1).astype(BF16)
            drow_ref[e:e + 1, :] = _col_to_row(terms[e][1], tq)

    blk = pl.BlockSpec((None, tq, LANES), lambda b, j, i: (b, i, j))
    tab = pl.BlockSpec((None, tq, LANES), lambda b, j, i: (b, i, 0))
    qblk = pl.BlockSpec((None, tq, 2 * LANES), lambda b, j, i: (b, i, j))
    return pl.pallas_call(
        body, name="mla_dq", grid=(bl, pairs, nq),
        in_specs=[qblk,
                  pl.BlockSpec((None, t, 2 * LANES), lambda b, j, i: (b, 0, j)),
                  pl.BlockSpec((None, t, LANES), lambda b, j, i: (b, 0, v_blk0 + j)),
                  pl.BlockSpec((None, t, LANES), lambda b, j, i: (b, 0, 0)),
                  blk, blk, blk, tab, tab, tab],
        out_specs=(qblk, pl.BlockSpec((None, None, None, 2, tq), lambda b, j, i: (b, j, i, 0, 0))),
        out_shape=(jax.ShapeDtypeStruct((bl, t, B_HEADS * LANES), BF16),
                   jax.ShapeDtypeStruct((bl, pairs, nq, 2, tq), F32)),
        scratch_shapes=[pltpu.VMEM((2, tq, LANES), F32)],
        compiler_params=_params(("parallel", "parallel", "arbitrary")),
    )(q_cat, kvup, kvup, kr, do, o, lse, *tabs)


def _mla_dkv(q_cat, kvup, kr, do, lse_rows, delta_rows, tq):
    bl, t, _ = q_cat.shape
    nq = t // tq
    pairs = B_HEADS // 2
    v_blk0 = (B_HEADS * LANES) // LANES

    def body(q_ref, k_ref, v_ref, kr_ref, do_ref, lrow_ref, drow_ref, dk_ref, dv_ref, acc_ref):
        kb = pl.program_id(2)
        v = v_ref[...]
        krv = kr_ref[...]
        ks = [k_ref[:, e * LANES:(e + 1) * LANES] + krv for e in range(2)]
        krow = lax.broadcasted_iota(jnp.int32, (tq, tq), 0)
        qcol = lax.broadcasted_iota(jnp.int32, (tq, tq), 1)
        tri = krow <= qcol
        lane = lax.broadcasted_iota(jnp.int32, (tq, LANES), 1)
        mine = [lane < B_VDIM, lane >= B_VDIM]

        for e in range(3):
            acc_ref[e] = jnp.zeros((tq, LANES), F32)

        def tile(qb, nblk, masked):
            w = nblk * tq
            rows = pl.ds(pl.multiple_of(qb * tq, tq), w)
            dov = do_ref[rows, :]
            lane_w = lax.broadcasted_iota(jnp.int32, (w, LANES), 1)
            mine_w = [lane_w < B_VDIM, lane_w >= B_VDIM]
            qs = [q_ref[rows, e * LANES:(e + 1) * LANES] for e in range(2)]
            does = [jnp.where(mine_w[e], dov, jnp.zeros_like(dov)) for e in range(2)]
            sts = [_dot_nt(ks[e], qs[e]) for e in range(2)]
            dpts = [_dot_nt(v, does[e]) for e in range(2)]

            def rows_of(ref, e):
                return jnp.concatenate([ref[qb + i, e:e + 1, :] for i in range(nblk)], axis=1)

            pts = []
            for e in range(2):
                st = sts[e]
                if masked:
                    r = lax.broadcasted_iota(jnp.int32, (tq, w), 0)
                    c = lax.broadcasted_iota(jnp.int32, (tq, w), 1)
                    st = jnp.where(r <= c, st, NEG)
                pts.append(jnp.exp2(st - rows_of(lrow_ref, e)))
            acc_ref[2] += _dot_nn(pts[0].astype(BF16), does[0]) + _dot_nn(pts[1].astype(BF16), does[1])
            for e in range(2):
                dst = (pts[e] * (dpts[e] - rows_of(drow_ref, e))).astype(BF16)
                acc_ref[e] += _dot_nn(dst, qs[e])

        rest = nq - 1 - kb
        odd = rest % 2

        @pl.when(odd == 1)
        def _():
            tile(kb, 2, True)

        @pl.when(odd == 0)
        def _():
            tile(kb, 1, True)

        def step(i, carry):
            tile(kb + 1 + odd + 2 * i, 2, False)
            return carry

        lax.fori_loop(0, rest // 2, step, 0)
        dk_ref[:, 0:LANES] = (acc_ref[0] * LN2).astype(BF16)
        dk_ref[:, LANES:2 * LANES] = (acc_ref[1] * LN2).astype(BF16)
        dv_ref[...] = acc_ref[2].astype(BF16)

    full = pl.BlockSpec((None, t, LANES), lambda b, j, i: (b, 0, j))
    rows = pl.BlockSpec((None, None, nq, 2, tq), lambda b, j, i: (b, j, 0, 0, 0))
    kblk = pl.BlockSpec((None, tq, 2 * LANES), lambda b, j, i: (b, i, j))
    return pl.pallas_call(
        body, name="mla_dkv", grid=(bl, pairs, nq),
        in_specs=[pl.BlockSpec((None, t, 2 * LANES), lambda b, j, i: (b, 0, j)),
                  kblk,
                  pl.BlockSpec((None, tq, LANES), lambda b, j, i: (b, i, v_blk0 + j)),
                  pl.BlockSpec((None, tq, LANES), lambda b, j, i: (b, i, 0)),
                  full, rows, rows],
        out_specs=(kblk, pl.BlockSpec((None, tq, LANES), lambda b, j, i: (b, i, j))),
        out_shape=(jax.ShapeDtypeStruct((bl, t, B_HEADS * LANES), BF16),
                   jax.ShapeDtypeStruct((bl, t, B_WIDTH), BF16)),
        scratch_shapes=[pltpu.VMEM((3, tq, LANES), F32)],
        compiler_params=_params(("parallel", "parallel", "arbitrary")),
    )(q_cat, kvup, kvup, kr, do, lse_rows, delta_rows)


def _adamw(w, g, m, v, *, name):
    r, c = w.shape
    tr = _row_tile(r, 256)
    c1 = 1.0 - ADAM_B1
    c2 = 1.0 - ADAM_B2
    bc1 = 1.0 - ADAM_B1 ** ADAM_STEP
    bc2 = 1.0 - ADAM_B2 ** ADAM_STEP

    def body(w_ref, g_ref, m_ref, v_ref, d_ref, nm_ref, nv_ref):
        gv = g_ref[...]
        nm = ADAM_B1 * m_ref[...] + c1 * gv
        nv = ADAM_B2 * v_ref[...] + c2 * (gv * gv)
        nm_ref[...] = nm
        nv_ref[...] = nv
        d_ref[...] = -ADAM_LR * ((nm / bc1) / (jnp.sqrt(nv / bc2) + ADAM_EPS) + ADAM_WD * w_ref[...])

    blk = pl.BlockSpec((tr, c), lambda i: (i, 0))
    sds = jax.ShapeDtypeStruct((r, c), F32)
    return pl.pallas_call(
        body, name=name, grid=(r // tr,), in_specs=[blk] * 4, out_specs=(blk,) * 3,
        out_shape=(sds,) * 3, compiler_params=_params(("parallel",)),
    )(w, g, m, v)


def _add_my_half(stacked, other, core, out_dtype, *, name):
    nch, a, c = stacked.shape
    h = a // 2
    tr = _row_tile(h, 256)
    nblk = h // tr

    def body(core_ref, s_ref, p_ref, o_ref):
        o_ref[...] = (s_ref[...] + p_ref[...]).astype(o_ref.dtype)

    return pl.pallas_call(
        body, name=name,
        grid_spec=pltpu.PrefetchScalarGridSpec(
            num_scalar_prefetch=1, grid=(nch, nblk),
            in_specs=[pl.BlockSpec((None, tr, c), lambda k, i, cr: (k, cr[0] * nblk + i, 0)),
                      pl.BlockSpec((None, tr, c), lambda k, i, cr: (k, i, 0))],
            out_specs=pl.BlockSpec((None, tr, c), lambda k, i, cr: (k, i, 0))),
        out_shape=jax.ShapeDtypeStruct((nch, h, c), out_dtype),
        compiler_params=_params(("parallel", "parallel")),
    )(core, stacked, other)


def _sum_chips(parts, own, chip, *, name):
    nch, h, c = parts.shape
    tr = _row_tile(h, 256)

    def body(chip_ref, p_ref, own_ref, o_ref):
        me = chip_ref[0]

        def slot(k):
            return jnp.where(me == k, own_ref[k].astype(F32), p_ref[k].astype(F32))

        acc = slot(0) + slot(1)
        for k in range(2, nch):
            acc = acc + slot(k)
        o_ref[...] = acc

    blk = pl.BlockSpec((nch, tr, c), lambda i, cr: (0, i, 0))
    return pl.pallas_call(
        body, name=name,
        grid_spec=pltpu.PrefetchScalarGridSpec(
            num_scalar_prefetch=1, grid=(h // tr,), in_specs=[blk, blk],
            out_specs=pl.BlockSpec((tr, c), lambda i, cr: (i, 0))),
        out_shape=jax.ShapeDtypeStruct((h, c), F32), compiler_params=_params(("parallel",)),
    )(chip, parts, own)


def _join_halves(mine, other, core, *, name):
    h, c = mine.shape
    tr = _row_tile(h, 256)
    nblk = h // tr

    def body(core_ref, m_ref, s_ref, o_ref):
        is_mine = pl.program_id(0) // nblk == core_ref[0]

        @pl.when(is_mine)
        def _():
            o_ref[...] = m_ref[...]

        @pl.when(jnp.logical_not(is_mine))
        def _():
            o_ref[...] = s_ref[...]

    blk = pl.BlockSpec((tr, c), lambda i, cr: (i % nblk, 0))
    return pl.pallas_call(
        body, name=name,
        grid_spec=pltpu.PrefetchScalarGridSpec(
            num_scalar_prefetch=1, grid=(2 * nblk,), in_specs=[blk, blk],
            out_specs=pl.BlockSpec((tr, c), lambda i, cr: (i, 0))),
        out_shape=jax.ShapeDtypeStruct((2 * h, c), F32), compiler_params=_params(("arbitrary",)),
    )(core, mine, other)


def _place():
    x, y, c = lax.axis_index("x"), lax.axis_index("y"), lax.axis_index("c")
    chips = [(1 - x, y), (x, 1 - y), (1 - x, 1 - y)]
    return x, y, c, chips


def _remote(src, dst, send_sems, recv_sems, k, to):
    return pltpu.make_async_remote_copy(src_ref=src, dst_ref=dst, send_sem=send_sems.at[k],
                                        recv_sem=recv_sems.at[k], device_id=to, device_id_type=MESH)


def _hbm_call(body, name, ins, out_shapes, n_remote):
    any_spec = pl.BlockSpec(memory_space=pl.ANY)
    return pl.pallas_call(
        body, name=name, in_specs=[any_spec] * len(ins), out_specs=tuple([any_spec] * len(out_shapes)),
        out_shape=tuple(out_shapes),
        scratch_shapes=[pltpu.SemaphoreType.DMA((n_remote,)), pltpu.SemaphoreType.DMA((n_remote,))],
    )(*ins)


def _all_gather_chips(shards, *, name):
    n = len(shards)

    def body(*refs):
        ins, outs = refs[:n], refs[n:2 * n]
        send_sems, recv_sems = refs[2 * n:]
        x, y, c, chips = _place()
        me = 2 * x + y
        sent = []
        for s in range(n):
            h = ins[s].shape[0] // 2
            for j, (px, py) in enumerate(chips):
                cp = _remote(ins[s].at[pl.ds(c * h, h)], outs[s].at[me, pl.ds(c * h, h)],
                             send_sems, recv_sems, s * 6 + j, (px, py, c))
                cp.start()
                sent.append(cp)
        for s in range(n):
            h = ins[s].shape[0] // 2
            for j, (px, py) in enumerate(chips):
                slab = outs[s].at[2 * px + py, pl.ds(c * h, h)]
                _remote(slab, slab, send_sems, recv_sems, s * 6 + j, (px, py, c)).wait_recv()
                cp = _remote(slab, slab, send_sems, recv_sems, s * 6 + 3 + j, (x, y, 1 - c))
                cp.start()
                sent.append(cp)
        for s in range(n):
            h = ins[s].shape[0] // 2
            for j, (px, py) in enumerate(chips):
                slab = outs[s].at[2 * px + py, pl.ds((1 - c) * h, h)]
                _remote(slab, slab, send_sems, recv_sems, s * 6 + 3 + j, (x, y, 1 - c)).wait_recv()
        for cp in sent:
            cp.wait_send()

    out_shapes = [jax.ShapeDtypeStruct((N_CHIPS,) + s.shape, s.dtype) for s in shards]
    return _hbm_call(body, name, shards, out_shapes, 6 * n)


def _pair_send_other_half(stacked, *, name):
    n = len(stacked)

    def body(*refs):
        ins, outs = refs[:n], refs[n:2 * n]
        send_sems, recv_sems = refs[2 * n:]
        x, y, c, _chips = _place()
        sent = []
        for s in range(n):
            h = ins[s].shape[1] // 2
            cp = _remote(ins[s].at[:, pl.ds((1 - c) * h, h)], outs[s], send_sems, recv_sems, s, (x, y, 1 - c))
            cp.start()
            sent.append(cp)
        for cp in sent:
            cp.wait_recv()
        for cp in sent:
            cp.wait_send()

    out_shapes = [jax.ShapeDtypeStruct((s.shape[0], s.shape[1] // 2, s.shape[2]), s.dtype) for s in stacked]
    return _hbm_call(body, name, stacked, out_shapes, n)


def _chip_exchange(halves, *, name):
    n = len(halves)

    def body(*refs):
        ins, outs = refs[:n], refs[n:2 * n]
        send_sems, recv_sems = refs[2 * n:]
        x, y, c, chips = _place()
        me = 2 * x + y
        sent = []
        for s in range(n):
            for j, (px, py) in enumerate(chips):
                cp = _remote(ins[s].at[2 * px + py], outs[s].at[me], send_sems, recv_sems, s * 3 + j, (px, py, c))
                cp.start()
                sent.append(cp)
        for s in range(n):
            for j, (px, py) in enumerate(chips):
                slab = outs[s].at[2 * px + py]
                _remote(slab, slab, send_sems, recv_sems, s * 3 + j, (px, py, c)).wait_recv()
        for cp in sent:
            cp.wait_send()

    out_shapes = [jax.ShapeDtypeStruct(s.shape, s.dtype) for s in halves]
    return _hbm_call(body, name, halves, out_shapes, 3 * n)


def _chip_exchange_start(halves, *, name):
    n = len(halves)
    hbm = pl.BlockSpec(memory_space=pltpu.HBM)
    sem = pl.BlockSpec(memory_space=pltpu.SEMAPHORE)

    def body(*refs):
        ins, lands = refs[:n], refs[n:2 * n]
        send_sems, recv_sems = refs[2 * n], refs[2 * n + 1]
        token = refs[-1]
        x, y, c, chips = _place()
        me = 2 * x + y
        for s in range(n):
            for j, (px, py) in enumerate(chips):
                _remote(ins[s].at[2 * px + py], lands[s].at[me], send_sems, recv_sems, s * 3 + j, (px, py, c)).start()
        token[...] = jnp.zeros_like(token)

    slabs = [pltpu.HBM(s.shape, s.dtype) for s in halves]
    outs = pl.pallas_call(
        body, name=name,
        out_shape=(pltpu.SemaphoreType.DMA((3 * n,)), pltpu.SemaphoreType.DMA((3 * n,)), *slabs, *slabs,
                   jax.ShapeDtypeStruct((8, LANES), F32)),
        in_specs=[hbm] * (2 * n), out_specs=(sem, sem, *([hbm] * (2 * n)), pl.BlockSpec(memory_space=pltpu.VMEM)),
        input_output_aliases={i: 2 + i for i in range(2 * n)},
        compiler_params=pltpu.CompilerParams(has_side_effects=pltpu.SideEffectType.DATAFLOW_SIDE_EFFECTING),
    )(*[pltpu.with_memory_space_constraint(s, pltpu.HBM) for s in halves],
      *[pltpu.with_memory_space_constraint(lax.empty(s.shape, s.dtype), pltpu.HBM) for s in halves])
    return outs[0], outs[1], list(outs[2:2 + n]), list(outs[2 + n:2 + 2 * n]), outs[-1]


def _chip_exchange_wait(send_sems, recv_sems, sent, lands, after, *, name):
    n = len(sent)
    hbm = pl.BlockSpec(memory_space=pltpu.HBM)
    sem = pl.BlockSpec(memory_space=pltpu.SEMAPHORE)

    def body(*refs):
        ins, lands_in = refs[:n], refs[n:2 * n]
        send_sems, recv_sems = refs[2 * n], refs[2 * n + 1]
        x, y, c, chips = _place()
        me = 2 * x + y
        for s in range(n):
            for j, (px, py) in enumerate(chips):
                k = 2 * px + py
                _remote(ins[s].at[k], lands_in[s].at[me], send_sems, recv_sems, s * 3 + j, (px, py, c)).wait_send()
                _remote(ins[s].at[k], lands_in[s].at[k], send_sems, recv_sems, s * 3 + j, (px, py, c)).wait_recv()

    slabs = [pltpu.HBM(s.shape, s.dtype) for s in sent]
    outs = pl.pallas_call(
        body, name=name, out_shape=(*slabs, *slabs),
        in_specs=[hbm] * (2 * n) + [sem, sem, pl.BlockSpec(memory_space=pl.ANY)],
        out_specs=tuple([hbm] * (2 * n)), input_output_aliases={i: i for i in range(2 * n)},
        compiler_params=pltpu.CompilerParams(has_side_effects=pltpu.SideEffectType.DATAFLOW_SIDE_EFFECTING),
    )(*sent, *lands, send_sems, recv_sems, after)
    return list(outs[n:])


def _pair_swap(halves, *, name):
    n = len(halves)

    def body(*refs):
        ins, outs = refs[:n], refs[n:2 * n]
        send_sems, recv_sems = refs[2 * n:]
        x, y, c, _chips = _place()
        sent = []
        for s in range(n):
            cp = _remote(ins[s], outs[s], send_sems, recv_sems, s, (x, y, 1 - c))
            cp.start()
            sent.append(cp)
        for cp in sent:
            cp.wait_recv()
        for cp in sent:
            cp.wait_send()

    out_shapes = [jax.ShapeDtypeStruct(s.shape, s.dtype) for s in halves]
    return _hbm_call(body, name, halves, out_shapes, n)


def _pack_rows(parts, row_multiple):
    flat = jnp.concatenate([p.reshape(-1) for p in parts])
    quantum = row_multiple * PACK_COLS
    pad = (-flat.shape[0]) % quantum
    flat = jnp.pad(flat, (0, pad))
    return flat.reshape(-1, PACK_COLS)


def _unpack(flat, shapes):
    out, pos = [], 0
    for shp in shapes:
        size = math.prod(shp)
        out.append(flat[pos:pos + size].reshape(shp))
        pos += size
    return out


def _to_chunks_cols(full):
    r, c4 = full.shape
    return full.reshape(r, N_CHIPS, c4 // N_CHIPS).transpose(1, 0, 2)


def _from_chunks_cols(stacked):
    nch, r, c = stacked.shape
    return stacked.transpose(1, 0, 2).reshape(r, nch * c)


def _class_major(a, bl, t, dil):
    w = a.shape[-1]
    if dil == 1:
        return a.reshape(bl, 1, t, w)
    return a.reshape(bl, t // dil, dil, w).transpose(0, 2, 1, 3)


def _natural(a):
    bl, dil, ln, w = a.shape
    if dil == 1:
        return a.reshape(bl * ln, w)
    return a.transpose(0, 2, 1, 3).reshape(bl * ln * dil, w)


def _train_step(x, positions, a_pre_norm, a_w_in, a_w_out, a_post_norm, kv_norm, kv_w_down, kv_latent_norm,
                kv_w_up, b_pre_norm, b_w_in, b_q_norm, b_w_q_up, b_w_out, b_post_norm, loss_target, moments):
    bl, t, d = x.shape
    n = bl * t
    qb = t // A_DILATIONS[-1]
    tq = _tile(t, 256)
    dq4 = d // N_CHIPS
    chip = 2 * lax.axis_index("x") + lax.axis_index("y")
    chip_arr = chip.astype(jnp.int32).reshape(1)
    core_arr = lax.axis_index("c").astype(jnp.int32).reshape(1)

    w_in_a_s = a_w_in[0].astype(BF16)
    outs_s = jnp.concatenate([a_w_out[0], b_w_out[0]], axis=0).astype(BF16)
    small_shapes = [kv_w_down.shape, kv_w_up.shape, b_w_in[0].shape, b_w_q_up[0].shape]
    small_s = _pack_rows([kv_w_down, kv_w_up, b_w_in[0], b_w_q_up[0]], 32).astype(BF16)
    gains_s = jnp.pad(jnp.concatenate([a_pre_norm[0], a_post_norm[0]]), (0, 16 * LANES - 2 * dq4)).reshape(16, LANES)
    shards = [w_in_a_s, outs_s, small_s, gains_s]
    gathered = _all_gather_chips(shards, name="gather_weights")
    g_in_a, g_outs, g_small, g_gains = [lax.dynamic_update_index_in_dim(g, s, chip, 0)
                                        for g, s in zip(gathered, shards)]

    w_in_a = _from_chunks_cols(g_in_a)
    w_out_a = g_outs[:, :A_WIDTH // N_CHIPS].reshape(A_WIDTH, d)
    w_out_b = g_outs[:, A_WIDTH // N_CHIPS:].reshape(B_WIDTH, d)
    sm = [_unpack(g_small[k].reshape(-1), small_shapes) for k in range(N_CHIPS)]
    w_down = jnp.concatenate([sm[k][0] for k in range(N_CHIPS)], axis=0)
    w_up = jnp.concatenate([sm[k][1] for k in range(N_CHIPS)], axis=1)
    w_in_b = jnp.concatenate([sm[k][2] for k in range(N_CHIPS)], axis=1)
    w_q_up = jnp.concatenate([sm[k][3] for k in range(N_CHIPS)], axis=1)
    gflat = g_gains.reshape(N_CHIPS, -1)
    g_a_pre = gflat[:, :dq4].reshape(1, d)
    g_a_post = gflat[:, dq4:2 * dq4].reshape(1, d)

    w_up_h = w_up.reshape(B_KV_LORA, B_HEADS, B_NOPE + B_VDIM)
    w_up_k = jnp.pad(w_up_h[:, :, :B_NOPE], ((0, 0), (0, 0), (0, LANES - B_NOPE))).reshape(B_KV_LORA, B_HEADS * LANES)
    w_up_v = w_up_h[:, :, B_NOPE:].reshape(B_KV_LORA, B_WIDTH)
    w_up_cat = jnp.concatenate([w_up_k, w_up_v], axis=1)
    w_q_up_p = jnp.pad(w_q_up.reshape(B_Q_LORA, B_HEADS, B_QK_DIM),
                       ((0, 0), (0, 0), (0, LANES - B_QK_DIM))).reshape(B_Q_LORA, B_HEADS * LANES)
    zeros_d = lambda c: jnp.zeros((d, c), BF16)
    w_down_p = jnp.concatenate([w_down[:, :B_KV_LORA], zeros_d(B_NOPE), w_down[:, B_KV_LORA:],
                                zeros_d(LANES - B_NOPE - B_ROPE)], axis=1)
    w_cq = w_in_b[:, :B_Q_LORA]
    w_z = w_in_b[:, B_Q_LORA:]

    tabs_a = _rope_tables(positions, A_ROPE_THETA, 0)
    tabs_b = _rope_tables(positions, B_ROPE_THETA, B_NOPE)

    h0 = x.reshape(n, d)
    hn_a = _rms_fwd(h0, g_a_pre, BF16, name="a_pre_norm", tr=1024)
    is_qk = lambda j: j != 2
    is_q = lambda j: j == 0
    z_blk_a = 3 * A_GROUPS
    z_a = _matmul(hn_a, w_in_a, "nn", BF16, name="a_proj_z", b_cols=(z_blk_a, 1))
    o_groups, lse_groups, qkv_cm, hn_cm, tabs_cm = [], [], [], [], []
    for g, dil in enumerate(A_DILATIONS):
        flat = lambda a: _class_major(a, bl, t, dil).reshape(n, a.shape[-1])
        hn_g = hn_a if dil == 1 else flat(hn_a)
        tabs_g = tabs_a if dil == 1 else lax.optimization_barrier(tuple(flat(tb) for tb in tabs_a))
        proj_g = _matmul(hn_g, w_in_a, "nn", BF16, name=f"a_proj_{g}", rope=(tabs_g, is_qk),
                         out_scale=(A_SCALE * LOG2E, is_q), b_cols=(3 * g, 3))
        src = proj_g.reshape(bl, dil, t // dil, 3 * A_WIDTH)
        hn_cm.append(hn_g)
        tabs_cm.append(tabs_g)
        qkv_cm.append(src)
        o_g, lse_g = _attn_a_fwd(src, 0, qb, BF16, name=f"attn_a_fwd_{g}")
        o_groups.append(_natural(o_g))
        lse_groups.append(_natural(lse_g))
    ypre_a, om_a, lse_a = _merge_gate_fwd(o_groups, lse_groups, z_a, 0)
    y_a = _matmul(ypre_a, w_out_a, "nn", F32, name="a_out")
    g_kvn = kv_norm.reshape(1, d)
    g_lat = kv_latent_norm.reshape(1, B_KV_LORA)
    h1, hn_kv, hn_b = _post_norm_block(y_a, g_a_post, h0, [g_kvn, b_pre_norm], name="a_post_norm")

    ckr = _matmul(hn_kv, w_down_p, "nn", F32, name="kv_down")
    c_kv, k_rope = _kv_latent_fwd(ckr, g_lat, tabs_b)
    kvup = _matmul(c_kv, w_up_cat, "nn", BF16, name="kv_up")
    z_b = _matmul(hn_b, w_z, "nn", BF16, name="b_proj_z")
    cq_raw = _matmul(hn_b, w_cq, "nn", F32, name="b_proj_q")
    c_q = _rms_fwd(cq_raw, b_q_norm, BF16, name="b_q_norm", tr=1024)
    always = lambda j: True
    q_cat = _matmul(c_q, w_q_up_p, "nn", BF16, name="b_q_up", rope=(tabs_b, always),
                    out_scale=(B_SCALE * LOG2E, always))
    r3 = lambda a: a.reshape(bl, t, a.shape[-1])
    tabs_b3 = tuple(r3(tb) for tb in tabs_b)
    ypre_b, o_b, lse_b, lse_rows_b = _mla_fwd(r3(q_cat), r3(kvup), r3(k_rope), r3(z_b), tq)
    y_b = _matmul(ypre_b.reshape(n, B_WIDTH), w_out_b, "nn", F32, name="b_out")
    dh2, loss_part = _post_norm_loss(y_b, b_post_norm, h1, loss_target.reshape(n, d))

    dy_b, dg_b_post = _rms_bwd(y_b, b_post_norm, dh2, BF16, name="b_post_norm_bwd", tr=1024)
    dypre_b = _matmul(dy_b, w_out_b, "nt", BF16, name="b_out_dx")
    dw_out_b = _matmul(ypre_b.reshape(n, B_WIDTH), dy_b, "tn", F32, name="b_out_dw", tm=1024, tk=2048)
    do_b, dz_b = _gate_bwd(dypre_b, o_b.reshape(n, B_WIDTH), z_b, 0, name="b_gate_bwd", with_delta=False)
    dq_cat, delta_rows_b = _mla_dq(r3(q_cat), r3(kvup), r3(k_rope), r3(do_b), o_b, lse_b, tabs_b3, tq)
    dq_cat = dq_cat.reshape(n, -1)
    dk_cat, dv_b = _mla_dkv(r3(q_cat), r3(kvup), r3(k_rope), r3(do_b), lse_rows_b, delta_rows_b, tq)
    dk_cat, dv_b = dk_cat.reshape(n, -1), dv_b.reshape(n, -1)
    dcq_n = _matmul(dq_cat, w_q_up_p, "nt", F32, name="b_q_up_dx")
    dw_q_up_p = _matmul(c_q, dq_cat, "tn", F32, name="b_q_up_dw", tm=1024, tk=2048)
    dcq, dg_b_q = _rms_bwd(cq_raw, b_q_norm, dcq_n, BF16, name="b_q_norm_bwd", tr=1024)
    dhn_b = _matmul(dz_b, w_z, "nt", F32, name="b_proj_z_dx")
    dhn_b = _matmul(dcq, w_cq, "nt", F32, name="b_proj_q_dx", add=dhn_b)
    dw_z = _matmul(hn_b, dz_b, "tn", F32, name="b_proj_z_dw", tm=1024, tk=2048)
    dw_cq = _matmul(hn_b, dcq, "tn", F32, name="b_proj_q_dw", tm=1024, tk=2048)
    dckv_n = _matmul(dk_cat, w_up_k, "nt", F32, name="kv_up_k_dx")
    dckv_n = _matmul(dv_b, w_up_v, "nt", F32, name="kv_up_v_dx", add=dckv_n)
    dw_up_k = _matmul(c_kv, dk_cat, "tn", F32, name="kv_up_k_dw", tm=1024, tk=2048)
    dw_up_v = _matmul(c_kv, dv_b, "tn", F32, name="kv_up_v_dw", tm=1024, tk=2048)
    dckr, dg_lat = _kv_latent_bwd(dckv_n, ckr, g_lat, dk_cat, tabs_b)
    dhn_kv = _matmul(dckr, w_down_p, "nt", F32, name="kv_down_dx")
    dw_down_p = _matmul(hn_kv, dckr, "tn", F32, name="kv_down_dw", tm=1024, tk=2048)
    dh1, dg_b_pre, dg_kvn = _rms_bwd_pair(h1, b_pre_norm, dhn_b, g_kvn, dhn_kv, dh2, name="h1_norms_bwd")

    dy_a, dg_a_post = _rms_bwd(y_a, g_a_post, dh1, BF16, name="a_post_norm_bwd", tr=1024)
    dypre_a = _matmul(dy_a, w_out_a, "nt", BF16, name="a_out_dx")
    dw_out_a = _matmul(ypre_a, dy_a, "tn", F32, name="a_out_dw", tm=1024, tk=2048)
    do_a, dz_a, delta_a = _gate_bwd(dypre_a, om_a, z_a, 0, name="a_gate_bwd", with_delta=True)
    dw_cols = A_IN_WIDTH // N_CHIPS
    dw_tn = _tile(dw_cols, 512)
    dw_kwargs = dict(tm=1024, tn=dw_tn, tk=4096, out_chunk_blocks=dw_cols // dw_tn)
    r_big = _matmul(hn_a, dz_a, "tn", F32, name="a_proj_dw_z", out_full=(N_CHIPS, d, dw_cols),
                    out_joff=z_blk_a * A_WIDTH // dw_tn, **dw_kwargs)
    dqkvs = []
    for g, dil in enumerate(A_DILATIONS):
        cm = lambda a: _class_major(a, bl, t, dil)
        swap = lambda a: jnp.swapaxes(a, 2, 3)
        lse_cm, delta_cm = cm(lse_a), cm(delta_a)
        tabs_g = tuple(tb.reshape(bl, dil, t // dil, LANES) for tb in tabs_cm[g])
        dqkv = _attn_a_bwd(qkv_cm[g], 0, cm(do_a), lse_cm, delta_cm, swap(lse_cm), swap(delta_cm),
                           tabs_g, qb, name=f"attn_a_bwd_{g}").reshape(n, 3 * A_WIDTH)
        dqkvs.append(dqkv)
        r_big = _matmul(hn_cm[g], dqkv, "tn", F32, name=f"a_proj_dw_{g}", out_into=r_big,
                        out_joff=3 * g * A_WIDTH // dw_tn, **dw_kwargs)
    r_outs = jnp.concatenate([dw_out_a.reshape(N_CHIPS, A_WIDTH // N_CHIPS, d),
                              dw_out_b.reshape(N_CHIPS, B_WIDTH // N_CHIPS, d)], axis=1)

    bulk = [r_big, r_outs]
    recv_b = _pair_send_other_half(bulk, name="reduce_pair_send")
    halves_b = [_add_my_half(s, p, core_arr, BF16, name=f"reduce_pair_add_{i}")
                for i, (s, p) in enumerate(zip(bulk, recv_b))]
    send_sems, recv_sems, sent_b, lands_b, token = _chip_exchange_start(halves_b, name="reduce_exchange_start")

    dhn_a = _matmul(dz_a, w_in_a, "nt", F32, name="a_proj_dx_z", b_koff=z_blk_a, after=token)
    dhn_more = []
    for g, dil in enumerate(A_DILATIONS):
        tk_dx = 3 * A_WIDTH
        if dil == 1:
            dhn_a = _matmul(dqkvs[g], w_in_a, "nt", F32, name=f"a_proj_dx_{g}", add=dhn_a, tk=tk_dx, b_koff=g,
                            after=token)
        else:
            part = _matmul(dqkvs[g], w_in_a, "nt", BF16, name=f"a_proj_dx_{g}", tk=tk_dx, b_koff=g, after=token)
            dhn_more.append(_natural(part.reshape(bl, dil, t // dil, d)))
    grad_x, dg_a_pre = _rms_bwd(h0, g_a_pre, dhn_a, F32, name="a_pre_norm_bwd", adds=(dh1,),
                                dy_more=tuple(dhn_more))

    dw_up = jnp.concatenate([dw_up_k.reshape(B_KV_LORA, B_HEADS, LANES)[:, :, :B_NOPE],
                             dw_up_v.reshape(B_KV_LORA, B_HEADS, B_VDIM)], axis=2).reshape(B_KV_LORA, -1)
    dw_q_up = dw_q_up_p.reshape(B_Q_LORA, B_HEADS, LANES)[:, :, :B_QK_DIM].reshape(B_Q_LORA, -1)
    dw_down = jnp.concatenate([dw_down_p[:, :B_KV_LORA], dw_down_p[:, B_KV_LORA + B_NOPE:B_KV_LORA + B_NOPE + B_ROPE]], axis=1)
    dw_in_b = jnp.concatenate([dw_cq, dw_z], axis=1)
    vec_rep = [dg_kvn.reshape(-1), dg_lat.reshape(-1), dg_b_pre.reshape(-1), dg_b_q.reshape(-1),
               dg_b_post.reshape(-1), loss_part.reshape(-1)]
    vec_shapes = [(dq4,), (dq4,)] + [v.shape for v in vec_rep]
    down_c = dw_down.reshape(N_CHIPS, dq4, -1)
    up_c = _to_chunks_cols(dw_up)
    inb_c = _to_chunks_cols(dw_in_b)
    qup_c = _to_chunks_cols(dw_q_up)
    small_chunks = []
    for k in range(N_CHIPS):
        vecs = [dg_a_pre.reshape(-1)[k * dq4:(k + 1) * dq4], dg_a_post.reshape(-1)[k * dq4:(k + 1) * dq4]] + vec_rep
        small_chunks.append(_pack_rows([down_c[k], up_c[k], inb_c[k], qup_c[k]] + vecs, 32))
    r_small = jnp.stack(small_chunks)

    recv_s = _pair_send_other_half([r_small], name="reduce_pair_send_small")
    halves_s = [_add_my_half(r_small, recv_s[0], core_arr, F32, name="reduce_pair_add_small")]
    parts_s = list(_chip_exchange(halves_s, name="reduce_exchange_small"))
    parts_b = _chip_exchange_wait(send_sems, recv_sems, sent_b, lands_b, grad_x, name="reduce_exchange_wait")
    sums = [_sum_chips(p, own, chip_arr, name=f"reduce_chip_sum_{i}")
            for i, (p, own) in enumerate(zip(parts_b + parts_s, sent_b + halves_s))]
    others = _pair_swap(sums, name="reduce_pair_swap")
    g_big, g_outs_r, g_small_r = [_join_halves(m, o, core_arr, name=f"reduce_join_{i}")
                                  for i, (m, o) in enumerate(zip(sums, others))]

    grads = {}
    grads["a_w_in"] = g_big
    grads["a_w_out"] = g_outs_r[:A_WIDTH // N_CHIPS]
    grads["b_w_out"] = g_outs_r[A_WIDTH // N_CHIPS:]
    small_out_shapes = [down_c.shape[1:], up_c.shape[1:], inb_c.shape[1:], qup_c.shape[1:]] + vec_shapes
    (grads["kv_w_down"], grads["kv_w_up"], grads["b_w_in"], grads["b_w_q_up"], grads["a_pre_norm"],
     grads["a_post_norm"], grads["kv_norm"], grads["kv_latent_norm"], grads["b_pre_norm"], grads["b_q_norm"],
     grads["b_post_norm"], loss_sum) = _unpack(g_small_r.reshape(-1), small_out_shapes)

    weights = dict(a_pre_norm=a_pre_norm, a_w_in=a_w_in, a_w_out=a_w_out, a_post_norm=a_post_norm, kv_norm=kv_norm,
                   kv_w_down=kv_w_down, kv_latent_norm=kv_latent_norm, kv_w_up=kv_w_up, b_pre_norm=b_pre_norm,
                   b_w_in=b_w_in, b_q_norm=b_q_norm, b_w_q_up=b_w_q_up, b_w_out=b_w_out, b_post_norm=b_post_norm)
    names = list(weights)
    out_g, out_d, out_m, out_v = [], [], [], []
    for i, nm in enumerate(names):
        w = weights[nm]
        two_d = (1, w.shape[0]) if w.ndim == 1 else (w.shape[-2], w.shape[-1])
        gw = grads[nm].reshape(two_d)
        dlt, new_m, new_v = _adamw(w.reshape(two_d), gw, moments[i].reshape(two_d),
                                   moments[len(names) + i].reshape(two_d), name=f"adamw_{nm}")
        out_g.append(gw.reshape(w.shape))
        out_d.append(dlt.reshape(w.shape))
        out_m.append(new_m.reshape(w.shape))
        out_v.append(new_v.reshape(w.shape))
    return (loss_sum.reshape(()), grad_x.reshape(bl, t, d), *out_g, *out_d, *out_m, *out_v)


def kernel(x, positions, a_pre_norm, a_w_in, a_w_out, a_post_norm, kv_norm, kv_w_down, kv_latent_norm, kv_w_up, b_pre_norm, b_w_in, b_q_norm, b_w_q_up, b_w_out, b_post_norm, loss_target, m_a_pre_norm, m_a_w_in, m_a_w_out, m_a_post_norm, m_kv_norm, m_kv_w_down, m_kv_latent_norm, m_kv_w_up, m_b_pre_norm, m_b_w_in, m_b_q_norm, m_b_w_q_up, m_b_w_out, m_b_post_norm, v_a_pre_norm, v_a_w_in, v_a_w_out, v_a_post_norm, v_kv_norm, v_kv_w_down, v_kv_latent_norm, v_kv_w_up, v_b_pre_norm, v_b_w_in, v_b_q_norm, v_b_w_q_up, v_b_w_out, v_b_post_norm):
    moments = (m_a_pre_norm, m_a_w_in, m_a_w_out, m_a_post_norm, m_kv_norm, m_kv_w_down, m_kv_latent_norm, m_kv_w_up,
               m_b_pre_norm, m_b_w_in, m_b_q_norm, m_b_w_q_up, m_b_w_out, m_b_post_norm,
               v_a_pre_norm, v_a_w_in, v_a_w_out, v_a_post_norm, v_kv_norm, v_kv_w_down, v_kv_latent_norm, v_kv_w_up,
               v_b_pre_norm, v_b_w_in, v_b_q_norm, v_b_w_q_up, v_b_w_out, v_b_post_norm)
    return _train_step(x, positions, a_pre_norm, a_w_in, a_w_out, a_post_norm, kv_norm, kv_w_down, kv_latent_norm,
                       kv_w_up, b_pre_norm, b_w_in, b_q_norm, b_w_q_up, b_w_out, b_post_norm, loss_target, moments)
```

```python
import math

import jax
import jax.numpy as jnp
from jax import lax
from jax.experimental import pallas as pl
from jax.experimental.pallas import tpu as pltpu

F32 = jnp.float32
BF16 = jnp.bfloat16
MESH = pl.DeviceIdType.MESH

NORM_EPS = 1e-6
NEG = -1e30
LANES = 128
VMEM_LIMIT = 56 * 1024 * 1024
LOG2E = math.log2(math.e)
LN2 = math.log(2.0)

A_GROUPS = 3
A_DILATIONS = (1, 4, 16)
A_HEADS = 8
A_HEAD_DIM = 128
A_WIDTH = A_HEADS * A_HEAD_DIM
A_ROPE_THETA = 500000.0
A_IN_WIDTH = A_GROUPS * 3 * A_WIDTH + A_WIDTH
A_SCALE = A_HEAD_DIM ** -0.5

B_HEADS = 16
B_NOPE = 64
B_ROPE = 32
B_QK_DIM = B_NOPE + B_ROPE
B_VDIM = 64
B_WIDTH = B_HEADS * B_VDIM
B_Q_LORA = 384
B_KV_LORA = 256
B_ROPE_THETA = 10000.0
B_SCALE = B_QK_DIM ** -0.5

ADAM_LR = 0.001
ADAM_B1 = 0.9
ADAM_B2 = 0.999
ADAM_EPS = 1e-08
ADAM_WD = 0.01
ADAM_STEP = 10

N_CHIPS = 4
PACK_COLS = 512


def _params(sem=None):
    return pltpu.CompilerParams(dimension_semantics=sem, vmem_limit_bytes=VMEM_LIMIT)


def _tile(n, want):
    t = min(n, want)
    assert n % t == 0, (n, want)
    return t


def _row_tile(n, want):
    for t in range(min(n, want), 0, -1):
        if n % t == 0 and (t % 16 == 0 or t == n):
            return t
    return n


def _rope_tables(positions, theta, lane0):
    half = 16
    inv_freq = 1.0 / (theta ** (jnp.arange(half, dtype=F32) * (2.0 / (2 * half))))
    n = positions.size
    per_row = LANES // half
    pos = jnp.repeat(positions.astype(F32).reshape(n // per_row, per_row), half, axis=1)
    ang = pos * jnp.tile(inv_freq, per_row)
    cos, sin = lax.optimization_barrier((jnp.cos(ang), jnp.sin(ang)))
    cos, sin = cos.reshape(n, half), sin.reshape(n, half)
    pre = jnp.zeros((n, lane0), F32)
    post = jnp.zeros((n, LANES - lane0 - 2 * half), F32)
    z16 = jnp.zeros((n, half), F32)
    c = jnp.concatenate([pre + 1.0, cos, cos, post + 1.0], axis=1)
    sa = jnp.concatenate([pre, -sin, z16, post], axis=1)
    sb = jnp.concatenate([pre, z16, sin, post], axis=1)
    return lax.optimization_barrier((c, sa, sb))


def _rope_apply(x, c, sa, sb, sign):
    k = x.shape[1] // LANES
    if k > 1:
        c, sa, sb = (jnp.concatenate([t] * k, axis=1) for t in (c, sa, sb))
    w = x.shape[1]
    up = pltpu.roll(x, w - 16, 1)
    dn = pltpu.roll(x, 16, 1)
    if sign > 0:
        return x * c + up * sa + dn * sb
    return x * c - up * sa - dn * sb


def _matmul(a, b, mode, out_dtype, *, name, tm=None, tn=1024, tk=None, add=None, rope=None,
            out_scale=None, b_koff=0, b_cols=None, out_into=None, out_full=None, out_joff=0,
            out_chunk_blocks=None, after=None):
    if mode == "nn":
        m, k = a.shape
        n = b.shape[1]
    elif mode == "nt":
        m, k = a.shape
        n = b.shape[0]
    else:
        k, m = a.shape
        n = b.shape[1]
    b_j0 = 0
    if b_cols is not None:
        tn = _tile(n, tn)
        b_j0, n = b_cols[0], b_cols[1] * tn
    if tm is None:
        if mode == "nt":
            tm = 512 if k > 2048 else 1024
        else:
            tm = 2048 if (k <= 512 and rope is None) else 1024
    if tk is None:
        tk = 3072 if mode == "nt" else 1024
    tm, tn, tk = _tile(m, tm), _tile(n, tn), _tile(k, tk)
    nk = k // tk
    if mode == "nn":
        a_spec = pl.BlockSpec((tm, tk), lambda j, i, kk: (i, kk))
        b_spec = pl.BlockSpec((tk, tn), lambda j, i, kk: (kk, j + b_j0))
        dims = (((1,), (0,)), ((), ()))
    elif mode == "nt":
        a_spec = pl.BlockSpec((tm, tk), lambda j, i, kk: (i, kk))
        b_spec = pl.BlockSpec((tn, tk), lambda j, i, kk: (j, kk + b_koff))
        dims = (((1,), (1,)), ((), ()))
    else:
        a_spec = pl.BlockSpec((tk, tm), lambda j, i, kk: (kk, i))
        b_spec = pl.BlockSpec((tk, tn), lambda j, i, kk: (kk, j))
        dims = (((0,), (0,)), ((), ()))
    operands = [a, b]
    in_specs = [a_spec, b_spec]
    if add is not None:
        operands.append(add)
        in_specs.append(pl.BlockSpec((tm, tn), lambda j, i, kk: (i, j)))
    if rope is not None:
        tables, rope_pred = rope
        for t in tables:
            operands.append(t)
            in_specs.append(pl.BlockSpec((tm, LANES), lambda j, i, kk: (i, 0)))
    aliases = {}
    if out_into is not None:
        aliases = {len(operands): 0}
        operands.append(out_into)
        in_specs.append(pl.BlockSpec(memory_space=pl.ANY))
        out_shape = jax.ShapeDtypeStruct(out_into.shape, out_into.dtype)
    elif out_full is not None:
        out_shape = jax.ShapeDtypeStruct(out_full, out_dtype)
    else:
        out_shape = jax.ShapeDtypeStruct((m, n), out_dtype)
    if after is not None:
        operands.append(after)
        in_specs.append(pl.BlockSpec(memory_space=pl.ANY))
    if out_chunk_blocks is not None:
        out_spec = pl.BlockSpec((None, tm, tn), lambda j, i, kk: ((j + out_joff) // out_chunk_blocks, i,
                                                                  (j + out_joff) % out_chunk_blocks))
    else:
        out_spec = pl.BlockSpec((tm, tn), lambda j, i, kk: (i, j + out_joff))

    def body(*refs):
        a_ref, b_ref = refs[0], refs[1]
        pos = 2
        add_ref = None
        if add is not None:
            add_ref = refs[pos]
            pos += 1
        tab_refs = None
        if rope is not None:
            tab_refs = refs[pos:pos + 3]
            pos += 3
        if out_into is not None:
            pos += 1
        if after is not None:
            pos += 1
        o_ref = refs[pos]
        acc_ref = refs[pos + 1] if nk > 1 else None

        def finish(res):
            if add_ref is not None:
                res = res + add_ref[...].astype(F32)
            if tab_refs is None:
                o_ref[...] = res.astype(o_ref.dtype)
                return
            j = pl.program_id(0)
            flag = rope_pred(j)

            roped = _rope_apply(res, tab_refs[0][...], tab_refs[1][...], tab_refs[2][...], 1)
            if out_scale is not None:
                value, scale_pred = out_scale
                use = scale_pred(j)
                roped = roped * (value if use is True else jnp.where(use, value, 1.0))
            if flag is True:
                o_ref[...] = roped.astype(o_ref.dtype)
                return

            @pl.when(flag)
            def _():
                o_ref[...] = roped.astype(o_ref.dtype)

            @pl.when(jnp.logical_not(flag))
            def _():
                o_ref[...] = res.astype(o_ref.dtype)

        part = lax.dot_general(a_ref[...].astype(BF16), b_ref[...].astype(BF16), dims,
                               preferred_element_type=F32)
        if nk == 1:
            finish(part)
            return
        kk = pl.program_id(2)

        @pl.when(kk == 0)
        def _():
            acc_ref[...] = part

        @pl.when(kk > 0)
        def _():
            acc_ref[...] += part

        @pl.when(kk == nk - 1)
        def _():
            finish(acc_ref[...])

    return pl.pallas_call(
        body, name=name, grid=(n // tn, m // tm, nk), in_specs=in_specs, out_specs=out_spec,
        out_shape=out_shape, input_output_aliases=aliases,
        scratch_shapes=[pltpu.VMEM((tm, tn), F32)] if nk > 1 else [],
        compiler_params=_params(("parallel", "parallel", "arbitrary")),
    )(*operands)


def _rms_fwd(x, g, out_dtype, *, name, add=None, tr=512):
    n, d = x.shape
    tr = _tile(n, tr)
    row = pl.BlockSpec((tr, d), lambda i: (i, 0))
    vec = pl.BlockSpec((1, d), lambda i: (0, 0))

    def body(*refs):
        x_ref, g_ref = refs[0], refs[1]
        o_ref = refs[-1]
        xv = x_ref[...].astype(F32)
        r = lax.rsqrt(jnp.mean(xv * xv, axis=-1, keepdims=True) + NORM_EPS)
        y = xv * r * g_ref[...]
        if add is not None:
            y = refs[2][...] + y
        o_ref[...] = y.astype(o_ref.dtype)

    ops = [x, g] + ([add] if add is not None else [])
    specs = [row, vec] + ([row] if add is not None else [])
    return pl.pallas_call(
        body, name=name, grid=(n // tr,), in_specs=specs, out_specs=row,
        out_shape=jax.ShapeDtypeStruct((n, d), out_dtype), compiler_params=_params(("parallel",)),
    )(*ops)


def _rms_bwd(x, g, dy, out_dtype, *, name, adds=(), dy_more=(), tr=512):
    n, d = x.shape
    tr = _tile(n, tr)
    steps = n // tr
    row = pl.BlockSpec((tr, d), lambda i: (i, 0))
    vec = pl.BlockSpec((1, d), lambda i: (0, 0))
    na = len(adds) + len(dy_more)

    def body(*refs):
        x_ref, g_ref, dy_ref = refs[:3]
        add_refs = refs[3:3 + len(adds)]
        more_refs = refs[3 + len(adds):3 + na]
        dx_ref, dg_ref, acc_ref = refs[3 + na:]
        i = pl.program_id(0)
        xv = x_ref[...].astype(F32)
        r = lax.rsqrt(jnp.mean(xv * xv, axis=-1, keepdims=True) + NORM_EPS)
        xh = xv * r
        dyv = dy_ref[...].astype(F32)
        for m_ref in more_refs:
            dyv = dyv + m_ref[...].astype(F32)
        part = (dyv * xh).reshape(tr // 8, 8, d).sum(axis=0)

        @pl.when(i == 0)
        def _():
            acc_ref[...] = part

        @pl.when(i > 0)
        def _():
            acc_ref[...] += part

        t = dyv * g_ref[...]
        dx = r * (t - xh * jnp.mean(t * xh, axis=-1, keepdims=True))
        for a_ref in add_refs:
            dx = dx + a_ref[...].astype(F32)
        dx_ref[...] = dx.astype(dx_ref.dtype)

        @pl.when(i == steps - 1)
        def _():
            dg_ref[...] = jnp.sum(acc_ref[...], axis=0, keepdims=True)

    return pl.pallas_call(
        body, name=name, grid=(steps,), in_specs=[row, vec, row] + [row] * na,
        out_specs=(row, vec),
        out_shape=(jax.ShapeDtypeStruct((n, d), out_dtype), jax.ShapeDtypeStruct((1, d), F32)),
        scratch_shapes=[pltpu.VMEM((8, d), F32)], compiler_params=_params(("arbitrary",)),
    )(x, g, dy, *adds, *dy_more)


def _rms(xv, g):
    return xv * lax.rsqrt(jnp.mean(xv * xv, axis=-1, keepdims=True) + NORM_EPS) * g


def _post_norm_block(y, g, h_in, next_gains, *, name, tr=1024):
    n, d = y.shape
    tr = _tile(n, tr)
    nk = len(next_gains)
    row = pl.BlockSpec((tr, d), lambda i: (i, 0))
    vec = pl.BlockSpec((1, d), lambda i: (0, 0))

    def body(*refs):
        y_ref, g_ref, h_ref = refs[:3]
        gk_refs = refs[3:3 + nk]
        o_ref = refs[3 + nk]
        hn_refs = refs[4 + nk:]
        h = h_ref[...] + _rms(y_ref[...], g_ref[...])
        o_ref[...] = h
        for gk_ref, hn_ref in zip(gk_refs, hn_refs):
            hn_ref[...] = _rms(h, gk_ref[...]).astype(BF16)

    return pl.pallas_call(
        body, name=name, grid=(n // tr,), in_specs=[row, vec, row] + [vec] * nk,
        out_specs=(row,) * (1 + nk),
        out_shape=(jax.ShapeDtypeStruct((n, d), F32),) + (jax.ShapeDtypeStruct((n, d), BF16),) * nk,
        compiler_params=_params(("parallel",)),
    )(y, g, h_in, *next_gains)


def _post_norm_loss(y, g, h_in, target, *, tr=1024):
    n, d = y.shape
    tr = _tile(n, tr)
    steps = n // tr
    row = pl.BlockSpec((tr, d), lambda i: (i, 0))

    def body(y_ref, g_ref, h_ref, t_ref, dh_ref, loss_ref, acc_ref):
        i = pl.program_id(0)
        e = h_ref[...] + _rms(y_ref[...], g_ref[...]) - t_ref[...]
        dh_ref[...] = e / d
        part = (e * e).reshape(tr // 8, 8, d).sum(axis=0)

        @pl.when(i == 0)
        def _():
            acc_ref[...] = part

        @pl.when(i > 0)
        def _():
            acc_ref[...] += part

        @pl.when(i == steps - 1)
        def _():
            s = jnp.sum(jnp.sum(acc_ref[...], axis=-1, keepdims=True), axis=0, keepdims=True)
            loss_ref[...] = 0.5 * s / d

    return pl.pallas_call(
        body, name="b_post_norm_loss", grid=(steps,),
        in_specs=[row, pl.BlockSpec((1, d), lambda i: (0, 0)), row, row],
        out_specs=(row, pl.BlockSpec((1, 1), lambda i: (0, 0))),
        out_shape=(jax.ShapeDtypeStruct((n, d), F32), jax.ShapeDtypeStruct((1, 1), F32)),
        scratch_shapes=[pltpu.VMEM((8, d), F32)], compiler_params=_params(("arbitrary",)),
    )(y, g, h_in, target)


def _rms_bwd_pair(x, g1, dy1, g2, dy2, add, *, name, tr=512):
    n, d = x.shape
    tr = _tile(n, tr)
    steps = n // tr
    row = pl.BlockSpec((tr, d), lambda i: (i, 0))
    vec = pl.BlockSpec((1, d), lambda i: (0, 0))

    def body(x_ref, g1_ref, d1_ref, g2_ref, d2_ref, add_ref, dx_ref, dg1_ref, dg2_ref, acc_ref):
        i = pl.program_id(0)
        xv = x_ref[...]
        r = lax.rsqrt(jnp.mean(xv * xv, axis=-1, keepdims=True) + NORM_EPS)
        xh = xv * r
        dx = add_ref[...]
        for k, (g_ref, d_ref) in enumerate(((g1_ref, d1_ref), (g2_ref, d2_ref))):
            dyv = d_ref[...].astype(F32)
            part = (dyv * xh).reshape(tr // 8, 8, d).sum(axis=0)

            @pl.when(i == 0)
            def _(part=part, k=k):
                acc_ref[k] = part

            @pl.when(i > 0)
            def _(part=part, k=k):
                acc_ref[k] += part

            t = dyv * g_ref[...]
            dx = dx + r * (t - xh * jnp.mean(t * xh, axis=-1, keepdims=True))
        dx_ref[...] = dx

        @pl.when(i == steps - 1)
        def _():
            dg1_ref[...] = jnp.sum(acc_ref[0], axis=0, keepdims=True)
            dg2_ref[...] = jnp.sum(acc_ref[1], axis=0, keepdims=True)

    return pl.pallas_call(
        body, name=name, grid=(steps,), in_specs=[row, vec, row, vec, row, row],
        out_specs=(row, vec, vec),
        out_shape=(jax.ShapeDtypeStruct((n, d), F32), jax.ShapeDtypeStruct((1, d), F32),
                   jax.ShapeDtypeStruct((1, d), F32)),
        scratch_shapes=[pltpu.VMEM((2, 8, d), F32)], compiler_params=_params(("arbitrary",)),
    )(x, g1, dy1, g2, dy2, add)


def _kv_latent_fwd(ckr, g_lat, tabs, *, tr=512):
    n = ckr.shape[0]
    tr = _tile(n, tr)
    lat = B_KV_LORA

    def body(c_ref, k_ref, g_ref, tc, tsa, tsb, ckv_ref, kr_ref):
        xv = c_ref[...]
        r = lax.rsqrt(jnp.mean(xv * xv, axis=-1, keepdims=True) + NORM_EPS)
        ckv_ref[...] = (xv * r * g_ref[...]).astype(BF16)
        kr_ref[...] = _rope_apply(k_ref[...], tc[...], tsa[...], tsb[...], 1).astype(BF16)

    tab = pl.BlockSpec((tr, LANES), lambda i: (i, 0))
    return pl.pallas_call(
        body, name="kv_latent_fwd", grid=(n // tr,),
        in_specs=[pl.BlockSpec((tr, lat), lambda i: (i, 0)),
                  pl.BlockSpec((tr, LANES), lambda i: (i, lat // LANES)),
                  pl.BlockSpec((1, lat), lambda i: (0, 0)), tab, tab, tab],
        out_specs=(pl.BlockSpec((tr, lat), lambda i: (i, 0)), tab),
        out_shape=(jax.ShapeDtypeStruct((n, lat), BF16), jax.ShapeDtypeStruct((n, LANES), BF16)),
        compiler_params=_params(("parallel",)),
    )(ckr, ckr, g_lat, *tabs)


def _kv_latent_bwd(dckv, ckr, g_lat, dk_cat, tabs, *, tr=512):
    n = ckr.shape[0]
    tr = _tile(n, tr)
    steps = n // tr
    lat = B_KV_LORA
    wk = dk_cat.shape[1]

    def body(d_ref, c_ref, g_ref, dk_ref, tc, tsa, tsb, o_ref, dg_ref, acc_ref):
        i = pl.program_id(0)
        xv = c_ref[...]
        r = lax.rsqrt(jnp.mean(xv * xv, axis=-1, keepdims=True) + NORM_EPS)
        xh = xv * r
        dyv = d_ref[...]
        part = (dyv * xh).reshape(tr // 8, 8, lat).sum(axis=0)

        @pl.when(i == 0)
        def _():
            acc_ref[...] = part

        @pl.when(i > 0)
        def _():
            acc_ref[...] += part

        t = dyv * g_ref[...]
        dx = r * (t - xh * jnp.mean(t * xh, axis=-1, keepdims=True))
        o_ref[:, 0:lat] = dx.astype(o_ref.dtype)
        dkr = dk_ref[:, 0:LANES].astype(F32)
        for h in range(1, wk // LANES):
            dkr = dkr + dk_ref[:, h * LANES:(h + 1) * LANES].astype(F32)
        o_ref[:, lat:lat + LANES] = _rope_apply(dkr, tc[...], tsa[...], tsb[...], -1).astype(o_ref.dtype)

        @pl.when(i == steps - 1)
        def _():
            dg_ref[...] = jnp.sum(acc_ref[...], axis=0, keepdims=True)

    tab = pl.BlockSpec((tr, LANES), lambda i: (i, 0))
    return pl.pallas_call(
        body, name="kv_latent_bwd", grid=(steps,),
        in_specs=[pl.BlockSpec((tr, lat), lambda i: (i, 0)), pl.BlockSpec((tr, lat), lambda i: (i, 0)),
                  pl.BlockSpec((1, lat), lambda i: (0, 0)), pl.BlockSpec((tr, wk), lambda i: (i, 0)),
                  tab, tab, tab],
        out_specs=(pl.BlockSpec((tr, lat + LANES), lambda i: (i, 0)), pl.BlockSpec((1, lat), lambda i: (0, 0))),
        out_shape=(jax.ShapeDtypeStruct((n, lat + LANES), BF16), jax.ShapeDtypeStruct((1, lat), F32)),
        scratch_shapes=[pltpu.VMEM((8, lat), F32)], compiler_params=_params(("arbitrary",)),
    )(dckv, ckr, g_lat, dk_cat, *tabs)


def _sigmoid(z):
    return 1.0 / (1.0 + jnp.exp(-z))


def _lane_place(cols, width):
    rows = cols[0].shape[0]
    lane = lax.broadcasted_iota(jnp.int32, (rows, width), 1)
    out = jnp.zeros((rows, width), F32)
    for h, col in enumerate(cols):
        out = jnp.where(lane == h, col, out)
    return out


def _merge_gate_fwd(outs, lses, proj, z_block, *, tr=1024):
    n, w = outs[0].shape
    tr = _tile(n, tr)
    ng = len(outs)

    def body(*refs):
        o_refs = refs[:ng]
        l_refs = refs[ng:2 * ng]
        z_ref = refs[2 * ng]
        y_ref, om_ref, lse_ref = refs[2 * ng + 1:]
        ls = [r[...] for r in l_refs]
        mx = ls[0]
        for l in ls[1:]:
            mx = jnp.maximum(mx, l)
        ssum = jnp.exp2(ls[0] - mx)
        for l in ls[1:]:
            ssum = ssum + jnp.exp2(l - mx)
        tot = mx + jnp.log2(ssum)
        lse_ref[...] = tot
        ws = [jnp.exp2(l - tot) for l in ls]
        for h in range(A_HEADS):
            sl = slice(h * A_HEAD_DIM, (h + 1) * A_HEAD_DIM)
            o = ws[0][:, h:h + 1] * o_refs[0][:, sl]
            for gi in range(1, ng):
                o = o + ws[gi][:, h:h + 1] * o_refs[gi][:, sl]
            z = z_ref[:, sl].astype(F32)
            om_ref[:, sl] = o.astype(BF16)
            y_ref[:, sl] = (o * (z * _sigmoid(z))).astype(BF16)

    row = pl.BlockSpec((tr, w), lambda i: (i, 0))
    lrow = pl.BlockSpec((tr, A_HEADS), lambda i: (i, 0))
    return pl.pallas_call(
        body, name="merge_gate_fwd", grid=(n // tr,),
        in_specs=[row] * ng + [lrow] * ng + [pl.BlockSpec((tr, w), lambda i: (i, z_block))],
        out_specs=(row, row, lrow),
        out_shape=(jax.ShapeDtypeStruct((n, w), BF16), jax.ShapeDtypeStruct((n, w), BF16),
                   jax.ShapeDtypeStruct((n, A_HEADS), F32)),
        compiler_params=_params(("parallel",)),
    )(*outs, *lses, proj)


def _gate_bwd(dy, o, z_arr, z_block, *, name, with_delta, tr=512):
    n, w = dy.shape
    tr = _tile(n, tr)

    def body(*refs):
        dy_ref, o_ref, z_ref, do_ref, dz_ref = refs[:5]
        dyv = dy_ref[...].astype(F32)
        ov = o_ref[...].astype(F32)
        z = z_ref[...].astype(F32)
        sig = _sigmoid(z)
        do = dyv * (z * sig)
        do_ref[...] = do.astype(BF16)
        dz_ref[...] = (dyv * ov * (sig * (1.0 + z * (1.0 - sig)))).astype(BF16)
        if with_delta:
            prod = do * ov
            cols = [jnp.sum(prod[:, h * A_HEAD_DIM:(h + 1) * A_HEAD_DIM], axis=-1, keepdims=True)
                    for h in range(A_HEADS)]
            refs[5][...] = _lane_place(cols, A_HEADS)

    row = pl.BlockSpec((tr, w), lambda i: (i, 0))
    out_specs = [row, row]
    out_shape = [jax.ShapeDtypeStruct((n, w), BF16), jax.ShapeDtypeStruct((n, w), BF16)]
    if with_delta:
        out_specs.append(pl.BlockSpec((tr, A_HEADS), lambda i: (i, 0)))
        out_shape.append(jax.ShapeDtypeStruct((n, A_HEADS), F32))
    return pl.pallas_call(
        body, name=name, grid=(n // tr,),
        in_specs=[row, row, pl.BlockSpec((tr, w), lambda i: (i, z_block))],
        out_specs=tuple(out_specs), out_shape=tuple(out_shape), compiler_params=_params(("parallel",)),
    )(dy, o, z_arr)


def _dot_nt(a, b):
    return lax.dot_general(a, b, (((1,), (1,)), ((), ())), preferred_element_type=F32)


def _dot_nn(a, b):
    return lax.dot_general(a, b, (((1,), (0,)), ((), ())), preferred_element_type=F32)


def _attn_a_fwd(qkv, cb0, qb, out_dtype, *, name):
    bl, dil, ln, _ = qkv.shape
    nb = ln // qb
    hw = A_WIDTH
    heads = range(A_HEADS)
    sls = [slice(h * A_HEAD_DIM, (h + 1) * A_HEAD_DIM) for h in heads]

    def body(*refs):
        if nb > 1:
            q_ref, kc_ref, vc_ref, kp_ref, vp_ref, o_ref, lse_ref = refs
        else:
            q_ref, kc_ref, vc_ref, o_ref, lse_ref = refs
        i = pl.program_id(2)
        qi = lax.broadcasted_iota(jnp.int32, (qb, qb), 0)
        ki = lax.broadcasted_iota(jnp.int32, (qb, qb), 1)
        mask_c = ki <= qi
        mask_p = jnp.logical_and(ki >= qi, i >= 1)
        s_c = [jnp.where(mask_c, _dot_nt(q_ref[:, sls[h]], kc_ref[:, sls[h]]), NEG) for h in heads]
        m = [jnp.max(s_c[h], axis=-1, keepdims=True) for h in heads]
        if nb > 1:
            s_p = [jnp.where(mask_p, _dot_nt(q_ref[:, sls[h]], kp_ref[:, sls[h]]), NEG) for h in heads]
            m = [jnp.maximum(m[h], jnp.max(s_p[h], axis=-1, keepdims=True)) for h in heads]
        p_c = [jnp.exp2(s_c[h] - m[h]) for h in heads]
        l = [jnp.sum(p_c[h], axis=-1, keepdims=True) for h in heads]
        acc = [_dot_nn(p_c[h].astype(BF16), vc_ref[:, sls[h]]) for h in heads]
        if nb > 1:
            p_p = [jnp.exp2(s_p[h] - m[h]) for h in heads]
            l = [l[h] + jnp.sum(p_p[h], axis=-1, keepdims=True) for h in heads]
            acc = [acc[h] + _dot_nn(p_p[h].astype(BF16), vp_ref[:, sls[h]]) for h in heads]
        for h in heads:
            o_ref[:, sls[h]] = (acc[h] / l[h]).astype(o_ref.dtype)
        lse_ref[...] = _lane_place([m[h] + jnp.log2(l[h]) for h in heads], A_HEADS)

    def spec(off, prev):
        if prev:
            return pl.BlockSpec((None, None, qb, hw), lambda b, r, i: (b, r, jnp.maximum(i - 1, 0), cb0 + off))
        return pl.BlockSpec((None, None, qb, hw), lambda b, r, i: (b, r, i, cb0 + off))

    return pl.pallas_call(
        body, name=name, grid=(bl, dil, nb),
        in_specs=[spec(0, False), spec(1, False), spec(2, False)] + ([spec(1, True), spec(2, True)] if nb > 1 else []),
        out_specs=(pl.BlockSpec((None, None, qb, hw), lambda b, r, i: (b, r, i, 0)),
                   pl.BlockSpec((None, None, qb, A_HEADS), lambda b, r, i: (b, r, i, 0))),
        out_shape=(jax.ShapeDtypeStruct((bl, dil, ln, hw), out_dtype),
                   jax.ShapeDtypeStruct((bl, dil, ln, A_HEADS), F32)),
        compiler_params=_params(("parallel", "parallel", "arbitrary")),
    )(*([qkv] * (5 if nb > 1 else 3)))


def _attn_a_bwd(qkv, cb0, do, lse, delta, lse_t, delta_t, tabs, qb, *, name):
    bl, dil, ln, _ = qkv.shape
    nb = ln // qb
    hw = A_WIDTH

    def body(*refs):
        if nb > 1:
            (q_ref, kc_ref, vc_ref, do_ref, lse_ref, dl_ref, lt_ref, dt_ref, tc, tsa, tsb,
             qn_ref, kp_ref, vp_ref, don_ref, ltn_ref, dtn_ref, o_ref) = refs
        else:
            q_ref, kc_ref, vc_ref, do_ref, lse_ref, dl_ref, lt_ref, dt_ref, tc, tsa, tsb, o_ref = refs
        i = pl.program_id(2)
        row = lax.broadcasted_iota(jnp.int32, (qb, qb), 0)
        col = lax.broadcasted_iota(jnp.int32, (qb, qb), 1)
        m_qc = col <= row
        m_kc = row <= col
        m_qp = jnp.logical_and(col >= row, i >= 1)
        m_kn = jnp.logical_and(row >= col, i + 1 < nb)
        c, sa, sb = tc[...], tsa[...], tsb[...]
        heads = range(A_HEADS)
        sls = [slice(h * A_HEAD_DIM, (h + 1) * A_HEAD_DIM) for h in heads]
        q, kc = [q_ref[:, sl] for sl in sls], [kc_ref[:, sl] for sl in sls]
        vc, dov = [vc_ref[:, sl] for sl in sls], [do_ref[:, sl] for sl in sls]
        lse_c = [lse_ref[:, h:h + 1] for h in heads]
        dl_c = [dl_ref[:, h:h + 1] for h in heads]
        s = [_dot_nt(q[h], kc[h]) for h in heads]
        st = [_dot_nt(kc[h], q[h]) for h in heads]
        dp = [_dot_nt(dov[h], vc[h]) for h in heads]
        dpt = [_dot_nt(vc[h], dov[h]) for h in heads]
        p = [jnp.exp2(jnp.where(m_qc, s[h], NEG) - lse_c[h]) for h in heads]
        pt = [jnp.exp2(jnp.where(m_kc, st[h], NEG) - lt_ref[h:h + 1, :]) for h in heads]
        dq = [_dot_nn((p[h] * (dp[h] - dl_c[h])).astype(BF16), kc[h]) for h in heads]
        dk = [_dot_nn((pt[h] * (dpt[h] - dt_ref[h:h + 1, :])).astype(BF16), q[h]) for h in heads]
        dv = [_dot_nn(pt[h].astype(BF16), dov[h]) for h in heads]
        if nb > 1:
            kp, vp = [kp_ref[:, sl] for sl in sls], [vp_ref[:, sl] for sl in sls]
            qn, don = [qn_ref[:, sl] for sl in sls], [don_ref[:, sl] for sl in sls]
            s = [_dot_nt(q[h], kp[h]) for h in heads]
            st = [_dot_nt(kc[h], qn[h]) for h in heads]
            dp = [_dot_nt(dov[h], vp[h]) for h in heads]
            dpt = [_dot_nt(vc[h], don[h]) for h in heads]
            p = [jnp.exp2(jnp.where(m_qp, s[h], NEG) - lse_c[h]) for h in heads]
            pt = [jnp.exp2(jnp.where(m_kn, st[h], NEG) - ltn_ref[h:h + 1, :]) for h in heads]
            dq = [dq[h] + _dot_nn((p[h] * (dp[h] - dl_c[h])).astype(BF16), kp[h]) for h in heads]
            dk = [dk[h] + _dot_nn((pt[h] * (dpt[h] - dtn_ref[h:h + 1, :])).astype(BF16), qn[h]) for h in heads]
            dv = [dv[h] + _dot_nn(pt[h].astype(BF16), don[h]) for h in heads]
        for h in heads:
            o_ref[:, h * A_HEAD_DIM:(h + 1) * A_HEAD_DIM] = _rope_apply(dq[h] * A_SCALE, c, sa, sb, -1).astype(BF16)
            o_ref[:, hw + h * A_HEAD_DIM:hw + (h + 1) * A_HEAD_DIM] = _rope_apply(dk[h] * LN2, c, sa, sb, -1).astype(BF16)
            o_ref[:, 2 * hw + h * A_HEAD_DIM:2 * hw + (h + 1) * A_HEAD_DIM] = dv[h].astype(BF16)

    def cur(w, col):
        return pl.BlockSpec((None, None, qb, w), lambda b, r, i: (b, r, i, col))

    def prev(w, col):
        return pl.BlockSpec((None, None, qb, w), lambda b, r, i: (b, r, jnp.maximum(i - 1, 0), col))

    def nxt(w, col):
        return pl.BlockSpec((None, None, qb, w), lambda b, r, i: (b, r, jnp.minimum(i + 1, nb - 1), col))

    t_cur = pl.BlockSpec((None, None, A_HEADS, qb), lambda b, r, i: (b, r, 0, i))
    t_nxt = pl.BlockSpec((None, None, A_HEADS, qb), lambda b, r, i: (b, r, 0, jnp.minimum(i + 1, nb - 1)))
    in_specs = [cur(hw, cb0), cur(hw, cb0 + 1), cur(hw, cb0 + 2), cur(hw, 0), cur(A_HEADS, 0), cur(A_HEADS, 0),
                t_cur, t_cur, cur(LANES, 0), cur(LANES, 0), cur(LANES, 0)]
    operands = [qkv, qkv, qkv, do, lse, delta, lse_t, delta_t, *tabs]
    if nb > 1:
        in_specs += [nxt(hw, cb0), prev(hw, cb0 + 1), prev(hw, cb0 + 2), nxt(hw, 0), t_nxt, t_nxt]
        operands += [qkv, qkv, qkv, do, lse_t, delta_t]
    return pl.pallas_call(
        body, name=name, grid=(bl, dil, nb), in_specs=in_specs, out_specs=cur(3 * hw, 0),
        out_shape=jax.ShapeDtypeStruct((bl, dil, ln, 3 * hw), BF16),
        compiler_params=_params(("parallel", "parallel", "arbitrary")),
    )(*operands)


def _head_terms(do, o, lse, e):
    rows = do.shape[0]
    lane = lax.broadcasted_iota(jnp.int32, (rows, LANES), 1)
    mine = (lane < B_VDIM) if e == 0 else (lane >= B_VDIM)
    prod = do.astype(F32) * o.astype(F32)
    dl = jnp.sum(jnp.where(mine, prod, 0.0), axis=-1, keepdims=True)
    do_e = jnp.where(mine, do, jnp.zeros_like(do))
    return do_e, dl, lse[:, e * B_VDIM:e * B_VDIM + 1]


def _col_to_row(col, rows):
    return jnp.transpose(jnp.broadcast_to(col, (rows, LANES)))[0:1, :]


def _mla_fwd(q_cat, kvup, kr, z, tq):
    bl, t, _ = q_cat.shape
    nq = t // tq
    pairs = B_HEADS // 2
    v_blk0 = (B_HEADS * LANES) // LANES

    def body(q_ref, k_ref, v_ref, kr_ref, z_ref, y_ref, o_ref, lse_ref, lrow_ref, m_ref, acc_ref):
        qi = pl.program_id(2)
        qs = [q_ref[:, e * LANES:(e + 1) * LANES] for e in range(2)]
        row = lax.broadcasted_iota(jnp.int32, (tq, tq), 0)
        col = lax.broadcasted_iota(jnp.int32, (tq, tq), 1)
        tri = col <= row
        sum_lane = [B_VDIM, 0]

        for e in range(2):
            m_ref[e] = jnp.full((tq, LANES), NEG, F32)
            acc_ref[e] = jnp.zeros((tq, LANES), F32)

        def tile(k0, w, masked):
            lane = lax.broadcasted_iota(jnp.int32, (w, LANES), 1)
            first = lane < B_VDIM
            krv = kr_ref[pl.ds(k0, w), :]
            v = v_ref[pl.ds(k0, w), :]
            vs = [jnp.where(first, v, jnp.where(lane == B_VDIM, 1.0, 0.0).astype(BF16)),
                  jnp.where(first, jnp.where(lane == 0, 1.0, 0.0).astype(BF16), v)]
            ss = []
            for e in range(2):
                k = k_ref[pl.ds(k0, w), e * LANES:(e + 1) * LANES] + krv
                s = _dot_nt(qs[e], k)
                if masked:
                    r = lax.broadcasted_iota(jnp.int32, (tq, w), 0)
                    c = lax.broadcasted_iota(jnp.int32, (tq, w), 1)
                    s = jnp.where(c <= r + (w - tq), s, NEG)
                ss.append(s)
            for e in range(2):
                m_old = m_ref[e]
                m_new = jnp.maximum(m_old, jnp.max(ss[e], axis=-1, keepdims=True))
                p = jnp.exp2(ss[e] - jnp.concatenate([m_new] * (w // LANES), axis=1)).astype(BF16)
                m_ref[e] = m_new
                acc_ref[e] = jnp.exp2(m_old - m_new) * acc_ref[e] + _dot_nn(p, vs[e])

        def step(kb2, carry):
            tile(pl.multiple_of(kb2 * 2 * tq, 2 * tq), 2 * tq, False)
            return carry

        lax.fori_loop(0, qi // 2, step, 0)

        @pl.when(qi % 2 == 1)
        def _():
            tile(pl.multiple_of((qi - 1) * tq, tq), 2 * tq, True)

        @pl.when(qi % 2 == 0)
        def _():
            tile(pl.multiple_of(qi * tq, tq), tq, True)
        lane = lax.broadcasted_iota(jnp.int32, (tq, LANES), 1)
        first = lane < B_VDIM
        accs = [acc_ref[e] for e in range(2)]
        ls = [accs[e][:, sum_lane[e]:sum_lane[e] + 1] for e in range(2)]
        outs = [accs[e] / ls[e] for e in range(2)]
        lses = [m_ref[e] + jnp.log2(ls[e]) for e in range(2)]
        o = jnp.where(first, outs[0], outs[1])
        zv = z_ref[...].astype(F32)
        o_ref[...] = o.astype(BF16)
        y_ref[...] = (o * (zv * _sigmoid(zv))).astype(BF16)
        lse_ref[...] = jnp.where(first, lses[0], lses[1])
        for e in range(2):
            lrow_ref[e:e + 1, :] = jnp.transpose(lses[e])[0:1, :]

    blk = pl.BlockSpec((None, tq, LANES), lambda b, j, i: (b, i, j))
    return pl.pallas_call(
        body, name="mla_fwd", grid=(bl, pairs, nq),
        in_specs=[pl.BlockSpec((None, tq, 2 * LANES), lambda b, j, i: (b, i, j)),
                  pl.BlockSpec((None, t, 2 * LANES), lambda b, j, i: (b, 0, j)),
                  pl.BlockSpec((None, t, LANES), lambda b, j, i: (b, 0, v_blk0 + j)),
                  pl.BlockSpec((None, t, LANES), lambda b, j, i: (b, 0, 0)),
                  blk],
        out_specs=(blk, blk, blk, pl.BlockSpec((None, None, None, 2, tq), lambda b, j, i: (b, j, i, 0, 0))),
        out_shape=(jax.ShapeDtypeStruct((bl, t, B_WIDTH), BF16), jax.ShapeDtypeStruct((bl, t, B_WIDTH), BF16),
                   jax.ShapeDtypeStruct((bl, t, B_WIDTH), F32),
                   jax.ShapeDtypeStruct((bl, pairs, nq, 2, tq), F32)),
        scratch_shapes=[pltpu.VMEM((2, tq, LANES), F32), pltpu.VMEM((2, tq, LANES), F32)],
        compiler_params=_params(("parallel", "parallel", "arbitrary")),
    )(q_cat, kvup, kvup, kr, z)


def _mla_dq(q_cat, kvup, kr, do, o, lse, tabs, tq):
    bl, t, _ = q_cat.shape
    nq = t // tq
    pairs = B_HEADS // 2
    v_blk0 = (B_HEADS * LANES) // LANES

    def body(q_ref, k_ref, v_ref, kr_ref, do_ref, o_ref, lse_ref, tc, tsa, tsb, dq_ref, drow_ref, acc_ref):
        qi = pl.program_id(2)
        dov, ov, lsev = do_ref[...], o_ref[...], lse_ref[...]
        qs = [q_ref[:, e * LANES:(e + 1) * LANES] for e in range(2)]
        terms = [_head_terms(dov, ov, lsev, e) for e in range(2)]
        row = lax.broadcasted_iota(jnp.int32, (tq, tq), 0)
        col = lax.broadcasted_iota(jnp.int32, (tq, tq), 1)
        tri = col <= row
        for e in range(2):
            acc_ref[e] = jnp.zeros((tq, LANES), F32)

        def tile(k0, w, masked):
            krv = kr_ref[pl.ds(k0, w), :]
            v = v_ref[pl.ds(k0, w), :]
            ks = [k_ref[pl.ds(k0, w), e * LANES:(e + 1) * LANES] + krv for e in range(2)]
            ss = [_dot_nt(qs[e], ks[e]) for e in range(2)]
            dps = [_dot_nt(terms[e][0], v) for e in range(2)]
            for e in range(2):
                s = ss[e]
                if masked:
                    r = lax.broadcasted_iota(jnp.int32, (tq, w), 0)
                    c = lax.broadcasted_iota(jnp.int32, (tq, w), 1)
                    s = jnp.where(c <= r + (w - tq), s, NEG)
                p = jnp.exp2(s - terms[e][2])
                ds = (p * (dps[e] - terms[e][1])).astype(BF16)
                acc_ref[e] += _dot_nn(ds, ks[e])

        def step(kb2, carry):
            tile(pl.multiple_of(kb2 * 2 * tq, 2 * tq), 2 * tq, False)
            return carry

        lax.fori_loop(0, qi // 2, step, 0)

        @pl.when(qi % 2 == 1)
        def _():
            tile(pl.multiple_of((qi - 1) * tq, tq), 2 * tq, True)

        @pl.when(qi % 2 == 0)
        def _():
            tile(pl.multiple_of(qi * tq, tq), tq, True)

        for e in range(2):
            dq_ref[:, e * LANES:(e + 1) * LANES] = _rope_apply(acc_ref[e] * B_SCALE, tc[...], tsa[...], tsb[...], -1).astype(BF16)
            drow_ref[e:e + 1, :] = _col_to_row(terms[e][1], tq)

    blk = pl.BlockSpec((None, tq, LANES), lambda b, j, i: (b, i, j))
    tab = pl.BlockSpec((None, tq, LANES), lambda b, j, i: (b, i, 0))
    qblk = pl.BlockSpec((None, tq, 2 * LANES), lambda b, j, i: (b, i, j))
    return pl.pallas_call(
        body, name="mla_dq", grid=(bl, pairs, nq),
        in_specs=[qblk,
                  pl.BlockSpec((None, t, 2 * LANES), lambda b, j, i: (b, 0, j)),
                  pl.BlockSpec((None, t, LANES), lambda b, j, i: (b, 0, v_blk0 + j)),
                  pl.BlockSpec((None, t, LANES), lambda b, j, i: (b, 0, 0)),
                  blk, blk, blk, tab, tab, tab],
        out_specs=(qblk, pl.BlockSpec((None, None, None, 2, tq), lambda b, j, i: (b, j, i, 0, 0))),
        out_shape=(jax.ShapeDtypeStruct((bl, t, B_HEADS * LANES), BF16),
                   jax.ShapeDtypeStruct((bl, pairs, nq, 2, tq), F32)),
        scratch_shapes=[pltpu.VMEM((2, tq, LANES), F32)],
        compiler_params=_params(("parallel", "parallel", "arbitrary")),
    )(q_cat, kvup, kvup, kr, do, o, lse, *tabs)


def _mla_dkv(q_cat, kvup, kr, do, lse_rows, delta_rows, tq):
    bl, t, _ = q_cat.shape
    nq = t // tq
    pairs = B_HEADS // 2
    v_blk0 = (B_HEADS * LANES) // LANES

    def body(q_ref, k_ref, v_ref, kr_ref, do_ref, lrow_ref, drow_ref, dk_ref, dv_ref, acc_ref):
        kb = pl.program_id(2)
        v = v_ref[...]
        krv = kr_ref[...]
        ks = [k_ref[:, e * LANES:(e + 1) * LANES] + krv for e in range(2)]
        krow = lax.broadcasted_iota(jnp.int32, (tq, tq), 0)
        qcol = lax.broadcasted_iota(jnp.int32, (tq, tq), 1)
        tri = krow <= qcol
        lane = lax.broadcasted_iota(jnp.int32, (tq, LANES), 1)
        mine = [lane < B_VDIM, lane >= B_VDIM]

        for e in range(3):
            acc_ref[e] = jnp.zeros((tq, LANES), F32)

        def tile(qb, nblk, masked):
            w = nblk * tq
            rows = pl.ds(pl.multiple_of(qb * tq, tq), w)
            dov = do_ref[rows, :]
            lane_w = lax.broadcasted_iota(jnp.int32, (w, LANES), 1)
            mine_w = [lane_w < B_VDIM, lane_w >= B_VDIM]
            qs = [q_ref[rows, e * LANES:(e + 1) * LANES] for e in range(2)]
            does = [jnp.where(mine_w[e], dov, jnp.zeros_like(dov)) for e in range(2)]
            sts = [_dot_nt(ks[e], qs[e]) for e in range(2)]
            dpts = [_dot_nt(v, does[e]) for e in range(2)]

            def rows_of(ref, e):
                return jnp.concatenate([ref[qb + i, e:e + 1, :] for i in range(nblk)], axis=1)

            pts = []
            for e in range(2):
                st = sts[e]
                if masked:
                    r = lax.broadcasted_iota(jnp.int32, (tq, w), 0)
                    c = lax.broadcasted_iota(jnp.int32, (tq, w), 1)
                    st = jnp.where(r <= c, st, NEG)
                pts.append(jnp.exp2(st - rows_of(lrow_ref, e)))
            acc_ref[2] += _dot_nn(pts[0].astype(BF16), does[0]) + _dot_nn(pts[1].astype(BF16), does[1])
            for e in range(2):
                dst = (pts[e] * (dpts[e] - rows_of(drow_ref, e))).astype(BF16)
                acc_ref[e] += _dot_nn(dst, qs[e])

        rest = nq - 1 - kb
        odd = rest % 2

        @pl.when(odd == 1)
        def _():
            tile(kb, 2, True)

        @pl.when(odd == 0)
        def _():
            tile(kb, 1, True)

        def step(i, carry):
            tile(kb + 1 + odd + 2 * i, 2, False)
            return carry

        lax.fori_loop(0, rest // 2, step, 0)
        dk_ref[:, 0:LANES] = (acc_ref[0] * LN2).astype(BF16)
        dk_ref[:, LANES:2 * LANES] = (acc_ref[1] * LN2).astype(BF16)
        dv_ref[...] = acc_ref[2].astype(BF16)

    full = pl.BlockSpec((None, t, LANES), lambda b, j, i: (b, 0, j))
    rows = pl.BlockSpec((None, None, nq, 2, tq), lambda b, j, i: (b, j, 0, 0, 0))
    kblk = pl.BlockSpec((None, tq, 2 * LANES), lambda b, j, i: (b, i, j))
    return pl.pallas_call(
        body, name="mla_dkv", grid=(bl, pairs, nq),
        in_specs=[pl.BlockSpec((None, t, 2 * LANES), lambda b, j, i: (b, 0, j)),
                  kblk,
                  pl.BlockSpec((None, tq, LANES), lambda b, j, i: (b, i, v_blk0 + j)),
                  pl.BlockSpec((None, tq, LANES), lambda b, j, i: (b, i, 0)),
                  full, rows, rows],
        out_specs=(kblk, pl.BlockSpec((None, tq, LANES), lambda b, j, i: (b, i, j))),
        out_shape=(jax.ShapeDtypeStruct((bl, t, B_HEADS * LANES), BF16),
                   jax.ShapeDtypeStruct((bl, t, B_WIDTH), BF16)),
        scratch_shapes=[pltpu.VMEM((3, tq, LANES), F32)],
        compiler_params=_params(("parallel", "parallel", "arbitrary")),
    )(q_cat, kvup, kvup, kr, do, lse_rows, delta_rows)


def _adamw(w, g, m, v, *, name):
    r, c = w.shape
    tr = _row_tile(r, 256)
    c1 = 1.0 - ADAM_B1
    c2 = 1.0 - ADAM_B2
    bc1 = 1.0 - ADAM_B1 ** ADAM_STEP
    bc2 = 1.0 - ADAM_B2 ** ADAM_STEP

    def body(w_ref, g_ref, m_ref, v_ref, d_ref, nm_ref, nv_ref):
        gv = g_ref[...]
        nm = ADAM_B1 * m_ref[...] + c1 * gv
        nv = ADAM_B2 * v_ref[...] + c2 * (gv * gv)
        nm_ref[...] = nm
        nv_ref[...] = nv
        d_ref[...] = -ADAM_LR * ((nm / bc1) / (jnp.sqrt(nv / bc2) + ADAM_EPS) + ADAM_WD * w_ref[...])

    blk = pl.BlockSpec((tr, c), lambda i: (i, 0))
    sds = jax.ShapeDtypeStruct((r, c), F32)
    return pl.pallas_call(
        body, name=name, grid=(r // tr,), in_specs=[blk] * 4, out_specs=(blk,) * 3,
        out_shape=(sds,) * 3, compiler_params=_params(("parallel",)),
    )(w, g, m, v)


def _add_my_half(stacked, other, core, out_dtype, *, name):
    nch, a, c = stacked.shape
    h = a // 2
    tr = _row_tile(h, 256)
    nblk = h // tr

    def body(core_ref, s_ref, p_ref, o_ref):
        o_ref[...] = (s_ref[...] + p_ref[...]).astype(o_ref.dtype)

    return pl.pallas_call(
        body, name=name,
        grid_spec=pltpu.PrefetchScalarGridSpec(
            num_scalar_prefetch=1, grid=(nch, nblk),
            in_specs=[pl.BlockSpec((None, tr, c), lambda k, i, cr: (k, cr[0] * nblk + i, 0)),
                      pl.BlockSpec((None, tr, c), lambda k, i, cr: (k, i, 0))],
            out_specs=pl.BlockSpec((None, tr, c), lambda k, i, cr: (k, i, 0))),
        out_shape=jax.ShapeDtypeStruct((nch, h, c), out_dtype),
        compiler_params=_params(("parallel", "parallel")),
    )(core, stacked, other)


def _sum_chips(parts, own, chip, *, name):
    nch, h, c = parts.shape
    tr = _row_tile(h, 256)

    def body(chip_ref, p_ref, own_ref, o_ref):
        me = chip_ref[0]

        def slot(k):
            return jnp.where(me == k, own_ref[k].astype(F32), p_ref[k].astype(F32))

        acc = slot(0) + slot(1)
        for k in range(2, nch):
            acc = acc + slot(k)
        o_ref[...] = acc

    blk = pl.BlockSpec((nch, tr, c), lambda i, cr: (0, i, 0))
    return pl.pallas_call(
        body, name=name,
        grid_spec=pltpu.PrefetchScalarGridSpec(
            num_scalar_prefetch=1, grid=(h // tr,), in_specs=[blk, blk],
            out_specs=pl.BlockSpec((tr, c), lambda i, cr: (i, 0))),
        out_shape=jax.ShapeDtypeStruct((h, c), F32), compiler_params=_params(("parallel",)),
    )(chip, parts, own)


def _join_halves(mine, other, core, *, name):
    h, c = mine.shape
    tr = _row_tile(h, 256)
    nblk = h // tr

    def body(core_ref, m_ref, s_ref, o_ref):
        is_mine = pl.program_id(0) // nblk == core_ref[0]

        @pl.when(is_mine)
        def _():
            o_ref[...] = m_ref[...]

        @pl.when(jnp.logical_not(is_mine))
        def _():
            o_ref[...] = s_ref[...]

    blk = pl.BlockSpec((tr, c), lambda i, cr: (i % nblk, 0))
    return pl.pallas_call(
        body, name=name,
        grid_spec=pltpu.PrefetchScalarGridSpec(
            num_scalar_prefetch=1, grid=(2 * nblk,), in_specs=[blk, blk],
            out_specs=pl.BlockSpec((tr, c), lambda i, cr: (i, 0))),
        out_shape=jax.ShapeDtypeStruct((2 * h, c), F32), compiler_params=_params(("arbitrary",)),
    )(core, mine, other)


def _place():
    x, y, c = lax.axis_index("x"), lax.axis_index("y"), lax.axis_index("c")
    chips = [(1 - x, y), (x, 1 - y), (1 - x, 1 - y)]
    return x, y, c, chips


def _remote(src, dst, send_sems, recv_sems, k, to):
    return pltpu.make_async_remote_copy(src_ref=src, dst_ref=dst, send_sem=send_sems.at[k],
                                        recv_sem=recv_sems.at[k], device_id=to, device_id_type=MESH)


def _hbm_call(body, name, ins, out_shapes, n_remote):
    any_spec = pl.BlockSpec(memory_space=pl.ANY)
    return pl.pallas_call(
        body, name=name, in_specs=[any_spec] * len(ins), out_specs=tuple([any_spec] * len(out_shapes)),
        out_shape=tuple(out_shapes),
        scratch_shapes=[pltpu.SemaphoreType.DMA((n_remote,)), pltpu.SemaphoreType.DMA((n_remote,))],
    )(*ins)


def _all_gather_chips(shards, *, name):
    n = len(shards)

    def body(*refs):
        ins, outs = refs[:n], refs[n:2 * n]
        send_sems, recv_sems = refs[2 * n:]
        x, y, c, chips = _place()
        me = 2 * x + y
        sent = []
        for s in range(n):
            h = ins[s].shape[0] // 2
            for j, (px, py) in enumerate(chips):
                cp = _remote(ins[s].at[pl.ds(c * h, h)], outs[s].at[me, pl.ds(c * h, h)],
                             send_sems, recv_sems, s * 6 + j, (px, py, c))
                cp.start()
                sent.append(cp)
        for s in range(n):
            h = ins[s].shape[0] // 2
            for j, (px, py) in enumerate(chips):
                slab = outs[s].at[2 * px + py, pl.ds(c * h, h)]
                _remote(slab, slab, send_sems, recv_sems, s * 6 + j, (px, py, c)).wait_recv()
                cp = _remote(slab, slab, send_sems, recv_sems, s * 6 + 3 + j, (x, y, 1 - c))
                cp.start()
                sent.append(cp)
        for s in range(n):
            h = ins[s].shape[0] // 2
            for j, (px, py) in enumerate(chips):
                slab = outs[s].at[2 * px + py, pl.ds((1 - c) * h, h)]
                _remote(slab, slab, send_sems, recv_sems, s * 6 + 3 + j, (x, y, 1 - c)).wait_recv()
        for cp in sent:
            cp.wait_send()

    out_shapes = [jax.ShapeDtypeStruct((N_CHIPS,) + s.shape, s.dtype) for s in shards]
    return _hbm_call(body, name, shards, out_shapes, 6 * n)


def _pair_send_other_half(stacked, *, name):
    n = len(stacked)

    def body(*refs):
        ins, outs = refs[:n], refs[n:2 * n]
        send_sems, recv_sems = refs[2 * n:]
        x, y, c, _chips = _place()
        sent = []
        for s in range(n):
            h = ins[s].shape[1] // 2
            cp = _remote(ins[s].at[:, pl.ds((1 - c) * h, h)], outs[s], send_sems, recv_sems, s, (x, y, 1 - c))
            cp.start()
            sent.append(cp)
        for cp in sent:
            cp.wait_recv()
        for cp in sent:
            cp.wait_send()

    out_shapes = [jax.ShapeDtypeStruct((s.shape[0], s.shape[1] // 2, s.shape[2]), s.dtype) for s in stacked]
    return _hbm_call(body, name, stacked, out_shapes, n)


def _chip_exchange(halves, *, name):
    n = len(halves)

    def body(*refs):
        ins, outs = refs[:n], refs[n:2 * n]
        send_sems, recv_sems = refs[2 * n:]
        x, y, c, chips = _place()
        me = 2 * x + y
        sent = []
        for s in range(n):
            for j, (px, py) in enumerate(chips):
                cp = _remote(ins[s].at[2 * px + py], outs[s].at[me], send_sems, recv_sems, s * 3 + j, (px, py, c))
                cp.start()
                sent.append(cp)
        for s in range(n):
            for j, (px, py) in enumerate(chips):
                slab = outs[s].at[2 * px + py]
                _remote(slab, slab, send_sems, recv_sems, s * 3 + j, (px, py, c)).wait_recv()
        for cp in sent:
            cp.wait_send()

    out_shapes = [jax.ShapeDtypeStruct(s.shape, s.dtype) for s in halves]
    return _hbm_call(body, name, halves, out_shapes, 3 * n)


def _chip_exchange_start(halves, *, name):
    n = len(halves)
    hbm = pl.BlockSpec(memory_space=pltpu.HBM)
    sem = pl.BlockSpec(memory_space=pltpu.SEMAPHORE)

    def body(*refs):
        ins, lands = refs[:n], refs[n:2 * n]
        send_sems, recv_sems = refs[2 * n], refs[2 * n + 1]
        token = refs[-1]
        x, y, c, chips = _place()
        me = 2 * x + y
        for s in range(n):
            for j, (px, py) in enumerate(chips):
                _remote(ins[s].at[2 * px + py], lands[s].at[me], send_sems, recv_sems, s * 3 + j, (px, py, c)).start()
        token[...] = jnp.zeros_like(token)

    slabs = [pltpu.HBM(s.shape, s.dtype) for s in halves]
    outs = pl.pallas_call(
        body, name=name,
        out_shape=(pltpu.SemaphoreType.DMA((3 * n,)), pltpu.SemaphoreType.DMA((3 * n,)), *slabs, *slabs,
                   jax.ShapeDtypeStruct((8, LANES), F32)),
        in_specs=[hbm] * (2 * n), out_specs=(sem, sem, *([hbm] * (2 * n)), pl.BlockSpec(memory_space=pltpu.VMEM)),
        input_output_aliases={i: 2 + i for i in range(2 * n)},
        compiler_params=pltpu.CompilerParams(has_side_effects=pltpu.SideEffectType.DATAFLOW_SIDE_EFFECTING),
    )(*[pltpu.with_memory_space_constraint(s, pltpu.HBM) for s in halves],
      *[pltpu.with_memory_space_constraint(lax.empty(s.shape, s.dtype), pltpu.HBM) for s in halves])
    return outs[0], outs[1], list(outs[2:2 + n]), list(outs[2 + n:2 + 2 * n]), outs[-1]


def _chip_exchange_wait(send_sems, recv_sems, sent, lands, after, *, name):
    n = len(sent)
    hbm = pl.BlockSpec(memory_space=pltpu.HBM)
    sem = pl.BlockSpec(memory_space=pltpu.SEMAPHORE)

    def body(*refs):
        ins, lands_in = refs[:n], refs[n:2 * n]
        send_sems, recv_sems = refs[2 * n], refs[2 * n + 1]
        x, y, c, chips = _place()
        me = 2 * x + y
        for s in range(n):
            for j, (px, py) in enumerate(chips):
                k = 2 * px + py
                _remote(ins[s].at[k], lands_in[s].at[me], send_sems, recv_sems, s * 3 + j, (px, py, c)).wait_send()
                _remote(ins[s].at[k], lands_in[s].at[k], send_sems, recv_sems, s * 3 + j, (px, py, c)).wait_recv()

    slabs = [pltpu.HBM(s.shape, s.dtype) for s in sent]
    outs = pl.pallas_call(
        body, name=name, out_shape=(*slabs, *slabs),
        in_specs=[hbm] * (2 * n) + [sem, sem, pl.BlockSpec(memory_space=pl.ANY)],
        out_specs=tuple([hbm] * (2 * n)), input_output_aliases={i: i for i in range(2 * n)},
        compiler_params=pltpu.CompilerParams(has_side_effects=pltpu.SideEffectType.DATAFLOW_SIDE_EFFECTING),
    )(*sent, *lands, send_sems, recv_sems, after)
    return list(outs[n:])


def _gather_late_start(shards, *, name):
    n = len(shards)
    hbm = pl.BlockSpec(memory_space=pltpu.HBM)
    sem = pl.BlockSpec(memory_space=pltpu.SEMAPHORE)

    def body(*refs):
        ins, lands = refs[:n], refs[n:2 * n]
        send_sems, recv_sems = refs[2 * n], refs[2 * n + 1]
        token = refs[-1]
        x, y, c, chips = _place()
        me = 2 * x + y
        for s in range(n):
            h = ins[s].shape[0] // 2
            for j, (px, py) in enumerate(chips):
                for cc in range(2):
                    pltpu.make_async_remote_copy(
                        src_ref=ins[s].at[pl.ds(c * h, h)], dst_ref=lands[s].at[me, pl.ds(c * h, h)],
                        send_sem=send_sems.at[s * 6 + j * 2 + cc], recv_sem=recv_sems.at[s * 6 + j * 2 + c],
                        device_id=(px, py, cc), device_id_type=MESH).start()
        token[...] = jnp.zeros_like(token)

    srcs = [pltpu.HBM(s.shape, s.dtype) for s in shards]
    zones = [pltpu.HBM((N_CHIPS,) + s.shape, s.dtype) for s in shards]
    outs = pl.pallas_call(
        body, name=name,
        out_shape=(pltpu.SemaphoreType.DMA((6 * n,)), pltpu.SemaphoreType.DMA((6 * n,)), *srcs, *zones,
                   jax.ShapeDtypeStruct((8, LANES), F32)),
        in_specs=[hbm] * (2 * n), out_specs=(sem, sem, *([hbm] * (2 * n)), pl.BlockSpec(memory_space=pltpu.VMEM)),
        input_output_aliases={i: 2 + i for i in range(2 * n)},
        compiler_params=pltpu.CompilerParams(has_side_effects=pltpu.SideEffectType.DATAFLOW_SIDE_EFFECTING),
    )(*[pltpu.with_memory_space_constraint(s, pltpu.HBM) for s in shards],
      *[pltpu.with_memory_space_constraint(lax.empty((N_CHIPS,) + s.shape, s.dtype), pltpu.HBM) for s in shards])
    return outs[0], outs[1], list(outs[2:2 + n]), list(outs[2 + n:2 + 2 * n]), outs[-1]


def _gather_late_wait(send_sems, recv_sems, sent, lands, after, *, name):
    n = len(sent)
    hbm = pl.BlockSpec(memory_space=pltpu.HBM)
    sem = pl.BlockSpec(memory_space=pltpu.SEMAPHORE)

    def body(*refs):
        ins, lands_in = refs[:n], refs[n:2 * n]
        send_sems, recv_sems = refs[2 * n], refs[2 * n + 1]
        x, y, c, chips = _place()
        me = 2 * x + y
        for s in range(n):
            h = ins[s].shape[0] // 2
            for j, (px, py) in enumerate(chips):
                k = 2 * px + py
                for cc in range(2):
                    pltpu.make_async_remote_copy(
                        src_ref=ins[s].at[pl.ds(c * h, h)], dst_ref=lands_in[s].at[me, pl.ds(c * h, h)],
                        send_sem=send_sems.at[s * 6 + j * 2 + cc], recv_sem=recv_sems.at[s * 6 + j * 2 + c],
                        device_id=(px, py, cc), device_id_type=MESH).wait_send()
                    slab = lands_in[s].at[k, pl.ds(cc * h, h)]
                    pltpu.make_async_remote_copy(
                        src_ref=slab, dst_ref=slab, send_sem=send_sems.at[s * 6 + j * 2 + cc],
                        recv_sem=recv_sems.at[s * 6 + j * 2 + cc], device_id=(px, py, cc),
                        device_id_type=MESH).wait_recv()

    srcs = [pltpu.HBM(s.shape, s.dtype) for s in sent]
    zones = [pltpu.HBM(l.shape, l.dtype) for l in lands]
    outs = pl.pallas_call(
        body, name=name, out_shape=(*srcs, *zones),
        in_specs=[hbm] * (2 * n) + [sem, sem, pl.BlockSpec(memory_space=pl.ANY)],
        out_specs=tuple([hbm] * (2 * n)), input_output_aliases={i: i for i in range(2 * n)},
        compiler_params=pltpu.CompilerParams(has_side_effects=pltpu.SideEffectType.DATAFLOW_SIDE_EFFECTING),
    )(*sent, *lands, send_sems, recv_sems, after)
    return list(outs[:n]), list(outs[n:])


def _pair_swap(halves, *, name):
    n = len(halves)

    def body(*refs):
        ins, outs = refs[:n], refs[n:2 * n]
        send_sems, recv_sems = refs[2 * n:]
        x, y, c, _chips = _place()
        sent = []
        for s in range(n):
            cp = _remote(ins[s], outs[s], send_sems, recv_sems, s, (x, y, 1 - c))
            cp.start()
            sent.append(cp)
        for cp in sent:
            cp.wait_recv()
        for cp in sent:
            cp.wait_send()

    out_shapes = [jax.ShapeDtypeStruct(s.shape, s.dtype) for s in halves]
    return _hbm_call(body, name, halves, out_shapes, n)


def _pack_rows(parts, row_multiple):
    flat = jnp.concatenate([p.reshape(-1) for p in parts])
    quantum = row_multiple * PACK_COLS
    pad = (-flat.shape[0]) % quantum
    flat = jnp.pad(flat, (0, pad))
    return flat.reshape(-1, PACK_COLS)


def _unpack(flat, shapes):
    out, pos = [], 0
    for shp in shapes:
        size = math.prod(shp)
        out.append(flat[pos:pos + size].reshape(shp))
        pos += size
    return out


def _to_chunks_cols(full):
    r, c4 = full.shape
    return full.reshape(r, N_CHIPS, c4 // N_CHIPS).transpose(1, 0, 2)


def _from_chunks_cols(stacked):
    nch, r, c = stacked.shape
    return stacked.transpose(1, 0, 2).reshape(r, nch * c)


def _class_major(a, bl, t, dil):
    w = a.shape[-1]
    if dil == 1:
        return a.reshape(bl, 1, t, w)
    return a.reshape(bl, t // dil, dil, w).transpose(0, 2, 1, 3)


def _natural(a):
    bl, dil, ln, w = a.shape
    if dil == 1:
        return a.reshape(bl * ln, w)
    return a.transpose(0, 2, 1, 3).reshape(bl * ln * dil, w)


def _train_step(x, positions, a_pre_norm, a_w_in, a_w_out, a_post_norm, kv_norm, kv_w_down, kv_latent_norm,
                kv_w_up, b_pre_norm, b_w_in, b_q_norm, b_w_q_up, b_w_out, b_post_norm, loss_target, moments):
    bl, t, d = x.shape
    n = bl * t
    qb = t // A_DILATIONS[-1]
    tq = _tile(t, 256)
    dq4 = d // N_CHIPS
    chip = 2 * lax.axis_index("x") + lax.axis_index("y")
    chip_arr = chip.astype(jnp.int32).reshape(1)
    core_arr = lax.axis_index("c").astype(jnp.int32).reshape(1)

    w_in_a_s = a_w_in[0].astype(BF16)
    outs_s = jnp.concatenate([a_w_out[0], b_w_out[0]], axis=0).astype(BF16)
    small_shapes = [kv_w_down.shape, kv_w_up.shape, b_w_in[0].shape, b_w_q_up[0].shape]
    small_s = _pack_rows([kv_w_down, kv_w_up, b_w_in[0], b_w_q_up[0]], 32).astype(BF16)
    gains_s = jnp.pad(jnp.concatenate([a_pre_norm[0], a_post_norm[0]]), (0, 16 * LANES - 2 * dq4)).reshape(16, LANES)
    early = [w_in_a_s, gains_s]
    g_in_a, g_gains = [lax.dynamic_update_index_in_dim(g, s, chip, 0)
                       for g, s in zip(_all_gather_chips(early, name="gather_weights"), early)]
    late_sems = _gather_late_start([outs_s, small_s], name="gather_late_start")

    w_in_a = _from_chunks_cols(g_in_a)
    gflat = g_gains.reshape(N_CHIPS, -1)
    g_a_pre = gflat[:, :dq4].reshape(1, d)
    g_a_post = gflat[:, dq4:2 * dq4].reshape(1, d)

    def late_weights(after):
        sent, lands = _gather_late_wait(late_sems[0], late_sems[1], late_sems[2], late_sems[3], after,
                                        name="gather_late_wait")
        g_outs, g_small = [lax.dynamic_update_index_in_dim(g, s, chip, 0) for g, s in zip(lands, sent)]
        w_out_a = g_outs[:, :A_WIDTH // N_CHIPS].reshape(A_WIDTH, d)
        w_out_b = g_outs[:, A_WIDTH // N_CHIPS:].reshape(B_WIDTH, d)
        sm = [_unpack(g_small[k].reshape(-1), small_shapes) for k in range(N_CHIPS)]
        w_down = jnp.concatenate([sm[k][0] for k in range(N_CHIPS)], axis=0)
        w_up = jnp.concatenate([sm[k][1] for k in range(N_CHIPS)], axis=1)
        w_in_b = jnp.concatenate([sm[k][2] for k in range(N_CHIPS)], axis=1)
        w_q_up = jnp.concatenate([sm[k][3] for k in range(N_CHIPS)], axis=1)
        w_up_h = w_up.reshape(B_KV_LORA, B_HEADS, B_NOPE + B_VDIM)
        w_up_k = jnp.pad(w_up_h[:, :, :B_NOPE], ((0, 0), (0, 0), (0, LANES - B_NOPE))).reshape(B_KV_LORA, B_HEADS * LANES)
        w_up_v = w_up_h[:, :, B_NOPE:].reshape(B_KV_LORA, B_WIDTH)
        w_up_cat = jnp.concatenate([w_up_k, w_up_v], axis=1)
        w_q_up_p = jnp.pad(w_q_up.reshape(B_Q_LORA, B_HEADS, B_QK_DIM),
                           ((0, 0), (0, 0), (0, LANES - B_QK_DIM))).reshape(B_Q_LORA, B_HEADS * LANES)
        zeros_d = lambda c: jnp.zeros((d, c), BF16)
        w_down_p = jnp.concatenate([w_down[:, :B_KV_LORA], zeros_d(B_NOPE), w_down[:, B_KV_LORA:],
                                    zeros_d(LANES - B_NOPE - B_ROPE)], axis=1)
        return (w_out_a, w_out_b, w_up_k, w_up_v, w_up_cat, w_q_up_p, w_down_p,
                w_in_b[:, :B_Q_LORA], w_in_b[:, B_Q_LORA:])

    tabs_a = _rope_tables(positions, A_ROPE_THETA, 0)
    tabs_b = _rope_tables(positions, B_ROPE_THETA, B_NOPE)

    h0 = x.reshape(n, d)
    hn_a = _rms_fwd(h0, g_a_pre, BF16, name="a_pre_norm", tr=1024)
    is_qk = lambda j: j != 2
    is_q = lambda j: j == 0
    z_blk_a = 3 * A_GROUPS
    z_a = _matmul(hn_a, w_in_a, "nn", BF16, name="a_proj_z", b_cols=(z_blk_a, 1))
    o_groups, lse_groups, qkv_cm, hn_cm, tabs_cm = [], [], [], [], []
    for g, dil in enumerate(A_DILATIONS):
        flat = lambda a: _class_major(a, bl, t, dil).reshape(n, a.shape[-1])
        hn_g = hn_a if dil == 1 else flat(hn_a)
        tabs_g = tabs_a if dil == 1 else lax.optimization_barrier(tuple(flat(tb) for tb in tabs_a))
        proj_g = _matmul(hn_g, w_in_a, "nn", BF16, name=f"a_proj_{g}", rope=(tabs_g, is_qk),
                         out_scale=(A_SCALE * LOG2E, is_q), b_cols=(3 * g, 3))
        src = proj_g.reshape(bl, dil, t // dil, 3 * A_WIDTH)
        hn_cm.append(hn_g)
        tabs_cm.append(tabs_g)
        qkv_cm.append(src)
        o_g, lse_g = _attn_a_fwd(src, 0, qb, BF16, name=f"attn_a_fwd_{g}")
        o_groups.append(_natural(o_g))
        lse_groups.append(_natural(lse_g))
    ypre_a, om_a, lse_a = _merge_gate_fwd(o_groups, lse_groups, z_a, 0)
    w_out_a, w_out_b, w_up_k, w_up_v, w_up_cat, w_q_up_p, w_down_p, w_cq, w_z = late_weights(ypre_a)
    y_a = _matmul(ypre_a, w_out_a, "nn", F32, name="a_out")
    g_kvn = kv_norm.reshape(1, d)
    g_lat = kv_latent_norm.reshape(1, B_KV_LORA)
    h1, hn_kv, hn_b = _post_norm_block(y_a, g_a_post, h0, [g_kvn, b_pre_norm], name="a_post_norm")

    ckr = _matmul(hn_kv, w_down_p, "nn", F32, name="kv_down")
    c_kv, k_rope = _kv_latent_fwd(ckr, g_lat, tabs_b)
    kvup = _matmul(c_kv, w_up_cat, "nn", BF16, name="kv_up")
    z_b = _matmul(hn_b, w_z, "nn", BF16, name="b_proj_z")
    cq_raw = _matmul(hn_b, w_cq, "nn", F32, name="b_proj_q")
    c_q = _rms_fwd(cq_raw, b_q_norm, BF16, name="b_q_norm", tr=1024)
    always = lambda j: True
    q_cat = _matmul(c_q, w_q_up_p, "nn", BF16, name="b_q_up", rope=(tabs_b, always),
                    out_scale=(B_SCALE * LOG2E, always))
    r3 = lambda a: a.reshape(bl, t, a.shape[-1])
    tabs_b3 = tuple(r3(tb) for tb in tabs_b)
    ypre_b, o_b, lse_b, lse_rows_b = _mla_fwd(r3(q_cat), r3(kvup), r3(k_rope), r3(z_b), tq)
    y_b = _matmul(ypre_b.reshape(n, B_WIDTH), w_out_b, "nn", F32, name="b_out")
    dh2, loss_part = _post_norm_loss(y_b, b_post_norm, h1, loss_target.reshape(n, d))

    dy_b, dg_b_post = _rms_bwd(y_b, b_post_norm, dh2, BF16, name="b_post_norm_bwd", tr=1024)
    dypre_b = _matmul(dy_b, w_out_b, "nt", BF16, name="b_out_dx")
    dw_out_b = _matmul(ypre_b.reshape(n, B_WIDTH), dy_b, "tn", F32, name="b_out_dw", tm=1024, tk=2048)
    do_b, dz_b = _gate_bwd(dypre_b, o_b.reshape(n, B_WIDTH), z_b, 0, name="b_gate_bwd", with_delta=False)
    dq_cat, delta_rows_b = _mla_dq(r3(q_cat), r3(kvup), r3(k_rope), r3(do_b), o_b, lse_b, tabs_b3, tq)
    dq_cat = dq_cat.reshape(n, -1)
    dk_cat, dv_b = _mla_dkv(r3(q_cat), r3(kvup), r3(k_rope), r3(do_b), lse_rows_b, delta_rows_b, tq)
    dk_cat, dv_b = dk_cat.reshape(n, -1), dv_b.reshape(n, -1)
    dcq_n = _matmul(dq_cat, w_q_up_p, "nt", F32, name="b_q_up_dx")
    dw_q_up_p = _matmul(c_q, dq_cat, "tn", F32, name="b_q_up_dw", tm=1024, tk=2048)
    dcq, dg_b_q = _rms_bwd(cq_raw, b_q_norm, dcq_n, BF16, name="b_q_norm_bwd", tr=1024)
    dhn_b = _matmul(dz_b, w_z, "nt", F32, name="b_proj_z_dx")
    dhn_b = _matmul(dcq, w_cq, "nt", F32, name="b_proj_q_dx", add=dhn_b)
    dw_z = _matmul(hn_b, dz_b, "tn", F32, name="b_proj_z_dw", tm=1024, tk=2048)
    dw_cq = _matmul(hn_b, dcq, "tn", F32, name="b_proj_q_dw", tm=1024, tk=2048)
    dckv_n = _matmul(dk_cat, w_up_k, "nt", F32, name="kv_up_k_dx")
    dckv_n = _matmul(dv_b, w_up_v, "nt", F32, name="kv_up_v_dx", add=dckv_n)
    dw_up_k = _matmul(c_kv, dk_cat, "tn", F32, name="kv_up_k_dw", tm=1024, tk=2048)
    dw_up_v = _matmul(c_kv, dv_b, "tn", F32, name="kv_up_v_dw", tm=1024, tk=2048)
    dckr, dg_lat = _kv_latent_bwd(dckv_n, ckr, g_lat, dk_cat, tabs_b)
    dhn_kv = _matmul(dckr, w_down_p, "nt", F32, name="kv_down_dx")
    dw_down_p = _matmul(hn_kv, dckr, "tn", F32, name="kv_down_dw", tm=1024, tk=2048)
    dh1, dg_b_pre, dg_kvn = _rms_bwd_pair(h1, b_pre_norm, dhn_b, g_kvn, dhn_kv, dh2, name="h1_norms_bwd")

    dy_a, dg_a_post = _rms_bwd(y_a, g_a_post, dh1, BF16, name="a_post_norm_bwd", tr=1024)
    dypre_a = _matmul(dy_a, w_out_a, "nt", BF16, name="a_out_dx")
    dw_out_a = _matmul(ypre_a, dy_a, "tn", F32, name="a_out_dw", tm=1024, tk=2048)
    do_a, dz_a, delta_a = _gate_bwd(dypre_a, om_a, z_a, 0, name="a_gate_bwd", with_delta=True)
    dw_cols = A_IN_WIDTH // N_CHIPS
    dw_tn = _tile(dw_cols, 512)
    dw_kwargs = dict(tm=1024, tn=dw_tn, tk=4096, out_chunk_blocks=dw_cols // dw_tn)
    r_big = _matmul(hn_a, dz_a, "tn", F32, name="a_proj_dw_z", out_full=(N_CHIPS, d, dw_cols),
                    out_joff=z_blk_a * A_WIDTH // dw_tn, **dw_kwargs)
    dqkvs = []
    for g, dil in enumerate(A_DILATIONS):
        cm = lambda a: _class_major(a, bl, t, dil)
        swap = lambda a: jnp.swapaxes(a, 2, 3)
        lse_cm, delta_cm = cm(lse_a), cm(delta_a)
        tabs_g = tuple(tb.reshape(bl, dil, t // dil, LANES) for tb in tabs_cm[g])
        dqkv = _attn_a_bwd(qkv_cm[g], 0, cm(do_a), lse_cm, delta_cm, swap(lse_cm), swap(delta_cm),
                           tabs_g, qb, name=f"attn_a_bwd_{g}").reshape(n, 3 * A_WIDTH)
        dqkvs.append(dqkv)
        r_big = _matmul(hn_cm[g], dqkv, "tn", F32, name=f"a_proj_dw_{g}", out_into=r_big,
                        out_joff=3 * g * A_WIDTH // dw_tn, **dw_kwargs)
    r_outs = jnp.concatenate([dw_out_a.reshape(N_CHIPS, A_WIDTH // N_CHIPS, d),
                              dw_out_b.reshape(N_CHIPS, B_WIDTH // N_CHIPS, d)], axis=1)

    bulk = [r_big, r_outs]
    recv_b = _pair_send_other_half(bulk, name="reduce_pair_send")
    halves_b = [_add_my_half(s, p, core_arr, BF16, name=f"reduce_pair_add_{i}")
                for i, (s, p) in enumerate(zip(bulk, recv_b))]
    send_sems, recv_sems, sent_b, lands_b, token = _chip_exchange_start(halves_b, name="reduce_exchange_start")

    dhn_a = _matmul(dz_a, w_in_a, "nt", F32, name="a_proj_dx_z", b_koff=z_blk_a, after=token)
    dhn_more = []
    for g, dil in enumerate(A_DILATIONS):
        tk_dx = 3 * A_WIDTH
        if dil == 1:
            dhn_a = _matmul(dqkvs[g], w_in_a, "nt", F32, name=f"a_proj_dx_{g}", add=dhn_a, tk=tk_dx, b_koff=g,
                            after=token)
        else:
            part = _matmul(dqkvs[g], w_in_a, "nt", BF16, name=f"a_proj_dx_{g}", tk=tk_dx, b_koff=g, after=token)
            dhn_more.append(_natural(part.reshape(bl, dil, t // dil, d)))
    grad_x, dg_a_pre = _rms_bwd(h0, g_a_pre, dhn_a, F32, name="a_pre_norm_bwd", adds=(dh1,),
                                dy_more=tuple(dhn_more))

    dw_up = jnp.concatenate([dw_up_k.reshape(B_KV_LORA, B_HEADS, LANES)[:, :, :B_NOPE],
                             dw_up_v.reshape(B_KV_LORA, B_HEADS, B_VDIM)], axis=2).reshape(B_KV_LORA, -1)
    dw_q_up = dw_q_up_p.reshape(B_Q_LORA, B_HEADS, LANES)[:, :, :B_QK_DIM].reshape(B_Q_LORA, -1)
    dw_down = jnp.concatenate([dw_down_p[:, :B_KV_LORA], dw_down_p[:, B_KV_LORA + B_NOPE:B_KV_LORA + B_NOPE + B_ROPE]], axis=1)
    dw_in_b = jnp.concatenate([dw_cq, dw_z], axis=1)
    vec_rep = [dg_kvn.reshape(-1), dg_lat.reshape(-1), dg_b_pre.reshape(-1), dg_b_q.reshape(-1),
               dg_b_post.reshape(-1), loss_part.reshape(-1)]
    vec_shapes = [(dq4,), (dq4,)] + [v.shape for v in vec_rep]
    down_c = dw_down.reshape(N_CHIPS, dq4, -1)
    up_c = _to_chunks_cols(dw_up)
    inb_c = _to_chunks_cols(dw_in_b)
    qup_c = _to_chunks_cols(dw_q_up)
    small_chunks = []
    for k in range(N_CHIPS):
        vecs = [dg_a_pre.reshape(-1)[k * dq4:(k + 1) * dq4], dg_a_post.reshape(-1)[k * dq4:(k + 1) * dq4]] + vec_rep
        small_chunks.append(_pack_rows([down_c[k], up_c[k], inb_c[k], qup_c[k]] + vecs, 32))
    r_small = jnp.stack(small_chunks)

    recv_s = _pair_send_other_half([r_small], name="reduce_pair_send_small")
    halves_s = [_add_my_half(r_small, recv_s[0], core_arr, F32, name="reduce_pair_add_small")]
    parts_s = list(_chip_exchange(halves_s, name="reduce_exchange_small"))
    parts_b = _chip_exchange_wait(send_sems, recv_sems, sent_b, lands_b, grad_x, name="reduce_exchange_wait")
    sums = [_sum_chips(p, own, chip_arr, name=f"reduce_chip_sum_{i}")
            for i, (p, own) in enumerate(zip(parts_b + parts_s, sent_b + halves_s))]
    others = _pair_swap(sums, name="reduce_pair_swap")
    g_big, g_outs_r, g_small_r = [_join_halves(m, o, core_arr, name=f"reduce_join_{i}")
                                  for i, (m, o) in enumerate(zip(sums, others))]

    grads = {}
    grads["a_w_in"] = g_big
    grads["a_w_out"] = g_outs_r[:A_WIDTH // N_CHIPS]
    grads["b_w_out"] = g_outs_r[A_WIDTH // N_CHIPS:]
    small_out_shapes = [down_c.shape[1:], up_c.shape[1:], inb_c.shape[1:], qup_c.shape[1:]] + vec_shapes
    (grads["kv_w_down"], grads["kv_w_up"], grads["b_w_in"], grads["b_w_q_up"], grads["a_pre_norm"],
     grads["a_post_norm"], grads["kv_norm"], grads["kv_latent_norm"], grads["b_pre_norm"], grads["b_q_norm"],
     grads["b_post_norm"], loss_sum) = _unpack(g_small_r.reshape(-1), small_out_shapes)

    weights = dict(a_pre_norm=a_pre_norm, a_w_in=a_w_in, a_w_out=a_w_out, a_post_norm=a_post_norm, kv_norm=kv_norm,
                   kv_w_down=kv_w_down, kv_latent_norm=kv_latent_norm, kv_w_up=kv_w_up, b_pre_norm=b_pre_norm,
                   b_w_in=b_w_in, b_q_norm=b_q_norm, b_w_q_up=b_w_q_up, b_w_out=b_w_out, b_post_norm=b_post_norm)
    names = list(weights)
    out_g, out_d, out_m, out_v = [], [], [], []
    for i, nm in enumerate(names):
        w = weights[nm]
        two_d = (1, w.shape[0]) if w.ndim == 1 else (w.shape[-2], w.shape[-1])
        gw = grads[nm].reshape(two_d)
        dlt, new_m, new_v = _adamw(w.reshape(two_d), gw, moments[i].reshape(two_d),
                                   moments[len(names) + i].reshape(two_d), name=f"adamw_{nm}")
        out_g.append(gw.reshape(w.shape))
        out_d.append(dlt.reshape(w.shape))
        out_m.append(new_m.reshape(w.shape))
        out_v.append(new_v.reshape(w.shape))
    return (loss_sum.reshape(()), grad_x.reshape(bl, t, d), *out_g, *out_d, *out_m, *out_v)


def kernel(x, positions, a_pre_norm, a_w_in, a_w_out, a_post_norm, kv_norm, kv_w_down, kv_latent_norm, kv_w_up, b_pre_norm, b_w_in, b_q_norm, b_w_q_up, b_w_out, b_post_norm, loss_target, m_a_pre_norm, m_a_w_in, m_a_w_out, m_a_post_norm, m_kv_norm, m_kv_w_down, m_kv_latent_norm, m_kv_w_up, m_b_pre_norm, m_b_w_in, m_b_q_norm, m_b_w_q_up, m_b_w_out, m_b_post_norm, v_a_pre_norm, v_a_w_in, v_a_w_out, v_a_post_norm, v_kv_norm, v_kv_w_down, v_kv_latent_norm, v_kv_w_up, v_b_pre_norm, v_b_w_in, v_b_q_norm, v_b_w_q_up, v_b_w_out, v_b_post_norm):
    moments = (m_a_pre_norm, m_a_w_in, m_a_w_out, m_a_post_norm, m_kv_norm, m_kv_w_down, m_kv_latent_norm, m_kv_w_up,
               m_b_pre_norm, m_b_w_in, m_b_q_norm, m_b_w_q_up, m_b_w_out, m_b_post_norm,
               v_a_pre_norm, v_a_w_in, v_a_w_out, v_a_post_norm, v_kv_norm, v_kv_w_down, v_kv_latent_norm, v_kv_w_up,
               v_b_pre_norm, v_b_w_in, v_b_q_norm, v_b_w_q_up, v_b_w_out, v_b_post_norm)
    return _train_step(x, positions, a_pre_norm, a_w_in, a_w_out, a_post_norm, kv_norm, kv_w_down, kv_latent_norm,
                       kv_w_up, b_pre_norm, b_w_in, b_q_norm, b_w_q_up, b_w_out, b_post_norm, loss_target, moments)
```
